```python
import math
import jax, jax.numpy as jnp
from jax import lax
import numpy as np

D_MODEL = 1024
BATCH = 4
SEQ = 4096
DEPTH = 1
DEC_BATCH = 128
DEC_SEQ = 4
PAST_LEN = 8192
PAGE_SIZE = 128

HEAD_DIM = 64
CONV_CH = D_MODEL // 4
WINDOWS = (128, 512, 2048)
DILATIONS = (1, 4, 16)
N_DIL_GROUPS = len(WINDOWS)
HEADS_PER_GROUP = (D_MODEL - CONV_CH) // (N_DIL_GROUPS * HEAD_DIM)
N_ATTN_HEADS = N_DIL_GROUPS * HEADS_PER_GROUP
ATTN_WIDTH = N_ATTN_HEADS * HEAD_DIM
N_STEPS = WINDOWS[0] // DILATIONS[0]
N_KEYS = N_STEPS + 1
BLK = 128
CONV_K = 3
N_BUCKETS = 32
MAX_DISTANCE = WINDOWS[-1]
N_EXPERTS = 64
TOP_K = 8
N_ROUTE_GROUPS = 8
TOPK_GROUPS = 4
D_EXPERT = D_MODEL // 4
ROUTED_SCALE = 2.5
MOE_CHUNK = 512
RMS_EPS = 1e-6
N_PROJ = 3 * ATTN_WIDTH + 3 * CONV_CH

kernel_name = 'hymba_dilated_swa_shortconv_moe_step'


def _rmsnorm(x, g):
    xf = x.astype(jnp.float32)
    y = xf * lax.rsqrt(jnp.mean(xf * xf, axis=-1, keepdims=True) + RMS_EPS)
    return (y * g.astype(jnp.float32)).astype(x.dtype)


def _adaln(c, w_ada, b_ada):
    m = jax.nn.silu(c) @ w_ada + b_ada
    return jnp.split(m[:, None, :], 6, axis=-1)


def _t5_bucket(dist):
    max_exact = N_BUCKETS // 2
    df = jnp.maximum(dist, 1).astype(jnp.float32)
    large = max_exact + (jnp.log(df / max_exact) / math.log(MAX_DISTANCE / max_exact)
                         * (N_BUCKETS - max_exact)).astype(jnp.int32)
    large = jnp.minimum(large, N_BUCKETS - 1)
    return jnp.where(dist < max_exact, dist, large)


def _step_bias(rel_bias_g, dil):
    dist = jnp.arange(N_KEYS, dtype=jnp.int32) * dil
    return rel_bias_g[_t5_bucket(dist)].T.astype(jnp.float32)


def _dilated_attn_prompt(q, k, v, bias_k, dil):
    B_, S_, H, dh = q.shape
    L = S_ // dil
    Bd = B_ * dil

    def to_sub(a):
        return a.reshape(B_, L, dil, H, dh).transpose(0, 2, 1, 3, 4).reshape(Bd, L, H, dh)

    qs, ks, vs = to_sub(q), to_sub(k), to_sub(v)
    nb = -(-L // BLK)
    Lp = nb * BLK
    pad_end = Lp - L
    qb = jnp.pad(qs, ((0, 0), (0, pad_end), (0, 0), (0, 0))).reshape(Bd, nb, BLK, H, dh)

    def band(a):
        ap = jnp.pad(a, ((0, 0), (BLK, pad_end), (0, 0), (0, 0)))
        prev = ap[:, :Lp].reshape(Bd, nb, BLK, H, dh)
        cur = ap[:, BLK:].reshape(Bd, nb, BLK, H, dh)
        return jnp.concatenate([prev, cur], axis=2)

    kb, vb = band(ks), band(vs)
    s = jnp.einsum('znqhd,znkhd->znhqk', qb, kb,
                   preferred_element_type=jnp.float32) * (HEAD_DIM ** -0.5)
    qi = jnp.arange(BLK)[:, None]
    kj = jnp.arange(2 * BLK)[None, :]
    step = qi + BLK - kj
    key_sub = jnp.arange(nb)[:, None, None] * BLK - BLK + kj[None]
    valid = (step >= 0)[None] & (step <= N_STEPS)[None] & (key_sub >= 0)
    bias = bias_k[:, jnp.clip(step, 0, N_STEPS)]
    s = jnp.where(valid[None, :, None], s + bias[None, None], -jnp.inf)
    m = jnp.max(s, axis=-1, keepdims=True)
    p = jnp.exp(s - m)
    z = jnp.sum(p, axis=-1, keepdims=True)
    o = jnp.einsum('znhqk,znkhd->znqhd', (p / z).astype(vb.dtype), vb)
    lse = (m + jnp.log(z))[..., 0]
    o = o.reshape(Bd, Lp, H, dh)[:, :L]
    o = o.reshape(B_, dil, L, H, dh).transpose(0, 2, 1, 3, 4).reshape(B_, S_, H, dh)
    lse = lse.transpose(0, 1, 3, 2).reshape(Bd, Lp, H)[:, :L]
    lse = lse.reshape(B_, dil, L, H).transpose(0, 2, 1, 3).reshape(B_, S_, H)
    return o, lse


def _dilated_attn_sample(q, k_new, v_new, k_buf, v_buf, bias_k, dil):
    Wb = k_buf.shape[1]
    T = q.shape[1]
    k_ext = jnp.concatenate([k_buf.astype(k_new.dtype), k_new], axis=1)
    v_ext = jnp.concatenate([v_buf.astype(v_new.dtype), v_new], axis=1)
    idx = Wb + jnp.arange(T)[:, None] - jnp.arange(N_KEYS)[None, :] * dil
    valid = idx >= 0
    idx_c = jnp.maximum(idx, 0)
    kg = k_ext[:, idx_c]
    vg = v_ext[:, idx_c]
    s = jnp.einsum('bthd,btkhd->bhtk', q, kg,
                   preferred_element_type=jnp.float32) * (HEAD_DIM ** -0.5)
    s = jnp.where(valid[None, None], s + bias_k[:, None, :][None], -jnp.inf)
    m = jnp.max(s, axis=-1, keepdims=True)
    p = jnp.exp(s - m)
    z = jnp.sum(p, axis=-1, keepdims=True)
    o = jnp.einsum('bhtk,btkhd->bthd', (p / z).astype(vg.dtype), vg)
    lse = (m + jnp.log(z))[..., 0].transpose(0, 2, 1)
    return o, lse, k_ext[:, -Wb:], v_ext[:, -Wb:]


def _moe(h, w_router, b_router, w_gate_e, w_up_e, w_down_e, w_gate_s, w_up_s, w_down_s):
    B_, T_, D = h.shape
    xt = h.reshape(-1, D)
    n = xt.shape[0]
    scores = jax.nn.sigmoid((xt @ w_router).astype(jnp.float32))
    sel = scores + b_router.astype(jnp.float32)
    grp = sel.reshape(n, N_ROUTE_GROUPS, N_EXPERTS // N_ROUTE_GROUPS)
    grp_score = jnp.sum(lax.top_k(grp, 2)[0], axis=-1)
    _, gidx = lax.top_k(grp_score, TOPK_GROUPS)
    gmask = jnp.any(gidx[..., None] == jnp.arange(N_ROUTE_GROUPS), axis=1)
    emask = jnp.repeat(gmask, N_EXPERTS // N_ROUTE_GROUPS, axis=1)
    _, eidx = lax.top_k(jnp.where(emask, sel, -jnp.inf), TOP_K)
    w = jnp.take_along_axis(scores, eidx, axis=-1)
    w = w / jnp.sum(w, axis=-1, keepdims=True) * ROUTED_SCALE
    gates = jnp.sum(jax.nn.one_hot(eidx, N_EXPERTS, dtype=jnp.float32) * w[..., None], axis=1)
    gates = gates.astype(xt.dtype)
    chunk = min(MOE_CHUNK, n)
    pad = (-n) % chunk
    xp = jnp.pad(xt, ((0, pad), (0, 0))).reshape(-1, chunk, D)
    gp = jnp.pad(gates, ((0, pad), (0, 0))).reshape(-1, chunk, N_EXPERTS)

    def chunk_fn(args):
        xc, gc = args
        a = jnp.einsum('nd,edf->nef', xc, w_gate_e)
        b = jnp.einsum('nd,edf->nef', xc, w_up_e)
        hc = jax.nn.silu(a) * b * gc[..., None]
        return jnp.einsum('nef,efd->nd', hc, w_down_e)

    routed = lax.map(chunk_fn, (xp, gp)).reshape(-1, D)[:n]
    shared = (jax.nn.silu(xt @ w_gate_s) * (xt @ w_up_s)) @ w_down_s
    return (routed + shared).reshape(B_, T_, D)


def _trunk_layer(x, c, conv_prev, attend, rel_bias, norm1_g, norm2_g, w_ada, b_ada, w_in,
                 q_norm_g, k_norm_g, conv_w, w_o, w_router, b_router, w_gate_e, w_up_e,
                 w_down_e, w_gate_s, w_up_s, w_down_s):
    B_, T_, _ = x.shape
    sh1, sc1, g1, sh2, sc2, g2 = _adaln(c, w_ada, b_ada)
    h = _rmsnorm(x, norm1_g) * (1 + sc1) + sh1
    proj = h @ w_in
    A, C = ATTN_WIDTH, CONV_CH
    q, k, v, u_in, gate_b, gate_c = jnp.split(
        proj, [A, 2 * A, 3 * A, 3 * A + C, 3 * A + 2 * C], axis=-1)
    q = _rmsnorm(q.reshape(B_, T_, N_ATTN_HEADS, HEAD_DIM), q_norm_g)
    k = _rmsnorm(k.reshape(B_, T_, N_ATTN_HEADS, HEAD_DIM), k_norm_g)
    v = v.reshape(B_, T_, N_ATTN_HEADS, HEAD_DIM)
    outs, lses, states = [], [], []
    for g in range(N_DIL_GROUPS):
        hs = slice(g * HEADS_PER_GROUP, (g + 1) * HEADS_PER_GROUP)
        bias_k = _step_bias(rel_bias[:, hs], DILATIONS[g])
        o, lse, k_keep, v_keep = attend(g, q[:, :, hs], k[:, :, hs], v[:, :, hs], bias_k)
        outs.append(o)
        lses.append(lse)
        states += [k_keep, v_keep]
    wg = jax.nn.softmax(jnp.stack(lses, axis=0), axis=0)
    attn = jnp.concatenate(
        [(outs[g] * wg[g][..., None].astype(outs[g].dtype)).reshape(B_, T_, -1)
         for g in range(N_DIL_GROUPS)], axis=-1)
    u = gate_c * u_in
    up = jnp.concatenate([conv_prev.astype(u.dtype), u], axis=1)
    conv = sum(conv_w[j] * up[:, j:j + T_] for j in range(CONV_K))
    conv_out = gate_b * conv
    x = x + g1 * (jnp.concatenate([attn, conv_out], axis=-1) @ w_o)
    h2 = _rmsnorm(x, norm2_g) * (1 + sc2) + sh2
    x = x + g2 * _moe(h2, w_router, b_router, w_gate_e, w_up_e, w_down_e,
                      w_gate_s, w_up_s, w_down_s)
    states.append(up[:, -(CONV_K - 1):])
    return x, states


def setup_inputs(seed: int = 0) -> dict:
    key = jax.random.key(seed)
    ks = jax.random.split(key, 40)
    f32 = jnp.float32

    def nrm(k, shape, s):
        return jax.random.normal(k, shape, f32) * s

    wbs = [min(w, PAST_LEN) for w in WINDOWS]
    cshape = lambda wb: (DEPTH, DEC_BATCH, wb, HEADS_PER_GROUP, HEAD_DIM)
    return {
        'x_prompt': nrm(ks[0], (BATCH, SEQ, D_MODEL), 1.0),
        'x_sample': nrm(ks[1], (DEC_BATCH, DEC_SEQ, D_MODEL), 1.0),
        'c_prompt': nrm(ks[2], (BATCH, D_MODEL), 1.0),
        'c_sample': nrm(ks[3], (DEC_BATCH, D_MODEL), 1.0),
        'cache_k_w128': nrm(ks[4], cshape(wbs[0]), 1.0),
        'cache_v_w128': nrm(ks[5], cshape(wbs[0]), 1.0),
        'cache_k_w512': nrm(ks[6], cshape(wbs[1]), 1.0),
        'cache_v_w512': nrm(ks[7], cshape(wbs[1]), 1.0),
        'cache_k_w2048': nrm(ks[8], cshape(wbs[2]), 1.0),
        'cache_v_w2048': nrm(ks[9], cshape(wbs[2]), 1.0),
        'state_conv': nrm(ks[10], (DEPTH, DEC_BATCH, CONV_K - 1, CONV_CH), 1.0),
        'rel_bias': nrm(ks[11], (N_BUCKETS, N_ATTN_HEADS), 0.5),
        'norm1_g': 1.0 + nrm(ks[12], (DEPTH, D_MODEL), 0.01),
        'norm2_g': 1.0 + nrm(ks[13], (DEPTH, D_MODEL), 0.01),
        'w_ada': nrm(ks[14], (DEPTH, D_MODEL, 6 * D_MODEL), 0.5 * D_MODEL ** -0.5),
        'b_ada': nrm(ks[15], (DEPTH, 6 * D_MODEL), 0.02),
        'w_in': nrm(ks[16], (DEPTH, D_MODEL, N_PROJ), D_MODEL ** -0.5),
        'q_norm_g': 1.0 + nrm(ks[17], (DEPTH, HEAD_DIM), 0.01),
        'k_norm_g': 1.0 + nrm(ks[18], (DEPTH, HEAD_DIM), 0.01),
        'conv_w': nrm(ks[19], (DEPTH, CONV_K, CONV_CH), CONV_K ** -0.5),
        'w_o': nrm(ks[20], (DEPTH, ATTN_WIDTH + CONV_CH, D_MODEL), (ATTN_WIDTH + CONV_CH) ** -0.5),
        'w_router': nrm(ks[21], (DEPTH, D_MODEL, N_EXPERTS), D_MODEL ** -0.5),
        'b_router': nrm(ks[22], (DEPTH, N_EXPERTS), 0.01),
        'w_gate_e': nrm(ks[23], (DEPTH, N_EXPERTS, D_MODEL, D_EXPERT), D_MODEL ** -0.5),
        'w_up_e': nrm(ks[24], (DEPTH, N_EXPERTS, D_MODEL, D_EXPERT), D_MODEL ** -0.5),
        'w_down_e': nrm(ks[25], (DEPTH, N_EXPERTS, D_EXPERT, D_MODEL), D_EXPERT ** -0.5),
        'w_gate_s': nrm(ks[26], (DEPTH, D_MODEL, D_EXPERT), D_MODEL ** -0.5),
        'w_up_s': nrm(ks[27], (DEPTH, D_MODEL, D_EXPERT), D_MODEL ** -0.5),
        'w_down_s': nrm(ks[28], (DEPTH, D_EXPERT, D_MODEL), D_EXPERT ** -0.5),
    }


def reference(x_prompt, x_sample, c_prompt, c_sample, cache_k_w128, cache_v_w128,
              cache_k_w512, cache_v_w512, cache_k_w2048, cache_v_w2048, state_conv,
              rel_bias, norm1_g, norm2_g, w_ada, b_ada, w_in, q_norm_g, k_norm_g, conv_w,
              w_o, w_router, b_router, w_gate_e, w_up_e, w_down_e, w_gate_s, w_up_s,
              w_down_s):
    layer_w = (norm1_g, norm2_g, w_ada, b_ada, w_in, q_norm_g, k_norm_g, conv_w, w_o,
               w_router, b_router, w_gate_e, w_up_e, w_down_e, w_gate_s, w_up_s, w_down_s)
    caches_k = (cache_k_w128, cache_k_w512, cache_k_w2048)
    caches_v = (cache_v_w128, cache_v_w512, cache_v_w2048)

    def prompt_attend(g, q, k, v, bias_k):
        o, lse = _dilated_attn_prompt(q, k, v, bias_k, DILATIONS[g])
        keep = min(WINDOWS[g], k.shape[1])
        return o, lse, k[:, -keep:], v[:, -keep:]

    yp = x_prompt
    states_p = []
    for l in range(DEPTH):
        conv0 = jnp.zeros((yp.shape[0], CONV_K - 1, CONV_CH), yp.dtype)
        yp, st = _trunk_layer(yp, c_prompt, conv0, prompt_attend, rel_bias,
                              *[w[l] for w in layer_w])
        states_p.append(st)

    ys = x_sample
    states_s = []
    for l in range(DEPTH):
        def sample_attend(g, q, k, v, bias_k, l=l):
            return _dilated_attn_sample(q, k, v, caches_k[g][l], caches_v[g][l],
                                        bias_k, DILATIONS[g])
        ys, st = _trunk_layer(ys, c_sample, state_conv[l], sample_attend, rel_bias,
                              *[w[l] for w in layer_w])
        states_s.append(st)

    sp = [jnp.stack([states_p[l][i] for l in range(DEPTH)]) for i in range(7)]
    ss = [jnp.stack([states_s[l][i] for l in range(DEPTH)]) for i in range(7)]
    k128_p, v128_p, k512_p, v512_p, k2048_p, v2048_p, conv_p = sp
    k128_s, v128_s, k512_s, v512_s, k2048_s, v2048_s, conv_s = ss
    return (yp, ys, k128_p, v128_p, k512_p, v512_p, k2048_p, v2048_p, conv_p,
            k128_s, v128_s, k512_s, v512_s, k2048_s, v2048_s, conv_s)
```

```python
import functools
import math

import jax
import jax.numpy as jnp
from jax import lax
from jax.experimental import pallas as pl
from jax.experimental.pallas import tpu as pltpu

D_MODEL = 1024
HEAD_DIM = 64
HEADS_PER_GROUP = 4
GROUP_W = HEADS_PER_GROUP * HEAD_DIM
WINDOWS = (128, 512, 2048)
DILATIONS = (1, 4, 16)
N_GROUPS = 3
ATTN_W = N_GROUPS * GROUP_W
CONV_CH = 256
N_PROJ = 3 * ATTN_W + 3 * CONV_CH
N_STEPS = 128
N_KEYS = N_STEPS + 1
BLK = 128
N_BUCKETS = 32
MAX_DISTANCE = WINDOWS[-1]
N_EXPERTS = 64
TOP_K = 8
N_ROUTE_GROUPS = 8
GROUP_SIZE = N_EXPERTS // N_ROUTE_GROUPS
TOPK_GROUPS = 4
D_EXPERT = 256
ROUTED_SCALE = 2.5
RMS_EPS = 1e-6
NEG = -1e30
LANES = 128
VMEM_LIMIT = 56 * 1024 * 1024

F32 = jnp.float32
BF16 = jnp.bfloat16


def _dot(a, b):
    return jnp.dot(a, b, preferred_element_type=F32)


def _dot_nt(a, b):
    return lax.dot_general(a, b, (((1,), (1,)), ((), ())), preferred_element_type=F32)


def _silu(a):
    return a / (1.0 + jnp.exp(-a))


def _params(sem):
    return pltpu.CompilerParams(dimension_semantics=sem, vmem_limit_bytes=VMEM_LIMIT)


def _ada_kernel(c_ref, w_ref, b_ref, o_ref):
    s = _silu(c_ref[...])
    o_ref[...] = _dot(s.astype(BF16), w_ref[...].astype(BF16)) + b_ref[...]


def _ada(c, w_ada, b_ada):
    n = c.shape[0]
    nc = w_ada.shape[1] // D_MODEL
    return pl.pallas_call(
        _ada_kernel,
        grid=(nc,),
        in_specs=[pl.BlockSpec((n, D_MODEL), lambda j: (0, 0)),
                  pl.BlockSpec((D_MODEL, D_MODEL), lambda j: (0, j)),
                  pl.BlockSpec((1, D_MODEL), lambda j: (0, j))],
        out_specs=pl.BlockSpec((n, D_MODEL), lambda j: (0, j)),
        out_shape=jax.ShapeDtypeStruct((n, w_ada.shape[1]), F32),
        compiler_params=_params(("arbitrary",)),
        name="ada",
    )(c, w_ada, b_ada)


def _proj_kernel(*refs, tile, dils, sample):
    if sample:
        (x_ref, sc_ref, sh_ref, ng_ref, w_ref, qg_ref, kg_ref, cw_ref, bd_ref, ha_ref, hb_ref,
         qf_ref, kf_ref, vf_ref, u_ref, co_ref, u_scr) = refs
    else:
        (x_ref, sc_ref, sh_ref, ng_ref, w_ref, qg_ref, kg_ref, cw_ref, bd_ref,
         q0, k0, v0, q1, k1, v1, q2, k2, v2, kf_ref, vf_ref, u_ref, co_ref, slab, u_scr) = refs
        qkv_out = ((q0, k0, v0), (q1, k1, v1), (q2, k2, v2))

    x = x_ref[0]
    ms = jnp.mean(x * x, axis=-1, keepdims=True)
    h = x * lax.rsqrt(ms + RMS_EPS) * ng_ref[...]
    h = h * (1.0 + sc_ref[0]) + sh_ref[0]
    proj = _dot(h.astype(BF16), w_ref[...])

    bd = bd_ref[...]

    def headnorm(z, g):
        zz = z * z
        hi = zz.astype(BF16)
        lo = (zz - hi.astype(F32)).astype(BF16)
        msq = _dot(hi, bd) + _dot(lo, bd)
        return z * lax.rsqrt(msq + RMS_EPS) * g

    slab_i = 0
    for g in range(N_GROUPS):
        c0 = g * GROUP_W
        qn = headnorm(proj[:, c0:c0 + GROUP_W], qg_ref[...]) * (HEAD_DIM ** -0.5)
        kn = headnorm(proj[:, ATTN_W + c0:ATTN_W + c0 + GROUP_W], kg_ref[...])
        vv = proj[:, 2 * ATTN_W + c0:2 * ATTN_W + c0 + GROUP_W]
        kf_ref[0, :, c0:c0 + GROUP_W] = kn
        vf_ref[0, :, c0:c0 + GROUP_W] = vv
        if sample:
            qf_ref[0, :, c0:c0 + GROUP_W] = qn
            continue
        d = dils[g]
        for val, out in zip((qn, kn, vv), qkv_out[g]):
            if d == 1:
                out[0, 0] = val.astype(BF16)
                continue
            n = tile // d
            for half in range(GROUP_W // LANES):
                slab[slab_i] = val[:, half * LANES:(half + 1) * LANES]
                for r in range(d):
                    out[0, r, :, half * LANES:(half + 1) * LANES] = (
                        slab[slab_i, pl.ds(r, n, stride=d), :].astype(BF16))
                slab_i += 1

    base = 3 * ATTN_W
    u = proj[:, base + 2 * CONV_CH:base + 3 * CONV_CH] * proj[:, base:base + CONV_CH]
    gate_b = proj[:, base + CONV_CH:base + 2 * CONV_CH]
    if sample:
        u_scr[0:8, :] = jnp.zeros((8, CONV_CH), F32)
    else:
        j = pl.program_id(1)

        @pl.when(j == 0)
        def _():
            u_scr[0:8, :] = jnp.zeros((8, CONV_CH), F32)

        @pl.when(j > 0)
        def _():
            u_scr[0:8, :] = u_scr[tile:tile + 8, :]

    u_scr[8:tile + 8, :] = u
    um1 = u_scr[7:tile + 7, :]
    um2 = u_scr[6:tile + 6, :]
    if sample:
        t = lax.broadcasted_iota(jnp.int32, (tile, CONV_CH), 0) % 4
        um1 = jnp.where(t >= 1, um1, 0.0) + hb_ref[...]
        um2 = jnp.where(t >= 2, um2, 0.0) + ha_ref[...]
    cw = cw_ref[...]
    conv = cw[0:1] * um2 + cw[1:2] * um1 + cw[2:3] * u
    u_ref[0] = u
    co_ref[0] = gate_b * conv


def _proj(x, sc, sh, ng, w_bf, qg, kg, cw, bd, hist, *, tile, dils, sample):
    B, S, _ = x.shape
    nt = S // tile
    tm = sc.shape[1]
    mod_spec = pl.BlockSpec((1, tm, D_MODEL), (lambda b, j: (b, j, 0)) if tm > 1 else (lambda b, j: (b, 0, 0)))
    const2 = lambda shape: pl.BlockSpec(shape, lambda b, j: (0, 0))
    in_specs = [pl.BlockSpec((1, tile, D_MODEL), lambda b, j: (b, j, 0)), mod_spec, mod_spec,
                const2((1, D_MODEL)), const2((D_MODEL, N_PROJ)), const2((1, GROUP_W)),
                const2((1, GROUP_W)), const2((3, CONV_CH)), const2((GROUP_W, GROUP_W))]
    args = [x, sc, sh, ng, w_bf, qg, kg, cw, bd]
    nat = lambda w: pl.BlockSpec((1, tile, w), lambda b, j: (b, j, 0))
    out_specs, out_shape = [], []
    scratch = []
    if sample:
        in_specs += [pl.BlockSpec((tile, CONV_CH), lambda b, j: (j, 0))] * 2
        args += list(hist)
        out_specs.append(nat(ATTN_W))
        out_shape.append(jax.ShapeDtypeStruct((B, S, ATTN_W), F32))
    else:
        for d in dils:
            for _ in range(3):
                out_specs.append(pl.BlockSpec((1, d, tile // d, GROUP_W), lambda b, j: (b, 0, j, 0)))
                out_shape.append(jax.ShapeDtypeStruct((B, d, S // d, GROUP_W), BF16))
        n_slabs = sum(3 * (GROUP_W // LANES) for d in dils if d > 1)
        scratch.append(pltpu.VMEM((n_slabs, tile, LANES), F32))
    out_specs += [nat(ATTN_W), nat(ATTN_W), nat(CONV_CH), nat(CONV_CH)]
    out_shape += [jax.ShapeDtypeStruct((B, S, ATTN_W), F32)] * 2
    out_shape += [jax.ShapeDtypeStruct((B, S, CONV_CH), F32)] * 2
    scratch.append(pltpu.VMEM((tile + 8, CONV_CH), F32))
    return pl.pallas_call(
        functools.partial(_proj_kernel, tile=tile, dils=dils, sample=sample),
        grid=(B, nt),
        in_specs=in_specs,
        out_specs=out_specs,
        out_shape=out_shape,
        scratch_shapes=scratch,
        compiler_params=_params(("arbitrary", "arbitrary")),
        name="proj_sample" if sample else "proj_prompt",
    )(*args)


def _attn_kernel(q_ref, kp_ref, kc_ref, vp_ref, vc_ref, b_ref, o_ref, l_ref):
    first = pl.program_id(1) == 0
    for hh in range(HEADS_PER_GROUP):
        sl = slice(hh * HEAD_DIM, (hh + 1) * HEAD_DIM)
        q = q_ref[0, :, sl]
        sp = _dot_nt(q, kp_ref[0, :, sl]) + b_ref[hh, :, 0:BLK]
        sp = jnp.where(first, NEG, sp)
        sc = _dot_nt(q, kc_ref[0, :, sl]) + b_ref[hh, :, BLK:2 * BLK]
        m = jnp.maximum(jnp.max(sp, axis=-1, keepdims=True), jnp.max(sc, axis=-1, keepdims=True))
        pp = jnp.exp(sp - m)
        pc = jnp.exp(sc - m)
        z = jnp.sum(pp, axis=-1, keepdims=True) + jnp.sum(pc, axis=-1, keepdims=True)
        o = _dot(pp.astype(BF16), vp_ref[0, :, sl]) + _dot(pc.astype(BF16), vc_ref[0, :, sl])
        o_ref[0, :, sl] = o / z
        l_ref[0, :, sl] = jnp.broadcast_to(m + jnp.log(z), (BLK, HEAD_DIM))


def _attn_prompt(q, k, v, bias):
    Z, L, _ = q.shape
    cur = pl.BlockSpec((1, BLK, GROUP_W), lambda z, i: (z, i, 0))
    prev = pl.BlockSpec((1, BLK, GROUP_W), lambda z, i: (z, jnp.maximum(i - 1, 0), 0))
    return pl.pallas_call(
        _attn_kernel,
        grid=(Z, L // BLK),
        in_specs=[cur, prev, cur, prev, cur,
                  pl.BlockSpec((HEADS_PER_GROUP, BLK, 2 * BLK), lambda z, i: (0, 0, 0))],
        out_specs=[cur, cur],
        out_shape=[jax.ShapeDtypeStruct((Z, L, GROUP_W), F32)] * 2,
        compiler_params=_params(("arbitrary", "arbitrary")),
        name="attn_prompt",
    )(q, k, k, v, v, bias)


def _attn_sample_kernel(q_ref, kc_ref, vc_ref, nk_ref, nv_ref, b1_ref, b2_ref,
                        ko_ref, vo_ref, o_ref, l_ref, *, bb, wb):
    lane = lax.broadcasted_iota(jnp.int32, (HEAD_DIM, LANES), 1)
    new_cols = lane >= LANES - 4
    for b in range(bb):
        for hh in range(HEADS_PER_GROUP):
            q = q_ref[b, hh]
            kt = kc_ref[b, hh]
            vt = vc_ref[b, hh]
            nkt = nk_ref[b, hh]
            nvt = nv_ref[b, hh]
            s1 = _dot(q, kt.astype(BF16)) + b1_ref[hh]
            s2 = _dot(q, nkt.astype(BF16)) + b2_ref[hh]
            m = jnp.maximum(jnp.max(s1, axis=-1, keepdims=True), jnp.max(s2, axis=-1, keepdims=True))
            p1 = jnp.exp(s1 - m)
            p2 = jnp.exp(s2 - m)
            z = jnp.sum(p1, axis=-1, keepdims=True) + jnp.sum(p2, axis=-1, keepdims=True)
            o = _dot_nt(p1.astype(BF16), vt.astype(BF16)) + _dot_nt(p2.astype(BF16), nvt.astype(BF16))
            o_ref[b, hh] = o / z
            l_ref[b, hh] = jnp.broadcast_to(m + jnp.log(z), (8, LANES))
            for src, new, dst in ((kt, nkt, ko_ref), (vt, nvt, vo_ref)):
                rolled = pltpu.roll(src, wb - 4, axis=1)
                if wb > LANES:
                    dst[b, hh, :, 0:wb - LANES] = rolled[:, 0:wb - LANES]
                dst[b, hh, :, wb - LANES:wb] = jnp.where(new_cols, new, rolled[:, wb - LANES:wb])


def _attn_sample(q, kc, vc, nk, nv, b1, b2, *, bb):
    nb, _, _, wb = kc.shape
    blk = lambda *tail: pl.BlockSpec((bb, HEADS_PER_GROUP) + tail, lambda i: (i, 0, 0, 0))
    const = lambda shape: pl.BlockSpec(shape, lambda i: (0, 0, 0))
    return pl.pallas_call(
        functools.partial(_attn_sample_kernel, bb=bb, wb=wb),
        grid=(nb // bb,),
        in_specs=[blk(8, HEAD_DIM), blk(HEAD_DIM, wb), blk(HEAD_DIM, wb), blk(HEAD_DIM, LANES),
                  blk(HEAD_DIM, LANES), const((HEADS_PER_GROUP, 8, wb)), const((HEADS_PER_GROUP, 8, LANES))],
        out_specs=[blk(HEAD_DIM, wb), blk(HEAD_DIM, wb), blk(8, HEAD_DIM), blk(8, LANES)],
        out_shape=[jax.ShapeDtypeStruct(kc.shape, F32), jax.ShapeDtypeStruct(kc.shape, F32),
                   jax.ShapeDtypeStruct((nb, HEADS_PER_GROUP, 8, HEAD_DIM), F32),
                   jax.ShapeDtypeStruct((nb, HEADS_PER_GROUP, 8, LANES), F32)],
        compiler_params=_params(("arbitrary",)),
        name="attn_sample",
    )(q, kc, vc, nk, nv, b1, b2)


def _first_max(v, ids, sentinel):
    m = jnp.max(v, axis=0, keepdims=True)
    idx = jnp.min(jnp.where(v == m, ids, sentinel), axis=0, keepdims=True)
    return m, ids == idx


def _oproj_kernel(x_ref, o0, o1, o2, l0, l1, l2, co_ref, g1_ref, sc_ref, sh_ref, ng_ref, wo_ref,
                  wr_ref, br_ref, x1_ref, h2_ref, gt_ref, slab, *, tile, dils):
    o_refs, l_refs = (o0, o1, o2), (l0, l1, l2)
    outs, lses = [], []
    slab_i = 0
    for g, d in enumerate(dils):
        if d == 1:
            outs.append(o_refs[g][0, 0])
            lses.append(l_refs[g][0, 0])
            continue
        n = tile // d
        for ref, dest in ((o_refs[g], outs), (l_refs[g], lses)):
            halves = []
            for half in range(GROUP_W // LANES):
                for r in range(d):
                    slab[slab_i, pl.ds(r, n, stride=d), :] = ref[0, r, :, half * LANES:(half + 1) * LANES]
                halves.append(slab[slab_i])
                slab_i += 1
            dest.append(jnp.concatenate(halves, axis=1))

    m = jnp.maximum(jnp.maximum(lses[0], lses[1]), lses[2])
    es = [jnp.exp(l - m) for l in lses]
    den = es[0] + es[1] + es[2]
    cat = [(outs[g] * (es[g] / den)).astype(BF16) for g in range(N_GROUPS)]
    cat.append(co_ref[0].astype(BF16))
    cat = jnp.concatenate(cat, axis=1)
    x1 = x_ref[0] + g1_ref[0] * _dot(cat, wo_ref[...])
    x1_ref[0] = x1

    ms = jnp.mean(x1 * x1, axis=-1, keepdims=True)
    h2 = x1 * lax.rsqrt(ms + RMS_EPS) * ng_ref[...]
    h2 = h2 * (1.0 + sc_ref[0]) + sh_ref[0]
    hh = h2.astype(BF16)
    h2_ref[0] = hh

    hl = (h2 - hh.astype(F32)).astype(BF16)
    wr = wr_ref[...]
    wh = wr.astype(BF16)
    wl = (wr - wh.astype(F32)).astype(BF16)
    logits = _dot_nt(wh, hh) + _dot_nt(wh, hl) + _dot_nt(wl, hh)
    scores = 1.0 / (1.0 + jnp.exp(-logits))
    sel = scores + br_ref[...]

    ids = lax.broadcasted_iota(jnp.int32, (GROUP_SIZE, tile), 0)
    ninf = -jnp.inf
    sel_g = [sel[g * GROUP_SIZE:(g + 1) * GROUP_SIZE] for g in range(N_ROUTE_GROUPS)]
    gscore = jnp.zeros((N_ROUTE_GROUPS, tile), F32)
    for g in range(N_ROUTE_GROUPS):
        m1, oh = _first_max(sel_g[g], ids, GROUP_SIZE)
        m2 = jnp.max(jnp.where(oh, ninf, sel_g[g]), axis=0, keepdims=True)
        gscore = jnp.where(ids == g, m1 + m2, gscore)
    gsel = jnp.zeros((N_ROUTE_GROUPS, tile), F32)
    for _ in range(TOPK_GROUPS):
        _, oh = _first_max(gscore, ids, N_ROUTE_GROUPS)
        gsel = jnp.where(oh, 1.0, gsel)
        gscore = jnp.where(oh, ninf, gscore)
    cand = [jnp.where(gsel[g:g + 1] > 0.0, sel_g[g], ninf) for g in range(N_ROUTE_GROUPS)]
    chosen = [jnp.zeros((GROUP_SIZE, tile), F32) for _ in range(N_ROUTE_GROUPS)]
    for _ in range(TOP_K):
        mx = cand[0]
        for g in range(1, N_ROUTE_GROUPS):
            mx = jnp.maximum(mx, cand[g])
        mx = jnp.max(mx, axis=0, keepdims=True)
        idx = jnp.where(cand[0] == mx, ids, N_EXPERTS)
        for g in range(1, N_ROUTE_GROUPS):
            idx = jnp.minimum(idx, jnp.where(cand[g] == mx, ids + g * GROUP_SIZE, N_EXPERTS))
        idx = jnp.min(idx, axis=0, keepdims=True)
        for g in range(N_ROUTE_GROUPS):
            oh = (ids + g * GROUP_SIZE) == idx
            chosen[g] = jnp.where(oh, 1.0, chosen[g])
            cand[g] = jnp.where(oh, ninf, cand[g])
    wts = [jnp.where(chosen[g] > 0.0, scores[g * GROUP_SIZE:(g + 1) * GROUP_SIZE], 0.0)
           for g in range(N_ROUTE_GROUPS)]
    wsum = jnp.sum(wts[0], axis=0, keepdims=True)
    for g in range(1, N_ROUTE_GROUPS):
        wsum = wsum + jnp.sum(wts[g], axis=0, keepdims=True)
    gates_t = jnp.concatenate([w / wsum * ROUTED_SCALE for w in wts]
                              + [jnp.zeros((LANES - N_EXPERTS, tile), F32)], axis=0)
    gt_ref[0] = gates_t.T


def _oproj(x, o_list, l_list, co, g1, sc, sh, ng, wo_bf, wr_t, br, *, tile, dils):
    B, S, _ = x.shape
    tm = g1.shape[1]
    mod_spec = pl.BlockSpec((1, tm, D_MODEL), (lambda b, j: (b, j, 0)) if tm > 1 else (lambda b, j: (b, 0, 0)))
    const2 = lambda shape: pl.BlockSpec(shape, lambda b, j: (0, 0))
    nat = lambda w: pl.BlockSpec((1, tile, w), lambda b, j: (b, j, 0))
    dspec = [pl.BlockSpec((1, d, tile // d, GROUP_W), lambda b, j: (b, 0, j, 0)) for d in dils]
    n_slabs = sum(2 * (GROUP_W // LANES) for d in dils if d > 1)
    return pl.pallas_call(
        functools.partial(_oproj_kernel, tile=tile, dils=dils),
        grid=(B, S // tile),
        in_specs=[nat(D_MODEL)] + dspec + dspec + [nat(CONV_CH), mod_spec, mod_spec, mod_spec,
                  const2((1, D_MODEL)), const2((D_MODEL, D_MODEL)), const2((N_EXPERTS, D_MODEL)),
                  const2((N_EXPERTS, 1))],
        out_specs=[nat(D_MODEL), nat(D_MODEL), nat(LANES)],
        out_shape=[jax.ShapeDtypeStruct((B, S, D_MODEL), F32), jax.ShapeDtypeStruct((B, S, D_MODEL), BF16),
                   jax.ShapeDtypeStruct((B, S, LANES), F32)],
        scratch_shapes=[pltpu.VMEM((max(n_slabs, 1), tile, LANES), F32)],
        compiler_params=_params(("arbitrary", "arbitrary")),
        name="oproj",
    )(x, *o_list, *l_list, co, g1, sc, sh, ng, wo_bf, wr_t, br)


def _moe_kernel(h_ref, gt_ref, x1_ref, g2_ref, wg_ref, wu_ref, wd_ref, wgs_ref, wus_ref, wds_ref,
                y_ref, acc, *, tile):
    e = pl.program_id(2)
    h = h_ref[0]

    @pl.when(e == 0)
    def _():
        a = _dot(h, wgs_ref[...].astype(BF16))
        b = _dot(h, wus_ref[...].astype(BF16))
        acc[...] = _dot((_silu(a) * b).astype(BF16), wds_ref[...].astype(BF16))

    a = _dot(h, wg_ref[0].astype(BF16))
    b = _dot(h, wu_ref[0].astype(BF16))
    lane = lax.broadcasted_iota(jnp.int32, (tile, LANES), 1)
    gcol = jnp.sum(jnp.where(lane == e, gt_ref[0], 0.0), axis=1, keepdims=True)
    hc = _silu(a) * b * gcol
    acc[...] += _dot(hc.astype(BF16), wd_ref[0].astype(BF16))

    @pl.when(e == N_EXPERTS - 1)
    def _():
        y_ref[0] = x1_ref[0] + g2_ref[0] * acc[...]


def _moe(h2, gates, x1, g2, wg, wu, wd, wgs, wus, wds, *, tile):
    B, S, _ = h2.shape
    tm = g2.shape[1]
    mod_spec = pl.BlockSpec((1, tm, D_MODEL),
                            (lambda b, j, e: (b, j, 0)) if tm > 1 else (lambda b, j, e: (b, 0, 0)))
    nat = lambda w: pl.BlockSpec((1, tile, w), lambda b, j, e: (b, j, 0))
    const2 = lambda shape: pl.BlockSpec(shape, lambda b, j, e: (0, 0))
    return pl.pallas_call(
        functools.partial(_moe_kernel, tile=tile),
        grid=(B, S // tile, N_EXPERTS),
        in_specs=[nat(D_MODEL), nat(LANES), nat(D_MODEL), mod_spec,
                  pl.BlockSpec((1, D_MODEL, D_EXPERT), lambda b, j, e: (e, 0, 0)),
                  pl.BlockSpec((1, D_MODEL, D_EXPERT), lambda b, j, e: (e, 0, 0)),
                  pl.BlockSpec((1, D_EXPERT, D_MODEL), lambda b, j, e: (e, 0, 0)),
                  const2((D_MODEL, D_EXPERT)), const2((D_MODEL, D_EXPERT)), const2((D_EXPERT, D_MODEL))],
        out_specs=nat(D_MODEL),
        out_shape=jax.ShapeDtypeStruct((B, S, D_MODEL), F32),
        scratch_shapes=[pltpu.VMEM((tile, D_MODEL), F32)],
        compiler_params=_params(("arbitrary", "arbitrary", "arbitrary")),
        name="moe",
    )(h2, gates, x1, g2, wg, wu, wd, wgs, wus, wds)


def _t5_bucket(dist):
    max_exact = N_BUCKETS // 2
    df = jnp.maximum(dist, 1).astype(F32)
    large = max_exact + (jnp.log(df / max_exact) / math.log(MAX_DISTANCE / max_exact)
                         * (N_BUCKETS - max_exact)).astype(jnp.int32)
    large = jnp.minimum(large, N_BUCKETS - 1)
    return jnp.where(dist < max_exact, dist, large)


def _step_bias(rel_bias, g):
    dist = jnp.arange(N_KEYS, dtype=jnp.int32) * DILATIONS[g]
    cols = rel_bias[:, g * HEADS_PER_GROUP:(g + 1) * HEADS_PER_GROUP]
    return cols[_t5_bucket(dist)].T.astype(F32)


def _prompt_bias(bias_k):
    qi = jnp.arange(BLK)[:, None]
    kj = jnp.arange(2 * BLK)[None, :]
    step = qi + BLK - kj
    valid = (step >= 0) & (step <= N_STEPS)
    return jnp.where(valid[None], bias_k[:, jnp.clip(step, 0, N_STEPS)], NEG)


def _sample_bias(bias_k, wb, d):
    t = jnp.arange(8)[:, None]
    pos = jnp.arange(wb)[None, :]
    delta = wb + t - pos
    step = delta // d
    valid = (t < 4) & (delta % d == 0) & (step <= N_STEPS)
    b1 = jnp.where(valid[None], bias_k[:, jnp.clip(step, 0, N_STEPS)], jnp.where(t < 4, NEG, 0.0)[None])
    tn = jnp.arange(LANES)[None, :] - (LANES - 4)
    dn = t - tn
    validn = (t < 4) & (tn >= 0) & (dn >= 0) & (dn % d == 0)
    b2 = jnp.where(validn[None], bias_k[:, jnp.clip(dn // d, 0, N_STEPS)], jnp.where(t < 4, NEG, 0.0)[None])
    return b1.astype(F32), b2.astype(F32)


def _layer(x, mods, hist, caches, rel_bias, weights, *, sample):
    (norm1_g, norm2_g, w_in_bf, qg, kg, conv_w, bd, w_o_bf, w_r_t, b_r, w_gate_e, w_up_e, w_down_e,
     w_gate_s, w_up_s, w_down_s) = weights
    sh1, sc1, g1, sh2, sc2, g2 = mods
    B, S, _ = x.shape
    tile = 512
    dils = (1, 1, 1) if sample else DILATIONS
    res = _proj(x, sc1, sh1, norm1_g, w_in_bf, qg, kg, conv_w, bd, hist, tile=tile, dils=dils, sample=sample)
    states = []
    o_list, l_list = [], []
    if sample:
        qf, kf, vf, u, co = res
        nb = S // 4

        def heads_t(a, g):
            a = a[0, :, g * GROUP_W:(g + 1) * GROUP_W].reshape(nb, 4, HEADS_PER_GROUP, HEAD_DIM)
            return a.transpose(0, 2, 1, 3)

        for g in range(N_GROUPS):
            d, wb = DILATIONS[g], WINDOWS[g]
            kc, vc = caches[g]
            bias_k = _step_bias(rel_bias, g)
            b1, b2 = _sample_bias(bias_k, wb, d)
            q = jnp.pad(heads_t(qf, g), ((0, 0), (0, 0), (0, 4), (0, 0))).astype(BF16)
            new_t = lambda a: jnp.pad(heads_t(a, g).transpose(0, 1, 3, 2),
                                      ((0, 0), (0, 0), (0, 0), (LANES - 4, 0)))
            ko, vo, o, lse = _attn_sample(q, kc, vc, new_t(kf), new_t(vf), b1, b2,
                                          bb={128: 8, 512: 4, 2048: 1}[wb])
            states += [ko.transpose(0, 3, 1, 2)[None], vo.transpose(0, 3, 1, 2)[None]]
            o = o[:, :, :4].transpose(0, 2, 1, 3).reshape(1, 1, S, GROUP_W)
            lse = jnp.broadcast_to(lse[:, :, :4, :1], (nb, HEADS_PER_GROUP, 4, HEAD_DIM))
            o_list.append(o)
            l_list.append(lse.transpose(0, 2, 1, 3).reshape(1, 1, S, GROUP_W))
        states.append(u.reshape(nb, 4, CONV_CH)[:, 2:][None])
    else:
        qkv, (kf, vf, u, co) = res[:9], res[9:]
        for g in range(N_GROUPS):
            d, w = DILATIONS[g], WINDOWS[g]
            q, k, v = (a.reshape(B * d, S // d, GROUP_W) for a in qkv[3 * g:3 * g + 3])
            o, lse = _attn_prompt(q, k, v, _prompt_bias(_step_bias(rel_bias, g)))
            o_list.append(o.reshape(B, d, S // d, GROUP_W))
            l_list.append(lse.reshape(B, d, S // d, GROUP_W))
            keep = lambda a: a[:, S - w:, g * GROUP_W:(g + 1) * GROUP_W].reshape(
                B, w, HEADS_PER_GROUP, HEAD_DIM)[None]
            states += [keep(kf), keep(vf)]
        states.append(u[:, S - 2:][None])
    x1, h2, gates = _oproj(x, o_list, l_list, co, g1, sc2, sh2, norm2_g, w_o_bf, w_r_t, b_r,
                           tile=tile, dils=dils)
    y = _moe(h2, gates, x1, g2, w_gate_e, w_up_e, w_down_e, w_gate_s, w_up_s, w_down_s,
             tile=512 if sample else 1024)
    return y, states


def kernel(x_prompt, x_sample, c_prompt, c_sample, cache_k_w128, cache_v_w128, cache_k_w512, cache_v_w512, cache_k_w2048, cache_v_w2048, state_conv, rel_bias, norm1_g, norm2_g, w_ada, b_ada, w_in, q_norm_g, k_norm_g, conv_w, w_o, w_router, b_router, w_gate_e, w_up_e, w_down_e, w_gate_s, w_up_s, w_down_s):
    B = x_prompt.shape[0]
    DB, T = x_sample.shape[:2]
    n_c = B + DB
    c_all = jnp.pad(jnp.concatenate([c_prompt, c_sample], axis=0), ((0, (-n_c) % 8), (0, 0)))
    mod = _ada(c_all, w_ada[0], b_ada)
    chunks = [mod[:, i * D_MODEL:(i + 1) * D_MODEL] for i in range(6)]
    mods_p = [c[:B].reshape(B, 1, D_MODEL) for c in chunks]
    mods_s = [jnp.repeat(c[B:n_c], T, axis=0).reshape(1, DB * T, D_MODEL) for c in chunks]

    eye = jnp.arange(GROUP_W) // HEAD_DIM
    bd = jnp.where(eye[:, None] == eye[None, :], 1.0 / HEAD_DIM, 0.0).astype(BF16)
    weights = (norm1_g, norm2_g, w_in[0].astype(BF16),
               jnp.tile(q_norm_g, (1, HEADS_PER_GROUP)), jnp.tile(k_norm_g, (1, HEADS_PER_GROUP)),
               conv_w[0], bd, w_o[0].astype(BF16), w_router[0].T, b_router.reshape(N_EXPERTS, 1),
               w_gate_e[0], w_up_e[0], w_down_e[0], w_gate_s[0], w_up_s[0], w_down_s[0])

    yp, st_p = _layer(x_prompt, mods_p, None, None, rel_bias, weights, sample=False)

    s0, s1 = state_conv[0, :, 0], state_conv[0, :, 1]
    zero = jnp.zeros_like(s0)
    hist_a = jnp.stack([s0, s1, zero, zero], axis=1).reshape(DB * T, CONV_CH)
    hist_b = jnp.stack([s1, zero, zero, zero], axis=1).reshape(DB * T, CONV_CH)
    caches = [(ck[0].transpose(0, 2, 3, 1), cv[0].transpose(0, 2, 3, 1))
              for ck, cv in ((cache_k_w128, cache_v_w128), (cache_k_w512, cache_v_w512),
                             (cache_k_w2048, cache_v_w2048))]
    ys, st_s = _layer(x_sample.reshape(1, DB * T, D_MODEL), mods_s, (hist_a, hist_b), caches, rel_bias,
                      weights, sample=True)
    return (yp, ys.reshape(DB, T, D_MODEL), *st_p, *st_s)
```

```python
import functools
import math

import jax
import jax.numpy as jnp
from jax import lax
from jax.experimental import pallas as pl
from jax.experimental.pallas import tpu as pltpu

D_MODEL = 1024
HEAD_DIM = 64
HEADS_PER_GROUP = 4
GROUP_W = HEADS_PER_GROUP * HEAD_DIM
WINDOWS = (128, 512, 2048)
DILATIONS = (1, 4, 16)
N_GROUPS = 3
ATTN_W = N_GROUPS * GROUP_W
CONV_CH = 256
N_PROJ = 3 * ATTN_W + 3 * CONV_CH
N_STEPS = 128
N_KEYS = N_STEPS + 1
BLK = 128
N_BUCKETS = 32
MAX_DISTANCE = WINDOWS[-1]
N_EXPERTS = 64
TOP_K = 8
N_ROUTE_GROUPS = 8
GROUP_SIZE = N_EXPERTS // N_ROUTE_GROUPS
TOPK_GROUPS = 4
D_EXPERT = 256
ROUTED_SCALE = 2.5
RMS_EPS = 1e-6
NEG = -1e30
LANES = 128
VMEM_LIMIT = 56 * 1024 * 1024

F32 = jnp.float32
BF16 = jnp.bfloat16


def _dot(a, b):
    return jnp.dot(a, b, preferred_element_type=F32)


def _dot_nt(a, b):
    return lax.dot_general(a, b, (((1,), (1,)), ((), ())), preferred_element_type=F32)


def _silu(a):
    return a / (1.0 + jnp.exp(-a))


def _params(sem):
    return pltpu.CompilerParams(dimension_semantics=sem, vmem_limit_bytes=VMEM_LIMIT)


def _ada_kernel(c_ref, w_ref, b_ref, o_ref):
    s = _silu(c_ref[...])
    o_ref[...] = _dot(s.astype(BF16), w_ref[...].astype(BF16)) + b_ref[...]


def _ada(c, w_ada, b_ada):
    n = c.shape[0]
    nc = w_ada.shape[1] // D_MODEL
    return pl.pallas_call(
        _ada_kernel,
        grid=(nc,),
        in_specs=[pl.BlockSpec((n, D_MODEL), lambda j: (0, 0)),
                  pl.BlockSpec((D_MODEL, D_MODEL), lambda j: (0, j)),
                  pl.BlockSpec((1, D_MODEL), lambda j: (0, j))],
        out_specs=pl.BlockSpec((n, D_MODEL), lambda j: (0, j)),
        out_shape=jax.ShapeDtypeStruct((n, w_ada.shape[1]), F32),
        compiler_params=_params(("arbitrary",)),
        name="ada",
    )(c, w_ada, b_ada)


def _proj_kernel(*refs, tile, dils, sample):
    if sample:
        (x_ref, sc_ref, sh_ref, ng_ref, w_ref, qg_ref, kg_ref, cw_ref, bd_ref, ha_ref, hb_ref,
         qf_ref, kf_ref, vf_ref, u_ref, co_ref, u_scr) = refs
    else:
        (x_ref, sc_ref, sh_ref, ng_ref, w_ref, qg_ref, kg_ref, cw_ref, bd_ref,
         q0, k0, v0, q1, k1, v1, q2, k2, v2, kf_ref, vf_ref, u_ref, co_ref, slab, u_scr) = refs
        qkv_out = ((q0, k0, v0), (q1, k1, v1), (q2, k2, v2))

    x = x_ref[0]
    ms = jnp.mean(x * x, axis=-1, keepdims=True)
    h = x * lax.rsqrt(ms + RMS_EPS) * ng_ref[...]
    h = h * (1.0 + sc_ref[0]) + sh_ref[0]
    proj = _dot(h.astype(BF16), w_ref[...])

    bd = bd_ref[...]

    def headnorm(z, g):
        zz = z * z
        hi = zz.astype(BF16)
        lo = (zz - hi.astype(F32)).astype(BF16)
        msq = _dot(hi, bd) + _dot(lo, bd)
        return z * lax.rsqrt(msq + RMS_EPS) * g

    slab_i = 0
    for g in range(N_GROUPS):
        c0 = g * GROUP_W
        qn = headnorm(proj[:, c0:c0 + GROUP_W], qg_ref[...]) * (HEAD_DIM ** -0.5)
        kn = headnorm(proj[:, ATTN_W + c0:ATTN_W + c0 + GROUP_W], kg_ref[...])
        vv = proj[:, 2 * ATTN_W + c0:2 * ATTN_W + c0 + GROUP_W]
        kf_ref[0, :, c0:c0 + GROUP_W] = kn
        vf_ref[0, :, c0:c0 + GROUP_W] = vv
        if sample:
            qf_ref[0, :, c0:c0 + GROUP_W] = qn
            continue
        d = dils[g]
        for val, out in zip((qn, kn, vv), qkv_out[g]):
            if d == 1:
                out[0, 0] = val.astype(BF16)
                continue
            n = tile // d
            for half in range(GROUP_W // LANES):
                slab[slab_i] = val[:, half * LANES:(half + 1) * LANES]
                for r in range(d):
                    out[0, r, :, half * LANES:(half + 1) * LANES] = (
                        slab[slab_i, pl.ds(r, n, stride=d), :].astype(BF16))
                slab_i += 1

    base = 3 * ATTN_W
    u = proj[:, base + 2 * CONV_CH:base + 3 * CONV_CH] * proj[:, base:base + CONV_CH]
    gate_b = proj[:, base + CONV_CH:base + 2 * CONV_CH]
    if sample:
        u_scr[0:8, :] = jnp.zeros((8, CONV_CH), F32)
    else:
        j = pl.program_id(1)

        @pl.when(j == 0)
        def _():
            u_scr[0:8, :] = jnp.zeros((8, CONV_CH), F32)

        @pl.when(j > 0)
        def _():
            u_scr[0:8, :] = u_scr[tile:tile + 8, :]

    u_scr[8:tile + 8, :] = u
    um1 = u_scr[7:tile + 7, :]
    um2 = u_scr[6:tile + 6, :]
    if sample:
        t = lax.broadcasted_iota(jnp.int32, (tile, CONV_CH), 0) % 4
        um1 = jnp.where(t >= 1, um1, 0.0) + hb_ref[...]
        um2 = jnp.where(t >= 2, um2, 0.0) + ha_ref[...]
    cw = cw_ref[...]
    conv = cw[0:1] * um2 + cw[1:2] * um1 + cw[2:3] * u
    u_ref[0] = u
    co_ref[0] = gate_b * conv


def _proj(x, sc, sh, ng, w_bf, qg, kg, cw, bd, hist, *, tile, dils, sample):
    B, S, _ = x.shape
    nt = S // tile
    tm = sc.shape[1]
    mod_spec = pl.BlockSpec((1, tm, D_MODEL), (lambda b, j: (b, j, 0)) if tm > 1 else (lambda b, j: (b, 0, 0)))
    const2 = lambda shape: pl.BlockSpec(shape, lambda b, j: (0, 0))
    in_specs = [pl.BlockSpec((1, tile, D_MODEL), lambda b, j: (b, j, 0)), mod_spec, mod_spec,
                const2((1, D_MODEL)), const2((D_MODEL, N_PROJ)), const2((1, GROUP_W)),
                const2((1, GROUP_W)), const2((3, CONV_CH)), const2((GROUP_W, GROUP_W))]
    args = [x, sc, sh, ng, w_bf, qg, kg, cw, bd]
    nat = lambda w: pl.BlockSpec((1, tile, w), lambda b, j: (b, j, 0))
    out_specs, out_shape = [], []
    scratch = []
    if sample:
        in_specs += [pl.BlockSpec((tile, CONV_CH), lambda b, j: (j, 0))] * 2
        args += list(hist)
        out_specs.append(nat(ATTN_W))
        out_shape.append(jax.ShapeDtypeStruct((B, S, ATTN_W), F32))
    else:
        for d in dils:
            for _ in range(3):
                out_specs.append(pl.BlockSpec((1, d, tile // d, GROUP_W), lambda b, j: (b, 0, j, 0)))
                out_shape.append(jax.ShapeDtypeStruct((B, d, S // d, GROUP_W), BF16))
        n_slabs = sum(3 * (GROUP_W // LANES) for d in dils if d > 1)
        scratch.append(pltpu.VMEM((n_slabs, tile, LANES), F32))
    out_specs += [nat(ATTN_W), nat(ATTN_W), nat(CONV_CH), nat(CONV_CH)]
    out_shape += [jax.ShapeDtypeStruct((B, S, ATTN_W), F32)] * 2
    out_shape += [jax.ShapeDtypeStruct((B, S, CONV_CH), F32)] * 2
    scratch.append(pltpu.VMEM((tile + 8, CONV_CH), F32))
    return pl.pallas_call(
        functools.partial(_proj_kernel, tile=tile, dils=dils, sample=sample),
        grid=(B, nt),
        in_specs=in_specs,
        out_specs=out_specs,
        out_shape=out_shape,
        scratch_shapes=scratch,
        compiler_params=_params(("arbitrary", "arbitrary")),
        name="proj_sample" if sample else "proj_prompt",
    )(*args)


def _attn_kernel(q_ref, kp_ref, kc_ref, vp_ref, vc_ref, b_ref, o_ref, l_ref, *, nq):
    first = pl.program_id(1) == 0
    low = lax.broadcasted_iota(jnp.int32, (BLK, LANES), 1) < HEAD_DIM
    for j in range(nq):
        rows = slice(j * BLK, (j + 1) * BLK)
        prows = slice((j - 1) * BLK, j * BLK)
        for pair in range(GROUP_W // LANES):
            sl = slice(pair * LANES, (pair + 1) * LANES)
            q = q_ref[0, rows, sl]
            if j == 0:
                kp, vp = kp_ref[0, :, sl], vp_ref[0, :, sl]
            else:
                kp, vp = kc_ref[0, prows, sl], vc_ref[0, prows, sl]
            kc, vc = kc_ref[0, rows, sl], vc_ref[0, rows, sl]
            o_sub, l_sub = [], []
            for sub in range(2):
                hh = 2 * pair + sub
                qm = jnp.where(low if sub == 0 else ~low, q, jnp.zeros_like(q))
                sp = _dot_nt(qm, kp) + b_ref[hh, :, 0:BLK]
                if j == 0:
                    sp = jnp.where(first, NEG, sp)
                sc = _dot_nt(qm, kc) + b_ref[hh, :, BLK:2 * BLK]
                m = jnp.max(jnp.maximum(sp, sc), axis=-1, keepdims=True)
                pp = jnp.exp(sp - m)
                pc = jnp.exp(sc - m)
                z = jnp.sum(pp + pc, axis=-1, keepdims=True)
                o = _dot(pp.astype(BF16), vp) + _dot(pc.astype(BF16), vc)
                o_sub.append(o / z)
                l_sub.append(jnp.broadcast_to(m + jnp.log(z), (BLK, LANES)))
            o_ref[0, rows, sl] = jnp.where(low, o_sub[0], o_sub[1])
            l_ref[0, rows, sl] = jnp.where(low, l_sub[0], l_sub[1])


def _attn_prompt(q, k, v, bias):
    Z, L, _ = q.shape
    qt = min(L, 4 * BLK)
    nq = qt // BLK
    cur = pl.BlockSpec((1, qt, GROUP_W), lambda z, i: (z, i, 0))
    prev = pl.BlockSpec((1, BLK, GROUP_W), lambda z, i: (z, jnp.maximum(i * nq - 1, 0), 0))
    return pl.pallas_call(
        functools.partial(_attn_kernel, nq=nq),
        grid=(Z, L // qt),
        in_specs=[cur, prev, cur, prev, cur,
                  pl.BlockSpec((HEADS_PER_GROUP, BLK, 2 * BLK), lambda z, i: (0, 0, 0))],
        out_specs=[cur, cur],
        out_shape=[jax.ShapeDtypeStruct((Z, L, GROUP_W), F32)] * 2,
        compiler_params=_params(("arbitrary", "arbitrary")),
        name="attn_prompt",
    )(q, k, k, v, v, bias)


def _attn_sample_kernel(q_ref, kc_ref, vc_ref, nk_ref, nv_ref, b1_ref, b2_ref,
                        ko_ref, vo_ref, o_ref, l_ref, *, bb, wb):
    lane = lax.broadcasted_iota(jnp.int32, (HEAD_DIM, LANES), 1)
    new_cols = lane >= LANES - 4
    for b in range(bb):
        for hh in range(HEADS_PER_GROUP):
            q = q_ref[b, hh]
            kt = kc_ref[b, hh]
            vt = vc_ref[b, hh]
            nkt = nk_ref[b, hh]
            nvt = nv_ref[b, hh]
            s1 = _dot(q, kt.astype(BF16)) + b1_ref[hh]
            s2 = _dot(q, nkt.astype(BF16)) + b2_ref[hh]
            m = jnp.maximum(jnp.max(s1, axis=-1, keepdims=True), jnp.max(s2, axis=-1, keepdims=True))
            p1 = jnp.exp(s1 - m)
            p2 = jnp.exp(s2 - m)
            z = jnp.sum(p1, axis=-1, keepdims=True) + jnp.sum(p2, axis=-1, keepdims=True)
            o = _dot_nt(p1.astype(BF16), vt.astype(BF16)) + _dot_nt(p2.astype(BF16), nvt.astype(BF16))
            o_ref[b, hh] = o / z
            l_ref[b, hh] = jnp.broadcast_to(m + jnp.log(z), (8, LANES))
            for src, new, dst in ((kt, nkt, ko_ref), (vt, nvt, vo_ref)):
                rolled = pltpu.roll(src, wb - 4, axis=1)
                if wb > LANES:
                    dst[b, hh, :, 0:wb - LANES] = rolled[:, 0:wb - LANES]
                dst[b, hh, :, wb - LANES:wb] = jnp.where(new_cols, new, rolled[:, wb - LANES:wb])


def _attn_sample(q, kc, vc, nk, nv, b1, b2, *, bb):
    nb, _, _, wb = kc.shape
    blk = lambda *tail: pl.BlockSpec((bb, HEADS_PER_GROUP) + tail, lambda i: (i, 0, 0, 0))
    const = lambda shape: pl.BlockSpec(shape, lambda i: (0, 0, 0))
    return pl.pallas_call(
        functools.partial(_attn_sample_kernel, bb=bb, wb=wb),
        grid=(nb // bb,),
        in_specs=[blk(8, HEAD_DIM), blk(HEAD_DIM, wb), blk(HEAD_DIM, wb), blk(HEAD_DIM, LANES),
                  blk(HEAD_DIM, LANES), const((HEADS_PER_GROUP, 8, wb)), const((HEADS_PER_GROUP, 8, LANES))],
        out_specs=[blk(HEAD_DIM, wb), blk(HEAD_DIM, wb), blk(8, HEAD_DIM), blk(8, LANES)],
        out_shape=[jax.ShapeDtypeStruct(kc.shape, F32), jax.ShapeDtypeStruct(kc.shape, F32),
                   jax.ShapeDtypeStruct((nb, HEADS_PER_GROUP, 8, HEAD_DIM), F32),
                   jax.ShapeDtypeStruct((nb, HEADS_PER_GROUP, 8, LANES), F32)],
        compiler_params=_params(("arbitrary",)),
        name="attn_sample",
    )(q, kc, vc, nk, nv, b1, b2)


def _first_max(v, ids, sentinel):
    m = jnp.max(v, axis=0, keepdims=True)
    idx = jnp.min(jnp.where(v == m, ids, sentinel), axis=0, keepdims=True)
    return m, ids == idx


def _oproj_kernel(x_ref, o0, o1, o2, l0, l1, l2, co_ref, g1_ref, sc_ref, sh_ref, ng_ref, wo_ref,
                  wr_ref, br_ref, x1_ref, h2_ref, gt_ref, slab, *, tile, dils):
    o_refs, l_refs = (o0, o1, o2), (l0, l1, l2)
    outs, lses = [], []
    slab_i = 0
    for g, d in enumerate(dils):
        if d == 1:
            outs.append(o_refs[g][0, 0])
            lses.append(l_refs[g][0, 0])
            continue
        n = tile // d
        for ref, dest in ((o_refs[g], outs), (l_refs[g], lses)):
            halves = []
            for half in range(GROUP_W // LANES):
                for r in range(d):
                    slab[slab_i, pl.ds(r, n, stride=d), :] = ref[0, r, :, half * LANES:(half + 1) * LANES]
                halves.append(slab[slab_i])
                slab_i += 1
            dest.append(jnp.concatenate(halves, axis=1))

    m = jnp.maximum(jnp.maximum(lses[0], lses[1]), lses[2])
    es = [jnp.exp(l - m) for l in lses]
    den = es[0] + es[1] + es[2]
    cat = [(outs[g] * (es[g] / den)).astype(BF16) for g in range(N_GROUPS)]
    cat.append(co_ref[0].astype(BF16))
    cat = jnp.concatenate(cat, axis=1)
    x1 = x_ref[0] + g1_ref[0] * _dot(cat, wo_ref[...])
    x1_ref[0] = x1

    ms = jnp.mean(x1 * x1, axis=-1, keepdims=True)
    h2 = x1 * lax.rsqrt(ms + RMS_EPS) * ng_ref[...]
    h2 = h2 * (1.0 + sc_ref[0]) + sh_ref[0]
    hh = h2.astype(BF16)
    h2_ref[0] = hh

    hl = (h2 - hh.astype(F32)).astype(BF16)
    wr = wr_ref[...]
    wh = wr.astype(BF16)
    wl = (wr - wh.astype(F32)).astype(BF16)
    logits = _dot_nt(wh, hh) + _dot_nt(wh, hl) + _dot_nt(wl, hh)
    scores = 1.0 / (1.0 + jnp.exp(-logits))
    sel = scores + br_ref[...]

    ids = lax.broadcasted_iota(jnp.int32, (GROUP_SIZE, tile), 0)
    ninf = -jnp.inf
    sel_g = [sel[g * GROUP_SIZE:(g + 1) * GROUP_SIZE] for g in range(N_ROUTE_GROUPS)]
    gscore = jnp.zeros((N_ROUTE_GROUPS, tile), F32)
    for g in range(N_ROUTE_GROUPS):
        m1, oh = _first_max(sel_g[g], ids, GROUP_SIZE)
        m2 = jnp.max(jnp.where(oh, ninf, sel_g[g]), axis=0, keepdims=True)
        gscore = jnp.where(ids == g, m1 + m2, gscore)
    gsel = jnp.zeros((N_ROUTE_GROUPS, tile), F32)
    for _ in range(TOPK_GROUPS):
        _, oh = _first_max(gscore, ids, N_ROUTE_GROUPS)
        gsel = jnp.where(oh, 1.0, gsel)
        gscore = jnp.where(oh, ninf, gscore)
    cand = [jnp.where(gsel[g:g + 1] > 0.0, sel_g[g], ninf) for g in range(N_ROUTE_GROUPS)]
    chosen = [jnp.zeros((GROUP_SIZE, tile), F32) for _ in range(N_ROUTE_GROUPS)]
    for _ in range(TOP_K):
        mx = cand[0]
        for g in range(1, N_ROUTE_GROUPS):
            mx = jnp.maximum(mx, cand[g])
        mx = jnp.max(mx, axis=0, keepdims=True)
        idx = jnp.where(cand[0] == mx, ids, N_EXPERTS)
        for g in range(1, N_ROUTE_GROUPS):
            idx = jnp.minimum(idx, jnp.where(cand[g] == mx, ids + g * GROUP_SIZE, N_EXPERTS))
        idx = jnp.min(idx, axis=0, keepdims=True)
        for g in range(N_ROUTE_GROUPS):
            oh = (ids + g * GROUP_SIZE) == idx
            chosen[g] = jnp.where(oh, 1.0, chosen[g])
            cand[g] = jnp.where(oh, ninf, cand[g])
    wts = [jnp.where(chosen[g] > 0.0, scores[g * GROUP_SIZE:(g + 1) * GROUP_SIZE], 0.0)
           for g in range(N_ROUTE_GROUPS)]
    wsum = jnp.sum(wts[0], axis=0, keepdims=True)
    for g in range(1, N_ROUTE_GROUPS):
        wsum = wsum + jnp.sum(wts[g], axis=0, keepdims=True)
    gates_t = jnp.concatenate([w / wsum * ROUTED_SCALE for w in wts]
                              + [jnp.zeros((LANES - N_EXPERTS, tile), F32)], axis=0)
    gt_ref[0] = gates_t.T


def _oproj(x, o_list, l_list, co, g1, sc, sh, ng, wo_bf, wr_t, br, *, tile, dils):
    B, S, _ = x.shape
    tm = g1.shape[1]
    mod_spec = pl.BlockSpec((1, tm, D_MODEL), (lambda b, j: (b, j, 0)) if tm > 1 else (lambda b, j: (b, 0, 0)))
    const2 = lambda shape: pl.BlockSpec(shape, lambda b, j: (0, 0))
    nat = lambda w: pl.BlockSpec((1, tile, w), lambda b, j: (b, j, 0))
    dspec = [pl.BlockSpec((1, d, tile // d, GROUP_W), lambda b, j: (b, 0, j, 0)) for d in dils]
    n_slabs = sum(2 * (GROUP_W // LANES) for d in dils if d > 1)
    return pl.pallas_call(
        functools.partial(_oproj_kernel, tile=tile, dils=dils),
        grid=(B, S // tile),
        in_specs=[nat(D_MODEL)] + dspec + dspec + [nat(CONV_CH), mod_spec, mod_spec, mod_spec,
                  const2((1, D_MODEL)), const2((D_MODEL, D_MODEL)), const2((N_EXPERTS, D_MODEL)),
                  const2((N_EXPERTS, 1))],
        out_specs=[nat(D_MODEL), nat(D_MODEL), nat(LANES)],
        out_shape=[jax.ShapeDtypeStruct((B, S, D_MODEL), F32), jax.ShapeDtypeStruct((B, S, D_MODEL), BF16),
                   jax.ShapeDtypeStruct((B, S, LANES), F32)],
        scratch_shapes=[pltpu.VMEM((max(n_slabs, 1), tile, LANES), F32)],
        compiler_params=_params(("arbitrary", "arbitrary")),
        name="oproj",
    )(x, *o_list, *l_list, co, g1, sc, sh, ng, wo_bf, wr_t, br)


def _moe_kernel(h_ref, gt_ref, x1_ref, g2_ref, wg_ref, wu_ref, wd_ref, wgs_ref, wus_ref, wds_ref,
                y_ref, acc, *, tile):
    e = pl.program_id(2)
    h = h_ref[0]

    @pl.when(e == 0)
    def _():
        a = _dot(h, wgs_ref[...].astype(BF16))
        b = _dot(h, wus_ref[...].astype(BF16))
        acc[...] = _dot((_silu(a) * b).astype(BF16), wds_ref[...].astype(BF16))

    a = _dot(h, wg_ref[0].astype(BF16))
    b = _dot(h, wu_ref[0].astype(BF16))
    lane = lax.broadcasted_iota(jnp.int32, (tile, LANES), 1)
    gcol = jnp.sum(jnp.where(lane == e, gt_ref[0], 0.0), axis=1, keepdims=True)
    hc = _silu(a) * b * gcol
    acc[...] += _dot(hc.astype(BF16), wd_ref[0].astype(BF16))

    @pl.when(e == N_EXPERTS - 1)
    def _():
        y_ref[0] = x1_ref[0] + g2_ref[0] * acc[...]


def _moe(h2, gates, x1, g2, wg, wu, wd, wgs, wus, wds, *, tile):
    B, S, _ = h2.shape
    tm = g2.shape[1]
    mod_spec = pl.BlockSpec((1, tm, D_MODEL),
                            (lambda b, j, e: (b, j, 0)) if tm > 1 else (lambda b, j, e: (b, 0, 0)))
    nat = lambda w: pl.BlockSpec((1, tile, w), lambda b, j, e: (b, j, 0))
    const2 = lambda shape: pl.BlockSpec(shape, lambda b, j, e: (0, 0))
    return pl.pallas_call(
        functools.partial(_moe_kernel, tile=tile),
        grid=(B, S // tile, N_EXPERTS),
        in_specs=[nat(D_MODEL), nat(LANES), nat(D_MODEL), mod_spec,
                  pl.BlockSpec((1, D_MODEL, D_EXPERT), lambda b, j, e: (e, 0, 0)),
                  pl.BlockSpec((1, D_MODEL, D_EXPERT), lambda b, j, e: (e, 0, 0)),
                  pl.BlockSpec((1, D_EXPERT, D_MODEL), lambda b, j, e: (e, 0, 0)),
                  const2((D_MODEL, D_EXPERT)), const2((D_MODEL, D_EXPERT)), const2((D_EXPERT, D_MODEL))],
        out_specs=nat(D_MODEL),
        out_shape=jax.ShapeDtypeStruct((B, S, D_MODEL), F32),
        scratch_shapes=[pltpu.VMEM((tile, D_MODEL), F32)],
        compiler_params=_params(("arbitrary", "arbitrary", "arbitrary")),
        name="moe",
    )(h2, gates, x1, g2, wg, wu, wd, wgs, wus, wds)


def _t5_bucket(dist):
    max_exact = N_BUCKETS // 2
    df = jnp.maximum(dist, 1).astype(F32)
    large = max_exact + (jnp.log(df / max_exact) / math.log(MAX_DISTANCE / max_exact)
                         * (N_BUCKETS - max_exact)).astype(jnp.int32)
    large = jnp.minimum(large, N_BUCKETS - 1)
    return jnp.where(dist < max_exact, dist, large)


def _step_bias(rel_bias, g):
    dist = jnp.arange(N_KEYS, dtype=jnp.int32) * DILATIONS[g]
    cols = rel_bias[:, g * HEADS_PER_GROUP:(g + 1) * HEADS_PER_GROUP]
    return cols[_t5_bucket(dist)].T.astype(F32)


def _prompt_bias(bias_k):
    h = bias_k.shape[0]
    n = 3 * BLK
    row = jnp.concatenate([bias_k[:, ::-1], jnp.full((h, n - N_KEYS), NEG, F32)], axis=1)
    t = jnp.tile(row, (1, BLK))[:, :BLK * (n - 1)].reshape(h, BLK, n - 1)
    return t[:, :, :2 * BLK]


def _sample_bias(bias_k, wb, d):
    h = bias_k.shape[0]
    rev = bias_k[:, :0:-1]
    rows = []
    for t in range(4):
        if d == 1:
            rows.append(jnp.concatenate([jnp.full((h, t), NEG, F32), rev[:, :wb - t]], axis=1))
        else:
            r = jnp.arange(d)[None, None, :]
            rows.append(jnp.where(r == t, rev[:, :, None], NEG).reshape(h, wb))
    b1 = jnp.concatenate([jnp.stack(rows, axis=1), jnp.zeros((h, 4, wb), F32)], axis=1)
    rows = []
    for t in range(4):
        cols = [bias_k[:, (t - tn) // d] if (t >= tn and (t - tn) % d == 0) else jnp.full((h,), NEG, F32)
                for tn in range(4)]
        rows.append(jnp.concatenate([jnp.full((h, LANES - 4), NEG, F32), jnp.stack(cols, axis=1)], axis=1))
    b2 = jnp.concatenate([jnp.stack(rows, axis=1), jnp.zeros((h, 4, LANES), F32)], axis=1)
    return b1, b2


def _layer(x, mods, hist, caches, rel_bias, weights, *, sample):
    (norm1_g, norm2_g, w_in_bf, qg, kg, conv_w, bd, w_o_bf, w_r_t, b_r, w_gate_e, w_up_e, w_down_e,
     w_gate_s, w_up_s, w_down_s) = weights
    sh1, sc1, g1, sh2, sc2, g2 = mods
    B, S, _ = x.shape
    tile = 512
    dils = (1, 1, 1) if sample else DILATIONS
    res = _proj(x, sc1, sh1, norm1_g, w_in_bf, qg, kg, conv_w, bd, hist, tile=tile, dils=dils, sample=sample)
    states = []
    o_list, l_list = [], []
    if sample:
        qf, kf, vf, u, co = res
        nb = S // 4

        def heads_t(a, g):
            a = a[0, :, g * GROUP_W:(g + 1) * GROUP_W].reshape(nb, 4, HEADS_PER_GROUP, HEAD_DIM)
            return a.transpose(0, 2, 1, 3)

        for g in range(N_GROUPS):
            d, wb = DILATIONS[g], WINDOWS[g]
            kc, vc = caches[g]
            bias_k = _step_bias(rel_bias, g)
            b1, b2 = _sample_bias(bias_k, wb, d)
            q = jnp.pad(heads_t(qf, g), ((0, 0), (0, 0), (0, 4), (0, 0))).astype(BF16)
            new_t = lambda a: jnp.pad(heads_t(a, g).transpose(0, 1, 3, 2),
                                      ((0, 0), (0, 0), (0, 0), (LANES - 4, 0)))
            ko, vo, o, lse = _attn_sample(q, kc, vc, new_t(kf), new_t(vf), b1, b2,
                                          bb={128: 8, 512: 4, 2048: 1}[wb])
            states += [ko.transpose(0, 3, 1, 2)[None], vo.transpose(0, 3, 1, 2)[None]]
            o = o[:, :, :4].transpose(0, 2, 1, 3).reshape(1, 1, S, GROUP_W)
            lse = jnp.broadcast_to(lse[:, :, :4, :1], (nb, HEADS_PER_GROUP, 4, HEAD_DIM))
            o_list.append(o)
            l_list.append(lse.transpose(0, 2, 1, 3).reshape(1, 1, S, GROUP_W))
        states.append(u.reshape(nb, 4, CONV_CH)[:, 2:][None])
    else:
        qkv, (kf, vf, u, co) = res[:9], res[9:]
        for g in range(N_GROUPS):
            d, w = DILATIONS[g], WINDOWS[g]
            q, k, v = (a.reshape(B * d, S // d, GROUP_W) for a in qkv[3 * g:3 * g + 3])
            o, lse = _attn_prompt(q, k, v, _prompt_bias(_step_bias(rel_bias, g)))
            o_list.append(o.reshape(B, d, S // d, GROUP_W))
            l_list.append(lse.reshape(B, d, S // d, GROUP_W))
            keep = lambda a: a[:, S - w:, g * GROUP_W:(g + 1) * GROUP_W].reshape(
                B, w, HEADS_PER_GROUP, HEAD_DIM)[None]
            states += [keep(kf), keep(vf)]
        states.append(u[:, S - 2:][None])
    x1, h2, gates = _oproj(x, o_list, l_list, co, g1, sc2, sh2, norm2_g, w_o_bf, w_r_t, b_r,
                           tile=tile, dils=dils)
    y = _moe(h2, gates, x1, g2, w_gate_e, w_up_e, w_down_e, w_gate_s, w_up_s, w_down_s,
             tile=512 if sample else 1024)
    return y, states


def kernel(x_prompt, x_sample, c_prompt, c_sample, cache_k_w128, cache_v_w128, cache_k_w512, cache_v_w512, cache_k_w2048, cache_v_w2048, state_conv, rel_bias, norm1_g, norm2_g, w_ada, b_ada, w_in, q_norm_g, k_norm_g, conv_w, w_o, w_router, b_router, w_gate_e, w_up_e, w_down_e, w_gate_s, w_up_s, w_down_s):
    B = x_prompt.shape[0]
    DB, T = x_sample.shape[:2]
    n_c = B + DB
    c_all = jnp.pad(jnp.concatenate([c_prompt, c_sample], axis=0), ((0, (-n_c) % 8), (0, 0)))
    mod = _ada(c_all, w_ada[0], b_ada)
    chunks = [mod[:, i * D_MODEL:(i + 1) * D_MODEL] for i in range(6)]
    mods_p = [c[:B].reshape(B, 1, D_MODEL) for c in chunks]
    mods_s = [jnp.repeat(c[B:n_c], T, axis=0).reshape(1, DB * T, D_MODEL) for c in chunks]

    eye = jnp.arange(GROUP_W) // HEAD_DIM
    bd = jnp.where(eye[:, None] == eye[None, :], 1.0 / HEAD_DIM, 0.0).astype(BF16)
    weights = (norm1_g, norm2_g, w_in[0].astype(BF16),
               jnp.tile(q_norm_g, (1, HEADS_PER_GROUP)), jnp.tile(k_norm_g, (1, HEADS_PER_GROUP)),
               conv_w[0], bd, w_o[0].astype(BF16), w_router[0].T, b_router.reshape(N_EXPERTS, 1),
               w_gate_e[0], w_up_e[0], w_down_e[0], w_gate_s[0], w_up_s[0], w_down_s[0])

    yp, st_p = _layer(x_prompt, mods_p, None, None, rel_bias, weights, sample=False)

    s0, s1 = state_conv[0, :, 0], state_conv[0, :, 1]
    zero = jnp.zeros_like(s0)
    hist_a = jnp.stack([s0, s1, zero, zero], axis=1).reshape(DB * T, CONV_CH)
    hist_b = jnp.stack([s1, zero, zero, zero], axis=1).reshape(DB * T, CONV_CH)
    caches = [(ck[0].transpose(0, 2, 3, 1), cv[0].transpose(0, 2, 3, 1))
              for ck, cv in ((cache_k_w128, cache_v_w128), (cache_k_w512, cache_v_w512),
                             (cache_k_w2048, cache_v_w2048))]
    ys, st_s = _layer(x_sample.reshape(1, DB * T, D_MODEL), mods_s, (hist_a, hist_b), caches, rel_bias,
                      weights, sample=True)
    return (yp, ys.reshape(DB, T, D_MODEL), *st_p, *st_s)
```

```python
import functools
import math

import jax
import jax.numpy as jnp
from jax import lax
from jax.experimental import pallas as pl
from jax.experimental.pallas import tpu as pltpu

D_MODEL = 1024
HEAD_DIM = 64
HEADS_PER_GROUP = 4
GROUP_W = HEADS_PER_GROUP * HEAD_DIM
WINDOWS = (128, 512, 2048)
DILATIONS = (1, 4, 16)
N_GROUPS = 3
ATTN_W = N_GROUPS * GROUP_W
CONV_CH = 256
N_PROJ = 3 * ATTN_W + 3 * CONV_CH
N_STEPS = 128
N_KEYS = N_STEPS + 1
BLK = 128
N_BUCKETS = 32
MAX_DISTANCE = WINDOWS[-1]
N_EXPERTS = 64
TOP_K = 8
N_ROUTE_GROUPS = 8
GROUP_SIZE = N_EXPERTS // N_ROUTE_GROUPS
TOPK_GROUPS = 4
D_EXPERT = 256
ROUTED_SCALE = 2.5
RMS_EPS = 1e-6
NEG = -1e30
LANES = 128
SUB = 256
CHUNK = 16
XG_W = D_MODEL + LANES
VMEM_LIMIT = 56 * 1024 * 1024

F32 = jnp.float32
BF16 = jnp.bfloat16


def _dot(a, b):
    return jnp.dot(a, b, preferred_element_type=F32)


def _dot_nt(a, b):
    return lax.dot_general(a, b, (((1,), (1,)), ((), ())), preferred_element_type=F32)


def _silu(a):
    return a / (1.0 + jnp.exp(-a))


def _params(sem):
    return pltpu.CompilerParams(dimension_semantics=sem, vmem_limit_bytes=VMEM_LIMIT)


def _ada_kernel(c_ref, w_ref, b_ref, o_ref):
    s = _silu(c_ref[...])
    o_ref[...] = _dot(s.astype(BF16), w_ref[...].astype(BF16)) + b_ref[...]


def _ada(c, w_ada, b_ada):
    n = c.shape[0]
    nc = w_ada.shape[1] // D_MODEL
    return pl.pallas_call(
        _ada_kernel,
        grid=(nc,),
        in_specs=[pl.BlockSpec((n, D_MODEL), lambda j: (0, 0)),
                  pl.BlockSpec((D_MODEL, D_MODEL), lambda j: (0, j)),
                  pl.BlockSpec((1, D_MODEL), lambda j: (0, j))],
        out_specs=pl.BlockSpec((n, D_MODEL), lambda j: (0, j)),
        out_shape=jax.ShapeDtypeStruct((n, w_ada.shape[1]), F32),
        compiler_params=_params(("arbitrary",)),
        name="ada",
    )(c, w_ada, b_ada)


def _proj_kernel(*refs, tile, dils, sample):
    if sample:
        (x_ref, sc_ref, sh_ref, ng_ref, w_ref, qg_ref, kg_ref, cw_ref, bd_ref, ha_ref, hb_ref,
         qf_ref, kf_ref, vf_ref, u_ref, co_ref, u_scr) = refs
    else:
        (x_ref, sc_ref, sh_ref, ng_ref, w_ref, qg_ref, kg_ref, cw_ref, bd_ref,
         q0, k0, v0, q1, k1, v1, q2, k2, v2, kf_ref, vf_ref, u_ref, co_ref, slab, u_scr) = refs
        qkv_out = ((q0, k0, v0), (q1, k1, v1), (q2, k2, v2))

    x = x_ref[0]
    ms = jnp.mean(x * x, axis=-1, keepdims=True)
    h = x * lax.rsqrt(ms + RMS_EPS) * ng_ref[...]
    h = h * (1.0 + sc_ref[0]) + sh_ref[0]
    proj = _dot(h.astype(BF16), w_ref[...])

    bd = bd_ref[...]

    def headnorm(z, g):
        zz = z * z
        hi = zz.astype(BF16)
        lo = (zz - hi.astype(F32)).astype(BF16)
        msq = _dot(hi, bd) + _dot(lo, bd)
        return z * lax.rsqrt(msq + RMS_EPS) * g

    slab_i = 0
    for g in range(N_GROUPS):
        c0 = g * GROUP_W
        qn = headnorm(proj[:, c0:c0 + GROUP_W], qg_ref[...]) * (HEAD_DIM ** -0.5)
        kn = headnorm(proj[:, ATTN_W + c0:ATTN_W + c0 + GROUP_W], kg_ref[...])
        vv = proj[:, 2 * ATTN_W + c0:2 * ATTN_W + c0 + GROUP_W]
        kf_ref[0, :, c0:c0 + GROUP_W] = kn
        vf_ref[0, :, c0:c0 + GROUP_W] = vv
        if sample:
            qf_ref[0, :, c0:c0 + GROUP_W] = qn
            continue
        d = dils[g]
        for val, out in zip((qn, kn, vv), qkv_out[g]):
            if d == 1:
                out[0, 0] = val.astype(BF16)
                continue
            n = tile // d
            for half in range(GROUP_W // LANES):
                slab[slab_i] = val[:, half * LANES:(half + 1) * LANES]
                for r in range(d):
                    out[0, r, :, half * LANES:(half + 1) * LANES] = (
                        slab[slab_i, pl.ds(r, n, stride=d), :].astype(BF16))
                slab_i += 1

    base = 3 * ATTN_W
    u = proj[:, base + 2 * CONV_CH:base + 3 * CONV_CH] * proj[:, base:base + CONV_CH]
    gate_b = proj[:, base + CONV_CH:base + 2 * CONV_CH]
    if sample:
        u_scr[0:8, :] = jnp.zeros((8, CONV_CH), F32)
    else:
        j = pl.program_id(1)

        @pl.when(j == 0)
        def _():
            u_scr[0:8, :] = jnp.zeros((8, CONV_CH), F32)

        @pl.when(j > 0)
        def _():
            u_scr[0:8, :] = u_scr[tile:tile + 8, :]

    u_scr[8:tile + 8, :] = u
    um1 = u_scr[7:tile + 7, :]
    um2 = u_scr[6:tile + 6, :]
    if sample:
        t = lax.broadcasted_iota(jnp.int32, (tile, CONV_CH), 0) % 4
        um1 = jnp.where(t >= 1, um1, 0.0) + hb_ref[...]
        um2 = jnp.where(t >= 2, um2, 0.0) + ha_ref[...]
    cw = cw_ref[...]
    conv = cw[0:1] * um2 + cw[1:2] * um1 + cw[2:3] * u
    u_ref[0] = u
    co_ref[0] = gate_b * conv


def _proj(x, sc, sh, ng, w_bf, qg, kg, cw, bd, hist, *, tile, dils, sample):
    B, S, _ = x.shape
    nt = S // tile
    tm = sc.shape[1]
    mod_spec = pl.BlockSpec((1, tm, D_MODEL), (lambda b, j: (b, j, 0)) if tm > 1 else (lambda b, j: (b, 0, 0)))
    const2 = lambda shape: pl.BlockSpec(shape, lambda b, j: (0, 0))
    in_specs = [pl.BlockSpec((1, tile, D_MODEL), lambda b, j: (b, j, 0)), mod_spec, mod_spec,
                const2((1, D_MODEL)), const2((D_MODEL, N_PROJ)), const2((1, GROUP_W)),
                const2((1, GROUP_W)), const2((3, CONV_CH)), const2((GROUP_W, GROUP_W))]
    args = [x, sc, sh, ng, w_bf, qg, kg, cw, bd]
    nat = lambda w: pl.BlockSpec((1, tile, w), lambda b, j: (b, j, 0))
    out_specs, out_shape = [], []
    scratch = []
    if sample:
        in_specs += [pl.BlockSpec((tile, CONV_CH), lambda b, j: (j, 0))] * 2
        args += list(hist)
        out_specs.append(nat(ATTN_W))
        out_shape.append(jax.ShapeDtypeStruct((B, S, ATTN_W), F32))
    else:
        for d in dils:
            for _ in range(3):
                out_specs.append(pl.BlockSpec((1, d, tile // d, GROUP_W), lambda b, j: (b, 0, j, 0)))
                out_shape.append(jax.ShapeDtypeStruct((B, d, S // d, GROUP_W), BF16))
        n_slabs = sum(3 * (GROUP_W // LANES) for d in dils if d > 1)
        scratch.append(pltpu.VMEM((n_slabs, tile, LANES), F32))
    out_specs += [nat(ATTN_W), nat(ATTN_W), nat(CONV_CH), nat(CONV_CH)]
    out_shape += [jax.ShapeDtypeStruct((B, S, ATTN_W), F32)] * 2
    out_shape += [jax.ShapeDtypeStruct((B, S, CONV_CH), F32)] * 2
    scratch.append(pltpu.VMEM((tile + 8, CONV_CH), F32))
    return pl.pallas_call(
        functools.partial(_proj_kernel, tile=tile, dils=dils, sample=sample),
        grid=(B, nt),
        in_specs=in_specs,
        out_specs=out_specs,
        out_shape=out_shape,
        scratch_shapes=scratch,
        compiler_params=_params(("arbitrary", "arbitrary")),
        name="proj_sample" if sample else "proj_prompt",
    )(*args)


def _attn_kernel(q_ref, kp_ref, kc_ref, vp_ref, vc_ref, b_ref, o_ref, l_ref, *, nq):
    first = pl.program_id(1) == 0
    low = lax.broadcasted_iota(jnp.int32, (BLK, LANES), 1) < HEAD_DIM
    for j in range(nq):
        rows = slice(j * BLK, (j + 1) * BLK)
        prows = slice((j - 1) * BLK, j * BLK)
        for pair in range(GROUP_W // LANES):
            sl = slice(pair * LANES, (pair + 1) * LANES)
            q = q_ref[0, rows, sl]
            if j == 0:
                kp, vp = kp_ref[0, :, sl], vp_ref[0, :, sl]
            else:
                kp, vp = kc_ref[0, prows, sl], vc_ref[0, prows, sl]
            kc, vc = kc_ref[0, rows, sl], vc_ref[0, rows, sl]
            o_sub, l_sub = [], []
            for sub in range(2):
                hh = 2 * pair + sub
                qm = jnp.where(low if sub == 0 else ~low, q, jnp.zeros_like(q))
                sp = _dot_nt(qm, kp) + b_ref[hh, :, 0:BLK]
                if j == 0:
                    sp = jnp.where(first, NEG, sp)
                sc = _dot_nt(qm, kc) + b_ref[hh, :, BLK:2 * BLK]
                m = jnp.max(jnp.maximum(sp, sc), axis=-1, keepdims=True)
                pp = jnp.exp(sp - m)
                pc = jnp.exp(sc - m)
                z = jnp.sum(pp + pc, axis=-1, keepdims=True)
                o = _dot(pp.astype(BF16), vp) + _dot(pc.astype(BF16), vc)
                o_sub.append(o / z)
                l_sub.append(jnp.broadcast_to(m + jnp.log(z), (BLK, LANES)))
            o_ref[0, rows, sl] = jnp.where(low, o_sub[0], o_sub[1])
            l_ref[0, rows, sl] = jnp.where(low, l_sub[0], l_sub[1])


def _attn_prompt(q, k, v, bias):
    Z, L, _ = q.shape
    qt = min(L, 4 * BLK)
    nq = qt // BLK
    cur = pl.BlockSpec((1, qt, GROUP_W), lambda z, i: (z, i, 0))
    prev = pl.BlockSpec((1, BLK, GROUP_W), lambda z, i: (z, jnp.maximum(i * nq - 1, 0), 0))
    return pl.pallas_call(
        functools.partial(_attn_kernel, nq=nq),
        grid=(Z, L // qt),
        in_specs=[cur, prev, cur, prev, cur,
                  pl.BlockSpec((HEADS_PER_GROUP, BLK, 2 * BLK), lambda z, i: (0, 0, 0))],
        out_specs=[cur, cur],
        out_shape=[jax.ShapeDtypeStruct((Z, L, GROUP_W), F32)] * 2,
        compiler_params=_params(("arbitrary", "arbitrary")),
        name="attn_prompt",
    )(q, k, k, v, v, bias)


def _attn_sample_kernel(q_ref, kc_ref, vc_ref, nk_ref, nv_ref, b1_ref, b2_ref,
                        ko_ref, vo_ref, o_ref, l_ref, *, bb, wb):
    lane = lax.broadcasted_iota(jnp.int32, (HEAD_DIM, LANES), 1)
    new_cols = lane >= LANES - 4
    for b in range(bb):
        for hh in range(HEADS_PER_GROUP):
            q = q_ref[b, hh]
            kt = kc_ref[b, hh]
            vt = vc_ref[b, hh]
            nkt = nk_ref[b, hh]
            nvt = nv_ref[b, hh]
            s1 = _dot(q, kt.astype(BF16)) + b1_ref[hh]
            s2 = _dot(q, nkt.astype(BF16)) + b2_ref[hh]
            m = jnp.maximum(jnp.max(s1, axis=-1, keepdims=True), jnp.max(s2, axis=-1, keepdims=True))
            p1 = jnp.exp(s1 - m)
            p2 = jnp.exp(s2 - m)
            z = jnp.sum(p1, axis=-1, keepdims=True) + jnp.sum(p2, axis=-1, keepdims=True)
            o = _dot_nt(p1.astype(BF16), vt.astype(BF16)) + _dot_nt(p2.astype(BF16), nvt.astype(BF16))
            o_ref[b, hh] = o / z
            l_ref[b, hh] = jnp.broadcast_to(m + jnp.log(z), (8, LANES))
            for src, new, dst in ((kt, nkt, ko_ref), (vt, nvt, vo_ref)):
                rolled = pltpu.roll(src, wb - 4, axis=1)
                if wb > LANES:
                    dst[b, hh, :, 0:wb - LANES] = rolled[:, 0:wb - LANES]
                dst[b, hh, :, wb - LANES:wb] = jnp.where(new_cols, new, rolled[:, wb - LANES:wb])


def _attn_sample(q, kc, vc, nk, nv, b1, b2, *, bb):
    nb, _, _, wb = kc.shape
    blk = lambda *tail: pl.BlockSpec((bb, HEADS_PER_GROUP) + tail, lambda i: (i, 0, 0, 0))
    const = lambda shape: pl.BlockSpec(shape, lambda i: (0, 0, 0))
    return pl.pallas_call(
        functools.partial(_attn_sample_kernel, bb=bb, wb=wb),
        grid=(nb // bb,),
        in_specs=[blk(8, HEAD_DIM), blk(HEAD_DIM, wb), blk(HEAD_DIM, wb), blk(HEAD_DIM, LANES),
                  blk(HEAD_DIM, LANES), const((HEADS_PER_GROUP, 8, wb)), const((HEADS_PER_GROUP, 8, LANES))],
        out_specs=[blk(HEAD_DIM, wb), blk(HEAD_DIM, wb), blk(8, HEAD_DIM), blk(8, LANES)],
        out_shape=[jax.ShapeDtypeStruct(kc.shape, F32), jax.ShapeDtypeStruct(kc.shape, F32),
                   jax.ShapeDtypeStruct((nb, HEADS_PER_GROUP, 8, HEAD_DIM), F32),
                   jax.ShapeDtypeStruct((nb, HEADS_PER_GROUP, 8, LANES), F32)],
        compiler_params=_params(("arbitrary",)),
        name="attn_sample",
    )(q, kc, vc, nk, nv, b1, b2)


def _first_max(v, ids, sentinel):
    m = jnp.max(v, axis=0, keepdims=True)
    idx = jnp.min(jnp.where(v == m, ids, sentinel), axis=0, keepdims=True)
    return m, ids == idx


def _oproj_kernel(x_ref, o0, o1, o2, l0, l1, l2, co_ref, g1_ref, sc_ref, sh_ref, ng_ref, wo_ref,
                  wr_ref, br_ref, x1_ref, xg_ref, gt_ref, posr_ref, post_ref, cnt_ref, slab, *, tile, dils):
    o_refs, l_refs = (o0, o1, o2), (l0, l1, l2)
    outs, lses = [], []
    slab_i = 0
    for g, d in enumerate(dils):
        if d == 1:
            outs.append(o_refs[g][0, 0])
            lses.append(l_refs[g][0, 0])
            continue
        n = tile // d
        for ref, dest in ((o_refs[g], outs), (l_refs[g], lses)):
            halves = []
            for half in range(GROUP_W // LANES):
                for r in range(d):
                    slab[slab_i, pl.ds(r, n, stride=d), :] = ref[0, r, :, half * LANES:(half + 1) * LANES]
                halves.append(slab[slab_i])
                slab_i += 1
            dest.append(jnp.concatenate(halves, axis=1))

    m = jnp.maximum(jnp.maximum(lses[0], lses[1]), lses[2])
    es = [jnp.exp(l - m) for l in lses]
    den = es[0] + es[1] + es[2]
    cat = [(outs[g] * (es[g] / den)).astype(BF16) for g in range(N_GROUPS)]
    cat.append(co_ref[0].astype(BF16))
    cat = jnp.concatenate(cat, axis=1)
    x1 = x_ref[0] + g1_ref[0] * _dot(cat, wo_ref[...])
    x1_ref[0] = x1

    ms = jnp.mean(x1 * x1, axis=-1, keepdims=True)
    h2 = x1 * lax.rsqrt(ms + RMS_EPS) * ng_ref[...]
    h2 = h2 * (1.0 + sc_ref[0]) + sh_ref[0]
    hh = h2.astype(BF16)
    xg_ref[0, :, 0:D_MODEL] = hh

    hl = (h2 - hh.astype(F32)).astype(BF16)
    wr = wr_ref[...]
    wh = wr.astype(BF16)
    wl = (wr - wh.astype(F32)).astype(BF16)
    logits = _dot_nt(wh, hh) + _dot_nt(wh, hl) + _dot_nt(wl, hh)
    scores = 1.0 / (1.0 + jnp.exp(-logits))
    sel = scores + br_ref[...]

    ids = lax.broadcasted_iota(jnp.int32, (GROUP_SIZE, tile), 0)
    ninf = -jnp.inf
    sel_g = [sel[g * GROUP_SIZE:(g + 1) * GROUP_SIZE] for g in range(N_ROUTE_GROUPS)]
    gscore = jnp.zeros((N_ROUTE_GROUPS, tile), F32)
    for g in range(N_ROUTE_GROUPS):
        m1, oh = _first_max(sel_g[g], ids, GROUP_SIZE)
        m2 = jnp.max(jnp.where(oh, ninf, sel_g[g]), axis=0, keepdims=True)
        gscore = jnp.where(ids == g, m1 + m2, gscore)
    gsel = jnp.zeros((N_ROUTE_GROUPS, tile), F32)
    for _ in range(TOPK_GROUPS):
        _, oh = _first_max(gscore, ids, N_ROUTE_GROUPS)
        gsel = jnp.where(oh, 1.0, gsel)
        gscore = jnp.where(oh, ninf, gscore)
    cand = [jnp.where(gsel[g:g + 1] > 0.0, sel_g[g], ninf) for g in range(N_ROUTE_GROUPS)]
    chosen = [jnp.zeros((GROUP_SIZE, tile), F32) for _ in range(N_ROUTE_GROUPS)]
    picks = []
    for _ in range(TOP_K):
        mx = cand[0]
        for g in range(1, N_ROUTE_GROUPS):
            mx = jnp.maximum(mx, cand[g])
        mx = jnp.max(mx, axis=0, keepdims=True)
        idx = jnp.where(cand[0] == mx, ids, N_EXPERTS)
        for g in range(1, N_ROUTE_GROUPS):
            idx = jnp.minimum(idx, jnp.where(cand[g] == mx, ids + g * GROUP_SIZE, N_EXPERTS))
        idx = jnp.min(idx, axis=0, keepdims=True)
        ohs = [(ids + g * GROUP_SIZE) == idx for g in range(N_ROUTE_GROUPS)]
        picks.append(ohs)
        for g in range(N_ROUTE_GROUPS):
            chosen[g] = jnp.where(ohs[g], 1.0, chosen[g])
            cand[g] = jnp.where(ohs[g], ninf, cand[g])
    wts = [jnp.where(chosen[g] > 0.0, scores[g * GROUP_SIZE:(g + 1) * GROUP_SIZE], 0.0)
           for g in range(N_ROUTE_GROUPS)]
    wsum = jnp.sum(wts[0], axis=0, keepdims=True)
    for g in range(1, N_ROUTE_GROUPS):
        wsum = wsum + jnp.sum(wts[g], axis=0, keepdims=True)
    gates_t = jnp.concatenate([w / wsum * ROUTED_SCALE for w in wts], axis=0)
    gt_ref[0] = jnp.concatenate([gates_t, jnp.zeros((LANES - N_EXPERTS, tile), F32)], axis=0).T
    g_hi = gates_t.astype(BF16).astype(F32)
    xg_ref[0, :, D_MODEL:D_MODEL + LANES] = jnp.concatenate([g_hi, gates_t - g_hi], axis=0).T.astype(BF16)

    er = lax.broadcasted_iota(jnp.int32, (N_EXPERTS, N_EXPERTS), 0)
    ec = lax.broadcasted_iota(jnp.int32, (N_EXPERTS, N_EXPERTS), 1)
    lower = jnp.where(ec < er, 1.0, 0.0).astype(BF16)
    tr = lax.broadcasted_iota(jnp.int32, (SUB, SUB), 0)
    tc = lax.broadcasted_iota(jnp.int32, (SUB, SUB), 1)
    upper = jnp.where(tr < tc, 1.0, 0.0).astype(BF16)
    for u in range(tile // SUB):
        cols = slice(u * SUB, (u + 1) * SUB)
        ch = jnp.concatenate([c[:, cols] for c in chosen], axis=0)
        cnt = jnp.sum(ch, axis=1, keepdims=True)
        cpad = jnp.broadcast_to(jnp.ceil(cnt / CHUNK) * CHUNK, (N_EXPERTS, LANES))
        cnt_ref[u] = cpad
        seg0 = _dot(lower, cpad.astype(BF16))[:, 0:1]
        slot = seg0 + _dot(ch.astype(BF16), upper)
        rows = []
        for k in range(TOP_K):
            acc = jnp.where(picks[k][0][:, cols], slot[0:GROUP_SIZE], 0.0)
            for g in range(1, N_ROUTE_GROUPS):
                acc = acc + jnp.where(picks[k][g][:, cols], slot[g * GROUP_SIZE:(g + 1) * GROUP_SIZE], 0.0)
            rows.append(jnp.sum(acc, axis=0, keepdims=True))
        posr = jnp.concatenate(rows, axis=0)
        posr_ref[u] = posr
        post_ref[0, cols, :] = jnp.concatenate([posr, jnp.zeros((LANES - TOP_K, SUB), F32)], axis=0).T


def _oproj(x, o_list, l_list, co, g1, sc, sh, ng, wo_bf, wr_t, br, *, tile, dils):
    B, S, _ = x.shape
    tm = g1.shape[1]
    mod_spec = pl.BlockSpec((1, tm, D_MODEL), (lambda b, j: (b, j, 0)) if tm > 1 else (lambda b, j: (b, 0, 0)))
    const2 = lambda shape: pl.BlockSpec(shape, lambda b, j: (0, 0))
    nat = lambda w: pl.BlockSpec((1, tile, w), lambda b, j: (b, j, 0))
    dspec = [pl.BlockSpec((1, d, tile // d, GROUP_W), lambda b, j: (b, 0, j, 0)) for d in dils]
    n_slabs = sum(2 * (GROUP_W // LANES) for d in dils if d > 1)
    nsub_t = tile // SUB
    return pl.pallas_call(
        functools.partial(_oproj_kernel, tile=tile, dils=dils),
        grid=(B, S // tile),
        in_specs=[nat(D_MODEL)] + dspec + dspec + [nat(CONV_CH), mod_spec, mod_spec, mod_spec,
                  const2((1, D_MODEL)), const2((D_MODEL, D_MODEL)), const2((N_EXPERTS, D_MODEL)),
                  const2((N_EXPERTS, 1))],
        out_specs=[nat(D_MODEL), nat(XG_W), nat(LANES),
                   pl.BlockSpec((nsub_t, TOP_K, SUB), lambda b, j: (b * (S // tile) + j, 0, 0)),
                   nat(LANES),
                   pl.BlockSpec((nsub_t, N_EXPERTS, LANES), lambda b, j: (b * (S // tile) + j, 0, 0))],
        out_shape=[jax.ShapeDtypeStruct((B, S, D_MODEL), F32), jax.ShapeDtypeStruct((B, S, XG_W), BF16),
                   jax.ShapeDtypeStruct((B, S, LANES), F32),
                   jax.ShapeDtypeStruct((B * S // SUB, TOP_K, SUB), F32),
                   jax.ShapeDtypeStruct((B, S, LANES), F32),
                   jax.ShapeDtypeStruct((B * S // SUB, N_EXPERTS, LANES), F32)],
        scratch_shapes=[pltpu.VMEM((max(n_slabs, 1), tile, LANES), F32)],
        compiler_params=_params(("arbitrary", "arbitrary")),
        name="oproj",
    )(x, *o_list, *l_list, co, g1, sc, sh, ng, wo_bf, wr_t, br)


def _moe_kernel(h_ref, gt_ref, x1_ref, g2_ref, wg_ref, wu_ref, wd_ref, wgs_ref, wus_ref, wds_ref,
                y_ref, acc, *, tile):
    e = pl.program_id(2)
    h = h_ref[0]

    @pl.when(e == 0)
    def _():
        a = _dot(h, wgs_ref[...].astype(BF16))
        b = _dot(h, wus_ref[...].astype(BF16))
        acc[...] = _dot((_silu(a) * b).astype(BF16), wds_ref[...].astype(BF16))

    a = _dot(h, wg_ref[0].astype(BF16))
    b = _dot(h, wu_ref[0].astype(BF16))
    lane = lax.broadcasted_iota(jnp.int32, (tile, LANES), 1)
    gcol = jnp.sum(jnp.where(lane == e, gt_ref[0], 0.0), axis=1, keepdims=True)
    hc = _silu(a) * b * gcol
    acc[...] += _dot(hc.astype(BF16), wd_ref[0].astype(BF16))

    @pl.when(e == N_EXPERTS - 1)
    def _():
        y_ref[0] = x1_ref[0] + g2_ref[0] * acc[...]


def _moe(h2, gates, x1, g2, wg, wu, wd, wgs, wus, wds, *, tile):
    B, S, _ = h2.shape
    tm = g2.shape[1]
    mod_spec = pl.BlockSpec((1, tm, D_MODEL),
                            (lambda b, j, e: (b, j, 0)) if tm > 1 else (lambda b, j, e: (b, 0, 0)))
    nat = lambda w: pl.BlockSpec((1, tile, w), lambda b, j, e: (b, j, 0))
    const2 = lambda shape: pl.BlockSpec(shape, lambda b, j, e: (0, 0))
    return pl.pallas_call(
        functools.partial(_moe_kernel, tile=tile),
        grid=(B, S // tile, N_EXPERTS),
        in_specs=[nat(D_MODEL), nat(LANES), nat(D_MODEL), mod_spec,
                  pl.BlockSpec((1, D_MODEL, D_EXPERT), lambda b, j, e: (e, 0, 0)),
                  pl.BlockSpec((1, D_MODEL, D_EXPERT), lambda b, j, e: (e, 0, 0)),
                  pl.BlockSpec((1, D_EXPERT, D_MODEL), lambda b, j, e: (e, 0, 0)),
                  const2((D_MODEL, D_EXPERT)), const2((D_MODEL, D_EXPERT)), const2((D_EXPERT, D_MODEL))],
        out_specs=nat(D_MODEL),
        out_shape=jax.ShapeDtypeStruct((B, S, D_MODEL), F32),
        scratch_shapes=[pltpu.VMEM((tile, D_MODEL), F32)],
        compiler_params=_params(("arbitrary", "arbitrary", "arbitrary")),
        name="moe",
    )(h2, gates, x1, g2, wg, wu, wd, wgs, wus, wds)


FFN_TM = 512
SUB_ROWS = SUB * TOP_K + N_EXPERTS * CHUNK
N_CHUNKS = SUB_ROWS // CHUNK
ZERO_CHUNK = N_CHUNKS
TABLE_W = 256
MXU_ROWS = 256


def _slot_onehot(first, pos_list, axis_iota):
    hit = axis_iota == (pos_list[0] - first)
    for p in pos_list[1:]:
        hit = hit | (axis_iota == (p - first))
    return jnp.where(hit, 1.0, 0.0).astype(BF16)


def _chunk_copy(src, src_chunk, dst, dst_chunk, sem):
    return pltpu.make_async_copy(src.at[pl.ds(pl.multiple_of(src_chunk * CHUNK, CHUNK), CHUNK)],
                                 dst.at[pl.ds(pl.multiple_of(dst_chunk * CHUNK, CHUNK), CHUNK)], sem)


def _dispatch_kernel(nblk_ref, ncopy_ref, ntile_ref, src_ref, dst_ref, posr_ref, xg_ref, xs_hbm, xs_scr,
                     zero_scr, sem, tail_sem, *, nsub, n_tiles_max):
    s = pl.program_id(0)

    zero_scr[...] = jnp.zeros((FFN_TM, XG_W), BF16)
    tail_blocks = [ntile_ref[0] + s + r * nsub for r in range(-(-(n_tiles_max) // nsub))]

    def tail_copy(t):
        return pltpu.make_async_copy(
            zero_scr, xs_hbm.at[pl.ds(pl.multiple_of(t * FFN_TM, FFN_TM), FFN_TM)], tail_sem)

    for t in tail_blocks:
        @pl.when(t < n_tiles_max)
        def _():
            tail_copy(t).start()

    xg = xg_ref[...]
    pos = [posr_ref[0, k:k + 1, :] for k in range(TOP_K)]
    rows = lax.broadcasted_iota(jnp.int32, (MXU_ROWS, SUB), 0).astype(F32)
    xs_scr[SUB_ROWS:SUB_ROWS + CHUNK, :] = jnp.zeros((CHUNK, XG_W), BF16)

    def sort_block(i, carry):
        r0 = pl.multiple_of(i * MXU_ROWS, MXU_ROWS)
        onehot = _slot_onehot((i * MXU_ROWS).astype(F32), pos, rows)
        xs_scr[pl.ds(r0, MXU_ROWS), :] = _dot(onehot, xg).astype(BF16)
        return carry

    lax.fori_loop(0, nblk_ref[s], sort_block, 0)
    n = ncopy_ref[s]

    def start(i, carry):
        _chunk_copy(xs_scr, src_ref[0, 0, i], xs_hbm, dst_ref[0, 0, i], sem).start()
        return carry

    def wait(i, carry):
        _chunk_copy(xs_scr, 0, xs_hbm, 0, sem).wait()
        return carry

    lax.fori_loop(0, n, start, 0)
    lax.fori_loop(0, n, wait, 0)
    for t in tail_blocks:
        @pl.when(t < n_tiles_max)
        def _():
            tail_copy(t).wait()


def _dispatch(nblk, ncopy, n_tiles, src, dst, posr, xg, n_rows):
    nsub = posr.shape[0]
    tab = pl.BlockSpec((1, 1, TABLE_W), lambda s, *_: (s, 0, 0), memory_space=pltpu.SMEM)
    return pl.pallas_call(
        functools.partial(_dispatch_kernel, nsub=nsub, n_tiles_max=n_rows // FFN_TM),
        grid_spec=pltpu.PrefetchScalarGridSpec(
            num_scalar_prefetch=3, grid=(nsub,),
            in_specs=[tab, tab, pl.BlockSpec((1, TOP_K, SUB), lambda s, *_: (s, 0, 0)),
                      pl.BlockSpec((SUB, XG_W), lambda s, *_: (s, 0))],
            out_specs=pl.BlockSpec(memory_space=pl.ANY),
            scratch_shapes=[pltpu.VMEM((SUB_ROWS + CHUNK, XG_W), BF16), pltpu.VMEM((FFN_TM, XG_W), BF16),
                            pltpu.SemaphoreType.DMA(()), pltpu.SemaphoreType.DMA(())]),
        out_shape=jax.ShapeDtypeStruct((n_rows, XG_W), BF16),
        compiler_params=_params(("arbitrary",)),
        name="moe_dispatch",
    )(nblk, ncopy, n_tiles, src, dst, posr, xg)


def _ffn_kernel(te_ref, nt_ref, xs_ref, wg_ref, wu_ref, wd_ref, ys_ref, wg_s, wu_s, wd_s):
    i = pl.program_id(0)

    @pl.when(i >= nt_ref[0])
    def _():
        ys_ref[...] = jnp.zeros((FFN_TM, D_MODEL), BF16)

    @pl.when(i < nt_ref[0])
    def _():
        e = te_ref[i]

        @pl.when((i == 0) | (e != te_ref[jnp.maximum(i - 1, 0)]))
        def _():
            wg_s[...] = wg_ref[0].astype(BF16)
            wu_s[...] = wu_ref[0].astype(BF16)
            wd_s[...] = wd_ref[0].astype(BF16)

        x = xs_ref[:, 0:D_MODEL]
        g = xs_ref[:, D_MODEL:XG_W].astype(F32)
        lane = lax.broadcasted_iota(jnp.int32, (FFN_TM, LANES), 1)
        gcol = jnp.sum(jnp.where((lane == e) | (lane == e + N_EXPERTS), g, 0.0), axis=1, keepdims=True)
        a = _dot(x, wg_s[...])
        b = _dot(x, wu_s[...])
        hc = _silu(a) * b * gcol
        ys_ref[...] = _dot(hc.astype(BF16), wd_s[...]).astype(BF16)


def _ffn(tile_expert, n_tiles, xs, wg, wu, wd):
    n_rows = xs.shape[0]
    row = lambda i, te, nt: (jnp.minimum(i, nt[0] - 1), 0)
    wspec = lambda shape: pl.BlockSpec((1,) + shape, lambda i, te, nt: (te[i], 0, 0))
    return pl.pallas_call(
        _ffn_kernel,
        grid_spec=pltpu.PrefetchScalarGridSpec(
            num_scalar_prefetch=2, grid=(n_rows // FFN_TM,),
            in_specs=[pl.BlockSpec((FFN_TM, XG_W), row), wspec((D_MODEL, D_EXPERT)),
                      wspec((D_MODEL, D_EXPERT)), wspec((D_EXPERT, D_MODEL))],
            out_specs=pl.BlockSpec((FFN_TM, D_MODEL), lambda i, te, nt: (i, 0)),
            scratch_shapes=[pltpu.VMEM((D_MODEL, D_EXPERT), BF16), pltpu.VMEM((D_MODEL, D_EXPERT), BF16),
                            pltpu.VMEM((D_EXPERT, D_MODEL), BF16)]),
        out_shape=jax.ShapeDtypeStruct((n_rows, D_MODEL), BF16),
        compiler_params=_params(("arbitrary",)),
        name="moe_ffn",
    )(tile_expert, n_tiles, xs, wg, wu, wd)


def _combine_kernel(nblk_ref, ncopy_ref, src_ref, post_ref, xg_ref, x1_ref, g2_ref, wgs_ref, wus_ref,
                    wds_ref, ys_hbm, y_ref, ys_scr, wgs_s, wus_s, wds_s, sem):
    s = pl.program_id(0)
    n = ncopy_ref[s]

    def start(i, carry):
        _chunk_copy(ys_hbm, src_ref[0, 0, i], ys_scr, i, sem).start()
        return carry

    def wait(i, carry):
        _chunk_copy(ys_hbm, 0, ys_scr, 0, sem).wait()
        return carry

    lax.fori_loop(0, n, start, 0)

    @pl.when(s == 0)
    def _():
        wgs_s[...] = wgs_ref[...].astype(BF16)
        wus_s[...] = wus_ref[...].astype(BF16)
        wds_s[...] = wds_ref[...].astype(BF16)

    h = xg_ref[:, 0:D_MODEL]
    a = _dot(h, wgs_s[...])
    b = _dot(h, wus_s[...])
    shared = _dot((_silu(a) * b).astype(BF16), wds_s[...])

    post = post_ref[...]
    pos = [post[:, k:k + 1] for k in range(TOP_K)]
    lanes = lax.broadcasted_iota(jnp.int32, (SUB, MXU_ROWS), 1).astype(F32)
    lax.fori_loop(0, n, wait, 0)

    def gather_block(i, acc):
        r0 = pl.multiple_of(i * MXU_ROWS, MXU_ROWS)
        onehot = _slot_onehot((i * MXU_ROWS).astype(F32), pos, lanes)
        return acc + _dot(onehot, ys_scr[pl.ds(r0, MXU_ROWS), :])

    routed = lax.fori_loop(0, nblk_ref[s], gather_block, jnp.zeros((SUB, D_MODEL), F32))
    y_ref[...] = x1_ref[...] + g2_ref[0] * (routed + shared)


def _combine(nblk, ncopy, src, post, xg, x1, g2, wgs, wus, wds, ys):
    n_tok = x1.shape[0]
    nsub = n_tok // SUB
    per_b = nsub // g2.shape[0]
    tab = pl.BlockSpec((1, 1, TABLE_W), lambda s, *_: (s, 0, 0), memory_space=pltpu.SMEM)
    const2 = lambda shape: pl.BlockSpec(shape, lambda s, *_: (0, 0))
    return pl.pallas_call(
        _combine_kernel,
        grid_spec=pltpu.PrefetchScalarGridSpec(
            num_scalar_prefetch=2, grid=(nsub,),
            in_specs=[tab, pl.BlockSpec((SUB, LANES), lambda s, *_: (s, 0)),
                      pl.BlockSpec((SUB, XG_W), lambda s, *_: (s, 0)),
                      pl.BlockSpec((SUB, D_MODEL), lambda s, *_: (s, 0)),
                      pl.BlockSpec((1, 1, D_MODEL), lambda s, *_: (s // per_b, 0, 0)),
                      const2((D_MODEL, D_EXPERT)), const2((D_MODEL, D_EXPERT)), const2((D_EXPERT, D_MODEL)),
                      pl.BlockSpec(memory_space=pl.ANY)],
            out_specs=pl.BlockSpec((SUB, D_MODEL), lambda s, *_: (s, 0)),
            scratch_shapes=[pltpu.VMEM((SUB_ROWS, D_MODEL), BF16), pltpu.VMEM((D_MODEL, D_EXPERT), BF16),
                            pltpu.VMEM((D_MODEL, D_EXPERT), BF16), pltpu.VMEM((D_EXPERT, D_MODEL), BF16),
                            pltpu.SemaphoreType.DMA(())]),
        out_shape=jax.ShapeDtypeStruct((n_tok, D_MODEL), F32),
        compiler_params=_params(("arbitrary",)),
        name="moe_combine",
    )(nblk, ncopy, src, post, xg, x1, g2, wgs, wus, wds, ys)


def _route_tables(cnt):
    nsub = cnt.shape[0]
    i32 = jnp.int32
    seg_end = jnp.cumsum(cnt, axis=1)
    seg_start = seg_end - cnt
    tot = jnp.sum(cnt, axis=0)
    tot_al = (tot + FFN_TM - 1) // FFN_TM * FFN_TM
    reg_end = jnp.cumsum(tot_al)
    reg_start = reg_end - tot_al
    base = reg_start[None, :] + jnp.cumsum(cnt, axis=0) - cnt
    n_real = seg_end[:, -1] // CHUNK
    k = jnp.arange(TABLE_W, dtype=i32)
    owner = jnp.sum((seg_end[:, None, :] // CHUNK) <= k[None, :, None], axis=2)
    owner = jnp.minimum(owner, N_EXPERTS - 1)
    delta = (base - seg_start) // CHUNK
    real_dst = jnp.take_along_axis(delta, owner, axis=1) + k[None, :]
    fill_n = (tot_al - tot) // CHUNK
    fill_dst0 = (reg_start + tot) // CHUNK
    src = jnp.broadcast_to(k[None, :], (nsub, TABLE_W))
    dst = real_dst
    ncopy = n_real
    for r in range(-(-N_EXPERTS // nsub)):
        e_of = jnp.arange(nsub, dtype=i32) + r * nsub
        ok = e_of < N_EXPERTS
        e_cl = jnp.minimum(e_of, N_EXPERTS - 1)
        fn = jnp.where(ok, fill_n[e_cl], 0)
        j = k[None, :] - ncopy[:, None]
        is_fill = (j >= 0) & (j < fn[:, None])
        src = jnp.where(is_fill, ZERO_CHUNK, src)
        dst = jnp.where(is_fill, fill_dst0[e_cl][:, None] + j, dst)
        ncopy = ncopy + fn
    disp_nblk = (n_real * CHUNK + MXU_ROWS - 1) // MXU_ROWS
    comb_n = disp_nblk * (MXU_ROWS // CHUNK)
    comb_src = jnp.where(k[None, :] < n_real[:, None], real_dst, real_dst[:, 0:1])
    n_tiles = reg_end[-1] // FFN_TM
    t = jnp.arange((nsub * SUB_ROWS + N_EXPERTS * FFN_TM) // FFN_TM, dtype=i32)
    tile_expert = jnp.minimum(jnp.sum((reg_end[None, :] // FFN_TM) <= t[:, None], axis=1), N_EXPERTS - 1)
    as3 = lambda a: a.astype(i32).reshape(nsub, 1, TABLE_W)
    return (disp_nblk.astype(i32), ncopy.astype(i32), as3(src), as3(dst), comb_n.astype(i32), as3(comb_src),
            tile_expert.astype(i32), n_tiles.astype(i32).reshape(1))


def _moe_sparse(xg, post, posr, cnt, x1, g2, wg, wu, wd, wgs, wus, wds):
    B, S, _ = x1.shape
    n_tok = B * S
    nsub = n_tok // SUB
    (disp_nblk, disp_n, disp_src, disp_dst, comb_n, comb_src, tile_expert, n_tiles) = _route_tables(
        cnt[:, :, 0].astype(jnp.int32))
    n_rows = nsub * SUB_ROWS + N_EXPERTS * FFN_TM
    xg2 = xg.reshape(n_tok, XG_W)
    xs = _dispatch(disp_nblk, disp_n, n_tiles, disp_src, disp_dst, posr, xg2, n_rows)
    ys = _ffn(tile_expert, n_tiles, xs, wg, wu, wd)
    y = _combine(disp_nblk, comb_n, comb_src, post.reshape(n_tok, LANES), xg2, x1.reshape(n_tok, D_MODEL),
                 g2, wgs, wus, wds, ys)
    return y.reshape(B, S, D_MODEL)


def _t5_bucket(dist):
    max_exact = N_BUCKETS // 2
    df = jnp.maximum(dist, 1).astype(F32)
    large = max_exact + (jnp.log(df / max_exact) / math.log(MAX_DISTANCE / max_exact)
                         * (N_BUCKETS - max_exact)).astype(jnp.int32)
    large = jnp.minimum(large, N_BUCKETS - 1)
    return jnp.where(dist < max_exact, dist, large)


def _step_bias(rel_bias, g):
    dist = jnp.arange(N_KEYS, dtype=jnp.int32) * DILATIONS[g]
    cols = rel_bias[:, g * HEADS_PER_GROUP:(g + 1) * HEADS_PER_GROUP]
    return cols[_t5_bucket(dist)].T.astype(F32)


def _prompt_bias(bias_k):
    h = bias_k.shape[0]
    n = 3 * BLK
    row = jnp.concatenate([bias_k[:, ::-1], jnp.full((h, n - N_KEYS), NEG, F32)], axis=1)
    t = jnp.tile(row, (1, BLK))[:, :BLK * (n - 1)].reshape(h, BLK, n - 1)
    return t[:, :, :2 * BLK]


def _sample_bias(bias_k, wb, d):
    h = bias_k.shape[0]
    rev = bias_k[:, :0:-1]
    rows = []
    for t in range(4):
        if d == 1:
            rows.append(jnp.concatenate([jnp.full((h, t), NEG, F32), rev[:, :wb - t]], axis=1))
        else:
            r = jnp.arange(d)[None, None, :]
            rows.append(jnp.where(r == t, rev[:, :, None], NEG).reshape(h, wb))
    b1 = jnp.concatenate([jnp.stack(rows, axis=1), jnp.zeros((h, 4, wb), F32)], axis=1)
    rows = []
    for t in range(4):
        cols = [bias_k[:, (t - tn) // d] if (t >= tn and (t - tn) % d == 0) else jnp.full((h,), NEG, F32)
                for tn in range(4)]
        rows.append(jnp.concatenate([jnp.full((h, LANES - 4), NEG, F32), jnp.stack(cols, axis=1)], axis=1))
    b2 = jnp.concatenate([jnp.stack(rows, axis=1), jnp.zeros((h, 4, LANES), F32)], axis=1)
    return b1, b2


def _layer(x, mods, hist, caches, rel_bias, weights, *, sample):
    (norm1_g, norm2_g, w_in_bf, qg, kg, conv_w, bd, w_o_bf, w_r_t, b_r, w_gate_e, w_up_e, w_down_e,
     w_gate_s, w_up_s, w_down_s) = weights
    sh1, sc1, g1, sh2, sc2, g2 = mods
    B, S, _ = x.shape
    tile = 512
    dils = (1, 1, 1) if sample else DILATIONS
    res = _proj(x, sc1, sh1, norm1_g, w_in_bf, qg, kg, conv_w, bd, hist, tile=tile, dils=dils, sample=sample)
    states = []
    o_list, l_list = [], []
    if sample:
        qf, kf, vf, u, co = res
        nb = S // 4

        def heads_t(a, g):
            a = a[0, :, g * GROUP_W:(g + 1) * GROUP_W].reshape(nb, 4, HEADS_PER_GROUP, HEAD_DIM)
            return a.transpose(0, 2, 1, 3)

        for g in range(N_GROUPS):
            d, wb = DILATIONS[g], WINDOWS[g]
            kc, vc = caches[g]
            bias_k = _step_bias(rel_bias, g)
            b1, b2 = _sample_bias(bias_k, wb, d)
            q = jnp.pad(heads_t(qf, g), ((0, 0), (0, 0), (0, 4), (0, 0))).astype(BF16)
            new_t = lambda a: jnp.pad(heads_t(a, g).transpose(0, 1, 3, 2),
                                      ((0, 0), (0, 0), (0, 0), (LANES - 4, 0)))
            ko, vo, o, lse = _attn_sample(q, kc, vc, new_t(kf), new_t(vf), b1, b2,
                                          bb={128: 8, 512: 4, 2048: 1}[wb])
            states += [ko.transpose(0, 3, 1, 2)[None], vo.transpose(0, 3, 1, 2)[None]]
            o = o[:, :, :4].transpose(0, 2, 1, 3).reshape(1, 1, S, GROUP_W)
            lse = jnp.broadcast_to(lse[:, :, :4, :1], (nb, HEADS_PER_GROUP, 4, HEAD_DIM))
            o_list.append(o)
            l_list.append(lse.transpose(0, 2, 1, 3).reshape(1, 1, S, GROUP_W))
        states.append(u.reshape(nb, 4, CONV_CH)[:, 2:][None])
    else:
        qkv, (kf, vf, u, co) = res[:9], res[9:]
        for g in range(N_GROUPS):
            d, w = DILATIONS[g], WINDOWS[g]
            q, k, v = (a.reshape(B * d, S // d, GROUP_W) for a in qkv[3 * g:3 * g + 3])
            o, lse = _attn_prompt(q, k, v, _prompt_bias(_step_bias(rel_bias, g)))
            o_list.append(o.reshape(B, d, S // d, GROUP_W))
            l_list.append(lse.reshape(B, d, S // d, GROUP_W))
            keep = lambda a: a[:, S - w:, g * GROUP_W:(g + 1) * GROUP_W].reshape(
                B, w, HEADS_PER_GROUP, HEAD_DIM)[None]
            states += [keep(kf), keep(vf)]
        states.append(u[:, S - 2:][None])
    x1, xg, gates, posr, post, cnt = _oproj(x, o_list, l_list, co, g1, sc2, sh2, norm2_g, w_o_bf, w_r_t, b_r,
                                            tile=tile, dils=dils)
    if sample:
        y = _moe(xg, gates, x1, g2, w_gate_e, w_up_e, w_down_e, w_gate_s, w_up_s, w_down_s, tile=tile)
    else:
        y = _moe_sparse(xg, post, posr, cnt, x1, g2, w_gate_e, w_up_e, w_down_e, w_gate_s, w_up_s, w_down_s)
    return y, states


def kernel(x_prompt, x_sample, c_prompt, c_sample, cache_k_w128, cache_v_w128, cache_k_w512, cache_v_w512, cache_k_w2048, cache_v_w2048, state_conv, rel_bias, norm1_g, norm2_g, w_ada, b_ada, w_in, q_norm_g, k_norm_g, conv_w, w_o, w_router, b_router, w_gate_e, w_up_e, w_down_e, w_gate_s, w_up_s, w_down_s):
    B = x_prompt.shape[0]
    DB, T = x_sample.shape[:2]
    n_c = B + DB
    c_all = jnp.pad(jnp.concatenate([c_prompt, c_sample], axis=0), ((0, (-n_c) % 8), (0, 0)))
    mod = _ada(c_all, w_ada[0], b_ada)
    chunks = [mod[:, i * D_MODEL:(i + 1) * D_MODEL] for i in range(6)]
    mods_p = [c[:B].reshape(B, 1, D_MODEL) for c in chunks]
    mods_s = [jnp.repeat(c[B:n_c], T, axis=0).reshape(1, DB * T, D_MODEL) for c in chunks]

    eye = jnp.arange(GROUP_W) // HEAD_DIM
    bd = jnp.where(eye[:, None] == eye[None, :], 1.0 / HEAD_DIM, 0.0).astype(BF16)
    weights = (norm1_g, norm2_g, w_in[0].astype(BF16),
               jnp.tile(q_norm_g, (1, HEADS_PER_GROUP)), jnp.tile(k_norm_g, (1, HEADS_PER_GROUP)),
               conv_w[0], bd, w_o[0].astype(BF16), w_router[0].T, b_router.reshape(N_EXPERTS, 1),
               w_gate_e[0], w_up_e[0], w_down_e[0], w_gate_s[0], w_up_s[0], w_down_s[0])

    yp, st_p = _layer(x_prompt, mods_p, None, None, rel_bias, weights, sample=False)

    s0, s1 = state_conv[0, :, 0], state_conv[0, :, 1]
    zero = jnp.zeros_like(s0)
    hist_a = jnp.stack([s0, s1, zero, zero], axis=1).reshape(DB * T, CONV_CH)
    hist_b = jnp.stack([s1, zero, zero, zero], axis=1).reshape(DB * T, CONV_CH)
    caches = [(ck[0].transpose(0, 2, 3, 1), cv[0].transpose(0, 2, 3, 1))
              for ck, cv in ((cache_k_w128, cache_v_w128), (cache_k_w512, cache_v_w512),
                             (cache_k_w2048, cache_v_w2048))]
    ys, st_s = _layer(x_sample.reshape(1, DB * T, D_MODEL), mods_s, (hist_a, hist_b), caches, rel_bias,
                      weights, sample=True)
    return (yp, ys.reshape(DB, T, D_MODEL), *st_p, *st_s)
```

```python
import functools
import math

import jax
import jax.numpy as jnp
from jax import lax
from jax.experimental import pallas as pl
from jax.experimental.pallas import tpu as pltpu

D_MODEL = 1024
HEAD_DIM = 64
HEADS_PER_GROUP = 4
GROUP_W = HEADS_PER_GROUP * HEAD_DIM
WINDOWS = (128, 512, 2048)
DILATIONS = (1, 4, 16)
N_GROUPS = 3
ATTN_W = N_GROUPS * GROUP_W
CONV_CH = 256
N_PROJ = 3 * ATTN_W + 3 * CONV_CH
N_STEPS = 128
N_KEYS = N_STEPS + 1
BLK = 128
N_BUCKETS = 32
MAX_DISTANCE = WINDOWS[-1]
N_EXPERTS = 64
TOP_K = 8
N_ROUTE_GROUPS = 8
GROUP_SIZE = N_EXPERTS // N_ROUTE_GROUPS
TOPK_GROUPS = 4
D_EXPERT = 256
ROUTED_SCALE = 2.5
RMS_EPS = 1e-6
NEG = -1e30
LANES = 128
SUB = 256
CHUNK = 16
XG_W = D_MODEL + LANES
VMEM_LIMIT = 56 * 1024 * 1024

F32 = jnp.float32
BF16 = jnp.bfloat16


def _dot(a, b):
    return jnp.dot(a, b, preferred_element_type=F32)


def _dot_nt(a, b):
    return lax.dot_general(a, b, (((1,), (1,)), ((), ())), preferred_element_type=F32)


def _silu(a):
    return a / (1.0 + jnp.exp(-a))


def _params(sem):
    return pltpu.CompilerParams(dimension_semantics=sem, vmem_limit_bytes=VMEM_LIMIT)


def _ada_kernel(c_ref, w_ref, b_ref, o_ref):
    s = _silu(c_ref[...])
    o_ref[...] = _dot(s.astype(BF16), w_ref[...].astype(BF16)) + b_ref[...]


def _ada(c, w_ada, b_ada):
    n = c.shape[0]
    nc = w_ada.shape[1] // D_MODEL
    return pl.pallas_call(
        _ada_kernel,
        grid=(nc,),
        in_specs=[pl.BlockSpec((n, D_MODEL), lambda j: (0, 0)),
                  pl.BlockSpec((D_MODEL, D_MODEL), lambda j: (0, j)),
                  pl.BlockSpec((1, D_MODEL), lambda j: (0, j))],
        out_specs=pl.BlockSpec((n, D_MODEL), lambda j: (0, j)),
        out_shape=jax.ShapeDtypeStruct((n, w_ada.shape[1]), F32),
        compiler_params=_params(("arbitrary",)),
        name="ada",
    )(c, w_ada, b_ada)


def _proj_kernel(*refs, tile, dils, sample):
    if sample:
        (x_ref, sc_ref, sh_ref, ng_ref, w_ref, qg_ref, kg_ref, cw_ref, bd_ref, ha_ref, hb_ref,
         qf_ref, kf_ref, vf_ref, u_ref, co_ref, u_scr) = refs
    else:
        (x_ref, sc_ref, sh_ref, ng_ref, w_ref, qg_ref, kg_ref, cw_ref, bd_ref,
         q0, k0, v0, q1, k1, v1, q2, k2, v2, kf_ref, vf_ref, u_ref, co_ref, slab, u_scr) = refs
        qkv_out = ((q0, k0, v0), (q1, k1, v1), (q2, k2, v2))

    x = x_ref[0]
    ms = jnp.mean(x * x, axis=-1, keepdims=True)
    h = x * lax.rsqrt(ms + RMS_EPS) * ng_ref[...]
    h = h * (1.0 + sc_ref[0]) + sh_ref[0]
    proj = _dot(h.astype(BF16), w_ref[...])

    bd = bd_ref[...]

    def headnorm(z, g):
        zz = z * z
        hi = zz.astype(BF16)
        lo = (zz - hi.astype(F32)).astype(BF16)
        msq = _dot(hi, bd) + _dot(lo, bd)
        return z * lax.rsqrt(msq + RMS_EPS) * g

    slab_i = 0
    for g in range(N_GROUPS):
        c0 = g * GROUP_W
        qn = headnorm(proj[:, c0:c0 + GROUP_W], qg_ref[...]) * (HEAD_DIM ** -0.5)
        kn = headnorm(proj[:, ATTN_W + c0:ATTN_W + c0 + GROUP_W], kg_ref[...])
        vv = proj[:, 2 * ATTN_W + c0:2 * ATTN_W + c0 + GROUP_W]
        kf_ref[0, :, c0:c0 + GROUP_W] = kn
        vf_ref[0, :, c0:c0 + GROUP_W] = vv
        if sample:
            qf_ref[0, :, c0:c0 + GROUP_W] = qn
            continue
        d = dils[g]
        for val, out in zip((qn, kn, vv), qkv_out[g]):
            if d == 1:
                out[0, 0] = val.astype(BF16)
                continue
            n = tile // d
            for half in range(GROUP_W // LANES):
                slab[slab_i] = val[:, half * LANES:(half + 1) * LANES]
                for r in range(d):
                    out[0, r, :, half * LANES:(half + 1) * LANES] = (
                        slab[slab_i, pl.ds(r, n, stride=d), :].astype(BF16))
                slab_i += 1

    base = 3 * ATTN_W
    u = proj[:, base + 2 * CONV_CH:base + 3 * CONV_CH] * proj[:, base:base + CONV_CH]
    gate_b = proj[:, base + CONV_CH:base + 2 * CONV_CH]
    if sample:
        u_scr[0:8, :] = jnp.zeros((8, CONV_CH), F32)
    else:
        j = pl.program_id(1)

        @pl.when(j == 0)
        def _():
            u_scr[0:8, :] = jnp.zeros((8, CONV_CH), F32)

        @pl.when(j > 0)
        def _():
            u_scr[0:8, :] = u_scr[tile:tile + 8, :]

    u_scr[8:tile + 8, :] = u
    um1 = u_scr[7:tile + 7, :]
    um2 = u_scr[6:tile + 6, :]
    if sample:
        t = lax.broadcasted_iota(jnp.int32, (tile, CONV_CH), 0) % 4
        um1 = jnp.where(t >= 1, um1, 0.0) + hb_ref[...]
        um2 = jnp.where(t >= 2, um2, 0.0) + ha_ref[...]
    cw = cw_ref[...]
    conv = cw[0:1] * um2 + cw[1:2] * um1 + cw[2:3] * u
    u_ref[0] = u
    co_ref[0] = gate_b * conv


def _proj(x, sc, sh, ng, w_bf, qg, kg, cw, bd, hist, *, tile, dils, sample):
    B, S, _ = x.shape
    nt = S // tile
    tm = sc.shape[1]
    mod_spec = pl.BlockSpec((1, tm, D_MODEL), (lambda b, j: (b, j, 0)) if tm > 1 else (lambda b, j: (b, 0, 0)))
    const2 = lambda shape: pl.BlockSpec(shape, lambda b, j: (0, 0))
    in_specs = [pl.BlockSpec((1, tile, D_MODEL), lambda b, j: (b, j, 0)), mod_spec, mod_spec,
                const2((1, D_MODEL)), const2((D_MODEL, N_PROJ)), const2((1, GROUP_W)),
                const2((1, GROUP_W)), const2((3, CONV_CH)), const2((GROUP_W, GROUP_W))]
    args = [x, sc, sh, ng, w_bf, qg, kg, cw, bd]
    nat = lambda w: pl.BlockSpec((1, tile, w), lambda b, j: (b, j, 0))
    out_specs, out_shape = [], []
    scratch = []
    if sample:
        in_specs += [pl.BlockSpec((tile, CONV_CH), lambda b, j: (j, 0))] * 2
        args += list(hist)
        out_specs.append(nat(ATTN_W))
        out_shape.append(jax.ShapeDtypeStruct((B, S, ATTN_W), F32))
    else:
        for d in dils:
            for _ in range(3):
                out_specs.append(pl.BlockSpec((1, d, tile // d, GROUP_W), lambda b, j: (b, 0, j, 0)))
                out_shape.append(jax.ShapeDtypeStruct((B, d, S // d, GROUP_W), BF16))
        n_slabs = sum(3 * (GROUP_W // LANES) for d in dils if d > 1)
        scratch.append(pltpu.VMEM((n_slabs, tile, LANES), F32))
    out_specs += [nat(ATTN_W), nat(ATTN_W), nat(CONV_CH), nat(CONV_CH)]
    out_shape += [jax.ShapeDtypeStruct((B, S, ATTN_W), F32)] * 2
    out_shape += [jax.ShapeDtypeStruct((B, S, CONV_CH), F32)] * 2
    scratch.append(pltpu.VMEM((tile + 8, CONV_CH), F32))
    return pl.pallas_call(
        functools.partial(_proj_kernel, tile=tile, dils=dils, sample=sample),
        grid=(B, nt),
        in_specs=in_specs,
        out_specs=out_specs,
        out_shape=out_shape,
        scratch_shapes=scratch,
        compiler_params=_params(("arbitrary", "arbitrary")),
        name="proj_sample" if sample else "proj_prompt",
    )(*args)


def _attn_kernel(q_ref, kp_ref, kc_ref, vp_ref, vc_ref, b_ref, o_ref, l_ref, *, nq):
    first = pl.program_id(1) == 0
    low = lax.broadcasted_iota(jnp.int32, (BLK, LANES), 1) < HEAD_DIM
    for j in range(nq):
        rows = slice(j * BLK, (j + 1) * BLK)
        prows = slice((j - 1) * BLK, j * BLK)
        for pair in range(GROUP_W // LANES):
            sl = slice(pair * LANES, (pair + 1) * LANES)
            q = q_ref[0, rows, sl]
            if j == 0:
                kp, vp = kp_ref[0, :, sl], vp_ref[0, :, sl]
            else:
                kp, vp = kc_ref[0, prows, sl], vc_ref[0, prows, sl]
            kc, vc = kc_ref[0, rows, sl], vc_ref[0, rows, sl]
            o_sub, l_sub = [], []
            for sub in range(2):
                hh = 2 * pair + sub
                qm = jnp.where(low if sub == 0 else ~low, q, jnp.zeros_like(q))
                sp = _dot_nt(qm, kp) + b_ref[hh, :, 0:BLK]
                if j == 0:
                    sp = jnp.where(first, NEG, sp)
                sc = _dot_nt(qm, kc) + b_ref[hh, :, BLK:2 * BLK]
                m = jnp.max(jnp.maximum(sp, sc), axis=-1, keepdims=True)
                pp = jnp.exp(sp - m)
                pc = jnp.exp(sc - m)
                z = jnp.sum(pp + pc, axis=-1, keepdims=True)
                o = _dot(pp.astype(BF16), vp) + _dot(pc.astype(BF16), vc)
                o_sub.append(o / z)
                l_sub.append(jnp.broadcast_to(m + jnp.log(z), (BLK, LANES)))
            o_ref[0, rows, sl] = jnp.where(low, o_sub[0], o_sub[1])
            l_ref[0, rows, sl] = jnp.where(low, l_sub[0], l_sub[1])


def _attn_prompt(q, k, v, bias):
    Z, L, _ = q.shape
    qt = min(L, 4 * BLK)
    nq = qt // BLK
    cur = pl.BlockSpec((1, qt, GROUP_W), lambda z, i: (z, i, 0))
    prev = pl.BlockSpec((1, BLK, GROUP_W), lambda z, i: (z, jnp.maximum(i * nq - 1, 0), 0))
    return pl.pallas_call(
        functools.partial(_attn_kernel, nq=nq),
        grid=(Z, L // qt),
        in_specs=[cur, prev, cur, prev, cur,
                  pl.BlockSpec((HEADS_PER_GROUP, BLK, 2 * BLK), lambda z, i: (0, 0, 0))],
        out_specs=[cur, cur],
        out_shape=[jax.ShapeDtypeStruct((Z, L, GROUP_W), F32)] * 2,
        compiler_params=_params(("arbitrary", "arbitrary")),
        name="attn_prompt",
    )(q, k, k, v, v, bias)


def _attn_sample_kernel(q_ref, kc_ref, vc_ref, nk_ref, nv_ref, b1_ref, b2_ref,
                        ko_ref, vo_ref, o_ref, l_ref, *, bb, wb):
    lane = lax.broadcasted_iota(jnp.int32, (HEAD_DIM, LANES), 1)
    new_cols = lane >= LANES - 4
    for b in range(bb):
        for hh in range(HEADS_PER_GROUP):
            q = q_ref[b, hh]
            kt = kc_ref[b, hh]
            vt = vc_ref[b, hh]
            nkt = nk_ref[b, hh]
            nvt = nv_ref[b, hh]
            s1 = _dot(q, kt.astype(BF16)) + b1_ref[hh]
            s2 = _dot(q, nkt.astype(BF16)) + b2_ref[hh]
            m = jnp.maximum(jnp.max(s1, axis=-1, keepdims=True), jnp.max(s2, axis=-1, keepdims=True))
            p1 = jnp.exp(s1 - m)
            p2 = jnp.exp(s2 - m)
            z = jnp.sum(p1, axis=-1, keepdims=True) + jnp.sum(p2, axis=-1, keepdims=True)
            o = _dot_nt(p1.astype(BF16), vt.astype(BF16)) + _dot_nt(p2.astype(BF16), nvt.astype(BF16))
            o_ref[b, hh] = o / z
            l_ref[b, hh] = jnp.broadcast_to(m + jnp.log(z), (8, LANES))
            for src, new, dst in ((kt, nkt, ko_ref), (vt, nvt, vo_ref)):
                rolled = pltpu.roll(src, wb - 4, axis=1)
                if wb > LANES:
                    dst[b, hh, :, 0:wb - LANES] = rolled[:, 0:wb - LANES]
                dst[b, hh, :, wb - LANES:wb] = jnp.where(new_cols, new, rolled[:, wb - LANES:wb])


def _attn_sample(q, kc, vc, nk, nv, b1, b2, *, bb):
    nb, _, _, wb = kc.shape
    blk = lambda *tail: pl.BlockSpec((bb, HEADS_PER_GROUP) + tail, lambda i: (i, 0, 0, 0))
    const = lambda shape: pl.BlockSpec(shape, lambda i: (0, 0, 0))
    return pl.pallas_call(
        functools.partial(_attn_sample_kernel, bb=bb, wb=wb),
        grid=(nb // bb,),
        in_specs=[blk(8, HEAD_DIM), blk(HEAD_DIM, wb), blk(HEAD_DIM, wb), blk(HEAD_DIM, LANES),
                  blk(HEAD_DIM, LANES), const((HEADS_PER_GROUP, 8, wb)), const((HEADS_PER_GROUP, 8, LANES))],
        out_specs=[blk(HEAD_DIM, wb), blk(HEAD_DIM, wb), blk(8, HEAD_DIM), blk(8, LANES)],
        out_shape=[jax.ShapeDtypeStruct(kc.shape, F32), jax.ShapeDtypeStruct(kc.shape, F32),
                   jax.ShapeDtypeStruct((nb, HEADS_PER_GROUP, 8, HEAD_DIM), F32),
                   jax.ShapeDtypeStruct((nb, HEADS_PER_GROUP, 8, LANES), F32)],
        compiler_params=_params(("arbitrary",)),
        name="attn_sample",
    )(q, kc, vc, nk, nv, b1, b2)


def _first_max(v, ids, sentinel):
    m = jnp.max(v, axis=0, keepdims=True)
    idx = jnp.min(jnp.where(v == m, ids, sentinel), axis=0, keepdims=True)
    return m, ids == idx


def _oproj_kernel(x_ref, o0, o1, o2, l0, l1, l2, co_ref, g1_ref, sc_ref, sh_ref, ng_ref, wo_ref,
                  wr_ref, br_ref, x1_ref, xg_ref, gt_ref, posr_ref, post_ref, cnt_ref, slab, *, tile, dils):
    o_refs, l_refs = (o0, o1, o2), (l0, l1, l2)
    outs, lses = [], []
    slab_i = 0
    for g, d in enumerate(dils):
        if d == 1:
            outs.append(o_refs[g][0, 0])
            lses.append(l_refs[g][0, 0])
            continue
        n = tile // d
        for ref, dest in ((o_refs[g], outs), (l_refs[g], lses)):
            halves = []
            for half in range(GROUP_W // LANES):
                for r in range(d):
                    slab[slab_i, pl.ds(r, n, stride=d), :] = ref[0, r, :, half * LANES:(half + 1) * LANES]
                halves.append(slab[slab_i])
                slab_i += 1
            dest.append(jnp.concatenate(halves, axis=1))

    m = jnp.maximum(jnp.maximum(lses[0], lses[1]), lses[2])
    es = [jnp.exp(l - m) for l in lses]
    den = es[0] + es[1] + es[2]
    cat = [(outs[g] * (es[g] / den)).astype(BF16) for g in range(N_GROUPS)]
    cat.append(co_ref[0].astype(BF16))
    cat = jnp.concatenate(cat, axis=1)
    x1 = x_ref[0] + g1_ref[0] * _dot(cat, wo_ref[...])
    x1_ref[0] = x1

    ms = jnp.mean(x1 * x1, axis=-1, keepdims=True)
    h2 = x1 * lax.rsqrt(ms + RMS_EPS) * ng_ref[...]
    h2 = h2 * (1.0 + sc_ref[0]) + sh_ref[0]
    hh = h2.astype(BF16)
    xg_ref[0, :, 0:D_MODEL] = hh

    hl = (h2 - hh.astype(F32)).astype(BF16)
    wr = wr_ref[...]
    wh = wr.astype(BF16)
    wl = (wr - wh.astype(F32)).astype(BF16)
    logits = _dot_nt(wh, hh) + _dot_nt(wh, hl) + _dot_nt(wl, hh)
    scores = 1.0 / (1.0 + jnp.exp(-logits))
    sel = scores + br_ref[...]

    ids = lax.broadcasted_iota(jnp.int32, (GROUP_SIZE, tile), 0)
    ninf = -jnp.inf
    sel_g = [sel[g * GROUP_SIZE:(g + 1) * GROUP_SIZE] for g in range(N_ROUTE_GROUPS)]
    gscore = jnp.zeros((N_ROUTE_GROUPS, tile), F32)
    for g in range(N_ROUTE_GROUPS):
        m1, oh = _first_max(sel_g[g], ids, GROUP_SIZE)
        m2 = jnp.max(jnp.where(oh, ninf, sel_g[g]), axis=0, keepdims=True)
        gscore = jnp.where(ids == g, m1 + m2, gscore)
    gsel = jnp.zeros((N_ROUTE_GROUPS, tile), F32)
    for _ in range(TOPK_GROUPS):
        _, oh = _first_max(gscore, ids, N_ROUTE_GROUPS)
        gsel = jnp.where(oh, 1.0, gsel)
        gscore = jnp.where(oh, ninf, gscore)
    cand = [jnp.where(gsel[g:g + 1] > 0.0, sel_g[g], ninf) for g in range(N_ROUTE_GROUPS)]
    chosen = [jnp.zeros((GROUP_SIZE, tile), F32) for _ in range(N_ROUTE_GROUPS)]
    picks = []
    for _ in range(TOP_K):
        mx = cand[0]
        for g in range(1, N_ROUTE_GROUPS):
            mx = jnp.maximum(mx, cand[g])
        mx = jnp.max(mx, axis=0, keepdims=True)
        idx = jnp.where(cand[0] == mx, ids, N_EXPERTS)
        for g in range(1, N_ROUTE_GROUPS):
            idx = jnp.minimum(idx, jnp.where(cand[g] == mx, ids + g * GROUP_SIZE, N_EXPERTS))
        idx = jnp.min(idx, axis=0, keepdims=True)
        ohs = [(ids + g * GROUP_SIZE) == idx for g in range(N_ROUTE_GROUPS)]
        picks.append(ohs)
        for g in range(N_ROUTE_GROUPS):
            chosen[g] = jnp.where(ohs[g], 1.0, chosen[g])
            cand[g] = jnp.where(ohs[g], ninf, cand[g])
    wts = [jnp.where(chosen[g] > 0.0, scores[g * GROUP_SIZE:(g + 1) * GROUP_SIZE], 0.0)
           for g in range(N_ROUTE_GROUPS)]
    wsum = jnp.sum(wts[0], axis=0, keepdims=True)
    for g in range(1, N_ROUTE_GROUPS):
        wsum = wsum + jnp.sum(wts[g], axis=0, keepdims=True)
    gates_t = jnp.concatenate([w / wsum * ROUTED_SCALE for w in wts], axis=0)
    gt_ref[0] = jnp.concatenate([gates_t, jnp.zeros((LANES - N_EXPERTS, tile), F32)], axis=0).T
    g_hi = gates_t.astype(BF16).astype(F32)
    xg_ref[0, :, D_MODEL:D_MODEL + LANES] = jnp.concatenate([g_hi, gates_t - g_hi], axis=0).T.astype(BF16)

    er = lax.broadcasted_iota(jnp.int32, (N_EXPERTS, N_EXPERTS), 0)
    ec = lax.broadcasted_iota(jnp.int32, (N_EXPERTS, N_EXPERTS), 1)
    lower = jnp.where(ec < er, 1.0, 0.0).astype(BF16)
    tr = lax.broadcasted_iota(jnp.int32, (SUB, SUB), 0)
    tc = lax.broadcasted_iota(jnp.int32, (SUB, SUB), 1)
    upper = jnp.where(tr < tc, 1.0, 0.0).astype(BF16)
    for u in range(tile // SUB):
        cols = slice(u * SUB, (u + 1) * SUB)
        ch = jnp.concatenate([c[:, cols] for c in chosen], axis=0)
        cnt = jnp.sum(ch, axis=1, keepdims=True)
        cpad = jnp.broadcast_to(jnp.ceil(cnt / CHUNK) * CHUNK, (N_EXPERTS, LANES))
        cnt_ref[u] = cpad
        seg0 = _dot(lower, cpad.astype(BF16))[:, 0:1]
        slot = seg0 + _dot(ch.astype(BF16), upper)
        rows = []
        for k in range(TOP_K):
            acc = jnp.where(picks[k][0][:, cols], slot[0:GROUP_SIZE], 0.0)
            for g in range(1, N_ROUTE_GROUPS):
                acc = acc + jnp.where(picks[k][g][:, cols], slot[g * GROUP_SIZE:(g + 1) * GROUP_SIZE], 0.0)
            rows.append(jnp.sum(acc, axis=0, keepdims=True))
        posr = jnp.concatenate(rows, axis=0)
        posr_ref[u] = posr
        post_ref[0, cols, :] = jnp.concatenate([posr, jnp.zeros((LANES - TOP_K, SUB), F32)], axis=0).T


def _oproj(x, o_list, l_list, co, g1, sc, sh, ng, wo_bf, wr_t, br, *, tile, dils):
    B, S, _ = x.shape
    tm = g1.shape[1]
    mod_spec = pl.BlockSpec((1, tm, D_MODEL), (lambda b, j: (b, j, 0)) if tm > 1 else (lambda b, j: (b, 0, 0)))
    const2 = lambda shape: pl.BlockSpec(shape, lambda b, j: (0, 0))
    nat = lambda w: pl.BlockSpec((1, tile, w), lambda b, j: (b, j, 0))
    dspec = [pl.BlockSpec((1, d, tile // d, GROUP_W), lambda b, j: (b, 0, j, 0)) for d in dils]
    n_slabs = sum(2 * (GROUP_W // LANES) for d in dils if d > 1)
    nsub_t = tile // SUB
    return pl.pallas_call(
        functools.partial(_oproj_kernel, tile=tile, dils=dils),
        grid=(B, S // tile),
        in_specs=[nat(D_MODEL)] + dspec + dspec + [nat(CONV_CH), mod_spec, mod_spec, mod_spec,
                  const2((1, D_MODEL)), const2((D_MODEL, D_MODEL)), const2((N_EXPERTS, D_MODEL)),
                  const2((N_EXPERTS, 1))],
        out_specs=[nat(D_MODEL), nat(XG_W), nat(LANES),
                   pl.BlockSpec((nsub_t, TOP_K, SUB), lambda b, j: (b * (S // tile) + j, 0, 0)),
                   nat(LANES),
                   pl.BlockSpec((nsub_t, N_EXPERTS, LANES), lambda b, j: (b * (S // tile) + j, 0, 0))],
        out_shape=[jax.ShapeDtypeStruct((B, S, D_MODEL), F32), jax.ShapeDtypeStruct((B, S, XG_W), BF16),
                   jax.ShapeDtypeStruct((B, S, LANES), F32),
                   jax.ShapeDtypeStruct((B * S // SUB, TOP_K, SUB), F32),
                   jax.ShapeDtypeStruct((B, S, LANES), F32),
                   jax.ShapeDtypeStruct((B * S // SUB, N_EXPERTS, LANES), F32)],
        scratch_shapes=[pltpu.VMEM((max(n_slabs, 1), tile, LANES), F32)],
        compiler_params=_params(("arbitrary", "arbitrary")),
        name="oproj",
    )(x, *o_list, *l_list, co, g1, sc, sh, ng, wo_bf, wr_t, br)


def _moe_kernel(h_ref, gt_ref, x1_ref, g2_ref, wg_ref, wu_ref, wd_ref, wgs_ref, wus_ref, wds_ref,
                y_ref, acc, *, tile):
    e = pl.program_id(2)
    h = h_ref[0]

    @pl.when(e == 0)
    def _():
        a = _dot(h, wgs_ref[...].astype(BF16))
        b = _dot(h, wus_ref[...].astype(BF16))
        acc[...] = _dot((_silu(a) * b).astype(BF16), wds_ref[...].astype(BF16))

    a = _dot(h, wg_ref[0].astype(BF16))
    b = _dot(h, wu_ref[0].astype(BF16))
    lane = lax.broadcasted_iota(jnp.int32, (tile, LANES), 1)
    gcol = jnp.sum(jnp.where(lane == e, gt_ref[0], 0.0), axis=1, keepdims=True)
    hc = _silu(a) * b * gcol
    acc[...] += _dot(hc.astype(BF16), wd_ref[0].astype(BF16))

    @pl.when(e == N_EXPERTS - 1)
    def _():
        y_ref[0] = x1_ref[0] + g2_ref[0] * acc[...]


def _moe(h2, gates, x1, g2, wg, wu, wd, wgs, wus, wds, *, tile):
    B, S, _ = h2.shape
    tm = g2.shape[1]
    mod_spec = pl.BlockSpec((1, tm, D_MODEL),
                            (lambda b, j, e: (b, j, 0)) if tm > 1 else (lambda b, j, e: (b, 0, 0)))
    nat = lambda w: pl.BlockSpec((1, tile, w), lambda b, j, e: (b, j, 0))
    const2 = lambda shape: pl.BlockSpec(shape, lambda b, j, e: (0, 0))
    return pl.pallas_call(
        functools.partial(_moe_kernel, tile=tile),
        grid=(B, S // tile, N_EXPERTS),
        in_specs=[nat(D_MODEL), nat(LANES), nat(D_MODEL), mod_spec,
                  pl.BlockSpec((1, D_MODEL, D_EXPERT), lambda b, j, e: (e, 0, 0)),
                  pl.BlockSpec((1, D_MODEL, D_EXPERT), lambda b, j, e: (e, 0, 0)),
                  pl.BlockSpec((1, D_EXPERT, D_MODEL), lambda b, j, e: (e, 0, 0)),
                  const2((D_MODEL, D_EXPERT)), const2((D_MODEL, D_EXPERT)), const2((D_EXPERT, D_MODEL))],
        out_specs=nat(D_MODEL),
        out_shape=jax.ShapeDtypeStruct((B, S, D_MODEL), F32),
        scratch_shapes=[pltpu.VMEM((tile, D_MODEL), F32)],
        compiler_params=_params(("arbitrary", "arbitrary", "arbitrary")),
        name="moe",
    )(h2, gates, x1, g2, wg, wu, wd, wgs, wus, wds)


FFN_TM = 512
SUB_ROWS = SUB * TOP_K + N_EXPERTS * CHUNK
N_CHUNKS = SUB_ROWS // CHUNK
ZERO_CHUNK = N_CHUNKS
TABLE_W = 256
MXU_ROWS = 256
BLOCKS_PER_TRIP = 2


def _slot_onehot(first, pos_list, axis_iota):
    shape = axis_iota.shape
    hit = axis_iota == jnp.broadcast_to(pos_list[0] - first, shape).astype(BF16)
    for p in pos_list[1:]:
        hit = hit | (axis_iota == jnp.broadcast_to(p - first, shape).astype(BF16))
    return jnp.where(hit, jnp.ones(shape, BF16), jnp.zeros(shape, BF16))


def _chunk_copy(src, src_chunk, dst, dst_chunk, sem):
    return pltpu.make_async_copy(src.at[src_chunk], dst.at[dst_chunk], sem)


def _dispatch_kernel(nblk_ref, ncopy_ref, ntile_ref, src_ref, dst_ref, posr_ref, xg_ref, xs_hbm, xs_scr,
                     zero_scr, sem, tail_sem, *, nsub, n_tiles_max):
    s = pl.program_id(0)

    tile_chunks = FFN_TM // CHUNK
    blk_chunks = MXU_ROWS // CHUNK
    zero_scr[...] = jnp.zeros((tile_chunks, CHUNK, XG_W), BF16)
    tail_blocks = [ntile_ref[0] + s + r * nsub for r in range(-(-(n_tiles_max) // nsub))]

    def tail_copy(t):
        return pltpu.make_async_copy(
            zero_scr, xs_hbm.at[pl.ds(pl.multiple_of(t * tile_chunks, tile_chunks), tile_chunks)], tail_sem)

    for t in tail_blocks:
        @pl.when(t < n_tiles_max)
        def _():
            tail_copy(t).start()

    xg = xg_ref[...]
    pos = [posr_ref[0, k:k + 1, :] for k in range(TOP_K)]
    rows = lax.broadcasted_iota(jnp.int32, (MXU_ROWS, SUB), 0).astype(BF16)
    xs_scr[ZERO_CHUNK] = jnp.zeros((CHUNK, XG_W), BF16)

    def sort_blocks(i, carry):
        for j in range(BLOCKS_PER_TRIP):
            blk = i * BLOCKS_PER_TRIP + j
            onehot = _slot_onehot((blk * MXU_ROWS).astype(F32), pos, rows)
            sorted_rows = _dot(onehot, xg).astype(BF16)
            xs_scr[pl.ds(pl.multiple_of(blk * blk_chunks, blk_chunks), blk_chunks)] = (
                sorted_rows.reshape(blk_chunks, CHUNK, XG_W))
        return carry

    lax.fori_loop(0, nblk_ref[s], sort_blocks, 0)
    n = ncopy_ref[s]

    def start(i, carry):
        _chunk_copy(xs_scr, src_ref[0, 0, i], xs_hbm, dst_ref[0, 0, i], sem).start()
        return carry

    def wait(i, carry):
        _chunk_copy(xs_scr, 0, xs_hbm, 0, sem).wait()
        return carry

    lax.fori_loop(0, n, start, 0)
    lax.fori_loop(0, n, wait, 0)
    for t in tail_blocks:
        @pl.when(t < n_tiles_max)
        def _():
            tail_copy(t).wait()


def _dispatch(nblk, ncopy, n_tiles, src, dst, posr, xg, n_rows):
    nsub = posr.shape[0]
    tab = pl.BlockSpec((1, 1, TABLE_W), lambda s, *_: (s, 0, 0), memory_space=pltpu.SMEM)
    return pl.pallas_call(
        functools.partial(_dispatch_kernel, nsub=nsub, n_tiles_max=n_rows // FFN_TM),
        grid_spec=pltpu.PrefetchScalarGridSpec(
            num_scalar_prefetch=3, grid=(nsub,),
            in_specs=[tab, tab, pl.BlockSpec((1, TOP_K, SUB), lambda s, *_: (s, 0, 0)),
                      pl.BlockSpec((SUB, XG_W), lambda s, *_: (s, 0))],
            out_specs=pl.BlockSpec(memory_space=pl.ANY),
            scratch_shapes=[pltpu.VMEM((N_CHUNKS + 1, CHUNK, XG_W), BF16),
                            pltpu.VMEM((FFN_TM // CHUNK, CHUNK, XG_W), BF16),
                            pltpu.SemaphoreType.DMA(()), pltpu.SemaphoreType.DMA(())]),
        out_shape=jax.ShapeDtypeStruct((n_rows // CHUNK, CHUNK, XG_W), BF16),
        compiler_params=_params(("arbitrary",)),
        name="moe_dispatch",
    )(nblk, ncopy, n_tiles, src, dst, posr, xg).reshape(n_rows, XG_W)


def _ffn_kernel(te_ref, nt_ref, xs_ref, wg_ref, wu_ref, wd_ref, ys_ref, wg_s, wu_s, wd_s):
    i = pl.program_id(0)

    @pl.when(i >= nt_ref[0])
    def _():
        ys_ref[...] = jnp.zeros((FFN_TM, D_MODEL), BF16)

    @pl.when(i < nt_ref[0])
    def _():
        e = te_ref[i]

        @pl.when((i == 0) | (e != te_ref[jnp.maximum(i - 1, 0)]))
        def _():
            wg_s[...] = wg_ref[0].astype(BF16)
            wu_s[...] = wu_ref[0].astype(BF16)
            wd_s[...] = wd_ref[0].astype(BF16)

        x = xs_ref[:, 0:D_MODEL]
        g = xs_ref[:, D_MODEL:XG_W].astype(F32)
        lane = lax.broadcasted_iota(jnp.int32, (FFN_TM, LANES), 1)
        gcol = jnp.sum(jnp.where((lane == e) | (lane == e + N_EXPERTS), g, 0.0), axis=1, keepdims=True)
        a = _dot(x, wg_s[...])
        b = _dot(x, wu_s[...])
        hc = _silu(a) * b * gcol
        ys_ref[...] = _dot(hc.astype(BF16), wd_s[...]).astype(BF16)


def _ffn(tile_expert, n_tiles, xs, wg, wu, wd):
    n_rows = xs.shape[0]
    row = lambda i, te, nt: (jnp.minimum(i, nt[0] - 1), 0)
    wspec = lambda shape: pl.BlockSpec((1,) + shape, lambda i, te, nt: (te[i], 0, 0))
    return pl.pallas_call(
        _ffn_kernel,
        grid_spec=pltpu.PrefetchScalarGridSpec(
            num_scalar_prefetch=2, grid=(n_rows // FFN_TM,),
            in_specs=[pl.BlockSpec((FFN_TM, XG_W), row), wspec((D_MODEL, D_EXPERT)),
                      wspec((D_MODEL, D_EXPERT)), wspec((D_EXPERT, D_MODEL))],
            out_specs=pl.BlockSpec((FFN_TM, D_MODEL), lambda i, te, nt: (i, 0)),
            scratch_shapes=[pltpu.VMEM((D_MODEL, D_EXPERT), BF16), pltpu.VMEM((D_MODEL, D_EXPERT), BF16),
                            pltpu.VMEM((D_EXPERT, D_MODEL), BF16)]),
        out_shape=jax.ShapeDtypeStruct((n_rows, D_MODEL), BF16),
        compiler_params=_params(("arbitrary",)),
        name="moe_ffn",
    )(tile_expert, n_tiles, xs, wg, wu, wd)


def _combine_kernel(nblk_ref, ncopy_ref, src_ref, post_ref, xg_ref, x1_ref, g2_ref, wgs_ref, wus_ref,
                    wds_ref, ys_hbm, y_ref, ys_scr, wgs_s, wus_s, wds_s, sem):
    s = pl.program_id(0)
    n = ncopy_ref[s]

    def start(i, carry):
        _chunk_copy(ys_hbm, src_ref[0, 0, i], ys_scr, i, sem).start()
        return carry

    def wait(i, carry):
        _chunk_copy(ys_hbm, 0, ys_scr, 0, sem).wait()
        return carry

    lax.fori_loop(0, n, start, 0)

    @pl.when(s == 0)
    def _():
        wgs_s[...] = wgs_ref[...].astype(BF16)
        wus_s[...] = wus_ref[...].astype(BF16)
        wds_s[...] = wds_ref[...].astype(BF16)

    h = xg_ref[:, 0:D_MODEL]
    a = _dot(h, wgs_s[...])
    b = _dot(h, wus_s[...])
    shared = _dot((_silu(a) * b).astype(BF16), wds_s[...])

    post = post_ref[...]
    pos = [jnp.broadcast_to(post[:, k:k + 1], (SUB, MXU_ROWS)) for k in range(TOP_K)]
    lanes = lax.broadcasted_iota(jnp.int32, (SUB, MXU_ROWS), 1).astype(BF16)
    lax.fori_loop(0, n, wait, 0)

    blk_chunks = MXU_ROWS // CHUNK

    def gather_blocks(i, acc):
        for j in range(BLOCKS_PER_TRIP):
            blk = i * BLOCKS_PER_TRIP + j
            onehot = _slot_onehot((blk * MXU_ROWS).astype(F32), pos, lanes)
            rows = ys_scr[pl.ds(pl.multiple_of(blk * blk_chunks, blk_chunks), blk_chunks)]
            acc = acc + _dot(onehot, rows.reshape(MXU_ROWS, D_MODEL))
        return acc

    routed = lax.fori_loop(0, nblk_ref[s], gather_blocks, jnp.zeros((SUB, D_MODEL), F32))
    y_ref[...] = x1_ref[...] + g2_ref[0] * (routed + shared)


def _combine(nblk, ncopy, src, post, xg, x1, g2, wgs, wus, wds, ys):
    n_tok = x1.shape[0]
    nsub = n_tok // SUB
    per_b = nsub // g2.shape[0]
    tab = pl.BlockSpec((1, 1, TABLE_W), lambda s, *_: (s, 0, 0), memory_space=pltpu.SMEM)
    const2 = lambda shape: pl.BlockSpec(shape, lambda s, *_: (0, 0))
    return pl.pallas_call(
        _combine_kernel,
        grid_spec=pltpu.PrefetchScalarGridSpec(
            num_scalar_prefetch=2, grid=(nsub,),
            in_specs=[tab, pl.BlockSpec((SUB, LANES), lambda s, *_: (s, 0)),
                      pl.BlockSpec((SUB, XG_W), lambda s, *_: (s, 0)),
                      pl.BlockSpec((SUB, D_MODEL), lambda s, *_: (s, 0)),
                      pl.BlockSpec((1, 1, D_MODEL), lambda s, *_: (s // per_b, 0, 0)),
                      const2((D_MODEL, D_EXPERT)), const2((D_MODEL, D_EXPERT)), const2((D_EXPERT, D_MODEL)),
                      pl.BlockSpec(memory_space=pl.ANY)],
            out_specs=pl.BlockSpec((SUB, D_MODEL), lambda s, *_: (s, 0)),
            scratch_shapes=[pltpu.VMEM((N_CHUNKS, CHUNK, D_MODEL), BF16), pltpu.VMEM((D_MODEL, D_EXPERT), BF16),
                            pltpu.VMEM((D_MODEL, D_EXPERT), BF16), pltpu.VMEM((D_EXPERT, D_MODEL), BF16),
                            pltpu.SemaphoreType.DMA(())]),
        out_shape=jax.ShapeDtypeStruct((n_tok, D_MODEL), F32),
        compiler_params=_params(("arbitrary",)),
        name="moe_combine",
    )(nblk, ncopy, src, post, xg, x1, g2, wgs, wus, wds, ys.reshape(-1, CHUNK, D_MODEL))


def _route_tables(cnt):
    nsub = cnt.shape[0]
    i32 = jnp.int32
    seg_end = jnp.cumsum(cnt, axis=1)
    seg_start = seg_end - cnt
    tot = jnp.sum(cnt, axis=0)
    tot_al = (tot + FFN_TM - 1) // FFN_TM * FFN_TM
    reg_end = jnp.cumsum(tot_al)
    reg_start = reg_end - tot_al
    base = reg_start[None, :] + jnp.cumsum(cnt, axis=0) - cnt
    n_real = seg_end[:, -1] // CHUNK
    k = jnp.arange(TABLE_W, dtype=i32)
    owner = jnp.sum((seg_end[:, None, :] // CHUNK) <= k[None, :, None], axis=2)
    owner = jnp.minimum(owner, N_EXPERTS - 1)
    delta = (base - seg_start) // CHUNK
    e_ids = jnp.arange(N_EXPERTS, dtype=i32)
    real_dst = jnp.sum(jnp.where(owner[:, :, None] == e_ids, delta[:, None, :], 0), axis=2) + k[None, :]
    fill_n = (tot_al - tot) // CHUNK
    fill_dst0 = (reg_start + tot) // CHUNK
    src = jnp.broadcast_to(k[None, :], (nsub, TABLE_W))
    dst = real_dst
    ncopy = n_real
    for r in range(-(-N_EXPERTS // nsub)):
        e_of = jnp.arange(nsub, dtype=i32) + r * nsub
        ok = e_of < N_EXPERTS
        e_cl = jnp.minimum(e_of, N_EXPERTS - 1)
        fn = jnp.where(ok, fill_n[e_cl], 0)
        j = k[None, :] - ncopy[:, None]
        is_fill = (j >= 0) & (j < fn[:, None])
        src = jnp.where(is_fill, ZERO_CHUNK, src)
        dst = jnp.where(is_fill, fill_dst0[e_cl][:, None] + j, dst)
        ncopy = ncopy + fn
    trip_rows = BLOCKS_PER_TRIP * MXU_ROWS
    disp_nblk = (n_real * CHUNK + trip_rows - 1) // trip_rows
    comb_n = disp_nblk * (trip_rows // CHUNK)
    comb_src = jnp.where(k[None, :] < n_real[:, None], real_dst, real_dst[:, 0:1])
    n_tiles = reg_end[-1] // FFN_TM
    t = jnp.arange((nsub * SUB_ROWS + N_EXPERTS * FFN_TM) // FFN_TM, dtype=i32)
    tile_expert = jnp.minimum(jnp.sum((reg_end[None, :] // FFN_TM) <= t[:, None], axis=1), N_EXPERTS - 1)
    as3 = lambda a: a.astype(i32).reshape(nsub, 1, TABLE_W)
    return (disp_nblk.astype(i32), ncopy.astype(i32), as3(src), as3(dst), comb_n.astype(i32), as3(comb_src),
            tile_expert.astype(i32), n_tiles.astype(i32).reshape(1))


def _moe_sparse(xg, post, posr, cnt, x1, g2, wg, wu, wd, wgs, wus, wds):
    B, S, _ = x1.shape
    n_tok = B * S
    nsub = n_tok // SUB
    (disp_nblk, disp_n, disp_src, disp_dst, comb_n, comb_src, tile_expert, n_tiles) = _route_tables(
        cnt[:, :, 0].astype(jnp.int32))
    n_rows = nsub * SUB_ROWS + N_EXPERTS * FFN_TM
    xg2 = xg.reshape(n_tok, XG_W)
    xs = _dispatch(disp_nblk, disp_n, n_tiles, disp_src, disp_dst, posr, xg2, n_rows)
    ys = _ffn(tile_expert, n_tiles, xs, wg, wu, wd)
    y = _combine(disp_nblk, comb_n, comb_src, post.reshape(n_tok, LANES), xg2, x1.reshape(n_tok, D_MODEL),
                 g2, wgs, wus, wds, ys)
    return y.reshape(B, S, D_MODEL)


def _t5_bucket(dist):
    max_exact = N_BUCKETS // 2
    df = jnp.maximum(dist, 1).astype(F32)
    large = max_exact + (jnp.log(df / max_exact) / math.log(MAX_DISTANCE / max_exact)
                         * (N_BUCKETS - max_exact)).astype(jnp.int32)
    large = jnp.minimum(large, N_BUCKETS - 1)
    return jnp.where(dist < max_exact, dist, large)


def _step_bias(rel_bias, g):
    dist = jnp.arange(N_KEYS, dtype=jnp.int32) * DILATIONS[g]
    cols = rel_bias[:, g * HEADS_PER_GROUP:(g + 1) * HEADS_PER_GROUP]
    return cols[_t5_bucket(dist)].T.astype(F32)


def _prompt_bias(bias_k):
    h = bias_k.shape[0]
    n = 3 * BLK
    row = jnp.concatenate([bias_k[:, ::-1], jnp.full((h, n - N_KEYS), NEG, F32)], axis=1)
    t = jnp.tile(row, (1, BLK))[:, :BLK * (n - 1)].reshape(h, BLK, n - 1)
    return t[:, :, :2 * BLK]


def _sample_bias(bias_k, wb, d):
    h = bias_k.shape[0]
    rev = bias_k[:, :0:-1]
    rows = []
    for t in range(4):
        if d == 1:
            rows.append(jnp.concatenate([jnp.full((h, t), NEG, F32), rev[:, :wb - t]], axis=1))
        else:
            r = jnp.arange(d)[None, None, :]
            rows.append(jnp.where(r == t, rev[:, :, None], NEG).reshape(h, wb))
    b1 = jnp.concatenate([jnp.stack(rows, axis=1), jnp.zeros((h, 4, wb), F32)], axis=1)
    rows = []
    for t in range(4):
        cols = [bias_k[:, (t - tn) // d] if (t >= tn and (t - tn) % d == 0) else jnp.full((h,), NEG, F32)
                for tn in range(4)]
        rows.append(jnp.concatenate([jnp.full((h, LANES - 4), NEG, F32), jnp.stack(cols, axis=1)], axis=1))
    b2 = jnp.concatenate([jnp.stack(rows, axis=1), jnp.zeros((h, 4, LANES), F32)], axis=1)
    return b1, b2


def _layer(x, mods, hist, caches, rel_bias, weights, *, sample):
    (norm1_g, norm2_g, w_in_bf, qg, kg, conv_w, bd, w_o_bf, w_r_t, b_r, w_gate_e, w_up_e, w_down_e,
     w_gate_s, w_up_s, w_down_s) = weights
    sh1, sc1, g1, sh2, sc2, g2 = mods
    B, S, _ = x.shape
    tile = 512
    dils = (1, 1, 1) if sample else DILATIONS
    res = _proj(x, sc1, sh1, norm1_g, w_in_bf, qg, kg, conv_w, bd, hist, tile=tile, dils=dils, sample=sample)
    states = []
    o_list, l_list = [], []
    if sample:
        qf, kf, vf, u, co = res
        nb = S // 4

        def heads_t(a, g):
            a = a[0, :, g * GROUP_W:(g + 1) * GROUP_W].reshape(nb, 4, HEADS_PER_GROUP, HEAD_DIM)
            return a.transpose(0, 2, 1, 3)

        for g in range(N_GROUPS):
            d, wb = DILATIONS[g], WINDOWS[g]
            kc, vc = caches[g]
            bias_k = _step_bias(rel_bias, g)
            b1, b2 = _sample_bias(bias_k, wb, d)
            q = jnp.pad(heads_t(qf, g), ((0, 0), (0, 0), (0, 4), (0, 0))).astype(BF16)
            new_t = lambda a: jnp.pad(heads_t(a, g).transpose(0, 1, 3, 2),
                                      ((0, 0), (0, 0), (0, 0), (LANES - 4, 0)))
            ko, vo, o, lse = _attn_sample(q, kc, vc, new_t(kf), new_t(vf), b1, b2,
                                          bb={128: 8, 512: 4, 2048: 1}[wb])
            states += [ko.transpose(0, 3, 1, 2)[None], vo.transpose(0, 3, 1, 2)[None]]
            o = o[:, :, :4].transpose(0, 2, 1, 3).reshape(1, 1, S, GROUP_W)
            lse = jnp.broadcast_to(lse[:, :, :4, :1], (nb, HEADS_PER_GROUP, 4, HEAD_DIM))
            o_list.append(o)
            l_list.append(lse.transpose(0, 2, 1, 3).reshape(1, 1, S, GROUP_W))
        states.append(u.reshape(nb, 4, CONV_CH)[:, 2:][None])
    else:
        qkv, (kf, vf, u, co) = res[:9], res[9:]
        for g in range(N_GROUPS):
            d, w = DILATIONS[g], WINDOWS[g]
            q, k, v = (a.reshape(B * d, S // d, GROUP_W) for a in qkv[3 * g:3 * g + 3])
            o, lse = _attn_prompt(q, k, v, _prompt_bias(_step_bias(rel_bias, g)))
            o_list.append(o.reshape(B, d, S // d, GROUP_W))
            l_list.append(lse.reshape(B, d, S // d, GROUP_W))
            keep = lambda a: a[:, S - w:, g * GROUP_W:(g + 1) * GROUP_W].reshape(
                B, w, HEADS_PER_GROUP, HEAD_DIM)[None]
            states += [keep(kf), keep(vf)]
        states.append(u[:, S - 2:][None])
    x1, xg, gates, posr, post, cnt = _oproj(x, o_list, l_list, co, g1, sc2, sh2, norm2_g, w_o_bf, w_r_t, b_r,
                                            tile=tile, dils=dils)
    if sample:
        y = _moe(xg, gates, x1, g2, w_gate_e, w_up_e, w_down_e, w_gate_s, w_up_s, w_down_s, tile=tile)
    else:
        y = _moe_sparse(xg, post, posr, cnt, x1, g2, w_gate_e, w_up_e, w_down_e, w_gate_s, w_up_s, w_down_s)
    return y, states


def kernel(x_prompt, x_sample, c_prompt, c_sample, cache_k_w128, cache_v_w128, cache_k_w512, cache_v_w512, cache_k_w2048, cache_v_w2048, state_conv, rel_bias, norm1_g, norm2_g, w_ada, b_ada, w_in, q_norm_g, k_norm_g, conv_w, w_o, w_router, b_router, w_gate_e, w_up_e, w_down_e, w_gate_s, w_up_s, w_down_s):
    B = x_prompt.shape[0]
    DB, T = x_sample.shape[:2]
    n_c = B + DB
    c_all = jnp.pad(jnp.concatenate([c_prompt, c_sample], axis=0), ((0, (-n_c) % 8), (0, 0)))
    mod = _ada(c_all, w_ada[0], b_ada)
    chunks = [mod[:, i * D_MODEL:(i + 1) * D_MODEL] for i in range(6)]
    mods_p = [c[:B].reshape(B, 1, D_MODEL) for c in chunks]
    mods_s = [jnp.repeat(c[B:n_c], T, axis=0).reshape(1, DB * T, D_MODEL) for c in chunks]

    eye = jnp.arange(GROUP_W) // HEAD_DIM
    bd = jnp.where(eye[:, None] == eye[None, :], 1.0 / HEAD_DIM, 0.0).astype(BF16)
    weights = (norm1_g, norm2_g, w_in[0].astype(BF16),
               jnp.tile(q_norm_g, (1, HEADS_PER_GROUP)), jnp.tile(k_norm_g, (1, HEADS_PER_GROUP)),
               conv_w[0], bd, w_o[0].astype(BF16), w_router[0].T, b_router.reshape(N_EXPERTS, 1),
               w_gate_e[0], w_up_e[0], w_down_e[0], w_gate_s[0], w_up_s[0], w_down_s[0])

    yp, st_p = _layer(x_prompt, mods_p, None, None, rel_bias, weights, sample=False)

    s0, s1 = state_conv[0, :, 0], state_conv[0, :, 1]
    zero = jnp.zeros_like(s0)
    hist_a = jnp.stack([s0, s1, zero, zero], axis=1).reshape(DB * T, CONV_CH)
    hist_b = jnp.stack([s1, zero, zero, zero], axis=1).reshape(DB * T, CONV_CH)
    caches = [(ck[0].transpose(0, 2, 3, 1), cv[0].transpose(0, 2, 3, 1))
              for ck, cv in ((cache_k_w128, cache_v_w128), (cache_k_w512, cache_v_w512),
                             (cache_k_w2048, cache_v_w2048))]
    ys, st_s = _layer(x_sample.reshape(1, DB * T, D_MODEL), mods_s, (hist_a, hist_b), caches, rel_bias,
                      weights, sample=True)
    return (yp, ys.reshape(DB, T, D_MODEL), *st_p, *st_s)
```

```python
import functools
import math

import jax
import jax.numpy as jnp
from jax import lax
from jax.experimental import pallas as pl
from jax.experimental.pallas import tpu as pltpu

D_MODEL = 1024
HEAD_DIM = 64
HEADS_PER_GROUP = 4
GROUP_W = HEADS_PER_GROUP * HEAD_DIM
WINDOWS = (128, 512, 2048)
DILATIONS = (1, 4, 16)
N_GROUPS = 3
ATTN_W = N_GROUPS * GROUP_W
CONV_CH = 256
N_PROJ = 3 * ATTN_W + 3 * CONV_CH
N_STEPS = 128
N_KEYS = N_STEPS + 1
BLK = 128
N_BUCKETS = 32
MAX_DISTANCE = WINDOWS[-1]
N_EXPERTS = 64
TOP_K = 8
N_ROUTE_GROUPS = 8
GROUP_SIZE = N_EXPERTS // N_ROUTE_GROUPS
TOPK_GROUPS = 4
D_EXPERT = 256
ROUTED_SCALE = 2.5
RMS_EPS = 1e-6
NEG = -1e30
LANES = 128
SUB = 256
CHUNK = 16
XG_W = D_MODEL + LANES
VMEM_LIMIT = 56 * 1024 * 1024

F32 = jnp.float32
BF16 = jnp.bfloat16


def _dot(a, b):
    return jnp.dot(a, b, preferred_element_type=F32)


def _dot_nt(a, b):
    return lax.dot_general(a, b, (((1,), (1,)), ((), ())), preferred_element_type=F32)


def _silu(a):
    return a / (1.0 + jnp.exp(-a))


def _params(sem):
    return pltpu.CompilerParams(dimension_semantics=sem, vmem_limit_bytes=VMEM_LIMIT)


def _ada_kernel(c_ref, w_ref, b_ref, o_ref):
    s = _silu(c_ref[...])
    o_ref[...] = _dot(s.astype(BF16), w_ref[...].astype(BF16)) + b_ref[...]


def _ada(c, w_ada, b_ada):
    n = c.shape[0]
    nc = w_ada.shape[1] // D_MODEL
    return pl.pallas_call(
        _ada_kernel,
        grid=(nc,),
        in_specs=[pl.BlockSpec((n, D_MODEL), lambda j: (0, 0)),
                  pl.BlockSpec((D_MODEL, D_MODEL), lambda j: (0, j)),
                  pl.BlockSpec((1, D_MODEL), lambda j: (0, j))],
        out_specs=pl.BlockSpec((n, D_MODEL), lambda j: (0, j)),
        out_shape=jax.ShapeDtypeStruct((n, w_ada.shape[1]), F32),
        compiler_params=_params(("arbitrary",)),
        name="ada",
    )(c, w_ada, b_ada)


def _proj_kernel(*refs, tile, dils, sample, nt):
    if sample:
        (x_ref, sc_ref, sh_ref, ng_ref, w_ref, qg_ref, kg_ref, cw_ref, bd_ref, ha_ref, hb_ref,
         qf_ref, kt_ref, vt_ref, u_ref, co_ref, u_scr) = refs
    else:
        (x_ref, sc_ref, sh_ref, ng_ref, w_ref, qg_ref, kg_ref, cw_ref, bd_ref,
         q0, k0, v0, q1, k1, v1, q2, k2, v2, kt0, vt0, kt1, vt1, kt2, vt2, u_ref, co_ref, slab, u_scr) = refs
        qkv_out = ((q0, k0, v0), (q1, k1, v1), (q2, k2, v2))
        tails = ((kt0, vt0), (kt1, vt1), (kt2, vt2))
        j = pl.program_id(1)

    x = x_ref[0]
    ms = jnp.mean(x * x, axis=-1, keepdims=True)
    h = x * lax.rsqrt(ms + RMS_EPS) * ng_ref[...]
    h = h * (1.0 + sc_ref[0]) + sh_ref[0]
    proj = _dot(h.astype(BF16), w_ref[...])

    bd = bd_ref[...]

    def headnorm(z, g):
        zz = z * z
        hi = zz.astype(BF16)
        lo = (zz - hi.astype(F32)).astype(BF16)
        msq = _dot(hi, bd) + _dot(lo, bd)
        return z * lax.rsqrt(msq + RMS_EPS) * g

    slab_i = 0
    for g in range(N_GROUPS):
        c0 = g * GROUP_W
        qn = headnorm(proj[:, c0:c0 + GROUP_W], qg_ref[...]) * (HEAD_DIM ** -0.5)
        kn = headnorm(proj[:, ATTN_W + c0:ATTN_W + c0 + GROUP_W], kg_ref[...])
        vv = proj[:, 2 * ATTN_W + c0:2 * ATTN_W + c0 + GROUP_W]
        if sample:
            qf_ref[0, :, c0:c0 + GROUP_W] = qn
            kt_ref[c0:c0 + GROUP_W, :] = kn.T
            vt_ref[c0:c0 + GROUP_W, :] = vv.T
            continue
        keep = min(WINDOWS[g], tile)

        @pl.when(j >= nt - max(WINDOWS[g] // tile, 1))
        def _(kn=kn, vv=vv, g=g, keep=keep):
            tails[g][0][0] = kn[tile - keep:, :].T
            tails[g][1][0] = vv[tile - keep:, :].T

        d = dils[g]
        for val, out in zip((qn, kn, vv), qkv_out[g]):
            if d == 1:
                out[0, 0] = val.astype(BF16)
                continue
            n = tile // d
            for half in range(GROUP_W // LANES):
                slab[slab_i] = val[:, half * LANES:(half + 1) * LANES]
                for r in range(d):
                    out[0, r, :, half * LANES:(half + 1) * LANES] = (
                        slab[slab_i, pl.ds(r, n, stride=d), :].astype(BF16))
                slab_i += 1

    base = 3 * ATTN_W
    u = proj[:, base + 2 * CONV_CH:base + 3 * CONV_CH] * proj[:, base:base + CONV_CH]
    gate_b = proj[:, base + CONV_CH:base + 2 * CONV_CH]
    if sample:
        u_scr[0:8, :] = jnp.zeros((8, CONV_CH), F32)
    else:
        @pl.when(j == 0)
        def _():
            u_scr[0:8, :] = jnp.zeros((8, CONV_CH), F32)

        @pl.when(j > 0)
        def _():
            u_scr[0:8, :] = u_scr[tile:tile + 8, :]

    u_scr[8:tile + 8, :] = u
    um1 = u_scr[7:tile + 7, :]
    um2 = u_scr[6:tile + 6, :]
    if sample:
        t = lax.broadcasted_iota(jnp.int32, (tile, CONV_CH), 0) % 4
        um1 = jnp.where(t >= 1, um1, 0.0) + hb_ref[...]
        um2 = jnp.where(t >= 2, um2, 0.0) + ha_ref[...]
    cw = cw_ref[...]
    conv = cw[0:1] * um2 + cw[1:2] * um1 + cw[2:3] * u
    u_ref[0] = u if sample else u[tile - 8:, :]
    co_ref[0] = (gate_b * conv).astype(BF16)


def _proj(x, sc, sh, ng, w_bf, qg, kg, cw, bd, hist, *, tile, dils, sample):
    B, S, _ = x.shape
    nt = S // tile
    tm = sc.shape[1]
    mod_spec = pl.BlockSpec((1, tm, D_MODEL), (lambda b, j: (b, j, 0)) if tm > 1 else (lambda b, j: (b, 0, 0)))
    const2 = lambda shape: pl.BlockSpec(shape, lambda b, j: (0, 0))
    in_specs = [pl.BlockSpec((1, tile, D_MODEL), lambda b, j: (b, j, 0)), mod_spec, mod_spec,
                const2((1, D_MODEL)), const2((D_MODEL, N_PROJ)), const2((1, GROUP_W)),
                const2((1, GROUP_W)), const2((3, CONV_CH)), const2((GROUP_W, GROUP_W))]
    args = [x, sc, sh, ng, w_bf, qg, kg, cw, bd]
    nat = lambda w: pl.BlockSpec((1, tile, w), lambda b, j: (b, j, 0))
    out_specs, out_shape = [], []
    scratch = []
    if sample:
        assert B == 1 and nt == 1
        in_specs += [pl.BlockSpec((tile, CONV_CH), lambda b, j: (j, 0))] * 2
        args += list(hist)
        out_specs += [nat(ATTN_W)] + [pl.BlockSpec((ATTN_W, tile), lambda b, j: (0, j))] * 2 + [nat(CONV_CH)]
        out_shape += [jax.ShapeDtypeStruct((B, S, ATTN_W), F32)]
        out_shape += [jax.ShapeDtypeStruct((ATTN_W, S), F32)] * 2
        out_shape += [jax.ShapeDtypeStruct((B, S, CONV_CH), F32)]
    else:
        for d in dils:
            for _ in range(3):
                out_specs.append(pl.BlockSpec((1, d, tile // d, GROUP_W), lambda b, j: (b, 0, j, 0)))
                out_shape.append(jax.ShapeDtypeStruct((B, d, S // d, GROUP_W), BF16))
        for w in WINDOWS:
            keep, first = min(w, tile), nt - max(w // tile, 1)
            for _ in range(2):
                out_specs.append(pl.BlockSpec((1, GROUP_W, keep),
                                              lambda b, j, first=first: (b, 0, jnp.maximum(j - first, 0))))
                out_shape.append(jax.ShapeDtypeStruct((B, GROUP_W, w), F32))
        out_specs.append(pl.BlockSpec((1, 8, CONV_CH), lambda b, j: (b, 0, 0)))
        out_shape.append(jax.ShapeDtypeStruct((B, 8, CONV_CH), F32))
        n_slabs = sum(3 * (GROUP_W // LANES) for d in dils if d > 1)
        scratch.append(pltpu.VMEM((n_slabs, tile, LANES), F32))
    out_specs.append(nat(CONV_CH))
    out_shape.append(jax.ShapeDtypeStruct((B, S, CONV_CH), BF16))
    scratch.append(pltpu.VMEM((tile + 8, CONV_CH), F32))
    return pl.pallas_call(
        functools.partial(_proj_kernel, tile=tile, dils=dils, sample=sample, nt=nt),
        grid=(B, nt),
        in_specs=in_specs,
        out_specs=out_specs,
        out_shape=out_shape,
        scratch_shapes=scratch,
        compiler_params=_params(("arbitrary", "arbitrary")),
        name="proj_sample" if sample else "proj_prompt",
    )(*args)


def _attn_kernel(q_ref, kp_ref, kc_ref, vp_ref, vc_ref, b_ref, o_ref, l_ref, s_scr, m_scr, o_scr, z_scr,
                 *, nq):
    first = pl.program_id(1) == 0
    low = lax.broadcasted_iota(jnp.int32, (BLK, LANES), 1) < HEAD_DIM
    ones = jnp.ones((BLK, LANES), BF16)
    chains = [(j, pair, sub) for j in range(nq) for pair in range(GROUP_W // LANES) for sub in range(2)]

    def operands(j, pair):
        rows = slice(j * BLK, (j + 1) * BLK)
        sl = slice(pair * LANES, (pair + 1) * LANES)
        if j == 0:
            return rows, sl, kp_ref[0, :, sl], vp_ref[0, :, sl]
        prows = slice((j - 1) * BLK, j * BLK)
        return rows, sl, kc_ref[0, prows, sl], vc_ref[0, prows, sl]

    for c, (j, pair, sub) in enumerate(chains):
        rows, sl, kp, _ = operands(j, pair)
        q = q_ref[0, rows, sl]
        qm = jnp.where(low if sub == 0 else ~low, q, jnp.zeros_like(q))
        hh = 2 * pair + sub
        sp = _dot_nt(qm, kp) + b_ref[hh, :, 0:BLK]
        if j == 0:
            sp = jnp.where(first, NEG, sp)
        sc = _dot_nt(qm, kc_ref[0, rows, sl]) + b_ref[hh, :, BLK:2 * BLK]
        s_scr[c, :, 0:BLK] = sp
        s_scr[c, :, BLK:2 * BLK] = sc
        m_scr[c] = jnp.broadcast_to(jnp.max(jnp.maximum(sp, sc), axis=-1, keepdims=True), (BLK, LANES))

    for c, (j, pair, sub) in enumerate(chains):
        rows, sl, _, vp = operands(j, pair)
        m = m_scr[c]
        pp = jnp.exp(s_scr[c, :, 0:BLK] - m).astype(BF16)
        pc = jnp.exp(s_scr[c, :, BLK:2 * BLK] - m).astype(BF16)
        o_scr[c] = _dot(pp, vp) + _dot(pc, vc_ref[0, rows, sl])
        z_scr[c] = _dot(pp, ones) + _dot(pc, ones)

    for c in range(0, len(chains), 2):
        j, pair, _ = chains[c]
        rows = slice(j * BLK, (j + 1) * BLK)
        sl = slice(pair * LANES, (pair + 1) * LANES)
        o_sub = [o_scr[c + sub] / z_scr[c + sub] for sub in range(2)]
        l_sub = [m_scr[c + sub] + jnp.log(z_scr[c + sub]) for sub in range(2)]
        o_ref[0, rows, sl] = jnp.where(low, o_sub[0], o_sub[1])
        l_ref[0, rows, sl] = jnp.where(low, l_sub[0], l_sub[1])


def _attn_prompt(q, k, v, bias):
    Z, L, _ = q.shape
    qt = min(L, 4 * BLK)
    nq = qt // BLK
    cur = pl.BlockSpec((1, qt, GROUP_W), lambda z, i: (z, i, 0))
    prev = pl.BlockSpec((1, BLK, GROUP_W), lambda z, i: (z, jnp.maximum(i * nq - 1, 0), 0))
    return pl.pallas_call(
        functools.partial(_attn_kernel, nq=nq),
        grid=(Z, L // qt),
        in_specs=[cur, prev, cur, prev, cur,
                  pl.BlockSpec((HEADS_PER_GROUP, BLK, 2 * BLK), lambda z, i: (0, 0, 0))],
        out_specs=[cur, cur],
        out_shape=[jax.ShapeDtypeStruct((Z, L, GROUP_W), F32)] * 2,
        scratch_shapes=[pltpu.VMEM((nq * HEADS_PER_GROUP, BLK, 2 * BLK), F32)]
        + [pltpu.VMEM((nq * HEADS_PER_GROUP, BLK, LANES), F32)] * 3,
        compiler_params=_params(("arbitrary", "arbitrary")),
        name="attn_prompt",
    )(q, k, k, v, v, bias)


def _attn_sample_kernel(q_ref, kc_ref, vc_ref, nk_ref, nv_ref, b1_ref, b2_ref,
                        ko_ref, vo_ref, o_ref, l_ref, *, bb, wb):
    lane = lax.broadcasted_iota(jnp.int32, (HEAD_DIM, LANES), 1)
    new_cols = lane >= LANES - 4
    per_tile = LANES // 4
    for b in range(bb):
        gb = pl.program_id(0) * bb + b
        tile_at = pl.ds(pl.multiple_of((gb // per_tile) * LANES, LANES), LANES)
        shift = (LANES - 4) - 4 * (gb % per_tile)
        for hh in range(HEADS_PER_GROUP):
            q = q_ref[b, hh]
            kt = kc_ref[b, hh]
            vt = vc_ref[b, hh]
            nkt = pltpu.roll(nk_ref[hh, :, tile_at], shift, axis=1)
            nvt = pltpu.roll(nv_ref[hh, :, tile_at], shift, axis=1)
            s1 = _dot(q, kt.astype(BF16)) + b1_ref[hh]
            s2 = _dot(q, nkt.astype(BF16)) + b2_ref[hh]
            m = jnp.maximum(jnp.max(s1, axis=-1, keepdims=True), jnp.max(s2, axis=-1, keepdims=True))
            p1 = jnp.exp(s1 - m)
            p2 = jnp.exp(s2 - m)
            z = jnp.sum(p1, axis=-1, keepdims=True) + jnp.sum(p2, axis=-1, keepdims=True)
            o = _dot_nt(p1.astype(BF16), vt.astype(BF16)) + _dot_nt(p2.astype(BF16), nvt.astype(BF16))
            o_ref[b, hh] = o / z
            l_ref[b, hh] = jnp.broadcast_to(m + jnp.log(z), (8, LANES))
            for src, new, dst in ((kt, nkt, ko_ref), (vt, nvt, vo_ref)):
                rolled = pltpu.roll(src, wb - 4, axis=1)
                if wb > LANES:
                    dst[b, hh, :, 0:wb - LANES] = rolled[:, 0:wb - LANES]
                dst[b, hh, :, wb - LANES:wb] = jnp.where(new_cols, new, rolled[:, wb - LANES:wb])


def _attn_sample(q, kc, vc, nk, nv, b1, b2, *, bb):
    nb, _, _, wb = kc.shape
    blk = lambda *tail: pl.BlockSpec((bb, HEADS_PER_GROUP) + tail, lambda i: (i, 0, 0, 0))
    const = lambda shape: pl.BlockSpec(shape, lambda i: (0, 0, 0))
    return pl.pallas_call(
        functools.partial(_attn_sample_kernel, bb=bb, wb=wb),
        grid=(nb // bb,),
        in_specs=[blk(8, HEAD_DIM), blk(HEAD_DIM, wb), blk(HEAD_DIM, wb), const(nk.shape), const(nv.shape),
                  const((HEADS_PER_GROUP, 8, wb)), const((HEADS_PER_GROUP, 8, LANES))],
        out_specs=[blk(HEAD_DIM, wb), blk(HEAD_DIM, wb), blk(8, HEAD_DIM), blk(8, LANES)],
        out_shape=[jax.ShapeDtypeStruct(kc.shape, F32), jax.ShapeDtypeStruct(kc.shape, F32),
                   jax.ShapeDtypeStruct((nb, HEADS_PER_GROUP, 8, HEAD_DIM), F32),
                   jax.ShapeDtypeStruct((nb, HEADS_PER_GROUP, 8, LANES), F32)],
        compiler_params=_params(("arbitrary",)),
        name="attn_sample",
    )(q, kc, vc, nk, nv, b1, b2)


def _first_max(v, ids, sentinel):
    m = jnp.max(v, axis=0, keepdims=True)
    idx = jnp.min(jnp.where(v == m, ids, sentinel), axis=0, keepdims=True)
    return m, ids == idx


def _oproj_kernel(x_ref, o0, o1, o2, l0, l1, l2, co_ref, g1_ref, sc_ref, sh_ref, ng_ref, wo_ref,
                  wr_ref, br_ref, x1_ref, xg_ref, gt_ref, posr_ref, post_ref, cnt_ref, slab, *, tile, dils):
    o_refs, l_refs = (o0, o1, o2), (l0, l1, l2)
    outs, lses = [], []
    slab_i = 0
    for g, d in enumerate(dils):
        if d == 1:
            outs.append(o_refs[g][0, 0])
            lses.append(l_refs[g][0, 0])
            continue
        n = tile // d
        for ref, dest in ((o_refs[g], outs), (l_refs[g], lses)):
            halves = []
            for half in range(GROUP_W // LANES):
                for r in range(d):
                    slab[slab_i, pl.ds(r, n, stride=d), :] = ref[0, r, :, half * LANES:(half + 1) * LANES]
                halves.append(slab[slab_i])
                slab_i += 1
            dest.append(jnp.concatenate(halves, axis=1))

    m = jnp.maximum(jnp.maximum(lses[0], lses[1]), lses[2])
    es = [jnp.exp(l - m) for l in lses]
    den = es[0] + es[1] + es[2]
    cat = [(outs[g] * (es[g] / den)).astype(BF16) for g in range(N_GROUPS)]
    cat.append(co_ref[0].astype(BF16))
    cat = jnp.concatenate(cat, axis=1)
    x1 = x_ref[0] + g1_ref[0] * _dot(cat, wo_ref[...])
    x1_ref[0] = x1

    ms = jnp.mean(x1 * x1, axis=-1, keepdims=True)
    h2 = x1 * lax.rsqrt(ms + RMS_EPS) * ng_ref[...]
    h2 = h2 * (1.0 + sc_ref[0]) + sh_ref[0]
    hh = h2.astype(BF16)
    xg_ref[0, :, 0:D_MODEL] = hh

    hl = (h2 - hh.astype(F32)).astype(BF16)
    wr = wr_ref[...]
    wh = wr.astype(BF16)
    wl = (wr - wh.astype(F32)).astype(BF16)
    logits = _dot_nt(wh, hh) + _dot_nt(wh, hl) + _dot_nt(wl, hh)
    scores = 1.0 / (1.0 + jnp.exp(-logits))
    sel = scores + br_ref[...]

    ids = lax.broadcasted_iota(jnp.int32, (GROUP_SIZE, tile), 0)
    ninf = -jnp.inf
    sel_g = [sel[g * GROUP_SIZE:(g + 1) * GROUP_SIZE] for g in range(N_ROUTE_GROUPS)]
    gscore = jnp.zeros((N_ROUTE_GROUPS, tile), F32)
    for g in range(N_ROUTE_GROUPS):
        m1, oh = _first_max(sel_g[g], ids, GROUP_SIZE)
        m2 = jnp.max(jnp.where(oh, ninf, sel_g[g]), axis=0, keepdims=True)
        gscore = jnp.where(ids == g, m1 + m2, gscore)
    gsel = jnp.zeros((N_ROUTE_GROUPS, tile), F32)
    for _ in range(TOPK_GROUPS):
        _, oh = _first_max(gscore, ids, N_ROUTE_GROUPS)
        gsel = jnp.where(oh, 1.0, gsel)
        gscore = jnp.where(oh, ninf, gscore)
    cand = [jnp.where(gsel[g:g + 1] > 0.0, sel_g[g], ninf) for g in range(N_ROUTE_GROUPS)]
    chosen = [jnp.zeros((GROUP_SIZE, tile), F32) for _ in range(N_ROUTE_GROUPS)]
    picks = []
    for _ in range(TOP_K):
        mx = cand[0]
        for g in range(1, N_ROUTE_GROUPS):
            mx = jnp.maximum(mx, cand[g])
        mx = jnp.max(mx, axis=0, keepdims=True)
        idx = jnp.where(cand[0] == mx, ids, N_EXPERTS)
        for g in range(1, N_ROUTE_GROUPS):
            idx = jnp.minimum(idx, jnp.where(cand[g] == mx, ids + g * GROUP_SIZE, N_EXPERTS))
        idx = jnp.min(idx, axis=0, keepdims=True)
        ohs = [(ids + g * GROUP_SIZE) == idx for g in range(N_ROUTE_GROUPS)]
        picks.append(ohs)
        for g in range(N_ROUTE_GROUPS):
            chosen[g] = jnp.where(ohs[g], 1.0, chosen[g])
            cand[g] = jnp.where(ohs[g], ninf, cand[g])
    wts = [jnp.where(chosen[g] > 0.0, scores[g * GROUP_SIZE:(g + 1) * GROUP_SIZE], 0.0)
           for g in range(N_ROUTE_GROUPS)]
    wsum = jnp.sum(wts[0], axis=0, keepdims=True)
    for g in range(1, N_ROUTE_GROUPS):
        wsum = wsum + jnp.sum(wts[g], axis=0, keepdims=True)
    gates_t = jnp.concatenate([w / wsum * ROUTED_SCALE for w in wts], axis=0)
    gt_ref[0] = jnp.concatenate([gates_t, jnp.zeros((LANES - N_EXPERTS, tile), F32)], axis=0).T
    g_hi = gates_t.astype(BF16).astype(F32)
    xg_ref[0, :, D_MODEL:D_MODEL + LANES] = jnp.concatenate([g_hi, gates_t - g_hi], axis=0).T.astype(BF16)

    er = lax.broadcasted_iota(jnp.int32, (N_EXPERTS, N_EXPERTS), 0)
    ec = lax.broadcasted_iota(jnp.int32, (N_EXPERTS, N_EXPERTS), 1)
    lower = jnp.where(ec < er, 1.0, 0.0).astype(BF16)
    tr = lax.broadcasted_iota(jnp.int32, (SUB, SUB), 0)
    tc = lax.broadcasted_iota(jnp.int32, (SUB, SUB), 1)
    upper = jnp.where(tr < tc, 1.0, 0.0).astype(BF16)
    for u in range(tile // SUB):
        cols = slice(u * SUB, (u + 1) * SUB)
        ch = jnp.concatenate([c[:, cols] for c in chosen], axis=0)
        cnt = jnp.sum(ch, axis=1, keepdims=True)
        cpad = jnp.broadcast_to(jnp.ceil(cnt / CHUNK) * CHUNK, (N_EXPERTS, LANES))
        cnt_ref[u] = cpad
        seg0 = _dot(lower, cpad.astype(BF16))[:, 0:1]
        slot = seg0 + _dot(ch.astype(BF16), upper)
        rows = []
        for k in range(TOP_K):
            acc = jnp.where(picks[k][0][:, cols], slot[0:GROUP_SIZE], 0.0)
            for g in range(1, N_ROUTE_GROUPS):
                acc = acc + jnp.where(picks[k][g][:, cols], slot[g * GROUP_SIZE:(g + 1) * GROUP_SIZE], 0.0)
            rows.append(jnp.sum(acc, axis=0, keepdims=True))
        posr = jnp.concatenate(rows, axis=0)
        posr_ref[u] = posr
        post_ref[0, cols, :] = jnp.concatenate([posr, jnp.zeros((LANES - TOP_K, SUB), F32)], axis=0).T


def _oproj(x, o_list, l_list, co, g1, sc, sh, ng, wo_bf, wr_t, br, *, tile, dils):
    B, S, _ = x.shape
    tm = g1.shape[1]
    mod_spec = pl.BlockSpec((1, tm, D_MODEL), (lambda b, j: (b, j, 0)) if tm > 1 else (lambda b, j: (b, 0, 0)))
    const2 = lambda shape: pl.BlockSpec(shape, lambda b, j: (0, 0))
    nat = lambda w: pl.BlockSpec((1, tile, w), lambda b, j: (b, j, 0))
    dspec = [pl.BlockSpec((1, d, tile // d, GROUP_W), lambda b, j: (b, 0, j, 0)) for d in dils]
    n_slabs = sum(2 * (GROUP_W // LANES) for d in dils if d > 1)
    nsub_t = tile // SUB
    return pl.pallas_call(
        functools.partial(_oproj_kernel, tile=tile, dils=dils),
        grid=(B, S // tile),
        in_specs=[nat(D_MODEL)] + dspec + dspec + [nat(CONV_CH), mod_spec, mod_spec, mod_spec,
                  const2((1, D_MODEL)), const2((D_MODEL, D_MODEL)), const2((N_EXPERTS, D_MODEL)),
                  const2((N_EXPERTS, 1))],
        out_specs=[nat(D_MODEL), nat(XG_W), nat(LANES),
                   pl.BlockSpec((nsub_t, TOP_K, SUB), lambda b, j: (b * (S // tile) + j, 0, 0)),
                   nat(LANES),
                   pl.BlockSpec((nsub_t, N_EXPERTS, LANES), lambda b, j: (b * (S // tile) + j, 0, 0))],
        out_shape=[jax.ShapeDtypeStruct((B, S, D_MODEL), F32), jax.ShapeDtypeStruct((B, S, XG_W), BF16),
                   jax.ShapeDtypeStruct((B, S, LANES), F32),
                   jax.ShapeDtypeStruct((B * S // SUB, TOP_K, SUB), F32),
                   jax.ShapeDtypeStruct((B, S, LANES), F32),
                   jax.ShapeDtypeStruct((B * S // SUB, N_EXPERTS, LANES), F32)],
        scratch_shapes=[pltpu.VMEM((max(n_slabs, 1), tile, LANES), F32)],
        compiler_params=_params(("arbitrary", "arbitrary")),
        name="oproj",
    )(x, *o_list, *l_list, co, g1, sc, sh, ng, wo_bf, wr_t, br)


def _moe_kernel(h_ref, gt_ref, x1_ref, g2_ref, wg_ref, wu_ref, wd_ref, wgs_ref, wus_ref, wds_ref,
                y_ref, acc, *, tile):
    e = pl.program_id(2)
    h = h_ref[0]

    @pl.when(e == 0)
    def _():
        a = _dot(h, wgs_ref[...].astype(BF16))
        b = _dot(h, wus_ref[...].astype(BF16))
        acc[...] = _dot((_silu(a) * b).astype(BF16), wds_ref[...].astype(BF16))

    a = _dot(h, wg_ref[0].astype(BF16))
    b = _dot(h, wu_ref[0].astype(BF16))
    lane = lax.broadcasted_iota(jnp.int32, (tile, LANES), 1)
    gcol = jnp.sum(jnp.where(lane == e, gt_ref[0], 0.0), axis=1, keepdims=True)
    hc = _silu(a) * b * gcol
    acc[...] += _dot(hc.astype(BF16), wd_ref[0].astype(BF16))

    @pl.when(e == N_EXPERTS - 1)
    def _():
        y_ref[0] = x1_ref[0] + g2_ref[0] * acc[...]


def _moe(h2, gates, x1, g2, wg, wu, wd, wgs, wus, wds, *, tile):
    B, S, _ = h2.shape
    tm = g2.shape[1]
    mod_spec = pl.BlockSpec((1, tm, D_MODEL),
                            (lambda b, j, e: (b, j, 0)) if tm > 1 else (lambda b, j, e: (b, 0, 0)))
    nat = lambda w: pl.BlockSpec((1, tile, w), lambda b, j, e: (b, j, 0))
    const2 = lambda shape: pl.BlockSpec(shape, lambda b, j, e: (0, 0))
    return pl.pallas_call(
        functools.partial(_moe_kernel, tile=tile),
        grid=(B, S // tile, N_EXPERTS),
        in_specs=[nat(D_MODEL), nat(LANES), nat(D_MODEL), mod_spec,
                  pl.BlockSpec((1, D_MODEL, D_EXPERT), lambda b, j, e: (e, 0, 0)),
                  pl.BlockSpec((1, D_MODEL, D_EXPERT), lambda b, j, e: (e, 0, 0)),
                  pl.BlockSpec((1, D_EXPERT, D_MODEL), lambda b, j, e: (e, 0, 0)),
                  const2((D_MODEL, D_EXPERT)), const2((D_MODEL, D_EXPERT)), const2((D_EXPERT, D_MODEL))],
        out_specs=nat(D_MODEL),
        out_shape=jax.ShapeDtypeStruct((B, S, D_MODEL), F32),
        scratch_shapes=[pltpu.VMEM((tile, D_MODEL), F32)],
        compiler_params=_params(("arbitrary", "arbitrary", "arbitrary")),
        name="moe",
    )(h2, gates, x1, g2, wg, wu, wd, wgs, wus, wds)


FFN_TM = 512
SUB_ROWS = SUB * TOP_K + N_EXPERTS * CHUNK
N_CHUNKS = SUB_ROWS // CHUNK
ZERO_CHUNK = N_CHUNKS
TABLE_W = 256
MXU_ROWS = 256
BLOCKS_PER_TRIP = 2


def _slot_onehot(first, pos_list, axis_iota):
    shape = axis_iota.shape
    hit = axis_iota == jnp.broadcast_to(pos_list[0] - first, shape).astype(BF16)
    for p in pos_list[1:]:
        hit = hit | (axis_iota == jnp.broadcast_to(p - first, shape).astype(BF16))
    return jnp.where(hit, jnp.ones(shape, BF16), jnp.zeros(shape, BF16))


def _chunk_copy(src, src_chunk, dst, dst_chunk, sem):
    return pltpu.make_async_copy(src.at[src_chunk], dst.at[dst_chunk], sem)


def _dispatch_kernel(nblk_ref, ncopy_ref, ntile_ref, src_ref, dst_ref, posr_ref, xg_ref, xs_hbm, xs_scr,
                     zero_scr, sem, tail_sem, *, nsub, n_tiles_max):
    s = pl.program_id(0)

    tile_chunks = FFN_TM // CHUNK
    blk_chunks = MXU_ROWS // CHUNK
    zero_scr[...] = jnp.zeros((tile_chunks, CHUNK, XG_W), BF16)
    tail_blocks = [ntile_ref[0] + s + r * nsub for r in range(-(-(n_tiles_max) // nsub))]

    def tail_copy(t):
        return pltpu.make_async_copy(
            zero_scr, xs_hbm.at[pl.ds(pl.multiple_of(t * tile_chunks, tile_chunks), tile_chunks)], tail_sem)

    for t in tail_blocks:
        @pl.when(t < n_tiles_max)
        def _():
            tail_copy(t).start()

    xg = xg_ref[...]
    pos = [posr_ref[0, k:k + 1, :] for k in range(TOP_K)]
    rows = lax.broadcasted_iota(jnp.int32, (MXU_ROWS, SUB), 0).astype(BF16)
    xs_scr[ZERO_CHUNK] = jnp.zeros((CHUNK, XG_W), BF16)

    def sort_blocks(i, carry):
        for j in range(BLOCKS_PER_TRIP):
            blk = i * BLOCKS_PER_TRIP + j
            onehot = _slot_onehot((blk * MXU_ROWS).astype(F32), pos, rows)
            sorted_rows = _dot(onehot, xg).astype(BF16)
            xs_scr[pl.ds(pl.multiple_of(blk * blk_chunks, blk_chunks), blk_chunks)] = (
                sorted_rows.reshape(blk_chunks, CHUNK, XG_W))
        return carry

    lax.fori_loop(0, nblk_ref[s], sort_blocks, 0)
    n = ncopy_ref[s]

    def start(i, carry):
        _chunk_copy(xs_scr, src_ref[0, 0, i], xs_hbm, dst_ref[0, 0, i], sem).start()
        return carry

    def wait(i, carry):
        _chunk_copy(xs_scr, 0, xs_hbm, 0, sem).wait()
        return carry

    lax.fori_loop(0, n, start, 0)
    lax.fori_loop(0, n, wait, 0)
    for t in tail_blocks:
        @pl.when(t < n_tiles_max)
        def _():
            tail_copy(t).wait()


def _dispatch(nblk, ncopy, n_tiles, src, dst, posr, xg, n_rows):
    nsub = posr.shape[0]
    tab = pl.BlockSpec((1, 1, TABLE_W), lambda s, *_: (s, 0, 0), memory_space=pltpu.SMEM)
    return pl.pallas_call(
        functools.partial(_dispatch_kernel, nsub=nsub, n_tiles_max=n_rows // FFN_TM),
        grid_spec=pltpu.PrefetchScalarGridSpec(
            num_scalar_prefetch=3, grid=(nsub,),
            in_specs=[tab, tab, pl.BlockSpec((1, TOP_K, SUB), lambda s, *_: (s, 0, 0)),
                      pl.BlockSpec((SUB, XG_W), lambda s, *_: (s, 0))],
            out_specs=pl.BlockSpec(memory_space=pl.ANY),
            scratch_shapes=[pltpu.VMEM((N_CHUNKS + 1, CHUNK, XG_W), BF16),
                            pltpu.VMEM((FFN_TM // CHUNK, CHUNK, XG_W), BF16),
                            pltpu.SemaphoreType.DMA(()), pltpu.SemaphoreType.DMA(())]),
        out_shape=jax.ShapeDtypeStruct((n_rows // CHUNK, CHUNK, XG_W), BF16),
        compiler_params=_params(("arbitrary",)),
        name="moe_dispatch",
    )(nblk, ncopy, n_tiles, src, dst, posr, xg).reshape(n_rows, XG_W)


def _ffn_kernel(te_ref, nt_ref, xs_ref, wg_ref, wu_ref, wd_ref, ys_ref, wg_s, wu_s, wd_s):
    i = pl.program_id(0)

    @pl.when(i >= nt_ref[0])
    def _():
        ys_ref[...] = jnp.zeros((FFN_TM, D_MODEL), BF16)

    @pl.when(i < nt_ref[0])
    def _():
        e = te_ref[i]

        @pl.when((i == 0) | (e != te_ref[jnp.maximum(i - 1, 0)]))
        def _():
            wg_s[...] = wg_ref[0].astype(BF16)
            wu_s[...] = wu_ref[0].astype(BF16)
            wd_s[...] = wd_ref[0].astype(BF16)

        x = xs_ref[:, 0:D_MODEL]
        g = xs_ref[:, D_MODEL:XG_W].astype(F32)
        lane = lax.broadcasted_iota(jnp.int32, (FFN_TM, LANES), 1)
        gcol = jnp.sum(jnp.where((lane == e) | (lane == e + N_EXPERTS), g, 0.0), axis=1, keepdims=True)
        a = _dot(x, wg_s[...])
        b = _dot(x, wu_s[...])
        hc = _silu(a) * b * gcol
        ys_ref[...] = _dot(hc.astype(BF16), wd_s[...]).astype(BF16)


def _ffn(tile_expert, n_tiles, xs, wg, wu, wd):
    n_rows = xs.shape[0]
    row = lambda i, te, nt: (jnp.minimum(i, nt[0] - 1), 0)
    wspec = lambda shape: pl.BlockSpec((1,) + shape, lambda i, te, nt: (te[i], 0, 0))
    return pl.pallas_call(
        _ffn_kernel,
        grid_spec=pltpu.PrefetchScalarGridSpec(
            num_scalar_prefetch=2, grid=(n_rows // FFN_TM,),
            in_specs=[pl.BlockSpec((FFN_TM, XG_W), row), wspec((D_MODEL, D_EXPERT)),
                      wspec((D_MODEL, D_EXPERT)), wspec((D_EXPERT, D_MODEL))],
            out_specs=pl.BlockSpec((FFN_TM, D_MODEL), lambda i, te, nt: (i, 0)),
            scratch_shapes=[pltpu.VMEM((D_MODEL, D_EXPERT), BF16), pltpu.VMEM((D_MODEL, D_EXPERT), BF16),
                            pltpu.VMEM((D_EXPERT, D_MODEL), BF16)]),
        out_shape=jax.ShapeDtypeStruct((n_rows, D_MODEL), BF16),
        compiler_params=_params(("arbitrary",)),
        name="moe_ffn",
    )(tile_expert, n_tiles, xs, wg, wu, wd)


def _combine_kernel(nblk_ref, ncopy_ref, src_ref, post_ref, xg_ref, x1_ref, g2_ref, wgs_ref, wus_ref,
                    wds_ref, ys_hbm, y_ref, ys_scr, wgs_s, wus_s, wds_s, sem):
    s = pl.program_id(0)
    n = ncopy_ref[s]

    def start(i, carry):
        _chunk_copy(ys_hbm, src_ref[0, 0, i], ys_scr, i, sem).start()
        return carry

    def wait(i, carry):
        _chunk_copy(ys_hbm, 0, ys_scr, 0, sem).wait()
        return carry

    lax.fori_loop(0, n, start, 0)

    @pl.when(s == 0)
    def _():
        wgs_s[...] = wgs_ref[...].astype(BF16)
        wus_s[...] = wus_ref[...].astype(BF16)
        wds_s[...] = wds_ref[...].astype(BF16)

    h = xg_ref[:, 0:D_MODEL]
    a = _dot(h, wgs_s[...])
    b = _dot(h, wus_s[...])
    shared = _dot((_silu(a) * b).astype(BF16), wds_s[...])

    post = post_ref[...]
    pos = [jnp.broadcast_to(post[:, k:k + 1], (SUB, MXU_ROWS)) for k in range(TOP_K)]
    lanes = lax.broadcasted_iota(jnp.int32, (SUB, MXU_ROWS), 1).astype(BF16)
    lax.fori_loop(0, n, wait, 0)

    blk_chunks = MXU_ROWS // CHUNK

    def gather_blocks(i, acc):
        for j in range(BLOCKS_PER_TRIP):
            blk = i * BLOCKS_PER_TRIP + j
            onehot = _slot_onehot((blk * MXU_ROWS).astype(F32), pos, lanes)
            rows = ys_scr[pl.ds(pl.multiple_of(blk * blk_chunks, blk_chunks), blk_chunks)]
            acc = acc + _dot(onehot, rows.reshape(MXU_ROWS, D_MODEL))
        return acc

    routed = lax.fori_loop(0, nblk_ref[s], gather_blocks, jnp.zeros((SUB, D_MODEL), F32))
    y_ref[...] = x1_ref[...] + g2_ref[0] * (routed + shared)


def _combine(nblk, ncopy, src, post, xg, x1, g2, wgs, wus, wds, ys):
    n_tok = x1.shape[0]
    nsub = n_tok // SUB
    per_b = nsub // g2.shape[0]
    tab = pl.BlockSpec((1, 1, TABLE_W), lambda s, *_: (s, 0, 0), memory_space=pltpu.SMEM)
    const2 = lambda shape: pl.BlockSpec(shape, lambda s, *_: (0, 0))
    return pl.pallas_call(
        _combine_kernel,
        grid_spec=pltpu.PrefetchScalarGridSpec(
            num_scalar_prefetch=2, grid=(nsub,),
            in_specs=[tab, pl.BlockSpec((SUB, LANES), lambda s, *_: (s, 0)),
                      pl.BlockSpec((SUB, XG_W), lambda s, *_: (s, 0)),
                      pl.BlockSpec((SUB, D_MODEL), lambda s, *_: (s, 0)),
                      pl.BlockSpec((1, 1, D_MODEL), lambda s, *_: (s // per_b, 0, 0)),
                      const2((D_MODEL, D_EXPERT)), const2((D_MODEL, D_EXPERT)), const2((D_EXPERT, D_MODEL)),
                      pl.BlockSpec(memory_space=pl.ANY)],
            out_specs=pl.BlockSpec((SUB, D_MODEL), lambda s, *_: (s, 0)),
            scratch_shapes=[pltpu.VMEM((N_CHUNKS, CHUNK, D_MODEL), BF16), pltpu.VMEM((D_MODEL, D_EXPERT), BF16),
                            pltpu.VMEM((D_MODEL, D_EXPERT), BF16), pltpu.VMEM((D_EXPERT, D_MODEL), BF16),
                            pltpu.SemaphoreType.DMA(())]),
        out_shape=jax.ShapeDtypeStruct((n_tok, D_MODEL), F32),
        compiler_params=_params(("arbitrary",)),
        name="moe_combine",
    )(nblk, ncopy, src, post, xg, x1, g2, wgs, wus, wds, ys.reshape(-1, CHUNK, D_MODEL))


def _route_tables(cnt):
    nsub = cnt.shape[0]
    i32 = jnp.int32
    seg_end = jnp.cumsum(cnt, axis=1)
    seg_start = seg_end - cnt
    tot = jnp.sum(cnt, axis=0)
    tot_al = (tot + FFN_TM - 1) // FFN_TM * FFN_TM
    reg_end = jnp.cumsum(tot_al)
    reg_start = reg_end - tot_al
    base = reg_start[None, :] + jnp.cumsum(cnt, axis=0) - cnt
    n_real = seg_end[:, -1] // CHUNK
    k = jnp.arange(TABLE_W, dtype=i32)
    owner = jnp.sum((seg_end[:, None, :] // CHUNK) <= k[None, :, None], axis=2)
    owner = jnp.minimum(owner, N_EXPERTS - 1)
    delta = (base - seg_start) // CHUNK
    e_ids = jnp.arange(N_EXPERTS, dtype=i32)
    real_dst = jnp.sum(jnp.where(owner[:, :, None] == e_ids, delta[:, None, :], 0), axis=2) + k[None, :]
    fill_n = (tot_al - tot) // CHUNK
    fill_dst0 = (reg_start + tot) // CHUNK
    src = jnp.broadcast_to(k[None, :], (nsub, TABLE_W))
    dst = real_dst
    ncopy = n_real
    for r in range(-(-N_EXPERTS // nsub)):
        e_of = jnp.arange(nsub, dtype=i32) + r * nsub
        ok = e_of < N_EXPERTS
        e_cl = jnp.minimum(e_of, N_EXPERTS - 1)
        fn = jnp.where(ok, fill_n[e_cl], 0)
        j = k[None, :] - ncopy[:, None]
        is_fill = (j >= 0) & (j < fn[:, None])
        src = jnp.where(is_fill, ZERO_CHUNK, src)
        dst = jnp.where(is_fill, fill_dst0[e_cl][:, None] + j, dst)
        ncopy = ncopy + fn
    trip_rows = BLOCKS_PER_TRIP * MXU_ROWS
    disp_nblk = (n_real * CHUNK + trip_rows - 1) // trip_rows
    comb_n = disp_nblk * (trip_rows // CHUNK)
    comb_src = jnp.where(k[None, :] < n_real[:, None], real_dst, real_dst[:, 0:1])
    n_tiles = reg_end[-1] // FFN_TM
    t = jnp.arange((nsub * SUB_ROWS + N_EXPERTS * FFN_TM) // FFN_TM, dtype=i32)
    tile_expert = jnp.minimum(jnp.sum((reg_end[None, :] // FFN_TM) <= t[:, None], axis=1), N_EXPERTS - 1)
    as3 = lambda a: a.astype(i32).reshape(nsub, 1, TABLE_W)
    return (disp_nblk.astype(i32), ncopy.astype(i32), as3(src), as3(dst), comb_n.astype(i32), as3(comb_src),
            tile_expert.astype(i32), n_tiles.astype(i32).reshape(1))


def _moe_sparse(xg, post, posr, cnt, x1, g2, wg, wu, wd, wgs, wus, wds):
    B, S, _ = x1.shape
    n_tok = B * S
    nsub = n_tok // SUB
    (disp_nblk, disp_n, disp_src, disp_dst, comb_n, comb_src, tile_expert, n_tiles) = _route_tables(
        cnt[:, :, 0].astype(jnp.int32))
    n_rows = nsub * SUB_ROWS + N_EXPERTS * FFN_TM
    xg2 = xg.reshape(n_tok, XG_W)
    xs = _dispatch(disp_nblk, disp_n, n_tiles, disp_src, disp_dst, posr, xg2, n_rows)
    ys = _ffn(tile_expert, n_tiles, xs, wg, wu, wd)
    y = _combine(disp_nblk, comb_n, comb_src, post.reshape(n_tok, LANES), xg2, x1.reshape(n_tok, D_MODEL),
                 g2, wgs, wus, wds, ys)
    return y.reshape(B, S, D_MODEL)


def _t5_bucket(dist):
    max_exact = N_BUCKETS // 2
    df = jnp.maximum(dist, 1).astype(F32)
    large = max_exact + (jnp.log(df / max_exact) / math.log(MAX_DISTANCE / max_exact)
                         * (N_BUCKETS - max_exact)).astype(jnp.int32)
    large = jnp.minimum(large, N_BUCKETS - 1)
    return jnp.where(dist < max_exact, dist, large)


def _step_bias(rel_bias, g):
    dist = jnp.arange(N_KEYS, dtype=jnp.int32) * DILATIONS[g]
    cols = rel_bias[:, g * HEADS_PER_GROUP:(g + 1) * HEADS_PER_GROUP]
    return cols[_t5_bucket(dist)].T.astype(F32)


def _prompt_bias(bias_k):
    h = bias_k.shape[0]
    n = 3 * BLK
    row = jnp.concatenate([bias_k[:, ::-1], jnp.full((h, n - N_KEYS), NEG, F32)], axis=1)
    t = jnp.tile(row, (1, BLK))[:, :BLK * (n - 1)].reshape(h, BLK, n - 1)
    return t[:, :, :2 * BLK]


def _sample_bias(bias_k, wb, d):
    h = bias_k.shape[0]
    rev = bias_k[:, :0:-1]
    rows = []
    for t in range(4):
        if d == 1:
            rows.append(jnp.concatenate([jnp.full((h, t), NEG, F32), rev[:, :wb - t]], axis=1))
        else:
            r = jnp.arange(d)[None, None, :]
            rows.append(jnp.where(r == t, rev[:, :, None], NEG).reshape(h, wb))
    b1 = jnp.concatenate([jnp.stack(rows, axis=1), jnp.zeros((h, 4, wb), F32)], axis=1)
    rows = []
    for t in range(4):
        cols = [bias_k[:, (t - tn) // d] if (t >= tn and (t - tn) % d == 0) else jnp.full((h,), NEG, F32)
                for tn in range(4)]
        rows.append(jnp.concatenate([jnp.full((h, LANES - 4), NEG, F32), jnp.stack(cols, axis=1)], axis=1))
    b2 = jnp.concatenate([jnp.stack(rows, axis=1), jnp.zeros((h, 4, LANES), F32)], axis=1)
    return b1, b2


def _layer(x, mods, hist, caches, rel_bias, weights, *, sample):
    (norm1_g, norm2_g, w_in_bf, qg, kg, conv_w, bd, w_o_bf, w_r_t, b_r, w_gate_e, w_up_e, w_down_e,
     w_gate_s, w_up_s, w_down_s) = weights
    sh1, sc1, g1, sh2, sc2, g2 = mods
    B, S, _ = x.shape
    tile = 512
    dils = (1, 1, 1) if sample else DILATIONS
    res = _proj(x, sc1, sh1, norm1_g, w_in_bf, qg, kg, conv_w, bd, hist, tile=tile, dils=dils, sample=sample)
    states = []
    o_list, l_list = [], []
    if sample:
        qf, kt, vt, u, co = res
        nb = S // 4
        for g in range(N_GROUPS):
            d, wb = DILATIONS[g], WINDOWS[g]
            kc, vc = caches[g]
            bias_k = _step_bias(rel_bias, g)
            b1, b2 = _sample_bias(bias_k, wb, d)
            q = qf[0, :, g * GROUP_W:(g + 1) * GROUP_W].reshape(nb, 4, HEADS_PER_GROUP, HEAD_DIM)
            q = jnp.pad(q.transpose(0, 2, 1, 3), ((0, 0), (0, 0), (0, 4), (0, 0))).astype(BF16)
            new_t = lambda a: a[g * GROUP_W:(g + 1) * GROUP_W].reshape(HEADS_PER_GROUP, HEAD_DIM, S)
            ko, vo, o, lse = _attn_sample(q, kc, vc, new_t(kt), new_t(vt), b1, b2,
                                          bb={128: 8, 512: 4, 2048: 1}[wb])
            states += [ko.transpose(0, 3, 1, 2)[None], vo.transpose(0, 3, 1, 2)[None]]
            o = o[:, :, :4].transpose(0, 2, 1, 3).reshape(1, 1, S, GROUP_W)
            lse = jnp.broadcast_to(lse[:, :, :4, :1], (nb, HEADS_PER_GROUP, 4, HEAD_DIM))
            o_list.append(o)
            l_list.append(lse.transpose(0, 2, 1, 3).reshape(1, 1, S, GROUP_W))
        states.append(u.reshape(nb, 4, CONV_CH)[:, 2:][None])
    else:
        qkv, tails, (u, co) = res[:9], res[9:15], res[15:]
        for g in range(N_GROUPS):
            d, w = DILATIONS[g], WINDOWS[g]
            q, k, v = (a.reshape(B * d, S // d, GROUP_W) for a in qkv[3 * g:3 * g + 3])
            o, lse = _attn_prompt(q, k, v, _prompt_bias(_step_bias(rel_bias, g)))
            o_list.append(o.reshape(B, d, S // d, GROUP_W))
            l_list.append(lse.reshape(B, d, S // d, GROUP_W))
            keep = lambda a: a.reshape(B, HEADS_PER_GROUP, HEAD_DIM, w).transpose(0, 3, 1, 2)[None]
            states += [keep(tails[2 * g]), keep(tails[2 * g + 1])]
        states.append(u[:, 6:][None])
    x1, xg, gates, posr, post, cnt = _oproj(x, o_list, l_list, co, g1, sc2, sh2, norm2_g, w_o_bf, w_r_t, b_r,
                                            tile=tile, dils=dils)
    if sample:
        y = _moe(xg, gates, x1, g2, w_gate_e, w_up_e, w_down_e, w_gate_s, w_up_s, w_down_s, tile=tile)
    else:
        y = _moe_sparse(xg, post, posr, cnt, x1, g2, w_gate_e, w_up_e, w_down_e, w_gate_s, w_up_s, w_down_s)
    return y, states


def kernel(x_prompt, x_sample, c_prompt, c_sample, cache_k_w128, cache_v_w128, cache_k_w512, cache_v_w512, cache_k_w2048, cache_v_w2048, state_conv, rel_bias, norm1_g, norm2_g, w_ada, b_ada, w_in, q_norm_g, k_norm_g, conv_w, w_o, w_router, b_router, w_gate_e, w_up_e, w_down_e, w_gate_s, w_up_s, w_down_s):
    B = x_prompt.shape[0]
    DB, T = x_sample.shape[:2]
    n_c = B + DB
    c_all = jnp.pad(jnp.concatenate([c_prompt, c_sample], axis=0), ((0, (-n_c) % 8), (0, 0)))
    mod = _ada(c_all, w_ada[0], b_ada)
    chunks = [mod[:, i * D_MODEL:(i + 1) * D_MODEL] for i in range(6)]
    mods_p = [c[:B].reshape(B, 1, D_MODEL) for c in chunks]
    mods_s = [jnp.repeat(c[B:n_c], T, axis=0).reshape(1, DB * T, D_MODEL) for c in chunks]

    eye = jnp.arange(GROUP_W) // HEAD_DIM
    bd = jnp.where(eye[:, None] == eye[None, :], 1.0 / HEAD_DIM, 0.0).astype(BF16)
    weights = (norm1_g, norm2_g, w_in[0].astype(BF16),
               jnp.tile(q_norm_g, (1, HEADS_PER_GROUP)), jnp.tile(k_norm_g, (1, HEADS_PER_GROUP)),
               conv_w[0], bd, w_o[0].astype(BF16), w_router[0].T, b_router.reshape(N_EXPERTS, 1),
               w_gate_e[0], w_up_e[0], w_down_e[0], w_gate_s[0], w_up_s[0], w_down_s[0])

    yp, st_p = _layer(x_prompt, mods_p, None, None, rel_bias, weights, sample=False)

    s0, s1 = state_conv[0, :, 0], state_conv[0, :, 1]
    zero = jnp.zeros_like(s0)
    hist_a = jnp.stack([s0, s1, zero, zero], axis=1).reshape(DB * T, CONV_CH)
    hist_b = jnp.stack([s1, zero, zero, zero], axis=1).reshape(DB * T, CONV_CH)
    caches = [(ck[0].transpose(0, 2, 3, 1), cv[0].transpose(0, 2, 3, 1))
              for ck, cv in ((cache_k_w128, cache_v_w128), (cache_k_w512, cache_v_w512),
                             (cache_k_w2048, cache_v_w2048))]
    ys, st_s = _layer(x_sample.reshape(1, DB * T, D_MODEL), mods_s, (hist_a, hist_b), caches, rel_bias,
                      weights, sample=True)
    return (yp, ys.reshape(DB, T, D_MODEL), *st_p, *st_s)
```

```python
import functools
import math

import jax
import jax.numpy as jnp
from jax import lax
from jax.experimental import pallas as pl
from jax.experimental.pallas import tpu as pltpu

D_MODEL = 1024
HEAD_DIM = 64
HEADS_PER_GROUP = 4
GROUP_W = HEADS_PER_GROUP * HEAD_DIM
WINDOWS = (128, 512, 2048)
DILATIONS = (1, 4, 16)
N_GROUPS = 3
ATTN_W = N_GROUPS * GROUP_W
CONV_CH = 256
N_PROJ = 3 * ATTN_W + 3 * CONV_CH
N_STEPS = 128
N_KEYS = N_STEPS + 1
BLK = 128
N_BUCKETS = 32
MAX_DISTANCE = WINDOWS[-1]
N_EXPERTS = 64
TOP_K = 8
N_ROUTE_GROUPS = 8
GROUP_SIZE = N_EXPERTS // N_ROUTE_GROUPS
TOPK_GROUPS = 4
D_EXPERT = 256
ROUTED_SCALE = 2.5
RMS_EPS = 1e-6
NEG = -1e30
LANES = 128
SUB = 256
CHUNK = 16
XG_W = D_MODEL + LANES
VMEM_LIMIT = 56 * 1024 * 1024

F32 = jnp.float32
BF16 = jnp.bfloat16


def _dot(a, b):
    return jnp.dot(a, b, preferred_element_type=F32)


def _dot_nt(a, b):
    return lax.dot_general(a, b, (((1,), (1,)), ((), ())), preferred_element_type=F32)


def _silu(a):
    return a / (1.0 + jnp.exp(-a))


def _params(sem):
    return pltpu.CompilerParams(dimension_semantics=sem, vmem_limit_bytes=VMEM_LIMIT)


def _ada_kernel(c_ref, w_ref, b_ref, o_ref):
    s = _silu(c_ref[...])
    o_ref[...] = _dot(s.astype(BF16), w_ref[...].astype(BF16)) + b_ref[...]


def _ada(c, w_ada, b_ada):
    n = c.shape[0]
    nc = w_ada.shape[1] // D_MODEL
    return pl.pallas_call(
        _ada_kernel,
        grid=(nc,),
        in_specs=[pl.BlockSpec((n, D_MODEL), lambda j: (0, 0)),
                  pl.BlockSpec((D_MODEL, D_MODEL), lambda j: (0, j)),
                  pl.BlockSpec((1, D_MODEL), lambda j: (0, j))],
        out_specs=pl.BlockSpec((n, D_MODEL), lambda j: (0, j)),
        out_shape=jax.ShapeDtypeStruct((n, w_ada.shape[1]), F32),
        compiler_params=_params(("arbitrary",)),
        name="ada",
    )(c, w_ada, b_ada)


def _proj_kernel(*refs, tile, dils, sample, nt):
    if sample:
        (x_ref, sc_ref, sh_ref, ng_ref, w_ref, qg_ref, kg_ref, cw_ref, bd_ref, ha_ref, hb_ref,
         qf_ref, kt_ref, vt_ref, u_ref, co_ref, u_scr) = refs
    else:
        (x_ref, sc_ref, sh_ref, ng_ref, w_ref, qg_ref, kg_ref, cw_ref, bd_ref,
         q0, k0, v0, q1, k1, v1, q2, k2, v2, kt0, vt0, kt1, vt1, kt2, vt2, u_ref, co_ref, slab, u_scr) = refs
        qkv_out = ((q0, k0, v0), (q1, k1, v1), (q2, k2, v2))
        tails = ((kt0, vt0), (kt1, vt1), (kt2, vt2))
        j = pl.program_id(1)

    x = x_ref[0]
    ms = jnp.mean(x * x, axis=-1, keepdims=True)
    h = x * lax.rsqrt(ms + RMS_EPS) * ng_ref[...]
    h = h * (1.0 + sc_ref[0]) + sh_ref[0]
    proj = _dot(h.astype(BF16), w_ref[...])

    bd = bd_ref[...]

    def headnorm(z, g):
        zz = z * z
        hi = zz.astype(BF16)
        lo = (zz - hi.astype(F32)).astype(BF16)
        msq = _dot(hi, bd) + _dot(lo, bd)
        return z * lax.rsqrt(msq + RMS_EPS) * g

    slab_i = 0
    for g in range(N_GROUPS):
        c0 = g * GROUP_W
        qn = headnorm(proj[:, c0:c0 + GROUP_W], qg_ref[...]) * (HEAD_DIM ** -0.5)
        kn = headnorm(proj[:, ATTN_W + c0:ATTN_W + c0 + GROUP_W], kg_ref[...])
        vv = proj[:, 2 * ATTN_W + c0:2 * ATTN_W + c0 + GROUP_W]
        if sample:
            qf_ref[0, :, c0:c0 + GROUP_W] = qn
            kt_ref[c0:c0 + GROUP_W, :] = kn.T
            vt_ref[c0:c0 + GROUP_W, :] = vv.T
            continue
        keep = min(WINDOWS[g], tile)

        @pl.when(j >= nt - max(WINDOWS[g] // tile, 1))
        def _(kn=kn, vv=vv, g=g, keep=keep):
            tails[g][0][0] = kn[tile - keep:, :].T
            tails[g][1][0] = vv[tile - keep:, :].T

        d = dils[g]
        for val, out in zip((qn, kn, vv), qkv_out[g]):
            if d == 1:
                out[0, 0] = val.astype(BF16)
                continue
            n = tile // d
            for half in range(GROUP_W // LANES):
                slab[slab_i] = val[:, half * LANES:(half + 1) * LANES]
                for r in range(d):
                    out[0, r, :, half * LANES:(half + 1) * LANES] = (
                        slab[slab_i, pl.ds(r, n, stride=d), :].astype(BF16))
                slab_i += 1

    base = 3 * ATTN_W
    u = proj[:, base + 2 * CONV_CH:base + 3 * CONV_CH] * proj[:, base:base + CONV_CH]
    gate_b = proj[:, base + CONV_CH:base + 2 * CONV_CH]
    if sample:
        u_scr[0:8, :] = jnp.zeros((8, CONV_CH), F32)
    else:
        @pl.when(j == 0)
        def _():
            u_scr[0:8, :] = jnp.zeros((8, CONV_CH), F32)

        @pl.when(j > 0)
        def _():
            u_scr[0:8, :] = u_scr[tile:tile + 8, :]

    u_scr[8:tile + 8, :] = u
    um1 = u_scr[7:tile + 7, :]
    um2 = u_scr[6:tile + 6, :]
    if sample:
        t = lax.broadcasted_iota(jnp.int32, (tile, CONV_CH), 0) % 4
        um1 = jnp.where(t >= 1, um1, 0.0) + hb_ref[...]
        um2 = jnp.where(t >= 2, um2, 0.0) + ha_ref[...]
    cw = cw_ref[...]
    conv = cw[0:1] * um2 + cw[1:2] * um1 + cw[2:3] * u
    u_ref[0] = u if sample else u[tile - 8:, :]
    co_ref[0] = (gate_b * conv).astype(BF16)


def _proj(x, sc, sh, ng, w_bf, qg, kg, cw, bd, hist, *, tile, dils, sample):
    B, S, _ = x.shape
    nt = S // tile
    tm = sc.shape[1]
    mod_spec = pl.BlockSpec((1, tm, D_MODEL), (lambda b, j: (b, j, 0)) if tm > 1 else (lambda b, j: (b, 0, 0)))
    const2 = lambda shape: pl.BlockSpec(shape, lambda b, j: (0, 0))
    in_specs = [pl.BlockSpec((1, tile, D_MODEL), lambda b, j: (b, j, 0)), mod_spec, mod_spec,
                const2((1, D_MODEL)), const2((D_MODEL, N_PROJ)), const2((1, GROUP_W)),
                const2((1, GROUP_W)), const2((3, CONV_CH)), const2((GROUP_W, GROUP_W))]
    args = [x, sc, sh, ng, w_bf, qg, kg, cw, bd]
    nat = lambda w: pl.BlockSpec((1, tile, w), lambda b, j: (b, j, 0))
    out_specs, out_shape = [], []
    scratch = []
    if sample:
        assert B == 1 and nt == 1
        in_specs += [pl.BlockSpec((tile, CONV_CH), lambda b, j: (j, 0))] * 2
        args += list(hist)
        out_specs += [nat(ATTN_W)] + [pl.BlockSpec((ATTN_W, tile), lambda b, j: (0, j))] * 2 + [nat(CONV_CH)]
        out_shape += [jax.ShapeDtypeStruct((B, S, ATTN_W), F32)]
        out_shape += [jax.ShapeDtypeStruct((ATTN_W, S), F32)] * 2
        out_shape += [jax.ShapeDtypeStruct((B, S, CONV_CH), F32)]
    else:
        for d in dils:
            for _ in range(3):
                out_specs.append(pl.BlockSpec((1, d, tile // d, GROUP_W), lambda b, j: (b, 0, j, 0)))
                out_shape.append(jax.ShapeDtypeStruct((B, d, S // d, GROUP_W), BF16))
        for w in WINDOWS:
            keep, first = min(w, tile), nt - max(w // tile, 1)
            for _ in range(2):
                out_specs.append(pl.BlockSpec((1, GROUP_W, keep),
                                              lambda b, j, first=first: (b, 0, jnp.maximum(j - first, 0))))
                out_shape.append(jax.ShapeDtypeStruct((B, GROUP_W, w), F32))
        out_specs.append(pl.BlockSpec((1, 8, CONV_CH), lambda b, j: (b, 0, 0)))
        out_shape.append(jax.ShapeDtypeStruct((B, 8, CONV_CH), F32))
        n_slabs = sum(3 * (GROUP_W // LANES) for d in dils if d > 1)
        scratch.append(pltpu.VMEM((n_slabs, tile, LANES), F32))
    out_specs.append(nat(CONV_CH))
    out_shape.append(jax.ShapeDtypeStruct((B, S, CONV_CH), BF16))
    scratch.append(pltpu.VMEM((tile + 8, CONV_CH), F32))
    return pl.pallas_call(
        functools.partial(_proj_kernel, tile=tile, dils=dils, sample=sample, nt=nt),
        grid=(B, nt),
        in_specs=in_specs,
        out_specs=out_specs,
        out_shape=out_shape,
        scratch_shapes=scratch,
        compiler_params=_params(("arbitrary", "arbitrary")),
        name="proj_sample" if sample else "proj_prompt",
    )(*args)


def _attn_kernel(q_ref, kp_ref, kc_ref, vp_ref, vc_ref, b_ref, o_ref, l_ref, s_scr, m_scr, o_scr, z_scr,
                 *, nq):
    first = pl.program_id(1) == 0
    low = lax.broadcasted_iota(jnp.int32, (BLK, LANES), 1) < HEAD_DIM
    ones = jnp.ones((BLK, LANES), BF16)
    chains = [(j, pair, sub) for j in range(nq) for pair in range(GROUP_W // LANES) for sub in range(2)]

    def operands(j, pair):
        rows = slice(j * BLK, (j + 1) * BLK)
        sl = slice(pair * LANES, (pair + 1) * LANES)
        if j == 0:
            return rows, sl, kp_ref[0, :, sl], vp_ref[0, :, sl]
        prows = slice((j - 1) * BLK, j * BLK)
        return rows, sl, kc_ref[0, prows, sl], vc_ref[0, prows, sl]

    for c, (j, pair, sub) in enumerate(chains):
        rows, sl, kp, _ = operands(j, pair)
        q = q_ref[0, rows, sl]
        qm = jnp.where(low if sub == 0 else ~low, q, jnp.zeros_like(q))
        hh = 2 * pair + sub
        sp = _dot_nt(qm, kp) + b_ref[hh, :, 0:BLK]
        if j == 0:
            sp = jnp.where(first, NEG, sp)
        sc = _dot_nt(qm, kc_ref[0, rows, sl]) + b_ref[hh, :, BLK:2 * BLK]
        s_scr[c, :, 0:BLK] = sp
        s_scr[c, :, BLK:2 * BLK] = sc
        m_scr[c] = jnp.broadcast_to(jnp.max(jnp.maximum(sp, sc), axis=-1, keepdims=True), (BLK, LANES))

    for c, (j, pair, sub) in enumerate(chains):
        rows, sl, _, vp = operands(j, pair)
        m = m_scr[c]
        pp = jnp.exp(s_scr[c, :, 0:BLK] - m).astype(BF16)
        pc = jnp.exp(s_scr[c, :, BLK:2 * BLK] - m).astype(BF16)
        o_scr[c] = _dot(pp, vp) + _dot(pc, vc_ref[0, rows, sl])
        z_scr[c] = _dot(pp, ones) + _dot(pc, ones)

    for c in range(0, len(chains), 2):
        j, pair, _ = chains[c]
        rows = slice(j * BLK, (j + 1) * BLK)
        sl = slice(pair * LANES, (pair + 1) * LANES)
        o_sub = [o_scr[c + sub] / z_scr[c + sub] for sub in range(2)]
        l_sub = [m_scr[c + sub] + jnp.log(z_scr[c + sub]) for sub in range(2)]
        o_ref[0, rows, sl] = jnp.where(low, o_sub[0], o_sub[1])
        l_ref[0, rows, sl] = jnp.where(low, l_sub[0], l_sub[1])


def _attn_prompt(q, k, v, bias):
    Z, L, _ = q.shape
    qt = min(L, 4 * BLK)
    nq = qt // BLK
    cur = pl.BlockSpec((1, qt, GROUP_W), lambda z, i: (z, i, 0))
    prev = pl.BlockSpec((1, BLK, GROUP_W), lambda z, i: (z, jnp.maximum(i * nq - 1, 0), 0))
    return pl.pallas_call(
        functools.partial(_attn_kernel, nq=nq),
        grid=(Z, L // qt),
        in_specs=[cur, prev, cur, prev, cur,
                  pl.BlockSpec((HEADS_PER_GROUP, BLK, 2 * BLK), lambda z, i: (0, 0, 0))],
        out_specs=[cur, cur],
        out_shape=[jax.ShapeDtypeStruct((Z, L, GROUP_W), F32)] * 2,
        scratch_shapes=[pltpu.VMEM((nq * HEADS_PER_GROUP, BLK, 2 * BLK), F32)]
        + [pltpu.VMEM((nq * HEADS_PER_GROUP, BLK, LANES), F32)] * 3,
        compiler_params=_params(("arbitrary", "arbitrary")),
        name="attn_prompt",
    )(q, k, k, v, v, bias)


def _attn_sample_kernel(q_ref, kc_ref, vc_ref, nk_ref, nv_ref, b1_ref, b2_ref,
                        ko_ref, vo_ref, o_ref, l_ref, *, bb, wb):
    lane = lax.broadcasted_iota(jnp.int32, (HEAD_DIM, LANES), 1)
    new_cols = lane >= LANES - 4
    per_tile = LANES // 4
    for b in range(bb):
        gb = pl.program_id(0) * bb + b
        tile_at = pl.ds(pl.multiple_of((gb // per_tile) * LANES, LANES), LANES)
        shift = (LANES - 4) - 4 * (gb % per_tile)
        for hh in range(HEADS_PER_GROUP):
            q = q_ref[b, hh]
            kt = kc_ref[b, hh]
            vt = vc_ref[b, hh]
            nkt = pltpu.roll(nk_ref[hh, :, tile_at], shift, axis=1)
            nvt = pltpu.roll(nv_ref[hh, :, tile_at], shift, axis=1)
            s1 = _dot(q, kt.astype(BF16)) + b1_ref[hh]
            s2 = _dot(q, nkt.astype(BF16)) + b2_ref[hh]
            m = jnp.maximum(jnp.max(s1, axis=-1, keepdims=True), jnp.max(s2, axis=-1, keepdims=True))
            p1 = jnp.exp(s1 - m)
            p2 = jnp.exp(s2 - m)
            z = jnp.sum(p1, axis=-1, keepdims=True) + jnp.sum(p2, axis=-1, keepdims=True)
            o = _dot_nt(p1.astype(BF16), vt.astype(BF16)) + _dot_nt(p2.astype(BF16), nvt.astype(BF16))
            o_ref[b, hh] = o / z
            l_ref[b, hh] = jnp.broadcast_to(m + jnp.log(z), (8, LANES))
            for src, new, dst in ((kt, nkt, ko_ref), (vt, nvt, vo_ref)):
                rolled = pltpu.roll(src, wb - 4, axis=1)
                if wb > LANES:
                    dst[b, hh, :, 0:wb - LANES] = rolled[:, 0:wb - LANES]
                dst[b, hh, :, wb - LANES:wb] = jnp.where(new_cols, new, rolled[:, wb - LANES:wb])


def _attn_sample(q, kc, vc, nk, nv, b1, b2, *, bb):
    nb, _, _, wb = kc.shape
    blk = lambda *tail: pl.BlockSpec((bb, HEADS_PER_GROUP) + tail, lambda i: (i, 0, 0, 0))
    const = lambda shape: pl.BlockSpec(shape, lambda i: (0, 0, 0))
    return pl.pallas_call(
        functools.partial(_attn_sample_kernel, bb=bb, wb=wb),
        grid=(nb // bb,),
        in_specs=[blk(8, HEAD_DIM), blk(HEAD_DIM, wb), blk(HEAD_DIM, wb), const(nk.shape), const(nv.shape),
                  const((HEADS_PER_GROUP, 8, wb)), const((HEADS_PER_GROUP, 8, LANES))],
        out_specs=[blk(HEAD_DIM, wb), blk(HEAD_DIM, wb), blk(8, HEAD_DIM), blk(8, LANES)],
        out_shape=[jax.ShapeDtypeStruct(kc.shape, F32), jax.ShapeDtypeStruct(kc.shape, F32),
                   jax.ShapeDtypeStruct((nb, HEADS_PER_GROUP, 8, HEAD_DIM), F32),
                   jax.ShapeDtypeStruct((nb, HEADS_PER_GROUP, 8, LANES), F32)],
        compiler_params=_params(("arbitrary",)),
        name="attn_sample",
    )(q, kc, vc, nk, nv, b1, b2)


def _first_max(v, ids, sentinel):
    m = jnp.max(v, axis=0, keepdims=True)
    idx = jnp.min(jnp.where(v == m, ids, sentinel), axis=0, keepdims=True)
    return m, ids == idx


def _oproj_kernel(x_ref, o0, o1, o2, l0, l1, l2, co_ref, g1_ref, sc_ref, sh_ref, ng_ref, wo_ref,
                  wr_ref, br_ref, x1_ref, xg_ref, gt_ref, posr_ref, post_ref, cnt_ref, slab, *, tile, dils):
    o_refs, l_refs = (o0, o1, o2), (l0, l1, l2)
    outs, lses = [], []
    slab_i = 0
    for g, d in enumerate(dils):
        if d == 1:
            outs.append(o_refs[g][0, 0])
            lses.append(l_refs[g][0, 0])
            continue
        n = tile // d
        for ref, dest in ((o_refs[g], outs), (l_refs[g], lses)):
            halves = []
            for half in range(GROUP_W // LANES):
                for r in range(d):
                    slab[slab_i, pl.ds(r, n, stride=d), :] = ref[0, r, :, half * LANES:(half + 1) * LANES]
                halves.append(slab[slab_i])
                slab_i += 1
            dest.append(jnp.concatenate(halves, axis=1))

    m = jnp.maximum(jnp.maximum(lses[0], lses[1]), lses[2])
    es = [jnp.exp(l - m) for l in lses]
    den = es[0] + es[1] + es[2]
    cat = [(outs[g] * (es[g] / den)).astype(BF16) for g in range(N_GROUPS)]
    cat.append(co_ref[0].astype(BF16))
    cat = jnp.concatenate(cat, axis=1)
    x1 = x_ref[0] + g1_ref[0] * _dot(cat, wo_ref[...])
    x1_ref[0] = x1

    ms = jnp.mean(x1 * x1, axis=-1, keepdims=True)
    h2 = x1 * lax.rsqrt(ms + RMS_EPS) * ng_ref[...]
    h2 = h2 * (1.0 + sc_ref[0]) + sh_ref[0]
    hh = h2.astype(BF16)
    xg_ref[0, :, 0:D_MODEL] = hh

    hl = (h2 - hh.astype(F32)).astype(BF16)
    wr = wr_ref[...]
    wh = wr.astype(BF16)
    wl = (wr - wh.astype(F32)).astype(BF16)
    logits = _dot_nt(wh, hh) + _dot_nt(wh, hl) + _dot_nt(wl, hh)
    scores = 1.0 / (1.0 + jnp.exp(-logits))
    sel = scores + br_ref[...]

    ids = lax.broadcasted_iota(jnp.int32, (GROUP_SIZE, tile), 0)
    ninf = -jnp.inf
    sel_g = [sel[g * GROUP_SIZE:(g + 1) * GROUP_SIZE] for g in range(N_ROUTE_GROUPS)]
    gscore = jnp.zeros((N_ROUTE_GROUPS, tile), F32)
    for g in range(N_ROUTE_GROUPS):
        m1, oh = _first_max(sel_g[g], ids, GROUP_SIZE)
        m2 = jnp.max(jnp.where(oh, ninf, sel_g[g]), axis=0, keepdims=True)
        gscore = jnp.where(ids == g, m1 + m2, gscore)
    gsel = jnp.zeros((N_ROUTE_GROUPS, tile), F32)
    for _ in range(TOPK_GROUPS):
        _, oh = _first_max(gscore, ids, N_ROUTE_GROUPS)
        gsel = jnp.where(oh, 1.0, gsel)
        gscore = jnp.where(oh, ninf, gscore)
    cand = [jnp.where(gsel[g:g + 1] > 0.0, sel_g[g], ninf) for g in range(N_ROUTE_GROUPS)]
    chosen = [jnp.zeros((GROUP_SIZE, tile), F32) for _ in range(N_ROUTE_GROUPS)]
    picks = []
    for _ in range(TOP_K):
        mx = cand[0]
        for g in range(1, N_ROUTE_GROUPS):
            mx = jnp.maximum(mx, cand[g])
        mx = jnp.max(mx, axis=0, keepdims=True)
        idx = jnp.where(cand[0] == mx, ids, N_EXPERTS)
        for g in range(1, N_ROUTE_GROUPS):
            idx = jnp.minimum(idx, jnp.where(cand[g] == mx, ids + g * GROUP_SIZE, N_EXPERTS))
        idx = jnp.min(idx, axis=0, keepdims=True)
        ohs = [(ids + g * GROUP_SIZE) == idx for g in range(N_ROUTE_GROUPS)]
        picks.append(ohs)
        for g in range(N_ROUTE_GROUPS):
            chosen[g] = jnp.where(ohs[g], 1.0, chosen[g])
            cand[g] = jnp.where(ohs[g], ninf, cand[g])
    wts = [jnp.where(chosen[g] > 0.0, scores[g * GROUP_SIZE:(g + 1) * GROUP_SIZE], 0.0)
           for g in range(N_ROUTE_GROUPS)]
    wsum = jnp.sum(wts[0], axis=0, keepdims=True)
    for g in range(1, N_ROUTE_GROUPS):
        wsum = wsum + jnp.sum(wts[g], axis=0, keepdims=True)
    gates_t = jnp.concatenate([w / wsum * ROUTED_SCALE for w in wts], axis=0)
    gt_ref[0] = jnp.concatenate([gates_t, jnp.zeros((LANES - N_EXPERTS, tile), F32)], axis=0).T
    g_hi = gates_t.astype(BF16).astype(F32)
    xg_ref[0, :, D_MODEL:D_MODEL + LANES] = jnp.concatenate([g_hi, gates_t - g_hi], axis=0).T.astype(BF16)

    er = lax.broadcasted_iota(jnp.int32, (N_EXPERTS, N_EXPERTS), 0)
    ec = lax.broadcasted_iota(jnp.int32, (N_EXPERTS, N_EXPERTS), 1)
    lower = jnp.where(ec < er, 1.0, 0.0).astype(BF16)
    tr = lax.broadcasted_iota(jnp.int32, (SUB, SUB), 0)
    tc = lax.broadcasted_iota(jnp.int32, (SUB, SUB), 1)
    upper = jnp.where(tr < tc, 1.0, 0.0).astype(BF16)
    for u in range(tile // SUB):
        cols = slice(u * SUB, (u + 1) * SUB)
        ch = jnp.concatenate([c[:, cols] for c in chosen], axis=0)
        cnt = jnp.sum(ch, axis=1, keepdims=True)
        cpad = jnp.broadcast_to(jnp.ceil(cnt / CHUNK) * CHUNK, (N_EXPERTS, LANES))
        cnt_ref[u] = cpad
        seg0 = _dot(lower, cpad.astype(BF16))[:, 0:1]
        slot = seg0 + _dot(ch.astype(BF16), upper)
        rows = []
        for k in range(TOP_K):
            acc = jnp.where(picks[k][0][:, cols], slot[0:GROUP_SIZE], 0.0)
            for g in range(1, N_ROUTE_GROUPS):
                acc = acc + jnp.where(picks[k][g][:, cols], slot[g * GROUP_SIZE:(g + 1) * GROUP_SIZE], 0.0)
            rows.append(jnp.sum(acc, axis=0, keepdims=True))
        posr = jnp.concatenate(rows, axis=0)
        posr_ref[u] = posr
        post_ref[0, cols, :] = jnp.concatenate([posr, jnp.zeros((LANES - TOP_K, SUB), F32)], axis=0).T


def _oproj(x, o_list, l_list, co, g1, sc, sh, ng, wo_bf, wr_t, br, *, tile, dils):
    B, S, _ = x.shape
    tm = g1.shape[1]
    mod_spec = pl.BlockSpec((1, tm, D_MODEL), (lambda b, j: (b, j, 0)) if tm > 1 else (lambda b, j: (b, 0, 0)))
    const2 = lambda shape: pl.BlockSpec(shape, lambda b, j: (0, 0))
    nat = lambda w: pl.BlockSpec((1, tile, w), lambda b, j: (b, j, 0))
    dspec = [pl.BlockSpec((1, d, tile // d, GROUP_W), lambda b, j: (b, 0, j, 0)) for d in dils]
    n_slabs = sum(2 * (GROUP_W // LANES) for d in dils if d > 1)
    nsub_t = tile // SUB
    return pl.pallas_call(
        functools.partial(_oproj_kernel, tile=tile, dils=dils),
        grid=(B, S // tile),
        in_specs=[nat(D_MODEL)] + dspec + dspec + [nat(CONV_CH), mod_spec, mod_spec, mod_spec,
                  const2((1, D_MODEL)), const2((D_MODEL, D_MODEL)), const2((N_EXPERTS, D_MODEL)),
                  const2((N_EXPERTS, 1))],
        out_specs=[nat(D_MODEL), nat(XG_W), nat(LANES),
                   pl.BlockSpec((nsub_t, TOP_K, SUB), lambda b, j: (b * (S // tile) + j, 0, 0)),
                   nat(LANES),
                   pl.BlockSpec((nsub_t, N_EXPERTS, LANES), lambda b, j: (b * (S // tile) + j, 0, 0))],
        out_shape=[jax.ShapeDtypeStruct((B, S, D_MODEL), F32), jax.ShapeDtypeStruct((B, S, XG_W), BF16),
                   jax.ShapeDtypeStruct((B, S, LANES), F32),
                   jax.ShapeDtypeStruct((B * S // SUB, TOP_K, SUB), F32),
                   jax.ShapeDtypeStruct((B, S, LANES), F32),
                   jax.ShapeDtypeStruct((B * S // SUB, N_EXPERTS, LANES), F32)],
        scratch_shapes=[pltpu.VMEM((max(n_slabs, 1), tile, LANES), F32)],
        compiler_params=_params(("arbitrary", "arbitrary")),
        name="oproj",
    )(x, *o_list, *l_list, co, g1, sc, sh, ng, wo_bf, wr_t, br)


def _moe_kernel(h_ref, gt_ref, x1_ref, g2_ref, wg_ref, wu_ref, wd_ref, wgs_ref, wus_ref, wds_ref,
                y_ref, acc, *, tile):
    e = pl.program_id(2)
    h = h_ref[0]

    @pl.when(e == 0)
    def _():
        a = _dot(h, wgs_ref[...].astype(BF16))
        b = _dot(h, wus_ref[...].astype(BF16))
        acc[...] = _dot((_silu(a) * b).astype(BF16), wds_ref[...].astype(BF16))

    a = _dot(h, wg_ref[0].astype(BF16))
    b = _dot(h, wu_ref[0].astype(BF16))
    lane = lax.broadcasted_iota(jnp.int32, (tile, LANES), 1)
    gcol = jnp.sum(jnp.where(lane == e, gt_ref[0], 0.0), axis=1, keepdims=True)
    hc = _silu(a) * b * gcol
    acc[...] += _dot(hc.astype(BF16), wd_ref[0].astype(BF16))

    @pl.when(e == N_EXPERTS - 1)
    def _():
        y_ref[0] = x1_ref[0] + g2_ref[0] * acc[...]


def _moe(h2, gates, x1, g2, wg, wu, wd, wgs, wus, wds, *, tile):
    B, S, _ = h2.shape
    tm = g2.shape[1]
    mod_spec = pl.BlockSpec((1, tm, D_MODEL),
                            (lambda b, j, e: (b, j, 0)) if tm > 1 else (lambda b, j, e: (b, 0, 0)))
    nat = lambda w: pl.BlockSpec((1, tile, w), lambda b, j, e: (b, j, 0))
    const2 = lambda shape: pl.BlockSpec(shape, lambda b, j, e: (0, 0))
    return pl.pallas_call(
        functools.partial(_moe_kernel, tile=tile),
        grid=(B, S // tile, N_EXPERTS),
        in_specs=[nat(D_MODEL), nat(LANES), nat(D_MODEL), mod_spec,
                  pl.BlockSpec((1, D_MODEL, D_EXPERT), lambda b, j, e: (e, 0, 0)),
                  pl.BlockSpec((1, D_MODEL, D_EXPERT), lambda b, j, e: (e, 0, 0)),
                  pl.BlockSpec((1, D_EXPERT, D_MODEL), lambda b, j, e: (e, 0, 0)),
                  const2((D_MODEL, D_EXPERT)), const2((D_MODEL, D_EXPERT)), const2((D_EXPERT, D_MODEL))],
        out_specs=nat(D_MODEL),
        out_shape=jax.ShapeDtypeStruct((B, S, D_MODEL), F32),
        scratch_shapes=[pltpu.VMEM((tile, D_MODEL), F32)],
        compiler_params=_params(("arbitrary", "arbitrary", "arbitrary")),
        name="moe",
    )(h2, gates, x1, g2, wg, wu, wd, wgs, wus, wds)


FFN_TM = 1024
SUB_ROWS = SUB * TOP_K + N_EXPERTS * CHUNK
N_CHUNKS = SUB_ROWS // CHUNK
ZERO_CHUNK = N_CHUNKS
TABLE_W = 256
MXU_ROWS = 256
BLOCKS_PER_TRIP = 2


def _slot_onehot(first, pos_list, axis_iota):
    shape = axis_iota.shape
    hit = axis_iota == jnp.broadcast_to(pos_list[0] - first, shape).astype(BF16)
    for p in pos_list[1:]:
        hit = hit | (axis_iota == jnp.broadcast_to(p - first, shape).astype(BF16))
    return jnp.where(hit, jnp.ones(shape, BF16), jnp.zeros(shape, BF16))


def _chunk_copy(src, src_chunk, dst, dst_chunk, sem):
    return pltpu.make_async_copy(src.at[src_chunk], dst.at[dst_chunk], sem)


def _for_each(n, body):
    def two(t, carry):
        body(2 * t)
        body(2 * t + 1)
        return carry

    lax.fori_loop(0, n // 2, two, 0)

    @pl.when(n % 2 == 1)
    def _():
        body(n - 1)


def _dispatch_kernel(nblk_ref, ncopy_ref, ntile_ref, src_ref, dst_ref, posr_ref, xg_ref, xs_hbm, xs_scr,
                     zero_scr, sem, tail_sem, *, nsub, n_tiles_max):
    s = pl.program_id(0)

    tile_chunks = FFN_TM // CHUNK
    blk_chunks = MXU_ROWS // CHUNK
    zero_scr[...] = jnp.zeros((tile_chunks, CHUNK, XG_W), BF16)
    tail_blocks = [ntile_ref[0] + s + r * nsub for r in range(-(-(n_tiles_max) // nsub))]

    def tail_copy(t):
        return pltpu.make_async_copy(
            zero_scr, xs_hbm.at[pl.ds(pl.multiple_of(t * tile_chunks, tile_chunks), tile_chunks)], tail_sem)

    for t in tail_blocks:
        @pl.when(t < n_tiles_max)
        def _():
            tail_copy(t).start()

    xg = xg_ref[...]
    pos = [posr_ref[0, k:k + 1, :] for k in range(TOP_K)]
    rows = lax.broadcasted_iota(jnp.int32, (MXU_ROWS, SUB), 0).astype(BF16)
    xs_scr[ZERO_CHUNK] = jnp.zeros((CHUNK, XG_W), BF16)

    def sort_blocks(i, carry):
        for j in range(BLOCKS_PER_TRIP):
            blk = i * BLOCKS_PER_TRIP + j
            onehot = _slot_onehot((blk * MXU_ROWS).astype(F32), pos, rows)
            sorted_rows = _dot(onehot, xg).astype(BF16)
            xs_scr[pl.ds(pl.multiple_of(blk * blk_chunks, blk_chunks), blk_chunks)] = (
                sorted_rows.reshape(blk_chunks, CHUNK, XG_W))
        return carry

    lax.fori_loop(0, nblk_ref[s], sort_blocks, 0)
    n = ncopy_ref[s]

    _for_each(n, lambda i: _chunk_copy(xs_scr, src_ref[0, 0, i], xs_hbm, dst_ref[0, 0, i], sem).start())
    _for_each(n, lambda i: _chunk_copy(xs_scr, 0, xs_hbm, 0, sem).wait())
    for t in tail_blocks:
        @pl.when(t < n_tiles_max)
        def _():
            tail_copy(t).wait()


def _dispatch(nblk, ncopy, n_tiles, src, dst, posr, xg, n_rows):
    nsub = posr.shape[0]
    tab = pl.BlockSpec((1, 1, TABLE_W), lambda s, *_: (s, 0, 0), memory_space=pltpu.SMEM)
    return pl.pallas_call(
        functools.partial(_dispatch_kernel, nsub=nsub, n_tiles_max=n_rows // FFN_TM),
        grid_spec=pltpu.PrefetchScalarGridSpec(
            num_scalar_prefetch=3, grid=(nsub,),
            in_specs=[tab, tab, pl.BlockSpec((1, TOP_K, SUB), lambda s, *_: (s, 0, 0)),
                      pl.BlockSpec((SUB, XG_W), lambda s, *_: (s, 0))],
            out_specs=pl.BlockSpec(memory_space=pl.ANY),
            scratch_shapes=[pltpu.VMEM((N_CHUNKS + 1, CHUNK, XG_W), BF16),
                            pltpu.VMEM((FFN_TM // CHUNK, CHUNK, XG_W), BF16),
                            pltpu.SemaphoreType.DMA(()), pltpu.SemaphoreType.DMA(())]),
        out_shape=jax.ShapeDtypeStruct((n_rows // CHUNK, CHUNK, XG_W), BF16),
        compiler_params=_params(("arbitrary",)),
        name="moe_dispatch",
    )(nblk, ncopy, n_tiles, src, dst, posr, xg).reshape(n_rows, XG_W)


def _ffn_kernel(te_ref, nt_ref, xs_ref, wg_ref, wu_ref, wd_ref, ys_ref, wg_s, wu_s, wd_s):
    i = pl.program_id(0)

    @pl.when(i >= nt_ref[0])
    def _():
        ys_ref[...] = jnp.zeros((FFN_TM, D_MODEL), BF16)

    @pl.when(i < nt_ref[0])
    def _():
        e = te_ref[i]

        @pl.when((i == 0) | (e != te_ref[jnp.maximum(i - 1, 0)]))
        def _():
            wg_s[...] = wg_ref[0].astype(BF16)
            wu_s[...] = wu_ref[0].astype(BF16)
            wd_s[...] = wd_ref[0].astype(BF16)

        x = xs_ref[:, 0:D_MODEL]
        g = xs_ref[:, D_MODEL:XG_W].astype(F32)
        lane = lax.broadcasted_iota(jnp.int32, (FFN_TM, LANES), 1)
        gcol = jnp.sum(jnp.where((lane == e) | (lane == e + N_EXPERTS), g, 0.0), axis=1, keepdims=True)
        a = _dot(x, wg_s[...])
        b = _dot(x, wu_s[...])
        hc = _silu(a) * b * gcol
        ys_ref[...] = _dot(hc.astype(BF16), wd_s[...]).astype(BF16)


def _ffn(tile_expert, n_tiles, xs, wg, wu, wd):
    n_rows = xs.shape[0]
    row = lambda i, te, nt: (jnp.minimum(i, nt[0] - 1), 0)
    wspec = lambda shape: pl.BlockSpec((1,) + shape, lambda i, te, nt: (te[i], 0, 0))
    return pl.pallas_call(
        _ffn_kernel,
        grid_spec=pltpu.PrefetchScalarGridSpec(
            num_scalar_prefetch=2, grid=(n_rows // FFN_TM,),
            in_specs=[pl.BlockSpec((FFN_TM, XG_W), row), wspec((D_MODEL, D_EXPERT)),
                      wspec((D_MODEL, D_EXPERT)), wspec((D_EXPERT, D_MODEL))],
            out_specs=pl.BlockSpec((FFN_TM, D_MODEL), lambda i, te, nt: (i, 0)),
            scratch_shapes=[pltpu.VMEM((D_MODEL, D_EXPERT), BF16), pltpu.VMEM((D_MODEL, D_EXPERT), BF16),
                            pltpu.VMEM((D_EXPERT, D_MODEL), BF16)]),
        out_shape=jax.ShapeDtypeStruct((n_rows, D_MODEL), BF16),
        compiler_params=_params(("arbitrary",)),
        name="moe_ffn",
    )(tile_expert, n_tiles, xs, wg, wu, wd)


def _combine_kernel(nblk_ref, ncopy_ref, src_ref, post_ref, xg_ref, x1_ref, g2_ref, wgs_ref, wus_ref,
                    wds_ref, ys_hbm, y_ref, ys_scr, wgs_s, wus_s, wds_s, sem):
    s = pl.program_id(0)
    n = ncopy_ref[s]

    _for_each(n, lambda i: _chunk_copy(ys_hbm, src_ref[0, 0, i], ys_scr, i, sem).start())

    @pl.when(s == 0)
    def _():
        wgs_s[...] = wgs_ref[...].astype(BF16)
        wus_s[...] = wus_ref[...].astype(BF16)
        wds_s[...] = wds_ref[...].astype(BF16)

    h = xg_ref[:, 0:D_MODEL]
    a = _dot(h, wgs_s[...])
    b = _dot(h, wus_s[...])
    shared = _dot((_silu(a) * b).astype(BF16), wds_s[...])

    post = post_ref[...]
    pos = [jnp.broadcast_to(post[:, k:k + 1], (SUB, MXU_ROWS)) for k in range(TOP_K)]
    lanes = lax.broadcasted_iota(jnp.int32, (SUB, MXU_ROWS), 1).astype(BF16)
    _for_each(n, lambda i: _chunk_copy(ys_hbm, 0, ys_scr, 0, sem).wait())

    blk_chunks = MXU_ROWS // CHUNK

    def gather_blocks(i, acc):
        for j in range(BLOCKS_PER_TRIP):
            blk = i * BLOCKS_PER_TRIP + j
            onehot = _slot_onehot((blk * MXU_ROWS).astype(F32), pos, lanes)
            rows = ys_scr[pl.ds(pl.multiple_of(blk * blk_chunks, blk_chunks), blk_chunks)]
            acc = acc + _dot(onehot, rows.reshape(MXU_ROWS, D_MODEL))
        return acc

    routed = lax.fori_loop(0, nblk_ref[s], gather_blocks, jnp.zeros((SUB, D_MODEL), F32))
    y_ref[...] = x1_ref[...] + g2_ref[0] * (routed + shared)


def _combine(nblk, ncopy, src, post, xg, x1, g2, wgs, wus, wds, ys):
    n_tok = x1.shape[0]
    nsub = n_tok // SUB
    per_b = nsub // g2.shape[0]
    tab = pl.BlockSpec((1, 1, TABLE_W), lambda s, *_: (s, 0, 0), memory_space=pltpu.SMEM)
    const2 = lambda shape: pl.BlockSpec(shape, lambda s, *_: (0, 0))
    return pl.pallas_call(
        _combine_kernel,
        grid_spec=pltpu.PrefetchScalarGridSpec(
            num_scalar_prefetch=2, grid=(nsub,),
            in_specs=[tab, pl.BlockSpec((SUB, LANES), lambda s, *_: (s, 0)),
                      pl.BlockSpec((SUB, XG_W), lambda s, *_: (s, 0)),
                      pl.BlockSpec((SUB, D_MODEL), lambda s, *_: (s, 0)),
                      pl.BlockSpec((1, 1, D_MODEL), lambda s, *_: (s // per_b, 0, 0)),
                      const2((D_MODEL, D_EXPERT)), const2((D_MODEL, D_EXPERT)), const2((D_EXPERT, D_MODEL)),
                      pl.BlockSpec(memory_space=pl.ANY)],
            out_specs=pl.BlockSpec((SUB, D_MODEL), lambda s, *_: (s, 0)),
            scratch_shapes=[pltpu.VMEM((N_CHUNKS, CHUNK, D_MODEL), BF16), pltpu.VMEM((D_MODEL, D_EXPERT), BF16),
                            pltpu.VMEM((D_MODEL, D_EXPERT), BF16), pltpu.VMEM((D_EXPERT, D_MODEL), BF16),
                            pltpu.SemaphoreType.DMA(())]),
        out_shape=jax.ShapeDtypeStruct((n_tok, D_MODEL), F32),
        compiler_params=_params(("arbitrary",)),
        name="moe_combine",
    )(nblk, ncopy, src, post, xg, x1, g2, wgs, wus, wds, ys.reshape(-1, CHUNK, D_MODEL))


def _route_tables(cnt):
    nsub = cnt.shape[0]
    i32 = jnp.int32
    seg_end = jnp.cumsum(cnt, axis=1)
    seg_start = seg_end - cnt
    tot = jnp.sum(cnt, axis=0)
    tot_al = (tot + FFN_TM - 1) // FFN_TM * FFN_TM
    reg_end = jnp.cumsum(tot_al)
    reg_start = reg_end - tot_al
    base = reg_start[None, :] + jnp.cumsum(cnt, axis=0) - cnt
    n_real = seg_end[:, -1] // CHUNK
    k = jnp.arange(TABLE_W, dtype=i32)
    owner = jnp.sum((seg_end[:, None, :] // CHUNK) <= k[None, :, None], axis=2)
    owner = jnp.minimum(owner, N_EXPERTS - 1)
    delta = (base - seg_start) // CHUNK
    e_ids = jnp.arange(N_EXPERTS, dtype=i32)
    real_dst = jnp.sum(jnp.where(owner[:, :, None] == e_ids, delta[:, None, :], 0), axis=2) + k[None, :]
    fill_n = (tot_al - tot) // CHUNK
    fill_dst0 = (reg_start + tot) // CHUNK
    src = jnp.broadcast_to(k[None, :], (nsub, TABLE_W))
    dst = real_dst
    ncopy = n_real
    for r in range(-(-N_EXPERTS // nsub)):
        e_of = jnp.arange(nsub, dtype=i32) + r * nsub
        ok = e_of < N_EXPERTS
        e_cl = jnp.minimum(e_of, N_EXPERTS - 1)
        fn = jnp.where(ok, fill_n[e_cl], 0)
        j = k[None, :] - ncopy[:, None]
        is_fill = (j >= 0) & (j < fn[:, None])
        src = jnp.where(is_fill, ZERO_CHUNK, src)
        dst = jnp.where(is_fill, fill_dst0[e_cl][:, None] + j, dst)
        ncopy = ncopy + fn
    trip_rows = BLOCKS_PER_TRIP * MXU_ROWS
    disp_nblk = (n_real * CHUNK + trip_rows - 1) // trip_rows
    comb_n = disp_nblk * (trip_rows // CHUNK)
    comb_src = jnp.where(k[None, :] < n_real[:, None], real_dst, real_dst[:, 0:1])
    n_tiles = reg_end[-1] // FFN_TM
    t = jnp.arange((nsub * SUB_ROWS + N_EXPERTS * FFN_TM) // FFN_TM, dtype=i32)
    tile_expert = jnp.minimum(jnp.sum((reg_end[None, :] // FFN_TM) <= t[:, None], axis=1), N_EXPERTS - 1)
    as3 = lambda a: a.astype(i32).reshape(nsub, 1, TABLE_W)
    return (disp_nblk.astype(i32), ncopy.astype(i32), as3(src), as3(dst), comb_n.astype(i32), as3(comb_src),
            tile_expert.astype(i32), n_tiles.astype(i32).reshape(1))


def _moe_sparse(xg, post, posr, cnt, x1, g2, wg, wu, wd, wgs, wus, wds):
    B, S, _ = x1.shape
    n_tok = B * S
    nsub = n_tok // SUB
    (disp_nblk, disp_n, disp_src, disp_dst, comb_n, comb_src, tile_expert, n_tiles) = _route_tables(
        cnt[:, :, 0].astype(jnp.int32))
    n_rows = nsub * SUB_ROWS + N_EXPERTS * FFN_TM
    xg2 = xg.reshape(n_tok, XG_W)
    xs = _dispatch(disp_nblk, disp_n, n_tiles, disp_src, disp_dst, posr, xg2, n_rows)
    ys = _ffn(tile_expert, n_tiles, xs, wg, wu, wd)
    y = _combine(disp_nblk, comb_n, comb_src, post.reshape(n_tok, LANES), xg2, x1.reshape(n_tok, D_MODEL),
                 g2, wgs, wus, wds, ys)
    return y.reshape(B, S, D_MODEL)


def _t5_bucket(dist):
    max_exact = N_BUCKETS // 2
    df = jnp.maximum(dist, 1).astype(F32)
    large = max_exact + (jnp.log(df / max_exact) / math.log(MAX_DISTANCE / max_exact)
                         * (N_BUCKETS - max_exact)).astype(jnp.int32)
    large = jnp.minimum(large, N_BUCKETS - 1)
    return jnp.where(dist < max_exact, dist, large)


def _step_bias(rel_bias, g):
    dist = jnp.arange(N_KEYS, dtype=jnp.int32) * DILATIONS[g]
    cols = rel_bias[:, g * HEADS_PER_GROUP:(g + 1) * HEADS_PER_GROUP]
    return cols[_t5_bucket(dist)].T.astype(F32)


def _prompt_bias(bias_k):
    h = bias_k.shape[0]
    n = 3 * BLK
    row = jnp.concatenate([bias_k[:, ::-1], jnp.full((h, n - N_KEYS), NEG, F32)], axis=1)
    t = jnp.tile(row, (1, BLK))[:, :BLK * (n - 1)].reshape(h, BLK, n - 1)
    return t[:, :, :2 * BLK]


def _sample_bias(bias_k, wb, d):
    h = bias_k.shape[0]
    rev = bias_k[:, :0:-1]
    rows = []
    for t in range(4):
        if d == 1:
            rows.append(jnp.concatenate([jnp.full((h, t), NEG, F32), rev[:, :wb - t]], axis=1))
        else:
            r = jnp.arange(d)[None, None, :]
            rows.append(jnp.where(r == t, rev[:, :, None], NEG).reshape(h, wb))
    b1 = jnp.concatenate([jnp.stack(rows, axis=1), jnp.zeros((h, 4, wb), F32)], axis=1)
    rows = []
    for t in range(4):
        cols = [bias_k[:, (t - tn) // d] if (t >= tn and (t - tn) % d == 0) else jnp.full((h,), NEG, F32)
                for tn in range(4)]
        rows.append(jnp.concatenate([jnp.full((h, LANES - 4), NEG, F32), jnp.stack(cols, axis=1)], axis=1))
    b2 = jnp.concatenate([jnp.stack(rows, axis=1), jnp.zeros((h, 4, LANES), F32)], axis=1)
    return b1, b2


def _layer(x, mods, hist, caches, rel_bias, weights, *, sample):
    (norm1_g, norm2_g, w_in_bf, qg, kg, conv_w, bd, w_o_bf, w_r_t, b_r, w_gate_e, w_up_e, w_down_e,
     w_gate_s, w_up_s, w_down_s) = weights
    sh1, sc1, g1, sh2, sc2, g2 = mods
    B, S, _ = x.shape
    tile = 512
    dils = (1, 1, 1) if sample else DILATIONS
    res = _proj(x, sc1, sh1, norm1_g, w_in_bf, qg, kg, conv_w, bd, hist, tile=tile, dils=dils, sample=sample)
    states = []
    o_list, l_list = [], []
    if sample:
        qf, kt, vt, u, co = res
        nb = S // 4
        for g in range(N_GROUPS):
            d, wb = DILATIONS[g], WINDOWS[g]
            kc, vc = caches[g]
            bias_k = _step_bias(rel_bias, g)
            b1, b2 = _sample_bias(bias_k, wb, d)
            q = qf[0, :, g * GROUP_W:(g + 1) * GROUP_W].reshape(nb, 4, HEADS_PER_GROUP, HEAD_DIM)
            q = jnp.pad(q.transpose(0, 2, 1, 3), ((0, 0), (0, 0), (0, 4), (0, 0))).astype(BF16)
            new_t = lambda a: a[g * GROUP_W:(g + 1) * GROUP_W].reshape(HEADS_PER_GROUP, HEAD_DIM, S)
            ko, vo, o, lse = _attn_sample(q, kc, vc, new_t(kt), new_t(vt), b1, b2,
                                          bb={128: 8, 512: 4, 2048: 1}[wb])
            states += [ko.transpose(0, 3, 1, 2)[None], vo.transpose(0, 3, 1, 2)[None]]
            o = o[:, :, :4].transpose(0, 2, 1, 3).reshape(1, 1, S, GROUP_W)
            lse = jnp.broadcast_to(lse[:, :, :4, :1], (nb, HEADS_PER_GROUP, 4, HEAD_DIM))
            o_list.append(o)
            l_list.append(lse.transpose(0, 2, 1, 3).reshape(1, 1, S, GROUP_W))
        states.append(u.reshape(nb, 4, CONV_CH)[:, 2:][None])
    else:
        qkv, tails, (u, co) = res[:9], res[9:15], res[15:]
        for g in range(N_GROUPS):
            d, w = DILATIONS[g], WINDOWS[g]
            q, k, v = (a.reshape(B * d, S // d, GROUP_W) for a in qkv[3 * g:3 * g + 3])
            o, lse = _attn_prompt(q, k, v, _prompt_bias(_step_bias(rel_bias, g)))
            o_list.append(o.reshape(B, d, S // d, GROUP_W))
            l_list.append(lse.reshape(B, d, S // d, GROUP_W))
            keep = lambda a: a.reshape(B, HEADS_PER_GROUP, HEAD_DIM, w).transpose(0, 3, 1, 2)[None]
            states += [keep(tails[2 * g]), keep(tails[2 * g + 1])]
        states.append(u[:, 6:][None])
    x1, xg, gates, posr, post, cnt = _oproj(x, o_list, l_list, co, g1, sc2, sh2, norm2_g, w_o_bf, w_r_t, b_r,
                                            tile=tile, dils=dils)
    if sample:
        y = _moe(xg, gates, x1, g2, w_gate_e, w_up_e, w_down_e, w_gate_s, w_up_s, w_down_s, tile=tile)
    else:
        y = _moe_sparse(xg, post, posr, cnt, x1, g2, w_gate_e, w_up_e, w_down_e, w_gate_s, w_up_s, w_down_s)
    return y, states


def kernel(x_prompt, x_sample, c_prompt, c_sample, cache_k_w128, cache_v_w128, cache_k_w512, cache_v_w512, cache_k_w2048, cache_v_w2048, state_conv, rel_bias, norm1_g, norm2_g, w_ada, b_ada, w_in, q_norm_g, k_norm_g, conv_w, w_o, w_router, b_router, w_gate_e, w_up_e, w_down_e, w_gate_s, w_up_s, w_down_s):
    B = x_prompt.shape[0]
    DB, T = x_sample.shape[:2]
    n_c = B + DB
    c_all = jnp.pad(jnp.concatenate([c_prompt, c_sample], axis=0), ((0, (-n_c) % 8), (0, 0)))
    mod = _ada(c_all, w_ada[0], b_ada)
    chunks = [mod[:, i * D_MODEL:(i + 1) * D_MODEL] for i in range(6)]
    mods_p = [c[:B].reshape(B, 1, D_MODEL) for c in chunks]
    mods_s = [jnp.repeat(c[B:n_c], T, axis=0).reshape(1, DB * T, D_MODEL) for c in chunks]

    eye = jnp.arange(GROUP_W) // HEAD_DIM
    bd = jnp.where(eye[:, None] == eye[None, :], 1.0 / HEAD_DIM, 0.0).astype(BF16)
    weights = (norm1_g, norm2_g, w_in[0].astype(BF16),
               jnp.tile(q_norm_g, (1, HEADS_PER_GROUP)), jnp.tile(k_norm_g, (1, HEADS_PER_GROUP)),
               conv_w[0], bd, w_o[0].astype(BF16), w_router[0].T, b_router.reshape(N_EXPERTS, 1),
               w_gate_e[0], w_up_e[0], w_down_e[0], w_gate_s[0], w_up_s[0], w_down_s[0])

    yp, st_p = _layer(x_prompt, mods_p, None, None, rel_bias, weights, sample=False)

    s0, s1 = state_conv[0, :, 0], state_conv[0, :, 1]
    zero = jnp.zeros_like(s0)
    hist_a = jnp.stack([s0, s1, zero, zero], axis=1).reshape(DB * T, CONV_CH)
    hist_b = jnp.stack([s1, zero, zero, zero], axis=1).reshape(DB * T, CONV_CH)
    caches = [(ck[0].transpose(0, 2, 3, 1), cv[0].transpose(0, 2, 3, 1))
              for ck, cv in ((cache_k_w128, cache_v_w128), (cache_k_w512, cache_v_w512),
                             (cache_k_w2048, cache_v_w2048))]
    ys, st_s = _layer(x_sample.reshape(1, DB * T, D_MODEL), mods_s, (hist_a, hist_b), caches, rel_bias,
                      weights, sample=True)
    return (yp, ys.reshape(DB, T, D_MODEL), *st_p, *st_s)
```

```python
import functools
import math

import jax
import jax.numpy as jnp
from jax import lax
from jax.experimental import pallas as pl
from jax.experimental.pallas import tpu as pltpu

D_MODEL = 1024
HEAD_DIM = 64
HEADS_PER_GROUP = 4
GROUP_W = HEADS_PER_GROUP * HEAD_DIM
WINDOWS = (128, 512, 2048)
DILATIONS = (1, 4, 16)
N_GROUPS = 3
ATTN_W = N_GROUPS * GROUP_W
CONV_CH = 256
N_PROJ = 3 * ATTN_W + 3 * CONV_CH
N_STEPS = 128
N_KEYS = N_STEPS + 1
BLK = 128
N_BUCKETS = 32
MAX_DISTANCE = WINDOWS[-1]
N_EXPERTS = 64
TOP_K = 8
N_ROUTE_GROUPS = 8
GROUP_SIZE = N_EXPERTS // N_ROUTE_GROUPS
TOPK_GROUPS = 4
D_EXPERT = 256
ROUTED_SCALE = 2.5
RMS_EPS = 1e-6
NEG = -1e30
LANES = 128
SUB = 256
CHUNK = 16
XG_W = D_MODEL + LANES
VMEM_LIMIT = 56 * 1024 * 1024

F32 = jnp.float32
BF16 = jnp.bfloat16


def _dot(a, b):
    return jnp.dot(a, b, preferred_element_type=F32)


def _dot_nt(a, b):
    return lax.dot_general(a, b, (((1,), (1,)), ((), ())), preferred_element_type=F32)


def _silu(a):
    return a / (1.0 + jnp.exp(-a))


def _params(sem):
    return pltpu.CompilerParams(dimension_semantics=sem, vmem_limit_bytes=VMEM_LIMIT)


def _ada_kernel(c_ref, w_ref, b_ref, o_ref):
    s = _silu(c_ref[...])
    o_ref[...] = _dot(s.astype(BF16), w_ref[...].astype(BF16)) + b_ref[...]


def _ada(c, w_ada, b_ada):
    n = c.shape[0]
    nc = w_ada.shape[1] // D_MODEL
    return pl.pallas_call(
        _ada_kernel,
        grid=(nc,),
        in_specs=[pl.BlockSpec((n, D_MODEL), lambda j: (0, 0)),
                  pl.BlockSpec((D_MODEL, D_MODEL), lambda j: (0, j)),
                  pl.BlockSpec((1, D_MODEL), lambda j: (0, j))],
        out_specs=pl.BlockSpec((n, D_MODEL), lambda j: (0, j)),
        out_shape=jax.ShapeDtypeStruct((n, w_ada.shape[1]), F32),
        compiler_params=_params(("arbitrary",)),
        name="ada",
    )(c, w_ada, b_ada)


def _proj_kernel(*refs, tile, dils, sample, nt):
    if sample:
        (x_ref, sc_ref, sh_ref, ng_ref, w_ref, qg_ref, kg_ref, cw_ref, bd_ref, ha_ref, hb_ref,
         qf_ref, kt_ref, vt_ref, u_ref, co_ref, u_scr) = refs
    else:
        (x_ref, sc_ref, sh_ref, ng_ref, w_ref, qg_ref, kg_ref, cw_ref, bd_ref,
         q0, k0, v0, q1, k1, v1, q2, k2, v2, kt0, vt0, kt1, vt1, kt2, vt2, u_ref, co_ref, slab, u_scr) = refs
        qkv_out = ((q0, k0, v0), (q1, k1, v1), (q2, k2, v2))
        tails = ((kt0, vt0), (kt1, vt1), (kt2, vt2))
        j = pl.program_id(1)

    x = x_ref[0]
    ms = jnp.mean(x * x, axis=-1, keepdims=True)
    h = x * lax.rsqrt(ms + RMS_EPS) * ng_ref[...]
    h = h * (1.0 + sc_ref[0]) + sh_ref[0]
    proj = _dot(h.astype(BF16), w_ref[...])

    bd = bd_ref[...]

    def headnorm(z, g):
        zz = z * z
        hi = zz.astype(BF16)
        lo = (zz - hi.astype(F32)).astype(BF16)
        msq = _dot(hi, bd) + _dot(lo, bd)
        return z * lax.rsqrt(msq + RMS_EPS) * g

    slab_i = 0
    for g in range(N_GROUPS):
        c0 = g * GROUP_W
        qn = headnorm(proj[:, c0:c0 + GROUP_W], qg_ref[...]) * (HEAD_DIM ** -0.5)
        kn = headnorm(proj[:, ATTN_W + c0:ATTN_W + c0 + GROUP_W], kg_ref[...])
        vv = proj[:, 2 * ATTN_W + c0:2 * ATTN_W + c0 + GROUP_W]
        if sample:
            qf_ref[0, :, c0:c0 + GROUP_W] = qn
            kt_ref[c0:c0 + GROUP_W, :] = kn.T
            vt_ref[c0:c0 + GROUP_W, :] = vv.T
            continue
        keep = min(WINDOWS[g], tile)

        @pl.when(j >= nt - max(WINDOWS[g] // tile, 1))
        def _(kn=kn, vv=vv, g=g, keep=keep):
            tails[g][0][0] = kn[tile - keep:, :].T
            tails[g][1][0] = vv[tile - keep:, :].T

        d = dils[g]
        for val, out in zip((qn, kn, vv), qkv_out[g]):
            if d == 1:
                out[0, 0] = val.astype(BF16)
                continue
            n = tile // d
            for half in range(GROUP_W // LANES):
                slab[slab_i] = val[:, half * LANES:(half + 1) * LANES]
                for r in range(d):
                    out[0, r, :, half * LANES:(half + 1) * LANES] = (
                        slab[slab_i, pl.ds(r, n, stride=d), :].astype(BF16))
                slab_i += 1

    base = 3 * ATTN_W
    u = proj[:, base + 2 * CONV_CH:base + 3 * CONV_CH] * proj[:, base:base + CONV_CH]
    gate_b = proj[:, base + CONV_CH:base + 2 * CONV_CH]
    if sample:
        u_scr[0:8, :] = jnp.zeros((8, CONV_CH), F32)
    else:
        @pl.when(j == 0)
        def _():
            u_scr[0:8, :] = jnp.zeros((8, CONV_CH), F32)

        @pl.when(j > 0)
        def _():
            u_scr[0:8, :] = u_scr[tile:tile + 8, :]

    u_scr[8:tile + 8, :] = u
    um1 = u_scr[7:tile + 7, :]
    um2 = u_scr[6:tile + 6, :]
    if sample:
        t = lax.broadcasted_iota(jnp.int32, (tile, CONV_CH), 0) % 4
        um1 = jnp.where(t >= 1, um1, 0.0) + hb_ref[...]
        um2 = jnp.where(t >= 2, um2, 0.0) + ha_ref[...]
    cw = cw_ref[...]
    conv = cw[0:1] * um2 + cw[1:2] * um1 + cw[2:3] * u
    u_ref[0] = u if sample else u[tile - 8:, :]
    co_ref[0] = (gate_b * conv).astype(BF16)


def _proj(x, sc, sh, ng, w_bf, qg, kg, cw, bd, hist, *, tile, dils, sample):
    B, S, _ = x.shape
    nt = S // tile
    tm = sc.shape[1]
    mod_spec = pl.BlockSpec((1, tm, D_MODEL), (lambda b, j: (b, j, 0)) if tm > 1 else (lambda b, j: (b, 0, 0)))
    const2 = lambda shape: pl.BlockSpec(shape, lambda b, j: (0, 0))
    in_specs = [pl.BlockSpec((1, tile, D_MODEL), lambda b, j: (b, j, 0)), mod_spec, mod_spec,
                const2((1, D_MODEL)), const2((D_MODEL, N_PROJ)), const2((1, GROUP_W)),
                const2((1, GROUP_W)), const2((3, CONV_CH)), const2((GROUP_W, GROUP_W))]
    args = [x, sc, sh, ng, w_bf, qg, kg, cw, bd]
    nat = lambda w: pl.BlockSpec((1, tile, w), lambda b, j: (b, j, 0))
    out_specs, out_shape = [], []
    scratch = []
    if sample:
        assert B == 1 and nt == 1
        in_specs += [pl.BlockSpec((tile, CONV_CH), lambda b, j: (j, 0))] * 2
        args += list(hist)
        out_specs += [nat(ATTN_W)] + [pl.BlockSpec((ATTN_W, tile), lambda b, j: (0, j))] * 2 + [nat(CONV_CH)]
        out_shape += [jax.ShapeDtypeStruct((B, S, ATTN_W), F32)]
        out_shape += [jax.ShapeDtypeStruct((ATTN_W, S), F32)] * 2
        out_shape += [jax.ShapeDtypeStruct((B, S, CONV_CH), F32)]
    else:
        for d in dils:
            for _ in range(3):
                out_specs.append(pl.BlockSpec((1, d, tile // d, GROUP_W), lambda b, j: (b, 0, j, 0)))
                out_shape.append(jax.ShapeDtypeStruct((B, d, S // d, GROUP_W), BF16))
        for w in WINDOWS:
            keep, first = min(w, tile), nt - max(w // tile, 1)
            for _ in range(2):
                out_specs.append(pl.BlockSpec((1, GROUP_W, keep),
                                              lambda b, j, first=first: (b, 0, jnp.maximum(j - first, 0))))
                out_shape.append(jax.ShapeDtypeStruct((B, GROUP_W, w), F32))
        out_specs.append(pl.BlockSpec((1, 8, CONV_CH), lambda b, j: (b, 0, 0)))
        out_shape.append(jax.ShapeDtypeStruct((B, 8, CONV_CH), F32))
        n_slabs = sum(3 * (GROUP_W // LANES) for d in dils if d > 1)
        scratch.append(pltpu.VMEM((n_slabs, tile, LANES), F32))
    out_specs.append(nat(CONV_CH))
    out_shape.append(jax.ShapeDtypeStruct((B, S, CONV_CH), BF16))
    scratch.append(pltpu.VMEM((tile + 8, CONV_CH), F32))
    return pl.pallas_call(
        functools.partial(_proj_kernel, tile=tile, dils=dils, sample=sample, nt=nt),
        grid=(B, nt),
        in_specs=in_specs,
        out_specs=out_specs,
        out_shape=out_shape,
        scratch_shapes=scratch,
        compiler_params=_params(("arbitrary", "arbitrary")),
        name="proj_sample" if sample else "proj_prompt",
    )(*args)


def _attn_kernel(q_ref, kp_ref, kc_ref, vp_ref, vc_ref, b_ref, o_ref, l_ref, s_scr, m_scr, o_scr, z_scr,
                 *, nq):
    first = pl.program_id(1) == 0
    low = lax.broadcasted_iota(jnp.int32, (BLK, LANES), 1) < HEAD_DIM
    ones = jnp.ones((BLK, LANES), BF16)
    chains = [(j, pair, sub) for j in range(nq) for pair in range(GROUP_W // LANES) for sub in range(2)]

    def operands(j, pair):
        rows = slice(j * BLK, (j + 1) * BLK)
        sl = slice(pair * LANES, (pair + 1) * LANES)
        if j == 0:
            return rows, sl, kp_ref[0, :, sl], vp_ref[0, :, sl]
        prows = slice((j - 1) * BLK, j * BLK)
        return rows, sl, kc_ref[0, prows, sl], vc_ref[0, prows, sl]

    for c, (j, pair, sub) in enumerate(chains):
        rows, sl, kp, _ = operands(j, pair)
        q = q_ref[0, rows, sl]
        qm = jnp.where(low if sub == 0 else ~low, q, jnp.zeros_like(q))
        hh = 2 * pair + sub
        sp = _dot_nt(qm, kp) + b_ref[hh, :, 0:BLK]
        if j == 0:
            sp = jnp.where(first, NEG, sp)
        sc = _dot_nt(qm, kc_ref[0, rows, sl]) + b_ref[hh, :, BLK:2 * BLK]
        s_scr[c, :, 0:BLK] = sp
        s_scr[c, :, BLK:2 * BLK] = sc
        m_scr[c] = jnp.broadcast_to(jnp.max(jnp.maximum(sp, sc), axis=-1, keepdims=True), (BLK, LANES))

    for c, (j, pair, sub) in enumerate(chains):
        rows, sl, _, vp = operands(j, pair)
        m = m_scr[c]
        pp = jnp.exp(s_scr[c, :, 0:BLK] - m).astype(BF16)
        pc = jnp.exp(s_scr[c, :, BLK:2 * BLK] - m).astype(BF16)
        o_scr[c] = _dot(pp, vp) + _dot(pc, vc_ref[0, rows, sl])
        z_scr[c] = _dot(pp, ones) + _dot(pc, ones)

    for c in range(0, len(chains), 2):
        j, pair, _ = chains[c]
        rows = slice(j * BLK, (j + 1) * BLK)
        sl = slice(pair * LANES, (pair + 1) * LANES)
        o_sub = [o_scr[c + sub] / z_scr[c + sub] for sub in range(2)]
        l_sub = [m_scr[c + sub] + jnp.log(z_scr[c + sub]) for sub in range(2)]
        o_ref[0, rows, sl] = jnp.where(low, o_sub[0], o_sub[1])
        l_ref[0, rows, sl] = jnp.where(low, l_sub[0], l_sub[1])


def _attn_prompt(q, k, v, bias):
    Z, L, _ = q.shape
    qt = min(L, 4 * BLK)
    nq = qt // BLK
    cur = pl.BlockSpec((1, qt, GROUP_W), lambda z, i: (z, i, 0))
    prev = pl.BlockSpec((1, BLK, GROUP_W), lambda z, i: (z, jnp.maximum(i * nq - 1, 0), 0))
    return pl.pallas_call(
        functools.partial(_attn_kernel, nq=nq),
        grid=(Z, L // qt),
        in_specs=[cur, prev, cur, prev, cur,
                  pl.BlockSpec((HEADS_PER_GROUP, BLK, 2 * BLK), lambda z, i: (0, 0, 0))],
        out_specs=[cur, cur],
        out_shape=[jax.ShapeDtypeStruct((Z, L, GROUP_W), F32)] * 2,
        scratch_shapes=[pltpu.VMEM((nq * HEADS_PER_GROUP, BLK, 2 * BLK), F32)]
        + [pltpu.VMEM((nq * HEADS_PER_GROUP, BLK, LANES), F32)] * 3,
        compiler_params=_params(("arbitrary", "arbitrary")),
        name="attn_prompt",
    )(q, k, k, v, v, bias)


def _attn_sample_kernel(q_ref, kc_ref, vc_ref, nk_ref, nv_ref, b1_ref, b2_ref,
                        ko_ref, vo_ref, o_ref, l_ref, *, bb, wb):
    lane = lax.broadcasted_iota(jnp.int32, (HEAD_DIM, LANES), 1)
    new_cols = lane >= LANES - 4
    per_tile = LANES // 4
    for b in range(bb):
        gb = pl.program_id(0) * bb + b
        tile_at = pl.ds(pl.multiple_of((gb // per_tile) * LANES, LANES), LANES)
        shift = (LANES - 4) - 4 * (gb % per_tile)
        for hh in range(HEADS_PER_GROUP):
            q = q_ref[b, hh]
            kt = kc_ref[b, hh]
            vt = vc_ref[b, hh]
            nkt = pltpu.roll(nk_ref[hh, :, tile_at], shift, axis=1)
            nvt = pltpu.roll(nv_ref[hh, :, tile_at], shift, axis=1)
            s1 = _dot(q, kt.astype(BF16)) + b1_ref[hh]
            s2 = _dot(q, nkt.astype(BF16)) + b2_ref[hh]
            m = jnp.maximum(jnp.max(s1, axis=-1, keepdims=True), jnp.max(s2, axis=-1, keepdims=True))
            p1 = jnp.exp(s1 - m)
            p2 = jnp.exp(s2 - m)
            z = jnp.sum(p1, axis=-1, keepdims=True) + jnp.sum(p2, axis=-1, keepdims=True)
            o = _dot_nt(p1.astype(BF16), vt.astype(BF16)) + _dot_nt(p2.astype(BF16), nvt.astype(BF16))
            o_ref[b, hh] = o / z
            l_ref[b, hh] = jnp.broadcast_to(m + jnp.log(z), (8, LANES))
            for src, new, dst in ((kt, nkt, ko_ref), (vt, nvt, vo_ref)):
                rolled = pltpu.roll(src, wb - 4, axis=1)
                if wb > LANES:
                    dst[b, hh, :, 0:wb - LANES] = rolled[:, 0:wb - LANES]
                dst[b, hh, :, wb - LANES:wb] = jnp.where(new_cols, new, rolled[:, wb - LANES:wb])


def _attn_sample(q, kc, vc, nk, nv, b1, b2, *, bb):
    nb, _, _, wb = kc.shape
    blk = lambda *tail: pl.BlockSpec((bb, HEADS_PER_GROUP) + tail, lambda i: (i, 0, 0, 0))
    const = lambda shape: pl.BlockSpec(shape, lambda i: (0, 0, 0))
    return pl.pallas_call(
        functools.partial(_attn_sample_kernel, bb=bb, wb=wb),
        grid=(nb // bb,),
        in_specs=[blk(8, HEAD_DIM), blk(HEAD_DIM, wb), blk(HEAD_DIM, wb), const(nk.shape), const(nv.shape),
                  const((HEADS_PER_GROUP, 8, wb)), const((HEADS_PER_GROUP, 8, LANES))],
        out_specs=[blk(HEAD_DIM, wb), blk(HEAD_DIM, wb), blk(8, HEAD_DIM), blk(8, LANES)],
        out_shape=[jax.ShapeDtypeStruct(kc.shape, F32), jax.ShapeDtypeStruct(kc.shape, F32),
                   jax.ShapeDtypeStruct((nb, HEADS_PER_GROUP, 8, HEAD_DIM), F32),
                   jax.ShapeDtypeStruct((nb, HEADS_PER_GROUP, 8, LANES), F32)],
        compiler_params=_params(("arbitrary",)),
        name="attn_sample",
    )(q, kc, vc, nk, nv, b1, b2)


def _first_max(v, ids, sentinel):
    m = jnp.max(v, axis=0, keepdims=True)
    idx = jnp.min(jnp.where(v == m, ids, sentinel), axis=0, keepdims=True)
    return m, ids == idx


def _oproj_kernel(x_ref, o0, o1, o2, l0, l1, l2, co_ref, g1_ref, sc_ref, sh_ref, ng_ref, wo_ref,
                  wr_ref, br_ref, x1_ref, xg_ref, gt_ref, posr_ref, post_ref, cnt_ref, slab, *, tile, dils):
    o_refs, l_refs = (o0, o1, o2), (l0, l1, l2)
    outs, lses = [], []
    slab_i = 0
    for g, d in enumerate(dils):
        if d == 1:
            outs.append(o_refs[g][0, 0])
            lses.append(l_refs[g][0, 0])
            continue
        n = tile // d
        for ref, dest in ((o_refs[g], outs), (l_refs[g], lses)):
            halves = []
            for half in range(GROUP_W // LANES):
                for r in range(d):
                    slab[slab_i, pl.ds(r, n, stride=d), :] = ref[0, r, :, half * LANES:(half + 1) * LANES]
                halves.append(slab[slab_i])
                slab_i += 1
            dest.append(jnp.concatenate(halves, axis=1))

    m = jnp.maximum(jnp.maximum(lses[0], lses[1]), lses[2])
    es = [jnp.exp(l - m) for l in lses]
    den = es[0] + es[1] + es[2]
    cat = [(outs[g] * (es[g] / den)).astype(BF16) for g in range(N_GROUPS)]
    cat.append(co_ref[0].astype(BF16))
    cat = jnp.concatenate(cat, axis=1)
    x1 = x_ref[0] + g1_ref[0] * _dot(cat, wo_ref[...])
    x1_ref[0] = x1

    ms = jnp.mean(x1 * x1, axis=-1, keepdims=True)
    h2 = x1 * lax.rsqrt(ms + RMS_EPS) * ng_ref[...]
    h2 = h2 * (1.0 + sc_ref[0]) + sh_ref[0]
    hh = h2.astype(BF16)
    xg_ref[0, :, 0:D_MODEL] = hh

    hl = (h2 - hh.astype(F32)).astype(BF16)
    wr = wr_ref[...]
    wh = wr.astype(BF16)
    wl = (wr - wh.astype(F32)).astype(BF16)
    logits = _dot_nt(wh, hh) + _dot_nt(wh, hl) + _dot_nt(wl, hh)
    scores = 1.0 / (1.0 + jnp.exp(-logits))
    sel = scores + br_ref[...]

    ids = lax.broadcasted_iota(jnp.int32, (GROUP_SIZE, tile), 0)
    ninf = -jnp.inf
    sel_g = [sel[g * GROUP_SIZE:(g + 1) * GROUP_SIZE] for g in range(N_ROUTE_GROUPS)]
    gscore = jnp.zeros((N_ROUTE_GROUPS, tile), F32)
    for g in range(N_ROUTE_GROUPS):
        m1, oh = _first_max(sel_g[g], ids, GROUP_SIZE)
        m2 = jnp.max(jnp.where(oh, ninf, sel_g[g]), axis=0, keepdims=True)
        gscore = jnp.where(ids == g, m1 + m2, gscore)
    gsel = jnp.zeros((N_ROUTE_GROUPS, tile), F32)
    for _ in range(TOPK_GROUPS):
        _, oh = _first_max(gscore, ids, N_ROUTE_GROUPS)
        gsel = jnp.where(oh, 1.0, gsel)
        gscore = jnp.where(oh, ninf, gscore)
    cand = [jnp.where(gsel[g:g + 1] > 0.0, sel_g[g], ninf) for g in range(N_ROUTE_GROUPS)]
    chosen = [jnp.zeros((GROUP_SIZE, tile), F32) for _ in range(N_ROUTE_GROUPS)]
    picks = []
    for _ in range(TOP_K):
        mx = cand[0]
        for g in range(1, N_ROUTE_GROUPS):
            mx = jnp.maximum(mx, cand[g])
        mx = jnp.max(mx, axis=0, keepdims=True)
        idx = jnp.where(cand[0] == mx, ids, N_EXPERTS)
        for g in range(1, N_ROUTE_GROUPS):
            idx = jnp.minimum(idx, jnp.where(cand[g] == mx, ids + g * GROUP_SIZE, N_EXPERTS))
        idx = jnp.min(idx, axis=0, keepdims=True)
        ohs = [(ids + g * GROUP_SIZE) == idx for g in range(N_ROUTE_GROUPS)]
        picks.append(ohs)
        for g in range(N_ROUTE_GROUPS):
            chosen[g] = jnp.where(ohs[g], 1.0, chosen[g])
            cand[g] = jnp.where(ohs[g], ninf, cand[g])
    wts = [jnp.where(chosen[g] > 0.0, scores[g * GROUP_SIZE:(g + 1) * GROUP_SIZE], 0.0)
           for g in range(N_ROUTE_GROUPS)]
    wsum = jnp.sum(wts[0], axis=0, keepdims=True)
    for g in range(1, N_ROUTE_GROUPS):
        wsum = wsum + jnp.sum(wts[g], axis=0, keepdims=True)
    gates_t = jnp.concatenate([w / wsum * ROUTED_SCALE for w in wts], axis=0)
    gt_ref[0] = jnp.concatenate([gates_t, jnp.zeros((LANES - N_EXPERTS, tile), F32)], axis=0).T
    g_hi = gates_t.astype(BF16).astype(F32)
    xg_ref[0, :, D_MODEL:D_MODEL + LANES] = jnp.concatenate([g_hi, gates_t - g_hi], axis=0).T.astype(BF16)

    er = lax.broadcasted_iota(jnp.int32, (N_EXPERTS, N_EXPERTS), 0)
    ec = lax.broadcasted_iota(jnp.int32, (N_EXPERTS, N_EXPERTS), 1)
    lower = jnp.where(ec < er, 1.0, 0.0).astype(BF16)
    tr = lax.broadcasted_iota(jnp.int32, (SUB, SUB), 0)
    tc = lax.broadcasted_iota(jnp.int32, (SUB, SUB), 1)
    upper = jnp.where(tr < tc, 1.0, 0.0).astype(BF16)
    for u in range(tile // SUB):
        cols = slice(u * SUB, (u + 1) * SUB)
        ch = jnp.concatenate([c[:, cols] for c in chosen], axis=0)
        cnt = jnp.sum(ch, axis=1, keepdims=True)
        cpad = jnp.broadcast_to(jnp.ceil(cnt / CHUNK) * CHUNK, (N_EXPERTS, LANES))
        cnt_ref[u] = cpad
        seg0 = _dot(lower, cpad.astype(BF16))[:, 0:1]
        slot = seg0 + _dot(ch.astype(BF16), upper)
        rows = []
        for k in range(TOP_K):
            acc = jnp.where(picks[k][0][:, cols], slot[0:GROUP_SIZE], 0.0)
            for g in range(1, N_ROUTE_GROUPS):
                acc = acc + jnp.where(picks[k][g][:, cols], slot[g * GROUP_SIZE:(g + 1) * GROUP_SIZE], 0.0)
            rows.append(jnp.sum(acc, axis=0, keepdims=True))
        posr = jnp.concatenate(rows, axis=0)
        posr_ref[u] = posr
        post_ref[0, cols, :] = jnp.concatenate([posr, jnp.zeros((LANES - TOP_K, SUB), F32)], axis=0).T


def _oproj(x, o_list, l_list, co, g1, sc, sh, ng, wo_bf, wr_t, br, *, tile, dils):
    B, S, _ = x.shape
    tm = g1.shape[1]
    mod_spec = pl.BlockSpec((1, tm, D_MODEL), (lambda b, j: (b, j, 0)) if tm > 1 else (lambda b, j: (b, 0, 0)))
    const2 = lambda shape: pl.BlockSpec(shape, lambda b, j: (0, 0))
    nat = lambda w: pl.BlockSpec((1, tile, w), lambda b, j: (b, j, 0))
    dspec = [pl.BlockSpec((1, d, tile // d, GROUP_W), lambda b, j: (b, 0, j, 0)) for d in dils]
    n_slabs = sum(2 * (GROUP_W // LANES) for d in dils if d > 1)
    nsub_t = tile // SUB
    return pl.pallas_call(
        functools.partial(_oproj_kernel, tile=tile, dils=dils),
        grid=(B, S // tile),
        in_specs=[nat(D_MODEL)] + dspec + dspec + [nat(CONV_CH), mod_spec, mod_spec, mod_spec,
                  const2((1, D_MODEL)), const2((D_MODEL, D_MODEL)), const2((N_EXPERTS, D_MODEL)),
                  const2((N_EXPERTS, 1))],
        out_specs=[nat(D_MODEL), nat(XG_W), nat(LANES),
                   pl.BlockSpec((nsub_t, TOP_K, SUB), lambda b, j: (b * (S // tile) + j, 0, 0)),
                   nat(LANES),
                   pl.BlockSpec((nsub_t, N_EXPERTS, LANES), lambda b, j: (b * (S // tile) + j, 0, 0))],
        out_shape=[jax.ShapeDtypeStruct((B, S, D_MODEL), F32), jax.ShapeDtypeStruct((B, S, XG_W), BF16),
                   jax.ShapeDtypeStruct((B, S, LANES), F32),
                   jax.ShapeDtypeStruct((B * S // SUB, TOP_K, SUB), F32),
                   jax.ShapeDtypeStruct((B, S, LANES), F32),
                   jax.ShapeDtypeStruct((B * S // SUB, N_EXPERTS, LANES), F32)],
        scratch_shapes=[pltpu.VMEM((max(n_slabs, 1), tile, LANES), F32)],
        compiler_params=_params(("arbitrary", "arbitrary")),
        name="oproj",
    )(x, *o_list, *l_list, co, g1, sc, sh, ng, wo_bf, wr_t, br)


def _moe_kernel(h_ref, gt_ref, x1_ref, g2_ref, wg_ref, wu_ref, wd_ref, wgs_ref, wus_ref, wds_ref,
                y_ref, acc, *, tile):
    e = pl.program_id(2)
    h = h_ref[0]

    @pl.when(e == 0)
    def _():
        a = _dot(h, wgs_ref[...].astype(BF16))
        b = _dot(h, wus_ref[...].astype(BF16))
        acc[...] = _dot((_silu(a) * b).astype(BF16), wds_ref[...].astype(BF16))

    a = _dot(h, wg_ref[0].astype(BF16))
    b = _dot(h, wu_ref[0].astype(BF16))
    lane = lax.broadcasted_iota(jnp.int32, (tile, LANES), 1)
    gcol = jnp.sum(jnp.where(lane == e, gt_ref[0], 0.0), axis=1, keepdims=True)
    hc = _silu(a) * b * gcol
    acc[...] += _dot(hc.astype(BF16), wd_ref[0].astype(BF16))

    @pl.when(e == N_EXPERTS - 1)
    def _():
        y_ref[0] = x1_ref[0] + g2_ref[0] * acc[...]


def _moe(h2, gates, x1, g2, wg, wu, wd, wgs, wus, wds, *, tile):
    B, S, _ = h2.shape
    tm = g2.shape[1]
    mod_spec = pl.BlockSpec((1, tm, D_MODEL),
                            (lambda b, j, e: (b, j, 0)) if tm > 1 else (lambda b, j, e: (b, 0, 0)))
    nat = lambda w: pl.BlockSpec((1, tile, w), lambda b, j, e: (b, j, 0))
    const2 = lambda shape: pl.BlockSpec(shape, lambda b, j, e: (0, 0))
    return pl.pallas_call(
        functools.partial(_moe_kernel, tile=tile),
        grid=(B, S // tile, N_EXPERTS),
        in_specs=[nat(D_MODEL), nat(LANES), nat(D_MODEL), mod_spec,
                  pl.BlockSpec((1, D_MODEL, D_EXPERT), lambda b, j, e: (e, 0, 0)),
                  pl.BlockSpec((1, D_MODEL, D_EXPERT), lambda b, j, e: (e, 0, 0)),
                  pl.BlockSpec((1, D_EXPERT, D_MODEL), lambda b, j, e: (e, 0, 0)),
                  const2((D_MODEL, D_EXPERT)), const2((D_MODEL, D_EXPERT)), const2((D_EXPERT, D_MODEL))],
        out_specs=nat(D_MODEL),
        out_shape=jax.ShapeDtypeStruct((B, S, D_MODEL), F32),
        scratch_shapes=[pltpu.VMEM((tile, D_MODEL), F32)],
        compiler_params=_params(("arbitrary", "arbitrary", "arbitrary")),
        name="moe",
    )(h2, gates, x1, g2, wg, wu, wd, wgs, wus, wds)


FFN_TM = 1024
SUB_ROWS = SUB * TOP_K + N_EXPERTS * CHUNK
N_CHUNKS = SUB_ROWS // CHUNK
ZERO_CHUNK = N_CHUNKS
TABLE_W = 256
MXU_ROWS = 256
BLOCKS_PER_TRIP = 2


def _slot_onehot(first, pos_list, axis_iota):
    shape = axis_iota.shape
    hit = axis_iota == jnp.broadcast_to(pos_list[0] - first, shape).astype(BF16)
    for p in pos_list[1:]:
        hit = hit | (axis_iota == jnp.broadcast_to(p - first, shape).astype(BF16))
    return jnp.where(hit, jnp.ones(shape, BF16), jnp.zeros(shape, BF16))


def _chunk_copy(src, src_chunk, dst, dst_chunk, sem):
    return pltpu.make_async_copy(src.at[src_chunk], dst.at[dst_chunk], sem)


def _for_each(n, body):
    def two(t, carry):
        body(2 * t)
        body(2 * t + 1)
        return carry

    lax.fori_loop(0, n // 2, two, 0)

    @pl.when(n % 2 == 1)
    def _():
        body(n - 1)


def _dispatch_kernel(nblk_ref, nreal_ref, ncopy_ref, ntile_ref, src_ref, dst_ref, posr_ref, xg_ref, xs_hbm,
                     xs_scr, zero_scr, sem, tail_sem, *, nsub, n_tiles_max):
    s = pl.program_id(0)

    tile_chunks = FFN_TM // CHUNK
    blk_chunks = MXU_ROWS // CHUNK
    zero_scr[...] = jnp.zeros((tile_chunks, CHUNK, XG_W), BF16)
    tail_blocks = [ntile_ref[0] + s + r * nsub for r in range(-(-(n_tiles_max) // nsub))]

    def tail_copy(t):
        return pltpu.make_async_copy(
            zero_scr, xs_hbm.at[pl.ds(pl.multiple_of(t * tile_chunks, tile_chunks), tile_chunks)], tail_sem)

    for t in tail_blocks:
        @pl.when(t < n_tiles_max)
        def _():
            tail_copy(t).start()

    xg = xg_ref[...]
    pos = [posr_ref[0, k:k + 1, :] for k in range(TOP_K)]
    rows = lax.broadcasted_iota(jnp.int32, (MXU_ROWS, SUB), 0).astype(BF16)
    xs_scr[ZERO_CHUNK] = jnp.zeros((CHUNK, XG_W), BF16)

    def sort_blocks(i, carry):
        for j in range(BLOCKS_PER_TRIP):
            blk = i * BLOCKS_PER_TRIP + j
            onehot = _slot_onehot((blk * MXU_ROWS).astype(F32), pos, rows)
            sorted_rows = _dot(onehot, xg).astype(BF16)
            xs_scr[pl.ds(pl.multiple_of(blk * blk_chunks, blk_chunks), blk_chunks)] = (
                sorted_rows.reshape(blk_chunks, CHUNK, XG_W))
        k0 = i * trip_chunks
        _for_each(jnp.clip(n_real - k0, 0, trip_chunks), lambda t: send(k0 + t))
        return carry

    n_real = nreal_ref[s]
    n = ncopy_ref[s]
    trip_chunks = BLOCKS_PER_TRIP * blk_chunks

    def send(i):
        _chunk_copy(xs_scr, src_ref[0, 0, i], xs_hbm, dst_ref[0, 0, i], sem).start()

    lax.fori_loop(0, nblk_ref[s], sort_blocks, 0)
    _for_each(n - n_real, lambda t: send(n_real + t))
    _for_each(n, lambda i: _chunk_copy(xs_scr, 0, xs_hbm, 0, sem).wait())
    for t in tail_blocks:
        @pl.when(t < n_tiles_max)
        def _():
            tail_copy(t).wait()


def _dispatch(nblk, n_real, ncopy, n_tiles, src, dst, posr, xg, n_rows):
    nsub = posr.shape[0]
    tab = pl.BlockSpec((1, 1, TABLE_W), lambda s, *_: (s, 0, 0), memory_space=pltpu.SMEM)
    return pl.pallas_call(
        functools.partial(_dispatch_kernel, nsub=nsub, n_tiles_max=n_rows // FFN_TM),
        grid_spec=pltpu.PrefetchScalarGridSpec(
            num_scalar_prefetch=4, grid=(nsub,),
            in_specs=[tab, tab, pl.BlockSpec((1, TOP_K, SUB), lambda s, *_: (s, 0, 0)),
                      pl.BlockSpec((SUB, XG_W), lambda s, *_: (s, 0))],
            out_specs=pl.BlockSpec(memory_space=pl.ANY),
            scratch_shapes=[pltpu.VMEM((N_CHUNKS + 1, CHUNK, XG_W), BF16),
                            pltpu.VMEM((FFN_TM // CHUNK, CHUNK, XG_W), BF16),
                            pltpu.SemaphoreType.DMA(()), pltpu.SemaphoreType.DMA(())]),
        out_shape=jax.ShapeDtypeStruct((n_rows // CHUNK, CHUNK, XG_W), BF16),
        compiler_params=_params(("arbitrary",)),
        name="moe_dispatch",
    )(nblk, n_real, ncopy, n_tiles, src, dst, posr, xg).reshape(n_rows, XG_W)


def _ffn_kernel(te_ref, nt_ref, xs_ref, wg_ref, wu_ref, wd_ref, ys_ref, wg_s, wu_s, wd_s):
    i = pl.program_id(0)

    @pl.when(i >= nt_ref[0])
    def _():
        ys_ref[...] = jnp.zeros((FFN_TM, D_MODEL), BF16)

    @pl.when(i < nt_ref[0])
    def _():
        e = te_ref[i]

        @pl.when((i == 0) | (e != te_ref[jnp.maximum(i - 1, 0)]))
        def _():
            wg_s[...] = wg_ref[0].astype(BF16)
            wu_s[...] = wu_ref[0].astype(BF16)
            wd_s[...] = wd_ref[0].astype(BF16)

        x = xs_ref[:, 0:D_MODEL]
        g = xs_ref[:, D_MODEL:XG_W].astype(F32)
        lane = lax.broadcasted_iota(jnp.int32, (FFN_TM, LANES), 1)
        gcol = jnp.sum(jnp.where((lane == e) | (lane == e + N_EXPERTS), g, 0.0), axis=1, keepdims=True)
        a = _dot(x, wg_s[...])
        b = _dot(x, wu_s[...])
        hc = _silu(a) * b * gcol
        ys_ref[...] = _dot(hc.astype(BF16), wd_s[...]).astype(BF16)


def _ffn(tile_expert, n_tiles, xs, wg, wu, wd):
    n_rows = xs.shape[0]
    row = lambda i, te, nt: (jnp.minimum(i, nt[0] - 1), 0)
    wspec = lambda shape: pl.BlockSpec((1,) + shape, lambda i, te, nt: (te[i], 0, 0))
    return pl.pallas_call(
        _ffn_kernel,
        grid_spec=pltpu.PrefetchScalarGridSpec(
            num_scalar_prefetch=2, grid=(n_rows // FFN_TM,),
            in_specs=[pl.BlockSpec((FFN_TM, XG_W), row), wspec((D_MODEL, D_EXPERT)),
                      wspec((D_MODEL, D_EXPERT)), wspec((D_EXPERT, D_MODEL))],
            out_specs=pl.BlockSpec((FFN_TM, D_MODEL), lambda i, te, nt: (i, 0)),
            scratch_shapes=[pltpu.VMEM((D_MODEL, D_EXPERT), BF16), pltpu.VMEM((D_MODEL, D_EXPERT), BF16),
                            pltpu.VMEM((D_EXPERT, D_MODEL), BF16)]),
        out_shape=jax.ShapeDtypeStruct((n_rows, D_MODEL), BF16),
        compiler_params=_params(("arbitrary",)),
        name="moe_ffn",
    )(tile_expert, n_tiles, xs, wg, wu, wd)


def _combine_kernel(nblk_ref, ncopy_ref, src_ref, post_ref, xg_ref, x1_ref, g2_ref, wgs_ref, wus_ref,
                    wds_ref, ys_hbm, y_ref, ys_scr, wgs_s, wus_s, wds_s, sem):
    s = pl.program_id(0)
    n = ncopy_ref[s]
    blk_chunks = MXU_ROWS // CHUNK
    trip_chunks = BLOCKS_PER_TRIP * blk_chunks

    def fetch_trip(t, carry):
        k0 = t * trip_chunks
        _for_each(trip_chunks,
                  lambda u: _chunk_copy(ys_hbm, src_ref[0, 0, k0 + u], ys_scr, k0 + u, sem.at[t]).start())
        return carry

    lax.fori_loop(0, nblk_ref[s], fetch_trip, 0)

    @pl.when(s == 0)
    def _():
        wgs_s[...] = wgs_ref[...].astype(BF16)
        wus_s[...] = wus_ref[...].astype(BF16)
        wds_s[...] = wds_ref[...].astype(BF16)

    h = xg_ref[:, 0:D_MODEL]
    a = _dot(h, wgs_s[...])
    b = _dot(h, wus_s[...])
    shared = _dot((_silu(a) * b).astype(BF16), wds_s[...])

    post = post_ref[...]
    pos = [jnp.broadcast_to(post[:, k:k + 1], (SUB, MXU_ROWS)) for k in range(TOP_K)]
    lanes = lax.broadcasted_iota(jnp.int32, (SUB, MXU_ROWS), 1).astype(BF16)

    def gather_blocks(i, acc):
        _for_each(trip_chunks, lambda t: _chunk_copy(ys_hbm, 0, ys_scr, 0, sem.at[i]).wait())
        for j in range(BLOCKS_PER_TRIP):
            blk = i * BLOCKS_PER_TRIP + j
            onehot = _slot_onehot((blk * MXU_ROWS).astype(F32), pos, lanes)
            rows = ys_scr[pl.ds(pl.multiple_of(blk * blk_chunks, blk_chunks), blk_chunks)]
            acc = acc + _dot(onehot, rows.reshape(MXU_ROWS, D_MODEL))
        return acc

    routed = lax.fori_loop(0, nblk_ref[s], gather_blocks, jnp.zeros((SUB, D_MODEL), F32))
    y_ref[...] = x1_ref[...] + g2_ref[0] * (routed + shared)


def _combine(nblk, ncopy, src, post, xg, x1, g2, wgs, wus, wds, ys):
    n_tok = x1.shape[0]
    nsub = n_tok // SUB
    per_b = nsub // g2.shape[0]
    tab = pl.BlockSpec((1, 1, TABLE_W), lambda s, *_: (s, 0, 0), memory_space=pltpu.SMEM)
    const2 = lambda shape: pl.BlockSpec(shape, lambda s, *_: (0, 0))
    return pl.pallas_call(
        _combine_kernel,
        grid_spec=pltpu.PrefetchScalarGridSpec(
            num_scalar_prefetch=2, grid=(nsub,),
            in_specs=[tab, pl.BlockSpec((SUB, LANES), lambda s, *_: (s, 0)),
                      pl.BlockSpec((SUB, XG_W), lambda s, *_: (s, 0)),
                      pl.BlockSpec((SUB, D_MODEL), lambda s, *_: (s, 0)),
                      pl.BlockSpec((1, 1, D_MODEL), lambda s, *_: (s // per_b, 0, 0)),
                      const2((D_MODEL, D_EXPERT)), const2((D_MODEL, D_EXPERT)), const2((D_EXPERT, D_MODEL)),
                      pl.BlockSpec(memory_space=pl.ANY)],
            out_specs=pl.BlockSpec((SUB, D_MODEL), lambda s, *_: (s, 0)),
            scratch_shapes=[pltpu.VMEM((N_CHUNKS, CHUNK, D_MODEL), BF16), pltpu.VMEM((D_MODEL, D_EXPERT), BF16),
                            pltpu.VMEM((D_MODEL, D_EXPERT), BF16), pltpu.VMEM((D_EXPERT, D_MODEL), BF16),
                            pltpu.SemaphoreType.DMA((SUB_ROWS // (BLOCKS_PER_TRIP * MXU_ROWS),))]),
        out_shape=jax.ShapeDtypeStruct((n_tok, D_MODEL), F32),
        compiler_params=_params(("arbitrary",)),
        name="moe_combine",
    )(nblk, ncopy, src, post, xg, x1, g2, wgs, wus, wds, ys.reshape(-1, CHUNK, D_MODEL))


def _route_tables(cnt):
    nsub = cnt.shape[0]
    i32 = jnp.int32
    seg_end = jnp.cumsum(cnt, axis=1)
    seg_start = seg_end - cnt
    tot = jnp.sum(cnt, axis=0)
    tot_al = (tot + FFN_TM - 1) // FFN_TM * FFN_TM
    reg_end = jnp.cumsum(tot_al)
    reg_start = reg_end - tot_al
    base = reg_start[None, :] + jnp.cumsum(cnt, axis=0) - cnt
    n_real = seg_end[:, -1] // CHUNK
    k = jnp.arange(TABLE_W, dtype=i32)
    owner = jnp.sum((seg_end[:, None, :] // CHUNK) <= k[None, :, None], axis=2)
    owner = jnp.minimum(owner, N_EXPERTS - 1)
    delta = (base - seg_start) // CHUNK
    e_ids = jnp.arange(N_EXPERTS, dtype=i32)
    real_dst = jnp.sum(jnp.where(owner[:, :, None] == e_ids, delta[:, None, :], 0), axis=2) + k[None, :]
    fill_n = (tot_al - tot) // CHUNK
    fill_dst0 = (reg_start + tot) // CHUNK
    src = jnp.broadcast_to(k[None, :], (nsub, TABLE_W))
    dst = real_dst
    ncopy = n_real
    for r in range(-(-N_EXPERTS // nsub)):
        e_of = jnp.arange(nsub, dtype=i32) + r * nsub
        ok = e_of < N_EXPERTS
        e_cl = jnp.minimum(e_of, N_EXPERTS - 1)
        fn = jnp.where(ok, fill_n[e_cl], 0)
        j = k[None, :] - ncopy[:, None]
        is_fill = (j >= 0) & (j < fn[:, None])
        src = jnp.where(is_fill, ZERO_CHUNK, src)
        dst = jnp.where(is_fill, fill_dst0[e_cl][:, None] + j, dst)
        ncopy = ncopy + fn
    trip_rows = BLOCKS_PER_TRIP * MXU_ROWS
    disp_nblk = (n_real * CHUNK + trip_rows - 1) // trip_rows
    comb_n = disp_nblk * (trip_rows // CHUNK)
    comb_src = jnp.where(k[None, :] < n_real[:, None], real_dst, real_dst[:, 0:1])
    n_tiles = reg_end[-1] // FFN_TM
    t = jnp.arange((nsub * SUB_ROWS + N_EXPERTS * FFN_TM) // FFN_TM, dtype=i32)
    tile_expert = jnp.minimum(jnp.sum((reg_end[None, :] // FFN_TM) <= t[:, None], axis=1), N_EXPERTS - 1)
    as3 = lambda a: a.astype(i32).reshape(nsub, 1, TABLE_W)
    return (disp_nblk.astype(i32), n_real.astype(i32), ncopy.astype(i32), as3(src), as3(dst),
            comb_n.astype(i32), as3(comb_src), tile_expert.astype(i32), n_tiles.astype(i32).reshape(1))


def _moe_sparse(xg, post, posr, cnt, x1, g2, wg, wu, wd, wgs, wus, wds):
    B, S, _ = x1.shape
    n_tok = B * S
    nsub = n_tok // SUB
    (disp_nblk, n_real, disp_n, disp_src, disp_dst, comb_n, comb_src, tile_expert, n_tiles) = _route_tables(
        cnt[:, :, 0].astype(jnp.int32))
    n_rows = nsub * SUB_ROWS + N_EXPERTS * FFN_TM
    xg2 = xg.reshape(n_tok, XG_W)
    xs = _dispatch(disp_nblk, n_real, disp_n, n_tiles, disp_src, disp_dst, posr, xg2, n_rows)
    ys = _ffn(tile_expert, n_tiles, xs, wg, wu, wd)
    y = _combine(disp_nblk, comb_n, comb_src, post.reshape(n_tok, LANES), xg2, x1.reshape(n_tok, D_MODEL),
                 g2, wgs, wus, wds, ys)
    return y.reshape(B, S, D_MODEL)


def _t5_bucket(dist):
    max_exact = N_BUCKETS // 2
    df = jnp.maximum(dist, 1).astype(F32)
    large = max_exact + (jnp.log(df / max_exact) / math.log(MAX_DISTANCE / max_exact)
                         * (N_BUCKETS - max_exact)).astype(jnp.int32)
    large = jnp.minimum(large, N_BUCKETS - 1)
    return jnp.where(dist < max_exact, dist, large)


def _step_bias(rel_bias, g):
    dist = jnp.arange(N_KEYS, dtype=jnp.int32) * DILATIONS[g]
    cols = rel_bias[:, g * HEADS_PER_GROUP:(g + 1) * HEADS_PER_GROUP]
    return cols[_t5_bucket(dist)].T.astype(F32)


def _prompt_bias(bias_k):
    h = bias_k.shape[0]
    n = 3 * BLK
    row = jnp.concatenate([bias_k[:, ::-1], jnp.full((h, n - N_KEYS), NEG, F32)], axis=1)
    t = jnp.tile(row, (1, BLK))[:, :BLK * (n - 1)].reshape(h, BLK, n - 1)
    return t[:, :, :2 * BLK]


def _sample_bias(bias_k, wb, d):
    h = bias_k.shape[0]
    rev = bias_k[:, :0:-1]
    rows = []
    for t in range(4):
        if d == 1:
            rows.append(jnp.concatenate([jnp.full((h, t), NEG, F32), rev[:, :wb - t]], axis=1))
        else:
            r = jnp.arange(d)[None, None, :]
            rows.append(jnp.where(r == t, rev[:, :, None], NEG).reshape(h, wb))
    b1 = jnp.concatenate([jnp.stack(rows, axis=1), jnp.zeros((h, 4, wb), F32)], axis=1)
    rows = []
    for t in range(4):
        cols = [bias_k[:, (t - tn) // d] if (t >= tn and (t - tn) % d == 0) else jnp.full((h,), NEG, F32)
                for tn in range(4)]
        rows.append(jnp.concatenate([jnp.full((h, LANES - 4), NEG, F32), jnp.stack(cols, axis=1)], axis=1))
    b2 = jnp.concatenate([jnp.stack(rows, axis=1), jnp.zeros((h, 4, LANES), F32)], axis=1)
    return b1, b2


def _layer(x, mods, hist, caches, rel_bias, weights, *, sample):
    (norm1_g, norm2_g, w_in_bf, qg, kg, conv_w, bd, w_o_bf, w_r_t, b_r, w_gate_e, w_up_e, w_down_e,
     w_gate_s, w_up_s, w_down_s) = weights
    sh1, sc1, g1, sh2, sc2, g2 = mods
    B, S, _ = x.shape
    tile = 512
    dils = (1, 1, 1) if sample else DILATIONS
    res = _proj(x, sc1, sh1, norm1_g, w_in_bf, qg, kg, conv_w, bd, hist, tile=tile, dils=dils, sample=sample)
    states = []
    o_list, l_list = [], []
    if sample:
        qf, kt, vt, u, co = res
        nb = S // 4
        for g in range(N_GROUPS):
            d, wb = DILATIONS[g], WINDOWS[g]
            kc, vc = caches[g]
            bias_k = _step_bias(rel_bias, g)
            b1, b2 = _sample_bias(bias_k, wb, d)
            q = qf[0, :, g * GROUP_W:(g + 1) * GROUP_W].reshape(nb, 4, HEADS_PER_GROUP, HEAD_DIM)
            q = jnp.pad(q.transpose(0, 2, 1, 3), ((0, 0), (0, 0), (0, 4), (0, 0))).astype(BF16)
            new_t = lambda a: a[g * GROUP_W:(g + 1) * GROUP_W].reshape(HEADS_PER_GROUP, HEAD_DIM, S)
            ko, vo, o, lse = _attn_sample(q, kc, vc, new_t(kt), new_t(vt), b1, b2,
                                          bb={128: 8, 512: 4, 2048: 1}[wb])
            states += [ko.transpose(0, 3, 1, 2)[None], vo.transpose(0, 3, 1, 2)[None]]
            o = o[:, :, :4].transpose(0, 2, 1, 3).reshape(1, 1, S, GROUP_W)
            lse = jnp.broadcast_to(lse[:, :, :4, :1], (nb, HEADS_PER_GROUP, 4, HEAD_DIM))
            o_list.append(o)
            l_list.append(lse.transpose(0, 2, 1, 3).reshape(1, 1, S, GROUP_W))
        states.append(u.reshape(nb, 4, CONV_CH)[:, 2:][None])
    else:
        qkv, tails, (u, co) = res[:9], res[9:15], res[15:]
        for g in range(N_GROUPS):
            d, w = DILATIONS[g], WINDOWS[g]
            q, k, v = (a.reshape(B * d, S // d, GROUP_W) for a in qkv[3 * g:3 * g + 3])
            o, lse = _attn_prompt(q, k, v, _prompt_bias(_step_bias(rel_bias, g)))
            o_list.append(o.reshape(B, d, S // d, GROUP_W))
            l_list.append(lse.reshape(B, d, S // d, GROUP_W))
            keep = lambda a: a.reshape(B, HEADS_PER_GROUP, HEAD_DIM, w).transpose(0, 3, 1, 2)[None]
            states += [keep(tails[2 * g]), keep(tails[2 * g + 1])]
        states.append(u[:, 6:][None])
    x1, xg, gates, posr, post, cnt = _oproj(x, o_list, l_list, co, g1, sc2, sh2, norm2_g, w_o_bf, w_r_t, b_r,
                                            tile=tile, dils=dils)
    if sample:
        y = _moe(xg, gates, x1, g2, w_gate_e, w_up_e, w_down_e, w_gate_s, w_up_s, w_down_s, tile=tile)
    else:
        y = _moe_sparse(xg, post, posr, cnt, x1, g2, w_gate_e, w_up_e, w_down_e, w_gate_s, w_up_s, w_down_s)
    return y, states


def kernel(x_prompt, x_sample, c_prompt, c_sample, cache_k_w128, cache_v_w128, cache_k_w512, cache_v_w512, cache_k_w2048, cache_v_w2048, state_conv, rel_bias, norm1_g, norm2_g, w_ada, b_ada, w_in, q_norm_g, k_norm_g, conv_w, w_o, w_router, b_router, w_gate_e, w_up_e, w_down_e, w_gate_s, w_up_s, w_down_s):
    B = x_prompt.shape[0]
    DB, T = x_sample.shape[:2]
    n_c = B + DB
    c_all = jnp.pad(jnp.concatenate([c_prompt, c_sample], axis=0), ((0, (-n_c) % 8), (0, 0)))
    mod = _ada(c_all, w_ada[0], b_ada)
    chunks = [mod[:, i * D_MODEL:(i + 1) * D_MODEL] for i in range(6)]
    mods_p = [c[:B].reshape(B, 1, D_MODEL) for c in chunks]
    mods_s = [jnp.repeat(c[B:n_c], T, axis=0).reshape(1, DB * T, D_MODEL) for c in chunks]

    eye = jnp.arange(GROUP_W) // HEAD_DIM
    bd = jnp.where(eye[:, None] == eye[None, :], 1.0 / HEAD_DIM, 0.0).astype(BF16)
    weights = (norm1_g, norm2_g, w_in[0].astype(BF16),
               jnp.tile(q_norm_g, (1, HEADS_PER_GROUP)), jnp.tile(k_norm_g, (1, HEADS_PER_GROUP)),
               conv_w[0], bd, w_o[0].astype(BF16), w_router[0].T, b_router.reshape(N_EXPERTS, 1),
               w_gate_e[0], w_up_e[0], w_down_e[0], w_gate_s[0], w_up_s[0], w_down_s[0])

    yp, st_p = _layer(x_prompt, mods_p, None, None, rel_bias, weights, sample=False)

    s0, s1 = state_conv[0, :, 0], state_conv[0, :, 1]
    zero = jnp.zeros_like(s0)
    hist_a = jnp.stack([s0, s1, zero, zero], axis=1).reshape(DB * T, CONV_CH)
    hist_b = jnp.stack([s1, zero, zero, zero], axis=1).reshape(DB * T, CONV_CH)
    caches = [(ck[0].transpose(0, 2, 3, 1), cv[0].transpose(0, 2, 3, 1))
              for ck, cv in ((cache_k_w128, cache_v_w128), (cache_k_w512, cache_v_w512),
                             (cache_k_w2048, cache_v_w2048))]
    ys, st_s = _layer(x_sample.reshape(1, DB * T, D_MODEL), mods_s, (hist_a, hist_b), caches, rel_bias,
                      weights, sample=True)
    return (yp, ys.reshape(DB, T, D_MODEL), *st_p, *st_s)
```

```python
import functools
import math

import jax
import jax.numpy as jnp
from jax import lax
from jax.experimental import pallas as pl
from jax.experimental.pallas import tpu as pltpu

D_MODEL = 1024
HEAD_DIM = 64
HEADS_PER_GROUP = 4
GROUP_W = HEADS_PER_GROUP * HEAD_DIM
WINDOWS = (128, 512, 2048)
DILATIONS = (1, 4, 16)
N_GROUPS = 3
ATTN_W = N_GROUPS * GROUP_W
CONV_CH = 256
N_PROJ = 3 * ATTN_W + 3 * CONV_CH
N_STEPS = 128
N_KEYS = N_STEPS + 1
BLK = 128
N_BUCKETS = 32
MAX_DISTANCE = WINDOWS[-1]
N_EXPERTS = 64
TOP_K = 8
N_ROUTE_GROUPS = 8
GROUP_SIZE = N_EXPERTS // N_ROUTE_GROUPS
TOPK_GROUPS = 4
D_EXPERT = 256
ROUTED_SCALE = 2.5
RMS_EPS = 1e-6
NEG = -1e30
LANES = 128
SUB = 256
CHUNK = 16
XG_W = D_MODEL + LANES
VMEM_LIMIT = 56 * 1024 * 1024

F32 = jnp.float32
BF16 = jnp.bfloat16


def _dot(a, b):
    return jnp.dot(a, b, preferred_element_type=F32)


def _dot_nt(a, b):
    return lax.dot_general(a, b, (((1,), (1,)), ((), ())), preferred_element_type=F32)


def _silu(a):
    return a / (1.0 + jnp.exp(-a))


def _params(sem):
    return pltpu.CompilerParams(dimension_semantics=sem, vmem_limit_bytes=VMEM_LIMIT)


def _ada_kernel(c_ref, w_ref, b_ref, o_ref):
    s = _silu(c_ref[...])
    o_ref[...] = _dot(s.astype(BF16), w_ref[...].astype(BF16)) + b_ref[...]


def _ada(c, w_ada, b_ada):
    n = c.shape[0]
    nc = w_ada.shape[1] // D_MODEL
    return pl.pallas_call(
        _ada_kernel,
        grid=(nc,),
        in_specs=[pl.BlockSpec((n, D_MODEL), lambda j: (0, 0)),
                  pl.BlockSpec((D_MODEL, D_MODEL), lambda j: (0, j)),
                  pl.BlockSpec((1, D_MODEL), lambda j: (0, j))],
        out_specs=pl.BlockSpec((n, D_MODEL), lambda j: (0, j)),
        out_shape=jax.ShapeDtypeStruct((n, w_ada.shape[1]), F32),
        compiler_params=_params(("arbitrary",)),
        name="ada",
    )(c, w_ada, b_ada)


def _proj_kernel(*refs, tile, dils, sample, nt):
    if sample:
        (x_ref, sc_ref, sh_ref, ng_ref, w_ref, qg_ref, kg_ref, cw_ref, bd_ref, ha_ref, hb_ref,
         qf_ref, kt_ref, vt_ref, u_ref, co_ref, u_scr) = refs
    else:
        (x_ref, sc_ref, sh_ref, ng_ref, w_ref, qg_ref, kg_ref, cw_ref, bd_ref,
         q0, k0, v0, q1, k1, v1, q2, k2, v2, kt0, vt0, kt1, vt1, kt2, vt2, u_ref, co_ref, slab, u_scr) = refs
        qkv_out = ((q0, k0, v0), (q1, k1, v1), (q2, k2, v2))
        tails = ((kt0, vt0), (kt1, vt1), (kt2, vt2))
        j = pl.program_id(1)

    x = x_ref[0]
    ms = jnp.mean(x * x, axis=-1, keepdims=True)
    h = x * lax.rsqrt(ms + RMS_EPS) * ng_ref[...]
    h = h * (1.0 + sc_ref[0]) + sh_ref[0]
    proj = _dot(h.astype(BF16), w_ref[...])

    bd = bd_ref[...]

    def headnorm(z, g):
        zz = z * z
        hi = zz.astype(BF16)
        lo = (zz - hi.astype(F32)).astype(BF16)
        msq = _dot(hi, bd) + _dot(lo, bd)
        return z * lax.rsqrt(msq + RMS_EPS) * g

    slab_i = 0
    for g in range(N_GROUPS):
        c0 = g * GROUP_W
        qn = headnorm(proj[:, c0:c0 + GROUP_W], qg_ref[...]) * (HEAD_DIM ** -0.5)
        kn = headnorm(proj[:, ATTN_W + c0:ATTN_W + c0 + GROUP_W], kg_ref[...])
        vv = proj[:, 2 * ATTN_W + c0:2 * ATTN_W + c0 + GROUP_W]
        if sample:
            qf_ref[0, :, c0:c0 + GROUP_W] = qn
            kt_ref[c0:c0 + GROUP_W, :] = kn.T
            vt_ref[c0:c0 + GROUP_W, :] = vv.T
            continue
        keep = min(WINDOWS[g], tile)

        @pl.when(j >= nt - max(WINDOWS[g] // tile, 1))
        def _(kn=kn, vv=vv, g=g, keep=keep):
            tails[g][0][0] = kn[tile - keep:, :].T
            tails[g][1][0] = vv[tile - keep:, :].T

        d = dils[g]
        for val, out in zip((qn, kn, vv), qkv_out[g]):
            if d == 1:
                out[0, 0] = val.astype(BF16)
                continue
            n = tile // d
            for half in range(GROUP_W // LANES):
                slab[slab_i] = val[:, half * LANES:(half + 1) * LANES]
                for r in range(d):
                    out[0, r, :, half * LANES:(half + 1) * LANES] = (
                        slab[slab_i, pl.ds(r, n, stride=d), :].astype(BF16))
                slab_i += 1

    base = 3 * ATTN_W
    u = proj[:, base + 2 * CONV_CH:base + 3 * CONV_CH] * proj[:, base:base + CONV_CH]
    gate_b = proj[:, base + CONV_CH:base + 2 * CONV_CH]
    if sample:
        u_scr[0:8, :] = jnp.zeros((8, CONV_CH), F32)
    else:
        @pl.when(j == 0)
        def _():
            u_scr[0:8, :] = jnp.zeros((8, CONV_CH), F32)

        @pl.when(j > 0)
        def _():
            u_scr[0:8, :] = u_scr[tile:tile + 8, :]

    u_scr[8:tile + 8, :] = u
    um1 = u_scr[7:tile + 7, :]
    um2 = u_scr[6:tile + 6, :]
    if sample:
        t = lax.broadcasted_iota(jnp.int32, (tile, CONV_CH), 0) % 4
        um1 = jnp.where(t >= 1, um1, 0.0) + hb_ref[...]
        um2 = jnp.where(t >= 2, um2, 0.0) + ha_ref[...]
    cw = cw_ref[...]
    conv = cw[0:1] * um2 + cw[1:2] * um1 + cw[2:3] * u
    u_ref[0] = u if sample else u[tile - 8:, :]
    co_ref[0] = (gate_b * conv).astype(BF16)


def _proj(x, sc, sh, ng, w_bf, qg, kg, cw, bd, hist, *, tile, dils, sample):
    B, S, _ = x.shape
    nt = S // tile
    tm = sc.shape[1]
    mod_spec = pl.BlockSpec((1, tm, D_MODEL), (lambda b, j: (b, j, 0)) if tm > 1 else (lambda b, j: (b, 0, 0)))
    const2 = lambda shape: pl.BlockSpec(shape, lambda b, j: (0, 0))
    in_specs = [pl.BlockSpec((1, tile, D_MODEL), lambda b, j: (b, j, 0)), mod_spec, mod_spec,
                const2((1, D_MODEL)), const2((D_MODEL, N_PROJ)), const2((1, GROUP_W)),
                const2((1, GROUP_W)), const2((3, CONV_CH)), const2((GROUP_W, GROUP_W))]
    args = [x, sc, sh, ng, w_bf, qg, kg, cw, bd]
    nat = lambda w: pl.BlockSpec((1, tile, w), lambda b, j: (b, j, 0))
    out_specs, out_shape = [], []
    scratch = []
    if sample:
        assert B == 1 and nt == 1
        in_specs += [pl.BlockSpec((tile, CONV_CH), lambda b, j: (j, 0))] * 2
        args += list(hist)
        out_specs += [nat(ATTN_W)] + [pl.BlockSpec((ATTN_W, tile), lambda b, j: (0, j))] * 2 + [nat(CONV_CH)]
        out_shape += [jax.ShapeDtypeStruct((B, S, ATTN_W), F32)]
        out_shape += [jax.ShapeDtypeStruct((ATTN_W, S), F32)] * 2
        out_shape += [jax.ShapeDtypeStruct((B, S, CONV_CH), F32)]
    else:
        for d in dils:
            for _ in range(3):
                out_specs.append(pl.BlockSpec((1, d, tile // d, GROUP_W), lambda b, j: (b, 0, j, 0)))
                out_shape.append(jax.ShapeDtypeStruct((B, d, S // d, GROUP_W), BF16))
        for w in WINDOWS:
            keep, first = min(w, tile), nt - max(w // tile, 1)
            for _ in range(2):
                out_specs.append(pl.BlockSpec((1, GROUP_W, keep),
                                              lambda b, j, first=first: (b, 0, jnp.maximum(j - first, 0))))
                out_shape.append(jax.ShapeDtypeStruct((B, GROUP_W, w), F32))
        out_specs.append(pl.BlockSpec((1, 8, CONV_CH), lambda b, j: (b, 0, 0)))
        out_shape.append(jax.ShapeDtypeStruct((B, 8, CONV_CH), F32))
        n_slabs = sum(3 * (GROUP_W // LANES) for d in dils if d > 1)
        scratch.append(pltpu.VMEM((n_slabs, tile, LANES), F32))
    out_specs.append(nat(CONV_CH))
    out_shape.append(jax.ShapeDtypeStruct((B, S, CONV_CH), BF16))
    scratch.append(pltpu.VMEM((tile + 8, CONV_CH), F32))
    return pl.pallas_call(
        functools.partial(_proj_kernel, tile=tile, dils=dils, sample=sample, nt=nt),
        grid=(B, nt),
        in_specs=in_specs,
        out_specs=out_specs,
        out_shape=out_shape,
        scratch_shapes=scratch,
        compiler_params=_params(("arbitrary", "arbitrary")),
        name="proj_sample" if sample else "proj_prompt",
    )(*args)


def _attn_kernel(q_ref, kp_ref, kc_ref, vp_ref, vc_ref, b_ref, o_ref, l_ref, s_scr, m_scr, o_scr, z_scr,
                 *, nq):
    first = pl.program_id(1) == 0
    low = lax.broadcasted_iota(jnp.int32, (BLK, LANES), 1) < HEAD_DIM
    ones = jnp.ones((BLK, LANES), BF16)
    chains = [(j, pair, sub) for j in range(nq) for pair in range(GROUP_W // LANES) for sub in range(2)]

    def operands(j, pair):
        rows = slice(j * BLK, (j + 1) * BLK)
        sl = slice(pair * LANES, (pair + 1) * LANES)
        if j == 0:
            return rows, sl, kp_ref[0, :, sl], vp_ref[0, :, sl]
        prows = slice((j - 1) * BLK, j * BLK)
        return rows, sl, kc_ref[0, prows, sl], vc_ref[0, prows, sl]

    for c, (j, pair, sub) in enumerate(chains):
        rows, sl, kp, _ = operands(j, pair)
        q = q_ref[0, rows, sl]
        qm = jnp.where(low if sub == 0 else ~low, q, jnp.zeros_like(q))
        hh = 2 * pair + sub
        sp = _dot_nt(qm, kp) + b_ref[hh, :, 0:BLK]
        if j == 0:
            sp = jnp.where(first, NEG, sp)
        sc = _dot_nt(qm, kc_ref[0, rows, sl]) + b_ref[hh, :, BLK:2 * BLK]
        s_scr[c, :, 0:BLK] = sp
        s_scr[c, :, BLK:2 * BLK] = sc
        m_scr[c] = jnp.broadcast_to(jnp.max(jnp.maximum(sp, sc), axis=-1, keepdims=True), (BLK, LANES))

    for c, (j, pair, sub) in enumerate(chains):
        rows, sl, _, vp = operands(j, pair)
        m = m_scr[c]
        pp = jnp.exp(s_scr[c, :, 0:BLK] - m).astype(BF16)
        pc = jnp.exp(s_scr[c, :, BLK:2 * BLK] - m).astype(BF16)
        o_scr[c] = _dot(pp, vp) + _dot(pc, vc_ref[0, rows, sl])
        z_scr[c] = _dot(pp, ones) + _dot(pc, ones)

    for c in range(0, len(chains), 2):
        j, pair, _ = chains[c]
        rows = slice(j * BLK, (j + 1) * BLK)
        sl = slice(pair * LANES, (pair + 1) * LANES)
        o_sub = [o_scr[c + sub] / z_scr[c + sub] for sub in range(2)]
        l_sub = [m_scr[c + sub] + jnp.log(z_scr[c + sub]) for sub in range(2)]
        o_ref[0, rows, sl] = jnp.where(low, o_sub[0], o_sub[1])
        l_ref[0, rows, sl] = jnp.where(low, l_sub[0], l_sub[1])


def _attn_prompt(q, k, v, bias):
    Z, L, _ = q.shape
    qt = min(L, 4 * BLK)
    nq = qt // BLK
    cur = pl.BlockSpec((1, qt, GROUP_W), lambda z, i: (z, i, 0))
    prev = pl.BlockSpec((1, BLK, GROUP_W), lambda z, i: (z, jnp.maximum(i * nq - 1, 0), 0))
    return pl.pallas_call(
        functools.partial(_attn_kernel, nq=nq),
        grid=(Z, L // qt),
        in_specs=[cur, prev, cur, prev, cur,
                  pl.BlockSpec((HEADS_PER_GROUP, BLK, 2 * BLK), lambda z, i: (0, 0, 0))],
        out_specs=[cur, cur],
        out_shape=[jax.ShapeDtypeStruct((Z, L, GROUP_W), F32)] * 2,
        scratch_shapes=[pltpu.VMEM((nq * HEADS_PER_GROUP, BLK, 2 * BLK), F32)]
        + [pltpu.VMEM((nq * HEADS_PER_GROUP, BLK, LANES), F32)] * 3,
        compiler_params=_params(("arbitrary", "arbitrary")),
        name="attn_prompt",
    )(q, k, k, v, v, bias)


def _attn_sample_kernel(q_ref, kc_ref, vc_ref, nk_ref, nv_ref, b1_ref, b2_ref,
                        ko_ref, vo_ref, o_ref, l_ref, *, bb, wb):
    lane = lax.broadcasted_iota(jnp.int32, (HEAD_DIM, LANES), 1)
    new_cols = lane >= LANES - 4
    per_tile = LANES // 4
    for b in range(bb):
        gb = pl.program_id(0) * bb + b
        tile_at = pl.ds(pl.multiple_of((gb // per_tile) * LANES, LANES), LANES)
        shift = (LANES - 4) - 4 * (gb % per_tile)
        for hh in range(HEADS_PER_GROUP):
            q = q_ref[b, hh]
            kt = kc_ref[b, hh]
            vt = vc_ref[b, hh]
            nkt = pltpu.roll(nk_ref[hh, :, tile_at], shift, axis=1)
            nvt = pltpu.roll(nv_ref[hh, :, tile_at], shift, axis=1)
            s1 = _dot(q, kt.astype(BF16)) + b1_ref[hh]
            s2 = _dot(q, nkt.astype(BF16)) + b2_ref[hh]
            m = jnp.maximum(jnp.max(s1, axis=-1, keepdims=True), jnp.max(s2, axis=-1, keepdims=True))
            p1 = jnp.exp(s1 - m)
            p2 = jnp.exp(s2 - m)
            z = jnp.sum(p1, axis=-1, keepdims=True) + jnp.sum(p2, axis=-1, keepdims=True)
            o = _dot_nt(p1.astype(BF16), vt.astype(BF16)) + _dot_nt(p2.astype(BF16), nvt.astype(BF16))
            o_ref[b, hh] = o / z
            l_ref[b, hh] = jnp.broadcast_to(m + jnp.log(z), (8, LANES))
            for src, new, dst in ((kt, nkt, ko_ref), (vt, nvt, vo_ref)):
                rolled = pltpu.roll(src, wb - 4, axis=1)
                if wb > LANES:
                    dst[b, hh, :, 0:wb - LANES] = rolled[:, 0:wb - LANES]
                dst[b, hh, :, wb - LANES:wb] = jnp.where(new_cols, new, rolled[:, wb - LANES:wb])


def _attn_sample(q, kc, vc, nk, nv, b1, b2, *, bb):
    nb, _, _, wb = kc.shape
    blk = lambda *tail: pl.BlockSpec((bb, HEADS_PER_GROUP) + tail, lambda i: (i, 0, 0, 0))
    const = lambda shape: pl.BlockSpec(shape, lambda i: (0, 0, 0))
    return pl.pallas_call(
        functools.partial(_attn_sample_kernel, bb=bb, wb=wb),
        grid=(nb // bb,),
        in_specs=[blk(8, HEAD_DIM), blk(HEAD_DIM, wb), blk(HEAD_DIM, wb), const(nk.shape), const(nv.shape),
                  const((HEADS_PER_GROUP, 8, wb)), const((HEADS_PER_GROUP, 8, LANES))],
        out_specs=[blk(HEAD_DIM, wb), blk(HEAD_DIM, wb), blk(8, HEAD_DIM), blk(8, LANES)],
        out_shape=[jax.ShapeDtypeStruct(kc.shape, F32), jax.ShapeDtypeStruct(kc.shape, F32),
                   jax.ShapeDtypeStruct((nb, HEADS_PER_GROUP, 8, HEAD_DIM), F32),
                   jax.ShapeDtypeStruct((nb, HEADS_PER_GROUP, 8, LANES), F32)],
        compiler_params=_params(("arbitrary",)),
        name="attn_sample",
    )(q, kc, vc, nk, nv, b1, b2)


def _first_max(v, ids, sentinel):
    m = jnp.max(v, axis=0, keepdims=True)
    idx = jnp.min(jnp.where(v == m, ids, sentinel), axis=0, keepdims=True)
    return m, ids == idx


def _oproj_kernel(x_ref, o0, o1, o2, l0, l1, l2, co_ref, g1_ref, sc_ref, sh_ref, ng_ref, wo_ref,
                  wr_ref, br_ref, x1_ref, xg_ref, gt_ref, posr_ref, post_ref, cnt_ref, slab, *, tile, dils):
    o_refs, l_refs = (o0, o1, o2), (l0, l1, l2)
    outs, lses = [], []
    slab_i = 0
    for g, d in enumerate(dils):
        if d == 1:
            outs.append(o_refs[g][0, 0])
            lses.append(l_refs[g][0, 0])
            continue
        n = tile // d
        for ref, dest in ((o_refs[g], outs), (l_refs[g], lses)):
            halves = []
            for half in range(GROUP_W // LANES):
                for r in range(d):
                    slab[slab_i, pl.ds(r, n, stride=d), :] = ref[0, r, :, half * LANES:(half + 1) * LANES]
                halves.append(slab[slab_i])
                slab_i += 1
            dest.append(jnp.concatenate(halves, axis=1))

    m = jnp.maximum(jnp.maximum(lses[0], lses[1]), lses[2])
    es = [jnp.exp(l - m) for l in lses]
    den = es[0] + es[1] + es[2]
    cat = [(outs[g] * (es[g] / den)).astype(BF16) for g in range(N_GROUPS)]
    cat.append(co_ref[0].astype(BF16))
    cat = jnp.concatenate(cat, axis=1)
    x1 = x_ref[0] + g1_ref[0] * _dot(cat, wo_ref[...])
    x1_ref[0] = x1

    ms = jnp.mean(x1 * x1, axis=-1, keepdims=True)
    h2 = x1 * lax.rsqrt(ms + RMS_EPS) * ng_ref[...]
    h2 = h2 * (1.0 + sc_ref[0]) + sh_ref[0]
    hh = h2.astype(BF16)
    xg_ref[0, :, 0:D_MODEL] = hh

    hl = (h2 - hh.astype(F32)).astype(BF16)
    wr = wr_ref[...]
    wh = wr.astype(BF16)
    wl = (wr - wh.astype(F32)).astype(BF16)
    logits = _dot_nt(wh, hh) + _dot_nt(wh, hl) + _dot_nt(wl, hh)
    scores = 1.0 / (1.0 + jnp.exp(-logits))
    sel = scores + br_ref[...]

    ids = lax.broadcasted_iota(jnp.int32, (GROUP_SIZE, tile), 0)
    ninf = -jnp.inf
    sel_g = [sel[g * GROUP_SIZE:(g + 1) * GROUP_SIZE] for g in range(N_ROUTE_GROUPS)]
    gscore = jnp.zeros((N_ROUTE_GROUPS, tile), F32)
    for g in range(N_ROUTE_GROUPS):
        m1, oh = _first_max(sel_g[g], ids, GROUP_SIZE)
        m2 = jnp.max(jnp.where(oh, ninf, sel_g[g]), axis=0, keepdims=True)
        gscore = jnp.where(ids == g, m1 + m2, gscore)
    gsel = jnp.zeros((N_ROUTE_GROUPS, tile), F32)
    for _ in range(TOPK_GROUPS):
        _, oh = _first_max(gscore, ids, N_ROUTE_GROUPS)
        gsel = jnp.where(oh, 1.0, gsel)
        gscore = jnp.where(oh, ninf, gscore)
    cand = [jnp.where(gsel[g:g + 1] > 0.0, sel_g[g], ninf) for g in range(N_ROUTE_GROUPS)]
    chosen = [jnp.zeros((GROUP_SIZE, tile), F32) for _ in range(N_ROUTE_GROUPS)]
    picks = []
    for _ in range(TOP_K):
        mx = cand[0]
        for g in range(1, N_ROUTE_GROUPS):
            mx = jnp.maximum(mx, cand[g])
        mx = jnp.max(mx, axis=0, keepdims=True)
        idx = jnp.where(cand[0] == mx, ids, N_EXPERTS)
        for g in range(1, N_ROUTE_GROUPS):
            idx = jnp.minimum(idx, jnp.where(cand[g] == mx, ids + g * GROUP_SIZE, N_EXPERTS))
        idx = jnp.min(idx, axis=0, keepdims=True)
        ohs = [(ids + g * GROUP_SIZE) == idx for g in range(N_ROUTE_GROUPS)]
        picks.append(ohs)
        for g in range(N_ROUTE_GROUPS):
            chosen[g] = jnp.where(ohs[g], 1.0, chosen[g])
            cand[g] = jnp.where(ohs[g], ninf, cand[g])
    wts = [jnp.where(chosen[g] > 0.0, scores[g * GROUP_SIZE:(g + 1) * GROUP_SIZE], 0.0)
           for g in range(N_ROUTE_GROUPS)]
    wsum = jnp.sum(wts[0], axis=0, keepdims=True)
    for g in range(1, N_ROUTE_GROUPS):
        wsum = wsum + jnp.sum(wts[g], axis=0, keepdims=True)
    gates_t = jnp.concatenate([w / wsum * ROUTED_SCALE for w in wts], axis=0)
    gt_ref[0] = jnp.concatenate([gates_t, jnp.zeros((LANES - N_EXPERTS, tile), F32)], axis=0).T
    g_hi = gates_t.astype(BF16).astype(F32)
    xg_ref[0, :, D_MODEL:D_MODEL + LANES] = jnp.concatenate([g_hi, gates_t - g_hi], axis=0).T.astype(BF16)

    er = lax.broadcasted_iota(jnp.int32, (N_EXPERTS, N_EXPERTS), 0)
    ec = lax.broadcasted_iota(jnp.int32, (N_EXPERTS, N_EXPERTS), 1)
    lower = jnp.where(ec < er, 1.0, 0.0).astype(BF16)
    tr = lax.broadcasted_iota(jnp.int32, (SUB, SUB), 0)
    tc = lax.broadcasted_iota(jnp.int32, (SUB, SUB), 1)
    upper = jnp.where(tr < tc, 1.0, 0.0).astype(BF16)
    for u in range(tile // SUB):
        cols = slice(u * SUB, (u + 1) * SUB)
        ch = jnp.concatenate([c[:, cols] for c in chosen], axis=0)
        cnt = jnp.sum(ch, axis=1, keepdims=True)
        cpad = jnp.broadcast_to(jnp.ceil(cnt / CHUNK) * CHUNK, (N_EXPERTS, LANES))
        cnt_ref[u] = cpad
        seg0 = _dot(lower, cpad.astype(BF16))[:, 0:1]
        slot = seg0 + _dot(ch.astype(BF16), upper)
        rows = []
        for k in range(TOP_K):
            acc = jnp.where(picks[k][0][:, cols], slot[0:GROUP_SIZE], 0.0)
            for g in range(1, N_ROUTE_GROUPS):
                acc = acc + jnp.where(picks[k][g][:, cols], slot[g * GROUP_SIZE:(g + 1) * GROUP_SIZE], 0.0)
            rows.append(jnp.sum(acc, axis=0, keepdims=True))
        posr = jnp.concatenate(rows, axis=0)
        posr_ref[u] = posr
        post_ref[0, cols, :] = jnp.concatenate([posr, jnp.zeros((LANES - TOP_K, SUB), F32)], axis=0).T


def _oproj(x, o_list, l_list, co, g1, sc, sh, ng, wo_bf, wr_t, br, *, tile, dils):
    B, S, _ = x.shape
    tm = g1.shape[1]
    mod_spec = pl.BlockSpec((1, tm, D_MODEL), (lambda b, j: (b, j, 0)) if tm > 1 else (lambda b, j: (b, 0, 0)))
    const2 = lambda shape: pl.BlockSpec(shape, lambda b, j: (0, 0))
    nat = lambda w: pl.BlockSpec((1, tile, w), lambda b, j: (b, j, 0))
    dspec = [pl.BlockSpec((1, d, tile // d, GROUP_W), lambda b, j: (b, 0, j, 0)) for d in dils]
    n_slabs = sum(2 * (GROUP_W // LANES) for d in dils if d > 1)
    nsub_t = tile // SUB
    return pl.pallas_call(
        functools.partial(_oproj_kernel, tile=tile, dils=dils),
        grid=(B, S // tile),
        in_specs=[nat(D_MODEL)] + dspec + dspec + [nat(CONV_CH), mod_spec, mod_spec, mod_spec,
                  const2((1, D_MODEL)), const2((D_MODEL, D_MODEL)), const2((N_EXPERTS, D_MODEL)),
                  const2((N_EXPERTS, 1))],
        out_specs=[nat(D_MODEL), nat(XG_W), nat(LANES),
                   pl.BlockSpec((nsub_t, TOP_K, SUB), lambda b, j: (b * (S // tile) + j, 0, 0)),
                   nat(LANES),
                   pl.BlockSpec((nsub_t, N_EXPERTS, LANES), lambda b, j: (b * (S // tile) + j, 0, 0))],
        out_shape=[jax.ShapeDtypeStruct((B, S, D_MODEL), F32), jax.ShapeDtypeStruct((B, S, XG_W), BF16),
                   jax.ShapeDtypeStruct((B, S, LANES), F32),
                   jax.ShapeDtypeStruct((B * S // SUB, TOP_K, SUB), F32),
                   jax.ShapeDtypeStruct((B, S, LANES), F32),
                   jax.ShapeDtypeStruct((B * S // SUB, N_EXPERTS, LANES), F32)],
        scratch_shapes=[pltpu.VMEM((max(n_slabs, 1), tile, LANES), F32)],
        compiler_params=_params(("arbitrary", "arbitrary")),
        name="oproj",
    )(x, *o_list, *l_list, co, g1, sc, sh, ng, wo_bf, wr_t, br)


def _moe_kernel(h_ref, gt_ref, x1_ref, g2_ref, wg_ref, wu_ref, wd_ref, wgs_ref, wus_ref, wds_ref,
                y_ref, acc, *, tile):
    e = pl.program_id(2)
    h = h_ref[0]

    @pl.when(e == 0)
    def _():
        a = _dot(h, wgs_ref[...].astype(BF16))
        b = _dot(h, wus_ref[...].astype(BF16))
        acc[...] = _dot((_silu(a) * b).astype(BF16), wds_ref[...].astype(BF16))

    a = _dot(h, wg_ref[0].astype(BF16))
    b = _dot(h, wu_ref[0].astype(BF16))
    lane = lax.broadcasted_iota(jnp.int32, (tile, LANES), 1)
    gcol = jnp.sum(jnp.where(lane == e, gt_ref[0], 0.0), axis=1, keepdims=True)
    hc = _silu(a) * b * gcol
    acc[...] += _dot(hc.astype(BF16), wd_ref[0].astype(BF16))

    @pl.when(e == N_EXPERTS - 1)
    def _():
        y_ref[0] = x1_ref[0] + g2_ref[0] * acc[...]


def _moe(h2, gates, x1, g2, wg, wu, wd, wgs, wus, wds, *, tile):
    B, S, _ = h2.shape
    tm = g2.shape[1]
    mod_spec = pl.BlockSpec((1, tm, D_MODEL),
                            (lambda b, j, e: (b, j, 0)) if tm > 1 else (lambda b, j, e: (b, 0, 0)))
    nat = lambda w: pl.BlockSpec((1, tile, w), lambda b, j, e: (b, j, 0))
    const2 = lambda shape: pl.BlockSpec(shape, lambda b, j, e: (0, 0))
    return pl.pallas_call(
        functools.partial(_moe_kernel, tile=tile),
        grid=(B, S // tile, N_EXPERTS),
        in_specs=[nat(D_MODEL), nat(LANES), nat(D_MODEL), mod_spec,
                  pl.BlockSpec((1, D_MODEL, D_EXPERT), lambda b, j, e: (e, 0, 0)),
                  pl.BlockSpec((1, D_MODEL, D_EXPERT), lambda b, j, e: (e, 0, 0)),
                  pl.BlockSpec((1, D_EXPERT, D_MODEL), lambda b, j, e: (e, 0, 0)),
                  const2((D_MODEL, D_EXPERT)), const2((D_MODEL, D_EXPERT)), const2((D_EXPERT, D_MODEL))],
        out_specs=nat(D_MODEL),
        out_shape=jax.ShapeDtypeStruct((B, S, D_MODEL), F32),
        scratch_shapes=[pltpu.VMEM((tile, D_MODEL), F32)],
        compiler_params=_params(("arbitrary", "arbitrary", "arbitrary")),
        name="moe",
    )(h2, gates, x1, g2, wg, wu, wd, wgs, wus, wds)


FFN_TM = 1024
FFN_SB = 256
SUB_ROWS = SUB * TOP_K + N_EXPERTS * CHUNK
N_CHUNKS = SUB_ROWS // CHUNK
ZERO_CHUNK = N_CHUNKS
TABLE_W = 256
MXU_ROWS = 256
BLOCKS_PER_TRIP = 2


def _slot_onehot(first, pos_list, axis_iota):
    shape = axis_iota.shape
    hit = axis_iota == jnp.broadcast_to(pos_list[0] - first, shape).astype(BF16)
    for p in pos_list[1:]:
        hit = hit | (axis_iota == jnp.broadcast_to(p - first, shape).astype(BF16))
    return jnp.where(hit, jnp.ones(shape, BF16), jnp.zeros(shape, BF16))


def _chunk_copy(src, src_chunk, dst, dst_chunk, sem):
    return pltpu.make_async_copy(src.at[src_chunk], dst.at[dst_chunk], sem)


def _for_each(n, body):
    def two(t, carry):
        body(2 * t)
        body(2 * t + 1)
        return carry

    lax.fori_loop(0, n // 2, two, 0)

    @pl.when(n % 2 == 1)
    def _():
        body(n - 1)


def _dispatch_kernel(nblk_ref, nreal_ref, ncopy_ref, ntile_ref, src_ref, dst_ref, posr_ref, xg_ref, xs_hbm,
                     xs_scr, zero_scr, sem, tail_sem, *, nsub, n_tiles_max):
    s = pl.program_id(0)

    tile_chunks = FFN_TM // CHUNK
    blk_chunks = MXU_ROWS // CHUNK
    zero_scr[...] = jnp.zeros((tile_chunks, CHUNK, XG_W), BF16)
    tail_blocks = [ntile_ref[0] + s + r * nsub for r in range(-(-(n_tiles_max) // nsub))]

    def tail_copy(t):
        return pltpu.make_async_copy(
            zero_scr, xs_hbm.at[pl.ds(pl.multiple_of(t * tile_chunks, tile_chunks), tile_chunks)], tail_sem)

    for t in tail_blocks:
        @pl.when(t < n_tiles_max)
        def _():
            tail_copy(t).start()

    xg = xg_ref[...]
    pos = [posr_ref[0, k:k + 1, :] for k in range(TOP_K)]
    rows = lax.broadcasted_iota(jnp.int32, (MXU_ROWS, SUB), 0).astype(BF16)
    xs_scr[ZERO_CHUNK] = jnp.zeros((CHUNK, XG_W), BF16)

    def sort_blocks(i, carry):
        for j in range(BLOCKS_PER_TRIP):
            blk = i * BLOCKS_PER_TRIP + j
            onehot = _slot_onehot((blk * MXU_ROWS).astype(F32), pos, rows)
            sorted_rows = _dot(onehot, xg).astype(BF16)
            xs_scr[pl.ds(pl.multiple_of(blk * blk_chunks, blk_chunks), blk_chunks)] = (
                sorted_rows.reshape(blk_chunks, CHUNK, XG_W))
        k0 = i * trip_chunks
        _for_each(jnp.clip(n_real - k0, 0, trip_chunks), lambda t: send(k0 + t))
        return carry

    n_real = nreal_ref[s]
    n = ncopy_ref[s]
    trip_chunks = BLOCKS_PER_TRIP * blk_chunks

    def send(i):
        _chunk_copy(xs_scr, src_ref[0, 0, i], xs_hbm, dst_ref[0, 0, i], sem).start()

    lax.fori_loop(0, nblk_ref[s], sort_blocks, 0)
    _for_each(n - n_real, lambda t: send(n_real + t))
    _for_each(n, lambda i: _chunk_copy(xs_scr, 0, xs_hbm, 0, sem).wait())
    for t in tail_blocks:
        @pl.when(t < n_tiles_max)
        def _():
            tail_copy(t).wait()


def _dispatch(nblk, n_real, ncopy, n_tiles, src, dst, posr, xg, n_rows):
    nsub = posr.shape[0]
    tab = pl.BlockSpec((1, 1, TABLE_W), lambda s, *_: (s, 0, 0), memory_space=pltpu.SMEM)
    return pl.pallas_call(
        functools.partial(_dispatch_kernel, nsub=nsub, n_tiles_max=n_rows // FFN_TM),
        grid_spec=pltpu.PrefetchScalarGridSpec(
            num_scalar_prefetch=4, grid=(nsub,),
            in_specs=[tab, tab, pl.BlockSpec((1, TOP_K, SUB), lambda s, *_: (s, 0, 0)),
                      pl.BlockSpec((SUB, XG_W), lambda s, *_: (s, 0))],
            out_specs=pl.BlockSpec(memory_space=pl.ANY),
            scratch_shapes=[pltpu.VMEM((N_CHUNKS + 1, CHUNK, XG_W), BF16),
                            pltpu.VMEM((FFN_TM // CHUNK, CHUNK, XG_W), BF16),
                            pltpu.SemaphoreType.DMA(()), pltpu.SemaphoreType.DMA(())]),
        out_shape=jax.ShapeDtypeStruct((n_rows // CHUNK, CHUNK, XG_W), BF16),
        compiler_params=_params(("arbitrary",)),
        name="moe_dispatch",
    )(nblk, n_real, ncopy, n_tiles, src, dst, posr, xg).reshape(n_rows, XG_W)


def _ffn_kernel(te_ref, rows_ref, nt_ref, xs_ref, wg_ref, wu_ref, wd_ref, ys_ref, wg_s, wu_s, wd_s):
    i = pl.program_id(0)
    e = te_ref[i]
    n_valid = rows_ref[i]

    @pl.when((i < nt_ref[0]) & ((i == 0) | (e != te_ref[jnp.maximum(i - 1, 0)])))
    def _():
        wg_s[...] = wg_ref[0].astype(BF16)
        wu_s[...] = wu_ref[0].astype(BF16)
        wd_s[...] = wd_ref[0].astype(BF16)

    def expert_rows(r0, n):
        x = xs_ref[r0:r0 + n, 0:D_MODEL]
        g = xs_ref[r0:r0 + n, D_MODEL:XG_W].astype(F32)
        lane = lax.broadcasted_iota(jnp.int32, (n, LANES), 1)
        gcol = jnp.sum(jnp.where((lane == e) | (lane == e + N_EXPERTS), g, 0.0), axis=1, keepdims=True)
        a = _dot(x, wg_s[...])
        b = _dot(x, wu_s[...])
        hc = _silu(a) * b * gcol
        ys_ref[r0:r0 + n, :] = _dot(hc.astype(BF16), wd_s[...]).astype(BF16)

    @pl.when(n_valid == FFN_TM)
    def _():
        expert_rows(0, FFN_TM)

    @pl.when(n_valid < FFN_TM)
    def _():
        for r0 in range(0, FFN_TM, FFN_SB):
            @pl.when(r0 < n_valid)
            def _(r0=r0):
                expert_rows(r0, FFN_SB)

            @pl.when(r0 >= n_valid)
            def _(r0=r0):
                ys_ref[r0:r0 + FFN_SB, :] = jnp.zeros((FFN_SB, D_MODEL), BF16)


def _ffn(tile_expert, tile_rows, n_tiles, xs, wg, wu, wd):
    n_rows = xs.shape[0]
    row = lambda i, te, tr, nt: (jnp.minimum(i, nt[0] - 1), 0)
    wspec = lambda shape: pl.BlockSpec((1,) + shape, lambda i, te, tr, nt: (te[i], 0, 0))
    return pl.pallas_call(
        _ffn_kernel,
        grid_spec=pltpu.PrefetchScalarGridSpec(
            num_scalar_prefetch=3, grid=(n_rows // FFN_TM,),
            in_specs=[pl.BlockSpec((FFN_TM, XG_W), row), wspec((D_MODEL, D_EXPERT)),
                      wspec((D_MODEL, D_EXPERT)), wspec((D_EXPERT, D_MODEL))],
            out_specs=pl.BlockSpec((FFN_TM, D_MODEL), lambda i, te, tr, nt: (i, 0)),
            scratch_shapes=[pltpu.VMEM((D_MODEL, D_EXPERT), BF16), pltpu.VMEM((D_MODEL, D_EXPERT), BF16),
                            pltpu.VMEM((D_EXPERT, D_MODEL), BF16)]),
        out_shape=jax.ShapeDtypeStruct((n_rows, D_MODEL), BF16),
        compiler_params=_params(("arbitrary",)),
        name="moe_ffn",
    )(tile_expert, tile_rows, n_tiles, xs, wg, wu, wd)


def _combine_kernel(nblk_ref, ncopy_ref, src_ref, post_ref, xg_ref, x1_ref, g2_ref, wgs_ref, wus_ref,
                    wds_ref, ys_hbm, y_ref, ys_scr, wgs_s, wus_s, wds_s, sem):
    s = pl.program_id(0)
    n = ncopy_ref[s]
    blk_chunks = MXU_ROWS // CHUNK
    trip_chunks = BLOCKS_PER_TRIP * blk_chunks

    def fetch_trip(t, carry):
        k0 = t * trip_chunks
        _for_each(trip_chunks,
                  lambda u: _chunk_copy(ys_hbm, src_ref[0, 0, k0 + u], ys_scr, k0 + u, sem.at[t]).start())
        return carry

    lax.fori_loop(0, nblk_ref[s], fetch_trip, 0)

    @pl.when(s == 0)
    def _():
        wgs_s[...] = wgs_ref[...].astype(BF16)
        wus_s[...] = wus_ref[...].astype(BF16)
        wds_s[...] = wds_ref[...].astype(BF16)

    h = xg_ref[:, 0:D_MODEL]
    a = _dot(h, wgs_s[...])
    b = _dot(h, wus_s[...])
    shared = _dot((_silu(a) * b).astype(BF16), wds_s[...])

    post = post_ref[...]
    pos = [jnp.broadcast_to(post[:, k:k + 1], (SUB, MXU_ROWS)) for k in range(TOP_K)]
    lanes = lax.broadcasted_iota(jnp.int32, (SUB, MXU_ROWS), 1).astype(BF16)

    def gather_blocks(i, acc):
        _for_each(trip_chunks, lambda t: _chunk_copy(ys_hbm, 0, ys_scr, 0, sem.at[i]).wait())
        for j in range(BLOCKS_PER_TRIP):
            blk = i * BLOCKS_PER_TRIP + j
            onehot = _slot_onehot((blk * MXU_ROWS).astype(F32), pos, lanes)
            rows = ys_scr[pl.ds(pl.multiple_of(blk * blk_chunks, blk_chunks), blk_chunks)]
            acc = acc + _dot(onehot, rows.reshape(MXU_ROWS, D_MODEL))
        return acc

    routed = lax.fori_loop(0, nblk_ref[s], gather_blocks, jnp.zeros((SUB, D_MODEL), F32))
    y_ref[...] = x1_ref[...] + g2_ref[0] * (routed + shared)


def _combine(nblk, ncopy, src, post, xg, x1, g2, wgs, wus, wds, ys):
    n_tok = x1.shape[0]
    nsub = n_tok // SUB
    per_b = nsub // g2.shape[0]
    tab = pl.BlockSpec((1, 1, TABLE_W), lambda s, *_: (s, 0, 0), memory_space=pltpu.SMEM)
    const2 = lambda shape: pl.BlockSpec(shape, lambda s, *_: (0, 0))
    return pl.pallas_call(
        _combine_kernel,
        grid_spec=pltpu.PrefetchScalarGridSpec(
            num_scalar_prefetch=2, grid=(nsub,),
            in_specs=[tab, pl.BlockSpec((SUB, LANES), lambda s, *_: (s, 0)),
                      pl.BlockSpec((SUB, XG_W), lambda s, *_: (s, 0)),
                      pl.BlockSpec((SUB, D_MODEL), lambda s, *_: (s, 0)),
                      pl.BlockSpec((1, 1, D_MODEL), lambda s, *_: (s // per_b, 0, 0)),
                      const2((D_MODEL, D_EXPERT)), const2((D_MODEL, D_EXPERT)), const2((D_EXPERT, D_MODEL)),
                      pl.BlockSpec(memory_space=pl.ANY)],
            out_specs=pl.BlockSpec((SUB, D_MODEL), lambda s, *_: (s, 0)),
            scratch_shapes=[pltpu.VMEM((N_CHUNKS, CHUNK, D_MODEL), BF16), pltpu.VMEM((D_MODEL, D_EXPERT), BF16),
                            pltpu.VMEM((D_MODEL, D_EXPERT), BF16), pltpu.VMEM((D_EXPERT, D_MODEL), BF16),
                            pltpu.SemaphoreType.DMA((SUB_ROWS // (BLOCKS_PER_TRIP * MXU_ROWS),))]),
        out_shape=jax.ShapeDtypeStruct((n_tok, D_MODEL), F32),
        compiler_params=_params(("arbitrary",)),
        name="moe_combine",
    )(nblk, ncopy, src, post, xg, x1, g2, wgs, wus, wds, ys.reshape(-1, CHUNK, D_MODEL))


def _route_tables(cnt):
    nsub = cnt.shape[0]
    i32 = jnp.int32
    seg_end = jnp.cumsum(cnt, axis=1)
    seg_start = seg_end - cnt
    tot = jnp.sum(cnt, axis=0)
    tot_al = (tot + FFN_TM - 1) // FFN_TM * FFN_TM
    reg_end = jnp.cumsum(tot_al)
    reg_start = reg_end - tot_al
    base = reg_start[None, :] + jnp.cumsum(cnt, axis=0) - cnt
    n_real = seg_end[:, -1] // CHUNK
    k = jnp.arange(TABLE_W, dtype=i32)
    owner = jnp.sum((seg_end[:, None, :] // CHUNK) <= k[None, :, None], axis=2)
    owner = jnp.minimum(owner, N_EXPERTS - 1)
    delta = (base - seg_start) // CHUNK
    e_ids = jnp.arange(N_EXPERTS, dtype=i32)
    real_dst = jnp.sum(jnp.where(owner[:, :, None] == e_ids, delta[:, None, :], 0), axis=2) + k[None, :]
    fill_n = (tot_al - tot) // CHUNK
    fill_dst0 = (reg_start + tot) // CHUNK
    src = jnp.broadcast_to(k[None, :], (nsub, TABLE_W))
    dst = real_dst
    ncopy = n_real
    for r in range(-(-N_EXPERTS // nsub)):
        e_of = jnp.arange(nsub, dtype=i32) + r * nsub
        ok = e_of < N_EXPERTS
        e_cl = jnp.minimum(e_of, N_EXPERTS - 1)
        fn = jnp.where(ok, fill_n[e_cl], 0)
        j = k[None, :] - ncopy[:, None]
        is_fill = (j >= 0) & (j < fn[:, None])
        src = jnp.where(is_fill, ZERO_CHUNK, src)
        dst = jnp.where(is_fill, fill_dst0[e_cl][:, None] + j, dst)
        ncopy = ncopy + fn
    trip_rows = BLOCKS_PER_TRIP * MXU_ROWS
    disp_nblk = (n_real * CHUNK + trip_rows - 1) // trip_rows
    comb_n = disp_nblk * (trip_rows // CHUNK)
    comb_src = jnp.where(k[None, :] < n_real[:, None], real_dst, real_dst[:, 0:1])
    n_tiles = reg_end[-1] // FFN_TM
    t = jnp.arange((nsub * SUB_ROWS + N_EXPERTS * FFN_TM) // FFN_TM, dtype=i32)
    tile_expert = jnp.minimum(jnp.sum((reg_end[None, :] // FFN_TM) <= t[:, None], axis=1), N_EXPERTS - 1)
    rows_end = jnp.sum(jnp.where(tile_expert[:, None] == e_ids, (reg_start + tot)[None, :], 0), axis=1)
    tile_rows = jnp.where(t < n_tiles, jnp.clip(rows_end - t * FFN_TM, 0, FFN_TM), 0)
    as3 = lambda a: a.astype(i32).reshape(nsub, 1, TABLE_W)
    return (disp_nblk.astype(i32), n_real.astype(i32), ncopy.astype(i32), as3(src), as3(dst),
            comb_n.astype(i32), as3(comb_src), tile_expert.astype(i32), tile_rows.astype(i32),
            n_tiles.astype(i32).reshape(1))


def _moe_sparse(xg, post, posr, cnt, x1, g2, wg, wu, wd, wgs, wus, wds):
    B, S, _ = x1.shape
    n_tok = B * S
    nsub = n_tok // SUB
    (disp_nblk, n_real, disp_n, disp_src, disp_dst, comb_n, comb_src, tile_expert, tile_rows,
     n_tiles) = _route_tables(cnt[:, :, 0].astype(jnp.int32))
    n_rows = nsub * SUB_ROWS + N_EXPERTS * FFN_TM
    xg2 = xg.reshape(n_tok, XG_W)
    xs = _dispatch(disp_nblk, n_real, disp_n, n_tiles, disp_src, disp_dst, posr, xg2, n_rows)
    ys = _ffn(tile_expert, tile_rows, n_tiles, xs, wg, wu, wd)
    y = _combine(disp_nblk, comb_n, comb_src, post.reshape(n_tok, LANES), xg2, x1.reshape(n_tok, D_MODEL),
                 g2, wgs, wus, wds, ys)
    return y.reshape(B, S, D_MODEL)


def _t5_bucket(dist):
    max_exact = N_BUCKETS // 2
    df = jnp.maximum(dist, 1).astype(F32)
    large = max_exact + (jnp.log(df / max_exact) / math.log(MAX_DISTANCE / max_exact)
                         * (N_BUCKETS - max_exact)).astype(jnp.int32)
    large = jnp.minimum(large, N_BUCKETS - 1)
    return jnp.where(dist < max_exact, dist, large)


def _step_bias(rel_bias, g):
    dist = jnp.arange(N_KEYS, dtype=jnp.int32) * DILATIONS[g]
    cols = rel_bias[:, g * HEADS_PER_GROUP:(g + 1) * HEADS_PER_GROUP]
    return cols[_t5_bucket(dist)].T.astype(F32)


def _prompt_bias(bias_k):
    h = bias_k.shape[0]
    n = 3 * BLK
    row = jnp.concatenate([bias_k[:, ::-1], jnp.full((h, n - N_KEYS), NEG, F32)], axis=1)
    t = jnp.tile(row, (1, BLK))[:, :BLK * (n - 1)].reshape(h, BLK, n - 1)
    return t[:, :, :2 * BLK]


def _sample_bias(bias_k, wb, d):
    h = bias_k.shape[0]
    rev = bias_k[:, :0:-1]
    rows = []
    for t in range(4):
        if d == 1:
            rows.append(jnp.concatenate([jnp.full((h, t), NEG, F32), rev[:, :wb - t]], axis=1))
        else:
            r = jnp.arange(d)[None, None, :]
            rows.append(jnp.where(r == t, rev[:, :, None], NEG).reshape(h, wb))
    b1 = jnp.concatenate([jnp.stack(rows, axis=1), jnp.zeros((h, 4, wb), F32)], axis=1)
    rows = []
    for t in range(4):
        cols = [bias_k[:, (t - tn) // d] if (t >= tn and (t - tn) % d == 0) else jnp.full((h,), NEG, F32)
                for tn in range(4)]
        rows.append(jnp.concatenate([jnp.full((h, LANES - 4), NEG, F32), jnp.stack(cols, axis=1)], axis=1))
    b2 = jnp.concatenate([jnp.stack(rows, axis=1), jnp.zeros((h, 4, LANES), F32)], axis=1)
    return b1, b2


def _layer(x, mods, hist, caches, rel_bias, weights, *, sample):
    (norm1_g, norm2_g, w_in_bf, qg, kg, conv_w, bd, w_o_bf, w_r_t, b_r, w_gate_e, w_up_e, w_down_e,
     w_gate_s, w_up_s, w_down_s) = weights
    sh1, sc1, g1, sh2, sc2, g2 = mods
    B, S, _ = x.shape
    tile = 512
    dils = (1, 1, 1) if sample else DILATIONS
    res = _proj(x, sc1, sh1, norm1_g, w_in_bf, qg, kg, conv_w, bd, hist, tile=tile, dils=dils, sample=sample)
    states = []
    o_list, l_list = [], []
    if sample:
        qf, kt, vt, u, co = res
        nb = S // 4
        for g in range(N_GROUPS):
            d, wb = DILATIONS[g], WINDOWS[g]
            kc, vc = caches[g]
            bias_k = _step_bias(rel_bias, g)
            b1, b2 = _sample_bias(bias_k, wb, d)
            q = qf[0, :, g * GROUP_W:(g + 1) * GROUP_W].reshape(nb, 4, HEADS_PER_GROUP, HEAD_DIM)
            q = jnp.pad(q.transpose(0, 2, 1, 3), ((0, 0), (0, 0), (0, 4), (0, 0))).astype(BF16)
            new_t = lambda a: a[g * GROUP_W:(g + 1) * GROUP_W].reshape(HEADS_PER_GROUP, HEAD_DIM, S)
            ko, vo, o, lse = _attn_sample(q, kc, vc, new_t(kt), new_t(vt), b1, b2,
                                          bb={128: 8, 512: 8, 2048: 2}[wb])
            states += [ko.transpose(0, 3, 1, 2)[None], vo.transpose(0, 3, 1, 2)[None]]
            o = o[:, :, :4].transpose(0, 2, 1, 3).reshape(1, 1, S, GROUP_W)
            lse = jnp.broadcast_to(lse[:, :, :4, :1], (nb, HEADS_PER_GROUP, 4, HEAD_DIM))
            o_list.append(o)
            l_list.append(lse.transpose(0, 2, 1, 3).reshape(1, 1, S, GROUP_W))
        states.append(u.reshape(nb, 4, CONV_CH)[:, 2:][None])
    else:
        qkv, tails, (u, co) = res[:9], res[9:15], res[15:]
        for g in range(N_GROUPS):
            d, w = DILATIONS[g], WINDOWS[g]
            q, k, v = (a.reshape(B * d, S // d, GROUP_W) for a in qkv[3 * g:3 * g + 3])
            o, lse = _attn_prompt(q, k, v, _prompt_bias(_step_bias(rel_bias, g)))
            o_list.append(o.reshape(B, d, S // d, GROUP_W))
            l_list.append(lse.reshape(B, d, S // d, GROUP_W))
            keep = lambda a: a.reshape(B, HEADS_PER_GROUP, HEAD_DIM, w).transpose(0, 3, 1, 2)[None]
            states += [keep(tails[2 * g]), keep(tails[2 * g + 1])]
        states.append(u[:, 6:][None])
    x1, xg, gates, posr, post, cnt = _oproj(x, o_list, l_list, co, g1, sc2, sh2, norm2_g, w_o_bf, w_r_t, b_r,
                                            tile=tile, dils=dils)
    if sample:
        y = _moe(xg, gates, x1, g2, w_gate_e, w_up_e, w_down_e, w_gate_s, w_up_s, w_down_s, tile=tile)
    else:
        y = _moe_sparse(xg, post, posr, cnt, x1, g2, w_gate_e, w_up_e, w_down_e, w_gate_s, w_up_s, w_down_s)
    return y, states


def kernel(x_prompt, x_sample, c_prompt, c_sample, cache_k_w128, cache_v_w128, cache_k_w512, cache_v_w512, cache_k_w2048, cache_v_w2048, state_conv, rel_bias, norm1_g, norm2_g, w_ada, b_ada, w_in, q_norm_g, k_norm_g, conv_w, w_o, w_router, b_router, w_gate_e, w_up_e, w_down_e, w_gate_s, w_up_s, w_down_s):
    B = x_prompt.shape[0]
    DB, T = x_sample.shape[:2]
    n_c = B + DB
    c_all = jnp.pad(jnp.concatenate([c_prompt, c_sample], axis=0), ((0, (-n_c) % 8), (0, 0)))
    mod = _ada(c_all, w_ada[0], b_ada)
    chunks = [mod[:, i * D_MODEL:(i + 1) * D_MODEL] for i in range(6)]
    mods_p = [c[:B].reshape(B, 1, D_MODEL) for c in chunks]
    mods_s = [jnp.repeat(c[B:n_c], T, axis=0).reshape(1, DB * T, D_MODEL) for c in chunks]

    eye = jnp.arange(GROUP_W) // HEAD_DIM
    bd = jnp.where(eye[:, None] == eye[None, :], 1.0 / HEAD_DIM, 0.0).astype(BF16)
    weights = (norm1_g, norm2_g, w_in[0].astype(BF16),
               jnp.tile(q_norm_g, (1, HEADS_PER_GROUP)), jnp.tile(k_norm_g, (1, HEADS_PER_GROUP)),
               conv_w[0], bd, w_o[0].astype(BF16), w_router[0].T, b_router.reshape(N_EXPERTS, 1),
               w_gate_e[0], w_up_e[0], w_down_e[0], w_gate_s[0], w_up_s[0], w_down_s[0])

    yp, st_p = _layer(x_prompt, mods_p, None, None, rel_bias, weights, sample=False)

    s0, s1 = state_conv[0, :, 0], state_conv[0, :, 1]
    zero = jnp.zeros_like(s0)
    hist_a = jnp.stack([s0, s1, zero, zero], axis=1).reshape(DB * T, CONV_CH)
    hist_b = jnp.stack([s1, zero, zero, zero], axis=1).reshape(DB * T, CONV_CH)
    caches = [(ck[0].transpose(0, 2, 3, 1), cv[0].transpose(0, 2, 3, 1))
              for ck, cv in ((cache_k_w128, cache_v_w128), (cache_k_w512, cache_v_w512),
                             (cache_k_w2048, cache_v_w2048))]
    ys, st_s = _layer(x_sample.reshape(1, DB * T, D_MODEL), mods_s, (hist_a, hist_b), caches, rel_bias,
                      weights, sample=True)
    return (yp, ys.reshape(DB, T, D_MODEL), *st_p, *st_s)
```

```python
import functools
import math

import jax
import jax.numpy as jnp
from jax import lax
from jax.experimental import pallas as pl
from jax.experimental.pallas import tpu as pltpu

D_MODEL = 1024
HEAD_DIM = 64
HEADS_PER_GROUP = 4
GROUP_W = HEADS_PER_GROUP * HEAD_DIM
WINDOWS = (128, 512, 2048)
DILATIONS = (1, 4, 16)
N_GROUPS = 3
ATTN_W = N_GROUPS * GROUP_W
CONV_CH = 256
N_PROJ = 3 * ATTN_W + 3 * CONV_CH
N_STEPS = 128
N_KEYS = N_STEPS + 1
BLK = 128
N_BUCKETS = 32
MAX_DISTANCE = WINDOWS[-1]
N_EXPERTS = 64
TOP_K = 8
N_ROUTE_GROUPS = 8
GROUP_SIZE = N_EXPERTS // N_ROUTE_GROUPS
TOPK_GROUPS = 4
D_EXPERT = 256
ROUTED_SCALE = 2.5
RMS_EPS = 1e-6
NEG = -1e30
LANES = 128
SUB = 256
CHUNK = 16
XG_W = D_MODEL + LANES
VMEM_LIMIT = 56 * 1024 * 1024

F32 = jnp.float32
BF16 = jnp.bfloat16


def _dot(a, b):
    return jnp.dot(a, b, preferred_element_type=F32)


def _dot_nt(a, b):
    return lax.dot_general(a, b, (((1,), (1,)), ((), ())), preferred_element_type=F32)


def _silu(a):
    return a / (1.0 + jnp.exp(-a))


def _params(sem):
    return pltpu.CompilerParams(dimension_semantics=sem, vmem_limit_bytes=VMEM_LIMIT)


def _ada_kernel(c_ref, w_ref, b_ref, o_ref):
    s = _silu(c_ref[...])
    o_ref[...] = _dot(s.astype(BF16), w_ref[...].astype(BF16)) + b_ref[...]


def _ada(c, w_ada, b_ada):
    n = c.shape[0]
    nc = w_ada.shape[1] // D_MODEL
    return pl.pallas_call(
        _ada_kernel,
        grid=(nc,),
        in_specs=[pl.BlockSpec((n, D_MODEL), lambda j: (0, 0)),
                  pl.BlockSpec((D_MODEL, D_MODEL), lambda j: (0, j)),
                  pl.BlockSpec((1, D_MODEL), lambda j: (0, j))],
        out_specs=pl.BlockSpec((n, D_MODEL), lambda j: (0, j)),
        out_shape=jax.ShapeDtypeStruct((n, w_ada.shape[1]), F32),
        compiler_params=_params(("arbitrary",)),
        name="ada",
    )(c, w_ada, b_ada)


def _proj_kernel(*refs, tile, dils, sample, nt):
    if sample:
        (x_ref, sc_ref, sh_ref, ng_ref, w_ref, qg_ref, kg_ref, cw_ref, bd_ref, ha_ref, hb_ref,
         qf_ref, kt_ref, vt_ref, u_ref, co_ref, u_scr) = refs
    else:
        (x_ref, sc_ref, sh_ref, ng_ref, w_ref, qg_ref, kg_ref, cw_ref, bd_ref,
         q0, k0, v0, q1, k1, v1, q2, k2, v2, kt0, vt0, kt1, vt1, kt2, vt2, u_ref, co_ref, slab, u_scr) = refs
        qkv_out = ((q0, k0, v0), (q1, k1, v1), (q2, k2, v2))
        tails = ((kt0, vt0), (kt1, vt1), (kt2, vt2))
        j = pl.program_id(1)

    x = x_ref[0]
    ms = jnp.mean(x * x, axis=-1, keepdims=True)
    h = x * lax.rsqrt(ms + RMS_EPS) * ng_ref[...]
    h = h * (1.0 + sc_ref[0]) + sh_ref[0]
    proj = _dot(h.astype(BF16), w_ref[...])

    bd = bd_ref[...]

    def headnorm(z, g):
        zz = z * z
        hi = zz.astype(BF16)
        lo = (zz - hi.astype(F32)).astype(BF16)
        msq = _dot(hi, bd) + _dot(lo, bd)
        return z * lax.rsqrt(msq + RMS_EPS) * g

    slab_i = 0
    for g in range(N_GROUPS):
        c0 = g * GROUP_W
        qn = headnorm(proj[:, c0:c0 + GROUP_W], qg_ref[...]) * (HEAD_DIM ** -0.5)
        kn = headnorm(proj[:, ATTN_W + c0:ATTN_W + c0 + GROUP_W], kg_ref[...])
        vv = proj[:, 2 * ATTN_W + c0:2 * ATTN_W + c0 + GROUP_W]
        if sample:
            qf_ref[0, :, c0:c0 + GROUP_W] = qn
            kt_ref[c0:c0 + GROUP_W, :] = kn.T
            vt_ref[c0:c0 + GROUP_W, :] = vv.T
            continue
        keep = min(WINDOWS[g], tile)

        @pl.when(j >= nt - max(WINDOWS[g] // tile, 1))
        def _(kn=kn, vv=vv, g=g, keep=keep):
            tails[g][0][0] = kn[tile - keep:, :].T
            tails[g][1][0] = vv[tile - keep:, :].T

        d = dils[g]
        for val, out in zip((qn, kn, vv), qkv_out[g]):
            if d == 1:
                out[0, 0] = val.astype(BF16)
                continue
            n = tile // d
            for half in range(GROUP_W // LANES):
                slab[slab_i] = val[:, half * LANES:(half + 1) * LANES]
                for r in range(d):
                    out[0, r, :, half * LANES:(half + 1) * LANES] = (
                        slab[slab_i, pl.ds(r, n, stride=d), :].astype(BF16))
                slab_i += 1

    base = 3 * ATTN_W
    u = proj[:, base + 2 * CONV_CH:base + 3 * CONV_CH] * proj[:, base:base + CONV_CH]
    gate_b = proj[:, base + CONV_CH:base + 2 * CONV_CH]
    if sample:
        u_scr[0:8, :] = jnp.zeros((8, CONV_CH), F32)
    else:
        @pl.when(j == 0)
        def _():
            u_scr[0:8, :] = jnp.zeros((8, CONV_CH), F32)

        @pl.when(j > 0)
        def _():
            u_scr[0:8, :] = u_scr[tile:tile + 8, :]

    u_scr[8:tile + 8, :] = u
    um1 = u_scr[7:tile + 7, :]
    um2 = u_scr[6:tile + 6, :]
    if sample:
        t = lax.broadcasted_iota(jnp.int32, (tile, CONV_CH), 0) % 4
        um1 = jnp.where(t >= 1, um1, 0.0) + hb_ref[...]
        um2 = jnp.where(t >= 2, um2, 0.0) + ha_ref[...]
    cw = cw_ref[...]
    conv = cw[0:1] * um2 + cw[1:2] * um1 + cw[2:3] * u
    u_ref[0] = u if sample else u[tile - 8:, :]
    co_ref[0] = (gate_b * conv).astype(BF16)


def _proj(x, sc, sh, ng, w_bf, qg, kg, cw, bd, hist, *, tile, dils, sample):
    B, S, _ = x.shape
    nt = S // tile
    tm = sc.shape[1]
    mod_spec = pl.BlockSpec((1, tm, D_MODEL), (lambda b, j: (b, j, 0)) if tm > 1 else (lambda b, j: (b, 0, 0)))
    const2 = lambda shape: pl.BlockSpec(shape, lambda b, j: (0, 0))
    in_specs = [pl.BlockSpec((1, tile, D_MODEL), lambda b, j: (b, j, 0)), mod_spec, mod_spec,
                const2((1, D_MODEL)), const2((D_MODEL, N_PROJ)), const2((1, GROUP_W)),
                const2((1, GROUP_W)), const2((3, CONV_CH)), const2((GROUP_W, GROUP_W))]
    args = [x, sc, sh, ng, w_bf, qg, kg, cw, bd]
    nat = lambda w: pl.BlockSpec((1, tile, w), lambda b, j: (b, j, 0))
    out_specs, out_shape = [], []
    scratch = []
    if sample:
        assert B == 1 and nt == 1
        in_specs += [pl.BlockSpec((tile, CONV_CH), lambda b, j: (j, 0))] * 2
        args += list(hist)
        out_specs += [nat(ATTN_W)] + [pl.BlockSpec((ATTN_W, tile), lambda b, j: (0, j))] * 2 + [nat(CONV_CH)]
        out_shape += [jax.ShapeDtypeStruct((B, S, ATTN_W), F32)]
        out_shape += [jax.ShapeDtypeStruct((ATTN_W, S), F32)] * 2
        out_shape += [jax.ShapeDtypeStruct((B, S, CONV_CH), F32)]
    else:
        for d in dils:
            for _ in range(3):
                out_specs.append(pl.BlockSpec((1, d, tile // d, GROUP_W), lambda b, j: (b, 0, j, 0)))
                out_shape.append(jax.ShapeDtypeStruct((B, d, S // d, GROUP_W), BF16))
        for w in WINDOWS:
            keep, first = min(w, tile), nt - max(w // tile, 1)
            for _ in range(2):
                out_specs.append(pl.BlockSpec((1, GROUP_W, keep),
                                              lambda b, j, first=first: (b, 0, jnp.maximum(j - first, 0))))
                out_shape.append(jax.ShapeDtypeStruct((B, GROUP_W, w), F32))
        out_specs.append(pl.BlockSpec((1, 8, CONV_CH), lambda b, j: (b, 0, 0)))
        out_shape.append(jax.ShapeDtypeStruct((B, 8, CONV_CH), F32))
        n_slabs = sum(3 * (GROUP_W // LANES) for d in dils if d > 1)
        scratch.append(pltpu.VMEM((n_slabs, tile, LANES), F32))
    out_specs.append(nat(CONV_CH))
    out_shape.append(jax.ShapeDtypeStruct((B, S, CONV_CH), BF16))
    scratch.append(pltpu.VMEM((tile + 8, CONV_CH), F32))
    return pl.pallas_call(
        functools.partial(_proj_kernel, tile=tile, dils=dils, sample=sample, nt=nt),
        grid=(B, nt),
        in_specs=in_specs,
        out_specs=out_specs,
        out_shape=out_shape,
        scratch_shapes=scratch,
        compiler_params=_params(("arbitrary", "arbitrary")),
        name="proj_sample" if sample else "proj_prompt",
    )(*args)


def _attn_kernel(q_ref, kp_ref, kc_ref, vp_ref, vc_ref, b_ref, o_ref, l_ref, s_scr, m_scr, o_scr, z_scr,
                 *, nq):
    first = pl.program_id(1) == 0
    low = lax.broadcasted_iota(jnp.int32, (BLK, LANES), 1) < HEAD_DIM
    ones = jnp.ones((BLK, LANES), BF16)
    chains = [(j, pair, sub) for j in range(nq) for pair in range(GROUP_W // LANES) for sub in range(2)]

    def operands(j, pair):
        rows = slice(j * BLK, (j + 1) * BLK)
        sl = slice(pair * LANES, (pair + 1) * LANES)
        if j == 0:
            return rows, sl, kp_ref[0, :, sl], vp_ref[0, :, sl]
        prows = slice((j - 1) * BLK, j * BLK)
        return rows, sl, kc_ref[0, prows, sl], vc_ref[0, prows, sl]

    for c, (j, pair, sub) in enumerate(chains):
        rows, sl, kp, _ = operands(j, pair)
        q = q_ref[0, rows, sl]
        qm = jnp.where(low if sub == 0 else ~low, q, jnp.zeros_like(q))
        hh = 2 * pair + sub
        sp = _dot_nt(qm, kp) + b_ref[hh, :, 0:BLK]
        if j == 0:
            sp = jnp.where(first, NEG, sp)
        sc = _dot_nt(qm, kc_ref[0, rows, sl]) + b_ref[hh, :, BLK:2 * BLK]
        s_scr[c, :, 0:BLK] = sp
        s_scr[c, :, BLK:2 * BLK] = sc
        m_scr[c] = jnp.broadcast_to(jnp.max(jnp.maximum(sp, sc), axis=-1, keepdims=True), (BLK, LANES))

    for c, (j, pair, sub) in enumerate(chains):
        rows, sl, _, vp = operands(j, pair)
        m = m_scr[c]
        pp = jnp.exp(s_scr[c, :, 0:BLK] - m).astype(BF16)
        pc = jnp.exp(s_scr[c, :, BLK:2 * BLK] - m).astype(BF16)
        o_scr[c] = _dot(pp, vp) + _dot(pc, vc_ref[0, rows, sl])
        z_scr[c] = _dot(pp, ones) + _dot(pc, ones)

    for c in range(0, len(chains), 2):
        j, pair, _ = chains[c]
        rows = slice(j * BLK, (j + 1) * BLK)
        sl = slice(pair * LANES, (pair + 1) * LANES)
        o_sub = [o_scr[c + sub] / z_scr[c + sub] for sub in range(2)]
        l_sub = [m_scr[c + sub] + jnp.log(z_scr[c + sub]) for sub in range(2)]
        o_ref[0, rows, sl] = jnp.where(low, o_sub[0], o_sub[1])
        l_ref[0, rows, sl] = jnp.where(low, l_sub[0], l_sub[1])


def _attn_prompt(q, k, v, bias):
    Z, L, _ = q.shape
    qt = min(L, 4 * BLK)
    nq = qt // BLK
    cur = pl.BlockSpec((1, qt, GROUP_W), lambda z, i: (z, i, 0))
    prev = pl.BlockSpec((1, BLK, GROUP_W), lambda z, i: (z, jnp.maximum(i * nq - 1, 0), 0))
    return pl.pallas_call(
        functools.partial(_attn_kernel, nq=nq),
        grid=(Z, L // qt),
        in_specs=[cur, prev, cur, prev, cur,
                  pl.BlockSpec((HEADS_PER_GROUP, BLK, 2 * BLK), lambda z, i: (0, 0, 0))],
        out_specs=[cur, cur],
        out_shape=[jax.ShapeDtypeStruct((Z, L, GROUP_W), F32)] * 2,
        scratch_shapes=[pltpu.VMEM((nq * HEADS_PER_GROUP, BLK, 2 * BLK), F32)]
        + [pltpu.VMEM((nq * HEADS_PER_GROUP, BLK, LANES), F32)] * 3,
        compiler_params=_params(("arbitrary", "arbitrary")),
        name="attn_prompt",
    )(q, k, k, v, v, bias)


def _attn_sample_batch(gb, b, q_ref, kc_ref, vc_ref, nk_ref, nv_ref, b1_ref, b2_ref, ko_ref, vo_ref, o_ref,
                       l_ref):
    wb = kc_ref.shape[-1]
    lane = lax.broadcasted_iota(jnp.int32, (HEAD_DIM, LANES), 1)
    new_cols = lane >= LANES - 4
    per_tile = LANES // 4
    tile_at = pl.ds(pl.multiple_of((gb // per_tile) * LANES, LANES), LANES)
    shift = (LANES - 4) - 4 * (gb % per_tile)
    for hh in range(HEADS_PER_GROUP):
        q = q_ref[b, hh]
        kt = kc_ref[b, hh]
        vt = vc_ref[b, hh]
        nkt = pltpu.roll(nk_ref[hh, :, tile_at], shift, axis=1)
        nvt = pltpu.roll(nv_ref[hh, :, tile_at], shift, axis=1)
        s1 = _dot(q, kt.astype(BF16)) + b1_ref[hh]
        s2 = _dot(q, nkt.astype(BF16)) + b2_ref[hh]
        m = jnp.maximum(jnp.max(s1, axis=-1, keepdims=True), jnp.max(s2, axis=-1, keepdims=True))
        p1 = jnp.exp(s1 - m)
        p2 = jnp.exp(s2 - m)
        z = jnp.sum(p1, axis=-1, keepdims=True) + jnp.sum(p2, axis=-1, keepdims=True)
        o = _dot_nt(p1.astype(BF16), vt.astype(BF16)) + _dot_nt(p2.astype(BF16), nvt.astype(BF16))
        o_ref[b, hh] = o / z
        l_ref[b, hh] = jnp.broadcast_to(m + jnp.log(z), (8, LANES))
        for src, new, dst in ((kt, nkt, ko_ref), (vt, nvt, vo_ref)):
            rolled = pltpu.roll(src, wb - 4, axis=1)
            if wb > LANES:
                dst[b, hh, :, 0:wb - LANES] = rolled[:, 0:wb - LANES]
            dst[b, hh, :, wb - LANES:wb] = jnp.where(new_cols, new, rolled[:, wb - LANES:wb])


def _attn_sample_kernel(*refs, bb):
    for b in range(bb):
        _attn_sample_batch(pl.program_id(0) * bb + b, b, *refs)


def _attn_sample(q, kc, vc, nk, nv, b1, b2, *, bb):
    nb, _, _, wb = kc.shape
    blk = lambda *tail: pl.BlockSpec((bb, HEADS_PER_GROUP) + tail, lambda i: (i, 0, 0, 0))
    const = lambda shape: pl.BlockSpec(shape, lambda i: (0, 0, 0))
    return pl.pallas_call(
        functools.partial(_attn_sample_kernel, bb=bb),
        grid=(nb // bb,),
        in_specs=[blk(8, HEAD_DIM), blk(HEAD_DIM, wb), blk(HEAD_DIM, wb), const(nk.shape), const(nv.shape),
                  const((HEADS_PER_GROUP, 8, wb)), const((HEADS_PER_GROUP, 8, LANES))],
        out_specs=[blk(HEAD_DIM, wb), blk(HEAD_DIM, wb), blk(8, HEAD_DIM), blk(8, LANES)],
        out_shape=[jax.ShapeDtypeStruct(kc.shape, F32), jax.ShapeDtypeStruct(kc.shape, F32),
                   jax.ShapeDtypeStruct((nb, HEADS_PER_GROUP, 8, HEAD_DIM), F32),
                   jax.ShapeDtypeStruct((nb, HEADS_PER_GROUP, 8, LANES), F32)],
        compiler_params=_params(("arbitrary",)),
        name="attn_sample",
    )(q, kc, vc, nk, nv, b1, b2)


def _first_max(v, ids, sentinel):
    m = jnp.max(v, axis=0, keepdims=True)
    idx = jnp.min(jnp.where(v == m, ids, sentinel), axis=0, keepdims=True)
    return m, ids == idx


def _oproj_kernel(x_ref, o0, o1, o2, l0, l1, l2, co_ref, g1_ref, sc_ref, sh_ref, ng_ref, wo_ref,
                  wr_ref, br_ref, x1_ref, xg_ref, gt_ref, posr_ref, post_ref, cnt_ref, slab, *, tile, dils):
    o_refs, l_refs = (o0, o1, o2), (l0, l1, l2)
    outs, lses = [], []
    slab_i = 0
    for g, d in enumerate(dils):
        if d == 1:
            outs.append(o_refs[g][0, 0])
            lses.append(l_refs[g][0, 0])
            continue
        n = tile // d
        for ref, dest in ((o_refs[g], outs), (l_refs[g], lses)):
            halves = []
            for half in range(GROUP_W // LANES):
                for r in range(d):
                    slab[slab_i, pl.ds(r, n, stride=d), :] = ref[0, r, :, half * LANES:(half + 1) * LANES]
                halves.append(slab[slab_i])
                slab_i += 1
            dest.append(jnp.concatenate(halves, axis=1))

    m = jnp.maximum(jnp.maximum(lses[0], lses[1]), lses[2])
    es = [jnp.exp(l - m) for l in lses]
    den = es[0] + es[1] + es[2]
    cat = [(outs[g] * (es[g] / den)).astype(BF16) for g in range(N_GROUPS)]
    cat.append(co_ref[0].astype(BF16))
    cat = jnp.concatenate(cat, axis=1)
    x1 = x_ref[0] + g1_ref[0] * _dot(cat, wo_ref[...])
    x1_ref[0] = x1

    ms = jnp.mean(x1 * x1, axis=-1, keepdims=True)
    h2 = x1 * lax.rsqrt(ms + RMS_EPS) * ng_ref[...]
    h2 = h2 * (1.0 + sc_ref[0]) + sh_ref[0]
    hh = h2.astype(BF16)
    xg_ref[0, :, 0:D_MODEL] = hh

    hl = (h2 - hh.astype(F32)).astype(BF16)
    wr = wr_ref[...]
    wh = wr.astype(BF16)
    wl = (wr - wh.astype(F32)).astype(BF16)
    logits = _dot_nt(wh, hh) + _dot_nt(wh, hl) + _dot_nt(wl, hh)
    scores = 1.0 / (1.0 + jnp.exp(-logits))
    sel = scores + br_ref[...]

    ids = lax.broadcasted_iota(jnp.int32, (GROUP_SIZE, tile), 0)
    ninf = -jnp.inf
    sel_g = [sel[g * GROUP_SIZE:(g + 1) * GROUP_SIZE] for g in range(N_ROUTE_GROUPS)]
    gscore = jnp.zeros((N_ROUTE_GROUPS, tile), F32)
    for g in range(N_ROUTE_GROUPS):
        m1, oh = _first_max(sel_g[g], ids, GROUP_SIZE)
        m2 = jnp.max(jnp.where(oh, ninf, sel_g[g]), axis=0, keepdims=True)
        gscore = jnp.where(ids == g, m1 + m2, gscore)
    gsel = jnp.zeros((N_ROUTE_GROUPS, tile), F32)
    for _ in range(TOPK_GROUPS):
        _, oh = _first_max(gscore, ids, N_ROUTE_GROUPS)
        gsel = jnp.where(oh, 1.0, gsel)
        gscore = jnp.where(oh, ninf, gscore)
    cand = [jnp.where(gsel[g:g + 1] > 0.0, sel_g[g], ninf) for g in range(N_ROUTE_GROUPS)]
    chosen = [jnp.zeros((GROUP_SIZE, tile), F32) for _ in range(N_ROUTE_GROUPS)]
    picks = []
    for _ in range(TOP_K):
        mx = cand[0]
        for g in range(1, N_ROUTE_GROUPS):
            mx = jnp.maximum(mx, cand[g])
        mx = jnp.max(mx, axis=0, keepdims=True)
        idx = jnp.where(cand[0] == mx, ids, N_EXPERTS)
        for g in range(1, N_ROUTE_GROUPS):
            idx = jnp.minimum(idx, jnp.where(cand[g] == mx, ids + g * GROUP_SIZE, N_EXPERTS))
        idx = jnp.min(idx, axis=0, keepdims=True)
        ohs = [(ids + g * GROUP_SIZE) == idx for g in range(N_ROUTE_GROUPS)]
        picks.append(ohs)
        for g in range(N_ROUTE_GROUPS):
            chosen[g] = jnp.where(ohs[g], 1.0, chosen[g])
            cand[g] = jnp.where(ohs[g], ninf, cand[g])
    wts = [jnp.where(chosen[g] > 0.0, scores[g * GROUP_SIZE:(g + 1) * GROUP_SIZE], 0.0)
           for g in range(N_ROUTE_GROUPS)]
    wsum = jnp.sum(wts[0], axis=0, keepdims=True)
    for g in range(1, N_ROUTE_GROUPS):
        wsum = wsum + jnp.sum(wts[g], axis=0, keepdims=True)
    gates_t = jnp.concatenate([w / wsum * ROUTED_SCALE for w in wts], axis=0)
    gt_ref[0] = jnp.concatenate([gates_t, jnp.zeros((LANES - N_EXPERTS, tile), F32)], axis=0).T
    g_hi = gates_t.astype(BF16).astype(F32)
    xg_ref[0, :, D_MODEL:D_MODEL + LANES] = jnp.concatenate([g_hi, gates_t - g_hi], axis=0).T.astype(BF16)

    er = lax.broadcasted_iota(jnp.int32, (N_EXPERTS, N_EXPERTS), 0)
    ec = lax.broadcasted_iota(jnp.int32, (N_EXPERTS, N_EXPERTS), 1)
    lower = jnp.where(ec < er, 1.0, 0.0).astype(BF16)
    tr = lax.broadcasted_iota(jnp.int32, (SUB, SUB), 0)
    tc = lax.broadcasted_iota(jnp.int32, (SUB, SUB), 1)
    upper = jnp.where(tr < tc, 1.0, 0.0).astype(BF16)
    for u in range(tile // SUB):
        cols = slice(u * SUB, (u + 1) * SUB)
        ch = jnp.concatenate([c[:, cols] for c in chosen], axis=0)
        cnt = jnp.sum(ch, axis=1, keepdims=True)
        cpad = jnp.broadcast_to(jnp.ceil(cnt / CHUNK) * CHUNK, (N_EXPERTS, LANES))
        cnt_ref[u] = cpad
        seg0 = _dot(lower, cpad.astype(BF16))[:, 0:1]
        slot = seg0 + _dot(ch.astype(BF16), upper)
        rows = []
        for k in range(TOP_K):
            acc = jnp.where(picks[k][0][:, cols], slot[0:GROUP_SIZE], 0.0)
            for g in range(1, N_ROUTE_GROUPS):
                acc = acc + jnp.where(picks[k][g][:, cols], slot[g * GROUP_SIZE:(g + 1) * GROUP_SIZE], 0.0)
            rows.append(jnp.sum(acc, axis=0, keepdims=True))
        posr = jnp.concatenate(rows, axis=0)
        posr_ref[u] = posr
        post_ref[0, cols, :] = jnp.concatenate([posr, jnp.zeros((LANES - TOP_K, SUB), F32)], axis=0).T


def _oproj(x, o_list, l_list, co, g1, sc, sh, ng, wo_bf, wr_t, br, *, tile, dils):
    B, S, _ = x.shape
    tm = g1.shape[1]
    mod_spec = pl.BlockSpec((1, tm, D_MODEL), (lambda b, j: (b, j, 0)) if tm > 1 else (lambda b, j: (b, 0, 0)))
    const2 = lambda shape: pl.BlockSpec(shape, lambda b, j: (0, 0))
    nat = lambda w: pl.BlockSpec((1, tile, w), lambda b, j: (b, j, 0))
    dspec = [pl.BlockSpec((1, d, tile // d, GROUP_W), lambda b, j: (b, 0, j, 0)) for d in dils]
    n_slabs = sum(2 * (GROUP_W // LANES) for d in dils if d > 1)
    nsub_t = tile // SUB
    return pl.pallas_call(
        functools.partial(_oproj_kernel, tile=tile, dils=dils),
        grid=(B, S // tile),
        in_specs=[nat(D_MODEL)] + dspec + dspec + [nat(CONV_CH), mod_spec, mod_spec, mod_spec,
                  const2((1, D_MODEL)), const2((D_MODEL, D_MODEL)), const2((N_EXPERTS, D_MODEL)),
                  const2((N_EXPERTS, 1))],
        out_specs=[nat(D_MODEL), nat(XG_W), nat(LANES),
                   pl.BlockSpec((nsub_t, TOP_K, SUB), lambda b, j: (b * (S // tile) + j, 0, 0)),
                   nat(LANES),
                   pl.BlockSpec((nsub_t, N_EXPERTS, LANES), lambda b, j: (b * (S // tile) + j, 0, 0))],
        out_shape=[jax.ShapeDtypeStruct((B, S, D_MODEL), F32), jax.ShapeDtypeStruct((B, S, XG_W), BF16),
                   jax.ShapeDtypeStruct((B, S, LANES), F32),
                   jax.ShapeDtypeStruct((B * S // SUB, TOP_K, SUB), F32),
                   jax.ShapeDtypeStruct((B, S, LANES), F32),
                   jax.ShapeDtypeStruct((B * S // SUB, N_EXPERTS, LANES), F32)],
        scratch_shapes=[pltpu.VMEM((max(n_slabs, 1), tile, LANES), F32)],
        compiler_params=_params(("arbitrary", "arbitrary")),
        name="oproj",
    )(x, *o_list, *l_list, co, g1, sc, sh, ng, wo_bf, wr_t, br)


def _moe_kernel(h_ref, gt_ref, x1_ref, g2_ref, wg_ref, wu_ref, wd_ref, wgs_ref, wus_ref, wds_ref,
                y_ref, acc, *, tile):
    e = pl.program_id(2)
    h = h_ref[0]

    @pl.when(e == 0)
    def _():
        a = _dot(h, wgs_ref[...].astype(BF16))
        b = _dot(h, wus_ref[...].astype(BF16))
        acc[...] = _dot((_silu(a) * b).astype(BF16), wds_ref[...].astype(BF16))

    a = _dot(h, wg_ref[0].astype(BF16))
    b = _dot(h, wu_ref[0].astype(BF16))
    lane = lax.broadcasted_iota(jnp.int32, (tile, LANES), 1)
    gcol = jnp.sum(jnp.where(lane == e, gt_ref[0], 0.0), axis=1, keepdims=True)
    hc = _silu(a) * b * gcol
    acc[...] += _dot(hc.astype(BF16), wd_ref[0].astype(BF16))

    @pl.when(e == N_EXPERTS - 1)
    def _():
        y_ref[0] = x1_ref[0] + g2_ref[0] * acc[...]


def _moe(h2, gates, x1, g2, wg, wu, wd, wgs, wus, wds, *, tile):
    B, S, _ = h2.shape
    tm = g2.shape[1]
    mod_spec = pl.BlockSpec((1, tm, D_MODEL),
                            (lambda b, j, e: (b, j, 0)) if tm > 1 else (lambda b, j, e: (b, 0, 0)))
    nat = lambda w: pl.BlockSpec((1, tile, w), lambda b, j, e: (b, j, 0))
    const2 = lambda shape: pl.BlockSpec(shape, lambda b, j, e: (0, 0))
    return pl.pallas_call(
        functools.partial(_moe_kernel, tile=tile),
        grid=(B, S // tile, N_EXPERTS),
        in_specs=[nat(D_MODEL), nat(LANES), nat(D_MODEL), mod_spec,
                  pl.BlockSpec((1, D_MODEL, D_EXPERT), lambda b, j, e: (e, 0, 0)),
                  pl.BlockSpec((1, D_MODEL, D_EXPERT), lambda b, j, e: (e, 0, 0)),
                  pl.BlockSpec((1, D_EXPERT, D_MODEL), lambda b, j, e: (e, 0, 0)),
                  const2((D_MODEL, D_EXPERT)), const2((D_MODEL, D_EXPERT)), const2((D_EXPERT, D_MODEL))],
        out_specs=nat(D_MODEL),
        out_shape=jax.ShapeDtypeStruct((B, S, D_MODEL), F32),
        scratch_shapes=[pltpu.VMEM((tile, D_MODEL), F32)],
        compiler_params=_params(("arbitrary", "arbitrary", "arbitrary")),
        name="moe",
    )(h2, gates, x1, g2, wg, wu, wd, wgs, wus, wds)


FFN_TM = 1024
SUB_ROWS = SUB * TOP_K + N_EXPERTS * CHUNK
N_CHUNKS = SUB_ROWS // CHUNK
ZERO_CHUNK = N_CHUNKS
TABLE_W = 256
MXU_ROWS = 256
BLOCKS_PER_TRIP = 2


def _slot_onehot(first, pos_list, axis_iota):
    shape = axis_iota.shape
    hit = axis_iota == jnp.broadcast_to(pos_list[0] - first, shape).astype(BF16)
    for p in pos_list[1:]:
        hit = hit | (axis_iota == jnp.broadcast_to(p - first, shape).astype(BF16))
    return jnp.where(hit, jnp.ones(shape, BF16), jnp.zeros(shape, BF16))


def _chunk_copy(src, src_chunk, dst, dst_chunk, sem):
    return pltpu.make_async_copy(src.at[src_chunk], dst.at[dst_chunk], sem)


def _for_each(n, body):
    def two(t, carry):
        body(2 * t)
        body(2 * t + 1)
        return carry

    lax.fori_loop(0, n // 2, two, 0)

    @pl.when(n % 2 == 1)
    def _():
        body(n - 1)


def _dispatch_kernel(nblk_ref, nreal_ref, ncopy_ref, ntile_ref, src_ref, dst_ref, posr_ref, xg_ref, xs_hbm,
                     xs_scr, zero_scr, sem, tail_sem, *, nsub, n_tiles_max):
    s = pl.program_id(0)

    tile_chunks = FFN_TM // CHUNK
    blk_chunks = MXU_ROWS // CHUNK
    zero_scr[...] = jnp.zeros((tile_chunks, CHUNK, XG_W), BF16)
    tail_blocks = [ntile_ref[0] + s + r * nsub for r in range(-(-(n_tiles_max) // nsub))]

    def tail_copy(t):
        return pltpu.make_async_copy(
            zero_scr, xs_hbm.at[pl.ds(pl.multiple_of(t * tile_chunks, tile_chunks), tile_chunks)], tail_sem)

    for t in tail_blocks:
        @pl.when(t < n_tiles_max)
        def _():
            tail_copy(t).start()

    xg = xg_ref[...]
    pos = [posr_ref[0, k:k + 1, :] for k in range(TOP_K)]
    rows = lax.broadcasted_iota(jnp.int32, (MXU_ROWS, SUB), 0).astype(BF16)
    xs_scr[ZERO_CHUNK] = jnp.zeros((CHUNK, XG_W), BF16)

    def sort_blocks(i, carry):
        for j in range(BLOCKS_PER_TRIP):
            blk = i * BLOCKS_PER_TRIP + j
            onehot = _slot_onehot((blk * MXU_ROWS).astype(F32), pos, rows)
            sorted_rows = _dot(onehot, xg).astype(BF16)
            xs_scr[pl.ds(pl.multiple_of(blk * blk_chunks, blk_chunks), blk_chunks)] = (
                sorted_rows.reshape(blk_chunks, CHUNK, XG_W))
        k0 = i * trip_chunks
        _for_each(jnp.clip(n_real - k0, 0, trip_chunks), lambda t: send(k0 + t))
        return carry

    n_real = nreal_ref[s]
    n = ncopy_ref[s]
    trip_chunks = BLOCKS_PER_TRIP * blk_chunks

    def send(i):
        _chunk_copy(xs_scr, src_ref[0, 0, i], xs_hbm, dst_ref[0, 0, i], sem).start()

    lax.fori_loop(0, nblk_ref[s], sort_blocks, 0)
    _for_each(n - n_real, lambda t: send(n_real + t))
    _for_each(n, lambda i: _chunk_copy(xs_scr, 0, xs_hbm, 0, sem).wait())
    for t in tail_blocks:
        @pl.when(t < n_tiles_max)
        def _():
            tail_copy(t).wait()


def _dispatch(nblk, n_real, ncopy, n_tiles, src, dst, posr, xg, n_rows):
    nsub = posr.shape[0]
    tab = pl.BlockSpec((1, 1, TABLE_W), lambda s, *_: (s, 0, 0), memory_space=pltpu.SMEM)
    return pl.pallas_call(
        functools.partial(_dispatch_kernel, nsub=nsub, n_tiles_max=n_rows // FFN_TM),
        grid_spec=pltpu.PrefetchScalarGridSpec(
            num_scalar_prefetch=4, grid=(nsub,),
            in_specs=[tab, tab, pl.BlockSpec((1, TOP_K, SUB), lambda s, *_: (s, 0, 0)),
                      pl.BlockSpec((SUB, XG_W), lambda s, *_: (s, 0))],
            out_specs=pl.BlockSpec(memory_space=pl.ANY),
            scratch_shapes=[pltpu.VMEM((N_CHUNKS + 1, CHUNK, XG_W), BF16),
                            pltpu.VMEM((FFN_TM // CHUNK, CHUNK, XG_W), BF16),
                            pltpu.SemaphoreType.DMA(()), pltpu.SemaphoreType.DMA(())]),
        out_shape=jax.ShapeDtypeStruct((n_rows // CHUNK, CHUNK, XG_W), BF16),
        compiler_params=_params(("arbitrary",)),
        name="moe_dispatch",
    )(nblk, n_real, ncopy, n_tiles, src, dst, posr, xg).reshape(n_rows, XG_W)


def _ffn_kernel(te_ref, nt_ref, rb_ref, rdo_ref, xs_ref, wg_ref, wu_ref, wd_ref,
                q_ref, kc_ref, vc_ref, nk_ref, nv_ref, b1_ref, b2_ref,
                ys_ref, ko_ref, vo_ref, o_ref, l_ref, wg_s, wu_s, wd_s):
    i = pl.program_id(0)

    @pl.when(rdo_ref[i] == 1)
    def _():
        _attn_sample_batch(rb_ref[i], 0, q_ref, kc_ref, vc_ref, nk_ref, nv_ref, b1_ref, b2_ref,
                           ko_ref, vo_ref, o_ref, l_ref)

    @pl.when(i >= nt_ref[0])
    def _():
        ys_ref[...] = jnp.zeros((FFN_TM, D_MODEL), BF16)

    @pl.when(i < nt_ref[0])
    def _():
        e = te_ref[i]

        @pl.when((i == 0) | (e != te_ref[jnp.maximum(i - 1, 0)]))
        def _():
            wg_s[...] = wg_ref[0].astype(BF16)
            wu_s[...] = wu_ref[0].astype(BF16)
            wd_s[...] = wd_ref[0].astype(BF16)

        x = xs_ref[:, 0:D_MODEL]
        g = xs_ref[:, D_MODEL:XG_W].astype(F32)
        lane = lax.broadcasted_iota(jnp.int32, (FFN_TM, LANES), 1)
        gcol = jnp.sum(jnp.where((lane == e) | (lane == e + N_EXPERTS), g, 0.0), axis=1, keepdims=True)
        a = _dot(x, wg_s[...])
        b = _dot(x, wu_s[...])
        hc = _silu(a) * b * gcol
        ys_ref[...] = _dot(hc.astype(BF16), wd_s[...]).astype(BF16)


def _ffn(tile_expert, n_tiles, xs, wg, wu, wd, ride):
    q, kc, vc, nk, nv, b1, b2 = ride
    nb, _, _, wb = kc.shape
    n_rows = xs.shape[0]
    n_steps = n_rows // FFN_TM
    step = jnp.arange(n_steps, dtype=jnp.int32)
    ride_b = jnp.minimum(step * nb // n_tiles[0], nb - 1)
    ride_do = ((step < n_tiles[0]) & ((step == 0) | (ride_b != jnp.roll(ride_b, 1)))).astype(jnp.int32)
    row = lambda i, te, nt, rb, rdo: (jnp.minimum(i, nt[0] - 1), 0)
    wspec = lambda shape: pl.BlockSpec((1,) + shape, lambda i, te, nt, rb, rdo: (te[i], 0, 0))
    rblk = lambda *tail: pl.BlockSpec((1, HEADS_PER_GROUP) + tail, lambda i, te, nt, rb, rdo: (rb[i], 0, 0, 0))
    const = lambda shape: pl.BlockSpec(shape, lambda i, te, nt, rb, rdo: (0, 0, 0))
    return pl.pallas_call(
        _ffn_kernel,
        grid_spec=pltpu.PrefetchScalarGridSpec(
            num_scalar_prefetch=4, grid=(n_steps,),
            in_specs=[pl.BlockSpec((FFN_TM, XG_W), row), wspec((D_MODEL, D_EXPERT)),
                      wspec((D_MODEL, D_EXPERT)), wspec((D_EXPERT, D_MODEL)),
                      rblk(8, HEAD_DIM), rblk(HEAD_DIM, wb), rblk(HEAD_DIM, wb), const(nk.shape), const(nv.shape),
                      const(b1.shape), const(b2.shape)],
            out_specs=[pl.BlockSpec((FFN_TM, D_MODEL), lambda i, te, nt, rb, rdo: (i, 0)),
                       rblk(HEAD_DIM, wb), rblk(HEAD_DIM, wb), rblk(8, HEAD_DIM), rblk(8, LANES)],
            scratch_shapes=[pltpu.VMEM((D_MODEL, D_EXPERT), BF16), pltpu.VMEM((D_MODEL, D_EXPERT), BF16),
                            pltpu.VMEM((D_EXPERT, D_MODEL), BF16)]),
        out_shape=[jax.ShapeDtypeStruct((n_rows, D_MODEL), BF16),
                   jax.ShapeDtypeStruct(kc.shape, F32), jax.ShapeDtypeStruct(kc.shape, F32),
                   jax.ShapeDtypeStruct((nb, HEADS_PER_GROUP, 8, HEAD_DIM), F32),
                   jax.ShapeDtypeStruct((nb, HEADS_PER_GROUP, 8, LANES), F32)],
        compiler_params=_params(("arbitrary",)),
        name="moe_ffn",
    )(tile_expert, n_tiles, ride_b, ride_do, xs, wg, wu, wd, q, kc, vc, nk, nv, b1, b2)


def _combine_kernel(nblk_ref, ncopy_ref, src_ref, post_ref, xg_ref, x1_ref, g2_ref, wgs_ref, wus_ref,
                    wds_ref, ys_hbm, y_ref, ys_scr, wgs_s, wus_s, wds_s, sem):
    s = pl.program_id(0)
    n = ncopy_ref[s]
    blk_chunks = MXU_ROWS // CHUNK
    trip_chunks = BLOCKS_PER_TRIP * blk_chunks

    def fetch_trip(t, carry):
        k0 = t * trip_chunks
        _for_each(trip_chunks,
                  lambda u: _chunk_copy(ys_hbm, src_ref[0, 0, k0 + u], ys_scr, k0 + u, sem.at[t]).start())
        return carry

    lax.fori_loop(0, nblk_ref[s], fetch_trip, 0)

    @pl.when(s == 0)
    def _():
        wgs_s[...] = wgs_ref[...].astype(BF16)
        wus_s[...] = wus_ref[...].astype(BF16)
        wds_s[...] = wds_ref[...].astype(BF16)

    h = xg_ref[:, 0:D_MODEL]
    a = _dot(h, wgs_s[...])
    b = _dot(h, wus_s[...])
    shared = _dot((_silu(a) * b).astype(BF16), wds_s[...])

    post = post_ref[...]
    pos = [jnp.broadcast_to(post[:, k:k + 1], (SUB, MXU_ROWS)) for k in range(TOP_K)]
    lanes = lax.broadcasted_iota(jnp.int32, (SUB, MXU_ROWS), 1).astype(BF16)

    def gather_blocks(i, acc):
        _for_each(trip_chunks, lambda t: _chunk_copy(ys_hbm, 0, ys_scr, 0, sem.at[i]).wait())
        for j in range(BLOCKS_PER_TRIP):
            blk = i * BLOCKS_PER_TRIP + j
            onehot = _slot_onehot((blk * MXU_ROWS).astype(F32), pos, lanes)
            rows = ys_scr[pl.ds(pl.multiple_of(blk * blk_chunks, blk_chunks), blk_chunks)]
            acc = acc + _dot(onehot, rows.reshape(MXU_ROWS, D_MODEL))
        return acc

    routed = lax.fori_loop(0, nblk_ref[s], gather_blocks, jnp.zeros((SUB, D_MODEL), F32))
    y_ref[...] = x1_ref[...] + g2_ref[0] * (routed + shared)


def _combine(nblk, ncopy, src, post, xg, x1, g2, wgs, wus, wds, ys):
    n_tok = x1.shape[0]
    nsub = n_tok // SUB
    per_b = nsub // g2.shape[0]
    tab = pl.BlockSpec((1, 1, TABLE_W), lambda s, *_: (s, 0, 0), memory_space=pltpu.SMEM)
    const2 = lambda shape: pl.BlockSpec(shape, lambda s, *_: (0, 0))
    return pl.pallas_call(
        _combine_kernel,
        grid_spec=pltpu.PrefetchScalarGridSpec(
            num_scalar_prefetch=2, grid=(nsub,),
            in_specs=[tab, pl.BlockSpec((SUB, LANES), lambda s, *_: (s, 0)),
                      pl.BlockSpec((SUB, XG_W), lambda s, *_: (s, 0)),
                      pl.BlockSpec((SUB, D_MODEL), lambda s, *_: (s, 0)),
                      pl.BlockSpec((1, 1, D_MODEL), lambda s, *_: (s // per_b, 0, 0)),
                      const2((D_MODEL, D_EXPERT)), const2((D_MODEL, D_EXPERT)), const2((D_EXPERT, D_MODEL)),
                      pl.BlockSpec(memory_space=pl.ANY)],
            out_specs=pl.BlockSpec((SUB, D_MODEL), lambda s, *_: (s, 0)),
            scratch_shapes=[pltpu.VMEM((N_CHUNKS, CHUNK, D_MODEL), BF16), pltpu.VMEM((D_MODEL, D_EXPERT), BF16),
                            pltpu.VMEM((D_MODEL, D_EXPERT), BF16), pltpu.VMEM((D_EXPERT, D_MODEL), BF16),
                            pltpu.SemaphoreType.DMA((SUB_ROWS // (BLOCKS_PER_TRIP * MXU_ROWS),))]),
        out_shape=jax.ShapeDtypeStruct((n_tok, D_MODEL), F32),
        compiler_params=_params(("arbitrary",)),
        name="moe_combine",
    )(nblk, ncopy, src, post, xg, x1, g2, wgs, wus, wds, ys.reshape(-1, CHUNK, D_MODEL))


def _route_tables(cnt):
    nsub = cnt.shape[0]
    i32 = jnp.int32
    seg_end = jnp.cumsum(cnt, axis=1)
    seg_start = seg_end - cnt
    tot = jnp.sum(cnt, axis=0)
    tot_al = (tot + FFN_TM - 1) // FFN_TM * FFN_TM
    reg_end = jnp.cumsum(tot_al)
    reg_start = reg_end - tot_al
    base = reg_start[None, :] + jnp.cumsum(cnt, axis=0) - cnt
    n_real = seg_end[:, -1] // CHUNK
    k = jnp.arange(TABLE_W, dtype=i32)
    owner = jnp.sum((seg_end[:, None, :] // CHUNK) <= k[None, :, None], axis=2)
    owner = jnp.minimum(owner, N_EXPERTS - 1)
    delta = (base - seg_start) // CHUNK
    e_ids = jnp.arange(N_EXPERTS, dtype=i32)
    real_dst = jnp.sum(jnp.where(owner[:, :, None] == e_ids, delta[:, None, :], 0), axis=2) + k[None, :]
    fill_n = (tot_al - tot) // CHUNK
    fill_dst0 = (reg_start + tot) // CHUNK
    src = jnp.broadcast_to(k[None, :], (nsub, TABLE_W))
    dst = real_dst
    ncopy = n_real
    for r in range(-(-N_EXPERTS // nsub)):
        e_of = jnp.arange(nsub, dtype=i32) + r * nsub
        ok = e_of < N_EXPERTS
        e_cl = jnp.minimum(e_of, N_EXPERTS - 1)
        fn = jnp.where(ok, fill_n[e_cl], 0)
        j = k[None, :] - ncopy[:, None]
        is_fill = (j >= 0) & (j < fn[:, None])
        src = jnp.where(is_fill, ZERO_CHUNK, src)
        dst = jnp.where(is_fill, fill_dst0[e_cl][:, None] + j, dst)
        ncopy = ncopy + fn
    trip_rows = BLOCKS_PER_TRIP * MXU_ROWS
    disp_nblk = (n_real * CHUNK + trip_rows - 1) // trip_rows
    comb_n = disp_nblk * (trip_rows // CHUNK)
    comb_src = jnp.where(k[None, :] < n_real[:, None], real_dst, real_dst[:, 0:1])
    n_tiles = reg_end[-1] // FFN_TM
    t = jnp.arange((nsub * SUB_ROWS + N_EXPERTS * FFN_TM) // FFN_TM, dtype=i32)
    tile_expert = jnp.minimum(jnp.sum((reg_end[None, :] // FFN_TM) <= t[:, None], axis=1), N_EXPERTS - 1)
    as3 = lambda a: a.astype(i32).reshape(nsub, 1, TABLE_W)
    return (disp_nblk.astype(i32), n_real.astype(i32), ncopy.astype(i32), as3(src), as3(dst),
            comb_n.astype(i32), as3(comb_src), tile_expert.astype(i32), n_tiles.astype(i32).reshape(1))


def _moe_sparse(xg, post, posr, cnt, x1, g2, wg, wu, wd, wgs, wus, wds, ride):
    B, S, _ = x1.shape
    n_tok = B * S
    nsub = n_tok // SUB
    (disp_nblk, n_real, disp_n, disp_src, disp_dst, comb_n, comb_src, tile_expert,
     n_tiles) = _route_tables(cnt[:, :, 0].astype(jnp.int32))
    n_rows = nsub * SUB_ROWS + N_EXPERTS * FFN_TM
    assert n_tok * TOP_K // FFN_TM >= ride[1].shape[0]
    xg2 = xg.reshape(n_tok, XG_W)
    xs = _dispatch(disp_nblk, n_real, disp_n, n_tiles, disp_src, disp_dst, posr, xg2, n_rows)
    ys, *ride_out = _ffn(tile_expert, n_tiles, xs, wg, wu, wd, ride)
    y = _combine(disp_nblk, comb_n, comb_src, post.reshape(n_tok, LANES), xg2, x1.reshape(n_tok, D_MODEL),
                 g2, wgs, wus, wds, ys)
    return y.reshape(B, S, D_MODEL), ride_out


def _t5_bucket(dist):
    max_exact = N_BUCKETS // 2
    df = jnp.maximum(dist, 1).astype(F32)
    large = max_exact + (jnp.log(df / max_exact) / math.log(MAX_DISTANCE / max_exact)
                         * (N_BUCKETS - max_exact)).astype(jnp.int32)
    large = jnp.minimum(large, N_BUCKETS - 1)
    return jnp.where(dist < max_exact, dist, large)


def _step_bias(rel_bias, g):
    dist = jnp.arange(N_KEYS, dtype=jnp.int32) * DILATIONS[g]
    cols = rel_bias[:, g * HEADS_PER_GROUP:(g + 1) * HEADS_PER_GROUP]
    return cols[_t5_bucket(dist)].T.astype(F32)


def _prompt_bias(bias_k):
    h = bias_k.shape[0]
    n = 3 * BLK
    row = jnp.concatenate([bias_k[:, ::-1], jnp.full((h, n - N_KEYS), NEG, F32)], axis=1)
    t = jnp.tile(row, (1, BLK))[:, :BLK * (n - 1)].reshape(h, BLK, n - 1)
    return t[:, :, :2 * BLK]


def _sample_bias(bias_k, wb, d):
    h = bias_k.shape[0]
    rev = bias_k[:, :0:-1]
    rows = []
    for t in range(4):
        if d == 1:
            rows.append(jnp.concatenate([jnp.full((h, t), NEG, F32), rev[:, :wb - t]], axis=1))
        else:
            r = jnp.arange(d)[None, None, :]
            rows.append(jnp.where(r == t, rev[:, :, None], NEG).reshape(h, wb))
    b1 = jnp.concatenate([jnp.stack(rows, axis=1), jnp.zeros((h, 4, wb), F32)], axis=1)
    rows = []
    for t in range(4):
        cols = [bias_k[:, (t - tn) // d] if (t >= tn and (t - tn) % d == 0) else jnp.full((h,), NEG, F32)
                for tn in range(4)]
        rows.append(jnp.concatenate([jnp.full((h, LANES - 4), NEG, F32), jnp.stack(cols, axis=1)], axis=1))
    b2 = jnp.concatenate([jnp.stack(rows, axis=1), jnp.zeros((h, 4, LANES), F32)], axis=1)
    return b1, b2


TILE = 512
RIDE_GROUP = N_GROUPS - 1


def _sample_front(x, mods, hist, caches, rel_bias, weights):
    norm1_g, _, w_in_bf, qg, kg, conv_w, bd = weights[:7]
    sh1, sc1 = mods[:2]
    S = x.shape[1]
    nb = S // 4
    qf, kt, vt, u, co = _proj(x, sc1, sh1, norm1_g, w_in_bf, qg, kg, conv_w, bd, hist,
                              tile=TILE, dils=(1, 1, 1), sample=True)
    done, ride = {}, None
    for g in range(N_GROUPS):
        d, wb = DILATIONS[g], WINDOWS[g]
        kc, vc = caches[g]
        b1, b2 = _sample_bias(_step_bias(rel_bias, g), wb, d)
        q = qf[0, :, g * GROUP_W:(g + 1) * GROUP_W].reshape(nb, 4, HEADS_PER_GROUP, HEAD_DIM)
        q = jnp.pad(q.transpose(0, 2, 1, 3), ((0, 0), (0, 0), (0, 4), (0, 0))).astype(BF16)
        new_t = lambda a: a[g * GROUP_W:(g + 1) * GROUP_W].reshape(HEADS_PER_GROUP, HEAD_DIM, S)
        operands = (q, kc, vc, new_t(kt), new_t(vt), b1, b2)
        if g == RIDE_GROUP:
            ride = operands
        else:
            done[g] = _attn_sample(*operands, bb={128: 8, 512: 8}[wb])
    return dict(x=x, mods=mods, u=u, co=co, done=done), ride


def _sample_back(front, ride_out, weights):
    (_, norm2_g, _, _, _, _, _, w_o_bf, w_r_t, b_r, w_gate_e, w_up_e, w_down_e,
     w_gate_s, w_up_s, w_down_s) = weights
    x, u, co = front["x"], front["u"], front["co"]
    _, _, g1, sh2, sc2, g2 = front["mods"]
    S = x.shape[1]
    nb = S // 4
    front["done"][RIDE_GROUP] = ride_out
    states, o_list, l_list = [], [], []
    for g in range(N_GROUPS):
        ko, vo, o, lse = front["done"][g]
        states += [ko.transpose(0, 3, 1, 2)[None], vo.transpose(0, 3, 1, 2)[None]]
        o_list.append(o[:, :, :4].transpose(0, 2, 1, 3).reshape(1, 1, S, GROUP_W))
        lse = jnp.broadcast_to(lse[:, :, :4, :1], (nb, HEADS_PER_GROUP, 4, HEAD_DIM))
        l_list.append(lse.transpose(0, 2, 1, 3).reshape(1, 1, S, GROUP_W))
    states.append(u.reshape(nb, 4, CONV_CH)[:, 2:][None])
    x1, xg, gates, _, _, _ = _oproj(x, o_list, l_list, co, g1, sc2, sh2, norm2_g, w_o_bf, w_r_t, b_r,
                                    tile=TILE, dils=(1, 1, 1))
    y = _moe(xg, gates, x1, g2, w_gate_e, w_up_e, w_down_e, w_gate_s, w_up_s, w_down_s, tile=TILE)
    return y, states


def _prompt_layer(x, mods, rel_bias, weights, ride):
    (norm1_g, norm2_g, w_in_bf, qg, kg, conv_w, bd, w_o_bf, w_r_t, b_r, w_gate_e, w_up_e, w_down_e,
     w_gate_s, w_up_s, w_down_s) = weights
    sh1, sc1, g1, sh2, sc2, g2 = mods
    B, S, _ = x.shape
    res = _proj(x, sc1, sh1, norm1_g, w_in_bf, qg, kg, conv_w, bd, None, tile=TILE, dils=DILATIONS, sample=False)
    qkv, tails, (u, co) = res[:9], res[9:15], res[15:]
    states, o_list, l_list = [], [], []
    for g in range(N_GROUPS):
        d, w = DILATIONS[g], WINDOWS[g]
        q, k, v = (a.reshape(B * d, S // d, GROUP_W) for a in qkv[3 * g:3 * g + 3])
        o, lse = _attn_prompt(q, k, v, _prompt_bias(_step_bias(rel_bias, g)))
        o_list.append(o.reshape(B, d, S // d, GROUP_W))
        l_list.append(lse.reshape(B, d, S // d, GROUP_W))
        keep = lambda a: a.reshape(B, HEADS_PER_GROUP, HEAD_DIM, w).transpose(0, 3, 1, 2)[None]
        states += [keep(tails[2 * g]), keep(tails[2 * g + 1])]
    states.append(u[:, 6:][None])
    x1, xg, _, posr, post, cnt = _oproj(x, o_list, l_list, co, g1, sc2, sh2, norm2_g, w_o_bf, w_r_t, b_r,
                                        tile=TILE, dils=DILATIONS)
    y, ride_out = _moe_sparse(xg, post, posr, cnt, x1, g2, w_gate_e, w_up_e, w_down_e, w_gate_s, w_up_s,
                              w_down_s, ride)
    return y, states, ride_out


def kernel(x_prompt, x_sample, c_prompt, c_sample, cache_k_w128, cache_v_w128, cache_k_w512, cache_v_w512, cache_k_w2048, cache_v_w2048, state_conv, rel_bias, norm1_g, norm2_g, w_ada, b_ada, w_in, q_norm_g, k_norm_g, conv_w, w_o, w_router, b_router, w_gate_e, w_up_e, w_down_e, w_gate_s, w_up_s, w_down_s):
    B = x_prompt.shape[0]
    DB, T = x_sample.shape[:2]
    n_c = B + DB
    c_all = jnp.pad(jnp.concatenate([c_prompt, c_sample], axis=0), ((0, (-n_c) % 8), (0, 0)))
    mod = _ada(c_all, w_ada[0], b_ada)
    chunks = [mod[:, i * D_MODEL:(i + 1) * D_MODEL] for i in range(6)]
    mods_p = [c[:B].reshape(B, 1, D_MODEL) for c in chunks]
    mods_s = [jnp.repeat(c[B:n_c], T, axis=0).reshape(1, DB * T, D_MODEL) for c in chunks]

    eye = jnp.arange(GROUP_W) // HEAD_DIM
    bd = jnp.where(eye[:, None] == eye[None, :], 1.0 / HEAD_DIM, 0.0).astype(BF16)
    weights = (norm1_g, norm2_g, w_in[0].astype(BF16),
               jnp.tile(q_norm_g, (1, HEADS_PER_GROUP)), jnp.tile(k_norm_g, (1, HEADS_PER_GROUP)),
               conv_w[0], bd, w_o[0].astype(BF16), w_router[0].T, b_router.reshape(N_EXPERTS, 1),
               w_gate_e[0], w_up_e[0], w_down_e[0], w_gate_s[0], w_up_s[0], w_down_s[0])

    s0, s1 = state_conv[0, :, 0], state_conv[0, :, 1]
    zero = jnp.zeros_like(s0)
    hist_a = jnp.stack([s0, s1, zero, zero], axis=1).reshape(DB * T, CONV_CH)
    hist_b = jnp.stack([s1, zero, zero, zero], axis=1).reshape(DB * T, CONV_CH)
    caches = [(ck[0].transpose(0, 2, 3, 1), cv[0].transpose(0, 2, 3, 1))
              for ck, cv in ((cache_k_w128, cache_v_w128), (cache_k_w512, cache_v_w512),
                             (cache_k_w2048, cache_v_w2048))]
    front, ride = _sample_front(x_sample.reshape(1, DB * T, D_MODEL), mods_s, (hist_a, hist_b), caches,
                                rel_bias, weights)
    yp, st_p, ride_out = _prompt_layer(x_prompt, mods_p, rel_bias, weights, ride)
    ys, st_s = _sample_back(front, ride_out, weights)
    return (yp, ys.reshape(DB, T, D_MODEL), *st_p, *st_s)
```

```python
import functools
import math

import jax
import jax.numpy as jnp
from jax import lax
from jax.experimental import pallas as pl
from jax.experimental.pallas import tpu as pltpu

D_MODEL = 1024
HEAD_DIM = 64
HEADS_PER_GROUP = 4
GROUP_W = HEADS_PER_GROUP * HEAD_DIM
WINDOWS = (128, 512, 2048)
DILATIONS = (1, 4, 16)
N_GROUPS = 3
ATTN_W = N_GROUPS * GROUP_W
CONV_CH = 256
N_PROJ = 3 * ATTN_W + 3 * CONV_CH
N_STEPS = 128
N_KEYS = N_STEPS + 1
BLK = 128
N_BUCKETS = 32
MAX_DISTANCE = WINDOWS[-1]
N_EXPERTS = 64
TOP_K = 8
N_ROUTE_GROUPS = 8
GROUP_SIZE = N_EXPERTS // N_ROUTE_GROUPS
TOPK_GROUPS = 4
D_EXPERT = 256
ROUTED_SCALE = 2.5
RMS_EPS = 1e-6
NEG = -1e30
LANES = 128
SUB = 256
CHUNK = 16
XG_W = D_MODEL + LANES
VMEM_LIMIT = 56 * 1024 * 1024

F32 = jnp.float32
BF16 = jnp.bfloat16


def _dot(a, b):
    return jnp.dot(a, b, preferred_element_type=F32)


def _dot_nt(a, b):
    return lax.dot_general(a, b, (((1,), (1,)), ((), ())), preferred_element_type=F32)


def _silu(a):
    return a / (1.0 + jnp.exp(-a))


def _params(sem):
    return pltpu.CompilerParams(dimension_semantics=sem, vmem_limit_bytes=VMEM_LIMIT)


def _ada_kernel(c_ref, w_ref, b_ref, o_ref):
    s = _silu(c_ref[...])
    o_ref[...] = _dot(s.astype(BF16), w_ref[...].astype(BF16)) + b_ref[...]


def _ada(c, w_ada, b_ada):
    n = c.shape[0]
    nc = w_ada.shape[1] // D_MODEL
    return pl.pallas_call(
        _ada_kernel,
        grid=(nc,),
        in_specs=[pl.BlockSpec((n, D_MODEL), lambda j: (0, 0)),
                  pl.BlockSpec((D_MODEL, D_MODEL), lambda j: (0, j)),
                  pl.BlockSpec((1, D_MODEL), lambda j: (0, j))],
        out_specs=pl.BlockSpec((n, D_MODEL), lambda j: (0, j)),
        out_shape=jax.ShapeDtypeStruct((n, w_ada.shape[1]), F32),
        compiler_params=_params(("arbitrary",)),
        name="ada",
    )(c, w_ada, b_ada)


def _proj_kernel(*refs, tile, dils, sample, nt):
    if sample:
        (x_ref, sc_ref, sh_ref, ng_ref, w_ref, qg_ref, kg_ref, cw_ref, bd_ref, ha_ref, hb_ref,
         qf_ref, kt_ref, vt_ref, u_ref, co_ref, u_scr) = refs
    else:
        (x_ref, sc_ref, sh_ref, ng_ref, w_ref, qg_ref, kg_ref, cw_ref, bd_ref,
         q0, k0, v0, q1, k1, v1, q2, k2, v2, kt0, vt0, kt1, vt1, kt2, vt2, u_ref, co_ref, slab, u_scr) = refs
        qkv_out = ((q0, k0, v0), (q1, k1, v1), (q2, k2, v2))
        tails = ((kt0, vt0), (kt1, vt1), (kt2, vt2))
        j = pl.program_id(1)

    x = x_ref[0]
    ms = jnp.mean(x * x, axis=-1, keepdims=True)
    h = x * lax.rsqrt(ms + RMS_EPS) * ng_ref[...]
    h = h * (1.0 + sc_ref[0]) + sh_ref[0]
    proj = _dot(h.astype(BF16), w_ref[...])

    bd = bd_ref[...]

    def headnorm(z, g):
        zz = z * z
        hi = zz.astype(BF16)
        lo = (zz - hi.astype(F32)).astype(BF16)
        msq = _dot(hi, bd) + _dot(lo, bd)
        return z * lax.rsqrt(msq + RMS_EPS) * g

    slab_i = 0
    for g in range(N_GROUPS):
        c0 = g * GROUP_W
        qn = headnorm(proj[:, c0:c0 + GROUP_W], qg_ref[...]) * (HEAD_DIM ** -0.5)
        kn = headnorm(proj[:, ATTN_W + c0:ATTN_W + c0 + GROUP_W], kg_ref[...])
        vv = proj[:, 2 * ATTN_W + c0:2 * ATTN_W + c0 + GROUP_W]
        if sample:
            qf_ref[0, :, c0:c0 + GROUP_W] = qn
            kt_ref[c0:c0 + GROUP_W, :] = kn.T
            vt_ref[c0:c0 + GROUP_W, :] = vv.T
            continue
        keep = min(WINDOWS[g], tile)

        @pl.when(j >= nt - max(WINDOWS[g] // tile, 1))
        def _(kn=kn, vv=vv, g=g, keep=keep):
            tails[g][0][0] = kn[tile - keep:, :].T
            tails[g][1][0] = vv[tile - keep:, :].T

        d = dils[g]
        for val, out in zip((qn, kn, vv), qkv_out[g]):
            if d == 1:
                out[0, 0] = val.astype(BF16)
                continue
            n = tile // d
            for half in range(GROUP_W // LANES):
                slab[slab_i] = val[:, half * LANES:(half + 1) * LANES]
                for r in range(d):
                    out[0, r, :, half * LANES:(half + 1) * LANES] = (
                        slab[slab_i, pl.ds(r, n, stride=d), :].astype(BF16))
                slab_i += 1

    base = 3 * ATTN_W
    u = proj[:, base + 2 * CONV_CH:base + 3 * CONV_CH] * proj[:, base:base + CONV_CH]
    gate_b = proj[:, base + CONV_CH:base + 2 * CONV_CH]
    if sample:
        u_scr[0:8, :] = jnp.zeros((8, CONV_CH), F32)
    else:
        @pl.when(j == 0)
        def _():
            u_scr[0:8, :] = jnp.zeros((8, CONV_CH), F32)

        @pl.when(j > 0)
        def _():
            u_scr[0:8, :] = u_scr[tile:tile + 8, :]

    u_scr[8:tile + 8, :] = u
    um1 = u_scr[7:tile + 7, :]
    um2 = u_scr[6:tile + 6, :]
    if sample:
        t = lax.broadcasted_iota(jnp.int32, (tile, CONV_CH), 0) % 4
        um1 = jnp.where(t >= 1, um1, 0.0) + hb_ref[...]
        um2 = jnp.where(t >= 2, um2, 0.0) + ha_ref[...]
    cw = cw_ref[...]
    conv = cw[0:1] * um2 + cw[1:2] * um1 + cw[2:3] * u
    u_ref[0] = u if sample else u[tile - 8:, :]
    co_ref[0] = (gate_b * conv).astype(BF16)


def _proj(x, sc, sh, ng, w_bf, qg, kg, cw, bd, hist, *, tile, dils, sample):
    B, S, _ = x.shape
    nt = S // tile
    tm = sc.shape[1]
    mod_spec = pl.BlockSpec((1, tm, D_MODEL), (lambda b, j: (b, j, 0)) if tm > 1 else (lambda b, j: (b, 0, 0)))
    const2 = lambda shape: pl.BlockSpec(shape, lambda b, j: (0, 0))
    in_specs = [pl.BlockSpec((1, tile, D_MODEL), lambda b, j: (b, j, 0)), mod_spec, mod_spec,
                const2((1, D_MODEL)), const2((D_MODEL, N_PROJ)), const2((1, GROUP_W)),
                const2((1, GROUP_W)), const2((3, CONV_CH)), const2((GROUP_W, GROUP_W))]
    args = [x, sc, sh, ng, w_bf, qg, kg, cw, bd]
    nat = lambda w: pl.BlockSpec((1, tile, w), lambda b, j: (b, j, 0))
    out_specs, out_shape = [], []
    scratch = []
    if sample:
        assert B == 1 and nt == 1
        in_specs += [pl.BlockSpec((tile, CONV_CH), lambda b, j: (j, 0))] * 2
        args += list(hist)
        out_specs += [nat(ATTN_W)] + [pl.BlockSpec((ATTN_W, tile), lambda b, j: (0, j))] * 2 + [nat(CONV_CH)]
        out_shape += [jax.ShapeDtypeStruct((B, S, ATTN_W), F32)]
        out_shape += [jax.ShapeDtypeStruct((ATTN_W, S), F32)] * 2
        out_shape += [jax.ShapeDtypeStruct((B, S, CONV_CH), F32)]
    else:
        for d in dils:
            for _ in range(3):
                out_specs.append(pl.BlockSpec((1, d, tile // d, GROUP_W), lambda b, j: (b, 0, j, 0)))
                out_shape.append(jax.ShapeDtypeStruct((B, d, S // d, GROUP_W), BF16))
        for w in WINDOWS:
            keep, first = min(w, tile), nt - max(w // tile, 1)
            for _ in range(2):
                out_specs.append(pl.BlockSpec((1, GROUP_W, keep),
                                              lambda b, j, first=first: (b, 0, jnp.maximum(j - first, 0))))
                out_shape.append(jax.ShapeDtypeStruct((B, GROUP_W, w), F32))
        out_specs.append(pl.BlockSpec((1, 8, CONV_CH), lambda b, j: (b, 0, 0)))
        out_shape.append(jax.ShapeDtypeStruct((B, 8, CONV_CH), F32))
        n_slabs = sum(3 * (GROUP_W // LANES) for d in dils if d > 1)
        scratch.append(pltpu.VMEM((n_slabs, tile, LANES), F32))
    out_specs.append(nat(CONV_CH))
    out_shape.append(jax.ShapeDtypeStruct((B, S, CONV_CH), BF16))
    scratch.append(pltpu.VMEM((tile + 8, CONV_CH), F32))
    return pl.pallas_call(
        functools.partial(_proj_kernel, tile=tile, dils=dils, sample=sample, nt=nt),
        grid=(B, nt),
        in_specs=in_specs,
        out_specs=out_specs,
        out_shape=out_shape,
        scratch_shapes=scratch,
        compiler_params=_params(("arbitrary", "arbitrary")),
        name="proj_sample" if sample else "proj_prompt",
    )(*args)


def _attn_kernel(q_ref, kp_ref, kc_ref, vp_ref, vc_ref, b_ref, o_ref, l_ref, s_scr, m_scr, o_scr, z_scr,
                 *, nq):
    first = pl.program_id(1) == 0
    low = lax.broadcasted_iota(jnp.int32, (BLK, LANES), 1) < HEAD_DIM
    ones = jnp.ones((BLK, LANES), BF16)
    chains = [(j, pair, sub) for j in range(nq) for pair in range(GROUP_W // LANES) for sub in range(2)]

    def operands(j, pair):
        rows = slice(j * BLK, (j + 1) * BLK)
        sl = slice(pair * LANES, (pair + 1) * LANES)
        if j == 0:
            return rows, sl, kp_ref[0, :, sl], vp_ref[0, :, sl]
        prows = slice((j - 1) * BLK, j * BLK)
        return rows, sl, kc_ref[0, prows, sl], vc_ref[0, prows, sl]

    for c, (j, pair, sub) in enumerate(chains):
        rows, sl, kp, _ = operands(j, pair)
        q = q_ref[0, rows, sl]
        qm = jnp.where(low if sub == 0 else ~low, q, jnp.zeros_like(q))
        hh = 2 * pair + sub
        sp = _dot_nt(qm, kp) + b_ref[hh, :, 0:BLK]
        if j == 0:
            sp = jnp.where(first, NEG, sp)
        sc = _dot_nt(qm, kc_ref[0, rows, sl]) + b_ref[hh, :, BLK:2 * BLK]
        s_scr[c, :, 0:BLK] = sp
        s_scr[c, :, BLK:2 * BLK] = sc
        m_scr[c] = jnp.broadcast_to(jnp.max(jnp.maximum(sp, sc), axis=-1, keepdims=True), (BLK, LANES))

    for c, (j, pair, sub) in enumerate(chains):
        rows, sl, _, vp = operands(j, pair)
        m = m_scr[c]
        pp = jnp.exp(s_scr[c, :, 0:BLK] - m).astype(BF16)
        pc = jnp.exp(s_scr[c, :, BLK:2 * BLK] - m).astype(BF16)
        o_scr[c] = _dot(pp, vp) + _dot(pc, vc_ref[0, rows, sl])
        z_scr[c] = _dot(pp, ones) + _dot(pc, ones)

    for c in range(0, len(chains), 2):
        j, pair, _ = chains[c]
        rows = slice(j * BLK, (j + 1) * BLK)
        sl = slice(pair * LANES, (pair + 1) * LANES)
        o_sub = [o_scr[c + sub] / z_scr[c + sub] for sub in range(2)]
        l_sub = [m_scr[c + sub] + jnp.log(z_scr[c + sub]) for sub in range(2)]
        o_ref[0, rows, sl] = jnp.where(low, o_sub[0], o_sub[1])
        l_ref[0, rows, sl] = jnp.where(low, l_sub[0], l_sub[1])


def _attn_prompt(q, k, v, bias):
    Z, L, _ = q.shape
    qt = min(L, 4 * BLK)
    nq = qt // BLK
    cur = pl.BlockSpec((1, qt, GROUP_W), lambda z, i: (z, i, 0))
    prev = pl.BlockSpec((1, BLK, GROUP_W), lambda z, i: (z, jnp.maximum(i * nq - 1, 0), 0))
    return pl.pallas_call(
        functools.partial(_attn_kernel, nq=nq),
        grid=(Z, L // qt),
        in_specs=[cur, prev, cur, prev, cur,
                  pl.BlockSpec((HEADS_PER_GROUP, BLK, 2 * BLK), lambda z, i: (0, 0, 0))],
        out_specs=[cur, cur],
        out_shape=[jax.ShapeDtypeStruct((Z, L, GROUP_W), F32)] * 2,
        scratch_shapes=[pltpu.VMEM((nq * HEADS_PER_GROUP, BLK, 2 * BLK), F32)]
        + [pltpu.VMEM((nq * HEADS_PER_GROUP, BLK, LANES), F32)] * 3,
        compiler_params=_params(("arbitrary", "arbitrary")),
        name="attn_prompt",
    )(q, k, k, v, v, bias)


def _attn_sample_batch(gb, b, q_ref, kc_ref, vc_ref, nk_ref, nv_ref, b1_ref, b2_ref, ko_ref, vo_ref, o_ref,
                       l_ref, heads=range(HEADS_PER_GROUP)):
    wb = kc_ref.shape[-1]
    lane = lax.broadcasted_iota(jnp.int32, (HEAD_DIM, LANES), 1)
    new_cols = lane >= LANES - 4
    per_tile = LANES // 4
    tile_at = pl.ds(pl.multiple_of((gb // per_tile) * LANES, LANES), LANES)
    shift = (LANES - 4) - 4 * (gb % per_tile)
    for hh in heads:
        q = q_ref[b, hh]
        kt = kc_ref[b, hh]
        vt = vc_ref[b, hh]
        nkt = pltpu.roll(nk_ref[hh, :, tile_at], shift, axis=1)
        nvt = pltpu.roll(nv_ref[hh, :, tile_at], shift, axis=1)
        s1 = _dot(q, kt.astype(BF16)) + b1_ref[hh]
        s2 = _dot(q, nkt.astype(BF16)) + b2_ref[hh]
        m = jnp.maximum(jnp.max(s1, axis=-1, keepdims=True), jnp.max(s2, axis=-1, keepdims=True))
        p1 = jnp.exp(s1 - m)
        p2 = jnp.exp(s2 - m)
        z = jnp.sum(p1, axis=-1, keepdims=True) + jnp.sum(p2, axis=-1, keepdims=True)
        o = _dot_nt(p1.astype(BF16), vt.astype(BF16)) + _dot_nt(p2.astype(BF16), nvt.astype(BF16))
        o_ref[b, hh] = o / z
        l_ref[b, hh] = jnp.broadcast_to(m + jnp.log(z), (8, LANES))
        for src, new, dst in ((kt, nkt, ko_ref), (vt, nvt, vo_ref)):
            rolled = pltpu.roll(src, wb - 4, axis=1)
            if wb > LANES:
                dst[b, hh, :, 0:wb - LANES] = rolled[:, 0:wb - LANES]
            dst[b, hh, :, wb - LANES:wb] = jnp.where(new_cols, new, rolled[:, wb - LANES:wb])


def _attn_sample_kernel(*refs, bb):
    for b in range(bb):
        _attn_sample_batch(pl.program_id(0) * bb + b, b, *refs)


def _attn_sample(q, kc, vc, nk, nv, b1, b2, *, bb):
    nb, _, _, wb = kc.shape
    blk = lambda *tail: pl.BlockSpec((bb, HEADS_PER_GROUP) + tail, lambda i: (i, 0, 0, 0))
    const = lambda shape: pl.BlockSpec(shape, lambda i: (0, 0, 0))
    return pl.pallas_call(
        functools.partial(_attn_sample_kernel, bb=bb),
        grid=(nb // bb,),
        in_specs=[blk(8, HEAD_DIM), blk(HEAD_DIM, wb), blk(HEAD_DIM, wb), const(nk.shape), const(nv.shape),
                  const((HEADS_PER_GROUP, 8, wb)), const((HEADS_PER_GROUP, 8, LANES))],
        out_specs=[blk(HEAD_DIM, wb), blk(HEAD_DIM, wb), blk(8, HEAD_DIM), blk(8, LANES)],
        out_shape=[jax.ShapeDtypeStruct(kc.shape, F32), jax.ShapeDtypeStruct(kc.shape, F32),
                   jax.ShapeDtypeStruct((nb, HEADS_PER_GROUP, 8, HEAD_DIM), F32),
                   jax.ShapeDtypeStruct((nb, HEADS_PER_GROUP, 8, LANES), F32)],
        compiler_params=_params(("arbitrary",)),
        name="attn_sample",
    )(q, kc, vc, nk, nv, b1, b2)


def _first_max(v, ids, sentinel):
    m = jnp.max(v, axis=0, keepdims=True)
    idx = jnp.min(jnp.where(v == m, ids, sentinel), axis=0, keepdims=True)
    return m, ids == idx


def _oproj_kernel(x_ref, o0, o1, o2, l0, l1, l2, co_ref, g1_ref, sc_ref, sh_ref, ng_ref, wo_ref,
                  wr_ref, br_ref, x1_ref, xg_ref, gt_ref, posr_ref, post_ref, cnt_ref, slab, *, tile, dils):
    o_refs, l_refs = (o0, o1, o2), (l0, l1, l2)
    outs, lses = [], []
    slab_i = 0
    for g, d in enumerate(dils):
        if d == 1:
            outs.append(o_refs[g][0, 0])
            lses.append(l_refs[g][0, 0])
            continue
        n = tile // d
        for ref, dest in ((o_refs[g], outs), (l_refs[g], lses)):
            halves = []
            for half in range(GROUP_W // LANES):
                for r in range(d):
                    slab[slab_i, pl.ds(r, n, stride=d), :] = ref[0, r, :, half * LANES:(half + 1) * LANES]
                halves.append(slab[slab_i])
                slab_i += 1
            dest.append(jnp.concatenate(halves, axis=1))

    m = jnp.maximum(jnp.maximum(lses[0], lses[1]), lses[2])
    es = [jnp.exp(l - m) for l in lses]
    den = es[0] + es[1] + es[2]
    cat = [(outs[g] * (es[g] / den)).astype(BF16) for g in range(N_GROUPS)]
    cat.append(co_ref[0].astype(BF16))
    cat = jnp.concatenate(cat, axis=1)
    x1 = x_ref[0] + g1_ref[0] * _dot(cat, wo_ref[...])
    x1_ref[0] = x1

    ms = jnp.mean(x1 * x1, axis=-1, keepdims=True)
    h2 = x1 * lax.rsqrt(ms + RMS_EPS) * ng_ref[...]
    h2 = h2 * (1.0 + sc_ref[0]) + sh_ref[0]
    hh = h2.astype(BF16)
    xg_ref[0, :, 0:D_MODEL] = hh

    hl = (h2 - hh.astype(F32)).astype(BF16)
    wr = wr_ref[...]
    wh = wr.astype(BF16)
    wl = (wr - wh.astype(F32)).astype(BF16)
    logits = _dot_nt(wh, hh) + _dot_nt(wh, hl) + _dot_nt(wl, hh)
    scores = 1.0 / (1.0 + jnp.exp(-logits))
    sel = scores + br_ref[...]

    ids = lax.broadcasted_iota(jnp.int32, (GROUP_SIZE, tile), 0)
    ninf = -jnp.inf
    sel_g = [sel[g * GROUP_SIZE:(g + 1) * GROUP_SIZE] for g in range(N_ROUTE_GROUPS)]
    gscore = jnp.zeros((N_ROUTE_GROUPS, tile), F32)
    for g in range(N_ROUTE_GROUPS):
        m1, oh = _first_max(sel_g[g], ids, GROUP_SIZE)
        m2 = jnp.max(jnp.where(oh, ninf, sel_g[g]), axis=0, keepdims=True)
        gscore = jnp.where(ids == g, m1 + m2, gscore)
    gsel = jnp.zeros((N_ROUTE_GROUPS, tile), F32)
    for _ in range(TOPK_GROUPS):
        _, oh = _first_max(gscore, ids, N_ROUTE_GROUPS)
        gsel = jnp.where(oh, 1.0, gsel)
        gscore = jnp.where(oh, ninf, gscore)
    cand = [jnp.where(gsel[g:g + 1] > 0.0, sel_g[g], ninf) for g in range(N_ROUTE_GROUPS)]
    chosen = [jnp.zeros((GROUP_SIZE, tile), F32) for _ in range(N_ROUTE_GROUPS)]
    picks = []
    for _ in range(TOP_K):
        mx = cand[0]
        for g in range(1, N_ROUTE_GROUPS):
            mx = jnp.maximum(mx, cand[g])
        mx = jnp.max(mx, axis=0, keepdims=True)
        idx = jnp.where(cand[0] == mx, ids, N_EXPERTS)
        for g in range(1, N_ROUTE_GROUPS):
            idx = jnp.minimum(idx, jnp.where(cand[g] == mx, ids + g * GROUP_SIZE, N_EXPERTS))
        idx = jnp.min(idx, axis=0, keepdims=True)
        ohs = [(ids + g * GROUP_SIZE) == idx for g in range(N_ROUTE_GROUPS)]
        picks.append(ohs)
        for g in range(N_ROUTE_GROUPS):
            chosen[g] = jnp.where(ohs[g], 1.0, chosen[g])
            cand[g] = jnp.where(ohs[g], ninf, cand[g])
    wts = [jnp.where(chosen[g] > 0.0, scores[g * GROUP_SIZE:(g + 1) * GROUP_SIZE], 0.0)
           for g in range(N_ROUTE_GROUPS)]
    wsum = jnp.sum(wts[0], axis=0, keepdims=True)
    for g in range(1, N_ROUTE_GROUPS):
        wsum = wsum + jnp.sum(wts[g], axis=0, keepdims=True)
    gates_t = jnp.concatenate([w / wsum * ROUTED_SCALE for w in wts], axis=0)
    gt_ref[0] = jnp.concatenate([gates_t, jnp.zeros((LANES - N_EXPERTS, tile), F32)], axis=0).T
    g_hi = gates_t.astype(BF16).astype(F32)
    xg_ref[0, :, D_MODEL:D_MODEL + LANES] = jnp.concatenate([g_hi, gates_t - g_hi], axis=0).T.astype(BF16)

    er = lax.broadcasted_iota(jnp.int32, (N_EXPERTS, N_EXPERTS), 0)
    ec = lax.broadcasted_iota(jnp.int32, (N_EXPERTS, N_EXPERTS), 1)
    lower = jnp.where(ec < er, 1.0, 0.0).astype(BF16)
    tr = lax.broadcasted_iota(jnp.int32, (SUB, SUB), 0)
    tc = lax.broadcasted_iota(jnp.int32, (SUB, SUB), 1)
    upper = jnp.where(tr < tc, 1.0, 0.0).astype(BF16)
    for u in range(tile // SUB):
        cols = slice(u * SUB, (u + 1) * SUB)
        ch = jnp.concatenate([c[:, cols] for c in chosen], axis=0)
        cnt = jnp.sum(ch, axis=1, keepdims=True)
        cpad = jnp.broadcast_to(jnp.ceil(cnt / CHUNK) * CHUNK, (N_EXPERTS, LANES))
        cnt_ref[u] = cpad
        seg0 = _dot(lower, cpad.astype(BF16))[:, 0:1]
        slot = seg0 + _dot(ch.astype(BF16), upper)
        rows = []
        for k in range(TOP_K):
            acc = jnp.where(picks[k][0][:, cols], slot[0:GROUP_SIZE], 0.0)
            for g in range(1, N_ROUTE_GROUPS):
                acc = acc + jnp.where(picks[k][g][:, cols], slot[g * GROUP_SIZE:(g + 1) * GROUP_SIZE], 0.0)
            rows.append(jnp.sum(acc, axis=0, keepdims=True))
        posr = jnp.concatenate(rows, axis=0)
        posr_ref[u] = posr
        post_ref[0, cols, :] = jnp.concatenate([posr, jnp.zeros((LANES - TOP_K, SUB), F32)], axis=0).T


def _oproj(x, o_list, l_list, co, g1, sc, sh, ng, wo_bf, wr_t, br, *, tile, dils):
    B, S, _ = x.shape
    tm = g1.shape[1]
    mod_spec = pl.BlockSpec((1, tm, D_MODEL), (lambda b, j: (b, j, 0)) if tm > 1 else (lambda b, j: (b, 0, 0)))
    const2 = lambda shape: pl.BlockSpec(shape, lambda b, j: (0, 0))
    nat = lambda w: pl.BlockSpec((1, tile, w), lambda b, j: (b, j, 0))
    dspec = [pl.BlockSpec((1, d, tile // d, GROUP_W), lambda b, j: (b, 0, j, 0)) for d in dils]
    n_slabs = sum(2 * (GROUP_W // LANES) for d in dils if d > 1)
    nsub_t = tile // SUB
    return pl.pallas_call(
        functools.partial(_oproj_kernel, tile=tile, dils=dils),
        grid=(B, S // tile),
        in_specs=[nat(D_MODEL)] + dspec + dspec + [nat(CONV_CH), mod_spec, mod_spec, mod_spec,
                  const2((1, D_MODEL)), const2((D_MODEL, D_MODEL)), const2((N_EXPERTS, D_MODEL)),
                  const2((N_EXPERTS, 1))],
        out_specs=[nat(D_MODEL), nat(XG_W), nat(LANES),
                   pl.BlockSpec((nsub_t, TOP_K, SUB), lambda b, j: (b * (S // tile) + j, 0, 0)),
                   nat(LANES),
                   pl.BlockSpec((nsub_t, N_EXPERTS, LANES), lambda b, j: (b * (S // tile) + j, 0, 0))],
        out_shape=[jax.ShapeDtypeStruct((B, S, D_MODEL), F32), jax.ShapeDtypeStruct((B, S, XG_W), BF16),
                   jax.ShapeDtypeStruct((B, S, LANES), F32),
                   jax.ShapeDtypeStruct((B * S // SUB, TOP_K, SUB), F32),
                   jax.ShapeDtypeStruct((B, S, LANES), F32),
                   jax.ShapeDtypeStruct((B * S // SUB, N_EXPERTS, LANES), F32)],
        scratch_shapes=[pltpu.VMEM((max(n_slabs, 1), tile, LANES), F32)],
        compiler_params=_params(("arbitrary", "arbitrary")),
        name="oproj",
    )(x, *o_list, *l_list, co, g1, sc, sh, ng, wo_bf, wr_t, br)


def _moe_kernel(h_ref, gt_ref, x1_ref, g2_ref, wg_ref, wu_ref, wd_ref, wgs_ref, wus_ref, wds_ref,
                y_ref, acc, *, tile):
    e = pl.program_id(2)
    h = h_ref[0]

    @pl.when(e == 0)
    def _():
        a = _dot(h, wgs_ref[...].astype(BF16))
        b = _dot(h, wus_ref[...].astype(BF16))
        acc[...] = _dot((_silu(a) * b).astype(BF16), wds_ref[...].astype(BF16))

    a = _dot(h, wg_ref[0].astype(BF16))
    b = _dot(h, wu_ref[0].astype(BF16))
    lane = lax.broadcasted_iota(jnp.int32, (tile, LANES), 1)
    gcol = jnp.sum(jnp.where(lane == e, gt_ref[0], 0.0), axis=1, keepdims=True)
    hc = _silu(a) * b * gcol
    acc[...] += _dot(hc.astype(BF16), wd_ref[0].astype(BF16))

    @pl.when(e == N_EXPERTS - 1)
    def _():
        y_ref[0] = x1_ref[0] + g2_ref[0] * acc[...]


def _moe(h2, gates, x1, g2, wg, wu, wd, wgs, wus, wds, *, tile):
    B, S, _ = h2.shape
    tm = g2.shape[1]
    mod_spec = pl.BlockSpec((1, tm, D_MODEL),
                            (lambda b, j, e: (b, j, 0)) if tm > 1 else (lambda b, j, e: (b, 0, 0)))
    nat = lambda w: pl.BlockSpec((1, tile, w), lambda b, j, e: (b, j, 0))
    const2 = lambda shape: pl.BlockSpec(shape, lambda b, j, e: (0, 0))
    return pl.pallas_call(
        functools.partial(_moe_kernel, tile=tile),
        grid=(B, S // tile, N_EXPERTS),
        in_specs=[nat(D_MODEL), nat(LANES), nat(D_MODEL), mod_spec,
                  pl.BlockSpec((1, D_MODEL, D_EXPERT), lambda b, j, e: (e, 0, 0)),
                  pl.BlockSpec((1, D_MODEL, D_EXPERT), lambda b, j, e: (e, 0, 0)),
                  pl.BlockSpec((1, D_EXPERT, D_MODEL), lambda b, j, e: (e, 0, 0)),
                  const2((D_MODEL, D_EXPERT)), const2((D_MODEL, D_EXPERT)), const2((D_EXPERT, D_MODEL))],
        out_specs=nat(D_MODEL),
        out_shape=jax.ShapeDtypeStruct((B, S, D_MODEL), F32),
        scratch_shapes=[pltpu.VMEM((tile, D_MODEL), F32)],
        compiler_params=_params(("arbitrary", "arbitrary", "arbitrary")),
        name="moe",
    )(h2, gates, x1, g2, wg, wu, wd, wgs, wus, wds)


FFN_TM = 1024
SUB_ROWS = SUB * TOP_K + N_EXPERTS * CHUNK
N_CHUNKS = SUB_ROWS // CHUNK
ZERO_CHUNK = N_CHUNKS
TABLE_W = 256
MXU_ROWS = 256
BLOCKS_PER_TRIP = 2


def _slot_onehot(first, pos_list, axis_iota):
    shape = axis_iota.shape
    hit = axis_iota == jnp.broadcast_to(pos_list[0] - first, shape).astype(BF16)
    for p in pos_list[1:]:
        hit = hit | (axis_iota == jnp.broadcast_to(p - first, shape).astype(BF16))
    return jnp.where(hit, jnp.ones(shape, BF16), jnp.zeros(shape, BF16))


def _chunk_copy(src, src_chunk, dst, dst_chunk, sem):
    return pltpu.make_async_copy(src.at[src_chunk], dst.at[dst_chunk], sem)


def _for_each(n, body):
    def two(t, carry):
        body(2 * t)
        body(2 * t + 1)
        return carry

    lax.fori_loop(0, n // 2, two, 0)

    @pl.when(n % 2 == 1)
    def _():
        body(n - 1)


def _dispatch_kernel(nblk_ref, nreal_ref, ncopy_ref, ntile_ref, src_ref, dst_ref, posr_ref, xg_ref, xs_hbm,
                     xs_scr, zero_scr, sem, tail_sem, *, nsub, n_tiles_max):
    s = pl.program_id(0)

    tile_chunks = FFN_TM // CHUNK
    blk_chunks = MXU_ROWS // CHUNK
    zero_scr[...] = jnp.zeros((tile_chunks, CHUNK, XG_W), BF16)
    tail_blocks = [ntile_ref[0] + s + r * nsub for r in range(-(-(n_tiles_max) // nsub))]

    def tail_copy(t):
        return pltpu.make_async_copy(
            zero_scr, xs_hbm.at[pl.ds(pl.multiple_of(t * tile_chunks, tile_chunks), tile_chunks)], tail_sem)

    for t in tail_blocks:
        @pl.when(t < n_tiles_max)
        def _():
            tail_copy(t).start()

    xg = xg_ref[...]
    pos = [posr_ref[0, k:k + 1, :] for k in range(TOP_K)]
    rows = lax.broadcasted_iota(jnp.int32, (MXU_ROWS, SUB), 0).astype(BF16)
    xs_scr[ZERO_CHUNK] = jnp.zeros((CHUNK, XG_W), BF16)

    def sort_blocks(i, carry):
        for j in range(BLOCKS_PER_TRIP):
            blk = i * BLOCKS_PER_TRIP + j
            onehot = _slot_onehot((blk * MXU_ROWS).astype(F32), pos, rows)
            sorted_rows = _dot(onehot, xg).astype(BF16)
            xs_scr[pl.ds(pl.multiple_of(blk * blk_chunks, blk_chunks), blk_chunks)] = (
                sorted_rows.reshape(blk_chunks, CHUNK, XG_W))
        k0 = i * trip_chunks
        _for_each(jnp.clip(n_real - k0, 0, trip_chunks), lambda t: send(k0 + t))
        return carry

    n_real = nreal_ref[s]
    n = ncopy_ref[s]
    trip_chunks = BLOCKS_PER_TRIP * blk_chunks

    def send(i):
        _chunk_copy(xs_scr, src_ref[0, 0, i], xs_hbm, dst_ref[0, 0, i], sem).start()

    lax.fori_loop(0, nblk_ref[s], sort_blocks, 0)
    _for_each(n - n_real, lambda t: send(n_real + t))
    _for_each(n, lambda i: _chunk_copy(xs_scr, 0, xs_hbm, 0, sem).wait())
    for t in tail_blocks:
        @pl.when(t < n_tiles_max)
        def _():
            tail_copy(t).wait()


def _dispatch(nblk, n_real, ncopy, n_tiles, src, dst, posr, xg, n_rows):
    nsub = posr.shape[0]
    tab = pl.BlockSpec((1, 1, TABLE_W), lambda s, *_: (s, 0, 0), memory_space=pltpu.SMEM)
    return pl.pallas_call(
        functools.partial(_dispatch_kernel, nsub=nsub, n_tiles_max=n_rows // FFN_TM),
        grid_spec=pltpu.PrefetchScalarGridSpec(
            num_scalar_prefetch=4, grid=(nsub,),
            in_specs=[tab, tab, pl.BlockSpec((1, TOP_K, SUB), lambda s, *_: (s, 0, 0)),
                      pl.BlockSpec((SUB, XG_W), lambda s, *_: (s, 0))],
            out_specs=pl.BlockSpec(memory_space=pl.ANY),
            scratch_shapes=[pltpu.VMEM((N_CHUNKS + 1, CHUNK, XG_W), BF16),
                            pltpu.VMEM((FFN_TM // CHUNK, CHUNK, XG_W), BF16),
                            pltpu.SemaphoreType.DMA(()), pltpu.SemaphoreType.DMA(())]),
        out_shape=jax.ShapeDtypeStruct((n_rows // CHUNK, CHUNK, XG_W), BF16),
        compiler_params=_params(("arbitrary",)),
        name="moe_dispatch",
    )(nblk, n_real, ncopy, n_tiles, src, dst, posr, xg).reshape(n_rows, XG_W)


def _ffn_kernel(te_ref, nt_ref, xs_ref, wg_ref, wu_ref, wd_ref,
                q_ref, kc_ref, vc_ref, nk_ref, nv_ref, b1_ref, b2_ref,
                ys_ref, ko_ref, vo_ref, o_ref, l_ref, wg_s, wu_s, wd_s, *, steps_per_batch):
    i = pl.program_id(0)
    n_heads = HEADS_PER_GROUP // steps_per_batch
    first_head = (i % steps_per_batch) * n_heads

    def ride():
        _attn_sample_batch(i // steps_per_batch, 0, q_ref, kc_ref, vc_ref, nk_ref, nv_ref, b1_ref, b2_ref,
                           ko_ref, vo_ref, o_ref, l_ref, heads=[first_head + j for j in range(n_heads)])

    @pl.when(i >= nt_ref[0])
    def _():
        ride()
        ys_ref[...] = jnp.zeros((FFN_TM, D_MODEL), BF16)

    @pl.when(i < nt_ref[0])
    def _():
        e = te_ref[i]

        @pl.when((i == 0) | (e != te_ref[jnp.maximum(i - 1, 0)]))
        def _():
            wg_s[...] = wg_ref[0].astype(BF16)
            wu_s[...] = wu_ref[0].astype(BF16)
            wd_s[...] = wd_ref[0].astype(BF16)

        ride()
        x = xs_ref[:, 0:D_MODEL]
        g = xs_ref[:, D_MODEL:XG_W].astype(F32)
        lane = lax.broadcasted_iota(jnp.int32, (FFN_TM, LANES), 1)
        gcol = jnp.sum(jnp.where((lane == e) | (lane == e + N_EXPERTS), g, 0.0), axis=1, keepdims=True)
        a = _dot(x, wg_s[...])
        b = _dot(x, wu_s[...])
        hc = _silu(a) * b * gcol
        ys_ref[...] = _dot(hc.astype(BF16), wd_s[...]).astype(BF16)


def _ffn(tile_expert, n_tiles, xs, wg, wu, wd, ride):
    q, kc, vc, nk, nv, b1, b2 = ride
    nb, _, _, wb = kc.shape
    n_rows = xs.shape[0]
    n_steps = n_rows // FFN_TM
    assert n_steps % nb == 0 and HEADS_PER_GROUP % (n_steps // nb) == 0
    per = n_steps // nb
    row = lambda i, te, nt: (jnp.minimum(i, nt[0] - 1), 0)
    wspec = lambda shape: pl.BlockSpec((1,) + shape, lambda i, te, nt: (te[i], 0, 0))
    rblk = lambda *tail: pl.BlockSpec((1, HEADS_PER_GROUP) + tail, lambda i, te, nt: (i // per, 0, 0, 0))
    const = lambda shape: pl.BlockSpec(shape, lambda i, te, nt: (0, 0, 0))
    return pl.pallas_call(
        functools.partial(_ffn_kernel, steps_per_batch=per),
        grid_spec=pltpu.PrefetchScalarGridSpec(
            num_scalar_prefetch=2, grid=(n_steps,),
            in_specs=[pl.BlockSpec((FFN_TM, XG_W), row), wspec((D_MODEL, D_EXPERT)),
                      wspec((D_MODEL, D_EXPERT)), wspec((D_EXPERT, D_MODEL)),
                      rblk(8, HEAD_DIM), rblk(HEAD_DIM, wb), rblk(HEAD_DIM, wb), const(nk.shape), const(nv.shape),
                      const(b1.shape), const(b2.shape)],
            out_specs=[pl.BlockSpec((FFN_TM, D_MODEL), lambda i, te, nt: (i, 0)),
                       rblk(HEAD_DIM, wb), rblk(HEAD_DIM, wb), rblk(8, HEAD_DIM), rblk(8, LANES)],
            scratch_shapes=[pltpu.VMEM((D_MODEL, D_EXPERT), BF16), pltpu.VMEM((D_MODEL, D_EXPERT), BF16),
                            pltpu.VMEM((D_EXPERT, D_MODEL), BF16)]),
        out_shape=[jax.ShapeDtypeStruct((n_rows, D_MODEL), BF16),
                   jax.ShapeDtypeStruct(kc.shape, F32), jax.ShapeDtypeStruct(kc.shape, F32),
                   jax.ShapeDtypeStruct((nb, HEADS_PER_GROUP, 8, HEAD_DIM), F32),
                   jax.ShapeDtypeStruct((nb, HEADS_PER_GROUP, 8, LANES), F32)],
        compiler_params=_params(("arbitrary",)),
        name="moe_ffn",
    )(tile_expert, n_tiles, xs, wg, wu, wd, q, kc, vc, nk, nv, b1, b2)


def _combine_kernel(nblk_ref, ncopy_ref, src_ref, post_ref, xg_ref, x1_ref, g2_ref, wgs_ref, wus_ref,
                    wds_ref, ys_hbm, y_ref, ys_scr, wgs_s, wus_s, wds_s, sem):
    s = pl.program_id(0)
    n = ncopy_ref[s]
    blk_chunks = MXU_ROWS // CHUNK
    trip_chunks = BLOCKS_PER_TRIP * blk_chunks

    def fetch_trip(t, carry):
        k0 = t * trip_chunks
        _for_each(trip_chunks,
                  lambda u: _chunk_copy(ys_hbm, src_ref[0, 0, k0 + u], ys_scr, k0 + u, sem.at[t]).start())
        return carry

    lax.fori_loop(0, nblk_ref[s], fetch_trip, 0)

    @pl.when(s == 0)
    def _():
        wgs_s[...] = wgs_ref[...].astype(BF16)
        wus_s[...] = wus_ref[...].astype(BF16)
        wds_s[...] = wds_ref[...].astype(BF16)

    h = xg_ref[:, 0:D_MODEL]
    a = _dot(h, wgs_s[...])
    b = _dot(h, wus_s[...])
    shared = _dot((_silu(a) * b).astype(BF16), wds_s[...])

    post = post_ref[...]
    pos = [jnp.broadcast_to(post[:, k:k + 1], (SUB, MXU_ROWS)) for k in range(TOP_K)]
    lanes = lax.broadcasted_iota(jnp.int32, (SUB, MXU_ROWS), 1).astype(BF16)

    def gather_blocks(i, acc):
        _for_each(trip_chunks, lambda t: _chunk_copy(ys_hbm, 0, ys_scr, 0, sem.at[i]).wait())
        for j in range(BLOCKS_PER_TRIP):
            blk = i * BLOCKS_PER_TRIP + j
            onehot = _slot_onehot((blk * MXU_ROWS).astype(F32), pos, lanes)
            rows = ys_scr[pl.ds(pl.multiple_of(blk * blk_chunks, blk_chunks), blk_chunks)]
            acc = acc + _dot(onehot, rows.reshape(MXU_ROWS, D_MODEL))
        return acc

    routed = lax.fori_loop(0, nblk_ref[s], gather_blocks, jnp.zeros((SUB, D_MODEL), F32))
    y_ref[...] = x1_ref[...] + g2_ref[0] * (routed + shared)


def _combine(nblk, ncopy, src, post, xg, x1, g2, wgs, wus, wds, ys):
    n_tok = x1.shape[0]
    nsub = n_tok // SUB
    per_b = nsub // g2.shape[0]
    tab = pl.BlockSpec((1, 1, TABLE_W), lambda s, *_: (s, 0, 0), memory_space=pltpu.SMEM)
    const2 = lambda shape: pl.BlockSpec(shape, lambda s, *_: (0, 0))
    return pl.pallas_call(
        _combine_kernel,
        grid_spec=pltpu.PrefetchScalarGridSpec(
            num_scalar_prefetch=2, grid=(nsub,),
            in_specs=[tab, pl.BlockSpec((SUB, LANES), lambda s, *_: (s, 0)),
                      pl.BlockSpec((SUB, XG_W), lambda s, *_: (s, 0)),
                      pl.BlockSpec((SUB, D_MODEL), lambda s, *_: (s, 0)),
                      pl.BlockSpec((1, 1, D_MODEL), lambda s, *_: (s // per_b, 0, 0)),
                      const2((D_MODEL, D_EXPERT)), const2((D_MODEL, D_EXPERT)), const2((D_EXPERT, D_MODEL)),
                      pl.BlockSpec(memory_space=pl.ANY)],
            out_specs=pl.BlockSpec((SUB, D_MODEL), lambda s, *_: (s, 0)),
            scratch_shapes=[pltpu.VMEM((N_CHUNKS, CHUNK, D_MODEL), BF16), pltpu.VMEM((D_MODEL, D_EXPERT), BF16),
                            pltpu.VMEM((D_MODEL, D_EXPERT), BF16), pltpu.VMEM((D_EXPERT, D_MODEL), BF16),
                            pltpu.SemaphoreType.DMA((SUB_ROWS // (BLOCKS_PER_TRIP * MXU_ROWS),))]),
        out_shape=jax.ShapeDtypeStruct((n_tok, D_MODEL), F32),
        compiler_params=_params(("arbitrary",)),
        name="moe_combine",
    )(nblk, ncopy, src, post, xg, x1, g2, wgs, wus, wds, ys.reshape(-1, CHUNK, D_MODEL))


def _route_tables(cnt):
    nsub = cnt.shape[0]
    i32 = jnp.int32
    seg_end = jnp.cumsum(cnt, axis=1)
    seg_start = seg_end - cnt
    tot = jnp.sum(cnt, axis=0)
    tot_al = (tot + FFN_TM - 1) // FFN_TM * FFN_TM
    reg_end = jnp.cumsum(tot_al)
    reg_start = reg_end - tot_al
    base = reg_start[None, :] + jnp.cumsum(cnt, axis=0) - cnt
    n_real = seg_end[:, -1] // CHUNK
    k = jnp.arange(TABLE_W, dtype=i32)
    owner = jnp.sum((seg_end[:, None, :] // CHUNK) <= k[None, :, None], axis=2)
    owner = jnp.minimum(owner, N_EXPERTS - 1)
    delta = (base - seg_start) // CHUNK
    e_ids = jnp.arange(N_EXPERTS, dtype=i32)
    real_dst = jnp.sum(jnp.where(owner[:, :, None] == e_ids, delta[:, None, :], 0), axis=2) + k[None, :]
    fill_n = (tot_al - tot) // CHUNK
    fill_dst0 = (reg_start + tot) // CHUNK
    src = jnp.broadcast_to(k[None, :], (nsub, TABLE_W))
    dst = real_dst
    ncopy = n_real
    for r in range(-(-N_EXPERTS // nsub)):
        e_of = jnp.arange(nsub, dtype=i32) + r * nsub
        ok = e_of < N_EXPERTS
        e_cl = jnp.minimum(e_of, N_EXPERTS - 1)
        fn = jnp.where(ok, fill_n[e_cl], 0)
        j = k[None, :] - ncopy[:, None]
        is_fill = (j >= 0) & (j < fn[:, None])
        src = jnp.where(is_fill, ZERO_CHUNK, src)
        dst = jnp.where(is_fill, fill_dst0[e_cl][:, None] + j, dst)
        ncopy = ncopy + fn
    trip_rows = BLOCKS_PER_TRIP * MXU_ROWS
    disp_nblk = (n_real * CHUNK + trip_rows - 1) // trip_rows
    comb_n = disp_nblk * (trip_rows // CHUNK)
    comb_src = jnp.where(k[None, :] < n_real[:, None], real_dst, real_dst[:, 0:1])
    n_tiles = reg_end[-1] // FFN_TM
    t = jnp.arange((nsub * SUB_ROWS + N_EXPERTS * FFN_TM) // FFN_TM, dtype=i32)
    tile_expert = jnp.minimum(jnp.sum((reg_end[None, :] // FFN_TM) <= t[:, None], axis=1), N_EXPERTS - 1)
    as3 = lambda a: a.astype(i32).reshape(nsub, 1, TABLE_W)
    return (disp_nblk.astype(i32), n_real.astype(i32), ncopy.astype(i32), as3(src), as3(dst),
            comb_n.astype(i32), as3(comb_src), tile_expert.astype(i32), n_tiles.astype(i32).reshape(1))


def _moe_sparse(xg, post, posr, cnt, x1, g2, wg, wu, wd, wgs, wus, wds, ride):
    B, S, _ = x1.shape
    n_tok = B * S
    nsub = n_tok // SUB
    (disp_nblk, n_real, disp_n, disp_src, disp_dst, comb_n, comb_src, tile_expert,
     n_tiles) = _route_tables(cnt[:, :, 0].astype(jnp.int32))
    n_rows = nsub * SUB_ROWS + N_EXPERTS * FFN_TM
    xg2 = xg.reshape(n_tok, XG_W)
    xs = _dispatch(disp_nblk, n_real, disp_n, n_tiles, disp_src, disp_dst, posr, xg2, n_rows)
    ys, *ride_out = _ffn(tile_expert, n_tiles, xs, wg, wu, wd, ride)
    y = _combine(disp_nblk, comb_n, comb_src, post.reshape(n_tok, LANES), xg2, x1.reshape(n_tok, D_MODEL),
                 g2, wgs, wus, wds, ys)
    return y.reshape(B, S, D_MODEL), ride_out


def _t5_bucket(dist):
    max_exact = N_BUCKETS // 2
    df = jnp.maximum(dist, 1).astype(F32)
    large = max_exact + (jnp.log(df / max_exact) / math.log(MAX_DISTANCE / max_exact)
                         * (N_BUCKETS - max_exact)).astype(jnp.int32)
    large = jnp.minimum(large, N_BUCKETS - 1)
    return jnp.where(dist < max_exact, dist, large)


def _step_bias(rel_bias, g):
    dist = jnp.arange(N_KEYS, dtype=jnp.int32) * DILATIONS[g]
    cols = rel_bias[:, g * HEADS_PER_GROUP:(g + 1) * HEADS_PER_GROUP]
    return cols[_t5_bucket(dist)].T.astype(F32)


def _prompt_bias(bias_k):
    h = bias_k.shape[0]
    n = 3 * BLK
    row = jnp.concatenate([bias_k[:, ::-1], jnp.full((h, n - N_KEYS), NEG, F32)], axis=1)
    t = jnp.tile(row, (1, BLK))[:, :BLK * (n - 1)].reshape(h, BLK, n - 1)
    return t[:, :, :2 * BLK]


def _sample_bias(bias_k, wb, d):
    h = bias_k.shape[0]
    rev = bias_k[:, :0:-1]
    rows = []
    for t in range(4):
        if d == 1:
            rows.append(jnp.concatenate([jnp.full((h, t), NEG, F32), rev[:, :wb - t]], axis=1))
        else:
            r = jnp.arange(d)[None, None, :]
            rows.append(jnp.where(r == t, rev[:, :, None], NEG).reshape(h, wb))
    b1 = jnp.concatenate([jnp.stack(rows, axis=1), jnp.zeros((h, 4, wb), F32)], axis=1)
    rows = []
    for t in range(4):
        cols = [bias_k[:, (t - tn) // d] if (t >= tn and (t - tn) % d == 0) else jnp.full((h,), NEG, F32)
                for tn in range(4)]
        rows.append(jnp.concatenate([jnp.full((h, LANES - 4), NEG, F32), jnp.stack(cols, axis=1)], axis=1))
    b2 = jnp.concatenate([jnp.stack(rows, axis=1), jnp.zeros((h, 4, LANES), F32)], axis=1)
    return b1, b2


TILE = 512
RIDE_GROUP = N_GROUPS - 1


def _sample_front(x, mods, hist, caches, rel_bias, weights):
    norm1_g, _, w_in_bf, qg, kg, conv_w, bd = weights[:7]
    sh1, sc1 = mods[:2]
    S = x.shape[1]
    nb = S // 4
    qf, kt, vt, u, co = _proj(x, sc1, sh1, norm1_g, w_in_bf, qg, kg, conv_w, bd, hist,
                              tile=TILE, dils=(1, 1, 1), sample=True)
    done, ride = {}, None
    for g in range(N_GROUPS):
        d, wb = DILATIONS[g], WINDOWS[g]
        kc, vc = caches[g]
        b1, b2 = _sample_bias(_step_bias(rel_bias, g), wb, d)
        q = qf[0, :, g * GROUP_W:(g + 1) * GROUP_W].reshape(nb, 4, HEADS_PER_GROUP, HEAD_DIM)
        q = jnp.pad(q.transpose(0, 2, 1, 3), ((0, 0), (0, 0), (0, 4), (0, 0))).astype(BF16)
        new_t = lambda a: a[g * GROUP_W:(g + 1) * GROUP_W].reshape(HEADS_PER_GROUP, HEAD_DIM, S)
        operands = (q, kc, vc, new_t(kt), new_t(vt), b1, b2)
        if g == RIDE_GROUP:
            ride = operands
        else:
            done[g] = _attn_sample(*operands, bb={128: 8, 512: 8}[wb])
    return dict(x=x, mods=mods, u=u, co=co, done=done), ride


def _sample_back(front, ride_out, weights):
    (_, norm2_g, _, _, _, _, _, w_o_bf, w_r_t, b_r, w_gate_e, w_up_e, w_down_e,
     w_gate_s, w_up_s, w_down_s) = weights
    x, u, co = front["x"], front["u"], front["co"]
    _, _, g1, sh2, sc2, g2 = front["mods"]
    S = x.shape[1]
    nb = S // 4
    front["done"][RIDE_GROUP] = ride_out
    states, o_list, l_list = [], [], []
    for g in range(N_GROUPS):
        ko, vo, o, lse = front["done"][g]
        states += [ko.transpose(0, 3, 1, 2)[None], vo.transpose(0, 3, 1, 2)[None]]
        o_list.append(o[:, :, :4].transpose(0, 2, 1, 3).reshape(1, 1, S, GROUP_W))
        lse = jnp.broadcast_to(lse[:, :, :4, :1], (nb, HEADS_PER_GROUP, 4, HEAD_DIM))
        l_list.append(lse.transpose(0, 2, 1, 3).reshape(1, 1, S, GROUP_W))
    states.append(u.reshape(nb, 4, CONV_CH)[:, 2:][None])
    x1, xg, gates, _, _, _ = _oproj(x, o_list, l_list, co, g1, sc2, sh2, norm2_g, w_o_bf, w_r_t, b_r,
                                    tile=TILE, dils=(1, 1, 1))
    y = _moe(xg, gates, x1, g2, w_gate_e, w_up_e, w_down_e, w_gate_s, w_up_s, w_down_s, tile=TILE)
    return y, states


def _prompt_layer(x, mods, rel_bias, weights, ride):
    (norm1_g, norm2_g, w_in_bf, qg, kg, conv_w, bd, w_o_bf, w_r_t, b_r, w_gate_e, w_up_e, w_down_e,
     w_gate_s, w_up_s, w_down_s) = weights
    sh1, sc1, g1, sh2, sc2, g2 = mods
    B, S, _ = x.shape
    res = _proj(x, sc1, sh1, norm1_g, w_in_bf, qg, kg, conv_w, bd, None, tile=TILE, dils=DILATIONS, sample=False)
    qkv, tails, (u, co) = res[:9], res[9:15], res[15:]
    states, o_list, l_list = [], [], []
    for g in range(N_GROUPS):
        d, w = DILATIONS[g], WINDOWS[g]
        q, k, v = (a.reshape(B * d, S // d, GROUP_W) for a in qkv[3 * g:3 * g + 3])
        o, lse = _attn_prompt(q, k, v, _prompt_bias(_step_bias(rel_bias, g)))
        o_list.append(o.reshape(B, d, S // d, GROUP_W))
        l_list.append(lse.reshape(B, d, S // d, GROUP_W))
        keep = lambda a: a.reshape(B, HEADS_PER_GROUP, HEAD_DIM, w).transpose(0, 3, 1, 2)[None]
        states += [keep(tails[2 * g]), keep(tails[2 * g + 1])]
    states.append(u[:, 6:][None])
    x1, xg, _, posr, post, cnt = _oproj(x, o_list, l_list, co, g1, sc2, sh2, norm2_g, w_o_bf, w_r_t, b_r,
                                        tile=TILE, dils=DILATIONS)
    y, ride_out = _moe_sparse(xg, post, posr, cnt, x1, g2, w_gate_e, w_up_e, w_down_e, w_gate_s, w_up_s,
                              w_down_s, ride)
    return y, states, ride_out


def kernel(x_prompt, x_sample, c_prompt, c_sample, cache_k_w128, cache_v_w128, cache_k_w512, cache_v_w512, cache_k_w2048, cache_v_w2048, state_conv, rel_bias, norm1_g, norm2_g, w_ada, b_ada, w_in, q_norm_g, k_norm_g, conv_w, w_o, w_router, b_router, w_gate_e, w_up_e, w_down_e, w_gate_s, w_up_s, w_down_s):
    B = x_prompt.shape[0]
    DB, T = x_sample.shape[:2]
    n_c = B + DB
    c_all = jnp.pad(jnp.concatenate([c_prompt, c_sample], axis=0), ((0, (-n_c) % 8), (0, 0)))
    mod = _ada(c_all, w_ada[0], b_ada)
    chunks = [mod[:, i * D_MODEL:(i + 1) * D_MODEL] for i in range(6)]
    mods_p = [c[:B].reshape(B, 1, D_MODEL) for c in chunks]
    mods_s = [jnp.repeat(c[B:n_c], T, axis=0).reshape(1, DB * T, D_MODEL) for c in chunks]

    eye = jnp.arange(GROUP_W) // HEAD_DIM
    bd = jnp.where(eye[:, None] == eye[None, :], 1.0 / HEAD_DIM, 0.0).astype(BF16)
    weights = (norm1_g, norm2_g, w_in[0].astype(BF16),
               jnp.tile(q_norm_g, (1, HEADS_PER_GROUP)), jnp.tile(k_norm_g, (1, HEADS_PER_GROUP)),
               conv_w[0], bd, w_o[0].astype(BF16), w_router[0].T, b_router.reshape(N_EXPERTS, 1),
               w_gate_e[0], w_up_e[0], w_down_e[0], w_gate_s[0], w_up_s[0], w_down_s[0])

    s0, s1 = state_conv[0, :, 0], state_conv[0, :, 1]
    zero = jnp.zeros_like(s0)
    hist_a = jnp.stack([s0, s1, zero, zero], axis=1).reshape(DB * T, CONV_CH)
    hist_b = jnp.stack([s1, zero, zero, zero], axis=1).reshape(DB * T, CONV_CH)
    caches = [(ck[0].transpose(0, 2, 3, 1), cv[0].transpose(0, 2, 3, 1))
              for ck, cv in ((cache_k_w128, cache_v_w128), (cache_k_w512, cache_v_w512),
                             (cache_k_w2048, cache_v_w2048))]
    front, ride = _sample_front(x_sample.reshape(1, DB * T, D_MODEL), mods_s, (hist_a, hist_b), caches,
                                rel_bias, weights)
    yp, st_p, ride_out = _prompt_layer(x_prompt, mods_p, rel_bias, weights, ride)
    ys, st_s = _sample_back(front, ride_out, weights)
    return (yp, ys.reshape(DB, T, D_MODEL), *st_p, *st_s)
```

```python
import functools
import math

import jax
import jax.numpy as jnp
from jax import lax
from jax.experimental import pallas as pl
from jax.experimental.pallas import tpu as pltpu

D_MODEL = 1024
HEAD_DIM = 64
HEADS_PER_GROUP = 4
GROUP_W = HEADS_PER_GROUP * HEAD_DIM
WINDOWS = (128, 512, 2048)
DILATIONS = (1, 4, 16)
N_GROUPS = 3
ATTN_W = N_GROUPS * GROUP_W
CONV_CH = 256
N_PROJ = 3 * ATTN_W + 3 * CONV_CH
N_STEPS = 128
N_KEYS = N_STEPS + 1
BLK = 128
N_BUCKETS = 32
MAX_DISTANCE = WINDOWS[-1]
N_EXPERTS = 64
TOP_K = 8
N_ROUTE_GROUPS = 8
GROUP_SIZE = N_EXPERTS // N_ROUTE_GROUPS
TOPK_GROUPS = 4
D_EXPERT = 256
ROUTED_SCALE = 2.5
RMS_EPS = 1e-6
NEG = -1e30
LANES = 128
SUB = 256
CHUNK = 16
XG_W = D_MODEL + LANES
VMEM_LIMIT = 56 * 1024 * 1024

F32 = jnp.float32
BF16 = jnp.bfloat16


def _dot(a, b):
    return jnp.dot(a, b, preferred_element_type=F32)


def _dot_nt(a, b):
    return lax.dot_general(a, b, (((1,), (1,)), ((), ())), preferred_element_type=F32)


def _silu(a):
    return a / (1.0 + jnp.exp(-a))


def _params(sem):
    return pltpu.CompilerParams(dimension_semantics=sem, vmem_limit_bytes=VMEM_LIMIT)


def _ada_kernel(c_ref, w_ref, b_ref, o_ref):
    s = _silu(c_ref[...])
    o_ref[...] = _dot(s.astype(BF16), w_ref[...].astype(BF16)) + b_ref[...]


def _ada(c, w_ada, b_ada):
    n = c.shape[0]
    nc = w_ada.shape[1] // D_MODEL
    return pl.pallas_call(
        _ada_kernel,
        grid=(nc,),
        in_specs=[pl.BlockSpec((n, D_MODEL), lambda j: (0, 0)),
                  pl.BlockSpec((D_MODEL, D_MODEL), lambda j: (0, j)),
                  pl.BlockSpec((1, D_MODEL), lambda j: (0, j))],
        out_specs=pl.BlockSpec((n, D_MODEL), lambda j: (0, j)),
        out_shape=jax.ShapeDtypeStruct((n, w_ada.shape[1]), F32),
        compiler_params=_params(("arbitrary",)),
        name="ada",
    )(c, w_ada, b_ada)


def _proj_kernel(*refs, tile, dils, sample, nt):
    if sample:
        (x_ref, sc_ref, sh_ref, ng_ref, w_ref, qg_ref, kg_ref, cw_ref, bd_ref, ha_ref, hb_ref,
         qf_ref, kt_ref, vt_ref, u_ref, co_ref, u_scr) = refs
    else:
        (x_ref, sc_ref, sh_ref, ng_ref, w_ref, qg_ref, kg_ref, cw_ref, bd_ref) = refs[:9]
        ride_in, refs = refs[9:9 + N_RIDE_IN], refs[9 + N_RIDE_IN:]
        (q0, k0, v0, q1, k1, v1, q2, k2, v2, kt0, vt0, kt1, vt1, kt2, vt2, u_ref, co_ref) = refs[:17]
        ride_out, (slab, u_scr) = refs[17:17 + N_RIDE_OUT], refs[17 + N_RIDE_OUT:]
        qkv_out = ((q0, k0, v0), (q1, k1, v1), (q2, k2, v2))
        tails = ((kt0, vt0), (kt1, vt1), (kt2, vt2))
        j = pl.program_id(1)
        _ride_block(pl.program_id(0) * nt + j, ride_in + ride_out)

    x = x_ref[0]
    ms = jnp.mean(x * x, axis=-1, keepdims=True)
    h = x * lax.rsqrt(ms + RMS_EPS) * ng_ref[...]
    h = h * (1.0 + sc_ref[0]) + sh_ref[0]
    proj = _dot(h.astype(BF16), w_ref[...])

    bd = bd_ref[...]

    def headnorm(z, g):
        zz = z * z
        hi = zz.astype(BF16)
        lo = (zz - hi.astype(F32)).astype(BF16)
        msq = _dot(hi, bd) + _dot(lo, bd)
        return z * lax.rsqrt(msq + RMS_EPS) * g

    slab_i = 0
    for g in range(N_GROUPS):
        c0 = g * GROUP_W
        qn = headnorm(proj[:, c0:c0 + GROUP_W], qg_ref[...]) * (HEAD_DIM ** -0.5)
        kn = headnorm(proj[:, ATTN_W + c0:ATTN_W + c0 + GROUP_W], kg_ref[...])
        vv = proj[:, 2 * ATTN_W + c0:2 * ATTN_W + c0 + GROUP_W]
        if sample:
            qf_ref[0, :, c0:c0 + GROUP_W] = qn
            kt_ref[c0:c0 + GROUP_W, :] = kn.T
            vt_ref[c0:c0 + GROUP_W, :] = vv.T
            continue
        keep = min(WINDOWS[g], tile)

        @pl.when(j >= nt - max(WINDOWS[g] // tile, 1))
        def _(kn=kn, vv=vv, g=g, keep=keep):
            tails[g][0][0] = kn[tile - keep:, :].T
            tails[g][1][0] = vv[tile - keep:, :].T

        d = dils[g]
        for val, out in zip((qn, kn, vv), qkv_out[g]):
            if d == 1:
                out[0, 0] = val.astype(BF16)
                continue
            n = tile // d
            for half in range(GROUP_W // LANES):
                slab[slab_i] = val[:, half * LANES:(half + 1) * LANES]
                for r in range(d):
                    out[0, r, :, half * LANES:(half + 1) * LANES] = (
                        slab[slab_i, pl.ds(r, n, stride=d), :].astype(BF16))
                slab_i += 1

    base = 3 * ATTN_W
    u = proj[:, base + 2 * CONV_CH:base + 3 * CONV_CH] * proj[:, base:base + CONV_CH]
    gate_b = proj[:, base + CONV_CH:base + 2 * CONV_CH]
    if sample:
        u_scr[0:8, :] = jnp.zeros((8, CONV_CH), F32)
    else:
        @pl.when(j == 0)
        def _():
            u_scr[0:8, :] = jnp.zeros((8, CONV_CH), F32)

        @pl.when(j > 0)
        def _():
            u_scr[0:8, :] = u_scr[tile:tile + 8, :]

    u_scr[8:tile + 8, :] = u
    um1 = u_scr[7:tile + 7, :]
    um2 = u_scr[6:tile + 6, :]
    if sample:
        t = lax.broadcasted_iota(jnp.int32, (tile, CONV_CH), 0) % 4
        um1 = jnp.where(t >= 1, um1, 0.0) + hb_ref[...]
        um2 = jnp.where(t >= 2, um2, 0.0) + ha_ref[...]
    cw = cw_ref[...]
    conv = cw[0:1] * um2 + cw[1:2] * um1 + cw[2:3] * u
    u_ref[0] = u if sample else u[tile - 8:, :]
    co_ref[0] = (gate_b * conv).astype(BF16)


def _proj(x, sc, sh, ng, w_bf, qg, kg, cw, bd, hist, *, tile, dils, sample, ride=None):
    B, S, _ = x.shape
    nt = S // tile
    tm = sc.shape[1]
    mod_spec = pl.BlockSpec((1, tm, D_MODEL), (lambda b, j: (b, j, 0)) if tm > 1 else (lambda b, j: (b, 0, 0)))
    const2 = lambda shape: pl.BlockSpec(shape, lambda b, j: (0, 0))
    in_specs = [pl.BlockSpec((1, tile, D_MODEL), lambda b, j: (b, j, 0)), mod_spec, mod_spec,
                const2((1, D_MODEL)), const2((D_MODEL, N_PROJ)), const2((1, GROUP_W)),
                const2((1, GROUP_W)), const2((3, CONV_CH)), const2((GROUP_W, GROUP_W))]
    args = [x, sc, sh, ng, w_bf, qg, kg, cw, bd]
    nat = lambda w: pl.BlockSpec((1, tile, w), lambda b, j: (b, j, 0))
    out_specs, out_shape = [], []
    scratch = []
    if sample:
        assert B == 1 and nt == 1
        in_specs += [pl.BlockSpec((tile, CONV_CH), lambda b, j: (j, 0))] * 2
        args += list(hist)
        out_specs += [nat(ATTN_W)] + [pl.BlockSpec((ATTN_W, tile), lambda b, j: (0, j))] * 2 + [nat(CONV_CH)]
        out_shape += [jax.ShapeDtypeStruct((B, S, ATTN_W), F32)]
        out_shape += [jax.ShapeDtypeStruct((ATTN_W, S), F32)] * 2
        out_shape += [jax.ShapeDtypeStruct((B, S, CONV_CH), F32)]
    else:
        for d in dils:
            for _ in range(3):
                out_specs.append(pl.BlockSpec((1, d, tile // d, GROUP_W), lambda b, j: (b, 0, j, 0)))
                out_shape.append(jax.ShapeDtypeStruct((B, d, S // d, GROUP_W), BF16))
        for w in WINDOWS:
            keep, first = min(w, tile), nt - max(w // tile, 1)
            for _ in range(2):
                out_specs.append(pl.BlockSpec((1, GROUP_W, keep),
                                              lambda b, j, first=first: (b, 0, jnp.maximum(j - first, 0))))
                out_shape.append(jax.ShapeDtypeStruct((B, GROUP_W, w), F32))
        out_specs.append(pl.BlockSpec((1, 8, CONV_CH), lambda b, j: (b, 0, 0)))
        out_shape.append(jax.ShapeDtypeStruct((B, 8, CONV_CH), F32))
        n_slabs = sum(3 * (GROUP_W // LANES) for d in dils if d > 1)
        scratch.append(pltpu.VMEM((n_slabs, tile, LANES), F32))
    out_specs.append(nat(CONV_CH))
    out_shape.append(jax.ShapeDtypeStruct((B, S, CONV_CH), BF16))
    scratch.append(pltpu.VMEM((tile + 8, CONV_CH), F32))
    if not sample:
        nb = ride[1].shape[0]
        r_in, r_out, r_shape = _ride_specs(ride, nb // (B * nt), lambda b, j: b * nt + j)
        in_specs, args = in_specs + r_in, args + list(ride)
        out_specs, out_shape = out_specs + r_out, out_shape + r_shape
    return pl.pallas_call(
        functools.partial(_proj_kernel, tile=tile, dils=dils, sample=sample, nt=nt),
        grid=(B, nt),
        in_specs=in_specs,
        out_specs=out_specs,
        out_shape=out_shape,
        scratch_shapes=scratch,
        compiler_params=_params(("arbitrary", "arbitrary")),
        name="proj_sample" if sample else "proj_prompt",
    )(*args)


def _attn_kernel(q_ref, kp_ref, kc_ref, vp_ref, vc_ref, b_ref, o_ref, l_ref, s_scr, m_scr, o_scr, z_scr,
                 *, nq):
    first = pl.program_id(1) == 0
    low = lax.broadcasted_iota(jnp.int32, (BLK, LANES), 1) < HEAD_DIM
    ones = jnp.ones((BLK, LANES), BF16)
    chains = [(j, pair, sub) for j in range(nq) for pair in range(GROUP_W // LANES) for sub in range(2)]

    def operands(j, pair):
        rows = slice(j * BLK, (j + 1) * BLK)
        sl = slice(pair * LANES, (pair + 1) * LANES)
        if j == 0:
            return rows, sl, kp_ref[0, :, sl], vp_ref[0, :, sl]
        prows = slice((j - 1) * BLK, j * BLK)
        return rows, sl, kc_ref[0, prows, sl], vc_ref[0, prows, sl]

    for c, (j, pair, sub) in enumerate(chains):
        rows, sl, kp, _ = operands(j, pair)
        q = q_ref[0, rows, sl]
        qm = jnp.where(low if sub == 0 else ~low, q, jnp.zeros_like(q))
        hh = 2 * pair + sub
        sp = _dot_nt(qm, kp) + b_ref[hh, :, 0:BLK]
        if j == 0:
            sp = jnp.where(first, NEG, sp)
        sc = _dot_nt(qm, kc_ref[0, rows, sl]) + b_ref[hh, :, BLK:2 * BLK]
        s_scr[c, :, 0:BLK] = sp
        s_scr[c, :, BLK:2 * BLK] = sc
        m_scr[c] = jnp.broadcast_to(jnp.max(jnp.maximum(sp, sc), axis=-1, keepdims=True), (BLK, LANES))

    for c, (j, pair, sub) in enumerate(chains):
        rows, sl, _, vp = operands(j, pair)
        m = m_scr[c]
        pp = jnp.exp(s_scr[c, :, 0:BLK] - m).astype(BF16)
        pc = jnp.exp(s_scr[c, :, BLK:2 * BLK] - m).astype(BF16)
        o_scr[c] = _dot(pp, vp) + _dot(pc, vc_ref[0, rows, sl])
        z_scr[c] = _dot(pp, ones) + _dot(pc, ones)

    for c in range(0, len(chains), 2):
        j, pair, _ = chains[c]
        rows = slice(j * BLK, (j + 1) * BLK)
        sl = slice(pair * LANES, (pair + 1) * LANES)
        o_sub = [o_scr[c + sub] / z_scr[c + sub] for sub in range(2)]
        l_sub = [m_scr[c + sub] + jnp.log(z_scr[c + sub]) for sub in range(2)]
        o_ref[0, rows, sl] = jnp.where(low, o_sub[0], o_sub[1])
        l_ref[0, rows, sl] = jnp.where(low, l_sub[0], l_sub[1])


def _attn_prompt(q, k, v, bias):
    Z, L, _ = q.shape
    qt = min(L, 4 * BLK)
    nq = qt // BLK
    cur = pl.BlockSpec((1, qt, GROUP_W), lambda z, i: (z, i, 0))
    prev = pl.BlockSpec((1, BLK, GROUP_W), lambda z, i: (z, jnp.maximum(i * nq - 1, 0), 0))
    return pl.pallas_call(
        functools.partial(_attn_kernel, nq=nq),
        grid=(Z, L // qt),
        in_specs=[cur, prev, cur, prev, cur,
                  pl.BlockSpec((HEADS_PER_GROUP, BLK, 2 * BLK), lambda z, i: (0, 0, 0))],
        out_specs=[cur, cur],
        out_shape=[jax.ShapeDtypeStruct((Z, L, GROUP_W), F32)] * 2,
        scratch_shapes=[pltpu.VMEM((nq * HEADS_PER_GROUP, BLK, 2 * BLK), F32)]
        + [pltpu.VMEM((nq * HEADS_PER_GROUP, BLK, LANES), F32)] * 3,
        compiler_params=_params(("arbitrary", "arbitrary")),
        name="attn_prompt",
    )(q, k, k, v, v, bias)


def _attn_sample_batch(gb, b, q_ref, kc_ref, vc_ref, nk_ref, nv_ref, b1_ref, b2_ref, ko_ref, vo_ref, o_ref,
                       l_ref, heads=range(HEADS_PER_GROUP)):
    wb = kc_ref.shape[-1]
    lane = lax.broadcasted_iota(jnp.int32, (HEAD_DIM, LANES), 1)
    new_cols = lane >= LANES - 4
    per_tile = LANES // 4
    tile_at = pl.ds(pl.multiple_of((gb // per_tile) * LANES, LANES), LANES)
    shift = (LANES - 4) - 4 * (gb % per_tile)
    for hh in heads:
        q = q_ref[b, hh]
        kt = kc_ref[b, hh]
        vt = vc_ref[b, hh]
        nkt = pltpu.roll(nk_ref[hh, :, tile_at], shift, axis=1)
        nvt = pltpu.roll(nv_ref[hh, :, tile_at], shift, axis=1)
        s1 = _dot(q, kt.astype(BF16)) + b1_ref[hh]
        s2 = _dot(q, nkt.astype(BF16)) + b2_ref[hh]
        m = jnp.maximum(jnp.max(s1, axis=-1, keepdims=True), jnp.max(s2, axis=-1, keepdims=True))
        p1 = jnp.exp(s1 - m)
        p2 = jnp.exp(s2 - m)
        z = jnp.sum(p1, axis=-1, keepdims=True) + jnp.sum(p2, axis=-1, keepdims=True)
        o = _dot_nt(p1.astype(BF16), vt.astype(BF16)) + _dot_nt(p2.astype(BF16), nvt.astype(BF16))
        o_ref[b, hh] = o / z
        l_ref[b, hh] = jnp.broadcast_to(m + jnp.log(z), (8, LANES))
        for src, new, dst in ((kt, nkt, ko_ref), (vt, nvt, vo_ref)):
            rolled = pltpu.roll(src, wb - 4, axis=1)
            if wb > LANES:
                dst[b, hh, :, 0:wb - LANES] = rolled[:, 0:wb - LANES]
            dst[b, hh, :, wb - LANES:wb] = jnp.where(new_cols, new, rolled[:, wb - LANES:wb])


N_RIDE_IN = 7
N_RIDE_OUT = 4


def _ride_specs(ride, bb, block_index):
    q, kc, vc, nk, nv, b1, b2 = ride
    nb, _, _, wb = kc.shape
    blk = lambda *tail: pl.BlockSpec((bb, HEADS_PER_GROUP) + tail, lambda *g: (block_index(*g), 0, 0, 0))
    const = lambda a: pl.BlockSpec(a.shape, lambda *g: (0,) * a.ndim)
    in_specs = [blk(8, HEAD_DIM), blk(HEAD_DIM, wb), blk(HEAD_DIM, wb), const(nk), const(nv), const(b1), const(b2)]
    out_specs = [blk(HEAD_DIM, wb), blk(HEAD_DIM, wb), blk(8, HEAD_DIM), blk(8, LANES)]
    out_shape = [jax.ShapeDtypeStruct(kc.shape, F32), jax.ShapeDtypeStruct(kc.shape, F32),
                 jax.ShapeDtypeStruct((nb, HEADS_PER_GROUP, 8, HEAD_DIM), F32),
                 jax.ShapeDtypeStruct((nb, HEADS_PER_GROUP, 8, LANES), F32)]
    return in_specs, out_specs, out_shape


def _ride_block(step, ride_refs):
    bb = ride_refs[0].shape[0]
    for b in range(bb):
        _attn_sample_batch(step * bb + b, b, *ride_refs)


def _first_max(v, ids, sentinel):
    m = jnp.max(v, axis=0, keepdims=True)
    idx = jnp.min(jnp.where(v == m, ids, sentinel), axis=0, keepdims=True)
    return m, ids == idx


def _oproj_kernel(*refs, tile, dils, ride):
    (x_ref, o0, o1, o2, l0, l1, l2, co_ref, g1_ref, sc_ref, sh_ref, ng_ref, wo_ref, wr_ref, br_ref) = refs[:15]
    refs = refs[15:]
    if ride:
        ride_in, refs = refs[:N_RIDE_IN], refs[N_RIDE_IN:]
        ride_out = refs[6:6 + N_RIDE_OUT]
        _ride_block(pl.program_id(0) * pl.num_programs(1) + pl.program_id(1), ride_in + ride_out)
    x1_ref, xg_ref, gt_ref, posr_ref, post_ref, cnt_ref = refs[:6]
    slab = refs[-1]
    o_refs, l_refs = (o0, o1, o2), (l0, l1, l2)
    outs, lses = [], []
    slab_i = 0
    for g, d in enumerate(dils):
        if d == 1:
            outs.append(o_refs[g][0, 0])
            lses.append(l_refs[g][0, 0])
            continue
        n = tile // d
        for ref, dest in ((o_refs[g], outs), (l_refs[g], lses)):
            halves = []
            for half in range(GROUP_W // LANES):
                for r in range(d):
                    slab[slab_i, pl.ds(r, n, stride=d), :] = ref[0, r, :, half * LANES:(half + 1) * LANES]
                halves.append(slab[slab_i])
                slab_i += 1
            dest.append(jnp.concatenate(halves, axis=1))

    m = jnp.maximum(jnp.maximum(lses[0], lses[1]), lses[2])
    es = [jnp.exp(l - m) for l in lses]
    den = es[0] + es[1] + es[2]
    cat = [(outs[g] * (es[g] / den)).astype(BF16) for g in range(N_GROUPS)]
    cat.append(co_ref[0].astype(BF16))
    cat = jnp.concatenate(cat, axis=1)
    x1 = x_ref[0] + g1_ref[0] * _dot(cat, wo_ref[...])
    x1_ref[0] = x1

    ms = jnp.mean(x1 * x1, axis=-1, keepdims=True)
    h2 = x1 * lax.rsqrt(ms + RMS_EPS) * ng_ref[...]
    h2 = h2 * (1.0 + sc_ref[0]) + sh_ref[0]
    hh = h2.astype(BF16)
    xg_ref[0, :, 0:D_MODEL] = hh

    hl = (h2 - hh.astype(F32)).astype(BF16)
    wr = wr_ref[...]
    wh = wr.astype(BF16)
    wl = (wr - wh.astype(F32)).astype(BF16)
    logits = _dot_nt(wh, hh) + _dot_nt(wh, hl) + _dot_nt(wl, hh)
    scores = 1.0 / (1.0 + jnp.exp(-logits))
    sel = scores + br_ref[...]

    ids = lax.broadcasted_iota(jnp.int32, (GROUP_SIZE, tile), 0)
    ninf = -jnp.inf
    sel_g = [sel[g * GROUP_SIZE:(g + 1) * GROUP_SIZE] for g in range(N_ROUTE_GROUPS)]
    gscore = jnp.zeros((N_ROUTE_GROUPS, tile), F32)
    for g in range(N_ROUTE_GROUPS):
        m1, oh = _first_max(sel_g[g], ids, GROUP_SIZE)
        m2 = jnp.max(jnp.where(oh, ninf, sel_g[g]), axis=0, keepdims=True)
        gscore = jnp.where(ids == g, m1 + m2, gscore)
    gsel = jnp.zeros((N_ROUTE_GROUPS, tile), F32)
    for _ in range(TOPK_GROUPS):
        _, oh = _first_max(gscore, ids, N_ROUTE_GROUPS)
        gsel = jnp.where(oh, 1.0, gsel)
        gscore = jnp.where(oh, ninf, gscore)
    cand = [jnp.where(gsel[g:g + 1] > 0.0, sel_g[g], ninf) for g in range(N_ROUTE_GROUPS)]
    chosen = [jnp.zeros((GROUP_SIZE, tile), F32) for _ in range(N_ROUTE_GROUPS)]
    picks = []
    for _ in range(TOP_K):
        mx = cand[0]
        for g in range(1, N_ROUTE_GROUPS):
            mx = jnp.maximum(mx, cand[g])
        mx = jnp.max(mx, axis=0, keepdims=True)
        idx = jnp.where(cand[0] == mx, ids, N_EXPERTS)
        for g in range(1, N_ROUTE_GROUPS):
            idx = jnp.minimum(idx, jnp.where(cand[g] == mx, ids + g * GROUP_SIZE, N_EXPERTS))
        idx = jnp.min(idx, axis=0, keepdims=True)
        ohs = [(ids + g * GROUP_SIZE) == idx for g in range(N_ROUTE_GROUPS)]
        picks.append(ohs)
        for g in range(N_ROUTE_GROUPS):
            chosen[g] = jnp.where(ohs[g], 1.0, chosen[g])
            cand[g] = jnp.where(ohs[g], ninf, cand[g])
    wts = [jnp.where(chosen[g] > 0.0, scores[g * GROUP_SIZE:(g + 1) * GROUP_SIZE], 0.0)
           for g in range(N_ROUTE_GROUPS)]
    wsum = jnp.sum(wts[0], axis=0, keepdims=True)
    for g in range(1, N_ROUTE_GROUPS):
        wsum = wsum + jnp.sum(wts[g], axis=0, keepdims=True)
    gates_t = jnp.concatenate([w / wsum * ROUTED_SCALE for w in wts], axis=0)
    gt_ref[0] = jnp.concatenate([gates_t, jnp.zeros((LANES - N_EXPERTS, tile), F32)], axis=0).T
    g_hi = gates_t.astype(BF16).astype(F32)
    xg_ref[0, :, D_MODEL:D_MODEL + LANES] = jnp.concatenate([g_hi, gates_t - g_hi], axis=0).T.astype(BF16)

    er = lax.broadcasted_iota(jnp.int32, (N_EXPERTS, N_EXPERTS), 0)
    ec = lax.broadcasted_iota(jnp.int32, (N_EXPERTS, N_EXPERTS), 1)
    lower = jnp.where(ec < er, 1.0, 0.0).astype(BF16)
    tr = lax.broadcasted_iota(jnp.int32, (SUB, SUB), 0)
    tc = lax.broadcasted_iota(jnp.int32, (SUB, SUB), 1)
    upper = jnp.where(tr < tc, 1.0, 0.0).astype(BF16)
    for u in range(tile // SUB):
        cols = slice(u * SUB, (u + 1) * SUB)
        ch = jnp.concatenate([c[:, cols] for c in chosen], axis=0)
        cnt = jnp.sum(ch, axis=1, keepdims=True)
        cpad = jnp.broadcast_to(jnp.ceil(cnt / CHUNK) * CHUNK, (N_EXPERTS, LANES))
        cnt_ref[u] = cpad
        seg0 = _dot(lower, cpad.astype(BF16))[:, 0:1]
        slot = seg0 + _dot(ch.astype(BF16), upper)
        rows = []
        for k in range(TOP_K):
            acc = jnp.where(picks[k][0][:, cols], slot[0:GROUP_SIZE], 0.0)
            for g in range(1, N_ROUTE_GROUPS):
                acc = acc + jnp.where(picks[k][g][:, cols], slot[g * GROUP_SIZE:(g + 1) * GROUP_SIZE], 0.0)
            rows.append(jnp.sum(acc, axis=0, keepdims=True))
        posr = jnp.concatenate(rows, axis=0)
        posr_ref[u] = posr
        post_ref[0, cols, :] = jnp.concatenate([posr, jnp.zeros((LANES - TOP_K, SUB), F32)], axis=0).T


def _oproj(x, o_list, l_list, co, g1, sc, sh, ng, wo_bf, wr_t, br, *, tile, dils, ride=None):
    B, S, _ = x.shape
    nt = S // tile
    r_in, r_out, r_shape, r_args = [], [], [], []
    if ride is not None:
        r_in, r_out, r_shape = _ride_specs(ride, ride[1].shape[0] // (B * nt), lambda b, j: b * nt + j)
        r_args = list(ride)
    tm = g1.shape[1]
    mod_spec = pl.BlockSpec((1, tm, D_MODEL), (lambda b, j: (b, j, 0)) if tm > 1 else (lambda b, j: (b, 0, 0)))
    const2 = lambda shape: pl.BlockSpec(shape, lambda b, j: (0, 0))
    nat = lambda w: pl.BlockSpec((1, tile, w), lambda b, j: (b, j, 0))
    dspec = [pl.BlockSpec((1, d, tile // d, GROUP_W), lambda b, j: (b, 0, j, 0)) for d in dils]
    n_slabs = sum(2 * (GROUP_W // LANES) for d in dils if d > 1)
    nsub_t = tile // SUB
    return pl.pallas_call(
        functools.partial(_oproj_kernel, tile=tile, dils=dils, ride=ride is not None),
        grid=(B, nt),
        in_specs=[nat(D_MODEL)] + dspec + dspec + [nat(CONV_CH), mod_spec, mod_spec, mod_spec,
                  const2((1, D_MODEL)), const2((D_MODEL, D_MODEL)), const2((N_EXPERTS, D_MODEL)),
                  const2((N_EXPERTS, 1))] + r_in,
        out_specs=[nat(D_MODEL), nat(XG_W), nat(LANES),
                   pl.BlockSpec((nsub_t, TOP_K, SUB), lambda b, j: (b * nt + j, 0, 0)),
                   nat(LANES),
                   pl.BlockSpec((nsub_t, N_EXPERTS, LANES), lambda b, j: (b * nt + j, 0, 0))] + r_out,
        out_shape=[jax.ShapeDtypeStruct((B, S, D_MODEL), F32), jax.ShapeDtypeStruct((B, S, XG_W), BF16),
                   jax.ShapeDtypeStruct((B, S, LANES), F32),
                   jax.ShapeDtypeStruct((B * S // SUB, TOP_K, SUB), F32),
                   jax.ShapeDtypeStruct((B, S, LANES), F32),
                   jax.ShapeDtypeStruct((B * S // SUB, N_EXPERTS, LANES), F32)] + r_shape,
        scratch_shapes=[pltpu.VMEM((max(n_slabs, 1), tile, LANES), F32)],
        compiler_params=_params(("arbitrary", "arbitrary")),
        name="oproj",
    )(x, *o_list, *l_list, co, g1, sc, sh, ng, wo_bf, wr_t, br, *r_args)


def _moe_kernel(h_ref, gt_ref, x1_ref, g2_ref, wg_ref, wu_ref, wd_ref, wgs_ref, wus_ref, wds_ref,
                y_ref, acc, *, tile):
    e = pl.program_id(2)
    h = h_ref[0]

    @pl.when(e == 0)
    def _():
        a = _dot(h, wgs_ref[...].astype(BF16))
        b = _dot(h, wus_ref[...].astype(BF16))
        acc[...] = _dot((_silu(a) * b).astype(BF16), wds_ref[...].astype(BF16))

    a = _dot(h, wg_ref[0].astype(BF16))
    b = _dot(h, wu_ref[0].astype(BF16))
    lane = lax.broadcasted_iota(jnp.int32, (tile, LANES), 1)
    gcol = jnp.sum(jnp.where(lane == e, gt_ref[0], 0.0), axis=1, keepdims=True)
    hc = _silu(a) * b * gcol
    acc[...] += _dot(hc.astype(BF16), wd_ref[0].astype(BF16))

    @pl.when(e == N_EXPERTS - 1)
    def _():
        y_ref[0] = x1_ref[0] + g2_ref[0] * acc[...]


def _moe(h2, gates, x1, g2, wg, wu, wd, wgs, wus, wds, *, tile):
    B, S, _ = h2.shape
    tm = g2.shape[1]
    mod_spec = pl.BlockSpec((1, tm, D_MODEL),
                            (lambda b, j, e: (b, j, 0)) if tm > 1 else (lambda b, j, e: (b, 0, 0)))
    nat = lambda w: pl.BlockSpec((1, tile, w), lambda b, j, e: (b, j, 0))
    const2 = lambda shape: pl.BlockSpec(shape, lambda b, j, e: (0, 0))
    return pl.pallas_call(
        functools.partial(_moe_kernel, tile=tile),
        grid=(B, S // tile, N_EXPERTS),
        in_specs=[nat(D_MODEL), nat(LANES), nat(D_MODEL), mod_spec,
                  pl.BlockSpec((1, D_MODEL, D_EXPERT), lambda b, j, e: (e, 0, 0)),
                  pl.BlockSpec((1, D_MODEL, D_EXPERT), lambda b, j, e: (e, 0, 0)),
                  pl.BlockSpec((1, D_EXPERT, D_MODEL), lambda b, j, e: (e, 0, 0)),
                  const2((D_MODEL, D_EXPERT)), const2((D_MODEL, D_EXPERT)), const2((D_EXPERT, D_MODEL))],
        out_specs=nat(D_MODEL),
        out_shape=jax.ShapeDtypeStruct((B, S, D_MODEL), F32),
        scratch_shapes=[pltpu.VMEM((tile, D_MODEL), F32)],
        compiler_params=_params(("arbitrary", "arbitrary", "arbitrary")),
        name="moe",
    )(h2, gates, x1, g2, wg, wu, wd, wgs, wus, wds)


FFN_TM = 1024
SUB_ROWS = SUB * TOP_K + N_EXPERTS * CHUNK
N_CHUNKS = SUB_ROWS // CHUNK
ZERO_CHUNK = N_CHUNKS
TABLE_W = 256
MXU_ROWS = 256
BLOCKS_PER_TRIP = 2


def _slot_onehot(first, pos_list, axis_iota):
    shape = axis_iota.shape
    hit = axis_iota == jnp.broadcast_to(pos_list[0] - first, shape).astype(BF16)
    for p in pos_list[1:]:
        hit = hit | (axis_iota == jnp.broadcast_to(p - first, shape).astype(BF16))
    return jnp.where(hit, jnp.ones(shape, BF16), jnp.zeros(shape, BF16))


def _chunk_copy(src, src_chunk, dst, dst_chunk, sem):
    return pltpu.make_async_copy(src.at[src_chunk], dst.at[dst_chunk], sem)


def _for_each(n, body):
    def two(t, carry):
        body(2 * t)
        body(2 * t + 1)
        return carry

    lax.fori_loop(0, n // 2, two, 0)

    @pl.when(n % 2 == 1)
    def _():
        body(n - 1)


def _dispatch_kernel(nblk_ref, nreal_ref, ncopy_ref, ntile_ref, src_ref, dst_ref, posr_ref, xg_ref, xs_hbm,
                     xs_scr, zero_scr, sem, tail_sem, *, nsub, n_tiles_max):
    s = pl.program_id(0)

    tile_chunks = FFN_TM // CHUNK
    blk_chunks = MXU_ROWS // CHUNK
    zero_scr[...] = jnp.zeros((tile_chunks, CHUNK, XG_W), BF16)
    tail_blocks = [ntile_ref[0] + s + r * nsub for r in range(-(-(n_tiles_max) // nsub))]

    def tail_copy(t):
        return pltpu.make_async_copy(
            zero_scr, xs_hbm.at[pl.ds(pl.multiple_of(t * tile_chunks, tile_chunks), tile_chunks)], tail_sem)

    for t in tail_blocks:
        @pl.when(t < n_tiles_max)
        def _():
            tail_copy(t).start()

    xg = xg_ref[...]
    pos = [posr_ref[0, k:k + 1, :] for k in range(TOP_K)]
    rows = lax.broadcasted_iota(jnp.int32, (MXU_ROWS, SUB), 0).astype(BF16)
    xs_scr[ZERO_CHUNK] = jnp.zeros((CHUNK, XG_W), BF16)

    def sort_blocks(i, carry):
        for j in range(BLOCKS_PER_TRIP):
            blk = i * BLOCKS_PER_TRIP + j
            onehot = _slot_onehot((blk * MXU_ROWS).astype(F32), pos, rows)
            sorted_rows = _dot(onehot, xg).astype(BF16)
            xs_scr[pl.ds(pl.multiple_of(blk * blk_chunks, blk_chunks), blk_chunks)] = (
                sorted_rows.reshape(blk_chunks, CHUNK, XG_W))
        k0 = i * trip_chunks
        _for_each(jnp.clip(n_real - k0, 0, trip_chunks), lambda t: send(k0 + t))
        return carry

    n_real = nreal_ref[s]
    n = ncopy_ref[s]
    trip_chunks = BLOCKS_PER_TRIP * blk_chunks

    def send(i):
        _chunk_copy(xs_scr, src_ref[0, 0, i], xs_hbm, dst_ref[0, 0, i], sem).start()

    lax.fori_loop(0, nblk_ref[s], sort_blocks, 0)
    _for_each(n - n_real, lambda t: send(n_real + t))
    _for_each(n, lambda i: _chunk_copy(xs_scr, 0, xs_hbm, 0, sem).wait())
    for t in tail_blocks:
        @pl.when(t < n_tiles_max)
        def _():
            tail_copy(t).wait()


def _dispatch(nblk, n_real, ncopy, n_tiles, src, dst, posr, xg, n_rows):
    nsub = posr.shape[0]
    tab = pl.BlockSpec((1, 1, TABLE_W), lambda s, *_: (s, 0, 0), memory_space=pltpu.SMEM)
    return pl.pallas_call(
        functools.partial(_dispatch_kernel, nsub=nsub, n_tiles_max=n_rows // FFN_TM),
        grid_spec=pltpu.PrefetchScalarGridSpec(
            num_scalar_prefetch=4, grid=(nsub,),
            in_specs=[tab, tab, pl.BlockSpec((1, TOP_K, SUB), lambda s, *_: (s, 0, 0)),
                      pl.BlockSpec((SUB, XG_W), lambda s, *_: (s, 0))],
            out_specs=pl.BlockSpec(memory_space=pl.ANY),
            scratch_shapes=[pltpu.VMEM((N_CHUNKS + 1, CHUNK, XG_W), BF16),
                            pltpu.VMEM((FFN_TM // CHUNK, CHUNK, XG_W), BF16),
                            pltpu.SemaphoreType.DMA(()), pltpu.SemaphoreType.DMA(())]),
        out_shape=jax.ShapeDtypeStruct((n_rows // CHUNK, CHUNK, XG_W), BF16),
        compiler_params=_params(("arbitrary",)),
        name="moe_dispatch",
    )(nblk, n_real, ncopy, n_tiles, src, dst, posr, xg).reshape(n_rows, XG_W)


def _ffn_kernel(te_ref, nt_ref, xs_ref, wg_ref, wu_ref, wd_ref,
                q_ref, kc_ref, vc_ref, nk_ref, nv_ref, b1_ref, b2_ref,
                ys_ref, ko_ref, vo_ref, o_ref, l_ref, wg_s, wu_s, wd_s, *, steps_per_batch):
    i = pl.program_id(0)
    n_heads = HEADS_PER_GROUP // steps_per_batch
    first_head = (i % steps_per_batch) * n_heads

    def ride():
        _attn_sample_batch(i // steps_per_batch, 0, q_ref, kc_ref, vc_ref, nk_ref, nv_ref, b1_ref, b2_ref,
                           ko_ref, vo_ref, o_ref, l_ref, heads=[first_head + j for j in range(n_heads)])

    @pl.when(i >= nt_ref[0])
    def _():
        ride()
        ys_ref[...] = jnp.zeros((FFN_TM, D_MODEL), BF16)

    @pl.when(i < nt_ref[0])
    def _():
        e = te_ref[i]

        @pl.when((i == 0) | (e != te_ref[jnp.maximum(i - 1, 0)]))
        def _():
            wg_s[...] = wg_ref[0].astype(BF16)
            wu_s[...] = wu_ref[0].astype(BF16)
            wd_s[...] = wd_ref[0].astype(BF16)

        ride()
        x = xs_ref[:, 0:D_MODEL]
        g = xs_ref[:, D_MODEL:XG_W].astype(F32)
        lane = lax.broadcasted_iota(jnp.int32, (FFN_TM, LANES), 1)
        gcol = jnp.sum(jnp.where((lane == e) | (lane == e + N_EXPERTS), g, 0.0), axis=1, keepdims=True)
        a = _dot(x, wg_s[...])
        b = _dot(x, wu_s[...])
        hc = _silu(a) * b * gcol
        ys_ref[...] = _dot(hc.astype(BF16), wd_s[...]).astype(BF16)


def _ffn(tile_expert, n_tiles, xs, wg, wu, wd, ride):
    q, kc, vc, nk, nv, b1, b2 = ride
    nb, _, _, wb = kc.shape
    n_rows = xs.shape[0]
    n_steps = n_rows // FFN_TM
    assert n_steps % nb == 0 and HEADS_PER_GROUP % (n_steps // nb) == 0
    per = n_steps // nb
    row = lambda i, te, nt: (jnp.minimum(i, nt[0] - 1), 0)
    wspec = lambda shape: pl.BlockSpec((1,) + shape, lambda i, te, nt: (te[i], 0, 0))
    rblk = lambda *tail: pl.BlockSpec((1, HEADS_PER_GROUP) + tail, lambda i, te, nt: (i // per, 0, 0, 0))
    const = lambda shape: pl.BlockSpec(shape, lambda i, te, nt: (0, 0, 0))
    return pl.pallas_call(
        functools.partial(_ffn_kernel, steps_per_batch=per),
        grid_spec=pltpu.PrefetchScalarGridSpec(
            num_scalar_prefetch=2, grid=(n_steps,),
            in_specs=[pl.BlockSpec((FFN_TM, XG_W), row), wspec((D_MODEL, D_EXPERT)),
                      wspec((D_MODEL, D_EXPERT)), wspec((D_EXPERT, D_MODEL)),
                      rblk(8, HEAD_DIM), rblk(HEAD_DIM, wb), rblk(HEAD_DIM, wb), const(nk.shape), const(nv.shape),
                      const(b1.shape), const(b2.shape)],
            out_specs=[pl.BlockSpec((FFN_TM, D_MODEL), lambda i, te, nt: (i, 0)),
                       rblk(HEAD_DIM, wb), rblk(HEAD_DIM, wb), rblk(8, HEAD_DIM), rblk(8, LANES)],
            scratch_shapes=[pltpu.VMEM((D_MODEL, D_EXPERT), BF16), pltpu.VMEM((D_MODEL, D_EXPERT), BF16),
                            pltpu.VMEM((D_EXPERT, D_MODEL), BF16)]),
        out_shape=[jax.ShapeDtypeStruct((n_rows, D_MODEL), BF16),
                   jax.ShapeDtypeStruct(kc.shape, F32), jax.ShapeDtypeStruct(kc.shape, F32),
                   jax.ShapeDtypeStruct((nb, HEADS_PER_GROUP, 8, HEAD_DIM), F32),
                   jax.ShapeDtypeStruct((nb, HEADS_PER_GROUP, 8, LANES), F32)],
        compiler_params=_params(("arbitrary",)),
        name="moe_ffn",
    )(tile_expert, n_tiles, xs, wg, wu, wd, q, kc, vc, nk, nv, b1, b2)


def _combine_kernel(nblk_ref, ncopy_ref, src_ref, post_ref, xg_ref, x1_ref, g2_ref, wgs_ref, wus_ref,
                    wds_ref, ys_hbm, y_ref, ys_scr, wgs_s, wus_s, wds_s, sem):
    s = pl.program_id(0)
    n = ncopy_ref[s]
    blk_chunks = MXU_ROWS // CHUNK
    trip_chunks = BLOCKS_PER_TRIP * blk_chunks

    def fetch_trip(t, carry):
        k0 = t * trip_chunks
        _for_each(trip_chunks,
                  lambda u: _chunk_copy(ys_hbm, src_ref[0, 0, k0 + u], ys_scr, k0 + u, sem.at[t]).start())
        return carry

    lax.fori_loop(0, nblk_ref[s], fetch_trip, 0)

    @pl.when(s == 0)
    def _():
        wgs_s[...] = wgs_ref[...].astype(BF16)
        wus_s[...] = wus_ref[...].astype(BF16)
        wds_s[...] = wds_ref[...].astype(BF16)

    h = xg_ref[:, 0:D_MODEL]
    a = _dot(h, wgs_s[...])
    b = _dot(h, wus_s[...])
    shared = _dot((_silu(a) * b).astype(BF16), wds_s[...])

    post = post_ref[...]
    pos = [jnp.broadcast_to(post[:, k:k + 1], (SUB, MXU_ROWS)) for k in range(TOP_K)]
    lanes = lax.broadcasted_iota(jnp.int32, (SUB, MXU_ROWS), 1).astype(BF16)

    def gather_blocks(i, acc):
        _for_each(trip_chunks, lambda t: _chunk_copy(ys_hbm, 0, ys_scr, 0, sem.at[i]).wait())
        for j in range(BLOCKS_PER_TRIP):
            blk = i * BLOCKS_PER_TRIP + j
            onehot = _slot_onehot((blk * MXU_ROWS).astype(F32), pos, lanes)
            rows = ys_scr[pl.ds(pl.multiple_of(blk * blk_chunks, blk_chunks), blk_chunks)]
            acc = acc + _dot(onehot, rows.reshape(MXU_ROWS, D_MODEL))
        return acc

    routed = lax.fori_loop(0, nblk_ref[s], gather_blocks, jnp.zeros((SUB, D_MODEL), F32))
    y_ref[...] = x1_ref[...] + g2_ref[0] * (routed + shared)


def _combine(nblk, ncopy, src, post, xg, x1, g2, wgs, wus, wds, ys):
    n_tok = x1.shape[0]
    nsub = n_tok // SUB
    per_b = nsub // g2.shape[0]
    tab = pl.BlockSpec((1, 1, TABLE_W), lambda s, *_: (s, 0, 0), memory_space=pltpu.SMEM)
    const2 = lambda shape: pl.BlockSpec(shape, lambda s, *_: (0, 0))
    return pl.pallas_call(
        _combine_kernel,
        grid_spec=pltpu.PrefetchScalarGridSpec(
            num_scalar_prefetch=2, grid=(nsub,),
            in_specs=[tab, pl.BlockSpec((SUB, LANES), lambda s, *_: (s, 0)),
                      pl.BlockSpec((SUB, XG_W), lambda s, *_: (s, 0)),
                      pl.BlockSpec((SUB, D_MODEL), lambda s, *_: (s, 0)),
                      pl.BlockSpec((1, 1, D_MODEL), lambda s, *_: (s // per_b, 0, 0)),
                      const2((D_MODEL, D_EXPERT)), const2((D_MODEL, D_EXPERT)), const2((D_EXPERT, D_MODEL)),
                      pl.BlockSpec(memory_space=pl.ANY)],
            out_specs=pl.BlockSpec((SUB, D_MODEL), lambda s, *_: (s, 0)),
            scratch_shapes=[pltpu.VMEM((N_CHUNKS, CHUNK, D_MODEL), BF16), pltpu.VMEM((D_MODEL, D_EXPERT), BF16),
                            pltpu.VMEM((D_MODEL, D_EXPERT), BF16), pltpu.VMEM((D_EXPERT, D_MODEL), BF16),
                            pltpu.SemaphoreType.DMA((SUB_ROWS // (BLOCKS_PER_TRIP * MXU_ROWS),))]),
        out_shape=jax.ShapeDtypeStruct((n_tok, D_MODEL), F32),
        compiler_params=_params(("arbitrary",)),
        name="moe_combine",
    )(nblk, ncopy, src, post, xg, x1, g2, wgs, wus, wds, ys.reshape(-1, CHUNK, D_MODEL))


def _route_tables(cnt):
    nsub = cnt.shape[0]
    i32 = jnp.int32
    seg_end = jnp.cumsum(cnt, axis=1)
    seg_start = seg_end - cnt
    tot = jnp.sum(cnt, axis=0)
    tot_al = (tot + FFN_TM - 1) // FFN_TM * FFN_TM
    reg_end = jnp.cumsum(tot_al)
    reg_start = reg_end - tot_al
    base = reg_start[None, :] + jnp.cumsum(cnt, axis=0) - cnt
    n_real = seg_end[:, -1] // CHUNK
    k = jnp.arange(TABLE_W, dtype=i32)
    owner = jnp.sum((seg_end[:, None, :] // CHUNK) <= k[None, :, None], axis=2)
    owner = jnp.minimum(owner, N_EXPERTS - 1)
    delta = (base - seg_start) // CHUNK
    e_ids = jnp.arange(N_EXPERTS, dtype=i32)
    real_dst = jnp.sum(jnp.where(owner[:, :, None] == e_ids, delta[:, None, :], 0), axis=2) + k[None, :]
    fill_n = (tot_al - tot) // CHUNK
    fill_dst0 = (reg_start + tot) // CHUNK
    src = jnp.broadcast_to(k[None, :], (nsub, TABLE_W))
    dst = real_dst
    ncopy = n_real
    for r in range(-(-N_EXPERTS // nsub)):
        e_of = jnp.arange(nsub, dtype=i32) + r * nsub
        ok = e_of < N_EXPERTS
        e_cl = jnp.minimum(e_of, N_EXPERTS - 1)
        fn = jnp.where(ok, fill_n[e_cl], 0)
        j = k[None, :] - ncopy[:, None]
        is_fill = (j >= 0) & (j < fn[:, None])
        src = jnp.where(is_fill, ZERO_CHUNK, src)
        dst = jnp.where(is_fill, fill_dst0[e_cl][:, None] + j, dst)
        ncopy = ncopy + fn
    trip_rows = BLOCKS_PER_TRIP * MXU_ROWS
    disp_nblk = (n_real * CHUNK + trip_rows - 1) // trip_rows
    comb_n = disp_nblk * (trip_rows // CHUNK)
    comb_src = jnp.where(k[None, :] < n_real[:, None], real_dst, real_dst[:, 0:1])
    n_tiles = reg_end[-1] // FFN_TM
    t = jnp.arange((nsub * SUB_ROWS + N_EXPERTS * FFN_TM) // FFN_TM, dtype=i32)
    tile_expert = jnp.minimum(jnp.sum((reg_end[None, :] // FFN_TM) <= t[:, None], axis=1), N_EXPERTS - 1)
    as3 = lambda a: a.astype(i32).reshape(nsub, 1, TABLE_W)
    return (disp_nblk.astype(i32), n_real.astype(i32), ncopy.astype(i32), as3(src), as3(dst),
            comb_n.astype(i32), as3(comb_src), tile_expert.astype(i32), n_tiles.astype(i32).reshape(1))


def _moe_sparse(xg, post, posr, cnt, x1, g2, wg, wu, wd, wgs, wus, wds, ride):
    B, S, _ = x1.shape
    n_tok = B * S
    nsub = n_tok // SUB
    (disp_nblk, n_real, disp_n, disp_src, disp_dst, comb_n, comb_src, tile_expert,
     n_tiles) = _route_tables(cnt[:, :, 0].astype(jnp.int32))
    n_rows = nsub * SUB_ROWS + N_EXPERTS * FFN_TM
    xg2 = xg.reshape(n_tok, XG_W)
    xs = _dispatch(disp_nblk, n_real, disp_n, n_tiles, disp_src, disp_dst, posr, xg2, n_rows)
    ys, *ride_out = _ffn(tile_expert, n_tiles, xs, wg, wu, wd, ride)
    y = _combine(disp_nblk, comb_n, comb_src, post.reshape(n_tok, LANES), xg2, x1.reshape(n_tok, D_MODEL),
                 g2, wgs, wus, wds, ys)
    return y.reshape(B, S, D_MODEL), ride_out


def _t5_bucket(dist):
    max_exact = N_BUCKETS // 2
    df = jnp.maximum(dist, 1).astype(F32)
    large = max_exact + (jnp.log(df / max_exact) / math.log(MAX_DISTANCE / max_exact)
                         * (N_BUCKETS - max_exact)).astype(jnp.int32)
    large = jnp.minimum(large, N_BUCKETS - 1)
    return jnp.where(dist < max_exact, dist, large)


def _step_bias(rel_bias, g):
    dist = jnp.arange(N_KEYS, dtype=jnp.int32) * DILATIONS[g]
    cols = rel_bias[:, g * HEADS_PER_GROUP:(g + 1) * HEADS_PER_GROUP]
    return cols[_t5_bucket(dist)].T.astype(F32)


def _prompt_bias(bias_k):
    h = bias_k.shape[0]
    n = 3 * BLK
    row = jnp.concatenate([bias_k[:, ::-1], jnp.full((h, n - N_KEYS), NEG, F32)], axis=1)
    t = jnp.tile(row, (1, BLK))[:, :BLK * (n - 1)].reshape(h, BLK, n - 1)
    return t[:, :, :2 * BLK]


def _sample_bias(bias_k, wb, d):
    h = bias_k.shape[0]
    rev = bias_k[:, :0:-1]
    rows = []
    for t in range(4):
        if d == 1:
            rows.append(jnp.concatenate([jnp.full((h, t), NEG, F32), rev[:, :wb - t]], axis=1))
        else:
            r = jnp.arange(d)[None, None, :]
            rows.append(jnp.where(r == t, rev[:, :, None], NEG).reshape(h, wb))
    b1 = jnp.concatenate([jnp.stack(rows, axis=1), jnp.zeros((h, 4, wb), F32)], axis=1)
    rows = []
    for t in range(4):
        cols = [bias_k[:, (t - tn) // d] if (t >= tn and (t - tn) % d == 0) else jnp.full((h,), NEG, F32)
                for tn in range(4)]
        rows.append(jnp.concatenate([jnp.full((h, LANES - 4), NEG, F32), jnp.stack(cols, axis=1)], axis=1))
    b2 = jnp.concatenate([jnp.stack(rows, axis=1), jnp.zeros((h, 4, LANES), F32)], axis=1)
    return b1, b2


TILE = 512
RIDE_ON_OPROJ, RIDE_ON_PROJ, RIDE_ON_FFN = 0, 1, 2


def _sample_front(x, mods, hist, caches, rel_bias, weights):
    norm1_g, _, w_in_bf, qg, kg, conv_w, bd = weights[:7]
    sh1, sc1 = mods[:2]
    S = x.shape[1]
    nb = S // 4
    qf, kt, vt, u, co = _proj(x, sc1, sh1, norm1_g, w_in_bf, qg, kg, conv_w, bd, hist,
                              tile=TILE, dils=(1, 1, 1), sample=True)
    rides = []
    for g in range(N_GROUPS):
        d, wb = DILATIONS[g], WINDOWS[g]
        kc, vc = caches[g]
        b1, b2 = _sample_bias(_step_bias(rel_bias, g), wb, d)
        q = qf[0, :, g * GROUP_W:(g + 1) * GROUP_W].reshape(nb, 4, HEADS_PER_GROUP, HEAD_DIM)
        q = jnp.pad(q.transpose(0, 2, 1, 3), ((0, 0), (0, 0), (0, 4), (0, 0))).astype(BF16)
        new_t = lambda a: a[g * GROUP_W:(g + 1) * GROUP_W].reshape(HEADS_PER_GROUP, HEAD_DIM, S)
        rides.append((q, kc, vc, new_t(kt), new_t(vt), b1, b2))
    return dict(x=x, mods=mods, u=u, co=co), rides


def _sample_back(front, ride_outs, weights):
    (_, norm2_g, _, _, _, _, _, w_o_bf, w_r_t, b_r, w_gate_e, w_up_e, w_down_e,
     w_gate_s, w_up_s, w_down_s) = weights
    x, u, co = front["x"], front["u"], front["co"]
    _, _, g1, sh2, sc2, g2 = front["mods"]
    S = x.shape[1]
    nb = S // 4
    states, o_list, l_list = [], [], []
    for g in range(N_GROUPS):
        ko, vo, o, lse = ride_outs[g]
        states += [ko.transpose(0, 3, 1, 2)[None], vo.transpose(0, 3, 1, 2)[None]]
        o_list.append(o[:, :, :4].transpose(0, 2, 1, 3).reshape(1, 1, S, GROUP_W))
        lse = jnp.broadcast_to(lse[:, :, :4, :1], (nb, HEADS_PER_GROUP, 4, HEAD_DIM))
        l_list.append(lse.transpose(0, 2, 1, 3).reshape(1, 1, S, GROUP_W))
    states.append(u.reshape(nb, 4, CONV_CH)[:, 2:][None])
    x1, xg, gates, _, _, _ = _oproj(x, o_list, l_list, co, g1, sc2, sh2, norm2_g, w_o_bf, w_r_t, b_r,
                                    tile=TILE, dils=(1, 1, 1))
    y = _moe(xg, gates, x1, g2, w_gate_e, w_up_e, w_down_e, w_gate_s, w_up_s, w_down_s, tile=TILE)
    return y, states


def _prompt_layer(x, mods, rel_bias, weights, rides):
    (norm1_g, norm2_g, w_in_bf, qg, kg, conv_w, bd, w_o_bf, w_r_t, b_r, w_gate_e, w_up_e, w_down_e,
     w_gate_s, w_up_s, w_down_s) = weights
    sh1, sc1, g1, sh2, sc2, g2 = mods
    B, S, _ = x.shape
    ride_outs = [None] * N_GROUPS
    res = _proj(x, sc1, sh1, norm1_g, w_in_bf, qg, kg, conv_w, bd, None, tile=TILE, dils=DILATIONS, sample=False,
                ride=rides[RIDE_ON_PROJ])
    qkv, tails, (u, co), ride_outs[RIDE_ON_PROJ] = res[:9], res[9:15], res[15:17], res[17:]
    states, o_list, l_list = [], [], []
    for g in range(N_GROUPS):
        d, w = DILATIONS[g], WINDOWS[g]
        q, k, v = (a.reshape(B * d, S // d, GROUP_W) for a in qkv[3 * g:3 * g + 3])
        o, lse = _attn_prompt(q, k, v, _prompt_bias(_step_bias(rel_bias, g)))
        o_list.append(o.reshape(B, d, S // d, GROUP_W))
        l_list.append(lse.reshape(B, d, S // d, GROUP_W))
        keep = lambda a: a.reshape(B, HEADS_PER_GROUP, HEAD_DIM, w).transpose(0, 3, 1, 2)[None]
        states += [keep(tails[2 * g]), keep(tails[2 * g + 1])]
    states.append(u[:, 6:][None])
    x1, xg, _, posr, post, cnt, *ride_outs[RIDE_ON_OPROJ] = _oproj(
        x, o_list, l_list, co, g1, sc2, sh2, norm2_g, w_o_bf, w_r_t, b_r, tile=TILE, dils=DILATIONS,
        ride=rides[RIDE_ON_OPROJ])
    y, ride_outs[RIDE_ON_FFN] = _moe_sparse(xg, post, posr, cnt, x1, g2, w_gate_e, w_up_e, w_down_e, w_gate_s,
                                            w_up_s, w_down_s, rides[RIDE_ON_FFN])
    return y, states, ride_outs


def kernel(x_prompt, x_sample, c_prompt, c_sample, cache_k_w128, cache_v_w128, cache_k_w512, cache_v_w512, cache_k_w2048, cache_v_w2048, state_conv, rel_bias, norm1_g, norm2_g, w_ada, b_ada, w_in, q_norm_g, k_norm_g, conv_w, w_o, w_router, b_router, w_gate_e, w_up_e, w_down_e, w_gate_s, w_up_s, w_down_s):
    B = x_prompt.shape[0]
    DB, T = x_sample.shape[:2]
    n_c = B + DB
    c_all = jnp.pad(jnp.concatenate([c_prompt, c_sample], axis=0), ((0, (-n_c) % 8), (0, 0)))
    mod = _ada(c_all, w_ada[0], b_ada)
    chunks = [mod[:, i * D_MODEL:(i + 1) * D_MODEL] for i in range(6)]
    mods_p = [c[:B].reshape(B, 1, D_MODEL) for c in chunks]
    mods_s = [jnp.repeat(c[B:n_c], T, axis=0).reshape(1, DB * T, D_MODEL) for c in chunks]

    eye = jnp.arange(GROUP_W) // HEAD_DIM
    bd = jnp.where(eye[:, None] == eye[None, :], 1.0 / HEAD_DIM, 0.0).astype(BF16)
    weights = (norm1_g, norm2_g, w_in[0].astype(BF16),
               jnp.tile(q_norm_g, (1, HEADS_PER_GROUP)), jnp.tile(k_norm_g, (1, HEADS_PER_GROUP)),
               conv_w[0], bd, w_o[0].astype(BF16), w_router[0].T, b_router.reshape(N_EXPERTS, 1),
               w_gate_e[0], w_up_e[0], w_down_e[0], w_gate_s[0], w_up_s[0], w_down_s[0])

    s0, s1 = state_conv[0, :, 0], state_conv[0, :, 1]
    zero = jnp.zeros_like(s0)
    hist_a = jnp.stack([s0, s1, zero, zero], axis=1).reshape(DB * T, CONV_CH)
    hist_b = jnp.stack([s1, zero, zero, zero], axis=1).reshape(DB * T, CONV_CH)
    caches = [(ck[0].transpose(0, 2, 3, 1), cv[0].transpose(0, 2, 3, 1))
              for ck, cv in ((cache_k_w128, cache_v_w128), (cache_k_w512, cache_v_w512),
                             (cache_k_w2048, cache_v_w2048))]
    front, ride = _sample_front(x_sample.reshape(1, DB * T, D_MODEL), mods_s, (hist_a, hist_b), caches,
                                rel_bias, weights)
    yp, st_p, ride_out = _prompt_layer(x_prompt, mods_p, rel_bias, weights, ride)
    ys, st_s = _sample_back(front, ride_out, weights)
    return (yp, ys.reshape(DB, T, D_MODEL), *st_p, *st_s)
```

```python
import functools
import math

import jax
import jax.numpy as jnp
from jax import lax
from jax.experimental import pallas as pl
from jax.experimental.pallas import tpu as pltpu

D_MODEL = 1024
HEAD_DIM = 64
HEADS_PER_GROUP = 4
GROUP_W = HEADS_PER_GROUP * HEAD_DIM
WINDOWS = (128, 512, 2048)
DILATIONS = (1, 4, 16)
N_GROUPS = 3
ATTN_W = N_GROUPS * GROUP_W
CONV_CH = 256
N_PROJ = 3 * ATTN_W + 3 * CONV_CH
N_STEPS = 128
N_KEYS = N_STEPS + 1
BLK = 128
N_BUCKETS = 32
MAX_DISTANCE = WINDOWS[-1]
N_EXPERTS = 64
TOP_K = 8
N_ROUTE_GROUPS = 8
GROUP_SIZE = N_EXPERTS // N_ROUTE_GROUPS
TOPK_GROUPS = 4
D_EXPERT = 256
ROUTED_SCALE = 2.5
RMS_EPS = 1e-6
NEG = -1e30
LANES = 128
SUB = 256
CHUNK = 16
XG_W = D_MODEL + LANES
VMEM_LIMIT = 56 * 1024 * 1024

F32 = jnp.float32
BF16 = jnp.bfloat16


def _dot(a, b):
    return jnp.dot(a, b, preferred_element_type=F32)


def _dot_nt(a, b):
    return lax.dot_general(a, b, (((1,), (1,)), ((), ())), preferred_element_type=F32)


def _silu(a):
    return a / (1.0 + jnp.exp(-a))


def _params(sem):
    return pltpu.CompilerParams(dimension_semantics=sem, vmem_limit_bytes=VMEM_LIMIT)


def _ada_kernel(c_ref, w_ref, b_ref, o_ref):
    s = _silu(c_ref[...])
    o_ref[...] = _dot(s.astype(BF16), w_ref[...].astype(BF16)) + b_ref[...]


def _ada(c, w_ada, b_ada):
    n = c.shape[0]
    nc = w_ada.shape[1] // D_MODEL
    return pl.pallas_call(
        _ada_kernel,
        grid=(nc,),
        in_specs=[pl.BlockSpec((n, D_MODEL), lambda j: (0, 0)),
                  pl.BlockSpec((D_MODEL, D_MODEL), lambda j: (0, j)),
                  pl.BlockSpec((1, D_MODEL), lambda j: (0, j))],
        out_specs=pl.BlockSpec((n, D_MODEL), lambda j: (0, j)),
        out_shape=jax.ShapeDtypeStruct((n, w_ada.shape[1]), F32),
        compiler_params=_params(("arbitrary",)),
        name="ada",
    )(c, w_ada, b_ada)


def _proj_kernel(*refs, tile, dils, sample, nt):
    if sample:
        (x_ref, sc_ref, sh_ref, ng_ref, w_ref, qg_ref, kg_ref, cw_ref, bd_ref, ha_ref, hb_ref,
         qf_ref, kt_ref, vt_ref, u_ref, co_ref, u_scr) = refs
    else:
        (x_ref, sc_ref, sh_ref, ng_ref, w_ref, qg_ref, kg_ref, cw_ref, bd_ref) = refs[:9]
        ride_in, refs = refs[9:9 + N_RIDE_IN], refs[9 + N_RIDE_IN:]
        (q0, k0, v0, q1, k1, v1, q2, k2, v2, kt0, vt0, kt1, vt1, kt2, vt2, u_ref, co_ref) = refs[:17]
        ride_out, (slab, u_scr) = refs[17:17 + N_RIDE_OUT], refs[17 + N_RIDE_OUT:]
        qkv_out = ((q0, k0, v0), (q1, k1, v1), (q2, k2, v2))
        tails = ((kt0, vt0), (kt1, vt1), (kt2, vt2))
        j = pl.program_id(1)
        _ride_block(pl.program_id(0) * nt + j, ride_in + ride_out)

    x = x_ref[0]
    ms = jnp.mean(x * x, axis=-1, keepdims=True)
    h = x * lax.rsqrt(ms + RMS_EPS) * ng_ref[...]
    h = h * (1.0 + sc_ref[0]) + sh_ref[0]
    proj = _dot(h.astype(BF16), w_ref[...])

    bd = bd_ref[...]

    def headnorm(z, g):
        msq = _dot((z * z).astype(BF16), bd)
        return z * lax.rsqrt(msq + RMS_EPS) * g

    slab_i = 0
    for g in range(N_GROUPS):
        c0 = g * GROUP_W
        qn = headnorm(proj[:, c0:c0 + GROUP_W], qg_ref[...]) * (HEAD_DIM ** -0.5)
        kn = headnorm(proj[:, ATTN_W + c0:ATTN_W + c0 + GROUP_W], kg_ref[...])
        vv = proj[:, 2 * ATTN_W + c0:2 * ATTN_W + c0 + GROUP_W]
        if sample:
            qf_ref[0, :, c0:c0 + GROUP_W] = qn
            kt_ref[c0:c0 + GROUP_W, :] = kn.T
            vt_ref[c0:c0 + GROUP_W, :] = vv.T
            continue
        keep = min(WINDOWS[g], tile)

        @pl.when(j >= nt - max(WINDOWS[g] // tile, 1))
        def _(kn=kn, vv=vv, g=g, keep=keep):
            tails[g][0][0] = kn[tile - keep:, :].T
            tails[g][1][0] = vv[tile - keep:, :].T

        d = dils[g]
        for val, out in zip((qn, kn, vv), qkv_out[g]):
            if d == 1:
                out[0, 0] = val.astype(BF16)
                continue
            n = tile // d
            for half in range(GROUP_W // LANES):
                slab[slab_i] = val[:, half * LANES:(half + 1) * LANES]
                for r in range(d):
                    out[0, r, :, half * LANES:(half + 1) * LANES] = (
                        slab[slab_i, pl.ds(r, n, stride=d), :].astype(BF16))
                slab_i += 1

    base = 3 * ATTN_W
    u = proj[:, base + 2 * CONV_CH:base + 3 * CONV_CH] * proj[:, base:base + CONV_CH]
    gate_b = proj[:, base + CONV_CH:base + 2 * CONV_CH]
    if sample:
        u_scr[0:8, :] = jnp.zeros((8, CONV_CH), F32)
    else:
        @pl.when(j == 0)
        def _():
            u_scr[0:8, :] = jnp.zeros((8, CONV_CH), F32)

        @pl.when(j > 0)
        def _():
            u_scr[0:8, :] = u_scr[tile:tile + 8, :]

    u_scr[8:tile + 8, :] = u
    um1 = u_scr[7:tile + 7, :]
    um2 = u_scr[6:tile + 6, :]
    if sample:
        t = lax.broadcasted_iota(jnp.int32, (tile, CONV_CH), 0) % 4
        um1 = jnp.where(t >= 1, um1, 0.0) + hb_ref[...]
        um2 = jnp.where(t >= 2, um2, 0.0) + ha_ref[...]
    cw = cw_ref[...]
    conv = cw[0:1] * um2 + cw[1:2] * um1 + cw[2:3] * u
    u_ref[0] = u if sample else u[tile - 8:, :]
    co_ref[0] = (gate_b * conv).astype(BF16)


def _proj(x, sc, sh, ng, w_bf, qg, kg, cw, bd, hist, *, tile, dils, sample, ride=None):
    B, S, _ = x.shape
    nt = S // tile
    tm = sc.shape[1]
    mod_spec = pl.BlockSpec((1, tm, D_MODEL), (lambda b, j: (b, j, 0)) if tm > 1 else (lambda b, j: (b, 0, 0)))
    const2 = lambda shape: pl.BlockSpec(shape, lambda b, j: (0, 0))
    in_specs = [pl.BlockSpec((1, tile, D_MODEL), lambda b, j: (b, j, 0)), mod_spec, mod_spec,
                const2((1, D_MODEL)), const2((D_MODEL, N_PROJ)), const2((1, GROUP_W)),
                const2((1, GROUP_W)), const2((3, CONV_CH)), const2((GROUP_W, GROUP_W))]
    args = [x, sc, sh, ng, w_bf, qg, kg, cw, bd]
    nat = lambda w: pl.BlockSpec((1, tile, w), lambda b, j: (b, j, 0))
    out_specs, out_shape = [], []
    scratch = []
    if sample:
        assert B == 1 and nt == 1
        in_specs += [pl.BlockSpec((tile, CONV_CH), lambda b, j: (j, 0))] * 2
        args += list(hist)
        out_specs += [nat(ATTN_W)] + [pl.BlockSpec((ATTN_W, tile), lambda b, j: (0, j))] * 2 + [nat(CONV_CH)]
        out_shape += [jax.ShapeDtypeStruct((B, S, ATTN_W), F32)]
        out_shape += [jax.ShapeDtypeStruct((ATTN_W, S), F32)] * 2
        out_shape += [jax.ShapeDtypeStruct((B, S, CONV_CH), F32)]
    else:
        for d in dils:
            for _ in range(3):
                out_specs.append(pl.BlockSpec((1, d, tile // d, GROUP_W), lambda b, j: (b, 0, j, 0)))
                out_shape.append(jax.ShapeDtypeStruct((B, d, S // d, GROUP_W), BF16))
        for w in WINDOWS:
            keep, first = min(w, tile), nt - max(w // tile, 1)
            for _ in range(2):
                out_specs.append(pl.BlockSpec((1, GROUP_W, keep),
                                              lambda b, j, first=first: (b, 0, jnp.maximum(j - first, 0))))
                out_shape.append(jax.ShapeDtypeStruct((B, GROUP_W, w), F32))
        out_specs.append(pl.BlockSpec((1, 8, CONV_CH), lambda b, j: (b, 0, 0)))
        out_shape.append(jax.ShapeDtypeStruct((B, 8, CONV_CH), F32))
        n_slabs = sum(3 * (GROUP_W // LANES) for d in dils if d > 1)
        scratch.append(pltpu.VMEM((n_slabs, tile, LANES), F32))
    out_specs.append(nat(CONV_CH))
    out_shape.append(jax.ShapeDtypeStruct((B, S, CONV_CH), BF16))
    scratch.append(pltpu.VMEM((tile + 8, CONV_CH), F32))
    if not sample:
        nb = ride[1].shape[0]
        r_in, r_out, r_shape = _ride_specs(ride, nb // (B * nt), lambda b, j: b * nt + j)
        in_specs, args = in_specs + r_in, args + list(ride)
        out_specs, out_shape = out_specs + r_out, out_shape + r_shape
    return pl.pallas_call(
        functools.partial(_proj_kernel, tile=tile, dils=dils, sample=sample, nt=nt),
        grid=(B, nt),
        in_specs=in_specs,
        out_specs=out_specs,
        out_shape=out_shape,
        scratch_shapes=scratch,
        compiler_params=_params(("arbitrary", "arbitrary")),
        name="proj_sample" if sample else "proj_prompt",
    )(*args)


def _attn_kernel(q_ref, kp_ref, kc_ref, vp_ref, vc_ref, b_ref, o_ref, l_ref, s_scr, m_scr, o_scr, z_scr,
                 *, nq):
    first = pl.program_id(1) == 0
    low = lax.broadcasted_iota(jnp.int32, (BLK, LANES), 1) < HEAD_DIM
    ones = jnp.ones((BLK, LANES), BF16)
    chains = [(j, pair, sub) for j in range(nq) for pair in range(GROUP_W // LANES) for sub in range(2)]

    def operands(j, pair):
        rows = slice(j * BLK, (j + 1) * BLK)
        sl = slice(pair * LANES, (pair + 1) * LANES)
        if j == 0:
            return rows, sl, kp_ref[0, :, sl], vp_ref[0, :, sl]
        prows = slice((j - 1) * BLK, j * BLK)
        return rows, sl, kc_ref[0, prows, sl], vc_ref[0, prows, sl]

    for c, (j, pair, sub) in enumerate(chains):
        rows, sl, kp, _ = operands(j, pair)
        q = q_ref[0, rows, sl]
        qm = jnp.where(low if sub == 0 else ~low, q, jnp.zeros_like(q))
        hh = 2 * pair + sub
        sp = _dot_nt(qm, kp) + b_ref[hh, :, 0:BLK]
        if j == 0:
            sp = jnp.where(first, NEG, sp)
        sc = _dot_nt(qm, kc_ref[0, rows, sl]) + b_ref[hh, :, BLK:2 * BLK]
        s_scr[c, :, 0:BLK] = sp
        s_scr[c, :, BLK:2 * BLK] = sc
        m_scr[c] = jnp.broadcast_to(jnp.max(jnp.maximum(sp, sc), axis=-1, keepdims=True), (BLK, LANES))

    for c, (j, pair, sub) in enumerate(chains):
        rows, sl, _, vp = operands(j, pair)
        m = m_scr[c]
        pp = jnp.exp(s_scr[c, :, 0:BLK] - m).astype(BF16)
        pc = jnp.exp(s_scr[c, :, BLK:2 * BLK] - m).astype(BF16)
        o_scr[c] = _dot(pp, vp) + _dot(pc, vc_ref[0, rows, sl])
        z_scr[c] = _dot(pp, ones) + _dot(pc, ones)

    for c in range(0, len(chains), 2):
        j, pair, _ = chains[c]
        rows = slice(j * BLK, (j + 1) * BLK)
        sl = slice(pair * LANES, (pair + 1) * LANES)
        o_sub = [o_scr[c + sub] / z_scr[c + sub] for sub in range(2)]
        l_sub = [m_scr[c + sub] + jnp.log(z_scr[c + sub]) for sub in range(2)]
        o_ref[0, rows, sl] = jnp.where(low, o_sub[0], o_sub[1])
        l_ref[0, rows, sl] = jnp.where(low, l_sub[0], l_sub[1])


def _attn_prompt(q, k, v, bias):
    Z, L, _ = q.shape
    qt = min(L, 4 * BLK)
    nq = qt // BLK
    cur = pl.BlockSpec((1, qt, GROUP_W), lambda z, i: (z, i, 0))
    prev = pl.BlockSpec((1, BLK, GROUP_W), lambda z, i: (z, jnp.maximum(i * nq - 1, 0), 0))
    return pl.pallas_call(
        functools.partial(_attn_kernel, nq=nq),
        grid=(Z, L // qt),
        in_specs=[cur, prev, cur, prev, cur,
                  pl.BlockSpec((HEADS_PER_GROUP, BLK, 2 * BLK), lambda z, i: (0, 0, 0))],
        out_specs=[cur, cur],
        out_shape=[jax.ShapeDtypeStruct((Z, L, GROUP_W), F32)] * 2,
        scratch_shapes=[pltpu.VMEM((nq * HEADS_PER_GROUP, BLK, 2 * BLK), F32)]
        + [pltpu.VMEM((nq * HEADS_PER_GROUP, BLK, LANES), F32)] * 3,
        compiler_params=_params(("arbitrary", "arbitrary")),
        name="attn_prompt",
    )(q, k, k, v, v, bias)


def _attn_sample_batch(gb, b, q_ref, kc_ref, vc_ref, nk_ref, nv_ref, b1_ref, b2_ref, ko_ref, vo_ref, o_ref,
                       l_ref, heads=range(HEADS_PER_GROUP)):
    wb = kc_ref.shape[-1]
    lane = lax.broadcasted_iota(jnp.int32, (HEAD_DIM, LANES), 1)
    new_cols = lane >= LANES - 4
    per_tile = LANES // 4
    tile_at = pl.ds(pl.multiple_of((gb // per_tile) * LANES, LANES), LANES)
    shift = (LANES - 4) - 4 * (gb % per_tile)
    for hh in heads:
        q = q_ref[b, hh]
        kt = kc_ref[b, hh]
        vt = vc_ref[b, hh]
        nkt = pltpu.roll(nk_ref[hh, :, tile_at], shift, axis=1)
        nvt = pltpu.roll(nv_ref[hh, :, tile_at], shift, axis=1)
        s1 = _dot(q, kt.astype(BF16)) + b1_ref[hh]
        s2 = _dot(q, nkt.astype(BF16)) + b2_ref[hh]
        m = jnp.maximum(jnp.max(s1, axis=-1, keepdims=True), jnp.max(s2, axis=-1, keepdims=True))
        p1 = jnp.exp(s1 - m)
        p2 = jnp.exp(s2 - m)
        z = jnp.sum(p1, axis=-1, keepdims=True) + jnp.sum(p2, axis=-1, keepdims=True)
        o = _dot_nt(p1.astype(BF16), vt.astype(BF16)) + _dot_nt(p2.astype(BF16), nvt.astype(BF16))
        o_ref[b, hh] = o / z
        l_ref[b, hh] = jnp.broadcast_to(m + jnp.log(z), (8, LANES))
        for src, new, dst in ((kt, nkt, ko_ref), (vt, nvt, vo_ref)):
            rolled = pltpu.roll(src, wb - 4, axis=1)
            if wb > LANES:
                dst[b, hh, :, 0:wb - LANES] = rolled[:, 0:wb - LANES]
            dst[b, hh, :, wb - LANES:wb] = jnp.where(new_cols, new, rolled[:, wb - LANES:wb])


N_RIDE_IN = 7
N_RIDE_OUT = 4


def _ride_specs(ride, bb, block_index):
    q, kc, vc, nk, nv, b1, b2 = ride
    nb, _, _, wb = kc.shape
    blk = lambda *tail: pl.BlockSpec((bb, HEADS_PER_GROUP) + tail, lambda *g: (block_index(*g), 0, 0, 0))
    const = lambda a: pl.BlockSpec(a.shape, lambda *g: (0,) * a.ndim)
    in_specs = [blk(8, HEAD_DIM), blk(HEAD_DIM, wb), blk(HEAD_DIM, wb), const(nk), const(nv), const(b1), const(b2)]
    out_specs = [blk(HEAD_DIM, wb), blk(HEAD_DIM, wb), blk(8, HEAD_DIM), blk(8, LANES)]
    out_shape = [jax.ShapeDtypeStruct(kc.shape, F32), jax.ShapeDtypeStruct(kc.shape, F32),
                 jax.ShapeDtypeStruct((nb, HEADS_PER_GROUP, 8, HEAD_DIM), F32),
                 jax.ShapeDtypeStruct((nb, HEADS_PER_GROUP, 8, LANES), F32)]
    return in_specs, out_specs, out_shape


def _ride_block(step, ride_refs):
    bb = ride_refs[0].shape[0]
    for b in range(bb):
        _attn_sample_batch(step * bb + b, b, *ride_refs)


def _first_max(v, ids, sentinel):
    m = jnp.max(v, axis=0, keepdims=True)
    idx = jnp.min(jnp.where(v == m, ids, sentinel), axis=0, keepdims=True)
    return m, ids == idx


def _oproj_kernel(*refs, tile, dils, ride):
    (x_ref, o0, o1, o2, l0, l1, l2, co_ref, g1_ref, sc_ref, sh_ref, ng_ref, wo_ref, wr_ref, br_ref) = refs[:15]
    refs = refs[15:]
    if ride:
        ride_in, refs = refs[:N_RIDE_IN], refs[N_RIDE_IN:]
        ride_out = refs[6:6 + N_RIDE_OUT]
        _ride_block(pl.program_id(0) * pl.num_programs(1) + pl.program_id(1), ride_in + ride_out)
    x1_ref, xg_ref, gt_ref, posr_ref, post_ref, cnt_ref = refs[:6]
    slab = refs[-1]
    o_refs, l_refs = (o0, o1, o2), (l0, l1, l2)
    outs, lses = [], []
    slab_i = 0
    for g, d in enumerate(dils):
        if d == 1:
            outs.append(o_refs[g][0, 0])
            lses.append(l_refs[g][0, 0])
            continue
        n = tile // d
        for ref, dest in ((o_refs[g], outs), (l_refs[g], lses)):
            halves = []
            for half in range(GROUP_W // LANES):
                for r in range(d):
                    slab[slab_i, pl.ds(r, n, stride=d), :] = ref[0, r, :, half * LANES:(half + 1) * LANES]
                halves.append(slab[slab_i])
                slab_i += 1
            dest.append(jnp.concatenate(halves, axis=1))

    m = jnp.maximum(jnp.maximum(lses[0], lses[1]), lses[2])
    es = [jnp.exp(l - m) for l in lses]
    den = es[0] + es[1] + es[2]
    cat = [(outs[g] * (es[g] / den)).astype(BF16) for g in range(N_GROUPS)]
    cat.append(co_ref[0].astype(BF16))
    cat = jnp.concatenate(cat, axis=1)
    x1 = x_ref[0] + g1_ref[0] * _dot(cat, wo_ref[...])
    x1_ref[0] = x1

    ms = jnp.mean(x1 * x1, axis=-1, keepdims=True)
    h2 = x1 * lax.rsqrt(ms + RMS_EPS) * ng_ref[...]
    h2 = h2 * (1.0 + sc_ref[0]) + sh_ref[0]
    hh = h2.astype(BF16)
    xg_ref[0, :, 0:D_MODEL] = hh

    hl = (h2 - hh.astype(F32)).astype(BF16)
    wr = wr_ref[...]
    wh = wr.astype(BF16)
    wl = (wr - wh.astype(F32)).astype(BF16)
    logits = _dot_nt(wh, hh) + _dot_nt(wh, hl) + _dot_nt(wl, hh)
    scores = 1.0 / (1.0 + jnp.exp(-logits))
    sel = scores + br_ref[...]

    ids = lax.broadcasted_iota(jnp.int32, (GROUP_SIZE, tile), 0)
    ninf = -jnp.inf
    sel_g = [sel[g * GROUP_SIZE:(g + 1) * GROUP_SIZE] for g in range(N_ROUTE_GROUPS)]
    gscore = jnp.zeros((N_ROUTE_GROUPS, tile), F32)
    for g in range(N_ROUTE_GROUPS):
        m1, oh = _first_max(sel_g[g], ids, GROUP_SIZE)
        m2 = jnp.max(jnp.where(oh, ninf, sel_g[g]), axis=0, keepdims=True)
        gscore = jnp.where(ids == g, m1 + m2, gscore)
    gsel = jnp.zeros((N_ROUTE_GROUPS, tile), F32)
    for _ in range(TOPK_GROUPS):
        _, oh = _first_max(gscore, ids, N_ROUTE_GROUPS)
        gsel = jnp.where(oh, 1.0, gsel)
        gscore = jnp.where(oh, ninf, gscore)
    cand = [jnp.where(gsel[g:g + 1] > 0.0, sel_g[g], ninf) for g in range(N_ROUTE_GROUPS)]
    chosen = [jnp.zeros((GROUP_SIZE, tile), F32) for _ in range(N_ROUTE_GROUPS)]
    picks = []
    for _ in range(TOP_K):
        mx = cand[0]
        for g in range(1, N_ROUTE_GROUPS):
            mx = jnp.maximum(mx, cand[g])
        mx = jnp.max(mx, axis=0, keepdims=True)
        idx = jnp.where(cand[0] == mx, ids, N_EXPERTS)
        for g in range(1, N_ROUTE_GROUPS):
            idx = jnp.minimum(idx, jnp.where(cand[g] == mx, ids + g * GROUP_SIZE, N_EXPERTS))
        idx = jnp.min(idx, axis=0, keepdims=True)
        ohs = [(ids + g * GROUP_SIZE) == idx for g in range(N_ROUTE_GROUPS)]
        picks.append(ohs)
        for g in range(N_ROUTE_GROUPS):
            chosen[g] = jnp.where(ohs[g], 1.0, chosen[g])
            cand[g] = jnp.where(ohs[g], ninf, cand[g])
    wts = [jnp.where(chosen[g] > 0.0, scores[g * GROUP_SIZE:(g + 1) * GROUP_SIZE], 0.0)
           for g in range(N_ROUTE_GROUPS)]
    wsum = jnp.sum(wts[0], axis=0, keepdims=True)
    for g in range(1, N_ROUTE_GROUPS):
        wsum = wsum + jnp.sum(wts[g], axis=0, keepdims=True)
    gates_t = jnp.concatenate([w / wsum * ROUTED_SCALE for w in wts], axis=0)
    gt_ref[0] = jnp.concatenate([gates_t, jnp.zeros((LANES - N_EXPERTS, tile), F32)], axis=0).T
    g_hi = gates_t.astype(BF16).astype(F32)
    xg_ref[0, :, D_MODEL:D_MODEL + LANES] = jnp.concatenate([g_hi, gates_t - g_hi], axis=0).T.astype(BF16)

    er = lax.broadcasted_iota(jnp.int32, (N_EXPERTS, N_EXPERTS), 0)
    ec = lax.broadcasted_iota(jnp.int32, (N_EXPERTS, N_EXPERTS), 1)
    lower = jnp.where(ec < er, 1.0, 0.0).astype(BF16)
    tr = lax.broadcasted_iota(jnp.int32, (SUB, SUB), 0)
    tc = lax.broadcasted_iota(jnp.int32, (SUB, SUB), 1)
    upper = jnp.where(tr < tc, 1.0, 0.0).astype(BF16)
    for u in range(tile // SUB):
        cols = slice(u * SUB, (u + 1) * SUB)
        ch = jnp.concatenate([c[:, cols] for c in chosen], axis=0)
        cnt = jnp.sum(ch, axis=1, keepdims=True)
        cpad = jnp.broadcast_to(jnp.ceil(cnt / CHUNK) * CHUNK, (N_EXPERTS, LANES))
        cnt_ref[u] = cpad
        seg0 = _dot(lower, cpad.astype(BF16))[:, 0:1]
        slot = seg0 + _dot(ch.astype(BF16), upper)
        rows = []
        for k in range(TOP_K):
            acc = jnp.where(picks[k][0][:, cols], slot[0:GROUP_SIZE], 0.0)
            for g in range(1, N_ROUTE_GROUPS):
                acc = acc + jnp.where(picks[k][g][:, cols], slot[g * GROUP_SIZE:(g + 1) * GROUP_SIZE], 0.0)
            rows.append(jnp.sum(acc, axis=0, keepdims=True))
        posr = jnp.concatenate(rows, axis=0)
        posr_ref[u] = posr
        post_ref[0, cols, :] = jnp.concatenate([posr, jnp.zeros((LANES - TOP_K, SUB), F32)], axis=0).T


def _oproj(x, o_list, l_list, co, g1, sc, sh, ng, wo_bf, wr_t, br, *, tile, dils, ride=None):
    B, S, _ = x.shape
    nt = S // tile
    r_in, r_out, r_shape, r_args = [], [], [], []
    if ride is not None:
        r_in, r_out, r_shape = _ride_specs(ride, ride[1].shape[0] // (B * nt), lambda b, j: b * nt + j)
        r_args = list(ride)
    tm = g1.shape[1]
    mod_spec = pl.BlockSpec((1, tm, D_MODEL), (lambda b, j: (b, j, 0)) if tm > 1 else (lambda b, j: (b, 0, 0)))
    const2 = lambda shape: pl.BlockSpec(shape, lambda b, j: (0, 0))
    nat = lambda w: pl.BlockSpec((1, tile, w), lambda b, j: (b, j, 0))
    dspec = [pl.BlockSpec((1, d, tile // d, GROUP_W), lambda b, j: (b, 0, j, 0)) for d in dils]
    n_slabs = sum(2 * (GROUP_W // LANES) for d in dils if d > 1)
    nsub_t = tile // SUB
    return pl.pallas_call(
        functools.partial(_oproj_kernel, tile=tile, dils=dils, ride=ride is not None),
        grid=(B, nt),
        in_specs=[nat(D_MODEL)] + dspec + dspec + [nat(CONV_CH), mod_spec, mod_spec, mod_spec,
                  const2((1, D_MODEL)), const2((D_MODEL, D_MODEL)), const2((N_EXPERTS, D_MODEL)),
                  const2((N_EXPERTS, 1))] + r_in,
        out_specs=[nat(D_MODEL), nat(XG_W), nat(LANES),
                   pl.BlockSpec((nsub_t, TOP_K, SUB), lambda b, j: (b * nt + j, 0, 0)),
                   nat(LANES),
                   pl.BlockSpec((nsub_t, N_EXPERTS, LANES), lambda b, j: (b * nt + j, 0, 0))] + r_out,
        out_shape=[jax.ShapeDtypeStruct((B, S, D_MODEL), F32), jax.ShapeDtypeStruct((B, S, XG_W), BF16),
                   jax.ShapeDtypeStruct((B, S, LANES), F32),
                   jax.ShapeDtypeStruct((B * S // SUB, TOP_K, SUB), F32),
                   jax.ShapeDtypeStruct((B, S, LANES), F32),
                   jax.ShapeDtypeStruct((B * S // SUB, N_EXPERTS, LANES), F32)] + r_shape,
        scratch_shapes=[pltpu.VMEM((max(n_slabs, 1), tile, LANES), F32)],
        compiler_params=_params(("arbitrary", "arbitrary")),
        name="oproj",
    )(x, *o_list, *l_list, co, g1, sc, sh, ng, wo_bf, wr_t, br, *r_args)


DENSE_EXPERTS_PER_STEP = 2


def _moe_kernel(h_ref, gt_ref, x1_ref, g2_ref, wg_ref, wu_ref, wd_ref, wgs_ref, wus_ref, wds_ref,
                y_ref, acc, *, tile):
    step = pl.program_id(2)
    h = h_ref[0]

    @pl.when(step == 0)
    def _():
        a = _dot(h, wgs_ref[...].astype(BF16))
        b = _dot(h, wus_ref[...].astype(BF16))
        acc[...] = _dot((_silu(a) * b).astype(BF16), wds_ref[...].astype(BF16))

    lane = lax.broadcasted_iota(jnp.int32, (tile, LANES), 1)
    routed = None
    for j in range(DENSE_EXPERTS_PER_STEP):
        e = step * DENSE_EXPERTS_PER_STEP + j
        a = _dot(h, wg_ref[j].astype(BF16))
        b = _dot(h, wu_ref[j].astype(BF16))
        gcol = jnp.sum(jnp.where(lane == e, gt_ref[0], 0.0), axis=1, keepdims=True)
        hc = _silu(a) * b * gcol
        y = _dot(hc.astype(BF16), wd_ref[j].astype(BF16))
        routed = y if routed is None else routed + y
    acc[...] += routed

    @pl.when(step == pl.num_programs(2) - 1)
    def _():
        y_ref[0] = x1_ref[0] + g2_ref[0] * acc[...]


def _moe(h2, gates, x1, g2, wg, wu, wd, wgs, wus, wds, *, tile):
    B, S, _ = h2.shape
    tm = g2.shape[1]
    mod_spec = pl.BlockSpec((1, tm, D_MODEL),
                            (lambda b, j, e: (b, j, 0)) if tm > 1 else (lambda b, j, e: (b, 0, 0)))
    nat = lambda w: pl.BlockSpec((1, tile, w), lambda b, j, e: (b, j, 0))
    const2 = lambda shape: pl.BlockSpec(shape, lambda b, j, e: (0, 0))
    return pl.pallas_call(
        functools.partial(_moe_kernel, tile=tile),
        grid=(B, S // tile, N_EXPERTS // DENSE_EXPERTS_PER_STEP),
        in_specs=[nat(D_MODEL), nat(LANES), nat(D_MODEL), mod_spec,
                  pl.BlockSpec((DENSE_EXPERTS_PER_STEP, D_MODEL, D_EXPERT), lambda b, j, e: (e, 0, 0)),
                  pl.BlockSpec((DENSE_EXPERTS_PER_STEP, D_MODEL, D_EXPERT), lambda b, j, e: (e, 0, 0)),
                  pl.BlockSpec((DENSE_EXPERTS_PER_STEP, D_EXPERT, D_MODEL), lambda b, j, e: (e, 0, 0)),
                  const2((D_MODEL, D_EXPERT)), const2((D_MODEL, D_EXPERT)), const2((D_EXPERT, D_MODEL))],
        out_specs=nat(D_MODEL),
        out_shape=jax.ShapeDtypeStruct((B, S, D_MODEL), F32),
        scratch_shapes=[pltpu.VMEM((tile, D_MODEL), F32)],
        compiler_params=_params(("arbitrary", "arbitrary", "arbitrary")),
        name="moe",
    )(h2, gates, x1, g2, wg, wu, wd, wgs, wus, wds)


FFN_TM = 1024
SUB_ROWS = SUB * TOP_K + N_EXPERTS * CHUNK
N_CHUNKS = SUB_ROWS // CHUNK
ZERO_CHUNK = N_CHUNKS
TABLE_W = 256
MXU_ROWS = 256
BLOCKS_PER_TRIP = 2


def _slot_onehot(first, pos_list, axis_iota):
    shape = axis_iota.shape
    hit = axis_iota == jnp.broadcast_to(pos_list[0] - first, shape).astype(BF16)
    for p in pos_list[1:]:
        hit = hit | (axis_iota == jnp.broadcast_to(p - first, shape).astype(BF16))
    return jnp.where(hit, jnp.ones(shape, BF16), jnp.zeros(shape, BF16))


def _chunk_copy(src, src_chunk, dst, dst_chunk, sem):
    return pltpu.make_async_copy(src.at[src_chunk], dst.at[dst_chunk], sem)


def _for_each(n, body):
    def two(t, carry):
        body(2 * t)
        body(2 * t + 1)
        return carry

    lax.fori_loop(0, n // 2, two, 0)

    @pl.when(n % 2 == 1)
    def _():
        body(n - 1)


def _dispatch_kernel(nblk_ref, nreal_ref, ncopy_ref, ntile_ref, src_ref, dst_ref, posr_ref, xg_ref, xs_hbm,
                     xs_scr, zero_scr, sem, tail_sem, *, nsub, n_tiles_max):
    s = pl.program_id(0)

    tile_chunks = FFN_TM // CHUNK
    blk_chunks = MXU_ROWS // CHUNK
    zero_scr[...] = jnp.zeros((tile_chunks, CHUNK, XG_W), BF16)
    tail_blocks = [ntile_ref[0] + s + r * nsub for r in range(-(-(n_tiles_max) // nsub))]

    def tail_copy(t):
        return pltpu.make_async_copy(
            zero_scr, xs_hbm.at[pl.ds(pl.multiple_of(t * tile_chunks, tile_chunks), tile_chunks)], tail_sem)

    for t in tail_blocks:
        @pl.when(t < n_tiles_max)
        def _():
            tail_copy(t).start()

    xg = xg_ref[...]
    pos = [posr_ref[0, k:k + 1, :] for k in range(TOP_K)]
    rows = lax.broadcasted_iota(jnp.int32, (MXU_ROWS, SUB), 0).astype(BF16)
    xs_scr[ZERO_CHUNK] = jnp.zeros((CHUNK, XG_W), BF16)

    def sort_blocks(i, carry):
        for j in range(BLOCKS_PER_TRIP):
            blk = i * BLOCKS_PER_TRIP + j
            onehot = _slot_onehot((blk * MXU_ROWS).astype(F32), pos, rows)
            sorted_rows = _dot(onehot, xg).astype(BF16)
            xs_scr[pl.ds(pl.multiple_of(blk * blk_chunks, blk_chunks), blk_chunks)] = (
                sorted_rows.reshape(blk_chunks, CHUNK, XG_W))
        k0 = i * trip_chunks
        _for_each(jnp.clip(n_real - k0, 0, trip_chunks), lambda t: send(k0 + t))
        return carry

    n_real = nreal_ref[s]
    n = ncopy_ref[s]
    trip_chunks = BLOCKS_PER_TRIP * blk_chunks

    def send(i):
        _chunk_copy(xs_scr, src_ref[0, 0, i], xs_hbm, dst_ref[0, 0, i], sem).start()

    lax.fori_loop(0, nblk_ref[s], sort_blocks, 0)
    _for_each(n - n_real, lambda t: send(n_real + t))
    _for_each(n, lambda i: _chunk_copy(xs_scr, 0, xs_hbm, 0, sem).wait())
    for t in tail_blocks:
        @pl.when(t < n_tiles_max)
        def _():
            tail_copy(t).wait()


def _dispatch(nblk, n_real, ncopy, n_tiles, src, dst, posr, xg, n_rows):
    nsub = posr.shape[0]
    tab = pl.BlockSpec((1, 1, TABLE_W), lambda s, *_: (s, 0, 0), memory_space=pltpu.SMEM)
    return pl.pallas_call(
        functools.partial(_dispatch_kernel, nsub=nsub, n_tiles_max=n_rows // FFN_TM),
        grid_spec=pltpu.PrefetchScalarGridSpec(
            num_scalar_prefetch=4, grid=(nsub,),
            in_specs=[tab, tab, pl.BlockSpec((1, TOP_K, SUB), lambda s, *_: (s, 0, 0)),
                      pl.BlockSpec((SUB, XG_W), lambda s, *_: (s, 0))],
            out_specs=pl.BlockSpec(memory_space=pl.ANY),
            scratch_shapes=[pltpu.VMEM((N_CHUNKS + 1, CHUNK, XG_W), BF16),
                            pltpu.VMEM((FFN_TM // CHUNK, CHUNK, XG_W), BF16),
                            pltpu.SemaphoreType.DMA(()), pltpu.SemaphoreType.DMA(())]),
        out_shape=jax.ShapeDtypeStruct((n_rows // CHUNK, CHUNK, XG_W), BF16),
        compiler_params=_params(("arbitrary",)),
        name="moe_dispatch",
    )(nblk, n_real, ncopy, n_tiles, src, dst, posr, xg).reshape(n_rows, XG_W)


def _ffn_kernel(te_ref, nt_ref, xs_ref, wg_ref, wu_ref, wd_ref,
                q_ref, kc_ref, vc_ref, nk_ref, nv_ref, b1_ref, b2_ref,
                ys_ref, ko_ref, vo_ref, o_ref, l_ref, wg_s, wu_s, wd_s, *, steps_per_batch):
    i = pl.program_id(0)
    n_heads = HEADS_PER_GROUP // steps_per_batch
    first_head = (i % steps_per_batch) * n_heads

    def ride():
        _attn_sample_batch(i // steps_per_batch, 0, q_ref, kc_ref, vc_ref, nk_ref, nv_ref, b1_ref, b2_ref,
                           ko_ref, vo_ref, o_ref, l_ref, heads=[first_head + j for j in range(n_heads)])

    @pl.when(i >= nt_ref[0])
    def _():
        ride()
        ys_ref[...] = jnp.zeros((FFN_TM, D_MODEL), BF16)

    @pl.when(i < nt_ref[0])
    def _():
        e = te_ref[i]

        @pl.when((i == 0) | (e != te_ref[jnp.maximum(i - 1, 0)]))
        def _():
            wg_s[...] = wg_ref[0].astype(BF16)
            wu_s[...] = wu_ref[0].astype(BF16)
            wd_s[...] = wd_ref[0].astype(BF16)

        ride()
        x = xs_ref[:, 0:D_MODEL]
        g = xs_ref[:, D_MODEL:XG_W].astype(F32)
        lane = lax.broadcasted_iota(jnp.int32, (FFN_TM, LANES), 1)
        gcol = jnp.sum(jnp.where((lane == e) | (lane == e + N_EXPERTS), g, 0.0), axis=1, keepdims=True)
        a = _dot(x, wg_s[...])
        b = _dot(x, wu_s[...])
        hc = _silu(a) * b * gcol
        ys_ref[...] = _dot(hc.astype(BF16), wd_s[...]).astype(BF16)


def _ffn(tile_expert, n_tiles, xs, wg, wu, wd, ride):
    q, kc, vc, nk, nv, b1, b2 = ride
    nb, _, _, wb = kc.shape
    n_rows = xs.shape[0]
    n_steps = n_rows // FFN_TM
    assert n_steps % nb == 0 and HEADS_PER_GROUP % (n_steps // nb) == 0
    per = n_steps // nb
    row = lambda i, te, nt: (jnp.minimum(i, nt[0] - 1), 0)
    wspec = lambda shape: pl.BlockSpec((1,) + shape, lambda i, te, nt: (te[i], 0, 0))
    rblk = lambda *tail: pl.BlockSpec((1, HEADS_PER_GROUP) + tail, lambda i, te, nt: (i // per, 0, 0, 0))
    const = lambda shape: pl.BlockSpec(shape, lambda i, te, nt: (0, 0, 0))
    return pl.pallas_call(
        functools.partial(_ffn_kernel, steps_per_batch=per),
        grid_spec=pltpu.PrefetchScalarGridSpec(
            num_scalar_prefetch=2, grid=(n_steps,),
            in_specs=[pl.BlockSpec((FFN_TM, XG_W), row), wspec((D_MODEL, D_EXPERT)),
                      wspec((D_MODEL, D_EXPERT)), wspec((D_EXPERT, D_MODEL)),
                      rblk(8, HEAD_DIM), rblk(HEAD_DIM, wb), rblk(HEAD_DIM, wb), const(nk.shape), const(nv.shape),
                      const(b1.shape), const(b2.shape)],
            out_specs=[pl.BlockSpec((FFN_TM, D_MODEL), lambda i, te, nt: (i, 0)),
                       rblk(HEAD_DIM, wb), rblk(HEAD_DIM, wb), rblk(8, HEAD_DIM), rblk(8, LANES)],
            scratch_shapes=[pltpu.VMEM((D_MODEL, D_EXPERT), BF16), pltpu.VMEM((D_MODEL, D_EXPERT), BF16),
                            pltpu.VMEM((D_EXPERT, D_MODEL), BF16)]),
        out_shape=[jax.ShapeDtypeStruct((n_rows, D_MODEL), BF16),
                   jax.ShapeDtypeStruct(kc.shape, F32), jax.ShapeDtypeStruct(kc.shape, F32),
                   jax.ShapeDtypeStruct((nb, HEADS_PER_GROUP, 8, HEAD_DIM), F32),
                   jax.ShapeDtypeStruct((nb, HEADS_PER_GROUP, 8, LANES), F32)],
        compiler_params=_params(("arbitrary",)),
        name="moe_ffn",
    )(tile_expert, n_tiles, xs, wg, wu, wd, q, kc, vc, nk, nv, b1, b2)


def _combine_kernel(nblk_ref, ncopy_ref, src_ref, post_ref, xg_ref, x1_ref, g2_ref, wgs_ref, wus_ref,
                    wds_ref, ys_hbm, y_ref, ys_scr, wgs_s, wus_s, wds_s, sem):
    s = pl.program_id(0)
    n = ncopy_ref[s]
    blk_chunks = MXU_ROWS // CHUNK
    trip_chunks = BLOCKS_PER_TRIP * blk_chunks

    def fetch_trip(t, carry):
        k0 = t * trip_chunks
        _for_each(trip_chunks,
                  lambda u: _chunk_copy(ys_hbm, src_ref[0, 0, k0 + u], ys_scr, k0 + u, sem.at[t]).start())
        return carry

    lax.fori_loop(0, nblk_ref[s], fetch_trip, 0)

    @pl.when(s == 0)
    def _():
        wgs_s[...] = wgs_ref[...].astype(BF16)
        wus_s[...] = wus_ref[...].astype(BF16)
        wds_s[...] = wds_ref[...].astype(BF16)

    h = xg_ref[:, 0:D_MODEL]
    a = _dot(h, wgs_s[...])
    b = _dot(h, wus_s[...])
    shared = _dot((_silu(a) * b).astype(BF16), wds_s[...])

    post = post_ref[...]
    pos = [jnp.broadcast_to(post[:, k:k + 1], (SUB, MXU_ROWS)) for k in range(TOP_K)]
    lanes = lax.broadcasted_iota(jnp.int32, (SUB, MXU_ROWS), 1).astype(BF16)

    def gather_blocks(i, acc):
        _for_each(trip_chunks, lambda t: _chunk_copy(ys_hbm, 0, ys_scr, 0, sem.at[i]).wait())
        for j in range(BLOCKS_PER_TRIP):
            blk = i * BLOCKS_PER_TRIP + j
            onehot = _slot_onehot((blk * MXU_ROWS).astype(F32), pos, lanes)
            rows = ys_scr[pl.ds(pl.multiple_of(blk * blk_chunks, blk_chunks), blk_chunks)]
            acc = acc + _dot(onehot, rows.reshape(MXU_ROWS, D_MODEL))
        return acc

    routed = lax.fori_loop(0, nblk_ref[s], gather_blocks, jnp.zeros((SUB, D_MODEL), F32))
    y_ref[...] = x1_ref[...] + g2_ref[0] * (routed + shared)


def _combine(nblk, ncopy, src, post, xg, x1, g2, wgs, wus, wds, ys):
    n_tok = x1.shape[0]
    nsub = n_tok // SUB
    per_b = nsub // g2.shape[0]
    tab = pl.BlockSpec((1, 1, TABLE_W), lambda s, *_: (s, 0, 0), memory_space=pltpu.SMEM)
    const2 = lambda shape: pl.BlockSpec(shape, lambda s, *_: (0, 0))
    return pl.pallas_call(
        _combine_kernel,
        grid_spec=pltpu.PrefetchScalarGridSpec(
            num_scalar_prefetch=2, grid=(nsub,),
            in_specs=[tab, pl.BlockSpec((SUB, LANES), lambda s, *_: (s, 0)),
                      pl.BlockSpec((SUB, XG_W), lambda s, *_: (s, 0)),
                      pl.BlockSpec((SUB, D_MODEL), lambda s, *_: (s, 0)),
                      pl.BlockSpec((1, 1, D_MODEL), lambda s, *_: (s // per_b, 0, 0)),
                      const2((D_MODEL, D_EXPERT)), const2((D_MODEL, D_EXPERT)), const2((D_EXPERT, D_MODEL)),
                      pl.BlockSpec(memory_space=pl.ANY)],
            out_specs=pl.BlockSpec((SUB, D_MODEL), lambda s, *_: (s, 0)),
            scratch_shapes=[pltpu.VMEM((N_CHUNKS, CHUNK, D_MODEL), BF16), pltpu.VMEM((D_MODEL, D_EXPERT), BF16),
                            pltpu.VMEM((D_MODEL, D_EXPERT), BF16), pltpu.VMEM((D_EXPERT, D_MODEL), BF16),
                            pltpu.SemaphoreType.DMA((SUB_ROWS // (BLOCKS_PER_TRIP * MXU_ROWS),))]),
        out_shape=jax.ShapeDtypeStruct((n_tok, D_MODEL), F32),
        compiler_params=_params(("arbitrary",)),
        name="moe_combine",
    )(nblk, ncopy, src, post, xg, x1, g2, wgs, wus, wds, ys.reshape(-1, CHUNK, D_MODEL))


def _route_tables(cnt):
    nsub = cnt.shape[0]
    i32 = jnp.int32
    seg_end = jnp.cumsum(cnt, axis=1)
    seg_start = seg_end - cnt
    tot = jnp.sum(cnt, axis=0)
    tot_al = (tot + FFN_TM - 1) // FFN_TM * FFN_TM
    reg_end = jnp.cumsum(tot_al)
    reg_start = reg_end - tot_al
    base = reg_start[None, :] + jnp.cumsum(cnt, axis=0) - cnt
    n_real = seg_end[:, -1] // CHUNK
    k = jnp.arange(TABLE_W, dtype=i32)
    owner = jnp.sum((seg_end[:, None, :] // CHUNK) <= k[None, :, None], axis=2)
    owner = jnp.minimum(owner, N_EXPERTS - 1)
    delta = (base - seg_start) // CHUNK
    e_ids = jnp.arange(N_EXPERTS, dtype=i32)
    real_dst = jnp.sum(jnp.where(owner[:, :, None] == e_ids, delta[:, None, :], 0), axis=2) + k[None, :]
    fill_n = (tot_al - tot) // CHUNK
    fill_dst0 = (reg_start + tot) // CHUNK
    src = jnp.broadcast_to(k[None, :], (nsub, TABLE_W))
    dst = real_dst
    ncopy = n_real
    for r in range(-(-N_EXPERTS // nsub)):
        e_of = jnp.arange(nsub, dtype=i32) + r * nsub
        ok = e_of < N_EXPERTS
        e_cl = jnp.minimum(e_of, N_EXPERTS - 1)
        fn = jnp.where(ok, fill_n[e_cl], 0)
        j = k[None, :] - ncopy[:, None]
        is_fill = (j >= 0) & (j < fn[:, None])
        src = jnp.where(is_fill, ZERO_CHUNK, src)
        dst = jnp.where(is_fill, fill_dst0[e_cl][:, None] + j, dst)
        ncopy = ncopy + fn
    trip_rows = BLOCKS_PER_TRIP * MXU_ROWS
    disp_nblk = (n_real * CHUNK + trip_rows - 1) // trip_rows
    comb_n = disp_nblk * (trip_rows // CHUNK)
    comb_src = jnp.where(k[None, :] < n_real[:, None], real_dst, real_dst[:, 0:1])
    n_tiles = reg_end[-1] // FFN_TM
    t = jnp.arange((nsub * SUB_ROWS + N_EXPERTS * FFN_TM) // FFN_TM, dtype=i32)
    tile_expert = jnp.minimum(jnp.sum((reg_end[None, :] // FFN_TM) <= t[:, None], axis=1), N_EXPERTS - 1)
    as3 = lambda a: a.astype(i32).reshape(nsub, 1, TABLE_W)
    return (disp_nblk.astype(i32), n_real.astype(i32), ncopy.astype(i32), as3(src), as3(dst),
            comb_n.astype(i32), as3(comb_src), tile_expert.astype(i32), n_tiles.astype(i32).reshape(1))


def _moe_sparse(xg, post, posr, cnt, x1, g2, wg, wu, wd, wgs, wus, wds, ride):
    B, S, _ = x1.shape
    n_tok = B * S
    nsub = n_tok // SUB
    (disp_nblk, n_real, disp_n, disp_src, disp_dst, comb_n, comb_src, tile_expert,
     n_tiles) = _route_tables(cnt[:, :, 0].astype(jnp.int32))
    n_rows = nsub * SUB_ROWS + N_EXPERTS * FFN_TM
    xg2 = xg.reshape(n_tok, XG_W)
    xs = _dispatch(disp_nblk, n_real, disp_n, n_tiles, disp_src, disp_dst, posr, xg2, n_rows)
    ys, *ride_out = _ffn(tile_expert, n_tiles, xs, wg, wu, wd, ride)
    y = _combine(disp_nblk, comb_n, comb_src, post.reshape(n_tok, LANES), xg2, x1.reshape(n_tok, D_MODEL),
                 g2, wgs, wus, wds, ys)
    return y.reshape(B, S, D_MODEL), ride_out


def _t5_bucket(dist):
    max_exact = N_BUCKETS // 2
    df = jnp.maximum(dist, 1).astype(F32)
    large = max_exact + (jnp.log(df / max_exact) / math.log(MAX_DISTANCE / max_exact)
                         * (N_BUCKETS - max_exact)).astype(jnp.int32)
    large = jnp.minimum(large, N_BUCKETS - 1)
    return jnp.where(dist < max_exact, dist, large)


def _step_bias(rel_bias, g):
    dist = jnp.arange(N_KEYS, dtype=jnp.int32) * DILATIONS[g]
    cols = rel_bias[:, g * HEADS_PER_GROUP:(g + 1) * HEADS_PER_GROUP]
    return cols[_t5_bucket(dist)].T.astype(F32)


def _prompt_bias(bias_k):
    h = bias_k.shape[0]
    n = 3 * BLK
    row = jnp.concatenate([bias_k[:, ::-1], jnp.full((h, n - N_KEYS), NEG, F32)], axis=1)
    t = jnp.tile(row, (1, BLK))[:, :BLK * (n - 1)].reshape(h, BLK, n - 1)
    return t[:, :, :2 * BLK]


def _sample_bias(bias_k, wb, d):
    h = bias_k.shape[0]
    rev = bias_k[:, :0:-1]
    rows = []
    for t in range(4):
        if d == 1:
            rows.append(jnp.concatenate([jnp.full((h, t), NEG, F32), rev[:, :wb - t]], axis=1))
        else:
            r = jnp.arange(d)[None, None, :]
            rows.append(jnp.where(r == t, rev[:, :, None], NEG).reshape(h, wb))
    b1 = jnp.concatenate([jnp.stack(rows, axis=1), jnp.zeros((h, 4, wb), F32)], axis=1)
    rows = []
    for t in range(4):
        cols = [bias_k[:, (t - tn) // d] if (t >= tn and (t - tn) % d == 0) else jnp.full((h,), NEG, F32)
                for tn in range(4)]
        rows.append(jnp.concatenate([jnp.full((h, LANES - 4), NEG, F32), jnp.stack(cols, axis=1)], axis=1))
    b2 = jnp.concatenate([jnp.stack(rows, axis=1), jnp.zeros((h, 4, LANES), F32)], axis=1)
    return b1, b2


TILE = 512
RIDE_ON_OPROJ, RIDE_ON_PROJ, RIDE_ON_FFN = 0, 1, 2


def _sample_front(x, mods, hist, caches, rel_bias, weights):
    norm1_g, _, w_in_bf, qg, kg, conv_w, bd = weights[:7]
    sh1, sc1 = mods[:2]
    S = x.shape[1]
    nb = S // 4
    qf, kt, vt, u, co = _proj(x, sc1, sh1, norm1_g, w_in_bf, qg, kg, conv_w, bd, hist,
                              tile=TILE, dils=(1, 1, 1), sample=True)
    rides = []
    for g in range(N_GROUPS):
        d, wb = DILATIONS[g], WINDOWS[g]
        kc, vc = caches[g]
        b1, b2 = _sample_bias(_step_bias(rel_bias, g), wb, d)
        q = qf[0, :, g * GROUP_W:(g + 1) * GROUP_W].reshape(nb, 4, HEADS_PER_GROUP, HEAD_DIM)
        q = jnp.pad(q.transpose(0, 2, 1, 3), ((0, 0), (0, 0), (0, 4), (0, 0))).astype(BF16)
        new_t = lambda a: a[g * GROUP_W:(g + 1) * GROUP_W].reshape(HEADS_PER_GROUP, HEAD_DIM, S)
        rides.append((q, kc, vc, new_t(kt), new_t(vt), b1, b2))
    return dict(x=x, mods=mods, u=u, co=co), rides


def _sample_back(front, ride_outs, weights):
    (_, norm2_g, _, _, _, _, _, w_o_bf, w_r_t, b_r, w_gate_e, w_up_e, w_down_e,
     w_gate_s, w_up_s, w_down_s) = weights
    x, u, co = front["x"], front["u"], front["co"]
    _, _, g1, sh2, sc2, g2 = front["mods"]
    S = x.shape[1]
    nb = S // 4
    states, o_list, l_list = [], [], []
    for g in range(N_GROUPS):
        ko, vo, o, lse = ride_outs[g]
        states += [ko.transpose(0, 3, 1, 2)[None], vo.transpose(0, 3, 1, 2)[None]]
        o_list.append(o[:, :, :4].transpose(0, 2, 1, 3).reshape(1, 1, S, GROUP_W))
        lse = jnp.broadcast_to(lse[:, :, :4, :1], (nb, HEADS_PER_GROUP, 4, HEAD_DIM))
        l_list.append(lse.transpose(0, 2, 1, 3).reshape(1, 1, S, GROUP_W))
    states.append(u.reshape(nb, 4, CONV_CH)[:, 2:][None])
    x1, xg, gates, _, _, _ = _oproj(x, o_list, l_list, co, g1, sc2, sh2, norm2_g, w_o_bf, w_r_t, b_r,
                                    tile=TILE, dils=(1, 1, 1))
    y = _moe(xg, gates, x1, g2, w_gate_e, w_up_e, w_down_e, w_gate_s, w_up_s, w_down_s, tile=TILE)
    return y, states


def _prompt_layer(x, mods, rel_bias, weights, rides):
    (norm1_g, norm2_g, w_in_bf, qg, kg, conv_w, bd, w_o_bf, w_r_t, b_r, w_gate_e, w_up_e, w_down_e,
     w_gate_s, w_up_s, w_down_s) = weights
    sh1, sc1, g1, sh2, sc2, g2 = mods
    B, S, _ = x.shape
    ride_outs = [None] * N_GROUPS
    res = _proj(x, sc1, sh1, norm1_g, w_in_bf, qg, kg, conv_w, bd, None, tile=TILE, dils=DILATIONS, sample=False,
                ride=rides[RIDE_ON_PROJ])
    qkv, tails, (u, co), ride_outs[RIDE_ON_PROJ] = res[:9], res[9:15], res[15:17], res[17:]
    states, o_list, l_list = [], [], []
    for g in range(N_GROUPS):
        d, w = DILATIONS[g], WINDOWS[g]
        q, k, v = (a.reshape(B * d, S // d, GROUP_W) for a in qkv[3 * g:3 * g + 3])
        o, lse = _attn_prompt(q, k, v, _prompt_bias(_step_bias(rel_bias, g)))
        o_list.append(o.reshape(B, d, S // d, GROUP_W))
        l_list.append(lse.reshape(B, d, S // d, GROUP_W))
        keep = lambda a: a.reshape(B, HEADS_PER_GROUP, HEAD_DIM, w).transpose(0, 3, 1, 2)[None]
        states += [keep(tails[2 * g]), keep(tails[2 * g + 1])]
    states.append(u[:, 6:][None])
    x1, xg, _, posr, post, cnt, *ride_outs[RIDE_ON_OPROJ] = _oproj(
        x, o_list, l_list, co, g1, sc2, sh2, norm2_g, w_o_bf, w_r_t, b_r, tile=TILE, dils=DILATIONS,
        ride=rides[RIDE_ON_OPROJ])
    y, ride_outs[RIDE_ON_FFN] = _moe_sparse(xg, post, posr, cnt, x1, g2, w_gate_e, w_up_e, w_down_e, w_gate_s,
                                            w_up_s, w_down_s, rides[RIDE_ON_FFN])
    return y, states, ride_outs


def kernel(x_prompt, x_sample, c_prompt, c_sample, cache_k_w128, cache_v_w128, cache_k_w512, cache_v_w512, cache_k_w2048, cache_v_w2048, state_conv, rel_bias, norm1_g, norm2_g, w_ada, b_ada, w_in, q_norm_g, k_norm_g, conv_w, w_o, w_router, b_router, w_gate_e, w_up_e, w_down_e, w_gate_s, w_up_s, w_down_s):
    B = x_prompt.shape[0]
    DB, T = x_sample.shape[:2]
    n_c = B + DB
    c_all = jnp.pad(jnp.concatenate([c_prompt, c_sample], axis=0), ((0, (-n_c) % 8), (0, 0)))
    mod = _ada(c_all, w_ada[0], b_ada)
    chunks = [mod[:, i * D_MODEL:(i + 1) * D_MODEL] for i in range(6)]
    mods_p = [c[:B].reshape(B, 1, D_MODEL) for c in chunks]
    mods_s = [jnp.repeat(c[B:n_c], T, axis=0).reshape(1, DB * T, D_MODEL) for c in chunks]

    eye = jnp.arange(GROUP_W) // HEAD_DIM
    bd = jnp.where(eye[:, None] == eye[None, :], 1.0 / HEAD_DIM, 0.0).astype(BF16)
    weights = (norm1_g, norm2_g, w_in[0].astype(BF16),
               jnp.tile(q_norm_g, (1, HEADS_PER_GROUP)), jnp.tile(k_norm_g, (1, HEADS_PER_GROUP)),
               conv_w[0], bd, w_o[0].astype(BF16), w_router[0].T, b_router.reshape(N_EXPERTS, 1),
               w_gate_e[0], w_up_e[0], w_down_e[0], w_gate_s[0], w_up_s[0], w_down_s[0])

    s0, s1 = state_conv[0, :, 0], state_conv[0, :, 1]
    zero = jnp.zeros_like(s0)
    hist_a = jnp.stack([s0, s1, zero, zero], axis=1).reshape(DB * T, CONV_CH)
    hist_b = jnp.stack([s1, zero, zero, zero], axis=1).reshape(DB * T, CONV_CH)
    caches = [(ck[0].transpose(0, 2, 3, 1), cv[0].transpose(0, 2, 3, 1))
              for ck, cv in ((cache_k_w128, cache_v_w128), (cache_k_w512, cache_v_w512),
                             (cache_k_w2048, cache_v_w2048))]
    front, ride = _sample_front(x_sample.reshape(1, DB * T, D_MODEL), mods_s, (hist_a, hist_b), caches,
                                rel_bias, weights)
    yp, st_p, ride_out = _prompt_layer(x_prompt, mods_p, rel_bias, weights, ride)
    ys, st_s = _sample_back(front, ride_out, weights)
    return (yp, ys.reshape(DB, T, D_MODEL), *st_p, *st_s)
```

```python
import functools
import math

import jax
import jax.numpy as jnp
from jax import lax
from jax.experimental import pallas as pl
from jax.experimental.pallas import tpu as pltpu

D_MODEL = 1024
HEAD_DIM = 64
HEADS_PER_GROUP = 4
GROUP_W = HEADS_PER_GROUP * HEAD_DIM
WINDOWS = (128, 512, 2048)
DILATIONS = (1, 4, 16)
N_GROUPS = 3
ATTN_W = N_GROUPS * GROUP_W
CONV_CH = 256
N_PROJ = 3 * ATTN_W + 3 * CONV_CH
N_STEPS = 128
N_KEYS = N_STEPS + 1
BLK = 128
N_BUCKETS = 32
MAX_DISTANCE = WINDOWS[-1]
N_EXPERTS = 64
TOP_K = 8
N_ROUTE_GROUPS = 8
GROUP_SIZE = N_EXPERTS // N_ROUTE_GROUPS
TOPK_GROUPS = 4
D_EXPERT = 256
ROUTED_SCALE = 2.5
RMS_EPS = 1e-6
NEG = -1e30
LANES = 128
SUB = 256
CHUNK = 16
XG_W = D_MODEL + LANES
VMEM_LIMIT = 56 * 1024 * 1024

F32 = jnp.float32
BF16 = jnp.bfloat16


def _dot(a, b):
    return jnp.dot(a, b, preferred_element_type=F32)


def _dot_nt(a, b):
    return lax.dot_general(a, b, (((1,), (1,)), ((), ())), preferred_element_type=F32)


def _silu(a):
    return a / (1.0 + jnp.exp(-a))


def _params(sem):
    return pltpu.CompilerParams(dimension_semantics=sem, vmem_limit_bytes=VMEM_LIMIT)


def _ada_kernel(c_ref, w_ref, b_ref, o_ref):
    s = _silu(c_ref[...])
    o_ref[...] = _dot(s.astype(BF16), w_ref[...].astype(BF16)) + b_ref[...]


def _ada(c, w_ada, b_ada):
    n = c.shape[0]
    nc = w_ada.shape[1] // D_MODEL
    return pl.pallas_call(
        _ada_kernel,
        grid=(nc,),
        in_specs=[pl.BlockSpec((n, D_MODEL), lambda j: (0, 0)),
                  pl.BlockSpec((D_MODEL, D_MODEL), lambda j: (0, j)),
                  pl.BlockSpec((1, D_MODEL), lambda j: (0, j))],
        out_specs=pl.BlockSpec((n, D_MODEL), lambda j: (0, j)),
        out_shape=jax.ShapeDtypeStruct((n, w_ada.shape[1]), F32),
        compiler_params=_params(("arbitrary",)),
        name="ada",
    )(c, w_ada, b_ada)


def _proj_kernel(*refs, tile, dils, sample, nt):
    if sample:
        (x_ref, sc_ref, sh_ref, ng_ref, w_ref, qg_ref, kg_ref, cw_ref, bd_ref, ha_ref, hb_ref,
         qf_ref, kt_ref, vt_ref, u_ref, co_ref, u_scr) = refs
    else:
        (x_ref, sc_ref, sh_ref, ng_ref, w_ref, qg_ref, kg_ref, cw_ref, bd_ref) = refs[:9]
        ride_in, refs = refs[9:9 + N_RIDE_IN], refs[9 + N_RIDE_IN:]
        (q0, k0, v0, q1, k1, v1, q2, k2, v2, kt0, vt0, kt1, vt1, kt2, vt2, u_ref, co_ref) = refs[:17]
        ride_out, (slab, u_scr) = refs[17:17 + N_RIDE_OUT], refs[17 + N_RIDE_OUT:]
        qkv_out = ((q0, k0, v0), (q1, k1, v1), (q2, k2, v2))
        tails = ((kt0, vt0), (kt1, vt1), (kt2, vt2))
        j = pl.program_id(1)
        _ride_block(pl.program_id(0) * nt + j, ride_in + ride_out)

    x = x_ref[0]
    ms = jnp.mean(x * x, axis=-1, keepdims=True)
    h = x * lax.rsqrt(ms + RMS_EPS) * ng_ref[...]
    h = h * (1.0 + sc_ref[0]) + sh_ref[0]
    proj = _dot(h.astype(BF16), w_ref[...])

    bd = bd_ref[...]

    def headnorm(z, g):
        msq = _dot((z * z).astype(BF16), bd)
        return z * lax.rsqrt(msq + RMS_EPS) * g

    slab_i = 0
    for g in range(N_GROUPS):
        c0 = g * GROUP_W
        qn = headnorm(proj[:, c0:c0 + GROUP_W], qg_ref[...]) * (HEAD_DIM ** -0.5)
        kn = headnorm(proj[:, ATTN_W + c0:ATTN_W + c0 + GROUP_W], kg_ref[...])
        vv = proj[:, 2 * ATTN_W + c0:2 * ATTN_W + c0 + GROUP_W]
        if sample:
            qf_ref[0, :, c0:c0 + GROUP_W] = qn
            kt_ref[c0:c0 + GROUP_W, :] = kn.T
            vt_ref[c0:c0 + GROUP_W, :] = vv.T
            continue
        keep = min(WINDOWS[g], tile)

        @pl.when(j >= nt - max(WINDOWS[g] // tile, 1))
        def _(kn=kn, vv=vv, g=g, keep=keep):
            tails[g][0][0] = kn[tile - keep:, :].T
            tails[g][1][0] = vv[tile - keep:, :].T

        d = dils[g]
        for val, out in zip((qn, kn, vv), qkv_out[g]):
            if d == 1:
                out[0, 0] = val.astype(BF16)
                continue
            n = tile // d
            for half in range(GROUP_W // LANES):
                slab[slab_i] = val[:, half * LANES:(half + 1) * LANES]
                for r in range(d):
                    out[0, r, :, half * LANES:(half + 1) * LANES] = (
                        slab[slab_i, pl.ds(r, n, stride=d), :].astype(BF16))
                slab_i += 1

    base = 3 * ATTN_W
    u = proj[:, base + 2 * CONV_CH:base + 3 * CONV_CH] * proj[:, base:base + CONV_CH]
    gate_b = proj[:, base + CONV_CH:base + 2 * CONV_CH]
    if sample:
        u_scr[0:8, :] = jnp.zeros((8, CONV_CH), F32)
    else:
        @pl.when(j == 0)
        def _():
            u_scr[0:8, :] = jnp.zeros((8, CONV_CH), F32)

        @pl.when(j > 0)
        def _():
            u_scr[0:8, :] = u_scr[tile:tile + 8, :]

    u_scr[8:tile + 8, :] = u
    um1 = u_scr[7:tile + 7, :]
    um2 = u_scr[6:tile + 6, :]
    if sample:
        t = lax.broadcasted_iota(jnp.int32, (tile, CONV_CH), 0) % 4
        um1 = jnp.where(t >= 1, um1, 0.0) + hb_ref[...]
        um2 = jnp.where(t >= 2, um2, 0.0) + ha_ref[...]
    cw = cw_ref[...]
    conv = cw[0:1] * um2 + cw[1:2] * um1 + cw[2:3] * u
    u_ref[0] = u if sample else u[tile - 8:, :]
    co_ref[0] = (gate_b * conv).astype(BF16)


def _proj(x, sc, sh, ng, w_bf, qg, kg, cw, bd, hist, *, tile, dils, sample, ride=None):
    B, S, _ = x.shape
    nt = S // tile
    tm = sc.shape[1]
    mod_spec = pl.BlockSpec((1, tm, D_MODEL), (lambda b, j: (b, j, 0)) if tm > 1 else (lambda b, j: (b, 0, 0)))
    const2 = lambda shape: pl.BlockSpec(shape, lambda b, j: (0, 0))
    in_specs = [pl.BlockSpec((1, tile, D_MODEL), lambda b, j: (b, j, 0)), mod_spec, mod_spec,
                const2((1, D_MODEL)), const2((D_MODEL, N_PROJ)), const2((1, GROUP_W)),
                const2((1, GROUP_W)), const2((3, CONV_CH)), const2((GROUP_W, GROUP_W))]
    args = [x, sc, sh, ng, w_bf, qg, kg, cw, bd]
    nat = lambda w: pl.BlockSpec((1, tile, w), lambda b, j: (b, j, 0))
    out_specs, out_shape = [], []
    scratch = []
    if sample:
        assert B == 1 and nt == 1
        in_specs += [pl.BlockSpec((tile, CONV_CH), lambda b, j: (j, 0))] * 2
        args += list(hist)
        out_specs += [nat(ATTN_W)] + [pl.BlockSpec((ATTN_W, tile), lambda b, j: (0, j))] * 2 + [nat(CONV_CH)]
        out_shape += [jax.ShapeDtypeStruct((B, S, ATTN_W), F32)]
        out_shape += [jax.ShapeDtypeStruct((ATTN_W, S), F32)] * 2
        out_shape += [jax.ShapeDtypeStruct((B, S, CONV_CH), F32)]
    else:
        for d in dils:
            for _ in range(3):
                out_specs.append(pl.BlockSpec((1, d, tile // d, GROUP_W), lambda b, j: (b, 0, j, 0)))
                out_shape.append(jax.ShapeDtypeStruct((B, d, S // d, GROUP_W), BF16))
        for w in WINDOWS:
            keep, first = min(w, tile), nt - max(w // tile, 1)
            for _ in range(2):
                out_specs.append(pl.BlockSpec((1, GROUP_W, keep),
                                              lambda b, j, first=first: (b, 0, jnp.maximum(j - first, 0))))
                out_shape.append(jax.ShapeDtypeStruct((B, GROUP_W, w), F32))
        out_specs.append(pl.BlockSpec((1, 8, CONV_CH), lambda b, j: (b, 0, 0)))
        out_shape.append(jax.ShapeDtypeStruct((B, 8, CONV_CH), F32))
        n_slabs = sum(3 * (GROUP_W // LANES) for d in dils if d > 1)
        scratch.append(pltpu.VMEM((n_slabs, tile, LANES), F32))
    out_specs.append(nat(CONV_CH))
    out_shape.append(jax.ShapeDtypeStruct((B, S, CONV_CH), BF16))
    scratch.append(pltpu.VMEM((tile + 8, CONV_CH), F32))
    if not sample:
        nb = ride[1].shape[0]
        r_in, r_out, r_shape = _ride_specs(ride, nb // (B * nt), lambda b, j: b * nt + j)
        in_specs, args = in_specs + r_in, args + list(ride)
        out_specs, out_shape = out_specs + r_out, out_shape + r_shape
    return pl.pallas_call(
        functools.partial(_proj_kernel, tile=tile, dils=dils, sample=sample, nt=nt),
        grid=(B, nt),
        in_specs=in_specs,
        out_specs=out_specs,
        out_shape=out_shape,
        scratch_shapes=scratch,
        compiler_params=_params(("arbitrary", "arbitrary")),
        name="proj_sample" if sample else "proj_prompt",
    )(*args)


def _attn_kernel(q_ref, kp_ref, kc_ref, vp_ref, vc_ref, b_ref, o_ref, l_ref, s_scr, m_scr, o_scr, z_scr,
                 *, nq):
    first = pl.program_id(1) == 0
    low = lax.broadcasted_iota(jnp.int32, (BLK, LANES), 1) < HEAD_DIM
    ones = jnp.ones((BLK, LANES), BF16)
    chains = [(j, pair, sub) for j in range(nq) for pair in range(GROUP_W // LANES) for sub in range(2)]

    def operands(j, pair):
        rows = slice(j * BLK, (j + 1) * BLK)
        sl = slice(pair * LANES, (pair + 1) * LANES)
        if j == 0:
            return rows, sl, kp_ref[0, :, sl], vp_ref[0, :, sl]
        prows = slice((j - 1) * BLK, j * BLK)
        return rows, sl, kc_ref[0, prows, sl], vc_ref[0, prows, sl]

    for c, (j, pair, sub) in enumerate(chains):
        rows, sl, kp, _ = operands(j, pair)
        q = q_ref[0, rows, sl]
        qm = jnp.where(low if sub == 0 else ~low, q, jnp.zeros_like(q))
        hh = 2 * pair + sub
        sp = _dot_nt(qm, kp) + b_ref[hh, :, 0:BLK]
        if j == 0:
            sp = jnp.where(first, NEG, sp)
        sc = _dot_nt(qm, kc_ref[0, rows, sl]) + b_ref[hh, :, BLK:2 * BLK]
        s_scr[c, :, 0:BLK] = sp
        s_scr[c, :, BLK:2 * BLK] = sc
        m_scr[c] = jnp.broadcast_to(jnp.max(jnp.maximum(sp, sc), axis=-1, keepdims=True), (BLK, LANES))

    for c, (j, pair, sub) in enumerate(chains):
        rows, sl, _, vp = operands(j, pair)
        m = m_scr[c]
        pp = jnp.exp(s_scr[c, :, 0:BLK] - m).astype(BF16)
        pc = jnp.exp(s_scr[c, :, BLK:2 * BLK] - m).astype(BF16)
        o_scr[c] = _dot(pp, vp) + _dot(pc, vc_ref[0, rows, sl])
        z_scr[c] = _dot(pp, ones) + _dot(pc, ones)

    for c in range(0, len(chains), 2):
        j, pair, _ = chains[c]
        rows = slice(j * BLK, (j + 1) * BLK)
        sl = slice(pair * LANES, (pair + 1) * LANES)
        o_sub = [o_scr[c + sub] / z_scr[c + sub] for sub in range(2)]
        l_sub = [m_scr[c + sub] + jnp.log(z_scr[c + sub]) for sub in range(2)]
        o_ref[0, rows, sl] = jnp.where(low, o_sub[0], o_sub[1])
        l_ref[0, rows, sl] = jnp.where(low, l_sub[0], l_sub[1])


def _attn_prompt(q, k, v, bias):
    Z, L, _ = q.shape
    qt = min(L, 4 * BLK)
    nq = qt // BLK
    cur = pl.BlockSpec((1, qt, GROUP_W), lambda z, i: (z, i, 0))
    prev = pl.BlockSpec((1, BLK, GROUP_W), lambda z, i: (z, jnp.maximum(i * nq - 1, 0), 0))
    return pl.pallas_call(
        functools.partial(_attn_kernel, nq=nq),
        grid=(Z, L // qt),
        in_specs=[cur, prev, cur, prev, cur,
                  pl.BlockSpec((HEADS_PER_GROUP, BLK, 2 * BLK), lambda z, i: (0, 0, 0))],
        out_specs=[cur, cur],
        out_shape=[jax.ShapeDtypeStruct((Z, L, GROUP_W), F32)] * 2,
        scratch_shapes=[pltpu.VMEM((nq * HEADS_PER_GROUP, BLK, 2 * BLK), F32)]
        + [pltpu.VMEM((nq * HEADS_PER_GROUP, BLK, LANES), F32)] * 3,
        compiler_params=_params(("arbitrary", "arbitrary")),
        name="attn_prompt",
    )(q, k, k, v, v, bias)


def _attn_sample_batch(gb, b, q_ref, kc_ref, vc_ref, nk_ref, nv_ref, b1_ref, b2_ref, ko_ref, vo_ref, o_ref,
                       l_ref, heads=range(HEADS_PER_GROUP)):
    wb = kc_ref.shape[-1]
    lane = lax.broadcasted_iota(jnp.int32, (HEAD_DIM, LANES), 1)
    new_cols = lane >= LANES - 4
    per_tile = LANES // 4
    tile_at = pl.ds(pl.multiple_of((gb // per_tile) * LANES, LANES), LANES)
    shift = (LANES - 4) - 4 * (gb % per_tile)
    for hh in heads:
        q = q_ref[b, hh]
        kt = kc_ref[b, hh]
        vt = vc_ref[b, hh]
        nkt = pltpu.roll(nk_ref[hh, :, tile_at], shift, axis=1)
        nvt = pltpu.roll(nv_ref[hh, :, tile_at], shift, axis=1)
        s1 = _dot(q, kt.astype(BF16)) + b1_ref[hh]
        s2 = _dot(q, nkt.astype(BF16)) + b2_ref[hh]
        m = jnp.maximum(jnp.max(s1, axis=-1, keepdims=True), jnp.max(s2, axis=-1, keepdims=True))
        p1 = jnp.exp(s1 - m)
        p2 = jnp.exp(s2 - m)
        z = jnp.sum(p1, axis=-1, keepdims=True) + jnp.sum(p2, axis=-1, keepdims=True)
        o = _dot_nt(p1.astype(BF16), vt.astype(BF16)) + _dot_nt(p2.astype(BF16), nvt.astype(BF16))
        o_ref[b, hh] = o / z
        l_ref[b, hh] = jnp.broadcast_to(m + jnp.log(z), (8, LANES))
        for src, new, dst in ((kt, nkt, ko_ref), (vt, nvt, vo_ref)):
            rolled = pltpu.roll(src, wb - 4, axis=1)
            if wb > LANES:
                dst[b, hh, :, 0:wb - LANES] = rolled[:, 0:wb - LANES]
            dst[b, hh, :, wb - LANES:wb] = jnp.where(new_cols, new, rolled[:, wb - LANES:wb])


N_RIDE_IN = 7
N_RIDE_OUT = 4


def _ride_specs(ride, bb, block_index):
    q, kc, vc, nk, nv, b1, b2 = ride
    nb, _, _, wb = kc.shape
    blk = lambda *tail: pl.BlockSpec((bb, HEADS_PER_GROUP) + tail, lambda *g: (block_index(*g), 0, 0, 0))
    const = lambda a: pl.BlockSpec(a.shape, lambda *g: (0,) * a.ndim)
    in_specs = [blk(8, HEAD_DIM), blk(HEAD_DIM, wb), blk(HEAD_DIM, wb), const(nk), const(nv), const(b1), const(b2)]
    out_specs = [blk(HEAD_DIM, wb), blk(HEAD_DIM, wb), blk(8, HEAD_DIM), blk(8, LANES)]
    out_shape = [jax.ShapeDtypeStruct(kc.shape, F32), jax.ShapeDtypeStruct(kc.shape, F32),
                 jax.ShapeDtypeStruct((nb, HEADS_PER_GROUP, 8, HEAD_DIM), F32),
                 jax.ShapeDtypeStruct((nb, HEADS_PER_GROUP, 8, LANES), F32)]
    return in_specs, out_specs, out_shape


def _ride_block(step, ride_refs):
    bb = ride_refs[0].shape[0]
    for b in range(bb):
        _attn_sample_batch(step * bb + b, b, *ride_refs)


def _first_max(v, ids, sentinel):
    m = jnp.max(v, axis=0, keepdims=True)
    idx = jnp.min(jnp.where(v == m, ids, sentinel), axis=0, keepdims=True)
    return m, ids == idx


def _oproj_kernel(*refs, tile, dils, ride):
    (x_ref, o0, o1, o2, l0, l1, l2, co_ref, g1_ref, sc_ref, sh_ref, ng_ref, wo_ref, wr_ref, br_ref) = refs[:15]
    refs = refs[15:]
    if ride:
        ride_in, refs = refs[:N_RIDE_IN], refs[N_RIDE_IN:]
        ride_out = refs[6:6 + N_RIDE_OUT]
        _ride_block(pl.program_id(0) * pl.num_programs(1) + pl.program_id(1), ride_in + ride_out)
    x1_ref, xg_ref, gt_ref, posr_ref, post_ref, cnt_ref = refs[:6]
    slab = refs[-1]
    o_refs, l_refs = (o0, o1, o2), (l0, l1, l2)
    outs, lses = [], []
    slab_i = 0
    for g, d in enumerate(dils):
        if d == 1:
            outs.append(o_refs[g][0, 0])
            lses.append(l_refs[g][0, 0])
            continue
        n = tile // d
        for ref, dest in ((o_refs[g], outs), (l_refs[g], lses)):
            halves = []
            for half in range(GROUP_W // LANES):
                for r in range(d):
                    slab[slab_i, pl.ds(r, n, stride=d), :] = ref[0, r, :, half * LANES:(half + 1) * LANES]
                halves.append(slab[slab_i])
                slab_i += 1
            dest.append(jnp.concatenate(halves, axis=1))

    m = jnp.maximum(jnp.maximum(lses[0], lses[1]), lses[2])
    es = [jnp.exp(l - m) for l in lses]
    den = es[0] + es[1] + es[2]
    cat = [(outs[g] * (es[g] / den)).astype(BF16) for g in range(N_GROUPS)]
    cat.append(co_ref[0].astype(BF16))
    cat = jnp.concatenate(cat, axis=1)
    x1 = x_ref[0] + g1_ref[0] * _dot(cat, wo_ref[...])
    x1_ref[0] = x1

    ms = jnp.mean(x1 * x1, axis=-1, keepdims=True)
    h2 = x1 * lax.rsqrt(ms + RMS_EPS) * ng_ref[...]
    h2 = h2 * (1.0 + sc_ref[0]) + sh_ref[0]
    hh = h2.astype(BF16)
    xg_ref[0, :, 0:D_MODEL] = hh

    hl = (h2 - hh.astype(F32)).astype(BF16)
    wr = wr_ref[...]
    wh = wr.astype(BF16)
    wl = (wr - wh.astype(F32)).astype(BF16)
    logits = _dot_nt(wh, hh) + _dot_nt(wh, hl) + _dot_nt(wl, hh)
    scores = 1.0 / (1.0 + jnp.exp(-logits))
    sel = scores + br_ref[...]

    ids = lax.broadcasted_iota(jnp.int32, (GROUP_SIZE, tile), 0)
    ninf = -jnp.inf
    sel_g = [sel[g * GROUP_SIZE:(g + 1) * GROUP_SIZE] for g in range(N_ROUTE_GROUPS)]
    gscore = jnp.zeros((N_ROUTE_GROUPS, tile), F32)
    for g in range(N_ROUTE_GROUPS):
        m1, oh = _first_max(sel_g[g], ids, GROUP_SIZE)
        m2 = jnp.max(jnp.where(oh, ninf, sel_g[g]), axis=0, keepdims=True)
        gscore = jnp.where(ids == g, m1 + m2, gscore)
    gsel = jnp.zeros((N_ROUTE_GROUPS, tile), F32)
    for _ in range(TOPK_GROUPS):
        _, oh = _first_max(gscore, ids, N_ROUTE_GROUPS)
        gsel = jnp.where(oh, 1.0, gsel)
        gscore = jnp.where(oh, ninf, gscore)
    cand = [jnp.where(gsel[g:g + 1] > 0.0, sel_g[g], ninf) for g in range(N_ROUTE_GROUPS)]
    chosen = [jnp.zeros((GROUP_SIZE, tile), F32) for _ in range(N_ROUTE_GROUPS)]
    picks = []
    for _ in range(TOP_K):
        mx = cand[0]
        for g in range(1, N_ROUTE_GROUPS):
            mx = jnp.maximum(mx, cand[g])
        mx = jnp.max(mx, axis=0, keepdims=True)
        idx = jnp.where(cand[0] == mx, ids, N_EXPERTS)
        for g in range(1, N_ROUTE_GROUPS):
            idx = jnp.minimum(idx, jnp.where(cand[g] == mx, ids + g * GROUP_SIZE, N_EXPERTS))
        idx = jnp.min(idx, axis=0, keepdims=True)
        ohs = [(ids + g * GROUP_SIZE) == idx for g in range(N_ROUTE_GROUPS)]
        picks.append(ohs)
        for g in range(N_ROUTE_GROUPS):
            chosen[g] = jnp.where(ohs[g], 1.0, chosen[g])
            cand[g] = jnp.where(ohs[g], ninf, cand[g])
    wts = [jnp.where(chosen[g] > 0.0, scores[g * GROUP_SIZE:(g + 1) * GROUP_SIZE], 0.0)
           for g in range(N_ROUTE_GROUPS)]
    wsum = jnp.sum(wts[0], axis=0, keepdims=True)
    for g in range(1, N_ROUTE_GROUPS):
        wsum = wsum + jnp.sum(wts[g], axis=0, keepdims=True)
    gates_t = jnp.concatenate([w / wsum * ROUTED_SCALE for w in wts], axis=0)
    gt_ref[0] = jnp.concatenate([gates_t, jnp.zeros((LANES - N_EXPERTS, tile), F32)], axis=0).T
    g_hi = gates_t.astype(BF16).astype(F32)
    xg_ref[0, :, D_MODEL:D_MODEL + LANES] = jnp.concatenate([g_hi, gates_t - g_hi], axis=0).T.astype(BF16)

    er = lax.broadcasted_iota(jnp.int32, (N_EXPERTS, N_EXPERTS), 0)
    ec = lax.broadcasted_iota(jnp.int32, (N_EXPERTS, N_EXPERTS), 1)
    lower = jnp.where(ec < er, 1.0, 0.0).astype(BF16)
    tr = lax.broadcasted_iota(jnp.int32, (SUB, SUB), 0)
    tc = lax.broadcasted_iota(jnp.int32, (SUB, SUB), 1)
    upper = jnp.where(tr < tc, 1.0, 0.0).astype(BF16)
    for u in range(tile // SUB):
        cols = slice(u * SUB, (u + 1) * SUB)
        ch = jnp.concatenate([c[:, cols] for c in chosen], axis=0)
        cnt = jnp.sum(ch, axis=1, keepdims=True)
        cpad = jnp.broadcast_to(jnp.ceil(cnt / CHUNK) * CHUNK, (N_EXPERTS, LANES))
        cnt_ref[u] = cpad
        seg0 = _dot(lower, cpad.astype(BF16))[:, 0:1]
        slot = seg0 + _dot(ch.astype(BF16), upper)
        rows = []
        for k in range(TOP_K):
            acc = jnp.where(picks[k][0][:, cols], slot[0:GROUP_SIZE], 0.0)
            for g in range(1, N_ROUTE_GROUPS):
                acc = acc + jnp.where(picks[k][g][:, cols], slot[g * GROUP_SIZE:(g + 1) * GROUP_SIZE], 0.0)
            rows.append(jnp.sum(acc, axis=0, keepdims=True))
        posr = jnp.concatenate(rows, axis=0)
        posr_ref[u] = posr
        post_ref[0, cols, :] = jnp.concatenate([posr, jnp.zeros((LANES - TOP_K, SUB), F32)], axis=0).T


def _oproj(x, o_list, l_list, co, g1, sc, sh, ng, wo_bf, wr_t, br, *, tile, dils, ride=None):
    B, S, _ = x.shape
    nt = S // tile
    r_in, r_out, r_shape, r_args = [], [], [], []
    if ride is not None:
        r_in, r_out, r_shape = _ride_specs(ride, ride[1].shape[0] // (B * nt), lambda b, j: b * nt + j)
        r_args = list(ride)
    tm = g1.shape[1]
    mod_spec = pl.BlockSpec((1, tm, D_MODEL), (lambda b, j: (b, j, 0)) if tm > 1 else (lambda b, j: (b, 0, 0)))
    const2 = lambda shape: pl.BlockSpec(shape, lambda b, j: (0, 0))
    nat = lambda w: pl.BlockSpec((1, tile, w), lambda b, j: (b, j, 0))
    dspec = [pl.BlockSpec((1, d, tile // d, GROUP_W), lambda b, j: (b, 0, j, 0)) for d in dils]
    n_slabs = sum(2 * (GROUP_W // LANES) for d in dils if d > 1)
    nsub_t = tile // SUB
    return pl.pallas_call(
        functools.partial(_oproj_kernel, tile=tile, dils=dils, ride=ride is not None),
        grid=(B, nt),
        in_specs=[nat(D_MODEL)] + dspec + dspec + [nat(CONV_CH), mod_spec, mod_spec, mod_spec,
                  const2((1, D_MODEL)), const2((D_MODEL, D_MODEL)), const2((N_EXPERTS, D_MODEL)),
                  const2((N_EXPERTS, 1))] + r_in,
        out_specs=[nat(D_MODEL), nat(XG_W), nat(LANES),
                   pl.BlockSpec((nsub_t, TOP_K, SUB), lambda b, j: (b * nt + j, 0, 0)),
                   nat(LANES),
                   pl.BlockSpec((nsub_t, N_EXPERTS, LANES), lambda b, j: (b * nt + j, 0, 0))] + r_out,
        out_shape=[jax.ShapeDtypeStruct((B, S, D_MODEL), F32), jax.ShapeDtypeStruct((B, S, XG_W), BF16),
                   jax.ShapeDtypeStruct((B, S, LANES), F32),
                   jax.ShapeDtypeStruct((B * S // SUB, TOP_K, SUB), F32),
                   jax.ShapeDtypeStruct((B, S, LANES), F32),
                   jax.ShapeDtypeStruct((B * S // SUB, N_EXPERTS, LANES), F32)] + r_shape,
        scratch_shapes=[pltpu.VMEM((max(n_slabs, 1), tile, LANES), F32)],
        compiler_params=_params(("arbitrary", "arbitrary")),
        name="oproj",
    )(x, *o_list, *l_list, co, g1, sc, sh, ng, wo_bf, wr_t, br, *r_args)


DENSE_EXPERTS_PER_STEP = 4


def _moe_kernel(h_ref, gt_ref, x1_ref, g2_ref, wg_ref, wu_ref, wd_ref, wgs_ref, wus_ref, wds_ref,
                y_ref, acc, *, tile):
    step = pl.program_id(2)
    h = h_ref[0]

    @pl.when(step == 0)
    def _():
        a = _dot(h, wgs_ref[...].astype(BF16))
        b = _dot(h, wus_ref[...].astype(BF16))
        acc[...] = _dot((_silu(a) * b).astype(BF16), wds_ref[...].astype(BF16))

    lane = lax.broadcasted_iota(jnp.int32, (tile, LANES), 1)
    routed = None
    for j in range(DENSE_EXPERTS_PER_STEP):
        e = step * DENSE_EXPERTS_PER_STEP + j
        a = _dot(h, wg_ref[j].astype(BF16))
        b = _dot(h, wu_ref[j].astype(BF16))
        gcol = jnp.sum(jnp.where(lane == e, gt_ref[0], 0.0), axis=1, keepdims=True)
        hc = _silu(a) * b * gcol
        y = _dot(hc.astype(BF16), wd_ref[j].astype(BF16))
        routed = y if routed is None else routed + y
    acc[...] += routed

    @pl.when(step == pl.num_programs(2) - 1)
    def _():
        y_ref[0] = x1_ref[0] + g2_ref[0] * acc[...]


def _moe(h2, gates, x1, g2, wg, wu, wd, wgs, wus, wds, *, tile):
    B, S, _ = h2.shape
    tm = g2.shape[1]
    mod_spec = pl.BlockSpec((1, tm, D_MODEL),
                            (lambda b, j, e: (b, j, 0)) if tm > 1 else (lambda b, j, e: (b, 0, 0)))
    nat = lambda w: pl.BlockSpec((1, tile, w), lambda b, j, e: (b, j, 0))
    const2 = lambda shape: pl.BlockSpec(shape, lambda b, j, e: (0, 0))
    return pl.pallas_call(
        functools.partial(_moe_kernel, tile=tile),
        grid=(B, S // tile, N_EXPERTS // DENSE_EXPERTS_PER_STEP),
        in_specs=[nat(D_MODEL), nat(LANES), nat(D_MODEL), mod_spec,
                  pl.BlockSpec((DENSE_EXPERTS_PER_STEP, D_MODEL, D_EXPERT), lambda b, j, e: (e, 0, 0)),
                  pl.BlockSpec((DENSE_EXPERTS_PER_STEP, D_MODEL, D_EXPERT), lambda b, j, e: (e, 0, 0)),
                  pl.BlockSpec((DENSE_EXPERTS_PER_STEP, D_EXPERT, D_MODEL), lambda b, j, e: (e, 0, 0)),
                  const2((D_MODEL, D_EXPERT)), const2((D_MODEL, D_EXPERT)), const2((D_EXPERT, D_MODEL))],
        out_specs=nat(D_MODEL),
        out_shape=jax.ShapeDtypeStruct((B, S, D_MODEL), F32),
        scratch_shapes=[pltpu.VMEM((tile, D_MODEL), F32)],
        compiler_params=_params(("arbitrary", "arbitrary", "arbitrary")),
        name="moe",
    )(h2, gates, x1, g2, wg, wu, wd, wgs, wus, wds)


FFN_TM = 1024
SUB_ROWS = SUB * TOP_K + N_EXPERTS * CHUNK
N_CHUNKS = SUB_ROWS // CHUNK
ZERO_CHUNK = N_CHUNKS
TABLE_W = 256
MXU_ROWS = 256
BLOCKS_PER_TRIP = 4


def _slot_onehot(first, pos_list, axis_iota):
    shape = axis_iota.shape
    hit = axis_iota == jnp.broadcast_to(pos_list[0] - first, shape).astype(BF16)
    for p in pos_list[1:]:
        hit = hit | (axis_iota == jnp.broadcast_to(p - first, shape).astype(BF16))
    return jnp.where(hit, jnp.ones(shape, BF16), jnp.zeros(shape, BF16))


def _chunk_copy(src, src_chunk, dst, dst_chunk, sem):
    return pltpu.make_async_copy(src.at[src_chunk], dst.at[dst_chunk], sem)


def _for_each(n, body):
    def two(t, carry):
        body(2 * t)
        body(2 * t + 1)
        return carry

    lax.fori_loop(0, n // 2, two, 0)

    @pl.when(n % 2 == 1)
    def _():
        body(n - 1)


def _dispatch_kernel(nblk_ref, nreal_ref, ncopy_ref, ntile_ref, src_ref, dst_ref, posr_ref, xg_ref, xs_hbm,
                     xs_scr, zero_scr, sem, tail_sem, *, nsub, n_tiles_max):
    s = pl.program_id(0)

    tile_chunks = FFN_TM // CHUNK
    blk_chunks = MXU_ROWS // CHUNK
    zero_scr[...] = jnp.zeros((tile_chunks, CHUNK, XG_W), BF16)
    tail_blocks = [ntile_ref[0] + s + r * nsub for r in range(-(-(n_tiles_max) // nsub))]

    def tail_copy(t):
        return pltpu.make_async_copy(
            zero_scr, xs_hbm.at[pl.ds(pl.multiple_of(t * tile_chunks, tile_chunks), tile_chunks)], tail_sem)

    for t in tail_blocks:
        @pl.when(t < n_tiles_max)
        def _():
            tail_copy(t).start()

    xg = xg_ref[...]
    pos = [posr_ref[0, k:k + 1, :] for k in range(TOP_K)]
    rows = lax.broadcasted_iota(jnp.int32, (MXU_ROWS, SUB), 0).astype(BF16)
    xs_scr[ZERO_CHUNK] = jnp.zeros((CHUNK, XG_W), BF16)

    def sort_blocks(i, carry):
        for j in range(BLOCKS_PER_TRIP):
            blk = i * BLOCKS_PER_TRIP + j
            onehot = _slot_onehot((blk * MXU_ROWS).astype(F32), pos, rows)
            sorted_rows = _dot(onehot, xg).astype(BF16)
            xs_scr[pl.ds(pl.multiple_of(blk * blk_chunks, blk_chunks), blk_chunks)] = (
                sorted_rows.reshape(blk_chunks, CHUNK, XG_W))
        k0 = i * trip_chunks
        _for_each(jnp.clip(n_real - k0, 0, trip_chunks), lambda t: send(k0 + t))
        return carry

    n_real = nreal_ref[s]
    n = ncopy_ref[s]
    trip_chunks = BLOCKS_PER_TRIP * blk_chunks

    def send(i):
        _chunk_copy(xs_scr, src_ref[0, 0, i], xs_hbm, dst_ref[0, 0, i], sem).start()

    lax.fori_loop(0, nblk_ref[s], sort_blocks, 0)
    _for_each(n - n_real, lambda t: send(n_real + t))
    _for_each(n, lambda i: _chunk_copy(xs_scr, 0, xs_hbm, 0, sem).wait())
    for t in tail_blocks:
        @pl.when(t < n_tiles_max)
        def _():
            tail_copy(t).wait()


def _dispatch(nblk, n_real, ncopy, n_tiles, src, dst, posr, xg, n_rows):
    nsub = posr.shape[0]
    tab = pl.BlockSpec((1, 1, TABLE_W), lambda s, *_: (s, 0, 0), memory_space=pltpu.SMEM)
    return pl.pallas_call(
        functools.partial(_dispatch_kernel, nsub=nsub, n_tiles_max=n_rows // FFN_TM),
        grid_spec=pltpu.PrefetchScalarGridSpec(
            num_scalar_prefetch=4, grid=(nsub,),
            in_specs=[tab, tab, pl.BlockSpec((1, TOP_K, SUB), lambda s, *_: (s, 0, 0)),
                      pl.BlockSpec((SUB, XG_W), lambda s, *_: (s, 0))],
            out_specs=pl.BlockSpec(memory_space=pl.ANY),
            scratch_shapes=[pltpu.VMEM((N_CHUNKS + 1, CHUNK, XG_W), BF16),
                            pltpu.VMEM((FFN_TM // CHUNK, CHUNK, XG_W), BF16),
                            pltpu.SemaphoreType.DMA(()), pltpu.SemaphoreType.DMA(())]),
        out_shape=jax.ShapeDtypeStruct((n_rows // CHUNK, CHUNK, XG_W), BF16),
        compiler_params=_params(("arbitrary",)),
        name="moe_dispatch",
    )(nblk, n_real, ncopy, n_tiles, src, dst, posr, xg).reshape(n_rows, XG_W)


def _ffn_kernel(te_ref, nt_ref, xs_ref, wg_ref, wu_ref, wd_ref,
                q_ref, kc_ref, vc_ref, nk_ref, nv_ref, b1_ref, b2_ref,
                ys_ref, ko_ref, vo_ref, o_ref, l_ref, wg_s, wu_s, wd_s, *, steps_per_batch):
    i = pl.program_id(0)
    n_heads = HEADS_PER_GROUP // steps_per_batch
    first_head = (i % steps_per_batch) * n_heads

    def ride():
        _attn_sample_batch(i // steps_per_batch, 0, q_ref, kc_ref, vc_ref, nk_ref, nv_ref, b1_ref, b2_ref,
                           ko_ref, vo_ref, o_ref, l_ref, heads=[first_head + j for j in range(n_heads)])

    @pl.when(i >= nt_ref[0])
    def _():
        ride()
        ys_ref[...] = jnp.zeros((FFN_TM, D_MODEL), BF16)

    @pl.when(i < nt_ref[0])
    def _():
        e = te_ref[i]

        @pl.when((i == 0) | (e != te_ref[jnp.maximum(i - 1, 0)]))
        def _():
            wg_s[...] = wg_ref[0].astype(BF16)
            wu_s[...] = wu_ref[0].astype(BF16)
            wd_s[...] = wd_ref[0].astype(BF16)

        ride()
        x = xs_ref[:, 0:D_MODEL]
        g = xs_ref[:, D_MODEL:XG_W].astype(F32)
        lane = lax.broadcasted_iota(jnp.int32, (FFN_TM, LANES), 1)
        gcol = jnp.sum(jnp.where((lane == e) | (lane == e + N_EXPERTS), g, 0.0), axis=1, keepdims=True)
        a = _dot(x, wg_s[...])
        b = _dot(x, wu_s[...])
        hc = _silu(a) * b * gcol
        ys_ref[...] = _dot(hc.astype(BF16), wd_s[...]).astype(BF16)


def _ffn(tile_expert, n_tiles, xs, wg, wu, wd, ride):
    q, kc, vc, nk, nv, b1, b2 = ride
    nb, _, _, wb = kc.shape
    n_rows = xs.shape[0]
    n_steps = n_rows // FFN_TM
    assert n_steps % nb == 0 and HEADS_PER_GROUP % (n_steps // nb) == 0
    per = n_steps // nb
    row = lambda i, te, nt: (jnp.minimum(i, nt[0] - 1), 0)
    wspec = lambda shape: pl.BlockSpec((1,) + shape, lambda i, te, nt: (te[i], 0, 0))
    rblk = lambda *tail: pl.BlockSpec((1, HEADS_PER_GROUP) + tail, lambda i, te, nt: (i // per, 0, 0, 0))
    const = lambda shape: pl.BlockSpec(shape, lambda i, te, nt: (0, 0, 0))
    return pl.pallas_call(
        functools.partial(_ffn_kernel, steps_per_batch=per),
        grid_spec=pltpu.PrefetchScalarGridSpec(
            num_scalar_prefetch=2, grid=(n_steps,),
            in_specs=[pl.BlockSpec((FFN_TM, XG_W), row), wspec((D_MODEL, D_EXPERT)),
                      wspec((D_MODEL, D_EXPERT)), wspec((D_EXPERT, D_MODEL)),
                      rblk(8, HEAD_DIM), rblk(HEAD_DIM, wb), rblk(HEAD_DIM, wb), const(nk.shape), const(nv.shape),
                      const(b1.shape), const(b2.shape)],
            out_specs=[pl.BlockSpec((FFN_TM, D_MODEL), lambda i, te, nt: (i, 0)),
                       rblk(HEAD_DIM, wb), rblk(HEAD_DIM, wb), rblk(8, HEAD_DIM), rblk(8, LANES)],
            scratch_shapes=[pltpu.VMEM((D_MODEL, D_EXPERT), BF16), pltpu.VMEM((D_MODEL, D_EXPERT), BF16),
                            pltpu.VMEM((D_EXPERT, D_MODEL), BF16)]),
        out_shape=[jax.ShapeDtypeStruct((n_rows, D_MODEL), BF16),
                   jax.ShapeDtypeStruct(kc.shape, F32), jax.ShapeDtypeStruct(kc.shape, F32),
                   jax.ShapeDtypeStruct((nb, HEADS_PER_GROUP, 8, HEAD_DIM), F32),
                   jax.ShapeDtypeStruct((nb, HEADS_PER_GROUP, 8, LANES), F32)],
        compiler_params=_params(("arbitrary",)),
        name="moe_ffn",
    )(tile_expert, n_tiles, xs, wg, wu, wd, q, kc, vc, nk, nv, b1, b2)


def _combine_kernel(nblk_ref, ncopy_ref, src_ref, post_ref, xg_ref, x1_ref, g2_ref, wgs_ref, wus_ref,
                    wds_ref, ys_hbm, y_ref, ys_scr, wgs_s, wus_s, wds_s, sem):
    s = pl.program_id(0)
    n = ncopy_ref[s]
    blk_chunks = MXU_ROWS // CHUNK
    trip_chunks = BLOCKS_PER_TRIP * blk_chunks

    def fetch_trip(t, carry):
        k0 = t * trip_chunks
        _for_each(trip_chunks,
                  lambda u: _chunk_copy(ys_hbm, src_ref[0, 0, k0 + u], ys_scr, k0 + u, sem.at[t]).start())
        return carry

    lax.fori_loop(0, nblk_ref[s], fetch_trip, 0)

    @pl.when(s == 0)
    def _():
        wgs_s[...] = wgs_ref[...].astype(BF16)
        wus_s[...] = wus_ref[...].astype(BF16)
        wds_s[...] = wds_ref[...].astype(BF16)

    h = xg_ref[:, 0:D_MODEL]
    a = _dot(h, wgs_s[...])
    b = _dot(h, wus_s[...])
    shared = _dot((_silu(a) * b).astype(BF16), wds_s[...])

    post = post_ref[...]
    pos = [jnp.broadcast_to(post[:, k:k + 1], (SUB, MXU_ROWS)) for k in range(TOP_K)]
    lanes = lax.broadcasted_iota(jnp.int32, (SUB, MXU_ROWS), 1).astype(BF16)

    def gather_blocks(i, acc):
        _for_each(trip_chunks, lambda t: _chunk_copy(ys_hbm, 0, ys_scr, 0, sem.at[i]).wait())
        for j in range(BLOCKS_PER_TRIP):
            blk = i * BLOCKS_PER_TRIP + j
            onehot = _slot_onehot((blk * MXU_ROWS).astype(F32), pos, lanes)
            rows = ys_scr[pl.ds(pl.multiple_of(blk * blk_chunks, blk_chunks), blk_chunks)]
            acc = acc + _dot(onehot, rows.reshape(MXU_ROWS, D_MODEL))
        return acc

    routed = lax.fori_loop(0, nblk_ref[s], gather_blocks, jnp.zeros((SUB, D_MODEL), F32))
    y_ref[...] = x1_ref[...] + g2_ref[0] * (routed + shared)


def _combine(nblk, ncopy, src, post, xg, x1, g2, wgs, wus, wds, ys):
    n_tok = x1.shape[0]
    nsub = n_tok // SUB
    per_b = nsub // g2.shape[0]
    tab = pl.BlockSpec((1, 1, TABLE_W), lambda s, *_: (s, 0, 0), memory_space=pltpu.SMEM)
    const2 = lambda shape: pl.BlockSpec(shape, lambda s, *_: (0, 0))
    return pl.pallas_call(
        _combine_kernel,
        grid_spec=pltpu.PrefetchScalarGridSpec(
            num_scalar_prefetch=2, grid=(nsub,),
            in_specs=[tab, pl.BlockSpec((SUB, LANES), lambda s, *_: (s, 0)),
                      pl.BlockSpec((SUB, XG_W), lambda s, *_: (s, 0)),
                      pl.BlockSpec((SUB, D_MODEL), lambda s, *_: (s, 0)),
                      pl.BlockSpec((1, 1, D_MODEL), lambda s, *_: (s // per_b, 0, 0)),
                      const2((D_MODEL, D_EXPERT)), const2((D_MODEL, D_EXPERT)), const2((D_EXPERT, D_MODEL)),
                      pl.BlockSpec(memory_space=pl.ANY)],
            out_specs=pl.BlockSpec((SUB, D_MODEL), lambda s, *_: (s, 0)),
            scratch_shapes=[pltpu.VMEM((N_CHUNKS, CHUNK, D_MODEL), BF16), pltpu.VMEM((D_MODEL, D_EXPERT), BF16),
                            pltpu.VMEM((D_MODEL, D_EXPERT), BF16), pltpu.VMEM((D_EXPERT, D_MODEL), BF16),
                            pltpu.SemaphoreType.DMA((SUB_ROWS // (BLOCKS_PER_TRIP * MXU_ROWS),))]),
        out_shape=jax.ShapeDtypeStruct((n_tok, D_MODEL), F32),
        compiler_params=_params(("arbitrary",)),
        name="moe_combine",
    )(nblk, ncopy, src, post, xg, x1, g2, wgs, wus, wds, ys.reshape(-1, CHUNK, D_MODEL))


def _route_tables(cnt):
    nsub = cnt.shape[0]
    i32 = jnp.int32
    seg_end = jnp.cumsum(cnt, axis=1)
    seg_start = seg_end - cnt
    tot = jnp.sum(cnt, axis=0)
    tot_al = (tot + FFN_TM - 1) // FFN_TM * FFN_TM
    reg_end = jnp.cumsum(tot_al)
    reg_start = reg_end - tot_al
    base = reg_start[None, :] + jnp.cumsum(cnt, axis=0) - cnt
    n_real = seg_end[:, -1] // CHUNK
    k = jnp.arange(TABLE_W, dtype=i32)
    owner = jnp.sum((seg_end[:, None, :] // CHUNK) <= k[None, :, None], axis=2)
    owner = jnp.minimum(owner, N_EXPERTS - 1)
    delta = (base - seg_start) // CHUNK
    e_ids = jnp.arange(N_EXPERTS, dtype=i32)
    real_dst = jnp.sum(jnp.where(owner[:, :, None] == e_ids, delta[:, None, :], 0), axis=2) + k[None, :]
    fill_n = (tot_al - tot) // CHUNK
    fill_dst0 = (reg_start + tot) // CHUNK
    src = jnp.broadcast_to(k[None, :], (nsub, TABLE_W))
    dst = real_dst
    ncopy = n_real
    for r in range(-(-N_EXPERTS // nsub)):
        e_of = jnp.arange(nsub, dtype=i32) + r * nsub
        ok = e_of < N_EXPERTS
        e_cl = jnp.minimum(e_of, N_EXPERTS - 1)
        fn = jnp.where(ok, fill_n[e_cl], 0)
        j = k[None, :] - ncopy[:, None]
        is_fill = (j >= 0) & (j < fn[:, None])
        src = jnp.where(is_fill, ZERO_CHUNK, src)
        dst = jnp.where(is_fill, fill_dst0[e_cl][:, None] + j, dst)
        ncopy = ncopy + fn
    trip_rows = BLOCKS_PER_TRIP * MXU_ROWS
    disp_nblk = (n_real * CHUNK + trip_rows - 1) // trip_rows
    comb_n = disp_nblk * (trip_rows // CHUNK)
    comb_src = jnp.where(k[None, :] < n_real[:, None], real_dst, real_dst[:, 0:1])
    n_tiles = reg_end[-1] // FFN_TM
    t = jnp.arange((nsub * SUB_ROWS + N_EXPERTS * FFN_TM) // FFN_TM, dtype=i32)
    tile_expert = jnp.minimum(jnp.sum((reg_end[None, :] // FFN_TM) <= t[:, None], axis=1), N_EXPERTS - 1)
    as3 = lambda a: a.astype(i32).reshape(nsub, 1, TABLE_W)
    return (disp_nblk.astype(i32), n_real.astype(i32), ncopy.astype(i32), as3(src), as3(dst),
            comb_n.astype(i32), as3(comb_src), tile_expert.astype(i32), n_tiles.astype(i32).reshape(1))


def _moe_sparse(xg, post, posr, cnt, x1, g2, wg, wu, wd, wgs, wus, wds, ride):
    B, S, _ = x1.shape
    n_tok = B * S
    nsub = n_tok // SUB
    (disp_nblk, n_real, disp_n, disp_src, disp_dst, comb_n, comb_src, tile_expert,
     n_tiles) = _route_tables(cnt[:, :, 0].astype(jnp.int32))
    n_rows = nsub * SUB_ROWS + N_EXPERTS * FFN_TM
    xg2 = xg.reshape(n_tok, XG_W)
    xs = _dispatch(disp_nblk, n_real, disp_n, n_tiles, disp_src, disp_dst, posr, xg2, n_rows)
    ys, *ride_out = _ffn(tile_expert, n_tiles, xs, wg, wu, wd, ride)
    y = _combine(disp_nblk, comb_n, comb_src, post.reshape(n_tok, LANES), xg2, x1.reshape(n_tok, D_MODEL),
                 g2, wgs, wus, wds, ys)
    return y.reshape(B, S, D_MODEL), ride_out


def _t5_bucket(dist):
    max_exact = N_BUCKETS // 2
    df = jnp.maximum(dist, 1).astype(F32)
    large = max_exact + (jnp.log(df / max_exact) / math.log(MAX_DISTANCE / max_exact)
                         * (N_BUCKETS - max_exact)).astype(jnp.int32)
    large = jnp.minimum(large, N_BUCKETS - 1)
    return jnp.where(dist < max_exact, dist, large)


def _step_bias(rel_bias, g):
    dist = jnp.arange(N_KEYS, dtype=jnp.int32) * DILATIONS[g]
    cols = rel_bias[:, g * HEADS_PER_GROUP:(g + 1) * HEADS_PER_GROUP]
    return cols[_t5_bucket(dist)].T.astype(F32)


def _prompt_bias(bias_k):
    h = bias_k.shape[0]
    n = 3 * BLK
    row = jnp.concatenate([bias_k[:, ::-1], jnp.full((h, n - N_KEYS), NEG, F32)], axis=1)
    t = jnp.tile(row, (1, BLK))[:, :BLK * (n - 1)].reshape(h, BLK, n - 1)
    return t[:, :, :2 * BLK]


def _sample_bias(bias_k, wb, d):
    h = bias_k.shape[0]
    rev = bias_k[:, :0:-1]
    rows = []
    for t in range(4):
        if d == 1:
            rows.append(jnp.concatenate([jnp.full((h, t), NEG, F32), rev[:, :wb - t]], axis=1))
        else:
            r = jnp.arange(d)[None, None, :]
            rows.append(jnp.where(r == t, rev[:, :, None], NEG).reshape(h, wb))
    b1 = jnp.concatenate([jnp.stack(rows, axis=1), jnp.zeros((h, 4, wb), F32)], axis=1)
    rows = []
    for t in range(4):
        cols = [bias_k[:, (t - tn) // d] if (t >= tn and (t - tn) % d == 0) else jnp.full((h,), NEG, F32)
                for tn in range(4)]
        rows.append(jnp.concatenate([jnp.full((h, LANES - 4), NEG, F32), jnp.stack(cols, axis=1)], axis=1))
    b2 = jnp.concatenate([jnp.stack(rows, axis=1), jnp.zeros((h, 4, LANES), F32)], axis=1)
    return b1, b2


TILE = 512
RIDE_ON_OPROJ, RIDE_ON_PROJ, RIDE_ON_FFN = 0, 1, 2


def _sample_front(x, mods, hist, caches, rel_bias, weights):
    norm1_g, _, w_in_bf, qg, kg, conv_w, bd = weights[:7]
    sh1, sc1 = mods[:2]
    S = x.shape[1]
    nb = S // 4
    qf, kt, vt, u, co = _proj(x, sc1, sh1, norm1_g, w_in_bf, qg, kg, conv_w, bd, hist,
                              tile=TILE, dils=(1, 1, 1), sample=True)
    rides = []
    for g in range(N_GROUPS):
        d, wb = DILATIONS[g], WINDOWS[g]
        kc, vc = caches[g]
        b1, b2 = _sample_bias(_step_bias(rel_bias, g), wb, d)
        q = qf[0, :, g * GROUP_W:(g + 1) * GROUP_W].reshape(nb, 4, HEADS_PER_GROUP, HEAD_DIM)
        q = jnp.pad(q.transpose(0, 2, 1, 3), ((0, 0), (0, 0), (0, 4), (0, 0))).astype(BF16)
        new_t = lambda a: a[g * GROUP_W:(g + 1) * GROUP_W].reshape(HEADS_PER_GROUP, HEAD_DIM, S)
        rides.append((q, kc, vc, new_t(kt), new_t(vt), b1, b2))
    return dict(x=x, mods=mods, u=u, co=co), rides


def _sample_back(front, ride_outs, weights):
    (_, norm2_g, _, _, _, _, _, w_o_bf, w_r_t, b_r, w_gate_e, w_up_e, w_down_e,
     w_gate_s, w_up_s, w_down_s) = weights
    x, u, co = front["x"], front["u"], front["co"]
    _, _, g1, sh2, sc2, g2 = front["mods"]
    S = x.shape[1]
    nb = S // 4
    states, o_list, l_list = [], [], []
    for g in range(N_GROUPS):
        ko, vo, o, lse = ride_outs[g]
        states += [ko.transpose(0, 3, 1, 2)[None], vo.transpose(0, 3, 1, 2)[None]]
        o_list.append(o[:, :, :4].transpose(0, 2, 1, 3).reshape(1, 1, S, GROUP_W))
        lse = jnp.broadcast_to(lse[:, :, :4, :1], (nb, HEADS_PER_GROUP, 4, HEAD_DIM))
        l_list.append(lse.transpose(0, 2, 1, 3).reshape(1, 1, S, GROUP_W))
    states.append(u.reshape(nb, 4, CONV_CH)[:, 2:][None])
    x1, xg, gates, _, _, _ = _oproj(x, o_list, l_list, co, g1, sc2, sh2, norm2_g, w_o_bf, w_r_t, b_r,
                                    tile=TILE, dils=(1, 1, 1))
    y = _moe(xg, gates, x1, g2, w_gate_e, w_up_e, w_down_e, w_gate_s, w_up_s, w_down_s, tile=TILE)
    return y, states


def _prompt_layer(x, mods, rel_bias, weights, rides):
    (norm1_g, norm2_g, w_in_bf, qg, kg, conv_w, bd, w_o_bf, w_r_t, b_r, w_gate_e, w_up_e, w_down_e,
     w_gate_s, w_up_s, w_down_s) = weights
    sh1, sc1, g1, sh2, sc2, g2 = mods
    B, S, _ = x.shape
    ride_outs = [None] * N_GROUPS
    res = _proj(x, sc1, sh1, norm1_g, w_in_bf, qg, kg, conv_w, bd, None, tile=TILE, dils=DILATIONS, sample=False,
                ride=rides[RIDE_ON_PROJ])
    qkv, tails, (u, co), ride_outs[RIDE_ON_PROJ] = res[:9], res[9:15], res[15:17], res[17:]
    states, o_list, l_list = [], [], []
    for g in range(N_GROUPS):
        d, w = DILATIONS[g], WINDOWS[g]
        q, k, v = (a.reshape(B * d, S // d, GROUP_W) for a in qkv[3 * g:3 * g + 3])
        o, lse = _attn_prompt(q, k, v, _prompt_bias(_step_bias(rel_bias, g)))
        o_list.append(o.reshape(B, d, S // d, GROUP_W))
        l_list.append(lse.reshape(B, d, S // d, GROUP_W))
        keep = lambda a: a.reshape(B, HEADS_PER_GROUP, HEAD_DIM, w).transpose(0, 3, 1, 2)[None]
        states += [keep(tails[2 * g]), keep(tails[2 * g + 1])]
    states.append(u[:, 6:][None])
    x1, xg, _, posr, post, cnt, *ride_outs[RIDE_ON_OPROJ] = _oproj(
        x, o_list, l_list, co, g1, sc2, sh2, norm2_g, w_o_bf, w_r_t, b_r, tile=TILE, dils=DILATIONS,
        ride=rides[RIDE_ON_OPROJ])
    y, ride_outs[RIDE_ON_FFN] = _moe_sparse(xg, post, posr, cnt, x1, g2, w_gate_e, w_up_e, w_down_e, w_gate_s,
                                            w_up_s, w_down_s, rides[RIDE_ON_FFN])
    return y, states, ride_outs


def kernel(x_prompt, x_sample, c_prompt, c_sample, cache_k_w128, cache_v_w128, cache_k_w512, cache_v_w512, cache_k_w2048, cache_v_w2048, state_conv, rel_bias, norm1_g, norm2_g, w_ada, b_ada, w_in, q_norm_g, k_norm_g, conv_w, w_o, w_router, b_router, w_gate_e, w_up_e, w_down_e, w_gate_s, w_up_s, w_down_s):
    B = x_prompt.shape[0]
    DB, T = x_sample.shape[:2]
    n_c = B + DB
    c_all = jnp.pad(jnp.concatenate([c_prompt, c_sample], axis=0), ((0, (-n_c) % 8), (0, 0)))
    mod = _ada(c_all, w_ada[0], b_ada)
    chunks = [mod[:, i * D_MODEL:(i + 1) * D_MODEL] for i in range(6)]
    mods_p = [c[:B].reshape(B, 1, D_MODEL) for c in chunks]
    mods_s = [jnp.repeat(c[B:n_c], T, axis=0).reshape(1, DB * T, D_MODEL) for c in chunks]

    eye = jnp.arange(GROUP_W) // HEAD_DIM
    bd = jnp.where(eye[:, None] == eye[None, :], 1.0 / HEAD_DIM, 0.0).astype(BF16)
    weights = (norm1_g, norm2_g, w_in[0].astype(BF16),
               jnp.tile(q_norm_g, (1, HEADS_PER_GROUP)), jnp.tile(k_norm_g, (1, HEADS_PER_GROUP)),
               conv_w[0], bd, w_o[0].astype(BF16), w_router[0].T, b_router.reshape(N_EXPERTS, 1),
               w_gate_e[0], w_up_e[0], w_down_e[0], w_gate_s[0], w_up_s[0], w_down_s[0])

    s0, s1 = state_conv[0, :, 0], state_conv[0, :, 1]
    zero = jnp.zeros_like(s0)
    hist_a = jnp.stack([s0, s1, zero, zero], axis=1).reshape(DB * T, CONV_CH)
    hist_b = jnp.stack([s1, zero, zero, zero], axis=1).reshape(DB * T, CONV_CH)
    caches = [(ck[0].transpose(0, 2, 3, 1), cv[0].transpose(0, 2, 3, 1))
              for ck, cv in ((cache_k_w128, cache_v_w128), (cache_k_w512, cache_v_w512),
                             (cache_k_w2048, cache_v_w2048))]
    front, ride = _sample_front(x_sample.reshape(1, DB * T, D_MODEL), mods_s, (hist_a, hist_b), caches,
                                rel_bias, weights)
    yp, st_p, ride_out = _prompt_layer(x_prompt, mods_p, rel_bias, weights, ride)
    ys, st_s = _sample_back(front, ride_out, weights)
    return (yp, ys.reshape(DB, T, D_MODEL), *st_p, *st_s)
```

```python
import functools
import math

import jax
import jax.numpy as jnp
from jax import lax
from jax.experimental import pallas as pl
from jax.experimental.pallas import tpu as pltpu

D_MODEL = 1024
HEAD_DIM = 64
HEADS_PER_GROUP = 4
GROUP_W = HEADS_PER_GROUP * HEAD_DIM
WINDOWS = (128, 512, 2048)
DILATIONS = (1, 4, 16)
N_GROUPS = 3
ATTN_W = N_GROUPS * GROUP_W
CONV_CH = 256
N_PROJ = 3 * ATTN_W + 3 * CONV_CH
N_STEPS = 128
N_KEYS = N_STEPS + 1
BLK = 128
N_BUCKETS = 32
MAX_DISTANCE = WINDOWS[-1]
N_EXPERTS = 64
TOP_K = 8
N_ROUTE_GROUPS = 8
GROUP_SIZE = N_EXPERTS // N_ROUTE_GROUPS
TOPK_GROUPS = 4
D_EXPERT = 256
ROUTED_SCALE = 2.5
RMS_EPS = 1e-6
NEG = -1e30
LANES = 128
SUB = 256
CHUNK = 16
XG_W = D_MODEL + LANES
VMEM_LIMIT = 56 * 1024 * 1024

F32 = jnp.float32
BF16 = jnp.bfloat16


def _dot(a, b):
    return jnp.dot(a, b, preferred_element_type=F32)


def _dot_nt(a, b):
    return lax.dot_general(a, b, (((1,), (1,)), ((), ())), preferred_element_type=F32)


def _silu(a):
    return a / (1.0 + jnp.exp(-a))


def _params(sem):
    return pltpu.CompilerParams(dimension_semantics=sem, vmem_limit_bytes=VMEM_LIMIT)


def _ada_kernel(c_ref, w_ref, b_ref, o_ref):
    s = _silu(c_ref[...])
    o_ref[...] = _dot(s.astype(BF16), w_ref[...].astype(BF16)) + b_ref[...]


def _ada(c, w_ada, b_ada):
    n = c.shape[0]
    nc = w_ada.shape[1] // D_MODEL
    return pl.pallas_call(
        _ada_kernel,
        grid=(nc,),
        in_specs=[pl.BlockSpec((n, D_MODEL), lambda j: (0, 0)),
                  pl.BlockSpec((D_MODEL, D_MODEL), lambda j: (0, j)),
                  pl.BlockSpec((1, D_MODEL), lambda j: (0, j))],
        out_specs=pl.BlockSpec((n, D_MODEL), lambda j: (0, j)),
        out_shape=jax.ShapeDtypeStruct((n, w_ada.shape[1]), F32),
        compiler_params=_params(("arbitrary",)),
        name="ada",
    )(c, w_ada, b_ada)


def _proj_kernel(*refs, tile, dils, sample, nt):
    if sample:
        (x_ref, sc_ref, sh_ref, ng_ref, w_ref, qg_ref, kg_ref, cw_ref, bd_ref, ha_ref, hb_ref,
         qf_ref, kt_ref, vt_ref, u_ref, co_ref, u_scr) = refs
    else:
        (x_ref, sc_ref, sh_ref, ng_ref, w_ref, qg_ref, kg_ref, cw_ref, bd_ref) = refs[:9]
        ride_in, refs = refs[9:9 + N_RIDE_IN], refs[9 + N_RIDE_IN:]
        (q0, k0, v0, q1, k1, v1, q2, k2, v2, kt0, vt0, kt1, vt1, kt2, vt2, u_ref, co_ref) = refs[:17]
        ride_out, (slab, u_scr) = refs[17:17 + N_RIDE_OUT], refs[17 + N_RIDE_OUT:]
        qkv_out = ((q0, k0, v0), (q1, k1, v1), (q2, k2, v2))
        tails = ((kt0, vt0), (kt1, vt1), (kt2, vt2))
        j = pl.program_id(1)
        _ride_block(pl.program_id(0) * nt + j, ride_in + ride_out)

    x = x_ref[0]
    ms = jnp.mean(x * x, axis=-1, keepdims=True)
    h = x * lax.rsqrt(ms + RMS_EPS) * ng_ref[...]
    h = h * (1.0 + sc_ref[0]) + sh_ref[0]
    proj = _dot(h.astype(BF16), w_ref[...])

    bd = bd_ref[...]

    def headnorm(z, g):
        msq = _dot((z * z).astype(BF16), bd)
        return z * lax.rsqrt(msq + RMS_EPS) * g

    slab_i = 0
    for g in range(N_GROUPS):
        c0 = g * GROUP_W
        qn = headnorm(proj[:, c0:c0 + GROUP_W], qg_ref[...]) * (HEAD_DIM ** -0.5)
        kn = headnorm(proj[:, ATTN_W + c0:ATTN_W + c0 + GROUP_W], kg_ref[...])
        vv = proj[:, 2 * ATTN_W + c0:2 * ATTN_W + c0 + GROUP_W]
        if sample:
            qf_ref[0, :, c0:c0 + GROUP_W] = qn
            kt_ref[c0:c0 + GROUP_W, :] = kn.T
            vt_ref[c0:c0 + GROUP_W, :] = vv.T
            continue
        keep = min(WINDOWS[g], tile)

        @pl.when(j >= nt - max(WINDOWS[g] // tile, 1))
        def _(kn=kn, vv=vv, g=g, keep=keep):
            tails[g][0][0] = kn[tile - keep:, :].T
            tails[g][1][0] = vv[tile - keep:, :].T

        d = dils[g]
        for val, out in zip((qn, kn, vv), qkv_out[g]):
            if d == 1:
                out[0, 0] = val.astype(BF16)
                continue
            n = tile // d
            for half in range(GROUP_W // LANES):
                slab[slab_i] = val[:, half * LANES:(half + 1) * LANES]
                for r in range(d):
                    out[0, r, :, half * LANES:(half + 1) * LANES] = (
                        slab[slab_i, pl.ds(r, n, stride=d), :].astype(BF16))
                slab_i += 1

    base = 3 * ATTN_W
    u = proj[:, base + 2 * CONV_CH:base + 3 * CONV_CH] * proj[:, base:base + CONV_CH]
    gate_b = proj[:, base + CONV_CH:base + 2 * CONV_CH]
    if sample:
        u_scr[0:8, :] = jnp.zeros((8, CONV_CH), F32)
    else:
        @pl.when(j == 0)
        def _():
            u_scr[0:8, :] = jnp.zeros((8, CONV_CH), F32)

        @pl.when(j > 0)
        def _():
            u_scr[0:8, :] = u_scr[tile:tile + 8, :]

    u_scr[8:tile + 8, :] = u
    um1 = u_scr[7:tile + 7, :]
    um2 = u_scr[6:tile + 6, :]
    if sample:
        t = lax.broadcasted_iota(jnp.int32, (tile, CONV_CH), 0) % 4
        um1 = jnp.where(t >= 1, um1, 0.0) + hb_ref[...]
        um2 = jnp.where(t >= 2, um2, 0.0) + ha_ref[...]
    cw = cw_ref[...]
    conv = cw[0:1] * um2 + cw[1:2] * um1 + cw[2:3] * u
    u_ref[0] = u if sample else u[tile - 8:, :]
    co_ref[0] = (gate_b * conv).astype(BF16)


def _proj(x, sc, sh, ng, w_bf, qg, kg, cw, bd, hist, *, tile, dils, sample, ride=None):
    B, S, _ = x.shape
    nt = S // tile
    tm = sc.shape[1]
    mod_spec = pl.BlockSpec((1, tm, D_MODEL), (lambda b, j: (b, j, 0)) if tm > 1 else (lambda b, j: (b, 0, 0)))
    const2 = lambda shape: pl.BlockSpec(shape, lambda b, j: (0, 0))
    in_specs = [pl.BlockSpec((1, tile, D_MODEL), lambda b, j: (b, j, 0)), mod_spec, mod_spec,
                const2((1, D_MODEL)), const2((D_MODEL, N_PROJ)), const2((1, GROUP_W)),
                const2((1, GROUP_W)), const2((3, CONV_CH)), const2((GROUP_W, GROUP_W))]
    args = [x, sc, sh, ng, w_bf, qg, kg, cw, bd]
    nat = lambda w: pl.BlockSpec((1, tile, w), lambda b, j: (b, j, 0))
    out_specs, out_shape = [], []
    scratch = []
    if sample:
        assert B == 1 and nt == 1
        in_specs += [pl.BlockSpec((tile, CONV_CH), lambda b, j: (j, 0))] * 2
        args += list(hist)
        out_specs += [nat(ATTN_W)] + [pl.BlockSpec((ATTN_W, tile), lambda b, j: (0, j))] * 2 + [nat(CONV_CH)]
        out_shape += [jax.ShapeDtypeStruct((B, S, ATTN_W), F32)]
        out_shape += [jax.ShapeDtypeStruct((ATTN_W, S), F32)] * 2
        out_shape += [jax.ShapeDtypeStruct((B, S, CONV_CH), F32)]
    else:
        for d in dils:
            for _ in range(3):
                out_specs.append(pl.BlockSpec((1, d, tile // d, GROUP_W), lambda b, j: (b, 0, j, 0)))
                out_shape.append(jax.ShapeDtypeStruct((B, d, S // d, GROUP_W), BF16))
        for w in WINDOWS:
            keep, first = min(w, tile), nt - max(w // tile, 1)
            for _ in range(2):
                out_specs.append(pl.BlockSpec((1, GROUP_W, keep),
                                              lambda b, j, first=first: (b, 0, jnp.maximum(j - first, 0))))
                out_shape.append(jax.ShapeDtypeStruct((B, GROUP_W, w), F32))
        out_specs.append(pl.BlockSpec((1, 8, CONV_CH), lambda b, j: (b, 0, 0)))
        out_shape.append(jax.ShapeDtypeStruct((B, 8, CONV_CH), F32))
        n_slabs = sum(3 * (GROUP_W // LANES) for d in dils if d > 1)
        scratch.append(pltpu.VMEM((n_slabs, tile, LANES), F32))
    out_specs.append(nat(CONV_CH))
    out_shape.append(jax.ShapeDtypeStruct((B, S, CONV_CH), BF16))
    scratch.append(pltpu.VMEM((tile + 8, CONV_CH), F32))
    if not sample:
        nb = ride[1].shape[0]
        r_in, r_out, r_shape = _ride_specs(ride, nb // (B * nt), lambda b, j: b * nt + j)
        in_specs, args = in_specs + r_in, args + list(ride)
        out_specs, out_shape = out_specs + r_out, out_shape + r_shape
    return pl.pallas_call(
        functools.partial(_proj_kernel, tile=tile, dils=dils, sample=sample, nt=nt),
        grid=(B, nt),
        in_specs=in_specs,
        out_specs=out_specs,
        out_shape=out_shape,
        scratch_shapes=scratch,
        compiler_params=_params(("arbitrary", "arbitrary")),
        name="proj_sample" if sample else "proj_prompt",
    )(*args)


def _attn_kernel(q_ref, kp_ref, kc_ref, vp_ref, vc_ref, b_ref, o_ref, l_ref, s_scr, m_scr, o_scr, z_scr,
                 *, nq):
    first = pl.program_id(1) == 0
    low = lax.broadcasted_iota(jnp.int32, (BLK, LANES), 1) < HEAD_DIM
    ones = jnp.ones((BLK, LANES), BF16)
    chains = [(j, pair, sub) for j in range(nq) for pair in range(GROUP_W // LANES) for sub in range(2)]

    def operands(j, pair):
        rows = slice(j * BLK, (j + 1) * BLK)
        sl = slice(pair * LANES, (pair + 1) * LANES)
        if j == 0:
            return rows, sl, kp_ref[0, :, sl], vp_ref[0, :, sl]
        prows = slice((j - 1) * BLK, j * BLK)
        return rows, sl, kc_ref[0, prows, sl], vc_ref[0, prows, sl]

    for c, (j, pair, sub) in enumerate(chains):
        rows, sl, kp, _ = operands(j, pair)
        q = q_ref[0, rows, sl]
        qm = jnp.where(low if sub == 0 else ~low, q, jnp.zeros_like(q))
        hh = 2 * pair + sub
        sp = _dot_nt(qm, kp) + b_ref[hh, :, 0:BLK]
        if j == 0:
            sp = jnp.where(first, NEG, sp)
        sc = _dot_nt(qm, kc_ref[0, rows, sl]) + b_ref[hh, :, BLK:2 * BLK]
        s_scr[c, :, 0:BLK] = sp
        s_scr[c, :, BLK:2 * BLK] = sc
        m_scr[c] = jnp.broadcast_to(jnp.max(jnp.maximum(sp, sc), axis=-1, keepdims=True), (BLK, LANES))

    for c, (j, pair, sub) in enumerate(chains):
        rows, sl, _, vp = operands(j, pair)
        m = m_scr[c]
        pp = jnp.exp(s_scr[c, :, 0:BLK] - m).astype(BF16)
        pc = jnp.exp(s_scr[c, :, BLK:2 * BLK] - m).astype(BF16)
        o_scr[c] = _dot(pp, vp) + _dot(pc, vc_ref[0, rows, sl])
        z_scr[c] = _dot(pp, ones) + _dot(pc, ones)

    for c in range(0, len(chains), 2):
        j, pair, _ = chains[c]
        rows = slice(j * BLK, (j + 1) * BLK)
        sl = slice(pair * LANES, (pair + 1) * LANES)
        o_sub = [o_scr[c + sub] / z_scr[c + sub] for sub in range(2)]
        l_sub = [m_scr[c + sub] + jnp.log(z_scr[c + sub]) for sub in range(2)]
        o_ref[0, rows, sl] = jnp.where(low, o_sub[0], o_sub[1])
        l_ref[0, rows, sl] = jnp.where(low, l_sub[0], l_sub[1])


def _attn_prompt(q, k, v, bias):
    Z, L, _ = q.shape
    qt = min(L, 4 * BLK)
    nq = qt // BLK
    cur = pl.BlockSpec((1, qt, GROUP_W), lambda z, i: (z, i, 0))
    prev = pl.BlockSpec((1, BLK, GROUP_W), lambda z, i: (z, jnp.maximum(i * nq - 1, 0), 0))
    return pl.pallas_call(
        functools.partial(_attn_kernel, nq=nq),
        grid=(Z, L // qt),
        in_specs=[cur, prev, cur, prev, cur,
                  pl.BlockSpec((HEADS_PER_GROUP, BLK, 2 * BLK), lambda z, i: (0, 0, 0))],
        out_specs=[cur, cur],
        out_shape=[jax.ShapeDtypeStruct((Z, L, GROUP_W), F32)] * 2,
        scratch_shapes=[pltpu.VMEM((nq * HEADS_PER_GROUP, BLK, 2 * BLK), F32)]
        + [pltpu.VMEM((nq * HEADS_PER_GROUP, BLK, LANES), F32)] * 3,
        compiler_params=_params(("arbitrary", "arbitrary")),
        name="attn_prompt",
    )(q, k, k, v, v, bias)


def _attn_sample_batch(gb, b, q_ref, kc_ref, vc_ref, nk_ref, nv_ref, b1_ref, b2_ref, ko_ref, vo_ref, o_ref,
                       l_ref, heads=range(HEADS_PER_GROUP)):
    wb = kc_ref.shape[-1]
    lane = lax.broadcasted_iota(jnp.int32, (HEAD_DIM, LANES), 1)
    new_cols = lane >= LANES - 4
    per_tile = LANES // 4
    tile_at = pl.ds(pl.multiple_of((gb // per_tile) * LANES, LANES), LANES)
    shift = (LANES - 4) - 4 * (gb % per_tile)
    for hh in heads:
        q = q_ref[b, hh]
        kt = kc_ref[b, hh]
        vt = vc_ref[b, hh]
        nkt = pltpu.roll(nk_ref[hh, :, tile_at], shift, axis=1)
        nvt = pltpu.roll(nv_ref[hh, :, tile_at], shift, axis=1)
        s1 = _dot(q, kt.astype(BF16)) + b1_ref[hh]
        s2 = _dot(q, nkt.astype(BF16)) + b2_ref[hh]
        m = jnp.maximum(jnp.max(s1, axis=-1, keepdims=True), jnp.max(s2, axis=-1, keepdims=True))
        p1 = jnp.exp(s1 - m)
        p2 = jnp.exp(s2 - m)
        z = jnp.sum(p1, axis=-1, keepdims=True) + jnp.sum(p2, axis=-1, keepdims=True)
        o = _dot_nt(p1.astype(BF16), vt.astype(BF16)) + _dot_nt(p2.astype(BF16), nvt.astype(BF16))
        o_ref[b, hh] = o / z
        l_ref[b, hh] = jnp.broadcast_to(m + jnp.log(z), (8, LANES))
        for src, new, dst in ((kt, nkt, ko_ref), (vt, nvt, vo_ref)):
            rolled = pltpu.roll(src, wb - 4, axis=1)
            if wb > LANES:
                dst[b, hh, :, 0:wb - LANES] = rolled[:, 0:wb - LANES]
            dst[b, hh, :, wb - LANES:wb] = jnp.where(new_cols, new, rolled[:, wb - LANES:wb])


N_RIDE_IN = 7
N_RIDE_OUT = 4


def _ride_specs(ride, bb, block_index):
    q, kc, vc, nk, nv, b1, b2 = ride
    nb, _, _, wb = kc.shape
    blk = lambda *tail: pl.BlockSpec((bb, HEADS_PER_GROUP) + tail, lambda *g: (block_index(*g), 0, 0, 0))
    const = lambda a: pl.BlockSpec(a.shape, lambda *g: (0,) * a.ndim)
    in_specs = [blk(8, HEAD_DIM), blk(HEAD_DIM, wb), blk(HEAD_DIM, wb), const(nk), const(nv), const(b1), const(b2)]
    out_specs = [blk(HEAD_DIM, wb), blk(HEAD_DIM, wb), blk(8, HEAD_DIM), blk(8, LANES)]
    out_shape = [jax.ShapeDtypeStruct(kc.shape, F32), jax.ShapeDtypeStruct(kc.shape, F32),
                 jax.ShapeDtypeStruct((nb, HEADS_PER_GROUP, 8, HEAD_DIM), F32),
                 jax.ShapeDtypeStruct((nb, HEADS_PER_GROUP, 8, LANES), F32)]
    return in_specs, out_specs, out_shape


def _ride_block(step, ride_refs):
    bb = ride_refs[0].shape[0]
    for b in range(bb):
        _attn_sample_batch(step * bb + b, b, *ride_refs)


def _first_max(v, ids, sentinel):
    m = jnp.max(v, axis=0, keepdims=True)
    idx = jnp.min(jnp.where(v == m, ids, sentinel), axis=0, keepdims=True)
    return m, ids == idx


def _oproj_kernel(*refs, tile, dils, ride):
    (x_ref, o0, o1, o2, l0, l1, l2, co_ref, g1_ref, sc_ref, sh_ref, ng_ref, wo_ref, wr_ref, br_ref) = refs[:15]
    refs = refs[15:]
    if ride:
        ride_in, refs = refs[:N_RIDE_IN], refs[N_RIDE_IN:]
        ride_out = refs[6:6 + N_RIDE_OUT]
        _ride_block(pl.program_id(0) * pl.num_programs(1) + pl.program_id(1), ride_in + ride_out)
    x1_ref, xg_ref, gt_ref, posr_ref, post_ref, cnt_ref = refs[:6]
    slab = refs[-1]
    o_refs, l_refs = (o0, o1, o2), (l0, l1, l2)
    outs, lses = [], []
    slab_i = 0
    for g, d in enumerate(dils):
        if d == 1:
            outs.append(o_refs[g][0, 0])
            lses.append(l_refs[g][0, 0])
            continue
        n = tile // d
        for ref, dest in ((o_refs[g], outs), (l_refs[g], lses)):
            halves = []
            for half in range(GROUP_W // LANES):
                for r in range(d):
                    slab[slab_i, pl.ds(r, n, stride=d), :] = ref[0, r, :, half * LANES:(half + 1) * LANES]
                halves.append(slab[slab_i])
                slab_i += 1
            dest.append(jnp.concatenate(halves, axis=1))

    m = jnp.maximum(jnp.maximum(lses[0], lses[1]), lses[2])
    es = [jnp.exp(l - m) for l in lses]
    den = es[0] + es[1] + es[2]
    cat = [(outs[g] * (es[g] / den)).astype(BF16) for g in range(N_GROUPS)]
    cat.append(co_ref[0].astype(BF16))
    cat = jnp.concatenate(cat, axis=1)
    x1 = x_ref[0] + g1_ref[0] * _dot(cat, wo_ref[...])
    x1_ref[0] = x1

    ms = jnp.mean(x1 * x1, axis=-1, keepdims=True)
    h2 = x1 * lax.rsqrt(ms + RMS_EPS) * ng_ref[...]
    h2 = h2 * (1.0 + sc_ref[0]) + sh_ref[0]
    hh = h2.astype(BF16)
    xg_ref[0, :, 0:D_MODEL] = hh

    hl = (h2 - hh.astype(F32)).astype(BF16)
    wr = wr_ref[...]
    wh = wr.astype(BF16)
    wl = (wr - wh.astype(F32)).astype(BF16)
    logits = _dot_nt(wh, hh) + _dot_nt(wh, hl) + _dot_nt(wl, hh)
    scores = 1.0 / (1.0 + jnp.exp(-logits))
    sel = scores + br_ref[...]

    ids = lax.broadcasted_iota(jnp.int32, (GROUP_SIZE, tile), 0)
    ninf = -jnp.inf
    sel_g = [sel[g * GROUP_SIZE:(g + 1) * GROUP_SIZE] for g in range(N_ROUTE_GROUPS)]
    gscore = jnp.zeros((N_ROUTE_GROUPS, tile), F32)
    for g in range(N_ROUTE_GROUPS):
        m1, oh = _first_max(sel_g[g], ids, GROUP_SIZE)
        m2 = jnp.max(jnp.where(oh, ninf, sel_g[g]), axis=0, keepdims=True)
        gscore = jnp.where(ids == g, m1 + m2, gscore)
    gsel = jnp.zeros((N_ROUTE_GROUPS, tile), F32)
    for _ in range(TOPK_GROUPS):
        _, oh = _first_max(gscore, ids, N_ROUTE_GROUPS)
        gsel = jnp.where(oh, 1.0, gsel)
        gscore = jnp.where(oh, ninf, gscore)
    cand = [jnp.where(gsel[g:g + 1] > 0.0, sel_g[g], ninf) for g in range(N_ROUTE_GROUPS)]
    chosen = [jnp.zeros((GROUP_SIZE, tile), F32) for _ in range(N_ROUTE_GROUPS)]
    picks = []
    for _ in range(TOP_K):
        mx = cand[0]
        for g in range(1, N_ROUTE_GROUPS):
            mx = jnp.maximum(mx, cand[g])
        mx = jnp.max(mx, axis=0, keepdims=True)
        idx = jnp.where(cand[0] == mx, ids, N_EXPERTS)
        for g in range(1, N_ROUTE_GROUPS):
            idx = jnp.minimum(idx, jnp.where(cand[g] == mx, ids + g * GROUP_SIZE, N_EXPERTS))
        idx = jnp.min(idx, axis=0, keepdims=True)
        ohs = [(ids + g * GROUP_SIZE) == idx for g in range(N_ROUTE_GROUPS)]
        picks.append(ohs)
        for g in range(N_ROUTE_GROUPS):
            chosen[g] = jnp.where(ohs[g], 1.0, chosen[g])
            cand[g] = jnp.where(ohs[g], ninf, cand[g])
    wts = [jnp.where(chosen[g] > 0.0, scores[g * GROUP_SIZE:(g + 1) * GROUP_SIZE], 0.0)
           for g in range(N_ROUTE_GROUPS)]
    wsum = jnp.sum(wts[0], axis=0, keepdims=True)
    for g in range(1, N_ROUTE_GROUPS):
        wsum = wsum + jnp.sum(wts[g], axis=0, keepdims=True)
    gates_t = jnp.concatenate([w / wsum * ROUTED_SCALE for w in wts], axis=0)
    gt_ref[0] = jnp.concatenate([gates_t, jnp.zeros((LANES - N_EXPERTS, tile), F32)], axis=0).T
    g_hi = gates_t.astype(BF16).astype(F32)
    xg_ref[0, :, D_MODEL:D_MODEL + LANES] = jnp.concatenate([g_hi, gates_t - g_hi], axis=0).T.astype(BF16)

    er = lax.broadcasted_iota(jnp.int32, (N_EXPERTS, N_EXPERTS), 0)
    ec = lax.broadcasted_iota(jnp.int32, (N_EXPERTS, N_EXPERTS), 1)
    lower = jnp.where(ec < er, 1.0, 0.0).astype(BF16)
    tr = lax.broadcasted_iota(jnp.int32, (SUB, SUB), 0)
    tc = lax.broadcasted_iota(jnp.int32, (SUB, SUB), 1)
    upper = jnp.where(tr < tc, 1.0, 0.0).astype(BF16)
    for u in range(tile // SUB):
        cols = slice(u * SUB, (u + 1) * SUB)
        ch = jnp.concatenate([c[:, cols] for c in chosen], axis=0)
        cnt = jnp.sum(ch, axis=1, keepdims=True)
        cpad = jnp.broadcast_to(jnp.ceil(cnt / CHUNK) * CHUNK, (N_EXPERTS, LANES))
        cnt_ref[u] = cpad
        seg0 = _dot(lower, cpad.astype(BF16))[:, 0:1]
        slot = seg0 + _dot(ch.astype(BF16), upper)
        rows = []
        for k in range(TOP_K):
            acc = jnp.where(picks[k][0][:, cols], slot[0:GROUP_SIZE], 0.0)
            for g in range(1, N_ROUTE_GROUPS):
                acc = acc + jnp.where(picks[k][g][:, cols], slot[g * GROUP_SIZE:(g + 1) * GROUP_SIZE], 0.0)
            rows.append(jnp.sum(acc, axis=0, keepdims=True))
        posr = jnp.concatenate(rows, axis=0)
        posr_ref[u] = posr
        post_ref[0, cols, :] = jnp.concatenate([posr, jnp.zeros((LANES - TOP_K, SUB), F32)], axis=0).T


def _oproj(x, o_list, l_list, co, g1, sc, sh, ng, wo_bf, wr_t, br, *, tile, dils, ride=None):
    B, S, _ = x.shape
    nt = S // tile
    r_in, r_out, r_shape, r_args = [], [], [], []
    if ride is not None:
        r_in, r_out, r_shape = _ride_specs(ride, ride[1].shape[0] // (B * nt), lambda b, j: b * nt + j)
        r_args = list(ride)
    tm = g1.shape[1]
    mod_spec = pl.BlockSpec((1, tm, D_MODEL), (lambda b, j: (b, j, 0)) if tm > 1 else (lambda b, j: (b, 0, 0)))
    const2 = lambda shape: pl.BlockSpec(shape, lambda b, j: (0, 0))
    nat = lambda w: pl.BlockSpec((1, tile, w), lambda b, j: (b, j, 0))
    dspec = [pl.BlockSpec((1, d, tile // d, GROUP_W), lambda b, j: (b, 0, j, 0)) for d in dils]
    n_slabs = sum(2 * (GROUP_W // LANES) for d in dils if d > 1)
    nsub_t = tile // SUB
    return pl.pallas_call(
        functools.partial(_oproj_kernel, tile=tile, dils=dils, ride=ride is not None),
        grid=(B, nt),
        in_specs=[nat(D_MODEL)] + dspec + dspec + [nat(CONV_CH), mod_spec, mod_spec, mod_spec,
                  const2((1, D_MODEL)), const2((D_MODEL, D_MODEL)), const2((N_EXPERTS, D_MODEL)),
                  const2((N_EXPERTS, 1))] + r_in,
        out_specs=[nat(D_MODEL), nat(XG_W), nat(LANES),
                   pl.BlockSpec((nsub_t, TOP_K, SUB), lambda b, j: (b * nt + j, 0, 0)),
                   nat(LANES),
                   pl.BlockSpec((nsub_t, N_EXPERTS, LANES), lambda b, j: (b * nt + j, 0, 0))] + r_out,
        out_shape=[jax.ShapeDtypeStruct((B, S, D_MODEL), F32), jax.ShapeDtypeStruct((B, S, XG_W), BF16),
                   jax.ShapeDtypeStruct((B, S, LANES), F32),
                   jax.ShapeDtypeStruct((B * S // SUB, TOP_K, SUB), F32),
                   jax.ShapeDtypeStruct((B, S, LANES), F32),
                   jax.ShapeDtypeStruct((B * S // SUB, N_EXPERTS, LANES), F32)] + r_shape,
        scratch_shapes=[pltpu.VMEM((max(n_slabs, 1), tile, LANES), F32)],
        compiler_params=_params(("arbitrary", "arbitrary")),
        name="oproj",
    )(x, *o_list, *l_list, co, g1, sc, sh, ng, wo_bf, wr_t, br, *r_args)


DENSE_EXPERTS_PER_STEP = 4


def _moe_kernel(h_ref, gt_ref, x1_ref, g2_ref, wg_ref, wu_ref, wd_ref, wgs_ref, wus_ref, wds_ref,
                y_ref, acc, *, tile):
    step = pl.program_id(2)
    h = h_ref[0]

    @pl.when(step == 0)
    def _():
        a = _dot(h, wgs_ref[...].astype(BF16))
        b = _dot(h, wus_ref[...].astype(BF16))
        acc[...] = _dot((_silu(a) * b).astype(BF16), wds_ref[...].astype(BF16))

    lane = lax.broadcasted_iota(jnp.int32, (tile, LANES), 1)
    routed = None
    for j in range(DENSE_EXPERTS_PER_STEP):
        e = step * DENSE_EXPERTS_PER_STEP + j
        a = _dot(h, wg_ref[j].astype(BF16))
        b = _dot(h, wu_ref[j].astype(BF16))
        gcol = jnp.sum(jnp.where(lane == e, gt_ref[0], 0.0), axis=1, keepdims=True)
        hc = _silu(a) * b * gcol
        y = _dot(hc.astype(BF16), wd_ref[j].astype(BF16))
        routed = y if routed is None else routed + y
    acc[...] += routed

    @pl.when(step == pl.num_programs(2) - 1)
    def _():
        y_ref[0] = x1_ref[0] + g2_ref[0] * acc[...]


def _moe(h2, gates, x1, g2, wg, wu, wd, wgs, wus, wds, *, tile):
    B, S, _ = h2.shape
    tm = g2.shape[1]
    mod_spec = pl.BlockSpec((1, tm, D_MODEL),
                            (lambda b, j, e: (b, j, 0)) if tm > 1 else (lambda b, j, e: (b, 0, 0)))
    nat = lambda w: pl.BlockSpec((1, tile, w), lambda b, j, e: (b, j, 0))
    const2 = lambda shape: pl.BlockSpec(shape, lambda b, j, e: (0, 0))
    return pl.pallas_call(
        functools.partial(_moe_kernel, tile=tile),
        grid=(B, S // tile, N_EXPERTS // DENSE_EXPERTS_PER_STEP),
        in_specs=[nat(D_MODEL), nat(LANES), nat(D_MODEL), mod_spec,
                  pl.BlockSpec((DENSE_EXPERTS_PER_STEP, D_MODEL, D_EXPERT), lambda b, j, e: (e, 0, 0)),
                  pl.BlockSpec((DENSE_EXPERTS_PER_STEP, D_MODEL, D_EXPERT), lambda b, j, e: (e, 0, 0)),
                  pl.BlockSpec((DENSE_EXPERTS_PER_STEP, D_EXPERT, D_MODEL), lambda b, j, e: (e, 0, 0)),
                  const2((D_MODEL, D_EXPERT)), const2((D_MODEL, D_EXPERT)), const2((D_EXPERT, D_MODEL))],
        out_specs=nat(D_MODEL),
        out_shape=jax.ShapeDtypeStruct((B, S, D_MODEL), F32),
        scratch_shapes=[pltpu.VMEM((tile, D_MODEL), F32)],
        compiler_params=_params(("arbitrary", "arbitrary", "arbitrary")),
        name="moe",
    )(h2, gates, x1, g2, wg, wu, wd, wgs, wus, wds)


FFN_TM = 1024
SUB_ROWS = SUB * TOP_K + N_EXPERTS * CHUNK
N_CHUNKS = SUB_ROWS // CHUNK
ZERO_CHUNK = N_CHUNKS
TABLE_W = 256
MXU_ROWS = 256
BLOCKS_PER_TRIP = 4


def _slot_onehot(first, pos_list, axis_iota):
    shape = axis_iota.shape
    hit = axis_iota == jnp.broadcast_to(pos_list[0] - first, shape).astype(BF16)
    for p in pos_list[1:]:
        hit = hit | (axis_iota == jnp.broadcast_to(p - first, shape).astype(BF16))
    return jnp.where(hit, jnp.ones(shape, BF16), jnp.zeros(shape, BF16))


def _chunk_copy(src, src_chunk, dst, dst_chunk, sem, n=1):
    return pltpu.make_async_copy(src.at[pl.ds(src_chunk, n)], dst.at[pl.ds(dst_chunk, n)], sem)


COPY_NONE, COPY_ONE, COPY_TWO = 0, 1, 2


def _by_kind(kind, fn):
    @pl.when(kind == COPY_ONE)
    def _():
        fn(1)

    @pl.when(kind == COPY_TWO)
    def _():
        fn(2)


def _for_each(n, body):
    def two(t, carry):
        body(2 * t)
        body(2 * t + 1)
        return carry

    lax.fori_loop(0, n // 2, two, 0)

    @pl.when(n % 2 == 1)
    def _():
        body(n - 1)


def _dispatch_kernel(nblk_ref, nreal_ref, ncopy_ref, ntile_ref, src_ref, dst_ref, kind_ref, posr_ref, xg_ref,
                     xs_hbm, xs_scr, zero_scr, sem, tail_sem, *, nsub, n_tiles_max):
    s = pl.program_id(0)

    tile_chunks = FFN_TM // CHUNK
    blk_chunks = MXU_ROWS // CHUNK
    zero_scr[...] = jnp.zeros((tile_chunks, CHUNK, XG_W), BF16)
    tail_blocks = [ntile_ref[0] + s + r * nsub for r in range(-(-(n_tiles_max) // nsub))]

    def tail_copy(t):
        return pltpu.make_async_copy(
            zero_scr, xs_hbm.at[pl.ds(pl.multiple_of(t * tile_chunks, tile_chunks), tile_chunks)], tail_sem)

    for t in tail_blocks:
        @pl.when(t < n_tiles_max)
        def _():
            tail_copy(t).start()

    xg = xg_ref[...]
    pos = [posr_ref[0, k:k + 1, :] for k in range(TOP_K)]
    rows = lax.broadcasted_iota(jnp.int32, (MXU_ROWS, SUB), 0).astype(BF16)
    xs_scr[ZERO_CHUNK] = jnp.zeros((CHUNK, XG_W), BF16)

    def sort_blocks(i, carry):
        for j in range(BLOCKS_PER_TRIP):
            blk = i * BLOCKS_PER_TRIP + j
            onehot = _slot_onehot((blk * MXU_ROWS).astype(F32), pos, rows)
            sorted_rows = _dot(onehot, xg).astype(BF16)
            xs_scr[pl.ds(pl.multiple_of(blk * blk_chunks, blk_chunks), blk_chunks)] = (
                sorted_rows.reshape(blk_chunks, CHUNK, XG_W))
        k0 = i * trip_chunks
        _for_each(jnp.clip(n_real - k0, 0, trip_chunks), lambda t: send(k0 + t))
        return carry

    n_real = nreal_ref[s]
    n = ncopy_ref[s]
    trip_chunks = BLOCKS_PER_TRIP * blk_chunks

    def send(i):
        _by_kind(kind_ref[0, 0, i],
                 lambda m: _chunk_copy(xs_scr, src_ref[0, 0, i], xs_hbm, dst_ref[0, 0, i], sem, m).start())

    lax.fori_loop(0, nblk_ref[s], sort_blocks, 0)
    _for_each(n - n_real, lambda t: send(n_real + t))
    _for_each(n, lambda i: _by_kind(kind_ref[0, 0, i], lambda m: _chunk_copy(xs_scr, 0, xs_hbm, 0, sem, m).wait()))
    for t in tail_blocks:
        @pl.when(t < n_tiles_max)
        def _():
            tail_copy(t).wait()


def _dispatch(nblk, n_real, ncopy, n_tiles, src, dst, kind, posr, xg, n_rows):
    nsub = posr.shape[0]
    tab = pl.BlockSpec((1, 1, TABLE_W), lambda s, *_: (s, 0, 0), memory_space=pltpu.SMEM)
    return pl.pallas_call(
        functools.partial(_dispatch_kernel, nsub=nsub, n_tiles_max=n_rows // FFN_TM),
        grid_spec=pltpu.PrefetchScalarGridSpec(
            num_scalar_prefetch=4, grid=(nsub,),
            in_specs=[tab, tab, tab, pl.BlockSpec((1, TOP_K, SUB), lambda s, *_: (s, 0, 0)),
                      pl.BlockSpec((SUB, XG_W), lambda s, *_: (s, 0))],
            out_specs=pl.BlockSpec(memory_space=pl.ANY),
            scratch_shapes=[pltpu.VMEM((N_CHUNKS + 1, CHUNK, XG_W), BF16),
                            pltpu.VMEM((FFN_TM // CHUNK, CHUNK, XG_W), BF16),
                            pltpu.SemaphoreType.DMA(()), pltpu.SemaphoreType.DMA(())]),
        out_shape=jax.ShapeDtypeStruct((n_rows // CHUNK, CHUNK, XG_W), BF16),
        compiler_params=_params(("arbitrary",)),
        name="moe_dispatch",
    )(nblk, n_real, ncopy, n_tiles, src, dst, kind, posr, xg).reshape(n_rows, XG_W)


def _ffn_kernel(te_ref, nt_ref, xs_ref, wg_ref, wu_ref, wd_ref,
                q_ref, kc_ref, vc_ref, nk_ref, nv_ref, b1_ref, b2_ref,
                ys_ref, ko_ref, vo_ref, o_ref, l_ref, wg_s, wu_s, wd_s, *, steps_per_batch):
    i = pl.program_id(0)
    n_heads = HEADS_PER_GROUP // steps_per_batch
    first_head = (i % steps_per_batch) * n_heads

    def ride():
        _attn_sample_batch(i // steps_per_batch, 0, q_ref, kc_ref, vc_ref, nk_ref, nv_ref, b1_ref, b2_ref,
                           ko_ref, vo_ref, o_ref, l_ref, heads=[first_head + j for j in range(n_heads)])

    @pl.when(i >= nt_ref[0])
    def _():
        ride()
        ys_ref[...] = jnp.zeros((FFN_TM, D_MODEL), BF16)

    @pl.when(i < nt_ref[0])
    def _():
        e = te_ref[i]

        @pl.when((i == 0) | (e != te_ref[jnp.maximum(i - 1, 0)]))
        def _():
            wg_s[...] = wg_ref[0].astype(BF16)
            wu_s[...] = wu_ref[0].astype(BF16)
            wd_s[...] = wd_ref[0].astype(BF16)

        ride()
        x = xs_ref[:, 0:D_MODEL]
        g = xs_ref[:, D_MODEL:XG_W].astype(F32)
        lane = lax.broadcasted_iota(jnp.int32, (FFN_TM, LANES), 1)
        gcol = jnp.sum(jnp.where((lane == e) | (lane == e + N_EXPERTS), g, 0.0), axis=1, keepdims=True)
        a = _dot(x, wg_s[...])
        b = _dot(x, wu_s[...])
        hc = _silu(a) * b * gcol
        ys_ref[...] = _dot(hc.astype(BF16), wd_s[...]).astype(BF16)


def _ffn(tile_expert, n_tiles, xs, wg, wu, wd, ride):
    q, kc, vc, nk, nv, b1, b2 = ride
    nb, _, _, wb = kc.shape
    n_rows = xs.shape[0]
    n_steps = n_rows // FFN_TM
    assert n_steps % nb == 0 and HEADS_PER_GROUP % (n_steps // nb) == 0
    per = n_steps // nb
    row = lambda i, te, nt: (jnp.minimum(i, nt[0] - 1), 0)
    wspec = lambda shape: pl.BlockSpec((1,) + shape, lambda i, te, nt: (te[i], 0, 0))
    rblk = lambda *tail: pl.BlockSpec((1, HEADS_PER_GROUP) + tail, lambda i, te, nt: (i // per, 0, 0, 0))
    const = lambda shape: pl.BlockSpec(shape, lambda i, te, nt: (0, 0, 0))
    return pl.pallas_call(
        functools.partial(_ffn_kernel, steps_per_batch=per),
        grid_spec=pltpu.PrefetchScalarGridSpec(
            num_scalar_prefetch=2, grid=(n_steps,),
            in_specs=[pl.BlockSpec((FFN_TM, XG_W), row), wspec((D_MODEL, D_EXPERT)),
                      wspec((D_MODEL, D_EXPERT)), wspec((D_EXPERT, D_MODEL)),
                      rblk(8, HEAD_DIM), rblk(HEAD_DIM, wb), rblk(HEAD_DIM, wb), const(nk.shape), const(nv.shape),
                      const(b1.shape), const(b2.shape)],
            out_specs=[pl.BlockSpec((FFN_TM, D_MODEL), lambda i, te, nt: (i, 0)),
                       rblk(HEAD_DIM, wb), rblk(HEAD_DIM, wb), rblk(8, HEAD_DIM), rblk(8, LANES)],
            scratch_shapes=[pltpu.VMEM((D_MODEL, D_EXPERT), BF16), pltpu.VMEM((D_MODEL, D_EXPERT), BF16),
                            pltpu.VMEM((D_EXPERT, D_MODEL), BF16)]),
        out_shape=[jax.ShapeDtypeStruct((n_rows, D_MODEL), BF16),
                   jax.ShapeDtypeStruct(kc.shape, F32), jax.ShapeDtypeStruct(kc.shape, F32),
                   jax.ShapeDtypeStruct((nb, HEADS_PER_GROUP, 8, HEAD_DIM), F32),
                   jax.ShapeDtypeStruct((nb, HEADS_PER_GROUP, 8, LANES), F32)],
        compiler_params=_params(("arbitrary",)),
        name="moe_ffn",
    )(tile_expert, n_tiles, xs, wg, wu, wd, q, kc, vc, nk, nv, b1, b2)


def _combine_kernel(nblk_ref, src_ref, kind_ref, post_ref, xg_ref, x1_ref, g2_ref, wgs_ref, wus_ref,
                    wds_ref, ys_hbm, y_ref, ys_scr, wgs_s, wus_s, wds_s, sem):
    s = pl.program_id(0)
    blk_chunks = MXU_ROWS // CHUNK
    trip_chunks = BLOCKS_PER_TRIP * blk_chunks

    def fetch_trip(t, carry):
        k0 = t * trip_chunks
        _for_each(trip_chunks, lambda u: _by_kind(
            kind_ref[0, 0, k0 + u],
            lambda m: _chunk_copy(ys_hbm, src_ref[0, 0, k0 + u], ys_scr, k0 + u, sem.at[t], m).start()))
        return carry

    lax.fori_loop(0, nblk_ref[s], fetch_trip, 0)

    @pl.when(s == 0)
    def _():
        wgs_s[...] = wgs_ref[...].astype(BF16)
        wus_s[...] = wus_ref[...].astype(BF16)
        wds_s[...] = wds_ref[...].astype(BF16)

    h = xg_ref[:, 0:D_MODEL]
    a = _dot(h, wgs_s[...])
    b = _dot(h, wus_s[...])
    shared = _dot((_silu(a) * b).astype(BF16), wds_s[...])

    post = post_ref[...]
    pos = [jnp.broadcast_to(post[:, k:k + 1], (SUB, MXU_ROWS)) for k in range(TOP_K)]
    lanes = lax.broadcasted_iota(jnp.int32, (SUB, MXU_ROWS), 1).astype(BF16)

    def gather_blocks(i, acc):
        _for_each(trip_chunks, lambda u: _by_kind(
            kind_ref[0, 0, i * trip_chunks + u],
            lambda m: _chunk_copy(ys_hbm, 0, ys_scr, 0, sem.at[i], m).wait()))
        for j in range(BLOCKS_PER_TRIP):
            blk = i * BLOCKS_PER_TRIP + j
            onehot = _slot_onehot((blk * MXU_ROWS).astype(F32), pos, lanes)
            rows = ys_scr[pl.ds(pl.multiple_of(blk * blk_chunks, blk_chunks), blk_chunks)]
            acc = acc + _dot(onehot, rows.reshape(MXU_ROWS, D_MODEL))
        return acc

    routed = lax.fori_loop(0, nblk_ref[s], gather_blocks, jnp.zeros((SUB, D_MODEL), F32))
    y_ref[...] = x1_ref[...] + g2_ref[0] * (routed + shared)


def _combine(nblk, src, kind, post, xg, x1, g2, wgs, wus, wds, ys):
    n_tok = x1.shape[0]
    nsub = n_tok // SUB
    per_b = nsub // g2.shape[0]
    tab = pl.BlockSpec((1, 1, TABLE_W), lambda s, *_: (s, 0, 0), memory_space=pltpu.SMEM)
    const2 = lambda shape: pl.BlockSpec(shape, lambda s, *_: (0, 0))
    return pl.pallas_call(
        _combine_kernel,
        grid_spec=pltpu.PrefetchScalarGridSpec(
            num_scalar_prefetch=1, grid=(nsub,),
            in_specs=[tab, tab, pl.BlockSpec((SUB, LANES), lambda s, *_: (s, 0)),
                      pl.BlockSpec((SUB, XG_W), lambda s, *_: (s, 0)),
                      pl.BlockSpec((SUB, D_MODEL), lambda s, *_: (s, 0)),
                      pl.BlockSpec((1, 1, D_MODEL), lambda s, *_: (s // per_b, 0, 0)),
                      const2((D_MODEL, D_EXPERT)), const2((D_MODEL, D_EXPERT)), const2((D_EXPERT, D_MODEL)),
                      pl.BlockSpec(memory_space=pl.ANY)],
            out_specs=pl.BlockSpec((SUB, D_MODEL), lambda s, *_: (s, 0)),
            scratch_shapes=[pltpu.VMEM((N_CHUNKS, CHUNK, D_MODEL), BF16), pltpu.VMEM((D_MODEL, D_EXPERT), BF16),
                            pltpu.VMEM((D_MODEL, D_EXPERT), BF16), pltpu.VMEM((D_EXPERT, D_MODEL), BF16),
                            pltpu.SemaphoreType.DMA((SUB_ROWS // (BLOCKS_PER_TRIP * MXU_ROWS),))]),
        out_shape=jax.ShapeDtypeStruct((n_tok, D_MODEL), F32),
        compiler_params=_params(("arbitrary",)),
        name="moe_combine",
    )(nblk, src, kind, post, xg, x1, g2, wgs, wus, wds, ys.reshape(-1, CHUNK, D_MODEL))


def _route_tables(cnt):
    nsub = cnt.shape[0]
    i32 = jnp.int32
    seg_end = jnp.cumsum(cnt, axis=1)
    seg_start = seg_end - cnt
    tot = jnp.sum(cnt, axis=0)
    tot_al = (tot + FFN_TM - 1) // FFN_TM * FFN_TM
    reg_end = jnp.cumsum(tot_al)
    reg_start = reg_end - tot_al
    base = reg_start[None, :] + jnp.cumsum(cnt, axis=0) - cnt
    n_real = seg_end[:, -1] // CHUNK
    k = jnp.arange(TABLE_W, dtype=i32)
    owner = jnp.sum((seg_end[:, None, :] // CHUNK) <= k[None, :, None], axis=2)
    owner = jnp.minimum(owner, N_EXPERTS - 1)
    delta = (base - seg_start) // CHUNK
    e_ids = jnp.arange(N_EXPERTS, dtype=i32)
    real_dst = jnp.sum(jnp.where(owner[:, :, None] == e_ids, delta[:, None, :], 0), axis=2) + k[None, :]
    fill_n = (tot_al - tot) // CHUNK
    fill_dst0 = (reg_start + tot) // CHUNK
    src = jnp.broadcast_to(k[None, :], (nsub, TABLE_W))
    dst = real_dst
    ncopy = n_real
    for r in range(-(-N_EXPERTS // nsub)):
        e_of = jnp.arange(nsub, dtype=i32) + r * nsub
        ok = e_of < N_EXPERTS
        e_cl = jnp.minimum(e_of, N_EXPERTS - 1)
        fn = jnp.where(ok, fill_n[e_cl], 0)
        j = k[None, :] - ncopy[:, None]
        is_fill = (j >= 0) & (j < fn[:, None])
        src = jnp.where(is_fill, ZERO_CHUNK, src)
        dst = jnp.where(is_fill, fill_dst0[e_cl][:, None] + j, dst)
        ncopy = ncopy + fn
    trip_rows = BLOCKS_PER_TRIP * MXU_ROWS
    trip_chunks = trip_rows // CHUNK
    disp_nblk = (n_real * CHUNK + trip_rows - 1) // trip_rows
    by_owner = lambda tab: jnp.sum(jnp.where(owner[:, :, None] == e_ids, tab[:, None, :], 0), axis=2)
    run_start = jnp.maximum(by_owner(seg_start // CHUNK), k[None, :] // trip_chunks * trip_chunks)
    run_end = jnp.minimum(by_owner(seg_end // CHUNK), (k[None, :] // trip_chunks + 1) * trip_chunks)
    is_real = k[None, :] < n_real[:, None]
    first = (k[None, :] - run_start) % 2 == 0
    kind = jnp.where(first, jnp.where(k[None, :] + 1 < run_end, COPY_TWO, COPY_ONE), COPY_NONE)
    disp_kind = jnp.where(is_real, kind, jnp.where(k[None, :] < ncopy[:, None], COPY_ONE, COPY_NONE))
    comb_n = disp_nblk * trip_chunks
    comb_src = jnp.where(is_real, real_dst, real_dst[:, 0:1])
    comb_kind = jnp.where(is_real, kind, jnp.where(k[None, :] < comb_n[:, None], COPY_ONE, COPY_NONE))
    n_tiles = reg_end[-1] // FFN_TM
    t = jnp.arange((nsub * SUB_ROWS + N_EXPERTS * FFN_TM) // FFN_TM, dtype=i32)
    tile_expert = jnp.minimum(jnp.sum((reg_end[None, :] // FFN_TM) <= t[:, None], axis=1), N_EXPERTS - 1)
    as3 = lambda a: a.astype(i32).reshape(nsub, 1, TABLE_W)
    return (disp_nblk.astype(i32), n_real.astype(i32), ncopy.astype(i32), as3(src), as3(dst), as3(disp_kind),
            as3(comb_src), as3(comb_kind), tile_expert.astype(i32), n_tiles.astype(i32).reshape(1))


def _moe_sparse(xg, post, posr, cnt, x1, g2, wg, wu, wd, wgs, wus, wds, ride):
    B, S, _ = x1.shape
    n_tok = B * S
    nsub = n_tok // SUB
    (disp_nblk, n_real, disp_n, disp_src, disp_dst, disp_kind, comb_src, comb_kind, tile_expert,
     n_tiles) = _route_tables(cnt[:, :, 0].astype(jnp.int32))
    n_rows = nsub * SUB_ROWS + N_EXPERTS * FFN_TM
    xg2 = xg.reshape(n_tok, XG_W)
    xs = _dispatch(disp_nblk, n_real, disp_n, n_tiles, disp_src, disp_dst, disp_kind, posr, xg2, n_rows)
    ys, *ride_out = _ffn(tile_expert, n_tiles, xs, wg, wu, wd, ride)
    y = _combine(disp_nblk, comb_src, comb_kind, post.reshape(n_tok, LANES), xg2, x1.reshape(n_tok, D_MODEL),
                 g2, wgs, wus, wds, ys)
    return y.reshape(B, S, D_MODEL), ride_out


def _t5_bucket(dist):
    max_exact = N_BUCKETS // 2
    df = jnp.maximum(dist, 1).astype(F32)
    large = max_exact + (jnp.log(df / max_exact) / math.log(MAX_DISTANCE / max_exact)
                         * (N_BUCKETS - max_exact)).astype(jnp.int32)
    large = jnp.minimum(large, N_BUCKETS - 1)
    return jnp.where(dist < max_exact, dist, large)


def _step_bias(rel_bias, g):
    dist = jnp.arange(N_KEYS, dtype=jnp.int32) * DILATIONS[g]
    cols = rel_bias[:, g * HEADS_PER_GROUP:(g + 1) * HEADS_PER_GROUP]
    return cols[_t5_bucket(dist)].T.astype(F32)


def _prompt_bias(bias_k):
    h = bias_k.shape[0]
    n = 3 * BLK
    row = jnp.concatenate([bias_k[:, ::-1], jnp.full((h, n - N_KEYS), NEG, F32)], axis=1)
    t = jnp.tile(row, (1, BLK))[:, :BLK * (n - 1)].reshape(h, BLK, n - 1)
    return t[:, :, :2 * BLK]


def _sample_bias(bias_k, wb, d):
    h = bias_k.shape[0]
    rev = bias_k[:, :0:-1]
    rows = []
    for t in range(4):
        if d == 1:
            rows.append(jnp.concatenate([jnp.full((h, t), NEG, F32), rev[:, :wb - t]], axis=1))
        else:
            r = jnp.arange(d)[None, None, :]
            rows.append(jnp.where(r == t, rev[:, :, None], NEG).reshape(h, wb))
    b1 = jnp.concatenate([jnp.stack(rows, axis=1), jnp.zeros((h, 4, wb), F32)], axis=1)
    rows = []
    for t in range(4):
        cols = [bias_k[:, (t - tn) // d] if (t >= tn and (t - tn) % d == 0) else jnp.full((h,), NEG, F32)
                for tn in range(4)]
        rows.append(jnp.concatenate([jnp.full((h, LANES - 4), NEG, F32), jnp.stack(cols, axis=1)], axis=1))
    b2 = jnp.concatenate([jnp.stack(rows, axis=1), jnp.zeros((h, 4, LANES), F32)], axis=1)
    return b1, b2


TILE = 512
RIDE_ON_OPROJ, RIDE_ON_PROJ, RIDE_ON_FFN = 0, 1, 2


def _sample_front(x, mods, hist, caches, rel_bias, weights):
    norm1_g, _, w_in_bf, qg, kg, conv_w, bd = weights[:7]
    sh1, sc1 = mods[:2]
    S = x.shape[1]
    nb = S // 4
    qf, kt, vt, u, co = _proj(x, sc1, sh1, norm1_g, w_in_bf, qg, kg, conv_w, bd, hist,
                              tile=TILE, dils=(1, 1, 1), sample=True)
    rides = []
    for g in range(N_GROUPS):
        d, wb = DILATIONS[g], WINDOWS[g]
        kc, vc = caches[g]
        b1, b2 = _sample_bias(_step_bias(rel_bias, g), wb, d)
        q = qf[0, :, g * GROUP_W:(g + 1) * GROUP_W].reshape(nb, 4, HEADS_PER_GROUP, HEAD_DIM)
        q = jnp.pad(q.transpose(0, 2, 1, 3), ((0, 0), (0, 0), (0, 4), (0, 0))).astype(BF16)
        new_t = lambda a: a[g * GROUP_W:(g + 1) * GROUP_W].reshape(HEADS_PER_GROUP, HEAD_DIM, S)
        rides.append((q, kc, vc, new_t(kt), new_t(vt), b1, b2))
    return dict(x=x, mods=mods, u=u, co=co), rides


def _sample_back(front, ride_outs, weights):
    (_, norm2_g, _, _, _, _, _, w_o_bf, w_r_t, b_r, w_gate_e, w_up_e, w_down_e,
     w_gate_s, w_up_s, w_down_s) = weights
    x, u, co = front["x"], front["u"], front["co"]
    _, _, g1, sh2, sc2, g2 = front["mods"]
    S = x.shape[1]
    nb = S // 4
    states, o_list, l_list = [], [], []
    for g in range(N_GROUPS):
        ko, vo, o, lse = ride_outs[g]
        states += [ko.transpose(0, 3, 1, 2)[None], vo.transpose(0, 3, 1, 2)[None]]
        o_list.append(o[:, :, :4].transpose(0, 2, 1, 3).reshape(1, 1, S, GROUP_W))
        lse = jnp.broadcast_to(lse[:, :, :4, :1], (nb, HEADS_PER_GROUP, 4, HEAD_DIM))
        l_list.append(lse.transpose(0, 2, 1, 3).reshape(1, 1, S, GROUP_W))
    states.append(u.reshape(nb, 4, CONV_CH)[:, 2:][None])
    x1, xg, gates, _, _, _ = _oproj(x, o_list, l_list, co, g1, sc2, sh2, norm2_g, w_o_bf, w_r_t, b_r,
                                    tile=TILE, dils=(1, 1, 1))
    y = _moe(xg, gates, x1, g2, w_gate_e, w_up_e, w_down_e, w_gate_s, w_up_s, w_down_s, tile=TILE)
    return y, states


def _prompt_layer(x, mods, rel_bias, weights, rides):
    (norm1_g, norm2_g, w_in_bf, qg, kg, conv_w, bd, w_o_bf, w_r_t, b_r, w_gate_e, w_up_e, w_down_e,
     w_gate_s, w_up_s, w_down_s) = weights
    sh1, sc1, g1, sh2, sc2, g2 = mods
    B, S, _ = x.shape
    ride_outs = [None] * N_GROUPS
    res = _proj(x, sc1, sh1, norm1_g, w_in_bf, qg, kg, conv_w, bd, None, tile=TILE, dils=DILATIONS, sample=False,
                ride=rides[RIDE_ON_PROJ])
    qkv, tails, (u, co), ride_outs[RIDE_ON_PROJ] = res[:9], res[9:15], res[15:17], res[17:]
    states, o_list, l_list = [], [], []
    for g in range(N_GROUPS):
        d, w = DILATIONS[g], WINDOWS[g]
        q, k, v = (a.reshape(B * d, S // d, GROUP_W) for a in qkv[3 * g:3 * g + 3])
        o, lse = _attn_prompt(q, k, v, _prompt_bias(_step_bias(rel_bias, g)))
        o_list.append(o.reshape(B, d, S // d, GROUP_W))
        l_list.append(lse.reshape(B, d, S // d, GROUP_W))
        keep = lambda a: a.reshape(B, HEADS_PER_GROUP, HEAD_DIM, w).transpose(0, 3, 1, 2)[None]
        states += [keep(tails[2 * g]), keep(tails[2 * g + 1])]
    states.append(u[:, 6:][None])
    x1, xg, _, posr, post, cnt, *ride_outs[RIDE_ON_OPROJ] = _oproj(
        x, o_list, l_list, co, g1, sc2, sh2, norm2_g, w_o_bf, w_r_t, b_r, tile=TILE, dils=DILATIONS,
        ride=rides[RIDE_ON_OPROJ])
    y, ride_outs[RIDE_ON_FFN] = _moe_sparse(xg, post, posr, cnt, x1, g2, w_gate_e, w_up_e, w_down_e, w_gate_s,
                                            w_up_s, w_down_s, rides[RIDE_ON_FFN])
    return y, states, ride_outs


def kernel(x_prompt, x_sample, c_prompt, c_sample, cache_k_w128, cache_v_w128, cache_k_w512, cache_v_w512, cache_k_w2048, cache_v_w2048, state_conv, rel_bias, norm1_g, norm2_g, w_ada, b_ada, w_in, q_norm_g, k_norm_g, conv_w, w_o, w_router, b_router, w_gate_e, w_up_e, w_down_e, w_gate_s, w_up_s, w_down_s):
    B = x_prompt.shape[0]
    DB, T = x_sample.shape[:2]
    n_c = B + DB
    c_all = jnp.pad(jnp.concatenate([c_prompt, c_sample], axis=0), ((0, (-n_c) % 8), (0, 0)))
    mod = _ada(c_all, w_ada[0], b_ada)
    chunks = [mod[:, i * D_MODEL:(i + 1) * D_MODEL] for i in range(6)]
    mods_p = [c[:B].reshape(B, 1, D_MODEL) for c in chunks]
    mods_s = [jnp.repeat(c[B:n_c], T, axis=0).reshape(1, DB * T, D_MODEL) for c in chunks]

    eye = jnp.arange(GROUP_W) // HEAD_DIM
    bd = jnp.where(eye[:, None] == eye[None, :], 1.0 / HEAD_DIM, 0.0).astype(BF16)
    weights = (norm1_g, norm2_g, w_in[0].astype(BF16),
               jnp.tile(q_norm_g, (1, HEADS_PER_GROUP)), jnp.tile(k_norm_g, (1, HEADS_PER_GROUP)),
               conv_w[0], bd, w_o[0].astype(BF16), w_router[0].T, b_router.reshape(N_EXPERTS, 1),
               w_gate_e[0], w_up_e[0], w_down_e[0], w_gate_s[0], w_up_s[0], w_down_s[0])

    s0, s1 = state_conv[0, :, 0], state_conv[0, :, 1]
    zero = jnp.zeros_like(s0)
    hist_a = jnp.stack([s0, s1, zero, zero], axis=1).reshape(DB * T, CONV_CH)
    hist_b = jnp.stack([s1, zero, zero, zero], axis=1).reshape(DB * T, CONV_CH)
    caches = [(ck[0].transpose(0, 2, 3, 1), cv[0].transpose(0, 2, 3, 1))
              for ck, cv in ((cache_k_w128, cache_v_w128), (cache_k_w512, cache_v_w512),
                             (cache_k_w2048, cache_v_w2048))]
    front, ride = _sample_front(x_sample.reshape(1, DB * T, D_MODEL), mods_s, (hist_a, hist_b), caches,
                                rel_bias, weights)
    yp, st_p, ride_out = _prompt_layer(x_prompt, mods_p, rel_bias, weights, ride)
    ys, st_s = _sample_back(front, ride_out, weights)
    return (yp, ys.reshape(DB, T, D_MODEL), *st_p, *st_s)
```

```python
import functools
import math

import jax
import jax.numpy as jnp
from jax import lax
from jax.experimental import pallas as pl
from jax.experimental.pallas import tpu as pltpu

D_MODEL = 1024
HEAD_DIM = 64
HEADS_PER_GROUP = 4
GROUP_W = HEADS_PER_GROUP * HEAD_DIM
WINDOWS = (128, 512, 2048)
DILATIONS = (1, 4, 16)
N_GROUPS = 3
ATTN_W = N_GROUPS * GROUP_W
CONV_CH = 256
N_PROJ = 3 * ATTN_W + 3 * CONV_CH
N_STEPS = 128
N_KEYS = N_STEPS + 1
BLK = 128
N_BUCKETS = 32
MAX_DISTANCE = WINDOWS[-1]
N_EXPERTS = 64
TOP_K = 8
N_ROUTE_GROUPS = 8
GROUP_SIZE = N_EXPERTS // N_ROUTE_GROUPS
TOPK_GROUPS = 4
D_EXPERT = 256
ROUTED_SCALE = 2.5
RMS_EPS = 1e-6
NEG = -1e30
LANES = 128
SUB = 256
CHUNK = 16
XG_W = D_MODEL + LANES
VMEM_LIMIT = 56 * 1024 * 1024

F32 = jnp.float32
BF16 = jnp.bfloat16


def _dot(a, b):
    return jnp.dot(a, b, preferred_element_type=F32)


def _dot_nt(a, b):
    return lax.dot_general(a, b, (((1,), (1,)), ((), ())), preferred_element_type=F32)


def _silu(a):
    return a / (1.0 + jnp.exp(-a))


def _params(sem):
    return pltpu.CompilerParams(dimension_semantics=sem, vmem_limit_bytes=VMEM_LIMIT)


def _ada_kernel(c_ref, w_ref, b_ref, o_ref):
    s = _silu(c_ref[...])
    o_ref[...] = _dot(s.astype(BF16), w_ref[...].astype(BF16)) + b_ref[...]


def _ada(c, w_ada, b_ada):
    n = c.shape[0]
    nc = w_ada.shape[1] // D_MODEL
    return pl.pallas_call(
        _ada_kernel,
        grid=(nc,),
        in_specs=[pl.BlockSpec((n, D_MODEL), lambda j: (0, 0)),
                  pl.BlockSpec((D_MODEL, D_MODEL), lambda j: (0, j)),
                  pl.BlockSpec((1, D_MODEL), lambda j: (0, j))],
        out_specs=pl.BlockSpec((n, D_MODEL), lambda j: (0, j)),
        out_shape=jax.ShapeDtypeStruct((n, w_ada.shape[1]), F32),
        compiler_params=_params(("arbitrary",)),
        name="ada",
    )(c, w_ada, b_ada)


def _proj_kernel(*refs, tile, dils, sample, nt):
    if sample:
        (x_ref, sc_ref, sh_ref, ng_ref, w_ref, qg_ref, kg_ref, cw_ref, bd_ref, ha_ref, hb_ref,
         qf_ref, kt_ref, vt_ref, u_ref, co_ref, u_scr) = refs
    else:
        (x_ref, sc_ref, sh_ref, ng_ref, w_ref, qg_ref, kg_ref, cw_ref, bd_ref) = refs[:9]
        ride_in, refs = refs[9:9 + N_RIDE_IN], refs[9 + N_RIDE_IN:]
        (q0, k0, v0, q1, k1, v1, q2, k2, v2, kt0, vt0, kt1, vt1, kt2, vt2, u_ref, co_ref) = refs[:17]
        ride_out, (slab, u_scr) = refs[17:17 + N_RIDE_OUT], refs[17 + N_RIDE_OUT:]
        qkv_out = ((q0, k0, v0), (q1, k1, v1), (q2, k2, v2))
        tails = ((kt0, vt0), (kt1, vt1), (kt2, vt2))
        j = pl.program_id(1)
        _ride_block(pl.program_id(0) * nt + j, ride_in + ride_out)

    x = x_ref[0]
    ms = jnp.mean(x * x, axis=-1, keepdims=True)
    h = x * lax.rsqrt(ms + RMS_EPS) * ng_ref[...]
    h = h * (1.0 + sc_ref[0]) + sh_ref[0]
    proj = _dot(h.astype(BF16), w_ref[...])

    bd = bd_ref[...]

    def headnorm(z, g):
        msq = _dot((z * z).astype(BF16), bd)
        return z * lax.rsqrt(msq + RMS_EPS) * g

    slab_i = 0
    for g in range(N_GROUPS):
        c0 = g * GROUP_W
        qn = headnorm(proj[:, c0:c0 + GROUP_W], qg_ref[...]) * (HEAD_DIM ** -0.5)
        kn = headnorm(proj[:, ATTN_W + c0:ATTN_W + c0 + GROUP_W], kg_ref[...])
        vv = proj[:, 2 * ATTN_W + c0:2 * ATTN_W + c0 + GROUP_W]
        if sample:
            qf_ref[0, :, c0:c0 + GROUP_W] = qn
            kt_ref[c0:c0 + GROUP_W, :] = kn.T
            vt_ref[c0:c0 + GROUP_W, :] = vv.T
            continue
        keep = min(WINDOWS[g], tile)

        @pl.when(j >= nt - max(WINDOWS[g] // tile, 1))
        def _(kn=kn, vv=vv, g=g, keep=keep):
            tails[g][0][0] = kn[tile - keep:, :].T
            tails[g][1][0] = vv[tile - keep:, :].T

        d = dils[g]
        for val, out in zip((qn, kn, vv), qkv_out[g]):
            if d == 1:
                out[0, 0] = val.astype(BF16)
                continue
            n = tile // d
            for half in range(GROUP_W // LANES):
                slab[slab_i] = val[:, half * LANES:(half + 1) * LANES]
                for r in range(d):
                    out[0, r, :, half * LANES:(half + 1) * LANES] = (
                        slab[slab_i, pl.ds(r, n, stride=d), :].astype(BF16))
                slab_i += 1

    base = 3 * ATTN_W
    u = proj[:, base + 2 * CONV_CH:base + 3 * CONV_CH] * proj[:, base:base + CONV_CH]
    gate_b = proj[:, base + CONV_CH:base + 2 * CONV_CH]
    if sample:
        u_scr[0:8, :] = jnp.zeros((8, CONV_CH), F32)
    else:
        @pl.when(j == 0)
        def _():
            u_scr[0:8, :] = jnp.zeros((8, CONV_CH), F32)

        @pl.when(j > 0)
        def _():
            u_scr[0:8, :] = u_scr[tile:tile + 8, :]

    u_scr[8:tile + 8, :] = u
    um1 = u_scr[7:tile + 7, :]
    um2 = u_scr[6:tile + 6, :]
    if sample:
        t = lax.broadcasted_iota(jnp.int32, (tile, CONV_CH), 0) % 4
        um1 = jnp.where(t >= 1, um1, 0.0) + hb_ref[...]
        um2 = jnp.where(t >= 2, um2, 0.0) + ha_ref[...]
    cw = cw_ref[...]
    conv = cw[0:1] * um2 + cw[1:2] * um1 + cw[2:3] * u
    u_ref[0] = u if sample else u[tile - 8:, :]
    co_ref[0] = (gate_b * conv).astype(BF16)


def _proj(x, sc, sh, ng, w_bf, qg, kg, cw, bd, hist, *, tile, dils, sample, ride=None):
    B, S, _ = x.shape
    nt = S // tile
    tm = sc.shape[1]
    mod_spec = pl.BlockSpec((1, tm, D_MODEL), (lambda b, j: (b, j, 0)) if tm > 1 else (lambda b, j: (b, 0, 0)))
    const2 = lambda shape: pl.BlockSpec(shape, lambda b, j: (0, 0))
    in_specs = [pl.BlockSpec((1, tile, D_MODEL), lambda b, j: (b, j, 0)), mod_spec, mod_spec,
                const2((1, D_MODEL)), const2((D_MODEL, N_PROJ)), const2((1, GROUP_W)),
                const2((1, GROUP_W)), const2((3, CONV_CH)), const2((GROUP_W, GROUP_W))]
    args = [x, sc, sh, ng, w_bf, qg, kg, cw, bd]
    nat = lambda w: pl.BlockSpec((1, tile, w), lambda b, j: (b, j, 0))
    out_specs, out_shape = [], []
    scratch = []
    if sample:
        assert B == 1 and nt == 1
        in_specs += [pl.BlockSpec((tile, CONV_CH), lambda b, j: (j, 0))] * 2
        args += list(hist)
        out_specs += [nat(ATTN_W)] + [pl.BlockSpec((ATTN_W, tile), lambda b, j: (0, j))] * 2 + [nat(CONV_CH)]
        out_shape += [jax.ShapeDtypeStruct((B, S, ATTN_W), F32)]
        out_shape += [jax.ShapeDtypeStruct((ATTN_W, S), F32)] * 2
        out_shape += [jax.ShapeDtypeStruct((B, S, CONV_CH), F32)]
    else:
        for d in dils:
            for _ in range(3):
                out_specs.append(pl.BlockSpec((1, d, tile // d, GROUP_W), lambda b, j: (b, 0, j, 0)))
                out_shape.append(jax.ShapeDtypeStruct((B, d, S // d, GROUP_W), BF16))
        for w in WINDOWS:
            keep, first = min(w, tile), nt - max(w // tile, 1)
            for _ in range(2):
                out_specs.append(pl.BlockSpec((1, GROUP_W, keep),
                                              lambda b, j, first=first: (b, 0, jnp.maximum(j - first, 0))))
                out_shape.append(jax.ShapeDtypeStruct((B, GROUP_W, w), F32))
        out_specs.append(pl.BlockSpec((1, 8, CONV_CH), lambda b, j: (b, 0, 0)))
        out_shape.append(jax.ShapeDtypeStruct((B, 8, CONV_CH), F32))
        n_slabs = sum(3 * (GROUP_W // LANES) for d in dils if d > 1)
        scratch.append(pltpu.VMEM((n_slabs, tile, LANES), F32))
    out_specs.append(nat(CONV_CH))
    out_shape.append(jax.ShapeDtypeStruct((B, S, CONV_CH), BF16))
    scratch.append(pltpu.VMEM((tile + 8, CONV_CH), F32))
    if not sample:
        nb = ride[1].shape[0]
        r_in, r_out, r_shape = _ride_specs(ride, nb // (B * nt), lambda b, j: b * nt + j)
        in_specs, args = in_specs + r_in, args + list(ride)
        out_specs, out_shape = out_specs + r_out, out_shape + r_shape
    return pl.pallas_call(
        functools.partial(_proj_kernel, tile=tile, dils=dils, sample=sample, nt=nt),
        grid=(B, nt),
        in_specs=in_specs,
        out_specs=out_specs,
        out_shape=out_shape,
        scratch_shapes=scratch,
        compiler_params=_params(("arbitrary", "arbitrary")),
        name="proj_sample" if sample else "proj_prompt",
    )(*args)


def _attn_kernel(q_ref, kp_ref, kc_ref, vp_ref, vc_ref, b_ref, o_ref, l_ref, s_scr, m_scr, o_scr, z_scr,
                 *, nq):
    first = pl.program_id(1) == 0
    low = lax.broadcasted_iota(jnp.int32, (BLK, LANES), 1) < HEAD_DIM
    ones = jnp.ones((BLK, LANES), BF16)
    chains = [(j, pair, sub) for j in range(nq) for pair in range(GROUP_W // LANES) for sub in range(2)]

    def operands(j, pair):
        rows = slice(j * BLK, (j + 1) * BLK)
        sl = slice(pair * LANES, (pair + 1) * LANES)
        if j == 0:
            return rows, sl, kp_ref[0, :, sl], vp_ref[0, :, sl]
        prows = slice((j - 1) * BLK, j * BLK)
        return rows, sl, kc_ref[0, prows, sl], vc_ref[0, prows, sl]

    for c, (j, pair, sub) in enumerate(chains):
        rows, sl, kp, _ = operands(j, pair)
        q = q_ref[0, rows, sl]
        qm = jnp.where(low if sub == 0 else ~low, q, jnp.zeros_like(q))
        hh = 2 * pair + sub
        sp = _dot_nt(qm, kp) + b_ref[hh, :, 0:BLK]
        if j == 0:
            sp = jnp.where(first, NEG, sp)
        sc = _dot_nt(qm, kc_ref[0, rows, sl]) + b_ref[hh, :, BLK:2 * BLK]
        s_scr[c, :, 0:BLK] = sp
        s_scr[c, :, BLK:2 * BLK] = sc
        m_scr[c] = jnp.broadcast_to(jnp.max(jnp.maximum(sp, sc), axis=-1, keepdims=True), (BLK, LANES))

    for c, (j, pair, sub) in enumerate(chains):
        rows, sl, _, vp = operands(j, pair)
        m = m_scr[c]
        pp = jnp.exp(s_scr[c, :, 0:BLK] - m).astype(BF16)
        pc = jnp.exp(s_scr[c, :, BLK:2 * BLK] - m).astype(BF16)
        o_scr[c] = _dot(pp, vp) + _dot(pc, vc_ref[0, rows, sl])
        z_scr[c] = _dot(pp, ones) + _dot(pc, ones)

    for c in range(0, len(chains), 2):
        j, pair, _ = chains[c]
        rows = slice(j * BLK, (j + 1) * BLK)
        sl = slice(pair * LANES, (pair + 1) * LANES)
        o_sub = [o_scr[c + sub] / z_scr[c + sub] for sub in range(2)]
        l_sub = [m_scr[c + sub] + jnp.log(z_scr[c + sub]) for sub in range(2)]
        o_ref[0, rows, sl] = jnp.where(low, o_sub[0], o_sub[1])
        l_ref[0, rows, sl] = jnp.where(low, l_sub[0], l_sub[1])


def _attn_prompt(q, k, v, bias):
    Z, L, _ = q.shape
    qt = min(L, 4 * BLK)
    nq = qt // BLK
    cur = pl.BlockSpec((1, qt, GROUP_W), lambda z, i: (z, i, 0))
    prev = pl.BlockSpec((1, BLK, GROUP_W), lambda z, i: (z, jnp.maximum(i * nq - 1, 0), 0))
    return pl.pallas_call(
        functools.partial(_attn_kernel, nq=nq),
        grid=(Z, L // qt),
        in_specs=[cur, prev, cur, prev, cur,
                  pl.BlockSpec((HEADS_PER_GROUP, BLK, 2 * BLK), lambda z, i: (0, 0, 0))],
        out_specs=[cur, cur],
        out_shape=[jax.ShapeDtypeStruct((Z, L, GROUP_W), F32)] * 2,
        scratch_shapes=[pltpu.VMEM((nq * HEADS_PER_GROUP, BLK, 2 * BLK), F32)]
        + [pltpu.VMEM((nq * HEADS_PER_GROUP, BLK, LANES), F32)] * 3,
        compiler_params=_params(("arbitrary", "arbitrary")),
        name="attn_prompt",
    )(q, k, k, v, v, bias)


def _attn_sample_batch(gb, b, q_ref, kc_ref, vc_ref, nk_ref, nv_ref, b1_ref, b2_ref, ko_ref, vo_ref, o_ref,
                       l_ref, heads=range(HEADS_PER_GROUP)):
    wb = kc_ref.shape[-1]
    lane = lax.broadcasted_iota(jnp.int32, (HEAD_DIM, LANES), 1)
    new_cols = lane >= LANES - 4
    per_tile = LANES // 4
    tile_at = pl.ds(pl.multiple_of((gb // per_tile) * LANES, LANES), LANES)
    shift = (LANES - 4) - 4 * (gb % per_tile)
    for hh in heads:
        q = q_ref[b, hh]
        kt = kc_ref[b, hh]
        vt = vc_ref[b, hh]
        nkt = pltpu.roll(nk_ref[hh, :, tile_at], shift, axis=1)
        nvt = pltpu.roll(nv_ref[hh, :, tile_at], shift, axis=1)
        s1 = _dot(q, kt.astype(BF16)) + b1_ref[hh]
        s2 = _dot(q, nkt.astype(BF16)) + b2_ref[hh]
        m = jnp.maximum(jnp.max(s1, axis=-1, keepdims=True), jnp.max(s2, axis=-1, keepdims=True))
        p1 = jnp.exp(s1 - m)
        p2 = jnp.exp(s2 - m)
        z = jnp.sum(p1, axis=-1, keepdims=True) + jnp.sum(p2, axis=-1, keepdims=True)
        o = _dot_nt(p1.astype(BF16), vt.astype(BF16)) + _dot_nt(p2.astype(BF16), nvt.astype(BF16))
        o_ref[b, hh] = o / z
        l_ref[b, hh] = jnp.broadcast_to(m + jnp.log(z), (8, LANES))
        for src, new, dst in ((kt, nkt, ko_ref), (vt, nvt, vo_ref)):
            rolled = pltpu.roll(src, wb - 4, axis=1)
            if wb > LANES:
                dst[b, hh, :, 0:wb - LANES] = rolled[:, 0:wb - LANES]
            dst[b, hh, :, wb - LANES:wb] = jnp.where(new_cols, new, rolled[:, wb - LANES:wb])


N_RIDE_IN = 7
N_RIDE_OUT = 4


def _ride_specs(ride, bb, block_index):
    q, kc, vc, nk, nv, b1, b2 = ride
    nb, _, _, wb = kc.shape
    blk = lambda *tail: pl.BlockSpec((bb, HEADS_PER_GROUP) + tail, lambda *g: (block_index(*g), 0, 0, 0))
    const = lambda a: pl.BlockSpec(a.shape, lambda *g: (0,) * a.ndim)
    in_specs = [blk(8, HEAD_DIM), blk(HEAD_DIM, wb), blk(HEAD_DIM, wb), const(nk), const(nv), const(b1), const(b2)]
    out_specs = [blk(HEAD_DIM, wb), blk(HEAD_DIM, wb), blk(8, HEAD_DIM), blk(8, LANES)]
    out_shape = [jax.ShapeDtypeStruct(kc.shape, F32), jax.ShapeDtypeStruct(kc.shape, F32),
                 jax.ShapeDtypeStruct((nb, HEADS_PER_GROUP, 8, HEAD_DIM), F32),
                 jax.ShapeDtypeStruct((nb, HEADS_PER_GROUP, 8, LANES), F32)]
    return in_specs, out_specs, out_shape


def _ride_block(step, ride_refs):
    bb = ride_refs[0].shape[0]
    for b in range(bb):
        _attn_sample_batch(step * bb + b, b, *ride_refs)


def _first_max(v, ids, sentinel):
    m = jnp.max(v, axis=0, keepdims=True)
    idx = jnp.min(jnp.where(v == m, ids, sentinel), axis=0, keepdims=True)
    return m, ids == idx


def _oproj_kernel(*refs, tile, dils, ride):
    (x_ref, o0, o1, o2, l0, l1, l2, co_ref, g1_ref, sc_ref, sh_ref, ng_ref, wo_ref, wr_ref, br_ref) = refs[:15]
    refs = refs[15:]
    if ride:
        ride_in, refs = refs[:N_RIDE_IN], refs[N_RIDE_IN:]
        ride_out = refs[6:6 + N_RIDE_OUT]
        _ride_block(pl.program_id(0) * pl.num_programs(1) + pl.program_id(1), ride_in + ride_out)
    x1_ref, xg_ref, gt_ref, posr_ref, post_ref, cnt_ref = refs[:6]
    slab = refs[-1]
    o_refs, l_refs = (o0, o1, o2), (l0, l1, l2)
    outs, lses = [], []
    slab_i = 0
    for g, d in enumerate(dils):
        if d == 1:
            outs.append(o_refs[g][0, 0])
            lses.append(l_refs[g][0, 0])
            continue
        n = tile // d
        for ref, dest in ((o_refs[g], outs), (l_refs[g], lses)):
            halves = []
            for half in range(GROUP_W // LANES):
                for r in range(d):
                    slab[slab_i, pl.ds(r, n, stride=d), :] = ref[0, r, :, half * LANES:(half + 1) * LANES]
                halves.append(slab[slab_i])
                slab_i += 1
            dest.append(jnp.concatenate(halves, axis=1))

    m = jnp.maximum(jnp.maximum(lses[0], lses[1]), lses[2])
    es = [jnp.exp(l - m) for l in lses]
    den = es[0] + es[1] + es[2]
    cat = [(outs[g] * (es[g] / den)).astype(BF16) for g in range(N_GROUPS)]
    cat.append(co_ref[0].astype(BF16))
    cat = jnp.concatenate(cat, axis=1)
    x1 = x_ref[0] + g1_ref[0] * _dot(cat, wo_ref[...])
    x1_ref[0] = x1

    ms = jnp.mean(x1 * x1, axis=-1, keepdims=True)
    h2 = x1 * lax.rsqrt(ms + RMS_EPS) * ng_ref[...]
    h2 = h2 * (1.0 + sc_ref[0]) + sh_ref[0]
    hh = h2.astype(BF16)
    xg_ref[0, :, 0:D_MODEL] = hh

    hl = (h2 - hh.astype(F32)).astype(BF16)
    wr = wr_ref[...]
    wh = wr.astype(BF16)
    wl = (wr - wh.astype(F32)).astype(BF16)
    logits = _dot_nt(wh, hh) + _dot_nt(wh, hl) + _dot_nt(wl, hh)
    scores = 1.0 / (1.0 + jnp.exp(-logits))
    sel = scores + br_ref[...]

    ids = lax.broadcasted_iota(jnp.int32, (GROUP_SIZE, tile), 0)
    ninf = -jnp.inf
    sel_g = [sel[g * GROUP_SIZE:(g + 1) * GROUP_SIZE] for g in range(N_ROUTE_GROUPS)]
    gscore = jnp.zeros((N_ROUTE_GROUPS, tile), F32)
    for g in range(N_ROUTE_GROUPS):
        m1, oh = _first_max(sel_g[g], ids, GROUP_SIZE)
        m2 = jnp.max(jnp.where(oh, ninf, sel_g[g]), axis=0, keepdims=True)
        gscore = jnp.where(ids == g, m1 + m2, gscore)
    gsel = jnp.zeros((N_ROUTE_GROUPS, tile), F32)
    for _ in range(TOPK_GROUPS):
        _, oh = _first_max(gscore, ids, N_ROUTE_GROUPS)
        gsel = jnp.where(oh, 1.0, gsel)
        gscore = jnp.where(oh, ninf, gscore)
    cand = [jnp.where(gsel[g:g + 1] > 0.0, sel_g[g], ninf) for g in range(N_ROUTE_GROUPS)]
    chosen = [jnp.zeros((GROUP_SIZE, tile), F32) for _ in range(N_ROUTE_GROUPS)]
    picks = []
    for _ in range(TOP_K):
        mx = cand[0]
        for g in range(1, N_ROUTE_GROUPS):
            mx = jnp.maximum(mx, cand[g])
        mx = jnp.max(mx, axis=0, keepdims=True)
        idx = jnp.where(cand[0] == mx, ids, N_EXPERTS)
        for g in range(1, N_ROUTE_GROUPS):
            idx = jnp.minimum(idx, jnp.where(cand[g] == mx, ids + g * GROUP_SIZE, N_EXPERTS))
        idx = jnp.min(idx, axis=0, keepdims=True)
        ohs = [(ids + g * GROUP_SIZE) == idx for g in range(N_ROUTE_GROUPS)]
        picks.append(ohs)
        for g in range(N_ROUTE_GROUPS):
            chosen[g] = jnp.where(ohs[g], 1.0, chosen[g])
            cand[g] = jnp.where(ohs[g], ninf, cand[g])
    wts = [jnp.where(chosen[g] > 0.0, scores[g * GROUP_SIZE:(g + 1) * GROUP_SIZE], 0.0)
           for g in range(N_ROUTE_GROUPS)]
    wsum = jnp.sum(wts[0], axis=0, keepdims=True)
    for g in range(1, N_ROUTE_GROUPS):
        wsum = wsum + jnp.sum(wts[g], axis=0, keepdims=True)
    gates_t = jnp.concatenate([w / wsum * ROUTED_SCALE for w in wts], axis=0)
    gt_ref[0] = jnp.concatenate([gates_t, jnp.zeros((LANES - N_EXPERTS, tile), F32)], axis=0).T
    g_hi = gates_t.astype(BF16).astype(F32)
    xg_ref[0, :, D_MODEL:D_MODEL + LANES] = jnp.concatenate([g_hi, gates_t - g_hi], axis=0).T.astype(BF16)

    er = lax.broadcasted_iota(jnp.int32, (N_EXPERTS, N_EXPERTS), 0)
    ec = lax.broadcasted_iota(jnp.int32, (N_EXPERTS, N_EXPERTS), 1)
    lower = jnp.where(ec < er, 1.0, 0.0).astype(BF16)
    tr = lax.broadcasted_iota(jnp.int32, (SUB, SUB), 0)
    tc = lax.broadcasted_iota(jnp.int32, (SUB, SUB), 1)
    upper = jnp.where(tr < tc, 1.0, 0.0).astype(BF16)
    for u in range(tile // SUB):
        cols = slice(u * SUB, (u + 1) * SUB)
        ch = jnp.concatenate([c[:, cols] for c in chosen], axis=0)
        cnt = jnp.sum(ch, axis=1, keepdims=True)
        cpad = jnp.broadcast_to(jnp.ceil(cnt / CHUNK) * CHUNK, (N_EXPERTS, LANES))
        cnt_ref[u] = cpad
        seg0 = _dot(lower, cpad.astype(BF16))[:, 0:1]
        slot = seg0 + _dot(ch.astype(BF16), upper)
        rows = []
        for k in range(TOP_K):
            acc = jnp.where(picks[k][0][:, cols], slot[0:GROUP_SIZE], 0.0)
            for g in range(1, N_ROUTE_GROUPS):
                acc = acc + jnp.where(picks[k][g][:, cols], slot[g * GROUP_SIZE:(g + 1) * GROUP_SIZE], 0.0)
            rows.append(jnp.sum(acc, axis=0, keepdims=True))
        posr = jnp.concatenate(rows, axis=0)
        posr_ref[u] = posr
        post_ref[0, cols, :] = jnp.concatenate([posr, jnp.zeros((LANES - TOP_K, SUB), F32)], axis=0).T


def _oproj(x, o_list, l_list, co, g1, sc, sh, ng, wo_bf, wr_t, br, *, tile, dils, ride=None):
    B, S, _ = x.shape
    nt = S // tile
    r_in, r_out, r_shape, r_args = [], [], [], []
    if ride is not None:
        r_in, r_out, r_shape = _ride_specs(ride, ride[1].shape[0] // (B * nt), lambda b, j: b * nt + j)
        r_args = list(ride)
    tm = g1.shape[1]
    mod_spec = pl.BlockSpec((1, tm, D_MODEL), (lambda b, j: (b, j, 0)) if tm > 1 else (lambda b, j: (b, 0, 0)))
    const2 = lambda shape: pl.BlockSpec(shape, lambda b, j: (0, 0))
    nat = lambda w: pl.BlockSpec((1, tile, w), lambda b, j: (b, j, 0))
    dspec = [pl.BlockSpec((1, d, tile // d, GROUP_W), lambda b, j: (b, 0, j, 0)) for d in dils]
    n_slabs = sum(2 * (GROUP_W // LANES) for d in dils if d > 1)
    nsub_t = tile // SUB
    return pl.pallas_call(
        functools.partial(_oproj_kernel, tile=tile, dils=dils, ride=ride is not None),
        grid=(B, nt),
        in_specs=[nat(D_MODEL)] + dspec + dspec + [nat(CONV_CH), mod_spec, mod_spec, mod_spec,
                  const2((1, D_MODEL)), const2((D_MODEL, D_MODEL)), const2((N_EXPERTS, D_MODEL)),
                  const2((N_EXPERTS, 1))] + r_in,
        out_specs=[nat(D_MODEL), nat(XG_W), nat(LANES),
                   pl.BlockSpec((nsub_t, TOP_K, SUB), lambda b, j: (b * nt + j, 0, 0)),
                   nat(LANES),
                   pl.BlockSpec((nsub_t, N_EXPERTS, LANES), lambda b, j: (b * nt + j, 0, 0))] + r_out,
        out_shape=[jax.ShapeDtypeStruct((B, S, D_MODEL), F32), jax.ShapeDtypeStruct((B, S, XG_W), BF16),
                   jax.ShapeDtypeStruct((B, S, LANES), F32),
                   jax.ShapeDtypeStruct((B * S // SUB, TOP_K, SUB), F32),
                   jax.ShapeDtypeStruct((B, S, LANES), F32),
                   jax.ShapeDtypeStruct((B * S // SUB, N_EXPERTS, LANES), F32)] + r_shape,
        scratch_shapes=[pltpu.VMEM((max(n_slabs, 1), tile, LANES), F32)],
        compiler_params=_params(("arbitrary", "arbitrary")),
        name="oproj",
    )(x, *o_list, *l_list, co, g1, sc, sh, ng, wo_bf, wr_t, br, *r_args)


DENSE_EXPERTS_PER_STEP = 4


def _moe_kernel(h_ref, gt_ref, x1_ref, g2_ref, wg_ref, wu_ref, wd_ref, wgs_ref, wus_ref, wds_ref,
                y_ref, acc, *, tile):
    step = pl.program_id(2)
    h = h_ref[0]

    @pl.when(step == 0)
    def _():
        a = _dot(h, wgs_ref[...].astype(BF16))
        b = _dot(h, wus_ref[...].astype(BF16))
        acc[...] = _dot((_silu(a) * b).astype(BF16), wds_ref[...].astype(BF16))

    lane = lax.broadcasted_iota(jnp.int32, (tile, LANES), 1)
    routed = None
    for j in range(DENSE_EXPERTS_PER_STEP):
        e = step * DENSE_EXPERTS_PER_STEP + j
        a = _dot(h, wg_ref[j].astype(BF16))
        b = _dot(h, wu_ref[j].astype(BF16))
        gcol = jnp.sum(jnp.where(lane == e, gt_ref[0], 0.0), axis=1, keepdims=True)
        hc = _silu(a) * b * gcol
        y = _dot(hc.astype(BF16), wd_ref[j].astype(BF16))
        routed = y if routed is None else routed + y
    acc[...] += routed

    @pl.when(step == pl.num_programs(2) - 1)
    def _():
        y_ref[0] = x1_ref[0] + g2_ref[0] * acc[...]


def _moe(h2, gates, x1, g2, wg, wu, wd, wgs, wus, wds, *, tile):
    B, S, _ = h2.shape
    tm = g2.shape[1]
    mod_spec = pl.BlockSpec((1, tm, D_MODEL),
                            (lambda b, j, e: (b, j, 0)) if tm > 1 else (lambda b, j, e: (b, 0, 0)))
    nat = lambda w: pl.BlockSpec((1, tile, w), lambda b, j, e: (b, j, 0))
    const2 = lambda shape: pl.BlockSpec(shape, lambda b, j, e: (0, 0))
    return pl.pallas_call(
        functools.partial(_moe_kernel, tile=tile),
        grid=(B, S // tile, N_EXPERTS // DENSE_EXPERTS_PER_STEP),
        in_specs=[nat(D_MODEL), nat(LANES), nat(D_MODEL), mod_spec,
                  pl.BlockSpec((DENSE_EXPERTS_PER_STEP, D_MODEL, D_EXPERT), lambda b, j, e: (e, 0, 0)),
                  pl.BlockSpec((DENSE_EXPERTS_PER_STEP, D_MODEL, D_EXPERT), lambda b, j, e: (e, 0, 0)),
                  pl.BlockSpec((DENSE_EXPERTS_PER_STEP, D_EXPERT, D_MODEL), lambda b, j, e: (e, 0, 0)),
                  const2((D_MODEL, D_EXPERT)), const2((D_MODEL, D_EXPERT)), const2((D_EXPERT, D_MODEL))],
        out_specs=nat(D_MODEL),
        out_shape=jax.ShapeDtypeStruct((B, S, D_MODEL), F32),
        scratch_shapes=[pltpu.VMEM((tile, D_MODEL), F32)],
        compiler_params=_params(("arbitrary", "arbitrary", "arbitrary")),
        name="moe",
    )(h2, gates, x1, g2, wg, wu, wd, wgs, wus, wds)


FFN_TM = 1024
SUB_ROWS = SUB * TOP_K + N_EXPERTS * CHUNK
N_CHUNKS = SUB_ROWS // CHUNK
ZERO_CHUNK = N_CHUNKS
TABLE_W = 256
MXU_ROWS = 256
BLOCKS_PER_TRIP = 4


def _slot_onehot(first, pos_list, axis_iota):
    shape = axis_iota.shape
    hit = axis_iota == jnp.broadcast_to(pos_list[0] - first, shape).astype(BF16)
    for p in pos_list[1:]:
        hit = hit | (axis_iota == jnp.broadcast_to(p - first, shape).astype(BF16))
    return jnp.where(hit, jnp.ones(shape, BF16), jnp.zeros(shape, BF16))


def _chunk_copy(src, src_chunk, dst, dst_chunk, sem):
    return pltpu.make_async_copy(src.at[src_chunk], dst.at[dst_chunk], sem)


def _for_each(n, body):
    def two(t, carry):
        body(2 * t)
        body(2 * t + 1)
        return carry

    lax.fori_loop(0, n // 2, two, 0)

    @pl.when(n % 2 == 1)
    def _():
        body(n - 1)


def _dispatch_kernel(nblk_ref, nreal_ref, ncopy_ref, ntile_ref, src_ref, dst_ref, posr_ref, xg_ref, xs_hbm,
                     xs_scr, zero_scr, sem, tail_sem, *, nsub, n_tiles_max):
    s = pl.program_id(0)

    tile_chunks = FFN_TM // CHUNK
    blk_chunks = MXU_ROWS // CHUNK
    zero_scr[...] = jnp.zeros((tile_chunks, CHUNK, XG_W), BF16)
    tail_blocks = [ntile_ref[0] + s + r * nsub for r in range(-(-(n_tiles_max) // nsub))]

    def tail_copy(t):
        return pltpu.make_async_copy(
            zero_scr, xs_hbm.at[pl.ds(pl.multiple_of(t * tile_chunks, tile_chunks), tile_chunks)], tail_sem)

    for t in tail_blocks:
        @pl.when(t < n_tiles_max)
        def _():
            tail_copy(t).start()

    xg = xg_ref[...]
    pos = [posr_ref[0, k:k + 1, :] for k in range(TOP_K)]
    rows = lax.broadcasted_iota(jnp.int32, (MXU_ROWS, SUB), 0).astype(BF16)
    xs_scr[ZERO_CHUNK] = jnp.zeros((CHUNK, XG_W), BF16)

    def sort_blocks(i, carry):
        for j in range(BLOCKS_PER_TRIP):
            blk = i * BLOCKS_PER_TRIP + j
            onehot = _slot_onehot((blk * MXU_ROWS).astype(F32), pos, rows)
            sorted_rows = _dot(onehot, xg).astype(BF16)
            xs_scr[pl.ds(pl.multiple_of(blk * blk_chunks, blk_chunks), blk_chunks)] = (
                sorted_rows.reshape(blk_chunks, CHUNK, XG_W))
        k0 = i * trip_chunks
        _for_each(jnp.clip(n_real - k0, 0, trip_chunks), lambda t: send(k0 + t))
        return carry

    n_real = nreal_ref[s]
    n = ncopy_ref[s]
    trip_chunks = BLOCKS_PER_TRIP * blk_chunks

    def send(i):
        _chunk_copy(xs_scr, src_ref[0, 0, i], xs_hbm, dst_ref[0, 0, i], sem).start()

    lax.fori_loop(0, nblk_ref[s], sort_blocks, 0)
    _for_each(n - n_real, lambda t: send(n_real + t))
    _for_each(n, lambda i: _chunk_copy(xs_scr, 0, xs_hbm, 0, sem).wait())
    for t in tail_blocks:
        @pl.when(t < n_tiles_max)
        def _():
            tail_copy(t).wait()


def _dispatch(nblk, n_real, ncopy, n_tiles, src, dst, posr, xg, n_rows):
    nsub = posr.shape[0]
    tab = pl.BlockSpec((1, 1, TABLE_W), lambda s, *_: (s, 0, 0), memory_space=pltpu.SMEM)
    return pl.pallas_call(
        functools.partial(_dispatch_kernel, nsub=nsub, n_tiles_max=n_rows // FFN_TM),
        grid_spec=pltpu.PrefetchScalarGridSpec(
            num_scalar_prefetch=4, grid=(nsub,),
            in_specs=[tab, tab, pl.BlockSpec((1, TOP_K, SUB), lambda s, *_: (s, 0, 0)),
                      pl.BlockSpec((SUB, XG_W), lambda s, *_: (s, 0))],
            out_specs=pl.BlockSpec(memory_space=pl.ANY),
            scratch_shapes=[pltpu.VMEM((N_CHUNKS + 1, CHUNK, XG_W), BF16),
                            pltpu.VMEM((FFN_TM // CHUNK, CHUNK, XG_W), BF16),
                            pltpu.SemaphoreType.DMA(()), pltpu.SemaphoreType.DMA(())]),
        out_shape=jax.ShapeDtypeStruct((n_rows // CHUNK, CHUNK, XG_W), BF16),
        compiler_params=_params(("arbitrary",)),
        name="moe_dispatch",
    )(nblk, n_real, ncopy, n_tiles, src, dst, posr, xg).reshape(n_rows, XG_W)


def _ffn_kernel(te_ref, nt_ref, xs_ref, wg_ref, wu_ref, wd_ref,
                q_ref, kc_ref, vc_ref, nk_ref, nv_ref, b1_ref, b2_ref,
                ys_ref, ko_ref, vo_ref, o_ref, l_ref, wg_s, wu_s, wd_s, *, steps_per_batch):
    i = pl.program_id(0)
    n_heads = HEADS_PER_GROUP // steps_per_batch
    first_head = (i % steps_per_batch) * n_heads

    def ride():
        _attn_sample_batch(i // steps_per_batch, 0, q_ref, kc_ref, vc_ref, nk_ref, nv_ref, b1_ref, b2_ref,
                           ko_ref, vo_ref, o_ref, l_ref, heads=[first_head + j for j in range(n_heads)])

    @pl.when(i >= nt_ref[0])
    def _():
        ride()
        ys_ref[...] = jnp.zeros((FFN_TM, D_MODEL), BF16)

    @pl.when(i < nt_ref[0])
    def _():
        e = te_ref[i]

        @pl.when((i == 0) | (e != te_ref[jnp.maximum(i - 1, 0)]))
        def _():
            wg_s[...] = wg_ref[0].astype(BF16)
            wu_s[...] = wu_ref[0].astype(BF16)
            wd_s[...] = wd_ref[0].astype(BF16)

        ride()
        x = xs_ref[:, 0:D_MODEL]
        g = xs_ref[:, D_MODEL:XG_W].astype(F32)
        lane = lax.broadcasted_iota(jnp.int32, (FFN_TM, LANES), 1)
        gcol = jnp.sum(jnp.where((lane == e) | (lane == e + N_EXPERTS), g, 0.0), axis=1, keepdims=True)
        a = _dot(x, wg_s[...])
        b = _dot(x, wu_s[...])
        hc = _silu(a) * b * gcol
        ys_ref[...] = _dot(hc.astype(BF16), wd_s[...]).astype(BF16)


def _ffn(tile_expert, n_tiles, xs, wg, wu, wd, ride):
    q, kc, vc, nk, nv, b1, b2 = ride
    nb, _, _, wb = kc.shape
    n_rows = xs.shape[0]
    n_steps = n_rows // FFN_TM
    assert n_steps % nb == 0 and HEADS_PER_GROUP % (n_steps // nb) == 0
    per = n_steps // nb
    row = lambda i, te, nt: (jnp.minimum(i, nt[0] - 1), 0)
    wspec = lambda shape: pl.BlockSpec((1,) + shape, lambda i, te, nt: (te[i], 0, 0))
    rblk = lambda *tail: pl.BlockSpec((1, HEADS_PER_GROUP) + tail, lambda i, te, nt: (i // per, 0, 0, 0))
    const = lambda shape: pl.BlockSpec(shape, lambda i, te, nt: (0, 0, 0))
    return pl.pallas_call(
        functools.partial(_ffn_kernel, steps_per_batch=per),
        grid_spec=pltpu.PrefetchScalarGridSpec(
            num_scalar_prefetch=2, grid=(n_steps,),
            in_specs=[pl.BlockSpec((FFN_TM, XG_W), row), wspec((D_MODEL, D_EXPERT)),
                      wspec((D_MODEL, D_EXPERT)), wspec((D_EXPERT, D_MODEL)),
                      rblk(8, HEAD_DIM), rblk(HEAD_DIM, wb), rblk(HEAD_DIM, wb), const(nk.shape), const(nv.shape),
                      const(b1.shape), const(b2.shape)],
            out_specs=[pl.BlockSpec((FFN_TM, D_MODEL), lambda i, te, nt: (i, 0)),
                       rblk(HEAD_DIM, wb), rblk(HEAD_DIM, wb), rblk(8, HEAD_DIM), rblk(8, LANES)],
            scratch_shapes=[pltpu.VMEM((D_MODEL, D_EXPERT), BF16), pltpu.VMEM((D_MODEL, D_EXPERT), BF16),
                            pltpu.VMEM((D_EXPERT, D_MODEL), BF16)]),
        out_shape=[jax.ShapeDtypeStruct((n_rows, D_MODEL), BF16),
                   jax.ShapeDtypeStruct(kc.shape, F32), jax.ShapeDtypeStruct(kc.shape, F32),
                   jax.ShapeDtypeStruct((nb, HEADS_PER_GROUP, 8, HEAD_DIM), F32),
                   jax.ShapeDtypeStruct((nb, HEADS_PER_GROUP, 8, LANES), F32)],
        compiler_params=_params(("arbitrary",)),
        name="moe_ffn",
    )(tile_expert, n_tiles, xs, wg, wu, wd, q, kc, vc, nk, nv, b1, b2)


def _combine_kernel(nblk_ref, src_ref, post_ref, xg_ref, x1_ref, g2_ref, wgs_ref, wus_ref,
                    wds_ref, ys_hbm, y_ref, ys_scr, wgs_s, wus_s, wds_s, sem):
    s = pl.program_id(0)
    blk_chunks = MXU_ROWS // CHUNK
    trip_chunks = BLOCKS_PER_TRIP * blk_chunks

    def fetch_trip(t, carry):
        k0 = t * trip_chunks
        _for_each(trip_chunks,
                  lambda u: _chunk_copy(ys_hbm, src_ref[0, 0, k0 + u], ys_scr, k0 + u, sem.at[t]).start())
        return carry

    lax.fori_loop(0, nblk_ref[s], fetch_trip, 0)

    @pl.when(s == 0)
    def _():
        wgs_s[...] = wgs_ref[...].astype(BF16)
        wus_s[...] = wus_ref[...].astype(BF16)
        wds_s[...] = wds_ref[...].astype(BF16)

    h = xg_ref[:, 0:D_MODEL]
    a = _dot(h, wgs_s[...])
    b = _dot(h, wus_s[...])
    shared = _dot((_silu(a) * b).astype(BF16), wds_s[...])

    post = post_ref[...]
    pos = [jnp.broadcast_to(post[:, k:k + 1], (SUB, MXU_ROWS)) for k in range(TOP_K)]
    lanes = lax.broadcasted_iota(jnp.int32, (SUB, MXU_ROWS), 1).astype(BF16)

    def gather_blocks(i, acc):
        _for_each(trip_chunks, lambda t: _chunk_copy(ys_hbm, 0, ys_scr, 0, sem.at[i]).wait())
        for j in range(BLOCKS_PER_TRIP):
            blk = i * BLOCKS_PER_TRIP + j
            onehot = _slot_onehot((blk * MXU_ROWS).astype(F32), pos, lanes)
            rows = ys_scr[pl.ds(pl.multiple_of(blk * blk_chunks, blk_chunks), blk_chunks)]
            acc = acc + _dot(onehot, rows.reshape(MXU_ROWS, D_MODEL))
        return acc

    routed = lax.fori_loop(0, nblk_ref[s], gather_blocks, jnp.zeros((SUB, D_MODEL), F32))
    y_ref[...] = x1_ref[...] + g2_ref[0] * (routed + shared)


def _combine(nblk, src, post, xg, x1, g2, wgs, wus, wds, ys):
    n_tok = x1.shape[0]
    nsub = n_tok // SUB
    per_b = nsub // g2.shape[0]
    tab = pl.BlockSpec((1, 1, TABLE_W), lambda s, *_: (s, 0, 0), memory_space=pltpu.SMEM)
    const2 = lambda shape: pl.BlockSpec(shape, lambda s, *_: (0, 0))
    return pl.pallas_call(
        _combine_kernel,
        grid_spec=pltpu.PrefetchScalarGridSpec(
            num_scalar_prefetch=1, grid=(nsub,),
            in_specs=[tab, pl.BlockSpec((SUB, LANES), lambda s, *_: (s, 0)),
                      pl.BlockSpec((SUB, XG_W), lambda s, *_: (s, 0)),
                      pl.BlockSpec((SUB, D_MODEL), lambda s, *_: (s, 0)),
                      pl.BlockSpec((1, 1, D_MODEL), lambda s, *_: (s // per_b, 0, 0)),
                      const2((D_MODEL, D_EXPERT)), const2((D_MODEL, D_EXPERT)), const2((D_EXPERT, D_MODEL)),
                      pl.BlockSpec(memory_space=pl.ANY)],
            out_specs=pl.BlockSpec((SUB, D_MODEL), lambda s, *_: (s, 0)),
            scratch_shapes=[pltpu.VMEM((N_CHUNKS, CHUNK, D_MODEL), BF16), pltpu.VMEM((D_MODEL, D_EXPERT), BF16),
                            pltpu.VMEM((D_MODEL, D_EXPERT), BF16), pltpu.VMEM((D_EXPERT, D_MODEL), BF16),
                            pltpu.SemaphoreType.DMA((SUB_ROWS // (BLOCKS_PER_TRIP * MXU_ROWS),))]),
        out_shape=jax.ShapeDtypeStruct((n_tok, D_MODEL), F32),
        compiler_params=_params(("arbitrary",)),
        name="moe_combine",
    )(nblk, src, post, xg, x1, g2, wgs, wus, wds, ys.reshape(-1, CHUNK, D_MODEL))


def _route_tables(cnt):
    nsub = cnt.shape[0]
    i32 = jnp.int32
    seg_end = jnp.cumsum(cnt, axis=1)
    seg_start = seg_end - cnt
    tot = jnp.sum(cnt, axis=0)
    tot_al = (tot + FFN_TM - 1) // FFN_TM * FFN_TM
    reg_end = jnp.cumsum(tot_al)
    reg_start = reg_end - tot_al
    base = reg_start[None, :] + jnp.cumsum(cnt, axis=0) - cnt
    n_real = seg_end[:, -1] // CHUNK
    k = jnp.arange(TABLE_W, dtype=i32)
    owner = jnp.sum((seg_end[:, None, :] // CHUNK) <= k[None, :, None], axis=2)
    owner = jnp.minimum(owner, N_EXPERTS - 1)
    delta = (base - seg_start) // CHUNK
    e_ids = jnp.arange(N_EXPERTS, dtype=i32)
    real_dst = jnp.sum(jnp.where(owner[:, :, None] == e_ids, delta[:, None, :], 0), axis=2) + k[None, :]
    fill_n = (tot_al - tot) // CHUNK
    fill_dst0 = (reg_start + tot) // CHUNK
    src = jnp.broadcast_to(k[None, :], (nsub, TABLE_W))
    dst = real_dst
    ncopy = n_real
    for r in range(-(-N_EXPERTS // nsub)):
        e_of = jnp.arange(nsub, dtype=i32) + r * nsub
        ok = e_of < N_EXPERTS
        e_cl = jnp.minimum(e_of, N_EXPERTS - 1)
        fn = jnp.where(ok, fill_n[e_cl], 0)
        j = k[None, :] - ncopy[:, None]
        is_fill = (j >= 0) & (j < fn[:, None])
        src = jnp.where(is_fill, ZERO_CHUNK, src)
        dst = jnp.where(is_fill, fill_dst0[e_cl][:, None] + j, dst)
        ncopy = ncopy + fn
    trip_rows = BLOCKS_PER_TRIP * MXU_ROWS
    disp_nblk = (n_real * CHUNK + trip_rows - 1) // trip_rows
    comb_src = jnp.where(k[None, :] < n_real[:, None], real_dst, real_dst[:, 0:1])
    n_tiles = reg_end[-1] // FFN_TM
    t = jnp.arange((nsub * SUB_ROWS + N_EXPERTS * FFN_TM) // FFN_TM, dtype=i32)
    tile_expert = jnp.minimum(jnp.sum((reg_end[None, :] // FFN_TM) <= t[:, None], axis=1), N_EXPERTS - 1)
    as3 = lambda a: a.astype(i32).reshape(nsub, 1, TABLE_W)
    return (disp_nblk.astype(i32), n_real.astype(i32), ncopy.astype(i32), as3(src), as3(dst),
            as3(comb_src), tile_expert.astype(i32), n_tiles.astype(i32).reshape(1))


def _moe_sparse(xg, post, posr, cnt, x1, g2, wg, wu, wd, wgs, wus, wds, ride):
    B, S, _ = x1.shape
    n_tok = B * S
    nsub = n_tok // SUB
    (disp_nblk, n_real, disp_n, disp_src, disp_dst, comb_src, tile_expert,
     n_tiles) = _route_tables(cnt[:, :, 0].astype(jnp.int32))
    n_rows = nsub * SUB_ROWS + N_EXPERTS * FFN_TM
    xg2 = xg.reshape(n_tok, XG_W)
    xs = _dispatch(disp_nblk, n_real, disp_n, n_tiles, disp_src, disp_dst, posr, xg2, n_rows)
    ys, *ride_out = _ffn(tile_expert, n_tiles, xs, wg, wu, wd, ride)
    y = _combine(disp_nblk, comb_src, post.reshape(n_tok, LANES), xg2, x1.reshape(n_tok, D_MODEL),
                 g2, wgs, wus, wds, ys)
    return y.reshape(B, S, D_MODEL), ride_out


def _t5_bucket(dist):
    max_exact = N_BUCKETS // 2
    df = jnp.maximum(dist, 1).astype(F32)
    large = max_exact + (jnp.log(df / max_exact) / math.log(MAX_DISTANCE / max_exact)
                         * (N_BUCKETS - max_exact)).astype(jnp.int32)
    large = jnp.minimum(large, N_BUCKETS - 1)
    return jnp.where(dist < max_exact, dist, large)


def _step_bias(rel_bias, g):
    dist = jnp.arange(N_KEYS, dtype=jnp.int32) * DILATIONS[g]
    cols = rel_bias[:, g * HEADS_PER_GROUP:(g + 1) * HEADS_PER_GROUP]
    return cols[_t5_bucket(dist)].T.astype(F32)


def _prompt_bias(bias_k):
    h = bias_k.shape[0]
    n = 3 * BLK
    row = jnp.concatenate([bias_k[:, ::-1], jnp.full((h, n - N_KEYS), NEG, F32)], axis=1)
    t = jnp.tile(row, (1, BLK))[:, :BLK * (n - 1)].reshape(h, BLK, n - 1)
    return t[:, :, :2 * BLK]


def _sample_bias(bias_k, wb, d):
    h = bias_k.shape[0]
    rev = bias_k[:, :0:-1]
    rows = []
    for t in range(4):
        if d == 1:
            rows.append(jnp.concatenate([jnp.full((h, t), NEG, F32), rev[:, :wb - t]], axis=1))
        else:
            r = jnp.arange(d)[None, None, :]
            rows.append(jnp.where(r == t, rev[:, :, None], NEG).reshape(h, wb))
    b1 = jnp.concatenate([jnp.stack(rows, axis=1), jnp.zeros((h, 4, wb), F32)], axis=1)
    rows = []
    for t in range(4):
        cols = [bias_k[:, (t - tn) // d] if (t >= tn and (t - tn) % d == 0) else jnp.full((h,), NEG, F32)
                for tn in range(4)]
        rows.append(jnp.concatenate([jnp.full((h, LANES - 4), NEG, F32), jnp.stack(cols, axis=1)], axis=1))
    b2 = jnp.concatenate([jnp.stack(rows, axis=1), jnp.zeros((h, 4, LANES), F32)], axis=1)
    return b1, b2


TILE = 512
RIDE_ON_OPROJ, RIDE_ON_PROJ, RIDE_ON_FFN = 0, 1, 2


def _sample_front(x, mods, hist, caches, rel_bias, weights):
    norm1_g, _, w_in_bf, qg, kg, conv_w, bd = weights[:7]
    sh1, sc1 = mods[:2]
    S = x.shape[1]
    nb = S // 4
    qf, kt, vt, u, co = _proj(x, sc1, sh1, norm1_g, w_in_bf, qg, kg, conv_w, bd, hist,
                              tile=TILE, dils=(1, 1, 1), sample=True)
    rides = []
    for g in range(N_GROUPS):
        d, wb = DILATIONS[g], WINDOWS[g]
        kc, vc = caches[g]
        b1, b2 = _sample_bias(_step_bias(rel_bias, g), wb, d)
        q = qf[0, :, g * GROUP_W:(g + 1) * GROUP_W].reshape(nb, 4, HEADS_PER_GROUP, HEAD_DIM)
        q = jnp.pad(q.transpose(0, 2, 1, 3), ((0, 0), (0, 0), (0, 4), (0, 0))).astype(BF16)
        new_t = lambda a: a[g * GROUP_W:(g + 1) * GROUP_W].reshape(HEADS_PER_GROUP, HEAD_DIM, S)
        rides.append((q, kc, vc, new_t(kt), new_t(vt), b1, b2))
    return dict(x=x, mods=mods, u=u, co=co), rides


def _sample_back(front, ride_outs, weights):
    (_, norm2_g, _, _, _, _, _, w_o_bf, w_r_t, b_r, w_gate_e, w_up_e, w_down_e,
     w_gate_s, w_up_s, w_down_s) = weights
    x, u, co = front["x"], front["u"], front["co"]
    _, _, g1, sh2, sc2, g2 = front["mods"]
    S = x.shape[1]
    nb = S // 4
    states, o_list, l_list = [], [], []
    for g in range(N_GROUPS):
        ko, vo, o, lse = ride_outs[g]
        states += [ko.transpose(0, 3, 1, 2)[None], vo.transpose(0, 3, 1, 2)[None]]
        o_list.append(o[:, :, :4].transpose(0, 2, 1, 3).reshape(1, 1, S, GROUP_W))
        lse = jnp.broadcast_to(lse[:, :, :4, :1], (nb, HEADS_PER_GROUP, 4, HEAD_DIM))
        l_list.append(lse.transpose(0, 2, 1, 3).reshape(1, 1, S, GROUP_W))
    states.append(u.reshape(nb, 4, CONV_CH)[:, 2:][None])
    x1, xg, gates, _, _, _ = _oproj(x, o_list, l_list, co, g1, sc2, sh2, norm2_g, w_o_bf, w_r_t, b_r,
                                    tile=TILE, dils=(1, 1, 1))
    y = _moe(xg, gates, x1, g2, w_gate_e, w_up_e, w_down_e, w_gate_s, w_up_s, w_down_s, tile=TILE)
    return y, states


def _prompt_layer(x, mods, rel_bias, weights, rides):
    (norm1_g, norm2_g, w_in_bf, qg, kg, conv_w, bd, w_o_bf, w_r_t, b_r, w_gate_e, w_up_e, w_down_e,
     w_gate_s, w_up_s, w_down_s) = weights
    sh1, sc1, g1, sh2, sc2, g2 = mods
    B, S, _ = x.shape
    ride_outs = [None] * N_GROUPS
    res = _proj(x, sc1, sh1, norm1_g, w_in_bf, qg, kg, conv_w, bd, None, tile=TILE, dils=DILATIONS, sample=False,
                ride=rides[RIDE_ON_PROJ])
    qkv, tails, (u, co), ride_outs[RIDE_ON_PROJ] = res[:9], res[9:15], res[15:17], res[17:]
    states, o_list, l_list = [], [], []
    for g in range(N_GROUPS):
        d, w = DILATIONS[g], WINDOWS[g]
        q, k, v = (a.reshape(B * d, S // d, GROUP_W) for a in qkv[3 * g:3 * g + 3])
        o, lse = _attn_prompt(q, k, v, _prompt_bias(_step_bias(rel_bias, g)))
        o_list.append(o.reshape(B, d, S // d, GROUP_W))
        l_list.append(lse.reshape(B, d, S // d, GROUP_W))
        keep = lambda a: a.reshape(B, HEADS_PER_GROUP, HEAD_DIM, w).transpose(0, 3, 1, 2)[None]
        states += [keep(tails[2 * g]), keep(tails[2 * g + 1])]
    states.append(u[:, 6:][None])
    x1, xg, _, posr, post, cnt, *ride_outs[RIDE_ON_OPROJ] = _oproj(
        x, o_list, l_list, co, g1, sc2, sh2, norm2_g, w_o_bf, w_r_t, b_r, tile=TILE, dils=DILATIONS,
        ride=rides[RIDE_ON_OPROJ])
    y, ride_outs[RIDE_ON_FFN] = _moe_sparse(xg, post, posr, cnt, x1, g2, w_gate_e, w_up_e, w_down_e, w_gate_s,
                                            w_up_s, w_down_s, rides[RIDE_ON_FFN])
    return y, states, ride_outs


def kernel(x_prompt, x_sample, c_prompt, c_sample, cache_k_w128, cache_v_w128, cache_k_w512, cache_v_w512, cache_k_w2048, cache_v_w2048, state_conv, rel_bias, norm1_g, norm2_g, w_ada, b_ada, w_in, q_norm_g, k_norm_g, conv_w, w_o, w_router, b_router, w_gate_e, w_up_e, w_down_e, w_gate_s, w_up_s, w_down_s):
    B = x_prompt.shape[0]
    DB, T = x_sample.shape[:2]
    n_c = B + DB
    c_all = jnp.pad(jnp.concatenate([c_prompt, c_sample], axis=0), ((0, (-n_c) % 8), (0, 0)))
    mod = _ada(c_all, w_ada[0], b_ada)
    chunks = [mod[:, i * D_MODEL:(i + 1) * D_MODEL] for i in range(6)]
    mods_p = [c[:B].reshape(B, 1, D_MODEL) for c in chunks]
    mods_s = [jnp.repeat(c[B:n_c], T, axis=0).reshape(1, DB * T, D_MODEL) for c in chunks]

    eye = jnp.arange(GROUP_W) // HEAD_DIM
    bd = jnp.where(eye[:, None] == eye[None, :], 1.0 / HEAD_DIM, 0.0).astype(BF16)
    weights = (norm1_g, norm2_g, w_in[0].astype(BF16),
               jnp.tile(q_norm_g, (1, HEADS_PER_GROUP)), jnp.tile(k_norm_g, (1, HEADS_PER_GROUP)),
               conv_w[0], bd, w_o[0].astype(BF16), w_router[0].T, b_router.reshape(N_EXPERTS, 1),
               w_gate_e[0], w_up_e[0], w_down_e[0], w_gate_s[0], w_up_s[0], w_down_s[0])

    s0, s1 = state_conv[0, :, 0], state_conv[0, :, 1]
    zero = jnp.zeros_like(s0)
    hist_a = jnp.stack([s0, s1, zero, zero], axis=1).reshape(DB * T, CONV_CH)
    hist_b = jnp.stack([s1, zero, zero, zero], axis=1).reshape(DB * T, CONV_CH)
    caches = [(ck[0].transpose(0, 2, 3, 1), cv[0].transpose(0, 2, 3, 1))
              for ck, cv in ((cache_k_w128, cache_v_w128), (cache_k_w512, cache_v_w512),
                             (cache_k_w2048, cache_v_w2048))]
    front, ride = _sample_front(x_sample.reshape(1, DB * T, D_MODEL), mods_s, (hist_a, hist_b), caches,
                                rel_bias, weights)
    yp, st_p, ride_out = _prompt_layer(x_prompt, mods_p, rel_bias, weights, ride)
    ys, st_s = _sample_back(front, ride_out, weights)
    return (yp, ys.reshape(DB, T, D_MODEL), *st_p, *st_s)
```

```python
import functools
import math

import jax
import jax.numpy as jnp
from jax import lax
from jax.experimental import pallas as pl
from jax.experimental.pallas import tpu as pltpu

D_MODEL = 1024
HEAD_DIM = 64
HEADS_PER_GROUP = 4
GROUP_W = HEADS_PER_GROUP * HEAD_DIM
WINDOWS = (128, 512, 2048)
DILATIONS = (1, 4, 16)
N_GROUPS = 3
ATTN_W = N_GROUPS * GROUP_W
CONV_CH = 256
N_PROJ = 3 * ATTN_W + 3 * CONV_CH
N_STEPS = 128
N_KEYS = N_STEPS + 1
BLK = 128
N_BUCKETS = 32
MAX_DISTANCE = WINDOWS[-1]
N_EXPERTS = 64
TOP_K = 8
N_ROUTE_GROUPS = 8
GROUP_SIZE = N_EXPERTS // N_ROUTE_GROUPS
TOPK_GROUPS = 4
D_EXPERT = 256
ROUTED_SCALE = 2.5
RMS_EPS = 1e-6
NEG = -1e30
LANES = 128
SUB = 256
CHUNK = 16
XG_W = D_MODEL + LANES
VMEM_LIMIT = 56 * 1024 * 1024

F32 = jnp.float32
BF16 = jnp.bfloat16


def _dot(a, b):
    return jnp.dot(a, b, preferred_element_type=F32)


def _dot_nt(a, b):
    return lax.dot_general(a, b, (((1,), (1,)), ((), ())), preferred_element_type=F32)


def _silu(a):
    return a / (1.0 + jnp.exp(-a))


def _params(sem):
    return pltpu.CompilerParams(dimension_semantics=sem, vmem_limit_bytes=VMEM_LIMIT)


def _ada_kernel(c_ref, w_ref, b_ref, o_ref):
    s = _silu(c_ref[...])
    o_ref[...] = _dot(s.astype(BF16), w_ref[...].astype(BF16)) + b_ref[...]


def _ada(c, w_ada, b_ada):
    n = c.shape[0]
    nc = w_ada.shape[1] // D_MODEL
    return pl.pallas_call(
        _ada_kernel,
        grid=(nc,),
        in_specs=[pl.BlockSpec((n, D_MODEL), lambda j: (0, 0)),
                  pl.BlockSpec((D_MODEL, D_MODEL), lambda j: (0, j)),
                  pl.BlockSpec((1, D_MODEL), lambda j: (0, j))],
        out_specs=pl.BlockSpec((n, D_MODEL), lambda j: (0, j)),
        out_shape=jax.ShapeDtypeStruct((n, w_ada.shape[1]), F32),
        compiler_params=_params(("arbitrary",)),
        name="ada",
    )(c, w_ada, b_ada)


def _proj_kernel(*refs, tile, dils, sample, nt):
    if sample:
        (x_ref, sc_ref, sh_ref, ng_ref, w_ref, qg_ref, kg_ref, cw_ref, bd_ref, ha_ref, hb_ref,
         qf_ref, kt_ref, vt_ref, u_ref, co_ref, u_scr) = refs
    else:
        (x_ref, sc_ref, sh_ref, ng_ref, w_ref, qg_ref, kg_ref, cw_ref, bd_ref) = refs[:9]
        ride_in, refs = refs[9:9 + N_RIDE_IN], refs[9 + N_RIDE_IN:]
        (q0, k0, v0, q1, k1, v1, q2, k2, v2, kt0, vt0, kt1, vt1, kt2, vt2, u_ref, co_ref) = refs[:17]
        ride_out, (slab, u_scr) = refs[17:17 + N_RIDE_OUT], refs[17 + N_RIDE_OUT:]
        qkv_out = ((q0, k0, v0), (q1, k1, v1), (q2, k2, v2))
        tails = ((kt0, vt0), (kt1, vt1), (kt2, vt2))
        j = pl.program_id(1)
        _ride_block(pl.program_id(0) * nt + j, ride_in + ride_out)

    x = x_ref[0]
    ms = jnp.mean(x * x, axis=-1, keepdims=True)
    h = x * lax.rsqrt(ms + RMS_EPS) * ng_ref[...]
    h = h * (1.0 + sc_ref[0]) + sh_ref[0]
    proj = _dot(h.astype(BF16), w_ref[...])

    bd = bd_ref[...]

    def headnorm(z, g):
        msq = _dot((z * z).astype(BF16), bd)
        return z * lax.rsqrt(msq + RMS_EPS) * g

    slab_i = 0
    for g in range(N_GROUPS):
        c0 = g * GROUP_W
        qn = headnorm(proj[:, c0:c0 + GROUP_W], qg_ref[...]) * (HEAD_DIM ** -0.5)
        kn = headnorm(proj[:, ATTN_W + c0:ATTN_W + c0 + GROUP_W], kg_ref[...])
        vv = proj[:, 2 * ATTN_W + c0:2 * ATTN_W + c0 + GROUP_W]
        if sample:
            qf_ref[0, :, c0:c0 + GROUP_W] = qn
            kt_ref[c0:c0 + GROUP_W, :] = kn.T
            vt_ref[c0:c0 + GROUP_W, :] = vv.T
            continue
        keep = min(WINDOWS[g], tile)

        @pl.when(j >= nt - max(WINDOWS[g] // tile, 1))
        def _(kn=kn, vv=vv, g=g, keep=keep):
            tails[g][0][0] = kn[tile - keep:, :].T
            tails[g][1][0] = vv[tile - keep:, :].T

        d = dils[g]
        for val, out in zip((qn, kn, vv), qkv_out[g]):
            if d == 1:
                out[0, 0] = val.astype(BF16)
                continue
            n = tile // d
            for half in range(GROUP_W // LANES):
                slab[slab_i] = val[:, half * LANES:(half + 1) * LANES]
                for r in range(d):
                    out[0, r, :, half * LANES:(half + 1) * LANES] = (
                        slab[slab_i, pl.ds(r, n, stride=d), :].astype(BF16))
                slab_i += 1

    base = 3 * ATTN_W
    u = proj[:, base + 2 * CONV_CH:base + 3 * CONV_CH] * proj[:, base:base + CONV_CH]
    gate_b = proj[:, base + CONV_CH:base + 2 * CONV_CH]
    if sample:
        u_scr[0:8, :] = jnp.zeros((8, CONV_CH), F32)
    else:
        @pl.when(j == 0)
        def _():
            u_scr[0:8, :] = jnp.zeros((8, CONV_CH), F32)

        @pl.when(j > 0)
        def _():
            u_scr[0:8, :] = u_scr[tile:tile + 8, :]

    u_scr[8:tile + 8, :] = u
    um1 = u_scr[7:tile + 7, :]
    um2 = u_scr[6:tile + 6, :]
    if sample:
        t = lax.broadcasted_iota(jnp.int32, (tile, CONV_CH), 0) % 4
        um1 = jnp.where(t >= 1, um1, 0.0) + hb_ref[...]
        um2 = jnp.where(t >= 2, um2, 0.0) + ha_ref[...]
    cw = cw_ref[...]
    conv = cw[0:1] * um2 + cw[1:2] * um1 + cw[2:3] * u
    u_ref[0] = u if sample else u[tile - 8:, :]
    co_ref[0] = (gate_b * conv).astype(BF16)


def _proj(x, sc, sh, ng, w_bf, qg, kg, cw, bd, hist, *, tile, dils, sample, ride=None):
    B, S, _ = x.shape
    nt = S // tile
    tm = sc.shape[1]
    mod_spec = pl.BlockSpec((1, tm, D_MODEL), (lambda b, j: (b, j, 0)) if tm > 1 else (lambda b, j: (b, 0, 0)))
    const2 = lambda shape: pl.BlockSpec(shape, lambda b, j: (0, 0))
    in_specs = [pl.BlockSpec((1, tile, D_MODEL), lambda b, j: (b, j, 0)), mod_spec, mod_spec,
                const2((1, D_MODEL)), const2((D_MODEL, N_PROJ)), const2((1, GROUP_W)),
                const2((1, GROUP_W)), const2((3, CONV_CH)), const2((GROUP_W, GROUP_W))]
    args = [x, sc, sh, ng, w_bf, qg, kg, cw, bd]
    nat = lambda w: pl.BlockSpec((1, tile, w), lambda b, j: (b, j, 0))
    out_specs, out_shape = [], []
    scratch = []
    if sample:
        assert B == 1 and nt == 1
        in_specs += [pl.BlockSpec((tile, CONV_CH), lambda b, j: (j, 0))] * 2
        args += list(hist)
        out_specs += [nat(ATTN_W)] + [pl.BlockSpec((ATTN_W, tile), lambda b, j: (0, j))] * 2 + [nat(CONV_CH)]
        out_shape += [jax.ShapeDtypeStruct((B, S, ATTN_W), F32)]
        out_shape += [jax.ShapeDtypeStruct((ATTN_W, S), F32)] * 2
        out_shape += [jax.ShapeDtypeStruct((B, S, CONV_CH), F32)]
    else:
        for d in dils:
            for _ in range(3):
                out_specs.append(pl.BlockSpec((1, d, tile // d, GROUP_W), lambda b, j: (b, 0, j, 0)))
                out_shape.append(jax.ShapeDtypeStruct((B, d, S // d, GROUP_W), BF16))
        for w in WINDOWS:
            keep, first = min(w, tile), nt - max(w // tile, 1)
            for _ in range(2):
                out_specs.append(pl.BlockSpec((1, GROUP_W, keep),
                                              lambda b, j, first=first: (b, 0, jnp.maximum(j - first, 0))))
                out_shape.append(jax.ShapeDtypeStruct((B, GROUP_W, w), F32))
        out_specs.append(pl.BlockSpec((1, 8, CONV_CH), lambda b, j: (b, 0, 0)))
        out_shape.append(jax.ShapeDtypeStruct((B, 8, CONV_CH), F32))
        n_slabs = sum(3 * (GROUP_W // LANES) for d in dils if d > 1)
        scratch.append(pltpu.VMEM((n_slabs, tile, LANES), F32))
    out_specs.append(nat(CONV_CH))
    out_shape.append(jax.ShapeDtypeStruct((B, S, CONV_CH), BF16))
    scratch.append(pltpu.VMEM((tile + 8, CONV_CH), F32))
    if not sample:
        nb = ride[1].shape[0]
        r_in, r_out, r_shape = _ride_specs(ride, nb // (B * nt), lambda b, j: b * nt + j)
        in_specs, args = in_specs + r_in, args + list(ride)
        out_specs, out_shape = out_specs + r_out, out_shape + r_shape
    return pl.pallas_call(
        functools.partial(_proj_kernel, tile=tile, dils=dils, sample=sample, nt=nt),
        grid=(B, nt),
        in_specs=in_specs,
        out_specs=out_specs,
        out_shape=out_shape,
        scratch_shapes=scratch,
        compiler_params=_params(("arbitrary", "arbitrary")),
        name="proj_sample" if sample else "proj_prompt",
    )(*args)


def _attn_kernel(q_ref, kp_ref, kc_ref, vp_ref, vc_ref, b_ref, o_ref, l_ref, s_scr, m_scr, o_scr, z_scr,
                 *, nq):
    first = pl.program_id(1) == 0
    low = lax.broadcasted_iota(jnp.int32, (BLK, LANES), 1) < HEAD_DIM
    ones = jnp.ones((BLK, LANES), BF16)
    chains = [(j, pair, sub) for j in range(nq) for pair in range(GROUP_W // LANES) for sub in range(2)]

    def operands(j, pair):
        rows = slice(j * BLK, (j + 1) * BLK)
        sl = slice(pair * LANES, (pair + 1) * LANES)
        if j == 0:
            return rows, sl, kp_ref[0, :, sl], vp_ref[0, :, sl]
        prows = slice((j - 1) * BLK, j * BLK)
        return rows, sl, kc_ref[0, prows, sl], vc_ref[0, prows, sl]

    for c, (j, pair, sub) in enumerate(chains):
        rows, sl, kp, _ = operands(j, pair)
        q = q_ref[0, rows, sl]
        qm = jnp.where(low if sub == 0 else ~low, q, jnp.zeros_like(q))
        hh = 2 * pair + sub
        sp = _dot_nt(qm, kp) + b_ref[hh, :, 0:BLK]
        if j == 0:
            sp = jnp.where(first, NEG, sp)
        sc = _dot_nt(qm, kc_ref[0, rows, sl]) + b_ref[hh, :, BLK:2 * BLK]
        s_scr[c, :, 0:BLK] = sp
        s_scr[c, :, BLK:2 * BLK] = sc
        m_scr[c] = jnp.broadcast_to(jnp.max(jnp.maximum(sp, sc), axis=-1, keepdims=True), (BLK, LANES))

    for c, (j, pair, sub) in enumerate(chains):
        rows, sl, _, vp = operands(j, pair)
        m = m_scr[c]
        pp = jnp.exp(s_scr[c, :, 0:BLK] - m).astype(BF16)
        pc = jnp.exp(s_scr[c, :, BLK:2 * BLK] - m).astype(BF16)
        o_scr[c] = _dot(pp, vp) + _dot(pc, vc_ref[0, rows, sl])
        z_scr[c] = _dot(pp, ones) + _dot(pc, ones)

    for c in range(0, len(chains), 2):
        j, pair, _ = chains[c]
        rows = slice(j * BLK, (j + 1) * BLK)
        sl = slice(pair * LANES, (pair + 1) * LANES)
        o_sub = [o_scr[c + sub] / z_scr[c + sub] for sub in range(2)]
        l_sub = [m_scr[c + sub] + jnp.log(z_scr[c + sub]) for sub in range(2)]
        o_ref[0, rows, sl] = jnp.where(low, o_sub[0], o_sub[1])
        l_ref[0, rows, sl] = jnp.where(low, l_sub[0], l_sub[1])


def _attn_prompt(q, k, v, bias):
    Z, L, _ = q.shape
    qt = min(L, 4 * BLK)
    nq = qt // BLK
    cur = pl.BlockSpec((1, qt, GROUP_W), lambda z, i: (z, i, 0))
    prev = pl.BlockSpec((1, BLK, GROUP_W), lambda z, i: (z, jnp.maximum(i * nq - 1, 0), 0))
    return pl.pallas_call(
        functools.partial(_attn_kernel, nq=nq),
        grid=(Z, L // qt),
        in_specs=[cur, prev, cur, prev, cur,
                  pl.BlockSpec((HEADS_PER_GROUP, BLK, 2 * BLK), lambda z, i: (0, 0, 0))],
        out_specs=[cur, cur],
        out_shape=[jax.ShapeDtypeStruct((Z, L, GROUP_W), F32)] * 2,
        scratch_shapes=[pltpu.VMEM((nq * HEADS_PER_GROUP, BLK, 2 * BLK), F32)]
        + [pltpu.VMEM((nq * HEADS_PER_GROUP, BLK, LANES), F32)] * 3,
        compiler_params=_params(("arbitrary", "arbitrary")),
        name="attn_prompt",
    )(q, k, k, v, v, bias)


def _attn_sample_batch(gb, b, q_ref, kc_ref, vc_ref, nk_ref, nv_ref, b1_ref, b2_ref, ko_ref, vo_ref, o_ref,
                       l_ref, heads=range(HEADS_PER_GROUP)):
    wb = kc_ref.shape[-1]
    lane = lax.broadcasted_iota(jnp.int32, (HEAD_DIM, LANES), 1)
    new_cols = lane >= LANES - 4
    per_tile = LANES // 4
    tile_at = pl.ds(pl.multiple_of((gb // per_tile) * LANES, LANES), LANES)
    shift = (LANES - 4) - 4 * (gb % per_tile)
    for hh in heads:
        q = q_ref[b, hh]
        kt = kc_ref[b, hh]
        vt = vc_ref[b, hh]
        nkt = pltpu.roll(nk_ref[hh, :, tile_at], shift, axis=1)
        nvt = pltpu.roll(nv_ref[hh, :, tile_at], shift, axis=1)
        s1 = _dot(q, kt.astype(BF16)) + b1_ref[hh]
        s2 = _dot(q, nkt.astype(BF16)) + b2_ref[hh]
        m = jnp.maximum(jnp.max(s1, axis=-1, keepdims=True), jnp.max(s2, axis=-1, keepdims=True))
        p1 = jnp.exp(s1 - m)
        p2 = jnp.exp(s2 - m)
        z = jnp.sum(p1, axis=-1, keepdims=True) + jnp.sum(p2, axis=-1, keepdims=True)
        o = _dot_nt(p1.astype(BF16), vt.astype(BF16)) + _dot_nt(p2.astype(BF16), nvt.astype(BF16))
        o_ref[b, hh] = o / z
        l_ref[b, hh] = jnp.broadcast_to(m + jnp.log(z), (8, LANES))
        for src, new, dst in ((kt, nkt, ko_ref), (vt, nvt, vo_ref)):
            rolled = pltpu.roll(src, wb - 4, axis=1)
            if wb > LANES:
                dst[b, hh, :, 0:wb - LANES] = rolled[:, 0:wb - LANES]
            dst[b, hh, :, wb - LANES:wb] = jnp.where(new_cols, new, rolled[:, wb - LANES:wb])


N_RIDE_IN = 7
N_RIDE_OUT = 4


def _ride_specs(ride, bb, block_index):
    q, kc, vc, nk, nv, b1, b2 = ride
    nb, _, _, wb = kc.shape
    blk = lambda *tail: pl.BlockSpec((bb, HEADS_PER_GROUP) + tail, lambda *g: (block_index(*g), 0, 0, 0))
    const = lambda a: pl.BlockSpec(a.shape, lambda *g: (0,) * a.ndim)
    in_specs = [blk(8, HEAD_DIM), blk(HEAD_DIM, wb), blk(HEAD_DIM, wb), const(nk), const(nv), const(b1), const(b2)]
    out_specs = [blk(HEAD_DIM, wb), blk(HEAD_DIM, wb), blk(8, HEAD_DIM), blk(8, LANES)]
    out_shape = [jax.ShapeDtypeStruct(kc.shape, F32), jax.ShapeDtypeStruct(kc.shape, F32),
                 jax.ShapeDtypeStruct((nb, HEADS_PER_GROUP, 8, HEAD_DIM), F32),
                 jax.ShapeDtypeStruct((nb, HEADS_PER_GROUP, 8, LANES), F32)]
    return in_specs, out_specs, out_shape


def _ride_block(step, ride_refs):
    bb = ride_refs[0].shape[0]
    for b in range(bb):
        _attn_sample_batch(step * bb + b, b, *ride_refs)


def _first_max(v, ids, sentinel):
    m = jnp.max(v, axis=0, keepdims=True)
    idx = jnp.min(jnp.where(v == m, ids, sentinel), axis=0, keepdims=True)
    return m, ids == idx


def _oproj_kernel(*refs, tile, dils, ride):
    (x_ref, o0, o1, o2, l0, l1, l2, co_ref, g1_ref, sc_ref, sh_ref, ng_ref, wo_ref, wr_ref, br_ref) = refs[:15]
    refs = refs[15:]
    if ride:
        ride_in, refs = refs[:N_RIDE_IN], refs[N_RIDE_IN:]
        ride_out = refs[6:6 + N_RIDE_OUT]
        _ride_block(pl.program_id(0) * pl.num_programs(1) + pl.program_id(1), ride_in + ride_out)
    x1_ref, xg_ref, gt_ref, posr_ref, post_ref, cnt_ref = refs[:6]
    slab = refs[-1]
    o_refs, l_refs = (o0, o1, o2), (l0, l1, l2)
    outs, lses = [], []
    slab_i = 0
    for g, d in enumerate(dils):
        if d == 1:
            outs.append(o_refs[g][0, 0])
            lses.append(l_refs[g][0, 0])
            continue
        n = tile // d
        for ref, dest in ((o_refs[g], outs), (l_refs[g], lses)):
            halves = []
            for half in range(GROUP_W // LANES):
                for r in range(d):
                    slab[slab_i, pl.ds(r, n, stride=d), :] = ref[0, r, :, half * LANES:(half + 1) * LANES]
                halves.append(slab[slab_i])
                slab_i += 1
            dest.append(jnp.concatenate(halves, axis=1))

    m = jnp.maximum(jnp.maximum(lses[0], lses[1]), lses[2])
    es = [jnp.exp(l - m) for l in lses]
    den = es[0] + es[1] + es[2]
    cat = [(outs[g] * (es[g] / den)).astype(BF16) for g in range(N_GROUPS)]
    cat.append(co_ref[0].astype(BF16))
    cat = jnp.concatenate(cat, axis=1)
    x1 = x_ref[0] + g1_ref[0] * _dot(cat, wo_ref[...])
    x1_ref[0] = x1

    ms = jnp.mean(x1 * x1, axis=-1, keepdims=True)
    h2 = x1 * lax.rsqrt(ms + RMS_EPS) * ng_ref[...]
    h2 = h2 * (1.0 + sc_ref[0]) + sh_ref[0]
    hh = h2.astype(BF16)
    xg_ref[0, :, 0:D_MODEL] = hh

    hl = (h2 - hh.astype(F32)).astype(BF16)
    wr = wr_ref[...]
    wh = wr.astype(BF16)
    wl = (wr - wh.astype(F32)).astype(BF16)
    logits = _dot_nt(wh, hh) + _dot_nt(wh, hl) + _dot_nt(wl, hh)
    scores = 1.0 / (1.0 + jnp.exp(-logits))
    sel = scores + br_ref[...]

    ids = lax.broadcasted_iota(jnp.int32, (GROUP_SIZE, tile), 0)
    ninf = -jnp.inf
    sel_g = [sel[g * GROUP_SIZE:(g + 1) * GROUP_SIZE] for g in range(N_ROUTE_GROUPS)]
    gscore = jnp.zeros((N_ROUTE_GROUPS, tile), F32)
    for g in range(N_ROUTE_GROUPS):
        m1, oh = _first_max(sel_g[g], ids, GROUP_SIZE)
        m2 = jnp.max(jnp.where(oh, ninf, sel_g[g]), axis=0, keepdims=True)
        gscore = jnp.where(ids == g, m1 + m2, gscore)
    gsel = jnp.zeros((N_ROUTE_GROUPS, tile), F32)
    for _ in range(TOPK_GROUPS):
        _, oh = _first_max(gscore, ids, N_ROUTE_GROUPS)
        gsel = jnp.where(oh, 1.0, gsel)
        gscore = jnp.where(oh, ninf, gscore)
    cand = [jnp.where(gsel[g:g + 1] > 0.0, sel_g[g], ninf) for g in range(N_ROUTE_GROUPS)]
    chosen = [jnp.zeros((GROUP_SIZE, tile), F32) for _ in range(N_ROUTE_GROUPS)]
    picks = []
    for _ in range(TOP_K):
        mx = cand[0]
        for g in range(1, N_ROUTE_GROUPS):
            mx = jnp.maximum(mx, cand[g])
        mx = jnp.max(mx, axis=0, keepdims=True)
        idx = jnp.where(cand[0] == mx, ids, N_EXPERTS)
        for g in range(1, N_ROUTE_GROUPS):
            idx = jnp.minimum(idx, jnp.where(cand[g] == mx, ids + g * GROUP_SIZE, N_EXPERTS))
        idx = jnp.min(idx, axis=0, keepdims=True)
        ohs = [(ids + g * GROUP_SIZE) == idx for g in range(N_ROUTE_GROUPS)]
        picks.append(ohs)
        for g in range(N_ROUTE_GROUPS):
            chosen[g] = jnp.where(ohs[g], 1.0, chosen[g])
            cand[g] = jnp.where(ohs[g], ninf, cand[g])
    wts = [jnp.where(chosen[g] > 0.0, scores[g * GROUP_SIZE:(g + 1) * GROUP_SIZE], 0.0)
           for g in range(N_ROUTE_GROUPS)]
    wsum = jnp.sum(wts[0], axis=0, keepdims=True)
    for g in range(1, N_ROUTE_GROUPS):
        wsum = wsum + jnp.sum(wts[g], axis=0, keepdims=True)
    gates_t = jnp.concatenate([w / wsum * ROUTED_SCALE for w in wts], axis=0)
    gt_ref[0] = jnp.concatenate([gates_t, jnp.zeros((LANES - N_EXPERTS, tile), F32)], axis=0).T
    g_hi = gates_t.astype(BF16).astype(F32)
    xg_ref[0, :, D_MODEL:D_MODEL + LANES] = jnp.concatenate([g_hi, gates_t - g_hi], axis=0).T.astype(BF16)

    er = lax.broadcasted_iota(jnp.int32, (N_EXPERTS, N_EXPERTS), 0)
    ec = lax.broadcasted_iota(jnp.int32, (N_EXPERTS, N_EXPERTS), 1)
    lower = jnp.where(ec < er, 1.0, 0.0).astype(BF16)
    tr = lax.broadcasted_iota(jnp.int32, (SUB, SUB), 0)
    tc = lax.broadcasted_iota(jnp.int32, (SUB, SUB), 1)
    upper = jnp.where(tr < tc, 1.0, 0.0).astype(BF16)
    for u in range(tile // SUB):
        cols = slice(u * SUB, (u + 1) * SUB)
        ch = jnp.concatenate([c[:, cols] for c in chosen], axis=0)
        cnt = jnp.sum(ch, axis=1, keepdims=True)
        cpad = jnp.broadcast_to(jnp.ceil(cnt / CHUNK) * CHUNK, (N_EXPERTS, LANES))
        cnt_ref[u] = cpad
        seg0 = _dot(lower, cpad.astype(BF16))[:, 0:1]
        slot = seg0 + _dot(ch.astype(BF16), upper)
        rows = []
        for k in range(TOP_K):
            acc = jnp.where(picks[k][0][:, cols], slot[0:GROUP_SIZE], 0.0)
            for g in range(1, N_ROUTE_GROUPS):
                acc = acc + jnp.where(picks[k][g][:, cols], slot[g * GROUP_SIZE:(g + 1) * GROUP_SIZE], 0.0)
            rows.append(jnp.sum(acc, axis=0, keepdims=True))
        posr = jnp.concatenate(rows, axis=0)
        posr_ref[u] = posr
        post_ref[0, cols, :] = jnp.concatenate([posr, jnp.zeros((LANES - TOP_K, SUB), F32)], axis=0).T


def _oproj(x, o_list, l_list, co, g1, sc, sh, ng, wo_bf, wr_t, br, *, tile, dils, ride=None):
    B, S, _ = x.shape
    nt = S // tile
    r_in, r_out, r_shape, r_args = [], [], [], []
    if ride is not None:
        r_in, r_out, r_shape = _ride_specs(ride, ride[1].shape[0] // (B * nt), lambda b, j: b * nt + j)
        r_args = list(ride)
    tm = g1.shape[1]
    mod_spec = pl.BlockSpec((1, tm, D_MODEL), (lambda b, j: (b, j, 0)) if tm > 1 else (lambda b, j: (b, 0, 0)))
    const2 = lambda shape: pl.BlockSpec(shape, lambda b, j: (0, 0))
    nat = lambda w: pl.BlockSpec((1, tile, w), lambda b, j: (b, j, 0))
    dspec = [pl.BlockSpec((1, d, tile // d, GROUP_W), lambda b, j: (b, 0, j, 0)) for d in dils]
    n_slabs = sum(2 * (GROUP_W // LANES) for d in dils if d > 1)
    nsub_t = tile // SUB
    return pl.pallas_call(
        functools.partial(_oproj_kernel, tile=tile, dils=dils, ride=ride is not None),
        grid=(B, nt),
        in_specs=[nat(D_MODEL)] + dspec + dspec + [nat(CONV_CH), mod_spec, mod_spec, mod_spec,
                  const2((1, D_MODEL)), const2((D_MODEL, D_MODEL)), const2((N_EXPERTS, D_MODEL)),
                  const2((N_EXPERTS, 1))] + r_in,
        out_specs=[nat(D_MODEL), nat(XG_W), nat(LANES),
                   pl.BlockSpec((nsub_t, TOP_K, SUB), lambda b, j: (b * nt + j, 0, 0)),
                   nat(LANES),
                   pl.BlockSpec((nsub_t, N_EXPERTS, LANES), lambda b, j: (b * nt + j, 0, 0))] + r_out,
        out_shape=[jax.ShapeDtypeStruct((B, S, D_MODEL), F32), jax.ShapeDtypeStruct((B, S, XG_W), BF16),
                   jax.ShapeDtypeStruct((B, S, LANES), F32),
                   jax.ShapeDtypeStruct((B * S // SUB, TOP_K, SUB), F32),
                   jax.ShapeDtypeStruct((B, S, LANES), F32),
                   jax.ShapeDtypeStruct((B * S // SUB, N_EXPERTS, LANES), F32)] + r_shape,
        scratch_shapes=[pltpu.VMEM((max(n_slabs, 1), tile, LANES), F32)],
        compiler_params=_params(("arbitrary", "arbitrary")),
        name="oproj",
    )(x, *o_list, *l_list, co, g1, sc, sh, ng, wo_bf, wr_t, br, *r_args)


DENSE_EXPERTS_PER_STEP = 4


def _moe_kernel(h_ref, gt_ref, x1_ref, g2_ref, wg_ref, wu_ref, wd_ref, wgs_ref, wus_ref, wds_ref,
                y_ref, acc, *, tile):
    step = pl.program_id(2)
    h = h_ref[0]

    @pl.when(step == 0)
    def _():
        a = _dot(h, wgs_ref[...].astype(BF16))
        b = _dot(h, wus_ref[...].astype(BF16))
        acc[...] = _dot((_silu(a) * b).astype(BF16), wds_ref[...].astype(BF16))

    lane = lax.broadcasted_iota(jnp.int32, (tile, LANES), 1)
    routed = None
    for j in range(DENSE_EXPERTS_PER_STEP):
        e = step * DENSE_EXPERTS_PER_STEP + j
        a = _dot(h, wg_ref[j].astype(BF16))
        b = _dot(h, wu_ref[j].astype(BF16))
        gcol = jnp.sum(jnp.where(lane == e, gt_ref[0], 0.0), axis=1, keepdims=True)
        hc = _silu(a) * b * gcol
        y = _dot(hc.astype(BF16), wd_ref[j].astype(BF16))
        routed = y if routed is None else routed + y
    acc[...] += routed

    @pl.when(step == pl.num_programs(2) - 1)
    def _():
        y_ref[0] = x1_ref[0] + g2_ref[0] * acc[...]


def _moe(h2, gates, x1, g2, wg, wu, wd, wgs, wus, wds, *, tile):
    B, S, _ = h2.shape
    tm = g2.shape[1]
    mod_spec = pl.BlockSpec((1, tm, D_MODEL),
                            (lambda b, j, e: (b, j, 0)) if tm > 1 else (lambda b, j, e: (b, 0, 0)))
    nat = lambda w: pl.BlockSpec((1, tile, w), lambda b, j, e: (b, j, 0))
    const2 = lambda shape: pl.BlockSpec(shape, lambda b, j, e: (0, 0))
    return pl.pallas_call(
        functools.partial(_moe_kernel, tile=tile),
        grid=(B, S // tile, N_EXPERTS // DENSE_EXPERTS_PER_STEP),
        in_specs=[nat(D_MODEL), nat(LANES), nat(D_MODEL), mod_spec,
                  pl.BlockSpec((DENSE_EXPERTS_PER_STEP, D_MODEL, D_EXPERT), lambda b, j, e: (e, 0, 0)),
                  pl.BlockSpec((DENSE_EXPERTS_PER_STEP, D_MODEL, D_EXPERT), lambda b, j, e: (e, 0, 0)),
                  pl.BlockSpec((DENSE_EXPERTS_PER_STEP, D_EXPERT, D_MODEL), lambda b, j, e: (e, 0, 0)),
                  const2((D_MODEL, D_EXPERT)), const2((D_MODEL, D_EXPERT)), const2((D_EXPERT, D_MODEL))],
        out_specs=nat(D_MODEL),
        out_shape=jax.ShapeDtypeStruct((B, S, D_MODEL), F32),
        scratch_shapes=[pltpu.VMEM((tile, D_MODEL), F32)],
        compiler_params=_params(("arbitrary", "arbitrary", "arbitrary")),
        name="moe",
    )(h2, gates, x1, g2, wg, wu, wd, wgs, wus, wds)


FFN_TM = 1024
SUB_ROWS = SUB * TOP_K + N_EXPERTS * CHUNK
N_CHUNKS = SUB_ROWS // CHUNK
ZERO_CHUNK = N_CHUNKS
TABLE_W = 256
MXU_ROWS = 256
BLOCKS_PER_TRIP = 4


def _slot_onehot(first, pos_list, axis_iota):
    shape = axis_iota.shape
    hit = axis_iota == jnp.broadcast_to(pos_list[0] - first, shape).astype(BF16)
    for p in pos_list[1:]:
        hit = hit | (axis_iota == jnp.broadcast_to(p - first, shape).astype(BF16))
    return jnp.where(hit, jnp.ones(shape, BF16), jnp.zeros(shape, BF16))


def _chunk_copy(src, src_chunk, dst, dst_chunk, sem):
    return pltpu.make_async_copy(src.at[src_chunk], dst.at[dst_chunk], sem)


def _for_each(n, body):
    def two(t, carry):
        body(2 * t)
        body(2 * t + 1)
        return carry

    lax.fori_loop(0, n // 2, two, 0)

    @pl.when(n % 2 == 1)
    def _():
        body(n - 1)


def _dispatch_kernel(nblk_ref, nreal_ref, ncopy_ref, ntile_ref, src_ref, dst_ref, posr_ref, xg_ref, xs_hbm,
                     xs_scr, zero_scr, sem, tail_sem, *, nsub, n_tiles_max):
    s = pl.program_id(0)

    tile_chunks = FFN_TM // CHUNK
    blk_chunks = MXU_ROWS // CHUNK
    zero_scr[...] = jnp.zeros((tile_chunks, CHUNK, XG_W), BF16)
    tail_blocks = [ntile_ref[0] + s + r * nsub for r in range(-(-(n_tiles_max) // nsub))]

    def tail_copy(t):
        return pltpu.make_async_copy(
            zero_scr, xs_hbm.at[pl.ds(pl.multiple_of(t * tile_chunks, tile_chunks), tile_chunks)], tail_sem)

    for t in tail_blocks:
        @pl.when(t < n_tiles_max)
        def _():
            tail_copy(t).start()

    xg = xg_ref[...]
    pos = [posr_ref[0, k:k + 1, :] for k in range(TOP_K)]
    rows = lax.broadcasted_iota(jnp.int32, (MXU_ROWS, SUB), 0).astype(BF16)
    xs_scr[ZERO_CHUNK] = jnp.zeros((CHUNK, XG_W), BF16)

    def sort_blocks(i, carry):
        for j in range(BLOCKS_PER_TRIP):
            blk = i * BLOCKS_PER_TRIP + j
            onehot = _slot_onehot((blk * MXU_ROWS).astype(F32), pos, rows)
            sorted_rows = _dot(onehot, xg).astype(BF16)
            xs_scr[pl.ds(pl.multiple_of(blk * blk_chunks, blk_chunks), blk_chunks)] = (
                sorted_rows.reshape(blk_chunks, CHUNK, XG_W))
        k0 = i * trip_chunks
        _for_each(jnp.clip(n_real - k0, 0, trip_chunks), lambda t: send(k0 + t))
        return carry

    n_real = nreal_ref[s]
    n = ncopy_ref[s]
    trip_chunks = BLOCKS_PER_TRIP * blk_chunks

    def send(i):
        _chunk_copy(xs_scr, src_ref[0, 0, i], xs_hbm, dst_ref[0, 0, i], sem).start()

    lax.fori_loop(0, nblk_ref[s], sort_blocks, 0)
    _for_each(n - n_real, lambda t: send(n_real + t))
    _for_each(n, lambda i: _chunk_copy(xs_scr, 0, xs_hbm, 0, sem).wait())
    for t in tail_blocks:
        @pl.when(t < n_tiles_max)
        def _():
            tail_copy(t).wait()


def _dispatch(nblk, n_real, ncopy, n_tiles, src, dst, posr, xg, n_rows):
    nsub = posr.shape[0]
    tab = pl.BlockSpec((1, 1, TABLE_W), lambda s, *_: (s, 0, 0), memory_space=pltpu.SMEM)
    return pl.pallas_call(
        functools.partial(_dispatch_kernel, nsub=nsub, n_tiles_max=n_rows // FFN_TM),
        grid_spec=pltpu.PrefetchScalarGridSpec(
            num_scalar_prefetch=4, grid=(nsub,),
            in_specs=[tab, tab, pl.BlockSpec((1, TOP_K, SUB), lambda s, *_: (s, 0, 0)),
                      pl.BlockSpec((SUB, XG_W), lambda s, *_: (s, 0))],
            out_specs=pl.BlockSpec(memory_space=pl.ANY),
            scratch_shapes=[pltpu.VMEM((N_CHUNKS + 1, CHUNK, XG_W), BF16),
                            pltpu.VMEM((FFN_TM // CHUNK, CHUNK, XG_W), BF16),
                            pltpu.SemaphoreType.DMA(()), pltpu.SemaphoreType.DMA(())]),
        out_shape=jax.ShapeDtypeStruct((n_rows // CHUNK, CHUNK, XG_W), BF16),
        compiler_params=_params(("arbitrary",)),
        name="moe_dispatch",
    )(nblk, n_real, ncopy, n_tiles, src, dst, posr, xg).reshape(n_rows, XG_W)


def _ffn_kernel(te_ref, nt_ref, xs_ref, wg_ref, wu_ref, wd_ref,
                q_ref, kc_ref, vc_ref, nk_ref, nv_ref, b1_ref, b2_ref,
                ys_ref, ko_ref, vo_ref, o_ref, l_ref, wg_s, wu_s, wd_s, *, steps_per_batch):
    i = pl.program_id(0)
    n_heads = HEADS_PER_GROUP // steps_per_batch
    first_head = (i % steps_per_batch) * n_heads

    def ride():
        _attn_sample_batch(i // steps_per_batch, 0, q_ref, kc_ref, vc_ref, nk_ref, nv_ref, b1_ref, b2_ref,
                           ko_ref, vo_ref, o_ref, l_ref, heads=[first_head + j for j in range(n_heads)])

    @pl.when(i >= nt_ref[0])
    def _():
        ride()
        ys_ref[...] = jnp.zeros((FFN_TM, D_MODEL), BF16)

    @pl.when(i < nt_ref[0])
    def _():
        e = te_ref[i]

        @pl.when((i == 0) | (e != te_ref[jnp.maximum(i - 1, 0)]))
        def _():
            wg_s[...] = wg_ref[0].astype(BF16)
            wu_s[...] = wu_ref[0].astype(BF16)
            wd_s[...] = wd_ref[0].astype(BF16)

        ride()
        x = xs_ref[:, 0:D_MODEL]
        g = xs_ref[:, D_MODEL:XG_W].astype(F32)
        lane = lax.broadcasted_iota(jnp.int32, (FFN_TM, LANES), 1)
        gcol = jnp.sum(jnp.where((lane == e) | (lane == e + N_EXPERTS), g, 0.0), axis=1, keepdims=True)
        a = _dot(x, wg_s[...])
        b = _dot(x, wu_s[...])
        hc = _silu(a) * b * gcol
        ys_ref[...] = _dot(hc.astype(BF16), wd_s[...]).astype(BF16)


def _ffn(tile_expert, n_tiles, xs, wg, wu, wd, ride):
    q, kc, vc, nk, nv, b1, b2 = ride
    nb, _, _, wb = kc.shape
    n_rows = xs.shape[0]
    n_steps = n_rows // FFN_TM
    assert n_steps % nb == 0 and HEADS_PER_GROUP % (n_steps // nb) == 0
    per = n_steps // nb
    row = lambda i, te, nt: (jnp.minimum(i, nt[0] - 1), 0)
    wspec = lambda shape: pl.BlockSpec((1,) + shape, lambda i, te, nt: (te[i], 0, 0))
    rblk = lambda *tail: pl.BlockSpec((1, HEADS_PER_GROUP) + tail, lambda i, te, nt: (i // per, 0, 0, 0))
    const = lambda shape: pl.BlockSpec(shape, lambda i, te, nt: (0, 0, 0))
    return pl.pallas_call(
        functools.partial(_ffn_kernel, steps_per_batch=per),
        grid_spec=pltpu.PrefetchScalarGridSpec(
            num_scalar_prefetch=2, grid=(n_steps,),
            in_specs=[pl.BlockSpec((FFN_TM, XG_W), row), wspec((D_MODEL, D_EXPERT)),
                      wspec((D_MODEL, D_EXPERT)), wspec((D_EXPERT, D_MODEL)),
                      rblk(8, HEAD_DIM), rblk(HEAD_DIM, wb), rblk(HEAD_DIM, wb), const(nk.shape), const(nv.shape),
                      const(b1.shape), const(b2.shape)],
            out_specs=[pl.BlockSpec((FFN_TM, D_MODEL), lambda i, te, nt: (i, 0)),
                       rblk(HEAD_DIM, wb), rblk(HEAD_DIM, wb), rblk(8, HEAD_DIM), rblk(8, LANES)],
            scratch_shapes=[pltpu.VMEM((D_MODEL, D_EXPERT), BF16), pltpu.VMEM((D_MODEL, D_EXPERT), BF16),
                            pltpu.VMEM((D_EXPERT, D_MODEL), BF16)]),
        out_shape=[jax.ShapeDtypeStruct((n_rows, D_MODEL), BF16),
                   jax.ShapeDtypeStruct(kc.shape, F32), jax.ShapeDtypeStruct(kc.shape, F32),
                   jax.ShapeDtypeStruct((nb, HEADS_PER_GROUP, 8, HEAD_DIM), F32),
                   jax.ShapeDtypeStruct((nb, HEADS_PER_GROUP, 8, LANES), F32)],
        compiler_params=_params(("arbitrary",)),
        name="moe_ffn",
    )(tile_expert, n_tiles, xs, wg, wu, wd, q, kc, vc, nk, nv, b1, b2)


def _combine_kernel(nblk_ref, ncopy_ref, src_ref, post_ref, xg_ref, x1_ref, g2_ref, wgs_ref, wus_ref,
                    wds_ref, ys_hbm, y_ref, ys_scr, wgs_s, wus_s, wds_s, sem):
    s = pl.program_id(0)
    n = ncopy_ref[s]
    blk_chunks = MXU_ROWS // CHUNK
    trip_chunks = BLOCKS_PER_TRIP * blk_chunks

    def fetch_trip(t, carry):
        k0 = t * trip_chunks
        _for_each(trip_chunks,
                  lambda u: _chunk_copy(ys_hbm, src_ref[0, 0, k0 + u], ys_scr, k0 + u, sem.at[t]).start())
        return carry

    lax.fori_loop(0, nblk_ref[s], fetch_trip, 0)

    @pl.when(s == 0)
    def _():
        wgs_s[...] = wgs_ref[...].astype(BF16)
        wus_s[...] = wus_ref[...].astype(BF16)
        wds_s[...] = wds_ref[...].astype(BF16)

    h = xg_ref[:, 0:D_MODEL]
    a = _dot(h, wgs_s[...])
    b = _dot(h, wus_s[...])
    shared = _dot((_silu(a) * b).astype(BF16), wds_s[...])

    post = post_ref[...]
    pos = [jnp.broadcast_to(post[:, k:k + 1], (SUB, MXU_ROWS)) for k in range(TOP_K)]
    lanes = lax.broadcasted_iota(jnp.int32, (SUB, MXU_ROWS), 1).astype(BF16)

    def gather_blocks(i, acc):
        _for_each(trip_chunks, lambda t: _chunk_copy(ys_hbm, 0, ys_scr, 0, sem.at[i]).wait())
        for j in range(BLOCKS_PER_TRIP):
            blk = i * BLOCKS_PER_TRIP + j
            onehot = _slot_onehot((blk * MXU_ROWS).astype(F32), pos, lanes)
            rows = ys_scr[pl.ds(pl.multiple_of(blk * blk_chunks, blk_chunks), blk_chunks)]
            acc = acc + _dot(onehot, rows.reshape(MXU_ROWS, D_MODEL))
        return acc

    routed = lax.fori_loop(0, nblk_ref[s], gather_blocks, jnp.zeros((SUB, D_MODEL), F32))
    y_ref[...] = x1_ref[...] + g2_ref[0] * (routed + shared)


def _combine(nblk, ncopy, src, post, xg, x1, g2, wgs, wus, wds, ys):
    n_tok = x1.shape[0]
    nsub = n_tok // SUB
    per_b = nsub // g2.shape[0]
    tab = pl.BlockSpec((1, 1, TABLE_W), lambda s, *_: (s, 0, 0), memory_space=pltpu.SMEM)
    const2 = lambda shape: pl.BlockSpec(shape, lambda s, *_: (0, 0))
    return pl.pallas_call(
        _combine_kernel,
        grid_spec=pltpu.PrefetchScalarGridSpec(
            num_scalar_prefetch=2, grid=(nsub,),
            in_specs=[tab, pl.BlockSpec((SUB, LANES), lambda s, *_: (s, 0)),
                      pl.BlockSpec((SUB, XG_W), lambda s, *_: (s, 0)),
                      pl.BlockSpec((SUB, D_MODEL), lambda s, *_: (s, 0)),
                      pl.BlockSpec((1, 1, D_MODEL), lambda s, *_: (s // per_b, 0, 0)),
                      const2((D_MODEL, D_EXPERT)), const2((D_MODEL, D_EXPERT)), const2((D_EXPERT, D_MODEL)),
                      pl.BlockSpec(memory_space=pl.ANY)],
            out_specs=pl.BlockSpec((SUB, D_MODEL), lambda s, *_: (s, 0)),
            scratch_shapes=[pltpu.VMEM((N_CHUNKS, CHUNK, D_MODEL), BF16), pltpu.VMEM((D_MODEL, D_EXPERT), BF16),
                            pltpu.VMEM((D_MODEL, D_EXPERT), BF16), pltpu.VMEM((D_EXPERT, D_MODEL), BF16),
                            pltpu.SemaphoreType.DMA((SUB_ROWS // (BLOCKS_PER_TRIP * MXU_ROWS),))]),
        out_shape=jax.ShapeDtypeStruct((n_tok, D_MODEL), F32),
        compiler_params=_params(("arbitrary",)),
        name="moe_combine",
    )(nblk, ncopy, src, post, xg, x1, g2, wgs, wus, wds, ys.reshape(-1, CHUNK, D_MODEL))


def _route_tables(cnt):
    nsub = cnt.shape[0]
    i32 = jnp.int32
    seg_end = jnp.cumsum(cnt, axis=1)
    seg_start = seg_end - cnt
    tot = jnp.sum(cnt, axis=0)
    tot_al = (tot + FFN_TM - 1) // FFN_TM * FFN_TM
    reg_end = jnp.cumsum(tot_al)
    reg_start = reg_end - tot_al
    base = reg_start[None, :] + jnp.cumsum(cnt, axis=0) - cnt
    n_real = seg_end[:, -1] // CHUNK
    k = jnp.arange(TABLE_W, dtype=i32)
    owner = jnp.sum((seg_end[:, None, :] // CHUNK) <= k[None, :, None], axis=2)
    owner = jnp.minimum(owner, N_EXPERTS - 1)
    delta = (base - seg_start) // CHUNK
    e_ids = jnp.arange(N_EXPERTS, dtype=i32)
    real_dst = jnp.sum(jnp.where(owner[:, :, None] == e_ids, delta[:, None, :], 0), axis=2) + k[None, :]
    fill_n = (tot_al - tot) // CHUNK
    fill_dst0 = (reg_start + tot) // CHUNK
    src = jnp.broadcast_to(k[None, :], (nsub, TABLE_W))
    dst = real_dst
    ncopy = n_real
    for r in range(-(-N_EXPERTS // nsub)):
        e_of = jnp.arange(nsub, dtype=i32) + r * nsub
        ok = e_of < N_EXPERTS
        e_cl = jnp.minimum(e_of, N_EXPERTS - 1)
        fn = jnp.where(ok, fill_n[e_cl], 0)
        j = k[None, :] - ncopy[:, None]
        is_fill = (j >= 0) & (j < fn[:, None])
        src = jnp.where(is_fill, ZERO_CHUNK, src)
        dst = jnp.where(is_fill, fill_dst0[e_cl][:, None] + j, dst)
        ncopy = ncopy + fn
    trip_rows = BLOCKS_PER_TRIP * MXU_ROWS
    disp_nblk = (n_real * CHUNK + trip_rows - 1) // trip_rows
    comb_n = disp_nblk * (trip_rows // CHUNK)
    comb_src = jnp.where(k[None, :] < n_real[:, None], real_dst, real_dst[:, 0:1])
    n_tiles = reg_end[-1] // FFN_TM
    t = jnp.arange((nsub * SUB_ROWS + N_EXPERTS * FFN_TM) // FFN_TM, dtype=i32)
    tile_expert = jnp.minimum(jnp.sum((reg_end[None, :] // FFN_TM) <= t[:, None], axis=1), N_EXPERTS - 1)
    as3 = lambda a: a.astype(i32).reshape(nsub, 1, TABLE_W)
    return (disp_nblk.astype(i32), n_real.astype(i32), ncopy.astype(i32), as3(src), as3(dst),
            comb_n.astype(i32), as3(comb_src), tile_expert.astype(i32), n_tiles.astype(i32).reshape(1))


def _moe_sparse(xg, post, posr, cnt, x1, g2, wg, wu, wd, wgs, wus, wds, ride):
    B, S, _ = x1.shape
    n_tok = B * S
    nsub = n_tok // SUB
    (disp_nblk, n_real, disp_n, disp_src, disp_dst, comb_n, comb_src, tile_expert,
     n_tiles) = _route_tables(cnt[:, :, 0].astype(jnp.int32))
    n_rows = nsub * SUB_ROWS + N_EXPERTS * FFN_TM
    xg2 = xg.reshape(n_tok, XG_W)
    xs = _dispatch(disp_nblk, n_real, disp_n, n_tiles, disp_src, disp_dst, posr, xg2, n_rows)
    ys, *ride_out = _ffn(tile_expert, n_tiles, xs, wg, wu, wd, ride)
    y = _combine(disp_nblk, comb_n, comb_src, post.reshape(n_tok, LANES), xg2, x1.reshape(n_tok, D_MODEL),
                 g2, wgs, wus, wds, ys)
    return y.reshape(B, S, D_MODEL), ride_out


def _t5_bucket(dist):
    max_exact = N_BUCKETS // 2
    df = jnp.maximum(dist, 1).astype(F32)
    large = max_exact + (jnp.log(df / max_exact) / math.log(MAX_DISTANCE / max_exact)
                         * (N_BUCKETS - max_exact)).astype(jnp.int32)
    large = jnp.minimum(large, N_BUCKETS - 1)
    return jnp.where(dist < max_exact, dist, large)


def _step_bias(rel_bias, g):
    dist = jnp.arange(N_KEYS, dtype=jnp.int32) * DILATIONS[g]
    cols = rel_bias[:, g * HEADS_PER_GROUP:(g + 1) * HEADS_PER_GROUP]
    return cols[_t5_bucket(dist)].T.astype(F32)


def _prompt_bias(bias_k):
    h = bias_k.shape[0]
    n = 3 * BLK
    row = jnp.concatenate([bias_k[:, ::-1], jnp.full((h, n - N_KEYS), NEG, F32)], axis=1)
    t = jnp.tile(row, (1, BLK))[:, :BLK * (n - 1)].reshape(h, BLK, n - 1)
    return t[:, :, :2 * BLK]


def _sample_bias(bias_k, wb, d):
    h = bias_k.shape[0]
    rev = bias_k[:, :0:-1]
    rows = []
    for t in range(4):
        if d == 1:
            rows.append(jnp.concatenate([jnp.full((h, t), NEG, F32), rev[:, :wb - t]], axis=1))
        else:
            r = jnp.arange(d)[None, None, :]
            rows.append(jnp.where(r == t, rev[:, :, None], NEG).reshape(h, wb))
    b1 = jnp.concatenate([jnp.stack(rows, axis=1), jnp.zeros((h, 4, wb), F32)], axis=1)
    rows = []
    for t in range(4):
        cols = [bias_k[:, (t - tn) // d] if (t >= tn and (t - tn) % d == 0) else jnp.full((h,), NEG, F32)
                for tn in range(4)]
        rows.append(jnp.concatenate([jnp.full((h, LANES - 4), NEG, F32), jnp.stack(cols, axis=1)], axis=1))
    b2 = jnp.concatenate([jnp.stack(rows, axis=1), jnp.zeros((h, 4, LANES), F32)], axis=1)
    return b1, b2


TILE = 512
RIDE_ON_OPROJ, RIDE_ON_PROJ, RIDE_ON_FFN = 0, 1, 2


def _sample_front(x, mods, hist, caches, rel_bias, weights):
    norm1_g, _, w_in_bf, qg, kg, conv_w, bd = weights[:7]
    sh1, sc1 = mods[:2]
    S = x.shape[1]
    nb = S // 4
    qf, kt, vt, u, co = _proj(x, sc1, sh1, norm1_g, w_in_bf, qg, kg, conv_w, bd, hist,
                              tile=TILE, dils=(1, 1, 1), sample=True)
    rides = []
    for g in range(N_GROUPS):
        d, wb = DILATIONS[g], WINDOWS[g]
        kc, vc = caches[g]
        b1, b2 = _sample_bias(_step_bias(rel_bias, g), wb, d)
        q = qf[0, :, g * GROUP_W:(g + 1) * GROUP_W].reshape(nb, 4, HEADS_PER_GROUP, HEAD_DIM)
        q = jnp.pad(q.transpose(0, 2, 1, 3), ((0, 0), (0, 0), (0, 4), (0, 0))).astype(BF16)
        new_t = lambda a: a[g * GROUP_W:(g + 1) * GROUP_W].reshape(HEADS_PER_GROUP, HEAD_DIM, S)
        rides.append((q, kc, vc, new_t(kt), new_t(vt), b1, b2))
    return dict(x=x, mods=mods, u=u, co=co), rides


def _sample_back(front, ride_outs, weights):
    (_, norm2_g, _, _, _, _, _, w_o_bf, w_r_t, b_r, w_gate_e, w_up_e, w_down_e,
     w_gate_s, w_up_s, w_down_s) = weights
    x, u, co = front["x"], front["u"], front["co"]
    _, _, g1, sh2, sc2, g2 = front["mods"]
    S = x.shape[1]
    nb = S // 4
    states, o_list, l_list = [], [], []
    for g in range(N_GROUPS):
        ko, vo, o, lse = ride_outs[g]
        states += [ko.transpose(0, 3, 1, 2)[None], vo.transpose(0, 3, 1, 2)[None]]
        o_list.append(o[:, :, :4].transpose(0, 2, 1, 3).reshape(1, 1, S, GROUP_W))
        lse = jnp.broadcast_to(lse[:, :, :4, :1], (nb, HEADS_PER_GROUP, 4, HEAD_DIM))
        l_list.append(lse.transpose(0, 2, 1, 3).reshape(1, 1, S, GROUP_W))
    states.append(u.reshape(nb, 4, CONV_CH)[:, 2:][None])
    x1, xg, gates, _, _, _ = _oproj(x, o_list, l_list, co, g1, sc2, sh2, norm2_g, w_o_bf, w_r_t, b_r,
                                    tile=TILE, dils=(1, 1, 1))
    y = _moe(xg, gates, x1, g2, w_gate_e, w_up_e, w_down_e, w_gate_s, w_up_s, w_down_s, tile=TILE)
    return y, states


def _prompt_layer(x, mods, rel_bias, weights, rides):
    (norm1_g, norm2_g, w_in_bf, qg, kg, conv_w, bd, w_o_bf, w_r_t, b_r, w_gate_e, w_up_e, w_down_e,
     w_gate_s, w_up_s, w_down_s) = weights
    sh1, sc1, g1, sh2, sc2, g2 = mods
    B, S, _ = x.shape
    ride_outs = [None] * N_GROUPS
    res = _proj(x, sc1, sh1, norm1_g, w_in_bf, qg, kg, conv_w, bd, None, tile=TILE, dils=DILATIONS, sample=False,
                ride=rides[RIDE_ON_PROJ])
    qkv, tails, (u, co), ride_outs[RIDE_ON_PROJ] = res[:9], res[9:15], res[15:17], res[17:]
    states, o_list, l_list = [], [], []
    for g in range(N_GROUPS):
        d, w = DILATIONS[g], WINDOWS[g]
        q, k, v = (a.reshape(B * d, S // d, GROUP_W) for a in qkv[3 * g:3 * g + 3])
        o, lse = _attn_prompt(q, k, v, _prompt_bias(_step_bias(rel_bias, g)))
        o_list.append(o.reshape(B, d, S // d, GROUP_W))
        l_list.append(lse.reshape(B, d, S // d, GROUP_W))
        keep = lambda a: a.reshape(B, HEADS_PER_GROUP, HEAD_DIM, w).transpose(0, 3, 1, 2)[None]
        states += [keep(tails[2 * g]), keep(tails[2 * g + 1])]
    states.append(u[:, 6:][None])
    x1, xg, _, posr, post, cnt, *ride_outs[RIDE_ON_OPROJ] = _oproj(
        x, o_list, l_list, co, g1, sc2, sh2, norm2_g, w_o_bf, w_r_t, b_r, tile=TILE, dils=DILATIONS,
        ride=rides[RIDE_ON_OPROJ])
    y, ride_outs[RIDE_ON_FFN] = _moe_sparse(xg, post, posr, cnt, x1, g2, w_gate_e, w_up_e, w_down_e, w_gate_s,
                                            w_up_s, w_down_s, rides[RIDE_ON_FFN])
    return y, states, ride_outs


def kernel(x_prompt, x_sample, c_prompt, c_sample, cache_k_w128, cache_v_w128, cache_k_w512, cache_v_w512, cache_k_w2048, cache_v_w2048, state_conv, rel_bias, norm1_g, norm2_g, w_ada, b_ada, w_in, q_norm_g, k_norm_g, conv_w, w_o, w_router, b_router, w_gate_e, w_up_e, w_down_e, w_gate_s, w_up_s, w_down_s):
    B = x_prompt.shape[0]
    DB, T = x_sample.shape[:2]
    n_c = B + DB
    c_all = jnp.pad(jnp.concatenate([c_prompt, c_sample], axis=0), ((0, (-n_c) % 8), (0, 0)))
    mod = _ada(c_all, w_ada[0], b_ada)
    chunks = [mod[:, i * D_MODEL:(i + 1) * D_MODEL] for i in range(6)]
    mods_p = [c[:B].reshape(B, 1, D_MODEL) for c in chunks]
    mods_s = [jnp.repeat(c[B:n_c], T, axis=0).reshape(1, DB * T, D_MODEL) for c in chunks]

    eye = jnp.arange(GROUP_W) // HEAD_DIM
    bd = jnp.where(eye[:, None] == eye[None, :], 1.0 / HEAD_DIM, 0.0).astype(BF16)
    weights = (norm1_g, norm2_g, w_in[0].astype(BF16),
               jnp.tile(q_norm_g, (1, HEADS_PER_GROUP)), jnp.tile(k_norm_g, (1, HEADS_PER_GROUP)),
               conv_w[0], bd, w_o[0].astype(BF16), w_router[0].T, b_router.reshape(N_EXPERTS, 1),
               w_gate_e[0], w_up_e[0], w_down_e[0], w_gate_s[0], w_up_s[0], w_down_s[0])

    s0, s1 = state_conv[0, :, 0], state_conv[0, :, 1]
    zero = jnp.zeros_like(s0)
    hist_a = jnp.stack([s0, s1, zero, zero], axis=1).reshape(DB * T, CONV_CH)
    hist_b = jnp.stack([s1, zero, zero, zero], axis=1).reshape(DB * T, CONV_CH)
    caches = [(ck[0].transpose(0, 2, 3, 1), cv[0].transpose(0, 2, 3, 1))
              for ck, cv in ((cache_k_w128, cache_v_w128), (cache_k_w512, cache_v_w512),
                             (cache_k_w2048, cache_v_w2048))]
    front, ride = _sample_front(x_sample.reshape(1, DB * T, D_MODEL), mods_s, (hist_a, hist_b), caches,
                                rel_bias, weights)
    yp, st_p, ride_out = _prompt_layer(x_prompt, mods_p, rel_bias, weights, ride)
    ys, st_s = _sample_back(front, ride_out, weights)
    return (yp, ys.reshape(DB, T, D_MODEL), *st_p, *st_s)
```

```python
import functools
import math

import jax
import jax.numpy as jnp
from jax import lax
from jax.experimental import pallas as pl
from jax.experimental.pallas import tpu as pltpu

D_MODEL = 1024
HEAD_DIM = 64
HEADS_PER_GROUP = 4
GROUP_W = HEADS_PER_GROUP * HEAD_DIM
WINDOWS = (128, 512, 2048)
DILATIONS = (1, 4, 16)
N_GROUPS = 3
ATTN_W = N_GROUPS * GROUP_W
CONV_CH = 256
N_PROJ = 3 * ATTN_W + 3 * CONV_CH
N_STEPS = 128
N_KEYS = N_STEPS + 1
BLK = 128
N_BUCKETS = 32
MAX_DISTANCE = WINDOWS[-1]
N_EXPERTS = 64
TOP_K = 8
N_ROUTE_GROUPS = 8
GROUP_SIZE = N_EXPERTS // N_ROUTE_GROUPS
TOPK_GROUPS = 4
D_EXPERT = 256
ROUTED_SCALE = 2.5
RMS_EPS = 1e-6
NEG = -1e30
LANES = 128
SUB = 256
CHUNK = 16
XG_W = D_MODEL + LANES
VMEM_LIMIT = 56 * 1024 * 1024

F32 = jnp.float32
BF16 = jnp.bfloat16


def _dot(a, b):
    return jnp.dot(a, b, preferred_element_type=F32)


def _dot_nt(a, b):
    return lax.dot_general(a, b, (((1,), (1,)), ((), ())), preferred_element_type=F32)


def _silu(a):
    return a / (1.0 + jnp.exp(-a))


def _params(sem):
    return pltpu.CompilerParams(dimension_semantics=sem, vmem_limit_bytes=VMEM_LIMIT)


def _ada_kernel(c_ref, w_ref, b_ref, o_ref):
    s = _silu(c_ref[...])
    o_ref[...] = _dot(s.astype(BF16), w_ref[...].astype(BF16)) + b_ref[...]


def _ada(c, w_ada, b_ada):
    n = c.shape[0]
    nc = w_ada.shape[1] // D_MODEL
    return pl.pallas_call(
        _ada_kernel,
        grid=(nc,),
        in_specs=[pl.BlockSpec((n, D_MODEL), lambda j: (0, 0)),
                  pl.BlockSpec((D_MODEL, D_MODEL), lambda j: (0, j)),
                  pl.BlockSpec((1, D_MODEL), lambda j: (0, j))],
        out_specs=pl.BlockSpec((n, D_MODEL), lambda j: (0, j)),
        out_shape=jax.ShapeDtypeStruct((n, w_ada.shape[1]), F32),
        compiler_params=_params(("arbitrary",)),
        name="ada",
    )(c, w_ada, b_ada)


def _proj_kernel(*refs, tile, dils, sample, nt):
    if sample:
        (x_ref, sc_ref, sh_ref, ng_ref, w_ref, qg_ref, kg_ref, cw_ref, bd_ref, ha_ref, hb_ref,
         qf_ref, kt_ref, vt_ref, u_ref, co_ref, u_scr) = refs
    else:
        (x_ref, sc_ref, sh_ref, ng_ref, w_ref, qg_ref, kg_ref, cw_ref, bd_ref) = refs[:9]
        ride_in, refs = refs[9:9 + N_RIDE_IN], refs[9 + N_RIDE_IN:]
        (q0, k0, v0, q1, k1, v1, q2, k2, v2, kt0, vt0, kt1, vt1, kt2, vt2, u_ref, co_ref) = refs[:17]
        ride_out, (slab, u_scr) = refs[17:17 + N_RIDE_OUT], refs[17 + N_RIDE_OUT:]
        qkv_out = ((q0, k0, v0), (q1, k1, v1), (q2, k2, v2))
        tails = ((kt0, vt0), (kt1, vt1), (kt2, vt2))
        j = pl.program_id(1)
        _ride_block(pl.program_id(0) * nt + j, ride_in + ride_out)

    x = x_ref[0]
    ms = jnp.mean(x * x, axis=-1, keepdims=True)
    h = x * lax.rsqrt(ms + RMS_EPS) * ng_ref[...]
    h = h * (1.0 + sc_ref[0]) + sh_ref[0]
    proj = _dot(h.astype(BF16), w_ref[...])

    bd = bd_ref[...]

    def headnorm(z, g):
        msq = _dot((z * z).astype(BF16), bd)
        return z * lax.rsqrt(msq + RMS_EPS) * g

    slab_i = 0
    for g in range(N_GROUPS):
        c0 = g * GROUP_W
        qn = headnorm(proj[:, c0:c0 + GROUP_W], qg_ref[...]) * (HEAD_DIM ** -0.5)
        kn = headnorm(proj[:, ATTN_W + c0:ATTN_W + c0 + GROUP_W], kg_ref[...])
        vv = proj[:, 2 * ATTN_W + c0:2 * ATTN_W + c0 + GROUP_W]
        if sample:
            qf_ref[0, :, c0:c0 + GROUP_W] = qn
            kt_ref[c0:c0 + GROUP_W, :] = kn.T
            vt_ref[c0:c0 + GROUP_W, :] = vv.T
            continue
        keep = min(WINDOWS[g], tile)

        @pl.when(j >= nt - max(WINDOWS[g] // tile, 1))
        def _(kn=kn, vv=vv, g=g, keep=keep):
            tails[g][0][0] = kn[tile - keep:, :].T
            tails[g][1][0] = vv[tile - keep:, :].T

        d = dils[g]
        for val, out in zip((qn, kn, vv), qkv_out[g]):
            if d == 1:
                out[0, 0] = val.astype(BF16)
                continue
            n = tile // d
            for half in range(GROUP_W // LANES):
                slab[slab_i] = val[:, half * LANES:(half + 1) * LANES]
                for r in range(d):
                    out[0, r, :, half * LANES:(half + 1) * LANES] = (
                        slab[slab_i, pl.ds(r, n, stride=d), :].astype(BF16))
                slab_i += 1

    base = 3 * ATTN_W
    u = proj[:, base + 2 * CONV_CH:base + 3 * CONV_CH] * proj[:, base:base + CONV_CH]
    gate_b = proj[:, base + CONV_CH:base + 2 * CONV_CH]
    if sample:
        u_scr[0:8, :] = jnp.zeros((8, CONV_CH), F32)
    else:
        @pl.when(j == 0)
        def _():
            u_scr[0:8, :] = jnp.zeros((8, CONV_CH), F32)

        @pl.when(j > 0)
        def _():
            u_scr[0:8, :] = u_scr[tile:tile + 8, :]

    u_scr[8:tile + 8, :] = u
    um1 = u_scr[7:tile + 7, :]
    um2 = u_scr[6:tile + 6, :]
    if sample:
        t = lax.broadcasted_iota(jnp.int32, (tile, CONV_CH), 0) % 4
        um1 = jnp.where(t >= 1, um1, 0.0) + hb_ref[...]
        um2 = jnp.where(t >= 2, um2, 0.0) + ha_ref[...]
    cw = cw_ref[...]
    conv = cw[0:1] * um2 + cw[1:2] * um1 + cw[2:3] * u
    u_ref[0] = u if sample else u[tile - 8:, :]
    co_ref[0] = (gate_b * conv).astype(BF16)


def _proj(x, sc, sh, ng, w_bf, qg, kg, cw, bd, hist, *, tile, dils, sample, ride=None):
    B, S, _ = x.shape
    nt = S // tile
    tm = sc.shape[1]
    mod_spec = pl.BlockSpec((1, tm, D_MODEL), (lambda b, j: (b, j, 0)) if tm > 1 else (lambda b, j: (b, 0, 0)))
    const2 = lambda shape: pl.BlockSpec(shape, lambda b, j: (0, 0))
    in_specs = [pl.BlockSpec((1, tile, D_MODEL), lambda b, j: (b, j, 0)), mod_spec, mod_spec,
                const2((1, D_MODEL)), const2((D_MODEL, N_PROJ)), const2((1, GROUP_W)),
                const2((1, GROUP_W)), const2((3, CONV_CH)), const2((GROUP_W, GROUP_W))]
    args = [x, sc, sh, ng, w_bf, qg, kg, cw, bd]
    nat = lambda w: pl.BlockSpec((1, tile, w), lambda b, j: (b, j, 0))
    out_specs, out_shape = [], []
    scratch = []
    if sample:
        assert B == 1 and nt == 1
        in_specs += [pl.BlockSpec((tile, CONV_CH), lambda b, j: (j, 0))] * 2
        args += list(hist)
        out_specs += [nat(ATTN_W)] + [pl.BlockSpec((ATTN_W, tile), lambda b, j: (0, j))] * 2 + [nat(CONV_CH)]
        out_shape += [jax.ShapeDtypeStruct((B, S, ATTN_W), F32)]
        out_shape += [jax.ShapeDtypeStruct((ATTN_W, S), F32)] * 2
        out_shape += [jax.ShapeDtypeStruct((B, S, CONV_CH), F32)]
    else:
        for d in dils:
            for _ in range(3):
                out_specs.append(pl.BlockSpec((1, d, tile // d, GROUP_W), lambda b, j: (b, 0, j, 0)))
                out_shape.append(jax.ShapeDtypeStruct((B, d, S // d, GROUP_W), BF16))
        for w in WINDOWS:
            keep, first = min(w, tile), nt - max(w // tile, 1)
            for _ in range(2):
                out_specs.append(pl.BlockSpec((1, GROUP_W, keep),
                                              lambda b, j, first=first: (b, 0, jnp.maximum(j - first, 0))))
                out_shape.append(jax.ShapeDtypeStruct((B, GROUP_W, w), F32))
        out_specs.append(pl.BlockSpec((1, 8, CONV_CH), lambda b, j: (b, 0, 0)))
        out_shape.append(jax.ShapeDtypeStruct((B, 8, CONV_CH), F32))
        n_slabs = sum(3 * (GROUP_W // LANES) for d in dils if d > 1)
        scratch.append(pltpu.VMEM((n_slabs, tile, LANES), F32))
    out_specs.append(nat(CONV_CH))
    out_shape.append(jax.ShapeDtypeStruct((B, S, CONV_CH), BF16))
    scratch.append(pltpu.VMEM((tile + 8, CONV_CH), F32))
    if not sample:
        nb = ride[1].shape[0]
        r_in, r_out, r_shape = _ride_specs(ride, nb // (B * nt), lambda b, j: b * nt + j)
        in_specs, args = in_specs + r_in, args + list(ride)
        out_specs, out_shape = out_specs + r_out, out_shape + r_shape
    return pl.pallas_call(
        functools.partial(_proj_kernel, tile=tile, dils=dils, sample=sample, nt=nt),
        grid=(B, nt),
        in_specs=in_specs,
        out_specs=out_specs,
        out_shape=out_shape,
        scratch_shapes=scratch,
        compiler_params=_params(("arbitrary", "arbitrary")),
        name="proj_sample" if sample else "proj_prompt",
    )(*args)


def _attn_kernel(q_ref, kp_ref, kc_ref, vp_ref, vc_ref, b_ref, o_ref, l_ref, s_scr, m_scr, o_scr, z_scr,
                 *, nq):
    first = pl.program_id(1) == 0
    low = lax.broadcasted_iota(jnp.int32, (BLK, LANES), 1) < HEAD_DIM
    ones = jnp.ones((BLK, LANES), BF16)
    chains = [(j, pair, sub) for j in range(nq) for pair in range(GROUP_W // LANES) for sub in range(2)]

    def operands(j, pair):
        rows = slice(j * BLK, (j + 1) * BLK)
        sl = slice(pair * LANES, (pair + 1) * LANES)
        if j == 0:
            return rows, sl, kp_ref[0, :, sl], vp_ref[0, :, sl]
        prows = slice((j - 1) * BLK, j * BLK)
        return rows, sl, kc_ref[0, prows, sl], vc_ref[0, prows, sl]

    for c, (j, pair, sub) in enumerate(chains):
        rows, sl, kp, _ = operands(j, pair)
        q = q_ref[0, rows, sl]
        qm = jnp.where(low if sub == 0 else ~low, q, jnp.zeros_like(q))
        hh = 2 * pair + sub
        sp = _dot_nt(qm, kp) + b_ref[hh, :, 0:BLK]
        if j == 0:
            sp = jnp.where(first, NEG, sp)
        sc = _dot_nt(qm, kc_ref[0, rows, sl]) + b_ref[hh, :, BLK:2 * BLK]
        s_scr[c, :, 0:BLK] = sp
        s_scr[c, :, BLK:2 * BLK] = sc
        m_scr[c] = jnp.broadcast_to(jnp.max(jnp.maximum(sp, sc), axis=-1, keepdims=True), (BLK, LANES))

    for c, (j, pair, sub) in enumerate(chains):
        rows, sl, _, vp = operands(j, pair)
        m = m_scr[c]
        pp = jnp.exp(s_scr[c, :, 0:BLK] - m).astype(BF16)
        pc = jnp.exp(s_scr[c, :, BLK:2 * BLK] - m).astype(BF16)
        o_scr[c] = _dot(pp, vp) + _dot(pc, vc_ref[0, rows, sl])
        z_scr[c] = _dot(pp, ones) + _dot(pc, ones)

    for c in range(0, len(chains), 2):
        j, pair, _ = chains[c]
        rows = slice(j * BLK, (j + 1) * BLK)
        sl = slice(pair * LANES, (pair + 1) * LANES)
        o_sub = [o_scr[c + sub] / z_scr[c + sub] for sub in range(2)]
        l_sub = [m_scr[c + sub] + jnp.log(z_scr[c + sub]) for sub in range(2)]
        o_ref[0, rows, sl] = jnp.where(low, o_sub[0], o_sub[1])
        l_ref[0, rows, sl] = jnp.where(low, l_sub[0], l_sub[1])


def _attn_prompt(q, k, v, bias):
    Z, L, _ = q.shape
    qt = min(L, 8 * BLK)
    nq = qt // BLK
    cur = pl.BlockSpec((1, qt, GROUP_W), lambda z, i: (z, i, 0))
    prev = pl.BlockSpec((1, BLK, GROUP_W), lambda z, i: (z, jnp.maximum(i * nq - 1, 0), 0))
    return pl.pallas_call(
        functools.partial(_attn_kernel, nq=nq),
        grid=(Z, L // qt),
        in_specs=[cur, prev, cur, prev, cur,
                  pl.BlockSpec((HEADS_PER_GROUP, BLK, 2 * BLK), lambda z, i: (0, 0, 0))],
        out_specs=[cur, cur],
        out_shape=[jax.ShapeDtypeStruct((Z, L, GROUP_W), F32)] * 2,
        scratch_shapes=[pltpu.VMEM((nq * HEADS_PER_GROUP, BLK, 2 * BLK), F32)]
        + [pltpu.VMEM((nq * HEADS_PER_GROUP, BLK, LANES), F32)] * 3,
        compiler_params=_params(("arbitrary", "arbitrary")),
        name="attn_prompt",
    )(q, k, k, v, v, bias)


def _attn_sample_batch(gb, b, q_ref, kc_ref, vc_ref, nk_ref, nv_ref, b1_ref, b2_ref, ko_ref, vo_ref, o_ref,
                       l_ref, heads=range(HEADS_PER_GROUP)):
    wb = kc_ref.shape[-1]
    lane = lax.broadcasted_iota(jnp.int32, (HEAD_DIM, LANES), 1)
    new_cols = lane >= LANES - 4
    per_tile = LANES // 4
    tile_at = pl.ds(pl.multiple_of((gb // per_tile) * LANES, LANES), LANES)
    shift = (LANES - 4) - 4 * (gb % per_tile)
    for hh in heads:
        q = q_ref[b, hh]
        kt = kc_ref[b, hh]
        vt = vc_ref[b, hh]
        nkt = pltpu.roll(nk_ref[hh, :, tile_at], shift, axis=1)
        nvt = pltpu.roll(nv_ref[hh, :, tile_at], shift, axis=1)
        s1 = _dot(q, kt.astype(BF16)) + b1_ref[hh]
        s2 = _dot(q, nkt.astype(BF16)) + b2_ref[hh]
        m = jnp.maximum(jnp.max(s1, axis=-1, keepdims=True), jnp.max(s2, axis=-1, keepdims=True))
        p1 = jnp.exp(s1 - m)
        p2 = jnp.exp(s2 - m)
        z = jnp.sum(p1, axis=-1, keepdims=True) + jnp.sum(p2, axis=-1, keepdims=True)
        o = _dot_nt(p1.astype(BF16), vt.astype(BF16)) + _dot_nt(p2.astype(BF16), nvt.astype(BF16))
        o_ref[b, hh] = o / z
        l_ref[b, hh] = jnp.broadcast_to(m + jnp.log(z), (8, LANES))
        for src, new, dst in ((kt, nkt, ko_ref), (vt, nvt, vo_ref)):
            rolled = pltpu.roll(src, wb - 4, axis=1)
            if wb > LANES:
                dst[b, hh, :, 0:wb - LANES] = rolled[:, 0:wb - LANES]
            dst[b, hh, :, wb - LANES:wb] = jnp.where(new_cols, new, rolled[:, wb - LANES:wb])


N_RIDE_IN = 7
N_RIDE_OUT = 4


def _ride_specs(ride, bb, block_index):
    q, kc, vc, nk, nv, b1, b2 = ride
    nb, _, _, wb = kc.shape
    blk = lambda *tail: pl.BlockSpec((bb, HEADS_PER_GROUP) + tail, lambda *g: (block_index(*g), 0, 0, 0))
    const = lambda a: pl.BlockSpec(a.shape, lambda *g: (0,) * a.ndim)
    in_specs = [blk(8, HEAD_DIM), blk(HEAD_DIM, wb), blk(HEAD_DIM, wb), const(nk), const(nv), const(b1), const(b2)]
    out_specs = [blk(HEAD_DIM, wb), blk(HEAD_DIM, wb), blk(8, HEAD_DIM), blk(8, LANES)]
    out_shape = [jax.ShapeDtypeStruct(kc.shape, F32), jax.ShapeDtypeStruct(kc.shape, F32),
                 jax.ShapeDtypeStruct((nb, HEADS_PER_GROUP, 8, HEAD_DIM), F32),
                 jax.ShapeDtypeStruct((nb, HEADS_PER_GROUP, 8, LANES), F32)]
    return in_specs, out_specs, out_shape


def _ride_block(step, ride_refs):
    bb = ride_refs[0].shape[0]
    for b in range(bb):
        _attn_sample_batch(step * bb + b, b, *ride_refs)


def _first_max(v, ids, sentinel):
    m = jnp.max(v, axis=0, keepdims=True)
    idx = jnp.min(jnp.where(v == m, ids, sentinel), axis=0, keepdims=True)
    return m, ids == idx


def _oproj_kernel(*refs, tile, dils, ride):
    (x_ref, o0, o1, o2, l0, l1, l2, co_ref, g1_ref, sc_ref, sh_ref, ng_ref, wo_ref, wr_ref, br_ref) = refs[:15]
    refs = refs[15:]
    if ride:
        ride_in, refs = refs[:N_RIDE_IN], refs[N_RIDE_IN:]
        ride_out = refs[6:6 + N_RIDE_OUT]
    x1_ref, xg_ref, gt_ref, posr_ref, post_ref, cnt_ref = refs[:6]
    slab = refs[-1]
    o_refs, l_refs = (o0, o1, o2), (l0, l1, l2)
    outs, lses = [], []
    slab_i = 0
    for g, d in enumerate(dils):
        if d == 1:
            outs.append(o_refs[g][0, 0])
            lses.append(l_refs[g][0, 0])
            continue
        n = tile // d
        for ref, dest in ((o_refs[g], outs), (l_refs[g], lses)):
            halves = []
            for half in range(GROUP_W // LANES):
                for r in range(d):
                    slab[slab_i, pl.ds(r, n, stride=d), :] = ref[0, r, :, half * LANES:(half + 1) * LANES]
                halves.append(slab[slab_i])
                slab_i += 1
            dest.append(jnp.concatenate(halves, axis=1))

    m = jnp.maximum(jnp.maximum(lses[0], lses[1]), lses[2])
    es = [jnp.exp(l - m) for l in lses]
    den = es[0] + es[1] + es[2]
    cat = [(outs[g] * (es[g] / den)).astype(BF16) for g in range(N_GROUPS)]
    cat.append(co_ref[0].astype(BF16))
    cat = jnp.concatenate(cat, axis=1)
    x1 = x_ref[0] + g1_ref[0] * _dot(cat, wo_ref[...])
    x1_ref[0] = x1

    ms = jnp.mean(x1 * x1, axis=-1, keepdims=True)
    h2 = x1 * lax.rsqrt(ms + RMS_EPS) * ng_ref[...]
    h2 = h2 * (1.0 + sc_ref[0]) + sh_ref[0]
    hh = h2.astype(BF16)
    xg_ref[0, :, 0:D_MODEL] = hh

    hl = (h2 - hh.astype(F32)).astype(BF16)
    wr = wr_ref[...]
    wh = wr.astype(BF16)
    wl = (wr - wh.astype(F32)).astype(BF16)
    logits = _dot_nt(wh, hh) + _dot_nt(wh, hl) + _dot_nt(wl, hh)
    scores = 1.0 / (1.0 + jnp.exp(-logits))
    sel = scores + br_ref[...]

    ids = lax.broadcasted_iota(jnp.int32, (GROUP_SIZE, tile), 0)
    ninf = -jnp.inf
    sel_g = [sel[g * GROUP_SIZE:(g + 1) * GROUP_SIZE] for g in range(N_ROUTE_GROUPS)]
    gscore = jnp.zeros((N_ROUTE_GROUPS, tile), F32)
    for g in range(N_ROUTE_GROUPS):
        m1, oh = _first_max(sel_g[g], ids, GROUP_SIZE)
        m2 = jnp.max(jnp.where(oh, ninf, sel_g[g]), axis=0, keepdims=True)
        gscore = jnp.where(ids == g, m1 + m2, gscore)
    gsel = jnp.zeros((N_ROUTE_GROUPS, tile), F32)
    for _ in range(TOPK_GROUPS):
        _, oh = _first_max(gscore, ids, N_ROUTE_GROUPS)
        gsel = jnp.where(oh, 1.0, gsel)
        gscore = jnp.where(oh, ninf, gscore)
    cand = [jnp.where(gsel[g:g + 1] > 0.0, sel_g[g], ninf) for g in range(N_ROUTE_GROUPS)]
    chosen = [jnp.zeros((GROUP_SIZE, tile), F32) for _ in range(N_ROUTE_GROUPS)]
    picks = []
    for _ in range(TOP_K):
        mx = cand[0]
        for g in range(1, N_ROUTE_GROUPS):
            mx = jnp.maximum(mx, cand[g])
        mx = jnp.max(mx, axis=0, keepdims=True)
        idx = jnp.where(cand[0] == mx, ids, N_EXPERTS)
        for g in range(1, N_ROUTE_GROUPS):
            idx = jnp.minimum(idx, jnp.where(cand[g] == mx, ids + g * GROUP_SIZE, N_EXPERTS))
        idx = jnp.min(idx, axis=0, keepdims=True)
        ohs = [(ids + g * GROUP_SIZE) == idx for g in range(N_ROUTE_GROUPS)]
        picks.append(ohs)
        for g in range(N_ROUTE_GROUPS):
            chosen[g] = jnp.where(ohs[g], 1.0, chosen[g])
            cand[g] = jnp.where(ohs[g], ninf, cand[g])
    wts = [jnp.where(chosen[g] > 0.0, scores[g * GROUP_SIZE:(g + 1) * GROUP_SIZE], 0.0)
           for g in range(N_ROUTE_GROUPS)]
    wsum = jnp.sum(wts[0], axis=0, keepdims=True)
    for g in range(1, N_ROUTE_GROUPS):
        wsum = wsum + jnp.sum(wts[g], axis=0, keepdims=True)
    gates_t = jnp.concatenate([w / wsum * ROUTED_SCALE for w in wts], axis=0)
    gt_ref[0] = jnp.concatenate([gates_t, jnp.zeros((LANES - N_EXPERTS, tile), F32)], axis=0).T
    g_hi = gates_t.astype(BF16).astype(F32)
    xg_ref[0, :, D_MODEL:D_MODEL + LANES] = jnp.concatenate([g_hi, gates_t - g_hi], axis=0).T.astype(BF16)

    er = lax.broadcasted_iota(jnp.int32, (N_EXPERTS, N_EXPERTS), 0)
    ec = lax.broadcasted_iota(jnp.int32, (N_EXPERTS, N_EXPERTS), 1)
    lower = jnp.where(ec < er, 1.0, 0.0).astype(BF16)
    tr = lax.broadcasted_iota(jnp.int32, (SUB, SUB), 0)
    tc = lax.broadcasted_iota(jnp.int32, (SUB, SUB), 1)
    upper = jnp.where(tr < tc, 1.0, 0.0).astype(BF16)
    for u in range(tile // SUB):
        cols = slice(u * SUB, (u + 1) * SUB)
        ch = jnp.concatenate([c[:, cols] for c in chosen], axis=0)
        cnt = jnp.sum(ch, axis=1, keepdims=True)
        cpad = jnp.broadcast_to(jnp.ceil(cnt / CHUNK) * CHUNK, (N_EXPERTS, LANES))
        cnt_ref[u] = cpad
        seg0 = _dot(lower, cpad.astype(BF16))[:, 0:1]
        slot = seg0 + _dot(ch.astype(BF16), upper)
        rows = []
        for k in range(TOP_K):
            acc = jnp.where(picks[k][0][:, cols], slot[0:GROUP_SIZE], 0.0)
            for g in range(1, N_ROUTE_GROUPS):
                acc = acc + jnp.where(picks[k][g][:, cols], slot[g * GROUP_SIZE:(g + 1) * GROUP_SIZE], 0.0)
            rows.append(jnp.sum(acc, axis=0, keepdims=True))
        posr = jnp.concatenate(rows, axis=0)
        posr_ref[u] = posr
        post_ref[0, cols, :] = jnp.concatenate([posr, jnp.zeros((LANES - TOP_K, SUB), F32)], axis=0).T
    if ride:
        _ride_block(pl.program_id(0) * pl.num_programs(1) + pl.program_id(1), ride_in + ride_out)


def _oproj(x, o_list, l_list, co, g1, sc, sh, ng, wo_bf, wr_t, br, *, tile, dils, ride=None):
    B, S, _ = x.shape
    nt = S // tile
    r_in, r_out, r_shape, r_args = [], [], [], []
    if ride is not None:
        r_in, r_out, r_shape = _ride_specs(ride, ride[1].shape[0] // (B * nt), lambda b, j: b * nt + j)
        r_args = list(ride)
    tm = g1.shape[1]
    mod_spec = pl.BlockSpec((1, tm, D_MODEL), (lambda b, j: (b, j, 0)) if tm > 1 else (lambda b, j: (b, 0, 0)))
    const2 = lambda shape: pl.BlockSpec(shape, lambda b, j: (0, 0))
    nat = lambda w: pl.BlockSpec((1, tile, w), lambda b, j: (b, j, 0))
    dspec = [pl.BlockSpec((1, d, tile // d, GROUP_W), lambda b, j: (b, 0, j, 0)) for d in dils]
    n_slabs = sum(2 * (GROUP_W // LANES) for d in dils if d > 1)
    nsub_t = tile // SUB
    return pl.pallas_call(
        functools.partial(_oproj_kernel, tile=tile, dils=dils, ride=ride is not None),
        grid=(B, nt),
        in_specs=[nat(D_MODEL)] + dspec + dspec + [nat(CONV_CH), mod_spec, mod_spec, mod_spec,
                  const2((1, D_MODEL)), const2((D_MODEL, D_MODEL)), const2((N_EXPERTS, D_MODEL)),
                  const2((N_EXPERTS, 1))] + r_in,
        out_specs=[nat(D_MODEL), nat(XG_W), nat(LANES),
                   pl.BlockSpec((nsub_t, TOP_K, SUB), lambda b, j: (b * nt + j, 0, 0)),
                   nat(LANES),
                   pl.BlockSpec((nsub_t, N_EXPERTS, LANES), lambda b, j: (b * nt + j, 0, 0))] + r_out,
        out_shape=[jax.ShapeDtypeStruct((B, S, D_MODEL), F32), jax.ShapeDtypeStruct((B, S, XG_W), BF16),
                   jax.ShapeDtypeStruct((B, S, LANES), F32),
                   jax.ShapeDtypeStruct((B * S // SUB, TOP_K, SUB), F32),
                   jax.ShapeDtypeStruct((B, S, LANES), F32),
                   jax.ShapeDtypeStruct((B * S // SUB, N_EXPERTS, LANES), F32)] + r_shape,
        scratch_shapes=[pltpu.VMEM((max(n_slabs, 1), tile, LANES), F32)],
        compiler_params=_params(("arbitrary", "arbitrary")),
        name="oproj",
    )(x, *o_list, *l_list, co, g1, sc, sh, ng, wo_bf, wr_t, br, *r_args)


DENSE_EXPERTS_PER_STEP = 4


def _moe_kernel(h_ref, gt_ref, x1_ref, g2_ref, wg_ref, wu_ref, wd_ref, wgs_ref, wus_ref, wds_ref,
                y_ref, acc, *, tile):
    step = pl.program_id(2)
    h = h_ref[0]

    @pl.when(step == 0)
    def _():
        a = _dot(h, wgs_ref[...].astype(BF16))
        b = _dot(h, wus_ref[...].astype(BF16))
        acc[...] = _dot((_silu(a) * b).astype(BF16), wds_ref[...].astype(BF16))

    lane = lax.broadcasted_iota(jnp.int32, (tile, LANES), 1)
    routed = None
    for j in range(DENSE_EXPERTS_PER_STEP):
        e = step * DENSE_EXPERTS_PER_STEP + j
        a = _dot(h, wg_ref[j].astype(BF16))
        b = _dot(h, wu_ref[j].astype(BF16))
        gcol = jnp.sum(jnp.where(lane == e, gt_ref[0], 0.0), axis=1, keepdims=True)
        hc = _silu(a) * b * gcol
        y = _dot(hc.astype(BF16), wd_ref[j].astype(BF16))
        routed = y if routed is None else routed + y
    acc[...] += routed

    @pl.when(step == pl.num_programs(2) - 1)
    def _():
        y_ref[0] = x1_ref[0] + g2_ref[0] * acc[...]


def _moe(h2, gates, x1, g2, wg, wu, wd, wgs, wus, wds, *, tile):
    B, S, _ = h2.shape
    tm = g2.shape[1]
    mod_spec = pl.BlockSpec((1, tm, D_MODEL),
                            (lambda b, j, e: (b, j, 0)) if tm > 1 else (lambda b, j, e: (b, 0, 0)))
    nat = lambda w: pl.BlockSpec((1, tile, w), lambda b, j, e: (b, j, 0))
    const2 = lambda shape: pl.BlockSpec(shape, lambda b, j, e: (0, 0))
    return pl.pallas_call(
        functools.partial(_moe_kernel, tile=tile),
        grid=(B, S // tile, N_EXPERTS // DENSE_EXPERTS_PER_STEP),
        in_specs=[nat(D_MODEL), nat(LANES), nat(D_MODEL), mod_spec,
                  pl.BlockSpec((DENSE_EXPERTS_PER_STEP, D_MODEL, D_EXPERT), lambda b, j, e: (e, 0, 0)),
                  pl.BlockSpec((DENSE_EXPERTS_PER_STEP, D_MODEL, D_EXPERT), lambda b, j, e: (e, 0, 0)),
                  pl.BlockSpec((DENSE_EXPERTS_PER_STEP, D_EXPERT, D_MODEL), lambda b, j, e: (e, 0, 0)),
                  const2((D_MODEL, D_EXPERT)), const2((D_MODEL, D_EXPERT)), const2((D_EXPERT, D_MODEL))],
        out_specs=nat(D_MODEL),
        out_shape=jax.ShapeDtypeStruct((B, S, D_MODEL), F32),
        scratch_shapes=[pltpu.VMEM((tile, D_MODEL), F32)],
        compiler_params=_params(("arbitrary", "arbitrary", "arbitrary")),
        name="moe",
    )(h2, gates, x1, g2, wg, wu, wd, wgs, wus, wds)


FFN_TM = 1024
SUB_ROWS = SUB * TOP_K + N_EXPERTS * CHUNK
N_CHUNKS = SUB_ROWS // CHUNK
ZERO_CHUNK = N_CHUNKS
TABLE_W = 256
MXU_ROWS = 256
BLOCKS_PER_TRIP = 4


def _slot_onehot(first, pos_list, axis_iota):
    shape = axis_iota.shape
    hit = axis_iota == jnp.broadcast_to(pos_list[0] - first, shape).astype(BF16)
    for p in pos_list[1:]:
        hit = hit | (axis_iota == jnp.broadcast_to(p - first, shape).astype(BF16))
    return jnp.where(hit, jnp.ones(shape, BF16), jnp.zeros(shape, BF16))


def _chunk_copy(src, src_chunk, dst, dst_chunk, sem):
    return pltpu.make_async_copy(src.at[src_chunk], dst.at[dst_chunk], sem)


def _for_each(n, body):
    def two(t, carry):
        body(2 * t)
        body(2 * t + 1)
        return carry

    lax.fori_loop(0, n // 2, two, 0)

    @pl.when(n % 2 == 1)
    def _():
        body(n - 1)


def _dispatch_kernel(nblk_ref, nreal_ref, ncopy_ref, ntile_ref, src_ref, dst_ref, posr_ref, xg_ref, xs_hbm,
                     xs_scr, zero_scr, sem, tail_sem, *, nsub, n_tiles_max):
    s = pl.program_id(0)

    tile_chunks = FFN_TM // CHUNK
    blk_chunks = MXU_ROWS // CHUNK
    zero_scr[...] = jnp.zeros((tile_chunks, CHUNK, XG_W), BF16)
    tail_blocks = [ntile_ref[0] + s + r * nsub for r in range(-(-(n_tiles_max) // nsub))]

    def tail_copy(t):
        return pltpu.make_async_copy(
            zero_scr, xs_hbm.at[pl.ds(pl.multiple_of(t * tile_chunks, tile_chunks), tile_chunks)], tail_sem)

    for t in tail_blocks:
        @pl.when(t < n_tiles_max)
        def _():
            tail_copy(t).start()

    xg = xg_ref[...]
    pos = [posr_ref[0, k:k + 1, :] for k in range(TOP_K)]
    rows = lax.broadcasted_iota(jnp.int32, (MXU_ROWS, SUB), 0).astype(BF16)
    xs_scr[ZERO_CHUNK] = jnp.zeros((CHUNK, XG_W), BF16)

    def sort_blocks(i, carry):
        for j in range(BLOCKS_PER_TRIP):
            blk = i * BLOCKS_PER_TRIP + j
            onehot = _slot_onehot((blk * MXU_ROWS).astype(F32), pos, rows)
            sorted_rows = _dot(onehot, xg).astype(BF16)
            xs_scr[pl.ds(pl.multiple_of(blk * blk_chunks, blk_chunks), blk_chunks)] = (
                sorted_rows.reshape(blk_chunks, CHUNK, XG_W))
        k0 = i * trip_chunks
        _for_each(jnp.clip(n_real - k0, 0, trip_chunks), lambda t: send(k0 + t))
        return carry

    n_real = nreal_ref[s]
    n = ncopy_ref[s]
    trip_chunks = BLOCKS_PER_TRIP * blk_chunks

    def send(i):
        _chunk_copy(xs_scr, src_ref[0, 0, i], xs_hbm, dst_ref[0, 0, i], sem).start()

    lax.fori_loop(0, nblk_ref[s], sort_blocks, 0)
    _for_each(n - n_real, lambda t: send(n_real + t))
    _for_each(n, lambda i: _chunk_copy(xs_scr, 0, xs_hbm, 0, sem).wait())
    for t in tail_blocks:
        @pl.when(t < n_tiles_max)
        def _():
            tail_copy(t).wait()


def _dispatch(nblk, n_real, ncopy, n_tiles, src, dst, posr, xg, n_rows):
    nsub = posr.shape[0]
    tab = pl.BlockSpec((1, 1, TABLE_W), lambda s, *_: (s, 0, 0), memory_space=pltpu.SMEM)
    return pl.pallas_call(
        functools.partial(_dispatch_kernel, nsub=nsub, n_tiles_max=n_rows // FFN_TM),
        grid_spec=pltpu.PrefetchScalarGridSpec(
            num_scalar_prefetch=4, grid=(nsub,),
            in_specs=[tab, tab, pl.BlockSpec((1, TOP_K, SUB), lambda s, *_: (s, 0, 0)),
                      pl.BlockSpec((SUB, XG_W), lambda s, *_: (s, 0))],
            out_specs=pl.BlockSpec(memory_space=pl.ANY),
            scratch_shapes=[pltpu.VMEM((N_CHUNKS + 1, CHUNK, XG_W), BF16),
                            pltpu.VMEM((FFN_TM // CHUNK, CHUNK, XG_W), BF16),
                            pltpu.SemaphoreType.DMA(()), pltpu.SemaphoreType.DMA(())]),
        out_shape=jax.ShapeDtypeStruct((n_rows // CHUNK, CHUNK, XG_W), BF16),
        compiler_params=_params(("arbitrary",)),
        name="moe_dispatch",
    )(nblk, n_real, ncopy, n_tiles, src, dst, posr, xg).reshape(n_rows, XG_W)


def _ffn_kernel(te_ref, nt_ref, xs_ref, wg_ref, wu_ref, wd_ref,
                q_ref, kc_ref, vc_ref, nk_ref, nv_ref, b1_ref, b2_ref,
                ys_ref, ko_ref, vo_ref, o_ref, l_ref, wg_s, wu_s, wd_s, *, steps_per_batch):
    i = pl.program_id(0)
    n_heads = HEADS_PER_GROUP // steps_per_batch
    first_head = (i % steps_per_batch) * n_heads

    def ride():
        _attn_sample_batch(i // steps_per_batch, 0, q_ref, kc_ref, vc_ref, nk_ref, nv_ref, b1_ref, b2_ref,
                           ko_ref, vo_ref, o_ref, l_ref, heads=[first_head + j for j in range(n_heads)])

    @pl.when(i >= nt_ref[0])
    def _():
        ride()
        ys_ref[...] = jnp.zeros((FFN_TM, D_MODEL), BF16)

    @pl.when(i < nt_ref[0])
    def _():
        e = te_ref[i]

        @pl.when((i == 0) | (e != te_ref[jnp.maximum(i - 1, 0)]))
        def _():
            wg_s[...] = wg_ref[0].astype(BF16)
            wu_s[...] = wu_ref[0].astype(BF16)
            wd_s[...] = wd_ref[0].astype(BF16)

        ride()
        x = xs_ref[:, 0:D_MODEL]
        g = xs_ref[:, D_MODEL:XG_W].astype(F32)
        lane = lax.broadcasted_iota(jnp.int32, (FFN_TM, LANES), 1)
        gcol = jnp.sum(jnp.where((lane == e) | (lane == e + N_EXPERTS), g, 0.0), axis=1, keepdims=True)
        a = _dot(x, wg_s[...])
        b = _dot(x, wu_s[...])
        hc = _silu(a) * b * gcol
        ys_ref[...] = _dot(hc.astype(BF16), wd_s[...]).astype(BF16)


def _ffn(tile_expert, n_tiles, xs, wg, wu, wd, ride):
    q, kc, vc, nk, nv, b1, b2 = ride
    nb, _, _, wb = kc.shape
    n_rows = xs.shape[0]
    n_steps = n_rows // FFN_TM
    assert n_steps % nb == 0 and HEADS_PER_GROUP % (n_steps // nb) == 0
    per = n_steps // nb
    row = lambda i, te, nt: (jnp.minimum(i, nt[0] - 1), 0)
    wspec = lambda shape: pl.BlockSpec((1,) + shape, lambda i, te, nt: (te[i], 0, 0))
    rblk = lambda *tail: pl.BlockSpec((1, HEADS_PER_GROUP) + tail, lambda i, te, nt: (i // per, 0, 0, 0))
    const = lambda shape: pl.BlockSpec(shape, lambda i, te, nt: (0, 0, 0))
    return pl.pallas_call(
        functools.partial(_ffn_kernel, steps_per_batch=per),
        grid_spec=pltpu.PrefetchScalarGridSpec(
            num_scalar_prefetch=2, grid=(n_steps,),
            in_specs=[pl.BlockSpec((FFN_TM, XG_W), row), wspec((D_MODEL, D_EXPERT)),
                      wspec((D_MODEL, D_EXPERT)), wspec((D_EXPERT, D_MODEL)),
                      rblk(8, HEAD_DIM), rblk(HEAD_DIM, wb), rblk(HEAD_DIM, wb), const(nk.shape), const(nv.shape),
                      const(b1.shape), const(b2.shape)],
            out_specs=[pl.BlockSpec((FFN_TM, D_MODEL), lambda i, te, nt: (i, 0)),
                       rblk(HEAD_DIM, wb), rblk(HEAD_DIM, wb), rblk(8, HEAD_DIM), rblk(8, LANES)],
            scratch_shapes=[pltpu.VMEM((D_MODEL, D_EXPERT), BF16), pltpu.VMEM((D_MODEL, D_EXPERT), BF16),
                            pltpu.VMEM((D_EXPERT, D_MODEL), BF16)]),
        out_shape=[jax.ShapeDtypeStruct((n_rows, D_MODEL), BF16),
                   jax.ShapeDtypeStruct(kc.shape, F32), jax.ShapeDtypeStruct(kc.shape, F32),
                   jax.ShapeDtypeStruct((nb, HEADS_PER_GROUP, 8, HEAD_DIM), F32),
                   jax.ShapeDtypeStruct((nb, HEADS_PER_GROUP, 8, LANES), F32)],
        compiler_params=_params(("arbitrary",)),
        name="moe_ffn",
    )(tile_expert, n_tiles, xs, wg, wu, wd, q, kc, vc, nk, nv, b1, b2)


def _combine_kernel(nblk_ref, ncopy_ref, src_ref, post_ref, xg_ref, x1_ref, g2_ref, wgs_ref, wus_ref,
                    wds_ref, ys_hbm, y_ref, ys_scr, wgs_s, wus_s, wds_s, sem):
    s = pl.program_id(0)
    n = ncopy_ref[s]
    blk_chunks = MXU_ROWS // CHUNK
    trip_chunks = BLOCKS_PER_TRIP * blk_chunks

    def fetch_trip(t, carry):
        k0 = t * trip_chunks
        _for_each(trip_chunks,
                  lambda u: _chunk_copy(ys_hbm, src_ref[0, 0, k0 + u], ys_scr, k0 + u, sem.at[t]).start())
        return carry

    lax.fori_loop(0, nblk_ref[s], fetch_trip, 0)

    @pl.when(s == 0)
    def _():
        wgs_s[...] = wgs_ref[...].astype(BF16)
        wus_s[...] = wus_ref[...].astype(BF16)
        wds_s[...] = wds_ref[...].astype(BF16)

    h = xg_ref[:, 0:D_MODEL]
    a = _dot(h, wgs_s[...])
    b = _dot(h, wus_s[...])
    shared = _dot((_silu(a) * b).astype(BF16), wds_s[...])

    post = post_ref[...]
    pos = [jnp.broadcast_to(post[:, k:k + 1], (SUB, MXU_ROWS)) for k in range(TOP_K)]
    lanes = lax.broadcasted_iota(jnp.int32, (SUB, MXU_ROWS), 1).astype(BF16)

    def gather_blocks(i, acc):
        _for_each(trip_chunks, lambda t: _chunk_copy(ys_hbm, 0, ys_scr, 0, sem.at[i]).wait())
        for j in range(BLOCKS_PER_TRIP):
            blk = i * BLOCKS_PER_TRIP + j
            onehot = _slot_onehot((blk * MXU_ROWS).astype(F32), pos, lanes)
            rows = ys_scr[pl.ds(pl.multiple_of(blk * blk_chunks, blk_chunks), blk_chunks)]
            acc = acc + _dot(onehot, rows.reshape(MXU_ROWS, D_MODEL))
        return acc

    routed = lax.fori_loop(0, nblk_ref[s], gather_blocks, jnp.zeros((SUB, D_MODEL), F32))
    y_ref[...] = x1_ref[...] + g2_ref[0] * (routed + shared)


def _combine(nblk, ncopy, src, post, xg, x1, g2, wgs, wus, wds, ys):
    n_tok = x1.shape[0]
    nsub = n_tok // SUB
    per_b = nsub // g2.shape[0]
    tab = pl.BlockSpec((1, 1, TABLE_W), lambda s, *_: (s, 0, 0), memory_space=pltpu.SMEM)
    const2 = lambda shape: pl.BlockSpec(shape, lambda s, *_: (0, 0))
    return pl.pallas_call(
        _combine_kernel,
        grid_spec=pltpu.PrefetchScalarGridSpec(
            num_scalar_prefetch=2, grid=(nsub,),
            in_specs=[tab, pl.BlockSpec((SUB, LANES), lambda s, *_: (s, 0)),
                      pl.BlockSpec((SUB, XG_W), lambda s, *_: (s, 0)),
                      pl.BlockSpec((SUB, D_MODEL), lambda s, *_: (s, 0)),
                      pl.BlockSpec((1, 1, D_MODEL), lambda s, *_: (s // per_b, 0, 0)),
                      const2((D_MODEL, D_EXPERT)), const2((D_MODEL, D_EXPERT)), const2((D_EXPERT, D_MODEL)),
                      pl.BlockSpec(memory_space=pl.ANY)],
            out_specs=pl.BlockSpec((SUB, D_MODEL), lambda s, *_: (s, 0)),
            scratch_shapes=[pltpu.VMEM((N_CHUNKS, CHUNK, D_MODEL), BF16), pltpu.VMEM((D_MODEL, D_EXPERT), BF16),
                            pltpu.VMEM((D_MODEL, D_EXPERT), BF16), pltpu.VMEM((D_EXPERT, D_MODEL), BF16),
                            pltpu.SemaphoreType.DMA((SUB_ROWS // (BLOCKS_PER_TRIP * MXU_ROWS),))]),
        out_shape=jax.ShapeDtypeStruct((n_tok, D_MODEL), F32),
        compiler_params=_params(("arbitrary",)),
        name="moe_combine",
    )(nblk, ncopy, src, post, xg, x1, g2, wgs, wus, wds, ys.reshape(-1, CHUNK, D_MODEL))


def _route_tables(cnt):
    nsub = cnt.shape[0]
    i32 = jnp.int32
    seg_end = jnp.cumsum(cnt, axis=1)
    seg_start = seg_end - cnt
    tot = jnp.sum(cnt, axis=0)
    tot_al = (tot + FFN_TM - 1) // FFN_TM * FFN_TM
    reg_end = jnp.cumsum(tot_al)
    reg_start = reg_end - tot_al
    base = reg_start[None, :] + jnp.cumsum(cnt, axis=0) - cnt
    n_real = seg_end[:, -1] // CHUNK
    k = jnp.arange(TABLE_W, dtype=i32)
    owner = jnp.sum((seg_end[:, None, :] // CHUNK) <= k[None, :, None], axis=2)
    owner = jnp.minimum(owner, N_EXPERTS - 1)
    delta = (base - seg_start) // CHUNK
    e_ids = jnp.arange(N_EXPERTS, dtype=i32)
    real_dst = jnp.sum(jnp.where(owner[:, :, None] == e_ids, delta[:, None, :], 0), axis=2) + k[None, :]
    fill_n = (tot_al - tot) // CHUNK
    fill_dst0 = (reg_start + tot) // CHUNK
    src = jnp.broadcast_to(k[None, :], (nsub, TABLE_W))
    dst = real_dst
    ncopy = n_real
    for r in range(-(-N_EXPERTS // nsub)):
        e_of = jnp.arange(nsub, dtype=i32) + r * nsub
        ok = e_of < N_EXPERTS
        e_cl = jnp.minimum(e_of, N_EXPERTS - 1)
        fn = jnp.where(ok, fill_n[e_cl], 0)
        j = k[None, :] - ncopy[:, None]
        is_fill = (j >= 0) & (j < fn[:, None])
        src = jnp.where(is_fill, ZERO_CHUNK, src)
        dst = jnp.where(is_fill, fill_dst0[e_cl][:, None] + j, dst)
        ncopy = ncopy + fn
    trip_rows = BLOCKS_PER_TRIP * MXU_ROWS
    disp_nblk = (n_real * CHUNK + trip_rows - 1) // trip_rows
    comb_n = disp_nblk * (trip_rows // CHUNK)
    comb_src = jnp.where(k[None, :] < n_real[:, None], real_dst, real_dst[:, 0:1])
    n_tiles = reg_end[-1] // FFN_TM
    t = jnp.arange((nsub * SUB_ROWS + N_EXPERTS * FFN_TM) // FFN_TM, dtype=i32)
    tile_expert = jnp.minimum(jnp.sum((reg_end[None, :] // FFN_TM) <= t[:, None], axis=1), N_EXPERTS - 1)
    as3 = lambda a: a.astype(i32).reshape(nsub, 1, TABLE_W)
    return (disp_nblk.astype(i32), n_real.astype(i32), ncopy.astype(i32), as3(src), as3(dst),
            comb_n.astype(i32), as3(comb_src), tile_expert.astype(i32), n_tiles.astype(i32).reshape(1))


def _moe_sparse(xg, post, posr, cnt, x1, g2, wg, wu, wd, wgs, wus, wds, ride):
    B, S, _ = x1.shape
    n_tok = B * S
    nsub = n_tok // SUB
    (disp_nblk, n_real, disp_n, disp_src, disp_dst, comb_n, comb_src, tile_expert,
     n_tiles) = _route_tables(cnt[:, :, 0].astype(jnp.int32))
    n_rows = nsub * SUB_ROWS + N_EXPERTS * FFN_TM
    xg2 = xg.reshape(n_tok, XG_W)
    xs = _dispatch(disp_nblk, n_real, disp_n, n_tiles, disp_src, disp_dst, posr, xg2, n_rows)
    ys, *ride_out = _ffn(tile_expert, n_tiles, xs, wg, wu, wd, ride)
    y = _combine(disp_nblk, comb_n, comb_src, post.reshape(n_tok, LANES), xg2, x1.reshape(n_tok, D_MODEL),
                 g2, wgs, wus, wds, ys)
    return y.reshape(B, S, D_MODEL), ride_out


def _t5_bucket(dist):
    max_exact = N_BUCKETS // 2
    df = jnp.maximum(dist, 1).astype(F32)
    large = max_exact + (jnp.log(df / max_exact) / math.log(MAX_DISTANCE / max_exact)
                         * (N_BUCKETS - max_exact)).astype(jnp.int32)
    large = jnp.minimum(large, N_BUCKETS - 1)
    return jnp.where(dist < max_exact, dist, large)


def _step_bias(rel_bias, g):
    dist = jnp.arange(N_KEYS, dtype=jnp.int32) * DILATIONS[g]
    cols = rel_bias[:, g * HEADS_PER_GROUP:(g + 1) * HEADS_PER_GROUP]
    return cols[_t5_bucket(dist)].T.astype(F32)


def _prompt_bias(bias_k):
    h = bias_k.shape[0]
    n = 3 * BLK
    row = jnp.concatenate([bias_k[:, ::-1], jnp.full((h, n - N_KEYS), NEG, F32)], axis=1)
    t = jnp.tile(row, (1, BLK))[:, :BLK * (n - 1)].reshape(h, BLK, n - 1)
    return t[:, :, :2 * BLK]


def _sample_bias(bias_k, wb, d):
    h = bias_k.shape[0]
    rev = bias_k[:, :0:-1]
    rows = []
    for t in range(4):
        if d == 1:
            rows.append(jnp.concatenate([jnp.full((h, t), NEG, F32), rev[:, :wb - t]], axis=1))
        else:
            r = jnp.arange(d)[None, None, :]
            rows.append(jnp.where(r == t, rev[:, :, None], NEG).reshape(h, wb))
    b1 = jnp.concatenate([jnp.stack(rows, axis=1), jnp.zeros((h, 4, wb), F32)], axis=1)
    rows = []
    for t in range(4):
        cols = [bias_k[:, (t - tn) // d] if (t >= tn and (t - tn) % d == 0) else jnp.full((h,), NEG, F32)
                for tn in range(4)]
        rows.append(jnp.concatenate([jnp.full((h, LANES - 4), NEG, F32), jnp.stack(cols, axis=1)], axis=1))
    b2 = jnp.concatenate([jnp.stack(rows, axis=1), jnp.zeros((h, 4, LANES), F32)], axis=1)
    return b1, b2


TILE = 512
RIDE_ON_OPROJ, RIDE_ON_PROJ, RIDE_ON_FFN = 0, 1, 2


def _sample_front(x, mods, hist, caches, rel_bias, weights):
    norm1_g, _, w_in_bf, qg, kg, conv_w, bd = weights[:7]
    sh1, sc1 = mods[:2]
    S = x.shape[1]
    nb = S // 4
    qf, kt, vt, u, co = _proj(x, sc1, sh1, norm1_g, w_in_bf, qg, kg, conv_w, bd, hist,
                              tile=TILE, dils=(1, 1, 1), sample=True)
    rides = []
    for g in range(N_GROUPS):
        d, wb = DILATIONS[g], WINDOWS[g]
        kc, vc = caches[g]
        b1, b2 = _sample_bias(_step_bias(rel_bias, g), wb, d)
        q = qf[0, :, g * GROUP_W:(g + 1) * GROUP_W].reshape(nb, 4, HEADS_PER_GROUP, HEAD_DIM)
        q = jnp.pad(q.transpose(0, 2, 1, 3), ((0, 0), (0, 0), (0, 4), (0, 0))).astype(BF16)
        new_t = lambda a: a[g * GROUP_W:(g + 1) * GROUP_W].reshape(HEADS_PER_GROUP, HEAD_DIM, S)
        rides.append((q, kc, vc, new_t(kt), new_t(vt), b1, b2))
    return dict(x=x, mods=mods, u=u, co=co), rides


def _sample_back(front, ride_outs, weights):
    (_, norm2_g, _, _, _, _, _, w_o_bf, w_r_t, b_r, w_gate_e, w_up_e, w_down_e,
     w_gate_s, w_up_s, w_down_s) = weights
    x, u, co = front["x"], front["u"], front["co"]
    _, _, g1, sh2, sc2, g2 = front["mods"]
    S = x.shape[1]
    nb = S // 4
    states, o_list, l_list = [], [], []
    for g in range(N_GROUPS):
        ko, vo, o, lse = ride_outs[g]
        states += [ko.transpose(0, 3, 1, 2)[None], vo.transpose(0, 3, 1, 2)[None]]
        o_list.append(o[:, :, :4].transpose(0, 2, 1, 3).reshape(1, 1, S, GROUP_W))
        lse = jnp.broadcast_to(lse[:, :, :4, :1], (nb, HEADS_PER_GROUP, 4, HEAD_DIM))
        l_list.append(lse.transpose(0, 2, 1, 3).reshape(1, 1, S, GROUP_W))
    states.append(u.reshape(nb, 4, CONV_CH)[:, 2:][None])
    x1, xg, gates, _, _, _ = _oproj(x, o_list, l_list, co, g1, sc2, sh2, norm2_g, w_o_bf, w_r_t, b_r,
                                    tile=TILE, dils=(1, 1, 1))
    y = _moe(xg, gates, x1, g2, w_gate_e, w_up_e, w_down_e, w_gate_s, w_up_s, w_down_s, tile=TILE)
    return y, states


def _prompt_layer(x, mods, rel_bias, weights, rides):
    (norm1_g, norm2_g, w_in_bf, qg, kg, conv_w, bd, w_o_bf, w_r_t, b_r, w_gate_e, w_up_e, w_down_e,
     w_gate_s, w_up_s, w_down_s) = weights
    sh1, sc1, g1, sh2, sc2, g2 = mods
    B, S, _ = x.shape
    ride_outs = [None] * N_GROUPS
    res = _proj(x, sc1, sh1, norm1_g, w_in_bf, qg, kg, conv_w, bd, None, tile=TILE, dils=DILATIONS, sample=False,
                ride=rides[RIDE_ON_PROJ])
    qkv, tails, (u, co), ride_outs[RIDE_ON_PROJ] = res[:9], res[9:15], res[15:17], res[17:]
    states, o_list, l_list = [], [], []
    for g in range(N_GROUPS):
        d, w = DILATIONS[g], WINDOWS[g]
        q, k, v = (a.reshape(B * d, S // d, GROUP_W) for a in qkv[3 * g:3 * g + 3])
        o, lse = _attn_prompt(q, k, v, _prompt_bias(_step_bias(rel_bias, g)))
        o_list.append(o.reshape(B, d, S // d, GROUP_W))
        l_list.append(lse.reshape(B, d, S // d, GROUP_W))
        keep = lambda a: a.reshape(B, HEADS_PER_GROUP, HEAD_DIM, w).transpose(0, 3, 1, 2)[None]
        states += [keep(tails[2 * g]), keep(tails[2 * g + 1])]
    states.append(u[:, 6:][None])
    x1, xg, _, posr, post, cnt, *ride_outs[RIDE_ON_OPROJ] = _oproj(
        x, o_list, l_list, co, g1, sc2, sh2, norm2_g, w_o_bf, w_r_t, b_r, tile=TILE, dils=DILATIONS,
        ride=rides[RIDE_ON_OPROJ])
    y, ride_outs[RIDE_ON_FFN] = _moe_sparse(xg, post, posr, cnt, x1, g2, w_gate_e, w_up_e, w_down_e, w_gate_s,
                                            w_up_s, w_down_s, rides[RIDE_ON_FFN])
    return y, states, ride_outs


def kernel(x_prompt, x_sample, c_prompt, c_sample, cache_k_w128, cache_v_w128, cache_k_w512, cache_v_w512, cache_k_w2048, cache_v_w2048, state_conv, rel_bias, norm1_g, norm2_g, w_ada, b_ada, w_in, q_norm_g, k_norm_g, conv_w, w_o, w_router, b_router, w_gate_e, w_up_e, w_down_e, w_gate_s, w_up_s, w_down_s):
    B = x_prompt.shape[0]
    DB, T = x_sample.shape[:2]
    n_c = B + DB
    c_all = jnp.pad(jnp.concatenate([c_prompt, c_sample], axis=0), ((0, (-n_c) % 8), (0, 0)))
    mod = _ada(c_all, w_ada[0], b_ada)
    chunks = [mod[:, i * D_MODEL:(i + 1) * D_MODEL] for i in range(6)]
    mods_p = [c[:B].reshape(B, 1, D_MODEL) for c in chunks]
    mods_s = [jnp.repeat(c[B:n_c], T, axis=0).reshape(1, DB * T, D_MODEL) for c in chunks]

    eye = jnp.arange(GROUP_W) // HEAD_DIM
    bd = jnp.where(eye[:, None] == eye[None, :], 1.0 / HEAD_DIM, 0.0).astype(BF16)
    weights = (norm1_g, norm2_g, w_in[0].astype(BF16),
               jnp.tile(q_norm_g, (1, HEADS_PER_GROUP)), jnp.tile(k_norm_g, (1, HEADS_PER_GROUP)),
               conv_w[0], bd, w_o[0].astype(BF16), w_router[0].T, b_router.reshape(N_EXPERTS, 1),
               w_gate_e[0], w_up_e[0], w_down_e[0], w_gate_s[0], w_up_s[0], w_down_s[0])

    s0, s1 = state_conv[0, :, 0], state_conv[0, :, 1]
    zero = jnp.zeros_like(s0)
    hist_a = jnp.stack([s0, s1, zero, zero], axis=1).reshape(DB * T, CONV_CH)
    hist_b = jnp.stack([s1, zero, zero, zero], axis=1).reshape(DB * T, CONV_CH)
    caches = [(ck[0].transpose(0, 2, 3, 1), cv[0].transpose(0, 2, 3, 1))
              for ck, cv in ((cache_k_w128, cache_v_w128), (cache_k_w512, cache_v_w512),
                             (cache_k_w2048, cache_v_w2048))]
    front, ride = _sample_front(x_sample.reshape(1, DB * T, D_MODEL), mods_s, (hist_a, hist_b), caches,
                                rel_bias, weights)
    yp, st_p, ride_out = _prompt_layer(x_prompt, mods_p, rel_bias, weights, ride)
    ys, st_s = _sample_back(front, ride_out, weights)
    return (yp, ys.reshape(DB, T, D_MODEL), *st_p, *st_s)
```

```python
import functools
import math

import jax
import jax.numpy as jnp
from jax import lax
from jax.experimental import pallas as pl
from jax.experimental.pallas import tpu as pltpu

D_MODEL = 1024
HEAD_DIM = 64
HEADS_PER_GROUP = 4
GROUP_W = HEADS_PER_GROUP * HEAD_DIM
WINDOWS = (128, 512, 2048)
DILATIONS = (1, 4, 16)
N_GROUPS = 3
ATTN_W = N_GROUPS * GROUP_W
CONV_CH = 256
N_PROJ = 3 * ATTN_W + 3 * CONV_CH
N_STEPS = 128
N_KEYS = N_STEPS + 1
BLK = 128
N_BUCKETS = 32
MAX_DISTANCE = WINDOWS[-1]
N_EXPERTS = 64
TOP_K = 8
N_ROUTE_GROUPS = 8
GROUP_SIZE = N_EXPERTS // N_ROUTE_GROUPS
TOPK_GROUPS = 4
D_EXPERT = 256
ROUTED_SCALE = 2.5
RMS_EPS = 1e-6
NEG = -1e30
LANES = 128
SUB = 256
CHUNK = 16
XG_W = D_MODEL + LANES
VMEM_LIMIT = 56 * 1024 * 1024

F32 = jnp.float32
BF16 = jnp.bfloat16


def _dot(a, b):
    return jnp.dot(a, b, preferred_element_type=F32)


def _dot_nt(a, b):
    return lax.dot_general(a, b, (((1,), (1,)), ((), ())), preferred_element_type=F32)


def _silu(a):
    return a / (1.0 + jnp.exp(-a))


def _params(sem):
    return pltpu.CompilerParams(dimension_semantics=sem, vmem_limit_bytes=VMEM_LIMIT)


def _ada_kernel(c_ref, w_ref, b_ref, o_ref):
    s = _silu(c_ref[...])
    o_ref[...] = _dot(s.astype(BF16), w_ref[...].astype(BF16)) + b_ref[...]


def _ada(c, w_ada, b_ada):
    n = c.shape[0]
    nc = w_ada.shape[1] // D_MODEL
    return pl.pallas_call(
        _ada_kernel,
        grid=(nc,),
        in_specs=[pl.BlockSpec((n, D_MODEL), lambda j: (0, 0)),
                  pl.BlockSpec((D_MODEL, D_MODEL), lambda j: (0, j)),
                  pl.BlockSpec((1, D_MODEL), lambda j: (0, j))],
        out_specs=pl.BlockSpec((n, D_MODEL), lambda j: (0, j)),
        out_shape=jax.ShapeDtypeStruct((n, w_ada.shape[1]), F32),
        compiler_params=_params(("arbitrary",)),
        name="ada",
    )(c, w_ada, b_ada)


def _proj_kernel(*refs, tile, dils, sample, nt):
    if sample:
        (x_ref, sc_ref, sh_ref, ng_ref, w_ref, qg_ref, kg_ref, cw_ref, bd_ref, ha_ref, hb_ref,
         qf_ref, kt_ref, vt_ref, u_ref, co_ref, u_scr) = refs
    else:
        (x_ref, sc_ref, sh_ref, ng_ref, w_ref, qg_ref, kg_ref, cw_ref, bd_ref) = refs[:9]
        ride_in, refs = refs[9:9 + N_RIDE_IN], refs[9 + N_RIDE_IN:]
        (q0, k0, v0, q1, k1, v1, q2, k2, v2, kt0, vt0, kt1, vt1, kt2, vt2, u_ref, co_ref) = refs[:17]
        ride_out, (slab, u_scr) = refs[17:17 + N_RIDE_OUT], refs[17 + N_RIDE_OUT:]
        qkv_out = ((q0, k0, v0), (q1, k1, v1), (q2, k2, v2))
        tails = ((kt0, vt0), (kt1, vt1), (kt2, vt2))
        j = pl.program_id(1)
        _ride_block(pl.program_id(0) * nt + j, ride_in + ride_out)

    x = x_ref[0]
    ms = jnp.mean(x * x, axis=-1, keepdims=True)
    h = x * lax.rsqrt(ms + RMS_EPS) * ng_ref[...]
    h = h * (1.0 + sc_ref[0]) + sh_ref[0]
    proj = _dot(h.astype(BF16), w_ref[...])

    bd = bd_ref[...]

    def headnorm(z, g):
        msq = _dot((z * z).astype(BF16), bd)
        return z * lax.rsqrt(msq + RMS_EPS) * g

    slab_i = 0
    for g in range(N_GROUPS):
        c0 = g * GROUP_W
        qn = headnorm(proj[:, c0:c0 + GROUP_W], qg_ref[...]) * (HEAD_DIM ** -0.5)
        kn = headnorm(proj[:, ATTN_W + c0:ATTN_W + c0 + GROUP_W], kg_ref[...])
        vv = proj[:, 2 * ATTN_W + c0:2 * ATTN_W + c0 + GROUP_W]
        if sample:
            qf_ref[0, :, c0:c0 + GROUP_W] = qn
            kt_ref[c0:c0 + GROUP_W, :] = kn.T
            vt_ref[c0:c0 + GROUP_W, :] = vv.T
            continue
        keep = min(WINDOWS[g], tile)

        @pl.when(j >= nt - max(WINDOWS[g] // tile, 1))
        def _(kn=kn, vv=vv, g=g, keep=keep):
            tails[g][0][0] = kn[tile - keep:, :].T
            tails[g][1][0] = vv[tile - keep:, :].T

        d = dils[g]
        for val, out in zip((qn, kn, vv), qkv_out[g]):
            if d == 1:
                out[0, 0] = val.astype(BF16)
                continue
            n = tile // d
            for half in range(GROUP_W // LANES):
                slab[slab_i] = val[:, half * LANES:(half + 1) * LANES]
                for r in range(d):
                    out[0, r, :, half * LANES:(half + 1) * LANES] = (
                        slab[slab_i, pl.ds(r, n, stride=d), :].astype(BF16))
                slab_i += 1

    base = 3 * ATTN_W
    u = proj[:, base + 2 * CONV_CH:base + 3 * CONV_CH] * proj[:, base:base + CONV_CH]
    gate_b = proj[:, base + CONV_CH:base + 2 * CONV_CH]
    if sample:
        u_scr[0:8, :] = jnp.zeros((8, CONV_CH), F32)
    else:
        @pl.when(j == 0)
        def _():
            u_scr[0:8, :] = jnp.zeros((8, CONV_CH), F32)

        @pl.when(j > 0)
        def _():
            u_scr[0:8, :] = u_scr[tile:tile + 8, :]

    u_scr[8:tile + 8, :] = u
    um1 = u_scr[7:tile + 7, :]
    um2 = u_scr[6:tile + 6, :]
    if sample:
        t = lax.broadcasted_iota(jnp.int32, (tile, CONV_CH), 0) % 4
        um1 = jnp.where(t >= 1, um1, 0.0) + hb_ref[...]
        um2 = jnp.where(t >= 2, um2, 0.0) + ha_ref[...]
    cw = cw_ref[...]
    conv = cw[0:1] * um2 + cw[1:2] * um1 + cw[2:3] * u
    u_ref[0] = u if sample else u[tile - 8:, :]
    co_ref[0] = (gate_b * conv).astype(BF16)


def _proj(x, sc, sh, ng, w_bf, qg, kg, cw, bd, hist, *, tile, dils, sample, ride=None):
    B, S, _ = x.shape
    nt = S // tile
    tm = sc.shape[1]
    mod_spec = pl.BlockSpec((1, tm, D_MODEL), (lambda b, j: (b, j, 0)) if tm > 1 else (lambda b, j: (b, 0, 0)))
    const2 = lambda shape: pl.BlockSpec(shape, lambda b, j: (0, 0))
    in_specs = [pl.BlockSpec((1, tile, D_MODEL), lambda b, j: (b, j, 0)), mod_spec, mod_spec,
                const2((1, D_MODEL)), const2((D_MODEL, N_PROJ)), const2((1, GROUP_W)),
                const2((1, GROUP_W)), const2((3, CONV_CH)), const2((GROUP_W, GROUP_W))]
    args = [x, sc, sh, ng, w_bf, qg, kg, cw, bd]
    nat = lambda w: pl.BlockSpec((1, tile, w), lambda b, j: (b, j, 0))
    out_specs, out_shape = [], []
    scratch = []
    if sample:
        assert B == 1 and nt == 1
        in_specs += [pl.BlockSpec((tile, CONV_CH), lambda b, j: (j, 0))] * 2
        args += list(hist)
        out_specs += [nat(ATTN_W)] + [pl.BlockSpec((ATTN_W, tile), lambda b, j: (0, j))] * 2 + [nat(CONV_CH)]
        out_shape += [jax.ShapeDtypeStruct((B, S, ATTN_W), F32)]
        out_shape += [jax.ShapeDtypeStruct((ATTN_W, S), F32)] * 2
        out_shape += [jax.ShapeDtypeStruct((B, S, CONV_CH), F32)]
    else:
        for d in dils:
            for _ in range(3):
                out_specs.append(pl.BlockSpec((1, d, tile // d, GROUP_W), lambda b, j: (b, 0, j, 0)))
                out_shape.append(jax.ShapeDtypeStruct((B, d, S // d, GROUP_W), BF16))
        for w in WINDOWS:
            keep, first = min(w, tile), nt - max(w // tile, 1)
            for _ in range(2):
                out_specs.append(pl.BlockSpec((1, GROUP_W, keep),
                                              lambda b, j, first=first: (b, 0, jnp.maximum(j - first, 0))))
                out_shape.append(jax.ShapeDtypeStruct((B, GROUP_W, w), F32))
        out_specs.append(pl.BlockSpec((1, 8, CONV_CH), lambda b, j: (b, 0, 0)))
        out_shape.append(jax.ShapeDtypeStruct((B, 8, CONV_CH), F32))
        n_slabs = sum(3 * (GROUP_W // LANES) for d in dils if d > 1)
        scratch.append(pltpu.VMEM((n_slabs, tile, LANES), F32))
    out_specs.append(nat(CONV_CH))
    out_shape.append(jax.ShapeDtypeStruct((B, S, CONV_CH), BF16))
    scratch.append(pltpu.VMEM((tile + 8, CONV_CH), F32))
    if not sample:
        nb = ride[1].shape[0]
        r_in, r_out, r_shape = _ride_specs(ride, nb // (B * nt), lambda b, j: b * nt + j)
        in_specs, args = in_specs + r_in, args + list(ride)
        out_specs, out_shape = out_specs + r_out, out_shape + r_shape
    return pl.pallas_call(
        functools.partial(_proj_kernel, tile=tile, dils=dils, sample=sample, nt=nt),
        grid=(B, nt),
        in_specs=in_specs,
        out_specs=out_specs,
        out_shape=out_shape,
        scratch_shapes=scratch,
        compiler_params=_params(("arbitrary", "arbitrary")),
        name="proj_sample" if sample else "proj_prompt",
    )(*args)


def _attn_kernel(q_ref, kp_ref, kc_ref, vp_ref, vc_ref, b_ref, o_ref, l_ref, s_scr, m_scr, o_scr, z_scr,
                 *, nq):
    first = pl.program_id(1) == 0
    low = lax.broadcasted_iota(jnp.int32, (BLK, LANES), 1) < HEAD_DIM
    ones = jnp.ones((BLK, LANES), BF16)
    chains = [(j, pair, sub) for j in range(nq) for pair in range(GROUP_W // LANES) for sub in range(2)]

    def operands(j, pair):
        rows = slice(j * BLK, (j + 1) * BLK)
        sl = slice(pair * LANES, (pair + 1) * LANES)
        if j == 0:
            return rows, sl, kp_ref[0, :, sl], vp_ref[0, :, sl]
        prows = slice((j - 1) * BLK, j * BLK)
        return rows, sl, kc_ref[0, prows, sl], vc_ref[0, prows, sl]

    for c, (j, pair, sub) in enumerate(chains):
        rows, sl, kp, _ = operands(j, pair)
        q = q_ref[0, rows, sl]
        qm = jnp.where(low if sub == 0 else ~low, q, jnp.zeros_like(q))
        hh = 2 * pair + sub
        sp = _dot_nt(qm, kp) + b_ref[hh, :, 0:BLK]
        if j == 0:
            sp = jnp.where(first, NEG, sp)
        sc = _dot_nt(qm, kc_ref[0, rows, sl]) + b_ref[hh, :, BLK:2 * BLK]
        s_scr[c, :, 0:BLK] = sp
        s_scr[c, :, BLK:2 * BLK] = sc
        m_scr[c] = jnp.broadcast_to(jnp.max(jnp.maximum(sp, sc), axis=-1, keepdims=True), (BLK, LANES))

    for c, (j, pair, sub) in enumerate(chains):
        rows, sl, _, vp = operands(j, pair)
        m = m_scr[c]
        pp = jnp.exp(s_scr[c, :, 0:BLK] - m).astype(BF16)
        pc = jnp.exp(s_scr[c, :, BLK:2 * BLK] - m).astype(BF16)
        o_scr[c] = _dot(pp, vp) + _dot(pc, vc_ref[0, rows, sl])
        z_scr[c] = _dot(pp, ones) + _dot(pc, ones)

    for c in range(0, len(chains), 2):
        j, pair, _ = chains[c]
        rows = slice(j * BLK, (j + 1) * BLK)
        sl = slice(pair * LANES, (pair + 1) * LANES)
        o_sub = [o_scr[c + sub] / z_scr[c + sub] for sub in range(2)]
        l_sub = [m_scr[c + sub] + jnp.log(z_scr[c + sub]) for sub in range(2)]
        o_ref[0, rows, sl] = jnp.where(low, o_sub[0], o_sub[1])
        l_ref[0, rows, sl] = jnp.where(low, l_sub[0], l_sub[1])


def _attn_prompt(q, k, v, bias):
    Z, L, _ = q.shape
    qt = min(L, 8 * BLK)
    nq = qt // BLK
    cur = pl.BlockSpec((1, qt, GROUP_W), lambda z, i: (z, i, 0))
    prev = pl.BlockSpec((1, BLK, GROUP_W), lambda z, i: (z, jnp.maximum(i * nq - 1, 0), 0))
    return pl.pallas_call(
        functools.partial(_attn_kernel, nq=nq),
        grid=(Z, L // qt),
        in_specs=[cur, prev, cur, prev, cur,
                  pl.BlockSpec((HEADS_PER_GROUP, BLK, 2 * BLK), lambda z, i: (0, 0, 0))],
        out_specs=[cur, cur],
        out_shape=[jax.ShapeDtypeStruct((Z, L, GROUP_W), F32)] * 2,
        scratch_shapes=[pltpu.VMEM((nq * HEADS_PER_GROUP, BLK, 2 * BLK), F32)]
        + [pltpu.VMEM((nq * HEADS_PER_GROUP, BLK, LANES), F32)] * 3,
        compiler_params=_params(("arbitrary", "arbitrary")),
        name="attn_prompt",
    )(q, k, k, v, v, bias)


def _attn_sample_batch(gb, b, q_ref, kc_ref, vc_ref, nk_ref, nv_ref, b1_ref, b2_ref, ko_ref, vo_ref, o_ref,
                       l_ref, heads=range(HEADS_PER_GROUP)):
    wb = kc_ref.shape[-1]
    lane = lax.broadcasted_iota(jnp.int32, (HEAD_DIM, LANES), 1)
    new_cols = lane >= LANES - 4
    per_tile = LANES // 4
    tile_at = pl.ds(pl.multiple_of((gb // per_tile) * LANES, LANES), LANES)
    shift = (LANES - 4) - 4 * (gb % per_tile)
    heads = list(heads)
    loaded, s1, s2 = [], [], []
    for hh in heads:
        q = q_ref[b, hh]
        kt = kc_ref[b, hh]
        vt = vc_ref[b, hh]
        nkt = pltpu.roll(nk_ref[hh, :, tile_at], shift, axis=1)
        nvt = pltpu.roll(nv_ref[hh, :, tile_at], shift, axis=1)
        loaded.append((kt, vt, nkt, nvt))
        s1.append(_dot(q, kt.astype(BF16)) + b1_ref[hh])
        s2.append(_dot(q, nkt.astype(BF16)) + b2_ref[hh])
    s1, s2 = jnp.concatenate(s1, axis=0), jnp.concatenate(s2, axis=0)
    m = jnp.maximum(jnp.max(s1, axis=-1, keepdims=True), jnp.max(s2, axis=-1, keepdims=True))
    p1 = jnp.exp(s1 - m)
    p2 = jnp.exp(s2 - m)
    z = jnp.sum(p1, axis=-1, keepdims=True) + jnp.sum(p2, axis=-1, keepdims=True)
    p1, p2 = p1.astype(BF16), p2.astype(BF16)
    lse = jnp.broadcast_to(m + jnp.log(z), (8 * len(heads), LANES))
    for i, hh in enumerate(heads):
        kt, vt, nkt, nvt = loaded[i]
        r = slice(8 * i, 8 * i + 8)
        o = _dot_nt(p1[r], vt.astype(BF16)) + _dot_nt(p2[r], nvt.astype(BF16))
        o_ref[b, hh] = o / z[r]
        l_ref[b, hh] = lse[r]
        for src, new, dst in ((kt, nkt, ko_ref), (vt, nvt, vo_ref)):
            rolled = pltpu.roll(src, wb - 4, axis=1)
            if wb > LANES:
                dst[b, hh, :, 0:wb - LANES] = rolled[:, 0:wb - LANES]
            dst[b, hh, :, wb - LANES:wb] = jnp.where(new_cols, new, rolled[:, wb - LANES:wb])


N_RIDE_IN = 7
N_RIDE_OUT = 4


def _ride_specs(ride, bb, block_index):
    q, kc, vc, nk, nv, b1, b2 = ride
    nb, _, _, wb = kc.shape
    blk = lambda *tail: pl.BlockSpec((bb, HEADS_PER_GROUP) + tail, lambda *g: (block_index(*g), 0, 0, 0))
    const = lambda a: pl.BlockSpec(a.shape, lambda *g: (0,) * a.ndim)
    in_specs = [blk(8, HEAD_DIM), blk(HEAD_DIM, wb), blk(HEAD_DIM, wb), const(nk), const(nv), const(b1), const(b2)]
    out_specs = [blk(HEAD_DIM, wb), blk(HEAD_DIM, wb), blk(8, HEAD_DIM), blk(8, LANES)]
    out_shape = [jax.ShapeDtypeStruct(kc.shape, F32), jax.ShapeDtypeStruct(kc.shape, F32),
                 jax.ShapeDtypeStruct((nb, HEADS_PER_GROUP, 8, HEAD_DIM), F32),
                 jax.ShapeDtypeStruct((nb, HEADS_PER_GROUP, 8, LANES), F32)]
    return in_specs, out_specs, out_shape


def _ride_block(step, ride_refs):
    bb = ride_refs[0].shape[0]
    for b in range(bb):
        _attn_sample_batch(step * bb + b, b, *ride_refs)


def _first_max(v, ids, sentinel):
    m = jnp.max(v, axis=0, keepdims=True)
    idx = jnp.min(jnp.where(v == m, ids, sentinel), axis=0, keepdims=True)
    return m, ids == idx


def _oproj_kernel(*refs, tile, dils, ride):
    (x_ref, o0, o1, o2, l0, l1, l2, co_ref, g1_ref, sc_ref, sh_ref, ng_ref, wo_ref, wr_ref, br_ref) = refs[:15]
    refs = refs[15:]
    if ride:
        ride_in, refs = refs[:N_RIDE_IN], refs[N_RIDE_IN:]
        ride_out = refs[6:6 + N_RIDE_OUT]
    x1_ref, xg_ref, gt_ref, posr_ref, post_ref, cnt_ref = refs[:6]
    slab = refs[-1]
    o_refs, l_refs = (o0, o1, o2), (l0, l1, l2)
    outs, lses = [], []
    slab_i = 0
    for g, d in enumerate(dils):
        if d == 1:
            outs.append(o_refs[g][0, 0])
            lses.append(l_refs[g][0, 0])
            continue
        n = tile // d
        for ref, dest in ((o_refs[g], outs), (l_refs[g], lses)):
            halves = []
            for half in range(GROUP_W // LANES):
                for r in range(d):
                    slab[slab_i, pl.ds(r, n, stride=d), :] = ref[0, r, :, half * LANES:(half + 1) * LANES]
                halves.append(slab[slab_i])
                slab_i += 1
            dest.append(jnp.concatenate(halves, axis=1))

    m = jnp.maximum(jnp.maximum(lses[0], lses[1]), lses[2])
    es = [jnp.exp(l - m) for l in lses]
    den = es[0] + es[1] + es[2]
    cat = [(outs[g] * (es[g] / den)).astype(BF16) for g in range(N_GROUPS)]
    cat.append(co_ref[0].astype(BF16))
    cat = jnp.concatenate(cat, axis=1)
    x1 = x_ref[0] + g1_ref[0] * _dot(cat, wo_ref[...])
    x1_ref[0] = x1

    ms = jnp.mean(x1 * x1, axis=-1, keepdims=True)
    h2 = x1 * lax.rsqrt(ms + RMS_EPS) * ng_ref[...]
    h2 = h2 * (1.0 + sc_ref[0]) + sh_ref[0]
    hh = h2.astype(BF16)
    xg_ref[0, :, 0:D_MODEL] = hh

    hl = (h2 - hh.astype(F32)).astype(BF16)
    wr = wr_ref[...]
    wh = wr.astype(BF16)
    wl = (wr - wh.astype(F32)).astype(BF16)
    logits = _dot_nt(wh, hh) + _dot_nt(wh, hl) + _dot_nt(wl, hh)
    scores = 1.0 / (1.0 + jnp.exp(-logits))
    sel = scores + br_ref[...]

    ids = lax.broadcasted_iota(jnp.int32, (GROUP_SIZE, tile), 0)
    ninf = -jnp.inf
    sel_g = [sel[g * GROUP_SIZE:(g + 1) * GROUP_SIZE] for g in range(N_ROUTE_GROUPS)]
    gscore = jnp.zeros((N_ROUTE_GROUPS, tile), F32)
    for g in range(N_ROUTE_GROUPS):
        m1, oh = _first_max(sel_g[g], ids, GROUP_SIZE)
        m2 = jnp.max(jnp.where(oh, ninf, sel_g[g]), axis=0, keepdims=True)
        gscore = jnp.where(ids == g, m1 + m2, gscore)
    gsel = jnp.zeros((N_ROUTE_GROUPS, tile), F32)
    for _ in range(TOPK_GROUPS):
        _, oh = _first_max(gscore, ids, N_ROUTE_GROUPS)
        gsel = jnp.where(oh, 1.0, gsel)
        gscore = jnp.where(oh, ninf, gscore)
    cand = [jnp.where(gsel[g:g + 1] > 0.0, sel_g[g], ninf) for g in range(N_ROUTE_GROUPS)]
    chosen = [jnp.zeros((GROUP_SIZE, tile), F32) for _ in range(N_ROUTE_GROUPS)]
    picks = []
    for _ in range(TOP_K):
        mx = cand[0]
        for g in range(1, N_ROUTE_GROUPS):
            mx = jnp.maximum(mx, cand[g])
        mx = jnp.max(mx, axis=0, keepdims=True)
        idx = jnp.where(cand[0] == mx, ids, N_EXPERTS)
        for g in range(1, N_ROUTE_GROUPS):
            idx = jnp.minimum(idx, jnp.where(cand[g] == mx, ids + g * GROUP_SIZE, N_EXPERTS))
        idx = jnp.min(idx, axis=0, keepdims=True)
        ohs = [(ids + g * GROUP_SIZE) == idx for g in range(N_ROUTE_GROUPS)]
        picks.append(ohs)
        for g in range(N_ROUTE_GROUPS):
            chosen[g] = jnp.where(ohs[g], 1.0, chosen[g])
            cand[g] = jnp.where(ohs[g], ninf, cand[g])
    wts = [jnp.where(chosen[g] > 0.0, scores[g * GROUP_SIZE:(g + 1) * GROUP_SIZE], 0.0)
           for g in range(N_ROUTE_GROUPS)]
    wsum = jnp.sum(wts[0], axis=0, keepdims=True)
    for g in range(1, N_ROUTE_GROUPS):
        wsum = wsum + jnp.sum(wts[g], axis=0, keepdims=True)
    gates_t = jnp.concatenate([w / wsum * ROUTED_SCALE for w in wts], axis=0)
    gt_ref[0] = jnp.concatenate([gates_t, jnp.zeros((LANES - N_EXPERTS, tile), F32)], axis=0).T
    g_hi = gates_t.astype(BF16).astype(F32)
    xg_ref[0, :, D_MODEL:D_MODEL + LANES] = jnp.concatenate([g_hi, gates_t - g_hi], axis=0).T.astype(BF16)

    er = lax.broadcasted_iota(jnp.int32, (N_EXPERTS, N_EXPERTS), 0)
    ec = lax.broadcasted_iota(jnp.int32, (N_EXPERTS, N_EXPERTS), 1)
    lower = jnp.where(ec < er, 1.0, 0.0).astype(BF16)
    tr = lax.broadcasted_iota(jnp.int32, (SUB, SUB), 0)
    tc = lax.broadcasted_iota(jnp.int32, (SUB, SUB), 1)
    upper = jnp.where(tr < tc, 1.0, 0.0).astype(BF16)
    for u in range(tile // SUB):
        cols = slice(u * SUB, (u + 1) * SUB)
        ch = jnp.concatenate([c[:, cols] for c in chosen], axis=0)
        cnt = jnp.sum(ch, axis=1, keepdims=True)
        cpad = jnp.broadcast_to(jnp.ceil(cnt / CHUNK) * CHUNK, (N_EXPERTS, LANES))
        cnt_ref[u] = cpad
        seg0 = _dot(lower, cpad.astype(BF16))[:, 0:1]
        slot = seg0 + _dot(ch.astype(BF16), upper)
        rows = []
        for k in range(TOP_K):
            acc = jnp.where(picks[k][0][:, cols], slot[0:GROUP_SIZE], 0.0)
            for g in range(1, N_ROUTE_GROUPS):
                acc = acc + jnp.where(picks[k][g][:, cols], slot[g * GROUP_SIZE:(g + 1) * GROUP_SIZE], 0.0)
            rows.append(jnp.sum(acc, axis=0, keepdims=True))
        posr = jnp.concatenate(rows, axis=0)
        posr_ref[u] = posr
        post_ref[0, cols, :] = jnp.concatenate([posr, jnp.zeros((LANES - TOP_K, SUB), F32)], axis=0).T
    if ride:
        _ride_block(pl.program_id(0) * pl.num_programs(1) + pl.program_id(1), ride_in + ride_out)


def _oproj(x, o_list, l_list, co, g1, sc, sh, ng, wo_bf, wr_t, br, *, tile, dils, ride=None):
    B, S, _ = x.shape
    nt = S // tile
    r_in, r_out, r_shape, r_args = [], [], [], []
    if ride is not None:
        r_in, r_out, r_shape = _ride_specs(ride, ride[1].shape[0] // (B * nt), lambda b, j: b * nt + j)
        r_args = list(ride)
    tm = g1.shape[1]
    mod_spec = pl.BlockSpec((1, tm, D_MODEL), (lambda b, j: (b, j, 0)) if tm > 1 else (lambda b, j: (b, 0, 0)))
    const2 = lambda shape: pl.BlockSpec(shape, lambda b, j: (0, 0))
    nat = lambda w: pl.BlockSpec((1, tile, w), lambda b, j: (b, j, 0))
    dspec = [pl.BlockSpec((1, d, tile // d, GROUP_W), lambda b, j: (b, 0, j, 0)) for d in dils]
    n_slabs = sum(2 * (GROUP_W // LANES) for d in dils if d > 1)
    nsub_t = tile // SUB
    return pl.pallas_call(
        functools.partial(_oproj_kernel, tile=tile, dils=dils, ride=ride is not None),
        grid=(B, nt),
        in_specs=[nat(D_MODEL)] + dspec + dspec + [nat(CONV_CH), mod_spec, mod_spec, mod_spec,
                  const2((1, D_MODEL)), const2((D_MODEL, D_MODEL)), const2((N_EXPERTS, D_MODEL)),
                  const2((N_EXPERTS, 1))] + r_in,
        out_specs=[nat(D_MODEL), nat(XG_W), nat(LANES),
                   pl.BlockSpec((nsub_t, TOP_K, SUB), lambda b, j: (b * nt + j, 0, 0)),
                   nat(LANES),
                   pl.BlockSpec((nsub_t, N_EXPERTS, LANES), lambda b, j: (b * nt + j, 0, 0))] + r_out,
        out_shape=[jax.ShapeDtypeStruct((B, S, D_MODEL), F32), jax.ShapeDtypeStruct((B, S, XG_W), BF16),
                   jax.ShapeDtypeStruct((B, S, LANES), F32),
                   jax.ShapeDtypeStruct((B * S // SUB, TOP_K, SUB), F32),
                   jax.ShapeDtypeStruct((B, S, LANES), F32),
                   jax.ShapeDtypeStruct((B * S // SUB, N_EXPERTS, LANES), F32)] + r_shape,
        scratch_shapes=[pltpu.VMEM((max(n_slabs, 1), tile, LANES), F32)],
        compiler_params=_params(("arbitrary", "arbitrary")),
        name="oproj",
    )(x, *o_list, *l_list, co, g1, sc, sh, ng, wo_bf, wr_t, br, *r_args)


DENSE_EXPERTS_PER_STEP = 4


def _moe_kernel(h_ref, gt_ref, x1_ref, g2_ref, wg_ref, wu_ref, wd_ref, wgs_ref, wus_ref, wds_ref,
                y_ref, acc, *, tile):
    step = pl.program_id(2)
    h = h_ref[0]

    @pl.when(step == 0)
    def _():
        a = _dot(h, wgs_ref[...].astype(BF16))
        b = _dot(h, wus_ref[...].astype(BF16))
        acc[...] = _dot((_silu(a) * b).astype(BF16), wds_ref[...].astype(BF16))

    lane = lax.broadcasted_iota(jnp.int32, (tile, LANES), 1)
    routed = None
    for j in range(DENSE_EXPERTS_PER_STEP):
        e = step * DENSE_EXPERTS_PER_STEP + j
        a = _dot(h, wg_ref[j].astype(BF16))
        b = _dot(h, wu_ref[j].astype(BF16))
        gcol = jnp.sum(jnp.where(lane == e, gt_ref[0], 0.0), axis=1, keepdims=True)
        hc = _silu(a) * b * gcol
        y = _dot(hc.astype(BF16), wd_ref[j].astype(BF16))
        routed = y if routed is None else routed + y
    acc[...] += routed

    @pl.when(step == pl.num_programs(2) - 1)
    def _():
        y_ref[0] = x1_ref[0] + g2_ref[0] * acc[...]


def _moe(h2, gates, x1, g2, wg, wu, wd, wgs, wus, wds, *, tile):
    B, S, _ = h2.shape
    tm = g2.shape[1]
    mod_spec = pl.BlockSpec((1, tm, D_MODEL),
                            (lambda b, j, e: (b, j, 0)) if tm > 1 else (lambda b, j, e: (b, 0, 0)))
    nat = lambda w: pl.BlockSpec((1, tile, w), lambda b, j, e: (b, j, 0))
    const2 = lambda shape: pl.BlockSpec(shape, lambda b, j, e: (0, 0))
    return pl.pallas_call(
        functools.partial(_moe_kernel, tile=tile),
        grid=(B, S // tile, N_EXPERTS // DENSE_EXPERTS_PER_STEP),
        in_specs=[nat(D_MODEL), nat(LANES), nat(D_MODEL), mod_spec,
                  pl.BlockSpec((DENSE_EXPERTS_PER_STEP, D_MODEL, D_EXPERT), lambda b, j, e: (e, 0, 0)),
                  pl.BlockSpec((DENSE_EXPERTS_PER_STEP, D_MODEL, D_EXPERT), lambda b, j, e: (e, 0, 0)),
                  pl.BlockSpec((DENSE_EXPERTS_PER_STEP, D_EXPERT, D_MODEL), lambda b, j, e: (e, 0, 0)),
                  const2((D_MODEL, D_EXPERT)), const2((D_MODEL, D_EXPERT)), const2((D_EXPERT, D_MODEL))],
        out_specs=nat(D_MODEL),
        out_shape=jax.ShapeDtypeStruct((B, S, D_MODEL), F32),
        scratch_shapes=[pltpu.VMEM((tile, D_MODEL), F32)],
        compiler_params=_params(("arbitrary", "arbitrary", "arbitrary")),
        name="moe",
    )(h2, gates, x1, g2, wg, wu, wd, wgs, wus, wds)


FFN_TM = 1024
SUB_ROWS = SUB * TOP_K + N_EXPERTS * CHUNK
N_CHUNKS = SUB_ROWS // CHUNK
ZERO_CHUNK = N_CHUNKS
TABLE_W = 256
MXU_ROWS = 256
BLOCKS_PER_TRIP = 4


def _slot_onehot(first, pos_list, axis_iota):
    shape = axis_iota.shape
    hit = axis_iota == jnp.broadcast_to(pos_list[0] - first, shape).astype(BF16)
    for p in pos_list[1:]:
        hit = hit | (axis_iota == jnp.broadcast_to(p - first, shape).astype(BF16))
    return jnp.where(hit, jnp.ones(shape, BF16), jnp.zeros(shape, BF16))


def _chunk_copy(src, src_chunk, dst, dst_chunk, sem):
    return pltpu.make_async_copy(src.at[src_chunk], dst.at[dst_chunk], sem)


def _for_each(n, body):
    def two(t, carry):
        body(2 * t)
        body(2 * t + 1)
        return carry

    lax.fori_loop(0, n // 2, two, 0)

    @pl.when(n % 2 == 1)
    def _():
        body(n - 1)


def _dispatch_kernel(nblk_ref, nreal_ref, ncopy_ref, ntile_ref, src_ref, dst_ref, posr_ref, xg_ref, xs_hbm,
                     xs_scr, zero_scr, sem, tail_sem, *, nsub, n_tiles_max):
    s = pl.program_id(0)

    tile_chunks = FFN_TM // CHUNK
    blk_chunks = MXU_ROWS // CHUNK
    zero_scr[...] = jnp.zeros((tile_chunks, CHUNK, XG_W), BF16)
    tail_blocks = [ntile_ref[0] + s + r * nsub for r in range(-(-(n_tiles_max) // nsub))]

    def tail_copy(t):
        return pltpu.make_async_copy(
            zero_scr, xs_hbm.at[pl.ds(pl.multiple_of(t * tile_chunks, tile_chunks), tile_chunks)], tail_sem)

    for t in tail_blocks:
        @pl.when(t < n_tiles_max)
        def _():
            tail_copy(t).start()

    xg = xg_ref[...]
    pos = [posr_ref[0, k:k + 1, :] for k in range(TOP_K)]
    rows = lax.broadcasted_iota(jnp.int32, (MXU_ROWS, SUB), 0).astype(BF16)
    xs_scr[ZERO_CHUNK] = jnp.zeros((CHUNK, XG_W), BF16)

    def sort_blocks(i, carry):
        for j in range(BLOCKS_PER_TRIP):
            blk = i * BLOCKS_PER_TRIP + j
            onehot = _slot_onehot((blk * MXU_ROWS).astype(F32), pos, rows)
            sorted_rows = _dot(onehot, xg).astype(BF16)
            xs_scr[pl.ds(pl.multiple_of(blk * blk_chunks, blk_chunks), blk_chunks)] = (
                sorted_rows.reshape(blk_chunks, CHUNK, XG_W))
        k0 = i * trip_chunks
        _for_each(jnp.clip(n_real - k0, 0, trip_chunks), lambda t: send(k0 + t))
        return carry

    n_real = nreal_ref[s]
    n = ncopy_ref[s]
    trip_chunks = BLOCKS_PER_TRIP * blk_chunks

    def send(i):
        _chunk_copy(xs_scr, src_ref[0, 0, i], xs_hbm, dst_ref[0, 0, i], sem).start()

    lax.fori_loop(0, nblk_ref[s], sort_blocks, 0)
    _for_each(n - n_real, lambda t: send(n_real + t))
    _for_each(n, lambda i: _chunk_copy(xs_scr, 0, xs_hbm, 0, sem).wait())
    for t in tail_blocks:
        @pl.when(t < n_tiles_max)
        def _():
            tail_copy(t).wait()


def _dispatch(nblk, n_real, ncopy, n_tiles, src, dst, posr, xg, n_rows):
    nsub = posr.shape[0]
    tab = pl.BlockSpec((1, 1, TABLE_W), lambda s, *_: (s, 0, 0), memory_space=pltpu.SMEM)
    return pl.pallas_call(
        functools.partial(_dispatch_kernel, nsub=nsub, n_tiles_max=n_rows // FFN_TM),
        grid_spec=pltpu.PrefetchScalarGridSpec(
            num_scalar_prefetch=4, grid=(nsub,),
            in_specs=[tab, tab, pl.BlockSpec((1, TOP_K, SUB), lambda s, *_: (s, 0, 0)),
                      pl.BlockSpec((SUB, XG_W), lambda s, *_: (s, 0))],
            out_specs=pl.BlockSpec(memory_space=pl.ANY),
            scratch_shapes=[pltpu.VMEM((N_CHUNKS + 1, CHUNK, XG_W), BF16),
                            pltpu.VMEM((FFN_TM // CHUNK, CHUNK, XG_W), BF16),
                            pltpu.SemaphoreType.DMA(()), pltpu.SemaphoreType.DMA(())]),
        out_shape=jax.ShapeDtypeStruct((n_rows // CHUNK, CHUNK, XG_W), BF16),
        compiler_params=_params(("arbitrary",)),
        name="moe_dispatch",
    )(nblk, n_real, ncopy, n_tiles, src, dst, posr, xg).reshape(n_rows, XG_W)


def _ffn_kernel(te_ref, nt_ref, xs_ref, wg_ref, wu_ref, wd_ref,
                q_ref, kc_ref, vc_ref, nk_ref, nv_ref, b1_ref, b2_ref,
                ys_ref, ko_ref, vo_ref, o_ref, l_ref, wg_s, wu_s, wd_s, *, steps_per_batch):
    i = pl.program_id(0)
    n_heads = HEADS_PER_GROUP // steps_per_batch
    first_head = (i % steps_per_batch) * n_heads

    def ride():
        _attn_sample_batch(i // steps_per_batch, 0, q_ref, kc_ref, vc_ref, nk_ref, nv_ref, b1_ref, b2_ref,
                           ko_ref, vo_ref, o_ref, l_ref, heads=[first_head + j for j in range(n_heads)])

    @pl.when(i >= nt_ref[0])
    def _():
        ride()
        ys_ref[...] = jnp.zeros((FFN_TM, D_MODEL), BF16)

    @pl.when(i < nt_ref[0])
    def _():
        e = te_ref[i]

        @pl.when((i == 0) | (e != te_ref[jnp.maximum(i - 1, 0)]))
        def _():
            wg_s[...] = wg_ref[0].astype(BF16)
            wu_s[...] = wu_ref[0].astype(BF16)
            wd_s[...] = wd_ref[0].astype(BF16)

        ride()
        x = xs_ref[:, 0:D_MODEL]
        g = xs_ref[:, D_MODEL:XG_W].astype(F32)
        lane = lax.broadcasted_iota(jnp.int32, (FFN_TM, LANES), 1)
        gcol = jnp.sum(jnp.where((lane == e) | (lane == e + N_EXPERTS), g, 0.0), axis=1, keepdims=True)
        a = _dot(x, wg_s[...])
        b = _dot(x, wu_s[...])
        hc = _silu(a) * b * gcol
        ys_ref[...] = _dot(hc.astype(BF16), wd_s[...]).astype(BF16)


def _ffn(tile_expert, n_tiles, xs, wg, wu, wd, ride):
    q, kc, vc, nk, nv, b1, b2 = ride
    nb, _, _, wb = kc.shape
    n_rows = xs.shape[0]
    n_steps = n_rows // FFN_TM
    assert n_steps % nb == 0 and HEADS_PER_GROUP % (n_steps // nb) == 0
    per = n_steps // nb
    row = lambda i, te, nt: (jnp.minimum(i, nt[0] - 1), 0)
    wspec = lambda shape: pl.BlockSpec((1,) + shape, lambda i, te, nt: (te[i], 0, 0))
    rblk = lambda *tail: pl.BlockSpec((1, HEADS_PER_GROUP) + tail, lambda i, te, nt: (i // per, 0, 0, 0))
    const = lambda shape: pl.BlockSpec(shape, lambda i, te, nt: (0, 0, 0))
    return pl.pallas_call(
        functools.partial(_ffn_kernel, steps_per_batch=per),
        grid_spec=pltpu.PrefetchScalarGridSpec(
            num_scalar_prefetch=2, grid=(n_steps,),
            in_specs=[pl.BlockSpec((FFN_TM, XG_W), row), wspec((D_MODEL, D_EXPERT)),
                      wspec((D_MODEL, D_EXPERT)), wspec((D_EXPERT, D_MODEL)),
                      rblk(8, HEAD_DIM), rblk(HEAD_DIM, wb), rblk(HEAD_DIM, wb), const(nk.shape), const(nv.shape),
                      const(b1.shape), const(b2.shape)],
            out_specs=[pl.BlockSpec((FFN_TM, D_MODEL), lambda i, te, nt: (i, 0)),
                       rblk(HEAD_DIM, wb), rblk(HEAD_DIM, wb), rblk(8, HEAD_DIM), rblk(8, LANES)],
            scratch_shapes=[pltpu.VMEM((D_MODEL, D_EXPERT), BF16), pltpu.VMEM((D_MODEL, D_EXPERT), BF16),
                            pltpu.VMEM((D_EXPERT, D_MODEL), BF16)]),
        out_shape=[jax.ShapeDtypeStruct((n_rows, D_MODEL), BF16),
                   jax.ShapeDtypeStruct(kc.shape, F32), jax.ShapeDtypeStruct(kc.shape, F32),
                   jax.ShapeDtypeStruct((nb, HEADS_PER_GROUP, 8, HEAD_DIM), F32),
                   jax.ShapeDtypeStruct((nb, HEADS_PER_GROUP, 8, LANES), F32)],
        compiler_params=_params(("arbitrary",)),
        name="moe_ffn",
    )(tile_expert, n_tiles, xs, wg, wu, wd, q, kc, vc, nk, nv, b1, b2)


def _combine_kernel(nblk_ref, ncopy_ref, src_ref, post_ref, xg_ref, x1_ref, g2_ref, wgs_ref, wus_ref,
                    wds_ref, ys_hbm, y_ref, ys_scr, wgs_s, wus_s, wds_s, sem):
    s = pl.program_id(0)
    n = ncopy_ref[s]
    blk_chunks = MXU_ROWS // CHUNK
    trip_chunks = BLOCKS_PER_TRIP * blk_chunks

    def fetch_trip(t, carry):
        k0 = t * trip_chunks
        _for_each(trip_chunks,
                  lambda u: _chunk_copy(ys_hbm, src_ref[0, 0, k0 + u], ys_scr, k0 + u, sem.at[t]).start())
        return carry

    lax.fori_loop(0, nblk_ref[s], fetch_trip, 0)

    @pl.when(s == 0)
    def _():
        wgs_s[...] = wgs_ref[...].astype(BF16)
        wus_s[...] = wus_ref[...].astype(BF16)
        wds_s[...] = wds_ref[...].astype(BF16)

    h = xg_ref[:, 0:D_MODEL]
    a = _dot(h, wgs_s[...])
    b = _dot(h, wus_s[...])
    shared = _dot((_silu(a) * b).astype(BF16), wds_s[...])

    post = post_ref[...]
    pos = [jnp.broadcast_to(post[:, k:k + 1], (SUB, MXU_ROWS)) for k in range(TOP_K)]
    lanes = lax.broadcasted_iota(jnp.int32, (SUB, MXU_ROWS), 1).astype(BF16)

    def gather_blocks(i, acc):
        _for_each(trip_chunks, lambda t: _chunk_copy(ys_hbm, 0, ys_scr, 0, sem.at[i]).wait())
        for j in range(BLOCKS_PER_TRIP):
            blk = i * BLOCKS_PER_TRIP + j
            onehot = _slot_onehot((blk * MXU_ROWS).astype(F32), pos, lanes)
            rows = ys_scr[pl.ds(pl.multiple_of(blk * blk_chunks, blk_chunks), blk_chunks)]
            acc = acc + _dot(onehot, rows.reshape(MXU_ROWS, D_MODEL))
        return acc

    routed = lax.fori_loop(0, nblk_ref[s], gather_blocks, jnp.zeros((SUB, D_MODEL), F32))
    y_ref[...] = x1_ref[...] + g2_ref[0] * (routed + shared)


def _combine(nblk, ncopy, src, post, xg, x1, g2, wgs, wus, wds, ys):
    n_tok = x1.shape[0]
    nsub = n_tok // SUB
    per_b = nsub // g2.shape[0]
    tab = pl.BlockSpec((1, 1, TABLE_W), lambda s, *_: (s, 0, 0), memory_space=pltpu.SMEM)
    const2 = lambda shape: pl.BlockSpec(shape, lambda s, *_: (0, 0))
    return pl.pallas_call(
        _combine_kernel,
        grid_spec=pltpu.PrefetchScalarGridSpec(
            num_scalar_prefetch=2, grid=(nsub,),
            in_specs=[tab, pl.BlockSpec((SUB, LANES), lambda s, *_: (s, 0)),
                      pl.BlockSpec((SUB, XG_W), lambda s, *_: (s, 0)),
                      pl.BlockSpec((SUB, D_MODEL), lambda s, *_: (s, 0)),
                      pl.BlockSpec((1, 1, D_MODEL), lambda s, *_: (s // per_b, 0, 0)),
                      const2((D_MODEL, D_EXPERT)), const2((D_MODEL, D_EXPERT)), const2((D_EXPERT, D_MODEL)),
                      pl.BlockSpec(memory_space=pl.ANY)],
            out_specs=pl.BlockSpec((SUB, D_MODEL), lambda s, *_: (s, 0)),
            scratch_shapes=[pltpu.VMEM((N_CHUNKS, CHUNK, D_MODEL), BF16), pltpu.VMEM((D_MODEL, D_EXPERT), BF16),
                            pltpu.VMEM((D_MODEL, D_EXPERT), BF16), pltpu.VMEM((D_EXPERT, D_MODEL), BF16),
                            pltpu.SemaphoreType.DMA((SUB_ROWS // (BLOCKS_PER_TRIP * MXU_ROWS),))]),
        out_shape=jax.ShapeDtypeStruct((n_tok, D_MODEL), F32),
        compiler_params=_params(("arbitrary",)),
        name="moe_combine",
    )(nblk, ncopy, src, post, xg, x1, g2, wgs, wus, wds, ys.reshape(-1, CHUNK, D_MODEL))


def _route_tables(cnt):
    nsub = cnt.shape[0]
    i32 = jnp.int32
    seg_end = jnp.cumsum(cnt, axis=1)
    seg_start = seg_end - cnt
    tot = jnp.sum(cnt, axis=0)
    tot_al = (tot + FFN_TM - 1) // FFN_TM * FFN_TM
    reg_end = jnp.cumsum(tot_al)
    reg_start = reg_end - tot_al
    base = reg_start[None, :] + jnp.cumsum(cnt, axis=0) - cnt
    n_real = seg_end[:, -1] // CHUNK
    k = jnp.arange(TABLE_W, dtype=i32)
    owner = jnp.sum((seg_end[:, None, :] // CHUNK) <= k[None, :, None], axis=2)
    owner = jnp.minimum(owner, N_EXPERTS - 1)
    delta = (base - seg_start) // CHUNK
    e_ids = jnp.arange(N_EXPERTS, dtype=i32)
    real_dst = jnp.sum(jnp.where(owner[:, :, None] == e_ids, delta[:, None, :], 0), axis=2) + k[None, :]
    fill_n = (tot_al - tot) // CHUNK
    fill_dst0 = (reg_start + tot) // CHUNK
    src = jnp.broadcast_to(k[None, :], (nsub, TABLE_W))
    dst = real_dst
    ncopy = n_real
    for r in range(-(-N_EXPERTS // nsub)):
        e_of = jnp.arange(nsub, dtype=i32) + r * nsub
        ok = e_of < N_EXPERTS
        e_cl = jnp.minimum(e_of, N_EXPERTS - 1)
        fn = jnp.where(ok, fill_n[e_cl], 0)
        j = k[None, :] - ncopy[:, None]
        is_fill = (j >= 0) & (j < fn[:, None])
        src = jnp.where(is_fill, ZERO_CHUNK, src)
        dst = jnp.where(is_fill, fill_dst0[e_cl][:, None] + j, dst)
        ncopy = ncopy + fn
    trip_rows = BLOCKS_PER_TRIP * MXU_ROWS
    disp_nblk = (n_real * CHUNK + trip_rows - 1) // trip_rows
    comb_n = disp_nblk * (trip_rows // CHUNK)
    comb_src = jnp.where(k[None, :] < n_real[:, None], real_dst, real_dst[:, 0:1])
    n_tiles = reg_end[-1] // FFN_TM
    t = jnp.arange((nsub * SUB_ROWS + N_EXPERTS * FFN_TM) // FFN_TM, dtype=i32)
    tile_expert = jnp.minimum(jnp.sum((reg_end[None, :] // FFN_TM) <= t[:, None], axis=1), N_EXPERTS - 1)
    as3 = lambda a: a.astype(i32).reshape(nsub, 1, TABLE_W)
    return (disp_nblk.astype(i32), n_real.astype(i32), ncopy.astype(i32), as3(src), as3(dst),
            comb_n.astype(i32), as3(comb_src), tile_expert.astype(i32), n_tiles.astype(i32).reshape(1))


def _moe_sparse(xg, post, posr, cnt, x1, g2, wg, wu, wd, wgs, wus, wds, ride):
    B, S, _ = x1.shape
    n_tok = B * S
    nsub = n_tok // SUB
    (disp_nblk, n_real, disp_n, disp_src, disp_dst, comb_n, comb_src, tile_expert,
     n_tiles) = _route_tables(cnt[:, :, 0].astype(jnp.int32))
    n_rows = nsub * SUB_ROWS + N_EXPERTS * FFN_TM
    xg2 = xg.reshape(n_tok, XG_W)
    xs = _dispatch(disp_nblk, n_real, disp_n, n_tiles, disp_src, disp_dst, posr, xg2, n_rows)
    ys, *ride_out = _ffn(tile_expert, n_tiles, xs, wg, wu, wd, ride)
    y = _combine(disp_nblk, comb_n, comb_src, post.reshape(n_tok, LANES), xg2, x1.reshape(n_tok, D_MODEL),
                 g2, wgs, wus, wds, ys)
    return y.reshape(B, S, D_MODEL), ride_out


def _t5_bucket(dist):
    max_exact = N_BUCKETS // 2
    df = jnp.maximum(dist, 1).astype(F32)
    large = max_exact + (jnp.log(df / max_exact) / math.log(MAX_DISTANCE / max_exact)
                         * (N_BUCKETS - max_exact)).astype(jnp.int32)
    large = jnp.minimum(large, N_BUCKETS - 1)
    return jnp.where(dist < max_exact, dist, large)


def _step_bias(rel_bias, g):
    dist = jnp.arange(N_KEYS, dtype=jnp.int32) * DILATIONS[g]
    cols = rel_bias[:, g * HEADS_PER_GROUP:(g + 1) * HEADS_PER_GROUP]
    return cols[_t5_bucket(dist)].T.astype(F32)


def _prompt_bias(bias_k):
    h = bias_k.shape[0]
    n = 3 * BLK
    row = jnp.concatenate([bias_k[:, ::-1], jnp.full((h, n - N_KEYS), NEG, F32)], axis=1)
    t = jnp.tile(row, (1, BLK))[:, :BLK * (n - 1)].reshape(h, BLK, n - 1)
    return t[:, :, :2 * BLK]


def _sample_bias(bias_k, wb, d):
    h = bias_k.shape[0]
    rev = bias_k[:, :0:-1]
    rows = []
    for t in range(4):
        if d == 1:
            rows.append(jnp.concatenate([jnp.full((h, t), NEG, F32), rev[:, :wb - t]], axis=1))
        else:
            r = jnp.arange(d)[None, None, :]
            rows.append(jnp.where(r == t, rev[:, :, None], NEG).reshape(h, wb))
    b1 = jnp.concatenate([jnp.stack(rows, axis=1), jnp.zeros((h, 4, wb), F32)], axis=1)
    rows = []
    for t in range(4):
        cols = [bias_k[:, (t - tn) // d] if (t >= tn and (t - tn) % d == 0) else jnp.full((h,), NEG, F32)
                for tn in range(4)]
        rows.append(jnp.concatenate([jnp.full((h, LANES - 4), NEG, F32), jnp.stack(cols, axis=1)], axis=1))
    b2 = jnp.concatenate([jnp.stack(rows, axis=1), jnp.zeros((h, 4, LANES), F32)], axis=1)
    return b1, b2


TILE = 512
RIDE_ON_OPROJ, RIDE_ON_PROJ, RIDE_ON_FFN = 0, 1, 2


def _sample_front(x, mods, hist, caches, rel_bias, weights):
    norm1_g, _, w_in_bf, qg, kg, conv_w, bd = weights[:7]
    sh1, sc1 = mods[:2]
    S = x.shape[1]
    nb = S // 4
    qf, kt, vt, u, co = _proj(x, sc1, sh1, norm1_g, w_in_bf, qg, kg, conv_w, bd, hist,
                              tile=TILE, dils=(1, 1, 1), sample=True)
    rides = []
    for g in range(N_GROUPS):
        d, wb = DILATIONS[g], WINDOWS[g]
        kc, vc = caches[g]
        b1, b2 = _sample_bias(_step_bias(rel_bias, g), wb, d)
        q = qf[0, :, g * GROUP_W:(g + 1) * GROUP_W].reshape(nb, 4, HEADS_PER_GROUP, HEAD_DIM)
        q = jnp.pad(q.transpose(0, 2, 1, 3), ((0, 0), (0, 0), (0, 4), (0, 0))).astype(BF16)
        new_t = lambda a: a[g * GROUP_W:(g + 1) * GROUP_W].reshape(HEADS_PER_GROUP, HEAD_DIM, S)
        rides.append((q, kc, vc, new_t(kt), new_t(vt), b1, b2))
    return dict(x=x, mods=mods, u=u, co=co), rides


def _sample_back(front, ride_outs, weights):
    (_, norm2_g, _, _, _, _, _, w_o_bf, w_r_t, b_r, w_gate_e, w_up_e, w_down_e,
     w_gate_s, w_up_s, w_down_s) = weights
    x, u, co = front["x"], front["u"], front["co"]
    _, _, g1, sh2, sc2, g2 = front["mods"]
    S = x.shape[1]
    nb = S // 4
    states, o_list, l_list = [], [], []
    for g in range(N_GROUPS):
        ko, vo, o, lse = ride_outs[g]
        states += [ko.transpose(0, 3, 1, 2)[None], vo.transpose(0, 3, 1, 2)[None]]
        o_list.append(o[:, :, :4].transpose(0, 2, 1, 3).reshape(1, 1, S, GROUP_W))
        lse = jnp.broadcast_to(lse[:, :, :4, :1], (nb, HEADS_PER_GROUP, 4, HEAD_DIM))
        l_list.append(lse.transpose(0, 2, 1, 3).reshape(1, 1, S, GROUP_W))
    states.append(u.reshape(nb, 4, CONV_CH)[:, 2:][None])
    x1, xg, gates, _, _, _ = _oproj(x, o_list, l_list, co, g1, sc2, sh2, norm2_g, w_o_bf, w_r_t, b_r,
                                    tile=TILE, dils=(1, 1, 1))
    y = _moe(xg, gates, x1, g2, w_gate_e, w_up_e, w_down_e, w_gate_s, w_up_s, w_down_s, tile=TILE)
    return y, states


def _prompt_layer(x, mods, rel_bias, weights, rides):
    (norm1_g, norm2_g, w_in_bf, qg, kg, conv_w, bd, w_o_bf, w_r_t, b_r, w_gate_e, w_up_e, w_down_e,
     w_gate_s, w_up_s, w_down_s) = weights
    sh1, sc1, g1, sh2, sc2, g2 = mods
    B, S, _ = x.shape
    ride_outs = [None] * N_GROUPS
    res = _proj(x, sc1, sh1, norm1_g, w_in_bf, qg, kg, conv_w, bd, None, tile=TILE, dils=DILATIONS, sample=False,
                ride=rides[RIDE_ON_PROJ])
    qkv, tails, (u, co), ride_outs[RIDE_ON_PROJ] = res[:9], res[9:15], res[15:17], res[17:]
    states, o_list, l_list = [], [], []
    for g in range(N_GROUPS):
        d, w = DILATIONS[g], WINDOWS[g]
        q, k, v = (a.reshape(B * d, S // d, GROUP_W) for a in qkv[3 * g:3 * g + 3])
        o, lse = _attn_prompt(q, k, v, _prompt_bias(_step_bias(rel_bias, g)))
        o_list.append(o.reshape(B, d, S // d, GROUP_W))
        l_list.append(lse.reshape(B, d, S // d, GROUP_W))
        keep = lambda a: a.reshape(B, HEADS_PER_GROUP, HEAD_DIM, w).transpose(0, 3, 1, 2)[None]
        states += [keep(tails[2 * g]), keep(tails[2 * g + 1])]
    states.append(u[:, 6:][None])
    x1, xg, _, posr, post, cnt, *ride_outs[RIDE_ON_OPROJ] = _oproj(
        x, o_list, l_list, co, g1, sc2, sh2, norm2_g, w_o_bf, w_r_t, b_r, tile=TILE, dils=DILATIONS,
        ride=rides[RIDE_ON_OPROJ])
    y, ride_outs[RIDE_ON_FFN] = _moe_sparse(xg, post, posr, cnt, x1, g2, w_gate_e, w_up_e, w_down_e, w_gate_s,
                                            w_up_s, w_down_s, rides[RIDE_ON_FFN])
    return y, states, ride_outs


def kernel(x_prompt, x_sample, c_prompt, c_sample, cache_k_w128, cache_v_w128, cache_k_w512, cache_v_w512, cache_k_w2048, cache_v_w2048, state_conv, rel_bias, norm1_g, norm2_g, w_ada, b_ada, w_in, q_norm_g, k_norm_g, conv_w, w_o, w_router, b_router, w_gate_e, w_up_e, w_down_e, w_gate_s, w_up_s, w_down_s):
    B = x_prompt.shape[0]
    DB, T = x_sample.shape[:2]
    n_c = B + DB
    c_all = jnp.pad(jnp.concatenate([c_prompt, c_sample], axis=0), ((0, (-n_c) % 8), (0, 0)))
    mod = _ada(c_all, w_ada[0], b_ada)
    chunks = [mod[:, i * D_MODEL:(i + 1) * D_MODEL] for i in range(6)]
    mods_p = [c[:B].reshape(B, 1, D_MODEL) for c in chunks]
    mods_s = [jnp.repeat(c[B:n_c], T, axis=0).reshape(1, DB * T, D_MODEL) for c in chunks]

    eye = jnp.arange(GROUP_W) // HEAD_DIM
    bd = jnp.where(eye[:, None] == eye[None, :], 1.0 / HEAD_DIM, 0.0).astype(BF16)
    weights = (norm1_g, norm2_g, w_in[0].astype(BF16),
               jnp.tile(q_norm_g, (1, HEADS_PER_GROUP)), jnp.tile(k_norm_g, (1, HEADS_PER_GROUP)),
               conv_w[0], bd, w_o[0].astype(BF16), w_router[0].T, b_router.reshape(N_EXPERTS, 1),
               w_gate_e[0], w_up_e[0], w_down_e[0], w_gate_s[0], w_up_s[0], w_down_s[0])

    s0, s1 = state_conv[0, :, 0], state_conv[0, :, 1]
    zero = jnp.zeros_like(s0)
    hist_a = jnp.stack([s0, s1, zero, zero], axis=1).reshape(DB * T, CONV_CH)
    hist_b = jnp.stack([s1, zero, zero, zero], axis=1).reshape(DB * T, CONV_CH)
    caches = [(ck[0].transpose(0, 2, 3, 1), cv[0].transpose(0, 2, 3, 1))
              for ck, cv in ((cache_k_w128, cache_v_w128), (cache_k_w512, cache_v_w512),
                             (cache_k_w2048, cache_v_w2048))]
    front, ride = _sample_front(x_sample.reshape(1, DB * T, D_MODEL), mods_s, (hist_a, hist_b), caches,
                                rel_bias, weights)
    yp, st_p, ride_out = _prompt_layer(x_prompt, mods_p, rel_bias, weights, ride)
    ys, st_s = _sample_back(front, ride_out, weights)
    return (yp, ys.reshape(DB, T, D_MODEL), *st_p, *st_s)
```

```python
import functools
import math

import jax
import jax.numpy as jnp
from jax import lax
from jax.experimental import pallas as pl
from jax.experimental.pallas import tpu as pltpu

D_MODEL = 1024
HEAD_DIM = 64
HEADS_PER_GROUP = 4
GROUP_W = HEADS_PER_GROUP * HEAD_DIM
WINDOWS = (128, 512, 2048)
DILATIONS = (1, 4, 16)
N_GROUPS = 3
ATTN_W = N_GROUPS * GROUP_W
CONV_CH = 256
N_PROJ = 3 * ATTN_W + 3 * CONV_CH
N_STEPS = 128
N_KEYS = N_STEPS + 1
BLK = 128
N_BUCKETS = 32
MAX_DISTANCE = WINDOWS[-1]
N_EXPERTS = 64
TOP_K = 8
N_ROUTE_GROUPS = 8
GROUP_SIZE = N_EXPERTS // N_ROUTE_GROUPS
TOPK_GROUPS = 4
D_EXPERT = 256
ROUTED_SCALE = 2.5
RMS_EPS = 1e-6
NEG = -1e30
LANES = 128
SUB = 256
CHUNK = 16
XG_W = D_MODEL + LANES
VMEM_LIMIT = 56 * 1024 * 1024

F32 = jnp.float32
BF16 = jnp.bfloat16


def _dot(a, b):
    return jnp.dot(a, b, preferred_element_type=F32)


def _dot_nt(a, b):
    return lax.dot_general(a, b, (((1,), (1,)), ((), ())), preferred_element_type=F32)


def _silu(a):
    return a / (1.0 + jnp.exp(-a))


def _params(sem):
    return pltpu.CompilerParams(dimension_semantics=sem, vmem_limit_bytes=VMEM_LIMIT)


def _ada_kernel(c_ref, w_ref, b_ref, o_ref):
    s = _silu(c_ref[...])
    o_ref[...] = _dot(s.astype(BF16), w_ref[...].astype(BF16)) + b_ref[...]


def _ada(c, w_ada, b_ada):
    n = c.shape[0]
    nc = w_ada.shape[1] // D_MODEL
    return pl.pallas_call(
        _ada_kernel,
        grid=(nc,),
        in_specs=[pl.BlockSpec((n, D_MODEL), lambda j: (0, 0)),
                  pl.BlockSpec((D_MODEL, D_MODEL), lambda j: (0, j)),
                  pl.BlockSpec((1, D_MODEL), lambda j: (0, j))],
        out_specs=pl.BlockSpec((n, D_MODEL), lambda j: (0, j)),
        out_shape=jax.ShapeDtypeStruct((n, w_ada.shape[1]), F32),
        compiler_params=_params(("arbitrary",)),
        name="ada",
    )(c, w_ada, b_ada)


def _proj_kernel(*refs, tile, dils, sample, nt):
    if sample:
        (x_ref, sc_ref, sh_ref, ng_ref, w_ref, qg_ref, kg_ref, cw_ref, bd_ref, ha_ref, hb_ref,
         qf_ref, kt_ref, vt_ref, u_ref, co_ref, u_scr) = refs
    else:
        (x_ref, sc_ref, sh_ref, ng_ref, w_ref, qg_ref, kg_ref, cw_ref, bd_ref) = refs[:9]
        ride_in, refs = refs[9:9 + N_RIDE_IN], refs[9 + N_RIDE_IN:]
        (q0, k0, v0, q1, k1, v1, q2, k2, v2, kt0, vt0, kt1, vt1, kt2, vt2, u_ref, co_ref) = refs[:17]
        ride_out, (slab, u_scr) = refs[17:17 + N_RIDE_OUT], refs[17 + N_RIDE_OUT:]
        qkv_out = ((q0, k0, v0), (q1, k1, v1), (q2, k2, v2))
        tails = ((kt0, vt0), (kt1, vt1), (kt2, vt2))
        j = pl.program_id(1)
        _ride_block(pl.program_id(0) * nt + j, ride_in + ride_out)

    x = x_ref[0]
    ms = jnp.mean(x * x, axis=-1, keepdims=True)
    h = x * lax.rsqrt(ms + RMS_EPS) * ng_ref[...]
    h = h * (1.0 + sc_ref[0]) + sh_ref[0]
    proj = _dot(h.astype(BF16), w_ref[...])

    bd = bd_ref[...]

    def headnorm(z, g):
        msq = _dot((z * z).astype(BF16), bd)
        return z * lax.rsqrt(msq + RMS_EPS) * g

    slab_i = 0
    for g in range(N_GROUPS):
        c0 = g * GROUP_W
        qn = headnorm(proj[:, c0:c0 + GROUP_W], qg_ref[...]) * (HEAD_DIM ** -0.5)
        kn = headnorm(proj[:, ATTN_W + c0:ATTN_W + c0 + GROUP_W], kg_ref[...])
        vv = proj[:, 2 * ATTN_W + c0:2 * ATTN_W + c0 + GROUP_W]
        if sample:
            qf_ref[0, :, c0:c0 + GROUP_W] = qn
            kt_ref[c0:c0 + GROUP_W, :] = kn.T
            vt_ref[c0:c0 + GROUP_W, :] = vv.T
            continue
        keep = min(WINDOWS[g], tile)

        @pl.when(j >= nt - max(WINDOWS[g] // tile, 1))
        def _(kn=kn, vv=vv, g=g, keep=keep):
            tails[g][0][0] = kn[tile - keep:, :].T
            tails[g][1][0] = vv[tile - keep:, :].T

        d = dils[g]
        for val, out in zip((qn, kn, vv), qkv_out[g]):
            if d == 1:
                out[0, 0] = val.astype(BF16)
                continue
            n = tile // d
            for half in range(GROUP_W // LANES):
                slab[slab_i] = val[:, half * LANES:(half + 1) * LANES]
                for r in range(d):
                    out[0, r, :, half * LANES:(half + 1) * LANES] = (
                        slab[slab_i, pl.ds(r, n, stride=d), :].astype(BF16))
                slab_i += 1

    base = 3 * ATTN_W
    u = proj[:, base + 2 * CONV_CH:base + 3 * CONV_CH] * proj[:, base:base + CONV_CH]
    gate_b = proj[:, base + CONV_CH:base + 2 * CONV_CH]
    if sample:
        u_scr[0:8, :] = jnp.zeros((8, CONV_CH), F32)
    else:
        @pl.when(j == 0)
        def _():
            u_scr[0:8, :] = jnp.zeros((8, CONV_CH), F32)

        @pl.when(j > 0)
        def _():
            u_scr[0:8, :] = u_scr[tile:tile + 8, :]

    u_scr[8:tile + 8, :] = u
    um1 = u_scr[7:tile + 7, :]
    um2 = u_scr[6:tile + 6, :]
    if sample:
        t = lax.broadcasted_iota(jnp.int32, (tile, CONV_CH), 0) % 4
        um1 = jnp.where(t >= 1, um1, 0.0) + hb_ref[...]
        um2 = jnp.where(t >= 2, um2, 0.0) + ha_ref[...]
    cw = cw_ref[...]
    conv = cw[0:1] * um2 + cw[1:2] * um1 + cw[2:3] * u
    u_ref[0] = u if sample else u[tile - 8:, :]
    co_ref[0] = (gate_b * conv).astype(BF16)


def _proj(x, sc, sh, ng, w_bf, qg, kg, cw, bd, hist, *, tile, dils, sample, ride=None):
    B, S, _ = x.shape
    nt = S // tile
    tm = sc.shape[1]
    mod_spec = pl.BlockSpec((1, tm, D_MODEL), (lambda b, j: (b, j, 0)) if tm > 1 else (lambda b, j: (b, 0, 0)))
    const2 = lambda shape: pl.BlockSpec(shape, lambda b, j: (0, 0))
    in_specs = [pl.BlockSpec((1, tile, D_MODEL), lambda b, j: (b, j, 0)), mod_spec, mod_spec,
                const2((1, D_MODEL)), const2((D_MODEL, N_PROJ)), const2((1, GROUP_W)),
                const2((1, GROUP_W)), const2((3, CONV_CH)), const2((GROUP_W, GROUP_W))]
    args = [x, sc, sh, ng, w_bf, qg, kg, cw, bd]
    nat = lambda w: pl.BlockSpec((1, tile, w), lambda b, j: (b, j, 0))
    out_specs, out_shape = [], []
    scratch = []
    if sample:
        assert B == 1 and nt == 1
        in_specs += [pl.BlockSpec((tile, CONV_CH), lambda b, j: (j, 0))] * 2
        args += list(hist)
        out_specs += [nat(ATTN_W)] + [pl.BlockSpec((ATTN_W, tile), lambda b, j: (0, j))] * 2 + [nat(CONV_CH)]
        out_shape += [jax.ShapeDtypeStruct((B, S, ATTN_W), F32)]
        out_shape += [jax.ShapeDtypeStruct((ATTN_W, S), F32)] * 2
        out_shape += [jax.ShapeDtypeStruct((B, S, CONV_CH), F32)]
    else:
        for d in dils:
            for _ in range(3):
                out_specs.append(pl.BlockSpec((1, d, tile // d, GROUP_W), lambda b, j: (b, 0, j, 0)))
                out_shape.append(jax.ShapeDtypeStruct((B, d, S // d, GROUP_W), BF16))
        for w in WINDOWS:
            keep, first = min(w, tile), nt - max(w // tile, 1)
            for _ in range(2):
                out_specs.append(pl.BlockSpec((1, GROUP_W, keep),
                                              lambda b, j, first=first: (b, 0, jnp.maximum(j - first, 0))))
                out_shape.append(jax.ShapeDtypeStruct((B, GROUP_W, w), F32))
        out_specs.append(pl.BlockSpec((1, 8, CONV_CH), lambda b, j: (b, 0, 0)))
        out_shape.append(jax.ShapeDtypeStruct((B, 8, CONV_CH), F32))
        n_slabs = sum(3 * (GROUP_W // LANES) for d in dils if d > 1)
        scratch.append(pltpu.VMEM((n_slabs, tile, LANES), F32))
    out_specs.append(nat(CONV_CH))
    out_shape.append(jax.ShapeDtypeStruct((B, S, CONV_CH), BF16))
    scratch.append(pltpu.VMEM((tile + 8, CONV_CH), F32))
    if not sample:
        nb = ride[1].shape[0]
        r_in, r_out, r_shape = _ride_specs(ride, nb // (B * nt), lambda b, j: b * nt + j)
        in_specs, args = in_specs + r_in, args + list(ride)
        out_specs, out_shape = out_specs + r_out, out_shape + r_shape
    return pl.pallas_call(
        functools.partial(_proj_kernel, tile=tile, dils=dils, sample=sample, nt=nt),
        grid=(B, nt),
        in_specs=in_specs,
        out_specs=out_specs,
        out_shape=out_shape,
        scratch_shapes=scratch,
        compiler_params=_params(("arbitrary", "arbitrary")),
        name="proj_sample" if sample else "proj_prompt",
    )(*args)


def _attn_kernel(q_ref, kp_ref, kc_ref, vp_ref, vc_ref, b_ref, o_ref, l_ref, s_scr, m_scr, o_scr, z_scr,
                 *, nq):
    first = pl.program_id(1) == 0
    low = lax.broadcasted_iota(jnp.int32, (BLK, LANES), 1) < HEAD_DIM
    ones = jnp.ones((BLK, LANES), BF16)
    chains = [(j, pair, sub) for j in range(nq) for pair in range(GROUP_W // LANES) for sub in range(2)]

    def operands(j, pair):
        rows = slice(j * BLK, (j + 1) * BLK)
        sl = slice(pair * LANES, (pair + 1) * LANES)
        if j == 0:
            return rows, sl, kp_ref[0, :, sl], vp_ref[0, :, sl]
        prows = slice((j - 1) * BLK, j * BLK)
        return rows, sl, kc_ref[0, prows, sl], vc_ref[0, prows, sl]

    for c, (j, pair, sub) in enumerate(chains):
        rows, sl, kp, _ = operands(j, pair)
        q = q_ref[0, rows, sl]
        qm = jnp.where(low if sub == 0 else ~low, q, jnp.zeros_like(q))
        hh = 2 * pair + sub
        sp = _dot_nt(qm, kp) + b_ref[hh, :, 0:BLK]
        if j == 0:
            sp = jnp.where(first, NEG, sp)
        sc = _dot_nt(qm, kc_ref[0, rows, sl]) + b_ref[hh, :, BLK:2 * BLK]
        s_scr[c, :, 0:BLK] = sp
        s_scr[c, :, BLK:2 * BLK] = sc
        m_scr[c] = jnp.broadcast_to(jnp.max(jnp.maximum(sp, sc), axis=-1, keepdims=True), (BLK, LANES))

    for c, (j, pair, sub) in enumerate(chains):
        rows, sl, _, vp = operands(j, pair)
        m = m_scr[c]
        pp = jnp.exp(s_scr[c, :, 0:BLK] - m).astype(BF16)
        pc = jnp.exp(s_scr[c, :, BLK:2 * BLK] - m).astype(BF16)
        o_scr[c] = _dot(pp, vp) + _dot(pc, vc_ref[0, rows, sl])
        z_scr[c] = _dot(pp, ones) + _dot(pc, ones)

    for c in range(0, len(chains), 2):
        j, pair, _ = chains[c]
        rows = slice(j * BLK, (j + 1) * BLK)
        sl = slice(pair * LANES, (pair + 1) * LANES)
        o_sub = [o_scr[c + sub] / z_scr[c + sub] for sub in range(2)]
        l_sub = [m_scr[c + sub] + jnp.log(z_scr[c + sub]) for sub in range(2)]
        o_ref[0, rows, sl] = jnp.where(low, o_sub[0], o_sub[1])
        l_ref[0, rows, sl] = jnp.where(low, l_sub[0], l_sub[1])


def _attn_prompt(q, k, v, bias):
    Z, L, _ = q.shape
    qt = min(L, 8 * BLK)
    nq = qt // BLK
    cur = pl.BlockSpec((1, qt, GROUP_W), lambda z, i: (z, i, 0))
    prev = pl.BlockSpec((1, BLK, GROUP_W), lambda z, i: (z, jnp.maximum(i * nq - 1, 0), 0))
    return pl.pallas_call(
        functools.partial(_attn_kernel, nq=nq),
        grid=(Z, L // qt),
        in_specs=[cur, prev, cur, prev, cur,
                  pl.BlockSpec((HEADS_PER_GROUP, BLK, 2 * BLK), lambda z, i: (0, 0, 0))],
        out_specs=[cur, cur],
        out_shape=[jax.ShapeDtypeStruct((Z, L, GROUP_W), F32)] * 2,
        scratch_shapes=[pltpu.VMEM((nq * HEADS_PER_GROUP, BLK, 2 * BLK), F32)]
        + [pltpu.VMEM((nq * HEADS_PER_GROUP, BLK, LANES), F32)] * 3,
        compiler_params=_params(("arbitrary", "arbitrary")),
        name="attn_prompt",
    )(q, k, k, v, v, bias)


def _attn_sample_items(items, q_ref, kc_ref, vc_ref, nk_ref, nv_ref, b1_ref, b2_ref, ko_ref, vo_ref, o_ref,
                       l_ref):
    wb = kc_ref.shape[-1]
    lane = lax.broadcasted_iota(jnp.int32, (HEAD_DIM, LANES), 1)
    new_cols = lane >= LANES - 4
    per_tile = LANES // 4
    loaded, s1, s2 = [], [], []
    for gb, b, hh in items:
        tile_at = pl.ds(pl.multiple_of((gb // per_tile) * LANES, LANES), LANES)
        shift = (LANES - 4) - 4 * (gb % per_tile)
        q = q_ref[b, hh]
        kt = kc_ref[b, hh]
        vt = vc_ref[b, hh]
        nkt = pltpu.roll(nk_ref[hh, :, tile_at], shift, axis=1)
        nvt = pltpu.roll(nv_ref[hh, :, tile_at], shift, axis=1)
        loaded.append((kt, vt, nkt, nvt))
        s1.append(_dot(q, kt.astype(BF16)) + b1_ref[hh])
        s2.append(_dot(q, nkt.astype(BF16)) + b2_ref[hh])
    s1, s2 = jnp.concatenate(s1, axis=0), jnp.concatenate(s2, axis=0)
    m = jnp.maximum(jnp.max(s1, axis=-1, keepdims=True), jnp.max(s2, axis=-1, keepdims=True))
    p1 = jnp.exp(s1 - m)
    p2 = jnp.exp(s2 - m)
    z = jnp.sum(p1, axis=-1, keepdims=True) + jnp.sum(p2, axis=-1, keepdims=True)
    p1, p2 = p1.astype(BF16), p2.astype(BF16)
    lse = jnp.broadcast_to(m + jnp.log(z), (8 * len(items), LANES))
    for i, (gb, b, hh) in enumerate(items):
        kt, vt, nkt, nvt = loaded[i]
        r = slice(8 * i, 8 * i + 8)
        o = _dot_nt(p1[r], vt.astype(BF16)) + _dot_nt(p2[r], nvt.astype(BF16))
        o_ref[b, hh] = o / z[r]
        l_ref[b, hh] = lse[r]
        for src, new, dst in ((kt, nkt, ko_ref), (vt, nvt, vo_ref)):
            rolled = pltpu.roll(src, wb - 4, axis=1)
            if wb > LANES:
                dst[b, hh, :, 0:wb - LANES] = rolled[:, 0:wb - LANES]
            dst[b, hh, :, wb - LANES:wb] = jnp.where(new_cols, new, rolled[:, wb - LANES:wb])


N_RIDE_IN = 7
N_RIDE_OUT = 4


def _ride_specs(ride, bb, block_index):
    q, kc, vc, nk, nv, b1, b2 = ride
    nb, _, _, wb = kc.shape
    blk = lambda *tail: pl.BlockSpec((bb, HEADS_PER_GROUP) + tail, lambda *g: (block_index(*g), 0, 0, 0))
    const = lambda a: pl.BlockSpec(a.shape, lambda *g: (0,) * a.ndim)
    in_specs = [blk(8, HEAD_DIM), blk(HEAD_DIM, wb), blk(HEAD_DIM, wb), const(nk), const(nv), const(b1), const(b2)]
    out_specs = [blk(HEAD_DIM, wb), blk(HEAD_DIM, wb), blk(8, HEAD_DIM), blk(8, LANES)]
    out_shape = [jax.ShapeDtypeStruct(kc.shape, F32), jax.ShapeDtypeStruct(kc.shape, F32),
                 jax.ShapeDtypeStruct((nb, HEADS_PER_GROUP, 8, HEAD_DIM), F32),
                 jax.ShapeDtypeStruct((nb, HEADS_PER_GROUP, 8, LANES), F32)]
    return in_specs, out_specs, out_shape


def _ride_block(step, ride_refs):
    bb = ride_refs[0].shape[0]
    _attn_sample_items([(step * bb + b, b, hh) for b in range(bb) for hh in range(HEADS_PER_GROUP)], *ride_refs)


def _first_max(v, ids, sentinel):
    m = jnp.max(v, axis=0, keepdims=True)
    idx = jnp.min(jnp.where(v == m, ids, sentinel), axis=0, keepdims=True)
    return m, ids == idx


def _oproj_kernel(*refs, tile, dils, ride):
    (x_ref, o0, o1, o2, l0, l1, l2, co_ref, g1_ref, sc_ref, sh_ref, ng_ref, wo_ref, wr_ref, br_ref) = refs[:15]
    refs = refs[15:]
    if ride:
        ride_in, refs = refs[:N_RIDE_IN], refs[N_RIDE_IN:]
        ride_out = refs[6:6 + N_RIDE_OUT]
    x1_ref, xg_ref, gt_ref, posr_ref, post_ref, cnt_ref = refs[:6]
    slab = refs[-1]
    o_refs, l_refs = (o0, o1, o2), (l0, l1, l2)
    outs, lses = [], []
    slab_i = 0
    for g, d in enumerate(dils):
        if d == 1:
            outs.append(o_refs[g][0, 0])
            lses.append(l_refs[g][0, 0])
            continue
        n = tile // d
        for ref, dest in ((o_refs[g], outs), (l_refs[g], lses)):
            halves = []
            for half in range(GROUP_W // LANES):
                for r in range(d):
                    slab[slab_i, pl.ds(r, n, stride=d), :] = ref[0, r, :, half * LANES:(half + 1) * LANES]
                halves.append(slab[slab_i])
                slab_i += 1
            dest.append(jnp.concatenate(halves, axis=1))

    m = jnp.maximum(jnp.maximum(lses[0], lses[1]), lses[2])
    es = [jnp.exp(l - m) for l in lses]
    den = es[0] + es[1] + es[2]
    cat = [(outs[g] * (es[g] / den)).astype(BF16) for g in range(N_GROUPS)]
    cat.append(co_ref[0].astype(BF16))
    cat = jnp.concatenate(cat, axis=1)
    x1 = x_ref[0] + g1_ref[0] * _dot(cat, wo_ref[...])
    x1_ref[0] = x1

    ms = jnp.mean(x1 * x1, axis=-1, keepdims=True)
    h2 = x1 * lax.rsqrt(ms + RMS_EPS) * ng_ref[...]
    h2 = h2 * (1.0 + sc_ref[0]) + sh_ref[0]
    hh = h2.astype(BF16)
    xg_ref[0, :, 0:D_MODEL] = hh

    hl = (h2 - hh.astype(F32)).astype(BF16)
    wr = wr_ref[...]
    wh = wr.astype(BF16)
    wl = (wr - wh.astype(F32)).astype(BF16)
    logits = _dot_nt(wh, hh) + _dot_nt(wh, hl) + _dot_nt(wl, hh)
    scores = 1.0 / (1.0 + jnp.exp(-logits))
    sel = scores + br_ref[...]

    ids = lax.broadcasted_iota(jnp.int32, (GROUP_SIZE, tile), 0)
    ninf = -jnp.inf
    sel_g = [sel[g * GROUP_SIZE:(g + 1) * GROUP_SIZE] for g in range(N_ROUTE_GROUPS)]
    gscore = jnp.zeros((N_ROUTE_GROUPS, tile), F32)
    for g in range(N_ROUTE_GROUPS):
        m1, oh = _first_max(sel_g[g], ids, GROUP_SIZE)
        m2 = jnp.max(jnp.where(oh, ninf, sel_g[g]), axis=0, keepdims=True)
        gscore = jnp.where(ids == g, m1 + m2, gscore)
    gsel = jnp.zeros((N_ROUTE_GROUPS, tile), F32)
    for _ in range(TOPK_GROUPS):
        _, oh = _first_max(gscore, ids, N_ROUTE_GROUPS)
        gsel = jnp.where(oh, 1.0, gsel)
        gscore = jnp.where(oh, ninf, gscore)
    cand = [jnp.where(gsel[g:g + 1] > 0.0, sel_g[g], ninf) for g in range(N_ROUTE_GROUPS)]
    chosen = [jnp.zeros((GROUP_SIZE, tile), F32) for _ in range(N_ROUTE_GROUPS)]
    picks = []
    for _ in range(TOP_K):
        mx = cand[0]
        for g in range(1, N_ROUTE_GROUPS):
            mx = jnp.maximum(mx, cand[g])
        mx = jnp.max(mx, axis=0, keepdims=True)
        idx = jnp.where(cand[0] == mx, ids, N_EXPERTS)
        for g in range(1, N_ROUTE_GROUPS):
            idx = jnp.minimum(idx, jnp.where(cand[g] == mx, ids + g * GROUP_SIZE, N_EXPERTS))
        idx = jnp.min(idx, axis=0, keepdims=True)
        ohs = [(ids + g * GROUP_SIZE) == idx for g in range(N_ROUTE_GROUPS)]
        picks.append(ohs)
        for g in range(N_ROUTE_GROUPS):
            chosen[g] = jnp.where(ohs[g], 1.0, chosen[g])
            cand[g] = jnp.where(ohs[g], ninf, cand[g])
    wts = [jnp.where(chosen[g] > 0.0, scores[g * GROUP_SIZE:(g + 1) * GROUP_SIZE], 0.0)
           for g in range(N_ROUTE_GROUPS)]
    wsum = jnp.sum(wts[0], axis=0, keepdims=True)
    for g in range(1, N_ROUTE_GROUPS):
        wsum = wsum + jnp.sum(wts[g], axis=0, keepdims=True)
    gates_t = jnp.concatenate([w / wsum * ROUTED_SCALE for w in wts], axis=0)
    gt_ref[0] = jnp.concatenate([gates_t, jnp.zeros((LANES - N_EXPERTS, tile), F32)], axis=0).T
    g_hi = gates_t.astype(BF16).astype(F32)
    xg_ref[0, :, D_MODEL:D_MODEL + LANES] = jnp.concatenate([g_hi, gates_t - g_hi], axis=0).T.astype(BF16)

    er = lax.broadcasted_iota(jnp.int32, (N_EXPERTS, N_EXPERTS), 0)
    ec = lax.broadcasted_iota(jnp.int32, (N_EXPERTS, N_EXPERTS), 1)
    lower = jnp.where(ec < er, 1.0, 0.0).astype(BF16)
    tr = lax.broadcasted_iota(jnp.int32, (SUB, SUB), 0)
    tc = lax.broadcasted_iota(jnp.int32, (SUB, SUB), 1)
    upper = jnp.where(tr < tc, 1.0, 0.0).astype(BF16)
    for u in range(tile // SUB):
        cols = slice(u * SUB, (u + 1) * SUB)
        ch = jnp.concatenate([c[:, cols] for c in chosen], axis=0)
        cnt = jnp.sum(ch, axis=1, keepdims=True)
        cpad = jnp.broadcast_to(jnp.ceil(cnt / CHUNK) * CHUNK, (N_EXPERTS, LANES))
        cnt_ref[u] = cpad
        seg0 = _dot(lower, cpad.astype(BF16))[:, 0:1]
        slot = seg0 + _dot(ch.astype(BF16), upper)
        rows = []
        for k in range(TOP_K):
            acc = jnp.where(picks[k][0][:, cols], slot[0:GROUP_SIZE], 0.0)
            for g in range(1, N_ROUTE_GROUPS):
                acc = acc + jnp.where(picks[k][g][:, cols], slot[g * GROUP_SIZE:(g + 1) * GROUP_SIZE], 0.0)
            rows.append(jnp.sum(acc, axis=0, keepdims=True))
        posr = jnp.concatenate(rows, axis=0)
        posr_ref[u] = posr
        post_ref[0, cols, :] = jnp.concatenate([posr, jnp.zeros((LANES - TOP_K, SUB), F32)], axis=0).T
    if ride:
        _ride_block(pl.program_id(0) * pl.num_programs(1) + pl.program_id(1), ride_in + ride_out)


def _oproj(x, o_list, l_list, co, g1, sc, sh, ng, wo_bf, wr_t, br, *, tile, dils, ride=None):
    B, S, _ = x.shape
    nt = S // tile
    r_in, r_out, r_shape, r_args = [], [], [], []
    if ride is not None:
        r_in, r_out, r_shape = _ride_specs(ride, ride[1].shape[0] // (B * nt), lambda b, j: b * nt + j)
        r_args = list(ride)
    tm = g1.shape[1]
    mod_spec = pl.BlockSpec((1, tm, D_MODEL), (lambda b, j: (b, j, 0)) if tm > 1 else (lambda b, j: (b, 0, 0)))
    const2 = lambda shape: pl.BlockSpec(shape, lambda b, j: (0, 0))
    nat = lambda w: pl.BlockSpec((1, tile, w), lambda b, j: (b, j, 0))
    dspec = [pl.BlockSpec((1, d, tile // d, GROUP_W), lambda b, j: (b, 0, j, 0)) for d in dils]
    n_slabs = sum(2 * (GROUP_W // LANES) for d in dils if d > 1)
    nsub_t = tile // SUB
    return pl.pallas_call(
        functools.partial(_oproj_kernel, tile=tile, dils=dils, ride=ride is not None),
        grid=(B, nt),
        in_specs=[nat(D_MODEL)] + dspec + dspec + [nat(CONV_CH), mod_spec, mod_spec, mod_spec,
                  const2((1, D_MODEL)), const2((D_MODEL, D_MODEL)), const2((N_EXPERTS, D_MODEL)),
                  const2((N_EXPERTS, 1))] + r_in,
        out_specs=[nat(D_MODEL), nat(XG_W), nat(LANES),
                   pl.BlockSpec((nsub_t, TOP_K, SUB), lambda b, j: (b * nt + j, 0, 0)),
                   nat(LANES),
                   pl.BlockSpec((nsub_t, N_EXPERTS, LANES), lambda b, j: (b * nt + j, 0, 0))] + r_out,
        out_shape=[jax.ShapeDtypeStruct((B, S, D_MODEL), F32), jax.ShapeDtypeStruct((B, S, XG_W), BF16),
                   jax.ShapeDtypeStruct((B, S, LANES), F32),
                   jax.ShapeDtypeStruct((B * S // SUB, TOP_K, SUB), F32),
                   jax.ShapeDtypeStruct((B, S, LANES), F32),
                   jax.ShapeDtypeStruct((B * S // SUB, N_EXPERTS, LANES), F32)] + r_shape,
        scratch_shapes=[pltpu.VMEM((max(n_slabs, 1), tile, LANES), F32)],
        compiler_params=_params(("arbitrary", "arbitrary")),
        name="oproj",
    )(x, *o_list, *l_list, co, g1, sc, sh, ng, wo_bf, wr_t, br, *r_args)


DENSE_EXPERTS_PER_STEP = 4


def _moe_kernel(h_ref, gt_ref, x1_ref, g2_ref, wg_ref, wu_ref, wd_ref, wgs_ref, wus_ref, wds_ref,
                y_ref, acc, *, tile):
    step = pl.program_id(2)
    h = h_ref[0]

    @pl.when(step == 0)
    def _():
        a = _dot(h, wgs_ref[...].astype(BF16))
        b = _dot(h, wus_ref[...].astype(BF16))
        acc[...] = _dot((_silu(a) * b).astype(BF16), wds_ref[...].astype(BF16))

    lane = lax.broadcasted_iota(jnp.int32, (tile, LANES), 1)
    routed = None
    for j in range(DENSE_EXPERTS_PER_STEP):
        e = step * DENSE_EXPERTS_PER_STEP + j
        a = _dot(h, wg_ref[j].astype(BF16))
        b = _dot(h, wu_ref[j].astype(BF16))
        gcol = jnp.sum(jnp.where(lane == e, gt_ref[0], 0.0), axis=1, keepdims=True)
        hc = _silu(a) * b * gcol
        y = _dot(hc.astype(BF16), wd_ref[j].astype(BF16))
        routed = y if routed is None else routed + y
    acc[...] += routed

    @pl.when(step == pl.num_programs(2) - 1)
    def _():
        y_ref[0] = x1_ref[0] + g2_ref[0] * acc[...]


def _moe(h2, gates, x1, g2, wg, wu, wd, wgs, wus, wds, *, tile):
    B, S, _ = h2.shape
    tm = g2.shape[1]
    mod_spec = pl.BlockSpec((1, tm, D_MODEL),
                            (lambda b, j, e: (b, j, 0)) if tm > 1 else (lambda b, j, e: (b, 0, 0)))
    nat = lambda w: pl.BlockSpec((1, tile, w), lambda b, j, e: (b, j, 0))
    const2 = lambda shape: pl.BlockSpec(shape, lambda b, j, e: (0, 0))
    return pl.pallas_call(
        functools.partial(_moe_kernel, tile=tile),
        grid=(B, S // tile, N_EXPERTS // DENSE_EXPERTS_PER_STEP),
        in_specs=[nat(D_MODEL), nat(LANES), nat(D_MODEL), mod_spec,
                  pl.BlockSpec((DENSE_EXPERTS_PER_STEP, D_MODEL, D_EXPERT), lambda b, j, e: (e, 0, 0)),
                  pl.BlockSpec((DENSE_EXPERTS_PER_STEP, D_MODEL, D_EXPERT), lambda b, j, e: (e, 0, 0)),
                  pl.BlockSpec((DENSE_EXPERTS_PER_STEP, D_EXPERT, D_MODEL), lambda b, j, e: (e, 0, 0)),
                  const2((D_MODEL, D_EXPERT)), const2((D_MODEL, D_EXPERT)), const2((D_EXPERT, D_MODEL))],
        out_specs=nat(D_MODEL),
        out_shape=jax.ShapeDtypeStruct((B, S, D_MODEL), F32),
        scratch_shapes=[pltpu.VMEM((tile, D_MODEL), F32)],
        compiler_params=_params(("arbitrary", "arbitrary", "arbitrary")),
        name="moe",
    )(h2, gates, x1, g2, wg, wu, wd, wgs, wus, wds)


FFN_TM = 1024
SUB_ROWS = SUB * TOP_K + N_EXPERTS * CHUNK
N_CHUNKS = SUB_ROWS // CHUNK
ZERO_CHUNK = N_CHUNKS
TABLE_W = 256
MXU_ROWS = 256
BLOCKS_PER_TRIP = 4


def _slot_onehot(first, pos_list, axis_iota):
    shape = axis_iota.shape
    hit = axis_iota == jnp.broadcast_to(pos_list[0] - first, shape).astype(BF16)
    for p in pos_list[1:]:
        hit = hit | (axis_iota == jnp.broadcast_to(p - first, shape).astype(BF16))
    return jnp.where(hit, jnp.ones(shape, BF16), jnp.zeros(shape, BF16))


def _chunk_copy(src, src_chunk, dst, dst_chunk, sem):
    return pltpu.make_async_copy(src.at[src_chunk], dst.at[dst_chunk], sem)


def _for_each(n, body):
    def two(t, carry):
        body(2 * t)
        body(2 * t + 1)
        return carry

    lax.fori_loop(0, n // 2, two, 0)

    @pl.when(n % 2 == 1)
    def _():
        body(n - 1)


def _dispatch_kernel(nblk_ref, nreal_ref, ncopy_ref, ntile_ref, src_ref, dst_ref, posr_ref, xg_ref, xs_hbm,
                     xs_scr, zero_scr, sem, tail_sem, *, nsub, n_tiles_max):
    s = pl.program_id(0)

    tile_chunks = FFN_TM // CHUNK
    blk_chunks = MXU_ROWS // CHUNK
    zero_scr[...] = jnp.zeros((tile_chunks, CHUNK, XG_W), BF16)
    tail_blocks = [ntile_ref[0] + s + r * nsub for r in range(-(-(n_tiles_max) // nsub))]

    def tail_copy(t):
        return pltpu.make_async_copy(
            zero_scr, xs_hbm.at[pl.ds(pl.multiple_of(t * tile_chunks, tile_chunks), tile_chunks)], tail_sem)

    for t in tail_blocks:
        @pl.when(t < n_tiles_max)
        def _():
            tail_copy(t).start()

    xg = xg_ref[...]
    pos = [posr_ref[0, k:k + 1, :] for k in range(TOP_K)]
    rows = lax.broadcasted_iota(jnp.int32, (MXU_ROWS, SUB), 0).astype(BF16)
    xs_scr[ZERO_CHUNK] = jnp.zeros((CHUNK, XG_W), BF16)

    def sort_blocks(i, carry):
        for j in range(BLOCKS_PER_TRIP):
            blk = i * BLOCKS_PER_TRIP + j
            onehot = _slot_onehot((blk * MXU_ROWS).astype(F32), pos, rows)
            sorted_rows = _dot(onehot, xg).astype(BF16)
            xs_scr[pl.ds(pl.multiple_of(blk * blk_chunks, blk_chunks), blk_chunks)] = (
                sorted_rows.reshape(blk_chunks, CHUNK, XG_W))
        k0 = i * trip_chunks
        _for_each(jnp.clip(n_real - k0, 0, trip_chunks), lambda t: send(k0 + t))
        return carry

    n_real = nreal_ref[s]
    n = ncopy_ref[s]
    trip_chunks = BLOCKS_PER_TRIP * blk_chunks

    def send(i):
        _chunk_copy(xs_scr, src_ref[0, 0, i], xs_hbm, dst_ref[0, 0, i], sem).start()

    lax.fori_loop(0, nblk_ref[s], sort_blocks, 0)
    _for_each(n - n_real, lambda t: send(n_real + t))
    _for_each(n, lambda i: _chunk_copy(xs_scr, 0, xs_hbm, 0, sem).wait())
    for t in tail_blocks:
        @pl.when(t < n_tiles_max)
        def _():
            tail_copy(t).wait()


def _dispatch(nblk, n_real, ncopy, n_tiles, src, dst, posr, xg, n_rows):
    nsub = posr.shape[0]
    tab = pl.BlockSpec((1, 1, TABLE_W), lambda s, *_: (s, 0, 0), memory_space=pltpu.SMEM)
    return pl.pallas_call(
        functools.partial(_dispatch_kernel, nsub=nsub, n_tiles_max=n_rows // FFN_TM),
        grid_spec=pltpu.PrefetchScalarGridSpec(
            num_scalar_prefetch=4, grid=(nsub,),
            in_specs=[tab, tab, pl.BlockSpec((1, TOP_K, SUB), lambda s, *_: (s, 0, 0)),
                      pl.BlockSpec((SUB, XG_W), lambda s, *_: (s, 0))],
            out_specs=pl.BlockSpec(memory_space=pl.ANY),
            scratch_shapes=[pltpu.VMEM((N_CHUNKS + 1, CHUNK, XG_W), BF16),
                            pltpu.VMEM((FFN_TM // CHUNK, CHUNK, XG_W), BF16),
                            pltpu.SemaphoreType.DMA(()), pltpu.SemaphoreType.DMA(())]),
        out_shape=jax.ShapeDtypeStruct((n_rows // CHUNK, CHUNK, XG_W), BF16),
        compiler_params=_params(("arbitrary",)),
        name="moe_dispatch",
    )(nblk, n_real, ncopy, n_tiles, src, dst, posr, xg).reshape(n_rows, XG_W)


def _ffn_kernel(te_ref, nt_ref, xs_ref, wg_ref, wu_ref, wd_ref,
                q_ref, kc_ref, vc_ref, nk_ref, nv_ref, b1_ref, b2_ref,
                ys_ref, ko_ref, vo_ref, o_ref, l_ref, wg_s, wu_s, wd_s, *, steps_per_batch):
    i = pl.program_id(0)
    n_heads = HEADS_PER_GROUP // steps_per_batch
    first_head = (i % steps_per_batch) * n_heads

    def ride():
        _attn_sample_items([(i // steps_per_batch, 0, first_head + j) for j in range(n_heads)],
                           q_ref, kc_ref, vc_ref, nk_ref, nv_ref, b1_ref, b2_ref, ko_ref, vo_ref, o_ref, l_ref)

    @pl.when(i >= nt_ref[0])
    def _():
        ride()
        ys_ref[...] = jnp.zeros((FFN_TM, D_MODEL), BF16)

    @pl.when(i < nt_ref[0])
    def _():
        e = te_ref[i]

        @pl.when((i == 0) | (e != te_ref[jnp.maximum(i - 1, 0)]))
        def _():
            wg_s[...] = wg_ref[0].astype(BF16)
            wu_s[...] = wu_ref[0].astype(BF16)
            wd_s[...] = wd_ref[0].astype(BF16)

        ride()
        x = xs_ref[:, 0:D_MODEL]
        g = xs_ref[:, D_MODEL:XG_W].astype(F32)
        lane = lax.broadcasted_iota(jnp.int32, (FFN_TM, LANES), 1)
        gcol = jnp.sum(jnp.where((lane == e) | (lane == e + N_EXPERTS), g, 0.0), axis=1, keepdims=True)
        a = _dot(x, wg_s[...])
        b = _dot(x, wu_s[...])
        hc = _silu(a) * b * gcol
        ys_ref[...] = _dot(hc.astype(BF16), wd_s[...]).astype(BF16)


def _ffn(tile_expert, n_tiles, xs, wg, wu, wd, ride):
    q, kc, vc, nk, nv, b1, b2 = ride
    nb, _, _, wb = kc.shape
    n_rows = xs.shape[0]
    n_steps = n_rows // FFN_TM
    assert n_steps % nb == 0 and HEADS_PER_GROUP % (n_steps // nb) == 0
    per = n_steps // nb
    row = lambda i, te, nt: (jnp.minimum(i, nt[0] - 1), 0)
    wspec = lambda shape: pl.BlockSpec((1,) + shape, lambda i, te, nt: (te[i], 0, 0))
    rblk = lambda *tail: pl.BlockSpec((1, HEADS_PER_GROUP) + tail, lambda i, te, nt: (i // per, 0, 0, 0))
    const = lambda shape: pl.BlockSpec(shape, lambda i, te, nt: (0, 0, 0))
    return pl.pallas_call(
        functools.partial(_ffn_kernel, steps_per_batch=per),
        grid_spec=pltpu.PrefetchScalarGridSpec(
            num_scalar_prefetch=2, grid=(n_steps,),
            in_specs=[pl.BlockSpec((FFN_TM, XG_W), row), wspec((D_MODEL, D_EXPERT)),
                      wspec((D_MODEL, D_EXPERT)), wspec((D_EXPERT, D_MODEL)),
                      rblk(8, HEAD_DIM), rblk(HEAD_DIM, wb), rblk(HEAD_DIM, wb), const(nk.shape), const(nv.shape),
                      const(b1.shape), const(b2.shape)],
            out_specs=[pl.BlockSpec((FFN_TM, D_MODEL), lambda i, te, nt: (i, 0)),
                       rblk(HEAD_DIM, wb), rblk(HEAD_DIM, wb), rblk(8, HEAD_DIM), rblk(8, LANES)],
            scratch_shapes=[pltpu.VMEM((D_MODEL, D_EXPERT), BF16), pltpu.VMEM((D_MODEL, D_EXPERT), BF16),
                            pltpu.VMEM((D_EXPERT, D_MODEL), BF16)]),
        out_shape=[jax.ShapeDtypeStruct((n_rows, D_MODEL), BF16),
                   jax.ShapeDtypeStruct(kc.shape, F32), jax.ShapeDtypeStruct(kc.shape, F32),
                   jax.ShapeDtypeStruct((nb, HEADS_PER_GROUP, 8, HEAD_DIM), F32),
                   jax.ShapeDtypeStruct((nb, HEADS_PER_GROUP, 8, LANES), F32)],
        compiler_params=_params(("arbitrary",)),
        name="moe_ffn",
    )(tile_expert, n_tiles, xs, wg, wu, wd, q, kc, vc, nk, nv, b1, b2)


def _combine_kernel(nblk_ref, ncopy_ref, src_ref, post_ref, xg_ref, x1_ref, g2_ref, wgs_ref, wus_ref,
                    wds_ref, ys_hbm, y_ref, ys_scr, wgs_s, wus_s, wds_s, sem):
    s = pl.program_id(0)
    n = ncopy_ref[s]
    blk_chunks = MXU_ROWS // CHUNK
    trip_chunks = BLOCKS_PER_TRIP * blk_chunks

    def fetch_trip(t, carry):
        k0 = t * trip_chunks
        _for_each(trip_chunks,
                  lambda u: _chunk_copy(ys_hbm, src_ref[0, 0, k0 + u], ys_scr, k0 + u, sem.at[t]).start())
        return carry

    lax.fori_loop(0, nblk_ref[s], fetch_trip, 0)

    @pl.when(s == 0)
    def _():
        wgs_s[...] = wgs_ref[...].astype(BF16)
        wus_s[...] = wus_ref[...].astype(BF16)
        wds_s[...] = wds_ref[...].astype(BF16)

    h = xg_ref[:, 0:D_MODEL]
    a = _dot(h, wgs_s[...])
    b = _dot(h, wus_s[...])
    shared = _dot((_silu(a) * b).astype(BF16), wds_s[...])

    post = post_ref[...]
    pos = [jnp.broadcast_to(post[:, k:k + 1], (SUB, MXU_ROWS)) for k in range(TOP_K)]
    lanes = lax.broadcasted_iota(jnp.int32, (SUB, MXU_ROWS), 1).astype(BF16)

    def gather_blocks(i, acc):
        _for_each(trip_chunks, lambda t: _chunk_copy(ys_hbm, 0, ys_scr, 0, sem.at[i]).wait())
        for j in range(BLOCKS_PER_TRIP):
            blk = i * BLOCKS_PER_TRIP + j
            onehot = _slot_onehot((blk * MXU_ROWS).astype(F32), pos, lanes)
            rows = ys_scr[pl.ds(pl.multiple_of(blk * blk_chunks, blk_chunks), blk_chunks)]
            acc = acc + _dot(onehot, rows.reshape(MXU_ROWS, D_MODEL))
        return acc

    routed = lax.fori_loop(0, nblk_ref[s], gather_blocks, jnp.zeros((SUB, D_MODEL), F32))
    y_ref[...] = x1_ref[...] + g2_ref[0] * (routed + shared)


def _combine(nblk, ncopy, src, post, xg, x1, g2, wgs, wus, wds, ys):
    n_tok = x1.shape[0]
    nsub = n_tok // SUB
    per_b = nsub // g2.shape[0]
    tab = pl.BlockSpec((1, 1, TABLE_W), lambda s, *_: (s, 0, 0), memory_space=pltpu.SMEM)
    const2 = lambda shape: pl.BlockSpec(shape, lambda s, *_: (0, 0))
    return pl.pallas_call(
        _combine_kernel,
        grid_spec=pltpu.PrefetchScalarGridSpec(
            num_scalar_prefetch=2, grid=(nsub,),
            in_specs=[tab, pl.BlockSpec((SUB, LANES), lambda s, *_: (s, 0)),
                      pl.BlockSpec((SUB, XG_W), lambda s, *_: (s, 0)),
                      pl.BlockSpec((SUB, D_MODEL), lambda s, *_: (s, 0)),
                      pl.BlockSpec((1, 1, D_MODEL), lambda s, *_: (s // per_b, 0, 0)),
                      const2((D_MODEL, D_EXPERT)), const2((D_MODEL, D_EXPERT)), const2((D_EXPERT, D_MODEL)),
                      pl.BlockSpec(memory_space=pl.ANY)],
            out_specs=pl.BlockSpec((SUB, D_MODEL), lambda s, *_: (s, 0)),
            scratch_shapes=[pltpu.VMEM((N_CHUNKS, CHUNK, D_MODEL), BF16), pltpu.VMEM((D_MODEL, D_EXPERT), BF16),
                            pltpu.VMEM((D_MODEL, D_EXPERT), BF16), pltpu.VMEM((D_EXPERT, D_MODEL), BF16),
                            pltpu.SemaphoreType.DMA((SUB_ROWS // (BLOCKS_PER_TRIP * MXU_ROWS),))]),
        out_shape=jax.ShapeDtypeStruct((n_tok, D_MODEL), F32),
        compiler_params=_params(("arbitrary",)),
        name="moe_combine",
    )(nblk, ncopy, src, post, xg, x1, g2, wgs, wus, wds, ys.reshape(-1, CHUNK, D_MODEL))


def _route_tables(cnt):
    nsub = cnt.shape[0]
    i32 = jnp.int32
    seg_end = jnp.cumsum(cnt, axis=1)
    seg_start = seg_end - cnt
    tot = jnp.sum(cnt, axis=0)
    tot_al = (tot + FFN_TM - 1) // FFN_TM * FFN_TM
    reg_end = jnp.cumsum(tot_al)
    reg_start = reg_end - tot_al
    base = reg_start[None, :] + jnp.cumsum(cnt, axis=0) - cnt
    n_real = seg_end[:, -1] // CHUNK
    k = jnp.arange(TABLE_W, dtype=i32)
    owner = jnp.sum((seg_end[:, None, :] // CHUNK) <= k[None, :, None], axis=2)
    owner = jnp.minimum(owner, N_EXPERTS - 1)
    delta = (base - seg_start) // CHUNK
    e_ids = jnp.arange(N_EXPERTS, dtype=i32)
    real_dst = jnp.sum(jnp.where(owner[:, :, None] == e_ids, delta[:, None, :], 0), axis=2) + k[None, :]
    fill_n = (tot_al - tot) // CHUNK
    fill_dst0 = (reg_start + tot) // CHUNK
    src = jnp.broadcast_to(k[None, :], (nsub, TABLE_W))
    dst = real_dst
    ncopy = n_real
    for r in range(-(-N_EXPERTS // nsub)):
        e_of = jnp.arange(nsub, dtype=i32) + r * nsub
        ok = e_of < N_EXPERTS
        e_cl = jnp.minimum(e_of, N_EXPERTS - 1)
        fn = jnp.where(ok, fill_n[e_cl], 0)
        j = k[None, :] - ncopy[:, None]
        is_fill = (j >= 0) & (j < fn[:, None])
        src = jnp.where(is_fill, ZERO_CHUNK, src)
        dst = jnp.where(is_fill, fill_dst0[e_cl][:, None] + j, dst)
        ncopy = ncopy + fn
    trip_rows = BLOCKS_PER_TRIP * MXU_ROWS
    disp_nblk = (n_real * CHUNK + trip_rows - 1) // trip_rows
    comb_n = disp_nblk * (trip_rows // CHUNK)
    comb_src = jnp.where(k[None, :] < n_real[:, None], real_dst, real_dst[:, 0:1])
    n_tiles = reg_end[-1] // FFN_TM
    t = jnp.arange((nsub * SUB_ROWS + N_EXPERTS * FFN_TM) // FFN_TM, dtype=i32)
    tile_expert = jnp.minimum(jnp.sum((reg_end[None, :] // FFN_TM) <= t[:, None], axis=1), N_EXPERTS - 1)
    as3 = lambda a: a.astype(i32).reshape(nsub, 1, TABLE_W)
    return (disp_nblk.astype(i32), n_real.astype(i32), ncopy.astype(i32), as3(src), as3(dst),
            comb_n.astype(i32), as3(comb_src), tile_expert.astype(i32), n_tiles.astype(i32).reshape(1))


def _moe_sparse(xg, post, posr, cnt, x1, g2, wg, wu, wd, wgs, wus, wds, ride):
    B, S, _ = x1.shape
    n_tok = B * S
    nsub = n_tok // SUB
    (disp_nblk, n_real, disp_n, disp_src, disp_dst, comb_n, comb_src, tile_expert,
     n_tiles) = _route_tables(cnt[:, :, 0].astype(jnp.int32))
    n_rows = nsub * SUB_ROWS + N_EXPERTS * FFN_TM
    xg2 = xg.reshape(n_tok, XG_W)
    xs = _dispatch(disp_nblk, n_real, disp_n, n_tiles, disp_src, disp_dst, posr, xg2, n_rows)
    ys, *ride_out = _ffn(tile_expert, n_tiles, xs, wg, wu, wd, ride)
    y = _combine(disp_nblk, comb_n, comb_src, post.reshape(n_tok, LANES), xg2, x1.reshape(n_tok, D_MODEL),
                 g2, wgs, wus, wds, ys)
    return y.reshape(B, S, D_MODEL), ride_out


def _t5_bucket(dist):
    max_exact = N_BUCKETS // 2
    df = jnp.maximum(dist, 1).astype(F32)
    large = max_exact + (jnp.log(df / max_exact) / math.log(MAX_DISTANCE / max_exact)
                         * (N_BUCKETS - max_exact)).astype(jnp.int32)
    large = jnp.minimum(large, N_BUCKETS - 1)
    return jnp.where(dist < max_exact, dist, large)


def _step_bias(rel_bias, g):
    dist = jnp.arange(N_KEYS, dtype=jnp.int32) * DILATIONS[g]
    cols = rel_bias[:, g * HEADS_PER_GROUP:(g + 1) * HEADS_PER_GROUP]
    return cols[_t5_bucket(dist)].T.astype(F32)


def _prompt_bias(bias_k):
    h = bias_k.shape[0]
    n = 3 * BLK
    row = jnp.concatenate([bias_k[:, ::-1], jnp.full((h, n - N_KEYS), NEG, F32)], axis=1)
    t = jnp.tile(row, (1, BLK))[:, :BLK * (n - 1)].reshape(h, BLK, n - 1)
    return t[:, :, :2 * BLK]


def _sample_bias(bias_k, wb, d):
    h = bias_k.shape[0]
    rev = bias_k[:, :0:-1]
    rows = []
    for t in range(4):
        if d == 1:
            rows.append(jnp.concatenate([jnp.full((h, t), NEG, F32), rev[:, :wb - t]], axis=1))
        else:
            r = jnp.arange(d)[None, None, :]
            rows.append(jnp.where(r == t, rev[:, :, None], NEG).reshape(h, wb))
    b1 = jnp.concatenate([jnp.stack(rows, axis=1), jnp.zeros((h, 4, wb), F32)], axis=1)
    rows = []
    for t in range(4):
        cols = [bias_k[:, (t - tn) // d] if (t >= tn and (t - tn) % d == 0) else jnp.full((h,), NEG, F32)
                for tn in range(4)]
        rows.append(jnp.concatenate([jnp.full((h, LANES - 4), NEG, F32), jnp.stack(cols, axis=1)], axis=1))
    b2 = jnp.concatenate([jnp.stack(rows, axis=1), jnp.zeros((h, 4, LANES), F32)], axis=1)
    return b1, b2


TILE = 512
RIDE_ON_OPROJ, RIDE_ON_PROJ, RIDE_ON_FFN = 0, 1, 2


def _sample_front(x, mods, hist, caches, rel_bias, weights):
    norm1_g, _, w_in_bf, qg, kg, conv_w, bd = weights[:7]
    sh1, sc1 = mods[:2]
    S = x.shape[1]
    nb = S // 4
    qf, kt, vt, u, co = _proj(x, sc1, sh1, norm1_g, w_in_bf, qg, kg, conv_w, bd, hist,
                              tile=TILE, dils=(1, 1, 1), sample=True)
    rides = []
    for g in range(N_GROUPS):
        d, wb = DILATIONS[g], WINDOWS[g]
        kc, vc = caches[g]
        b1, b2 = _sample_bias(_step_bias(rel_bias, g), wb, d)
        q = qf[0, :, g * GROUP_W:(g + 1) * GROUP_W].reshape(nb, 4, HEADS_PER_GROUP, HEAD_DIM)
        q = jnp.pad(q.transpose(0, 2, 1, 3), ((0, 0), (0, 0), (0, 4), (0, 0))).astype(BF16)
        new_t = lambda a: a[g * GROUP_W:(g + 1) * GROUP_W].reshape(HEADS_PER_GROUP, HEAD_DIM, S)
        rides.append((q, kc, vc, new_t(kt), new_t(vt), b1, b2))
    return dict(x=x, mods=mods, u=u, co=co), rides


def _sample_back(front, ride_outs, weights):
    (_, norm2_g, _, _, _, _, _, w_o_bf, w_r_t, b_r, w_gate_e, w_up_e, w_down_e,
     w_gate_s, w_up_s, w_down_s) = weights
    x, u, co = front["x"], front["u"], front["co"]
    _, _, g1, sh2, sc2, g2 = front["mods"]
    S = x.shape[1]
    nb = S // 4
    states, o_list, l_list = [], [], []
    for g in range(N_GROUPS):
        ko, vo, o, lse = ride_outs[g]
        states += [ko.transpose(0, 3, 1, 2)[None], vo.transpose(0, 3, 1, 2)[None]]
        o_list.append(o[:, :, :4].transpose(0, 2, 1, 3).reshape(1, 1, S, GROUP_W))
        lse = jnp.broadcast_to(lse[:, :, :4, :1], (nb, HEADS_PER_GROUP, 4, HEAD_DIM))
        l_list.append(lse.transpose(0, 2, 1, 3).reshape(1, 1, S, GROUP_W))
    states.append(u.reshape(nb, 4, CONV_CH)[:, 2:][None])
    x1, xg, gates, _, _, _ = _oproj(x, o_list, l_list, co, g1, sc2, sh2, norm2_g, w_o_bf, w_r_t, b_r,
                                    tile=TILE, dils=(1, 1, 1))
    y = _moe(xg, gates, x1, g2, w_gate_e, w_up_e, w_down_e, w_gate_s, w_up_s, w_down_s, tile=TILE)
    return y, states


def _prompt_layer(x, mods, rel_bias, weights, rides):
    (norm1_g, norm2_g, w_in_bf, qg, kg, conv_w, bd, w_o_bf, w_r_t, b_r, w_gate_e, w_up_e, w_down_e,
     w_gate_s, w_up_s, w_down_s) = weights
    sh1, sc1, g1, sh2, sc2, g2 = mods
    B, S, _ = x.shape
    ride_outs = [None] * N_GROUPS
    res = _proj(x, sc1, sh1, norm1_g, w_in_bf, qg, kg, conv_w, bd, None, tile=TILE, dils=DILATIONS, sample=False,
                ride=rides[RIDE_ON_PROJ])
    qkv, tails, (u, co), ride_outs[RIDE_ON_PROJ] = res[:9], res[9:15], res[15:17], res[17:]
    states, o_list, l_list = [], [], []
    for g in range(N_GROUPS):
        d, w = DILATIONS[g], WINDOWS[g]
        q, k, v = (a.reshape(B * d, S // d, GROUP_W) for a in qkv[3 * g:3 * g + 3])
        o, lse = _attn_prompt(q, k, v, _prompt_bias(_step_bias(rel_bias, g)))
        o_list.append(o.reshape(B, d, S // d, GROUP_W))
        l_list.append(lse.reshape(B, d, S // d, GROUP_W))
        keep = lambda a: a.reshape(B, HEADS_PER_GROUP, HEAD_DIM, w).transpose(0, 3, 1, 2)[None]
        states += [keep(tails[2 * g]), keep(tails[2 * g + 1])]
    states.append(u[:, 6:][None])
    x1, xg, _, posr, post, cnt, *ride_outs[RIDE_ON_OPROJ] = _oproj(
        x, o_list, l_list, co, g1, sc2, sh2, norm2_g, w_o_bf, w_r_t, b_r, tile=TILE, dils=DILATIONS,
        ride=rides[RIDE_ON_OPROJ])
    y, ride_outs[RIDE_ON_FFN] = _moe_sparse(xg, post, posr, cnt, x1, g2, w_gate_e, w_up_e, w_down_e, w_gate_s,
                                            w_up_s, w_down_s, rides[RIDE_ON_FFN])
    return y, states, ride_outs


def kernel(x_prompt, x_sample, c_prompt, c_sample, cache_k_w128, cache_v_w128, cache_k_w512, cache_v_w512, cache_k_w2048, cache_v_w2048, state_conv, rel_bias, norm1_g, norm2_g, w_ada, b_ada, w_in, q_norm_g, k_norm_g, conv_w, w_o, w_router, b_router, w_gate_e, w_up_e, w_down_e, w_gate_s, w_up_s, w_down_s):
    B = x_prompt.shape[0]
    DB, T = x_sample.shape[:2]
    n_c = B + DB
    c_all = jnp.pad(jnp.concatenate([c_prompt, c_sample], axis=0), ((0, (-n_c) % 8), (0, 0)))
    mod = _ada(c_all, w_ada[0], b_ada)
    chunks = [mod[:, i * D_MODEL:(i + 1) * D_MODEL] for i in range(6)]
    mods_p = [c[:B].reshape(B, 1, D_MODEL) for c in chunks]
    mods_s = [jnp.repeat(c[B:n_c], T, axis=0).reshape(1, DB * T, D_MODEL) for c in chunks]

    eye = jnp.arange(GROUP_W) // HEAD_DIM
    bd = jnp.where(eye[:, None] == eye[None, :], 1.0 / HEAD_DIM, 0.0).astype(BF16)
    weights = (norm1_g, norm2_g, w_in[0].astype(BF16),
               jnp.tile(q_norm_g, (1, HEADS_PER_GROUP)), jnp.tile(k_norm_g, (1, HEADS_PER_GROUP)),
               conv_w[0], bd, w_o[0].astype(BF16), w_router[0].T, b_router.reshape(N_EXPERTS, 1),
               w_gate_e[0], w_up_e[0], w_down_e[0], w_gate_s[0], w_up_s[0], w_down_s[0])

    s0, s1 = state_conv[0, :, 0], state_conv[0, :, 1]
    zero = jnp.zeros_like(s0)
    hist_a = jnp.stack([s0, s1, zero, zero], axis=1).reshape(DB * T, CONV_CH)
    hist_b = jnp.stack([s1, zero, zero, zero], axis=1).reshape(DB * T, CONV_CH)
    caches = [(ck[0].transpose(0, 2, 3, 1), cv[0].transpose(0, 2, 3, 1))
              for ck, cv in ((cache_k_w128, cache_v_w128), (cache_k_w512, cache_v_w512),
                             (cache_k_w2048, cache_v_w2048))]
    front, ride = _sample_front(x_sample.reshape(1, DB * T, D_MODEL), mods_s, (hist_a, hist_b), caches,
                                rel_bias, weights)
    yp, st_p, ride_out = _prompt_layer(x_prompt, mods_p, rel_bias, weights, ride)
    ys, st_s = _sample_back(front, ride_out, weights)
    return (yp, ys.reshape(DB, T, D_MODEL), *st_p, *st_s)
```

```python
import functools
import math

import jax
import jax.numpy as jnp
from jax import lax
from jax.experimental import pallas as pl
from jax.experimental.pallas import tpu as pltpu

D_MODEL = 1024
HEAD_DIM = 64
HEADS_PER_GROUP = 4
GROUP_W = HEADS_PER_GROUP * HEAD_DIM
WINDOWS = (128, 512, 2048)
DILATIONS = (1, 4, 16)
N_GROUPS = 3
ATTN_W = N_GROUPS * GROUP_W
CONV_CH = 256
N_PROJ = 3 * ATTN_W + 3 * CONV_CH
N_STEPS = 128
N_KEYS = N_STEPS + 1
BLK = 128
N_BUCKETS = 32
MAX_DISTANCE = WINDOWS[-1]
N_EXPERTS = 64
TOP_K = 8
N_ROUTE_GROUPS = 8
GROUP_SIZE = N_EXPERTS // N_ROUTE_GROUPS
TOPK_GROUPS = 4
D_EXPERT = 256
ROUTED_SCALE = 2.5
RMS_EPS = 1e-6
NEG = -1e30
LANES = 128
SUB = 256
CHUNK = 16
XG_W = D_MODEL + LANES
VMEM_LIMIT = 56 * 1024 * 1024

F32 = jnp.float32
BF16 = jnp.bfloat16


def _dot(a, b):
    return jnp.dot(a, b, preferred_element_type=F32)


def _dot_nt(a, b):
    return lax.dot_general(a, b, (((1,), (1,)), ((), ())), preferred_element_type=F32)


def _silu(a):
    return a / (1.0 + jnp.exp(-a))


def _params(sem):
    return pltpu.CompilerParams(dimension_semantics=sem, vmem_limit_bytes=VMEM_LIMIT)


def _ada_kernel(c_ref, w_ref, b_ref, o_ref):
    s = _silu(c_ref[...])
    o_ref[...] = _dot(s.astype(BF16), w_ref[...].astype(BF16)) + b_ref[...]


def _ada(c, w_ada, b_ada):
    n = c.shape[0]
    nc = w_ada.shape[1] // D_MODEL
    return pl.pallas_call(
        _ada_kernel,
        grid=(nc,),
        in_specs=[pl.BlockSpec((n, D_MODEL), lambda j: (0, 0)),
                  pl.BlockSpec((D_MODEL, D_MODEL), lambda j: (0, j)),
                  pl.BlockSpec((1, D_MODEL), lambda j: (0, j))],
        out_specs=pl.BlockSpec((n, D_MODEL), lambda j: (0, j)),
        out_shape=jax.ShapeDtypeStruct((n, w_ada.shape[1]), F32),
        compiler_params=_params(("arbitrary",)),
        name="ada",
    )(c, w_ada, b_ada)


def _proj_kernel(*refs, tile, dils, sample, nt):
    if sample:
        (x_ref, sc_ref, sh_ref, ng_ref, w_ref, qg_ref, kg_ref, cw_ref, bd_ref, ha_ref, hb_ref,
         qf_ref, kt_ref, vt_ref, u_ref, co_ref, u_scr) = refs
    else:
        (x_ref, sc_ref, sh_ref, ng_ref, w_ref, qg_ref, kg_ref, cw_ref, bd_ref) = refs[:9]
        ride_in, refs = refs[9:9 + N_RIDE_IN], refs[9 + N_RIDE_IN:]
        (q0, k0, v0, q1, k1, v1, q2, k2, v2, kt0, vt0, kt1, vt1, kt2, vt2, u_ref, co_ref) = refs[:17]
        ride_out, (slab, u_scr) = refs[17:17 + N_RIDE_OUT], refs[17 + N_RIDE_OUT:]
        qkv_out = ((q0, k0, v0), (q1, k1, v1), (q2, k2, v2))
        tails = ((kt0, vt0), (kt1, vt1), (kt2, vt2))
        j = pl.program_id(1)
        _ride_block(pl.program_id(0) * nt + j, ride_in + ride_out)

    x = x_ref[0]
    ms = jnp.mean(x * x, axis=-1, keepdims=True)
    h = x * lax.rsqrt(ms + RMS_EPS) * ng_ref[...]
    h = h * (1.0 + sc_ref[0]) + sh_ref[0]
    proj = _dot(h.astype(BF16), w_ref[...])

    bd = bd_ref[...]

    def headnorm(z, g):
        msq = _dot((z * z).astype(BF16), bd)
        return z * lax.rsqrt(msq + RMS_EPS) * g

    slab_i = 0
    for g in range(N_GROUPS):
        c0 = g * GROUP_W
        qn = headnorm(proj[:, c0:c0 + GROUP_W], qg_ref[...]) * (HEAD_DIM ** -0.5)
        kn = headnorm(proj[:, ATTN_W + c0:ATTN_W + c0 + GROUP_W], kg_ref[...])
        vv = proj[:, 2 * ATTN_W + c0:2 * ATTN_W + c0 + GROUP_W]
        if sample:
            qf_ref[0, :, c0:c0 + GROUP_W] = qn
            kt_ref[c0:c0 + GROUP_W, :] = kn.T
            vt_ref[c0:c0 + GROUP_W, :] = vv.T
            continue
        keep = min(WINDOWS[g], tile)

        @pl.when(j >= nt - max(WINDOWS[g] // tile, 1))
        def _(kn=kn, vv=vv, g=g, keep=keep):
            tails[g][0][0] = kn[tile - keep:, :].T
            tails[g][1][0] = vv[tile - keep:, :].T

        d = dils[g]
        for val, out in zip((qn, kn, vv), qkv_out[g]):
            if d == 1:
                out[0, 0] = val.astype(BF16)
                continue
            n = tile // d
            for half in range(GROUP_W // LANES):
                slab[slab_i] = val[:, half * LANES:(half + 1) * LANES]
                for r in range(d):
                    out[0, r, :, half * LANES:(half + 1) * LANES] = (
                        slab[slab_i, pl.ds(r, n, stride=d), :].astype(BF16))
                slab_i += 1

    base = 3 * ATTN_W
    u = proj[:, base + 2 * CONV_CH:base + 3 * CONV_CH] * proj[:, base:base + CONV_CH]
    gate_b = proj[:, base + CONV_CH:base + 2 * CONV_CH]
    if sample:
        u_scr[0:8, :] = jnp.zeros((8, CONV_CH), F32)
    else:
        @pl.when(j == 0)
        def _():
            u_scr[0:8, :] = jnp.zeros((8, CONV_CH), F32)

        @pl.when(j > 0)
        def _():
            u_scr[0:8, :] = u_scr[tile:tile + 8, :]

    u_scr[8:tile + 8, :] = u
    um1 = u_scr[7:tile + 7, :]
    um2 = u_scr[6:tile + 6, :]
    if sample:
        t = lax.broadcasted_iota(jnp.int32, (tile, CONV_CH), 0) % 4
        um1 = jnp.where(t >= 1, um1, 0.0) + hb_ref[...]
        um2 = jnp.where(t >= 2, um2, 0.0) + ha_ref[...]
    cw = cw_ref[...]
    conv = cw[0:1] * um2 + cw[1:2] * um1 + cw[2:3] * u
    u_ref[0] = u if sample else u[tile - 8:, :]
    co_ref[0] = (gate_b * conv).astype(BF16)


def _proj(x, sc, sh, ng, w_bf, qg, kg, cw, bd, hist, *, tile, dils, sample, ride=None):
    B, S, _ = x.shape
    nt = S // tile
    tm = sc.shape[1]
    mod_spec = pl.BlockSpec((1, tm, D_MODEL), (lambda b, j: (b, j, 0)) if tm > 1 else (lambda b, j: (b, 0, 0)))
    const2 = lambda shape: pl.BlockSpec(shape, lambda b, j: (0, 0))
    in_specs = [pl.BlockSpec((1, tile, D_MODEL), lambda b, j: (b, j, 0)), mod_spec, mod_spec,
                const2((1, D_MODEL)), const2((D_MODEL, N_PROJ)), const2((1, GROUP_W)),
                const2((1, GROUP_W)), const2((3, CONV_CH)), const2((GROUP_W, GROUP_W))]
    args = [x, sc, sh, ng, w_bf, qg, kg, cw, bd]
    nat = lambda w: pl.BlockSpec((1, tile, w), lambda b, j: (b, j, 0))
    out_specs, out_shape = [], []
    scratch = []
    if sample:
        assert B == 1 and nt == 1
        in_specs += [pl.BlockSpec((tile, CONV_CH), lambda b, j: (j, 0))] * 2
        args += list(hist)
        out_specs += [nat(ATTN_W)] + [pl.BlockSpec((ATTN_W, tile), lambda b, j: (0, j))] * 2 + [nat(CONV_CH)]
        out_shape += [jax.ShapeDtypeStruct((B, S, ATTN_W), F32)]
        out_shape += [jax.ShapeDtypeStruct((ATTN_W, S), F32)] * 2
        out_shape += [jax.ShapeDtypeStruct((B, S, CONV_CH), F32)]
    else:
        for d in dils:
            for _ in range(3):
                out_specs.append(pl.BlockSpec((1, d, tile // d, GROUP_W), lambda b, j: (b, 0, j, 0)))
                out_shape.append(jax.ShapeDtypeStruct((B, d, S // d, GROUP_W), BF16))
        for w in WINDOWS:
            keep, first = min(w, tile), nt - max(w // tile, 1)
            for _ in range(2):
                out_specs.append(pl.BlockSpec((1, GROUP_W, keep),
                                              lambda b, j, first=first: (b, 0, jnp.maximum(j - first, 0))))
                out_shape.append(jax.ShapeDtypeStruct((B, GROUP_W, w), F32))
        out_specs.append(pl.BlockSpec((1, 8, CONV_CH), lambda b, j: (b, 0, 0)))
        out_shape.append(jax.ShapeDtypeStruct((B, 8, CONV_CH), F32))
        n_slabs = sum(3 * (GROUP_W // LANES) for d in dils if d > 1)
        scratch.append(pltpu.VMEM((n_slabs, tile, LANES), F32))
    out_specs.append(nat(CONV_CH))
    out_shape.append(jax.ShapeDtypeStruct((B, S, CONV_CH), BF16))
    scratch.append(pltpu.VMEM((tile + 8, CONV_CH), F32))
    if not sample:
        nb = ride[1].shape[0]
        r_in, r_out, r_shape = _ride_specs(ride, nb // (B * nt), lambda b, j: b * nt + j)
        in_specs, args = in_specs + r_in, args + list(ride)
        out_specs, out_shape = out_specs + r_out, out_shape + r_shape
    return pl.pallas_call(
        functools.partial(_proj_kernel, tile=tile, dils=dils, sample=sample, nt=nt),
        grid=(B, nt),
        in_specs=in_specs,
        out_specs=out_specs,
        out_shape=out_shape,
        scratch_shapes=scratch,
        compiler_params=_params(("arbitrary", "arbitrary")),
        name="proj_sample" if sample else "proj_prompt",
    )(*args)


def _attn_kernel(q_ref, kp_ref, kc_ref, vp_ref, vc_ref, b_ref, o_ref, l_ref, s_scr, m_scr, o_scr, z_scr,
                 *, nq):
    first = pl.program_id(1) == 0
    low = lax.broadcasted_iota(jnp.int32, (BLK, LANES), 1) < HEAD_DIM
    ones = jnp.ones((BLK, LANES), BF16)
    chains = [(j, pair, sub) for j in range(nq) for pair in range(GROUP_W // LANES) for sub in range(2)]

    def operands(j, pair):
        rows = slice(j * BLK, (j + 1) * BLK)
        sl = slice(pair * LANES, (pair + 1) * LANES)
        if j == 0:
            return rows, sl, kp_ref[0, :, sl], vp_ref[0, :, sl]
        prows = slice((j - 1) * BLK, j * BLK)
        return rows, sl, kc_ref[0, prows, sl], vc_ref[0, prows, sl]

    for c, (j, pair, sub) in enumerate(chains):
        rows, sl, kp, _ = operands(j, pair)
        q = q_ref[0, rows, sl]
        qm = jnp.where(low if sub == 0 else ~low, q, jnp.zeros_like(q))
        hh = 2 * pair + sub
        sp = _dot_nt(qm, kp) + b_ref[hh, :, 0:BLK]
        if j == 0:
            sp = jnp.where(first, NEG, sp)
        sc = _dot_nt(qm, kc_ref[0, rows, sl]) + b_ref[hh, :, BLK:2 * BLK]
        s_scr[c, :, 0:BLK] = sp
        s_scr[c, :, BLK:2 * BLK] = sc
        m_scr[c] = jnp.broadcast_to(jnp.max(jnp.maximum(sp, sc), axis=-1, keepdims=True), (BLK, LANES))

    for c, (j, pair, sub) in enumerate(chains):
        rows, sl, _, vp = operands(j, pair)
        m = m_scr[c]
        pp = jnp.exp(s_scr[c, :, 0:BLK] - m).astype(BF16)
        pc = jnp.exp(s_scr[c, :, BLK:2 * BLK] - m).astype(BF16)
        o_scr[c] = _dot(pp, vp) + _dot(pc, vc_ref[0, rows, sl])
        z_scr[c] = _dot(pp, ones) + _dot(pc, ones)

    for c in range(0, len(chains), 2):
        j, pair, _ = chains[c]
        rows = slice(j * BLK, (j + 1) * BLK)
        sl = slice(pair * LANES, (pair + 1) * LANES)
        o_sub = [o_scr[c + sub] / z_scr[c + sub] for sub in range(2)]
        l_sub = [m_scr[c + sub] + jnp.log(z_scr[c + sub]) for sub in range(2)]
        o_ref[0, rows, sl] = jnp.where(low, o_sub[0], o_sub[1])
        l_ref[0, rows, sl] = jnp.where(low, l_sub[0], l_sub[1])


def _attn_prompt(q, k, v, bias):
    Z, L, _ = q.shape
    qt = min(L, 8 * BLK)
    nq = qt // BLK
    cur = pl.BlockSpec((1, qt, GROUP_W), lambda z, i: (z, i, 0))
    prev = pl.BlockSpec((1, BLK, GROUP_W), lambda z, i: (z, jnp.maximum(i * nq - 1, 0), 0))
    return pl.pallas_call(
        functools.partial(_attn_kernel, nq=nq),
        grid=(Z, L // qt),
        in_specs=[cur, prev, cur, prev, cur,
                  pl.BlockSpec((HEADS_PER_GROUP, BLK, 2 * BLK), lambda z, i: (0, 0, 0))],
        out_specs=[cur, cur],
        out_shape=[jax.ShapeDtypeStruct((Z, L, GROUP_W), F32)] * 2,
        scratch_shapes=[pltpu.VMEM((nq * HEADS_PER_GROUP, BLK, 2 * BLK), F32)]
        + [pltpu.VMEM((nq * HEADS_PER_GROUP, BLK, LANES), F32)] * 3,
        compiler_params=_params(("arbitrary", "arbitrary")),
        name="attn_prompt",
    )(q, k, k, v, v, bias)


def _attn_sample_items(items, q_ref, kc_ref, vc_ref, nk_ref, nv_ref, b1_ref, b2_ref, ko_ref, vo_ref, o_ref,
                       l_ref):
    wb = kc_ref.shape[-1]
    lane = lax.broadcasted_iota(jnp.int32, (HEAD_DIM, LANES), 1)
    new_cols = lane >= LANES - 4
    per_tile = LANES // 4
    loaded, s1, s2 = [], [], []
    for gb, b, hh in items:
        tile_at = pl.ds(pl.multiple_of((gb // per_tile) * LANES, LANES), LANES)
        shift = (LANES - 4) - 4 * (gb % per_tile)
        q = q_ref[b, hh]
        kt = kc_ref[b, hh]
        vt = vc_ref[b, hh]
        nkt = pltpu.roll(nk_ref[hh, :, tile_at], shift, axis=1)
        nvt = pltpu.roll(nv_ref[hh, :, tile_at], shift, axis=1)
        loaded.append((kt, vt, nkt, nvt))
        s1.append(_dot(q, kt.astype(BF16)) + b1_ref[hh])
        s2.append(_dot(q, nkt.astype(BF16)) + b2_ref[hh])
    s1, s2 = jnp.concatenate(s1, axis=0), jnp.concatenate(s2, axis=0)
    m = jnp.maximum(jnp.max(s1, axis=-1, keepdims=True), jnp.max(s2, axis=-1, keepdims=True))
    p1 = jnp.exp(s1 - m)
    p2 = jnp.exp(s2 - m)
    z = jnp.sum(p1, axis=-1, keepdims=True) + jnp.sum(p2, axis=-1, keepdims=True)
    p1, p2 = p1.astype(BF16), p2.astype(BF16)
    lse = jnp.broadcast_to(m + jnp.log(z), (8 * len(items), LANES))
    for i, (gb, b, hh) in enumerate(items):
        kt, vt, nkt, nvt = loaded[i]
        r = slice(8 * i, 8 * i + 8)
        o = _dot_nt(p1[r], vt.astype(BF16)) + _dot_nt(p2[r], nvt.astype(BF16))
        o_ref[b, hh] = o / z[r]
        l_ref[b, hh] = lse[r]
        for src, new, dst in ((kt, nkt, ko_ref), (vt, nvt, vo_ref)):
            rolled = pltpu.roll(src, wb - 4, axis=1)
            if wb > LANES:
                dst[b, hh, :, 0:wb - LANES] = rolled[:, 0:wb - LANES]
            dst[b, hh, :, wb - LANES:wb] = jnp.where(new_cols, new, rolled[:, wb - LANES:wb])


N_RIDE_IN = 7
N_RIDE_OUT = 4


def _ride_specs(ride, bb, block_index):
    q, kc, vc, nk, nv, b1, b2 = ride
    nb, _, _, wb = kc.shape
    blk = lambda *tail: pl.BlockSpec((bb, HEADS_PER_GROUP) + tail, lambda *g: (block_index(*g), 0, 0, 0))
    const = lambda a: pl.BlockSpec(a.shape, lambda *g: (0,) * a.ndim)
    in_specs = [blk(8, HEAD_DIM), blk(HEAD_DIM, wb), blk(HEAD_DIM, wb), const(nk), const(nv), const(b1), const(b2)]
    out_specs = [blk(HEAD_DIM, wb), blk(HEAD_DIM, wb), blk(8, HEAD_DIM), blk(8, LANES)]
    out_shape = [jax.ShapeDtypeStruct(kc.shape, F32), jax.ShapeDtypeStruct(kc.shape, F32),
                 jax.ShapeDtypeStruct((nb, HEADS_PER_GROUP, 8, HEAD_DIM), F32),
                 jax.ShapeDtypeStruct((nb, HEADS_PER_GROUP, 8, LANES), F32)]
    return in_specs, out_specs, out_shape


def _ride_block(step, ride_refs):
    bb = ride_refs[0].shape[0]
    _attn_sample_items([(step * bb + b, b, hh) for b in range(bb) for hh in range(HEADS_PER_GROUP)], *ride_refs)


def _first_max(v, ids, sentinel):
    m = jnp.max(v, axis=0, keepdims=True)
    idx = jnp.min(jnp.where(v == m, ids, sentinel), axis=0, keepdims=True)
    return m, ids == idx


def _oproj_kernel(*refs, tile, dils, ride):
    (x_ref, o0, o1, o2, l0, l1, l2, co_ref, g1_ref, sc_ref, sh_ref, ng_ref, wo_ref, wr_ref, br_ref) = refs[:15]
    refs = refs[15:]
    if ride:
        ride_in, refs = refs[:N_RIDE_IN], refs[N_RIDE_IN:]
        ride_out = refs[6:6 + N_RIDE_OUT]
    x1_ref, xg_ref, gt_ref, posr_ref, post_ref, cnt_ref = refs[:6]
    slab = refs[-1]
    o_refs, l_refs = (o0, o1, o2), (l0, l1, l2)
    outs, lses = [], []
    slab_i = 0
    for g, d in enumerate(dils):
        if d == 1:
            outs.append(o_refs[g][0, 0])
            lses.append(l_refs[g][0, 0])
            continue
        n = tile // d
        for ref, dest in ((o_refs[g], outs), (l_refs[g], lses)):
            halves = []
            for half in range(GROUP_W // LANES):
                for r in range(d):
                    slab[slab_i, pl.ds(r, n, stride=d), :] = ref[0, r, :, half * LANES:(half + 1) * LANES]
                halves.append(slab[slab_i])
                slab_i += 1
            dest.append(jnp.concatenate(halves, axis=1))

    m = jnp.maximum(jnp.maximum(lses[0], lses[1]), lses[2])
    es = [jnp.exp(l - m) for l in lses]
    den = es[0] + es[1] + es[2]
    cat = [(outs[g] * (es[g] / den)).astype(BF16) for g in range(N_GROUPS)]
    cat.append(co_ref[0].astype(BF16))
    cat = jnp.concatenate(cat, axis=1)
    x1 = x_ref[0] + g1_ref[0] * _dot(cat, wo_ref[...])
    x1_ref[0] = x1

    ms = jnp.mean(x1 * x1, axis=-1, keepdims=True)
    h2 = x1 * lax.rsqrt(ms + RMS_EPS) * ng_ref[...]
    h2 = h2 * (1.0 + sc_ref[0]) + sh_ref[0]
    hh = h2.astype(BF16)
    xg_ref[0, :, 0:D_MODEL] = hh

    hl = (h2 - hh.astype(F32)).astype(BF16)
    wr = wr_ref[...]
    wh = wr.astype(BF16)
    wl = (wr - wh.astype(F32)).astype(BF16)
    logits = _dot_nt(wh, hh) + _dot_nt(wh, hl) + _dot_nt(wl, hh)
    scores = 1.0 / (1.0 + jnp.exp(-logits))
    sel = scores + br_ref[...]

    ids = lax.broadcasted_iota(jnp.int32, (GROUP_SIZE, tile), 0)
    ninf = -jnp.inf
    sel_g = [sel[g * GROUP_SIZE:(g + 1) * GROUP_SIZE] for g in range(N_ROUTE_GROUPS)]
    gscore = jnp.zeros((N_ROUTE_GROUPS, tile), F32)
    for g in range(N_ROUTE_GROUPS):
        m1, oh = _first_max(sel_g[g], ids, GROUP_SIZE)
        m2 = jnp.max(jnp.where(oh, ninf, sel_g[g]), axis=0, keepdims=True)
        gscore = jnp.where(ids == g, m1 + m2, gscore)
    gsel = jnp.zeros((N_ROUTE_GROUPS, tile), F32)
    for _ in range(TOPK_GROUPS):
        _, oh = _first_max(gscore, ids, N_ROUTE_GROUPS)
        gsel = jnp.where(oh, 1.0, gsel)
        gscore = jnp.where(oh, ninf, gscore)
    cand = [jnp.where(gsel[g:g + 1] > 0.0, sel_g[g], ninf) for g in range(N_ROUTE_GROUPS)]
    chosen = [jnp.zeros((GROUP_SIZE, tile), F32) for _ in range(N_ROUTE_GROUPS)]
    picks = []
    for _ in range(TOP_K):
        mx = cand[0]
        for g in range(1, N_ROUTE_GROUPS):
            mx = jnp.maximum(mx, cand[g])
        mx = jnp.max(mx, axis=0, keepdims=True)
        idx = jnp.where(cand[0] == mx, ids, N_EXPERTS)
        for g in range(1, N_ROUTE_GROUPS):
            idx = jnp.minimum(idx, jnp.where(cand[g] == mx, ids + g * GROUP_SIZE, N_EXPERTS))
        idx = jnp.min(idx, axis=0, keepdims=True)
        ohs = [(ids + g * GROUP_SIZE) == idx for g in range(N_ROUTE_GROUPS)]
        picks.append(ohs)
        for g in range(N_ROUTE_GROUPS):
            chosen[g] = jnp.where(ohs[g], 1.0, chosen[g])
            cand[g] = jnp.where(ohs[g], ninf, cand[g])
    wts = [jnp.where(chosen[g] > 0.0, scores[g * GROUP_SIZE:(g + 1) * GROUP_SIZE], 0.0)
           for g in range(N_ROUTE_GROUPS)]
    wsum = jnp.sum(wts[0], axis=0, keepdims=True)
    for g in range(1, N_ROUTE_GROUPS):
        wsum = wsum + jnp.sum(wts[g], axis=0, keepdims=True)
    gates_t = jnp.concatenate([w / wsum * ROUTED_SCALE for w in wts], axis=0)
    gt_ref[0] = jnp.concatenate([gates_t, jnp.zeros((LANES - N_EXPERTS, tile), F32)], axis=0).T
    g_hi = gates_t.astype(BF16).astype(F32)
    xg_ref[0, :, D_MODEL:D_MODEL + LANES] = jnp.concatenate([g_hi, gates_t - g_hi], axis=0).T.astype(BF16)

    er = lax.broadcasted_iota(jnp.int32, (N_EXPERTS, N_EXPERTS), 0)
    ec = lax.broadcasted_iota(jnp.int32, (N_EXPERTS, N_EXPERTS), 1)
    lower = jnp.where(ec < er, 1.0, 0.0).astype(BF16)
    tr = lax.broadcasted_iota(jnp.int32, (SUB, SUB), 0)
    tc = lax.broadcasted_iota(jnp.int32, (SUB, SUB), 1)
    upper = jnp.where(tr < tc, 1.0, 0.0).astype(BF16)
    for u in range(tile // SUB):
        cols = slice(u * SUB, (u + 1) * SUB)
        ch = jnp.concatenate([c[:, cols] for c in chosen], axis=0)
        cnt = jnp.sum(ch, axis=1, keepdims=True)
        cpad = jnp.broadcast_to(jnp.ceil(cnt / CHUNK) * CHUNK, (N_EXPERTS, LANES))
        cnt_ref[u] = cpad
        seg0 = _dot(lower, cpad.astype(BF16))[:, 0:1]
        slot = seg0 + _dot(ch.astype(BF16), upper)
        rows = []
        for k in range(TOP_K):
            acc = jnp.where(picks[k][0][:, cols], slot[0:GROUP_SIZE], 0.0)
            for g in range(1, N_ROUTE_GROUPS):
                acc = acc + jnp.where(picks[k][g][:, cols], slot[g * GROUP_SIZE:(g + 1) * GROUP_SIZE], 0.0)
            rows.append(jnp.sum(acc, axis=0, keepdims=True))
        posr = jnp.concatenate(rows, axis=0)
        posr_ref[u] = posr
        post_ref[0, cols, :] = jnp.concatenate([posr, jnp.zeros((LANES - TOP_K, SUB), F32)], axis=0).T
    if ride:
        _ride_block(pl.program_id(0) * pl.num_programs(1) + pl.program_id(1), ride_in + ride_out)


def _oproj(x, o_list, l_list, co, g1, sc, sh, ng, wo_bf, wr_t, br, *, tile, dils, ride=None):
    B, S, _ = x.shape
    nt = S // tile
    r_in, r_out, r_shape, r_args = [], [], [], []
    if ride is not None:
        r_in, r_out, r_shape = _ride_specs(ride, ride[1].shape[0] // (B * nt), lambda b, j: b * nt + j)
        r_args = list(ride)
    tm = g1.shape[1]
    mod_spec = pl.BlockSpec((1, tm, D_MODEL), (lambda b, j: (b, j, 0)) if tm > 1 else (lambda b, j: (b, 0, 0)))
    const2 = lambda shape: pl.BlockSpec(shape, lambda b, j: (0, 0))
    nat = lambda w: pl.BlockSpec((1, tile, w), lambda b, j: (b, j, 0))
    dspec = [pl.BlockSpec((1, d, tile // d, GROUP_W), lambda b, j: (b, 0, j, 0)) for d in dils]
    n_slabs = sum(2 * (GROUP_W // LANES) for d in dils if d > 1)
    nsub_t = tile // SUB
    return pl.pallas_call(
        functools.partial(_oproj_kernel, tile=tile, dils=dils, ride=ride is not None),
        grid=(B, nt),
        in_specs=[nat(D_MODEL)] + dspec + dspec + [nat(CONV_CH), mod_spec, mod_spec, mod_spec,
                  const2((1, D_MODEL)), const2((D_MODEL, D_MODEL)), const2((N_EXPERTS, D_MODEL)),
                  const2((N_EXPERTS, 1))] + r_in,
        out_specs=[nat(D_MODEL), nat(XG_W), nat(LANES),
                   pl.BlockSpec((nsub_t, TOP_K, SUB), lambda b, j: (b * nt + j, 0, 0)),
                   nat(LANES),
                   pl.BlockSpec((nsub_t, N_EXPERTS, LANES), lambda b, j: (b * nt + j, 0, 0))] + r_out,
        out_shape=[jax.ShapeDtypeStruct((B, S, D_MODEL), F32), jax.ShapeDtypeStruct((B, S, XG_W), BF16),
                   jax.ShapeDtypeStruct((B, S, LANES), F32),
                   jax.ShapeDtypeStruct((B * S // SUB, TOP_K, SUB), F32),
                   jax.ShapeDtypeStruct((B, S, LANES), F32),
                   jax.ShapeDtypeStruct((B * S // SUB, N_EXPERTS, LANES), F32)] + r_shape,
        scratch_shapes=[pltpu.VMEM((max(n_slabs, 1), tile, LANES), F32)],
        compiler_params=_params(("arbitrary", "arbitrary")),
        name="oproj",
    )(x, *o_list, *l_list, co, g1, sc, sh, ng, wo_bf, wr_t, br, *r_args)


DENSE_EXPERTS_PER_STEP = 4


def _moe_kernel(h_ref, gt_ref, x1_ref, g2_ref, wg_ref, wu_ref, wd_ref, wgs_ref, wus_ref, wds_ref,
                y_ref, acc, *, tile):
    step = pl.program_id(2)
    h = h_ref[0]

    @pl.when(step == 0)
    def _():
        a = _dot(h, wgs_ref[...].astype(BF16))
        b = _dot(h, wus_ref[...].astype(BF16))
        acc[...] = _dot((_silu(a) * b).astype(BF16), wds_ref[...].astype(BF16))

    lane = lax.broadcasted_iota(jnp.int32, (tile, LANES), 1)
    routed = None
    for j in range(DENSE_EXPERTS_PER_STEP):
        e = step * DENSE_EXPERTS_PER_STEP + j
        a = _dot(h, wg_ref[j].astype(BF16))
        b = _dot(h, wu_ref[j].astype(BF16))
        gcol = jnp.sum(jnp.where(lane == e, gt_ref[0], 0.0), axis=1, keepdims=True)
        hc = _silu(a) * b * gcol
        y = _dot(hc.astype(BF16), wd_ref[j].astype(BF16))
        routed = y if routed is None else routed + y
    acc[...] += routed

    @pl.when(step == pl.num_programs(2) - 1)
    def _():
        y_ref[0] = x1_ref[0] + g2_ref[0] * acc[...]


def _moe(h2, gates, x1, g2, wg, wu, wd, wgs, wus, wds, *, tile):
    B, S, _ = h2.shape
    tm = g2.shape[1]
    mod_spec = pl.BlockSpec((1, tm, D_MODEL),
                            (lambda b, j, e: (b, j, 0)) if tm > 1 else (lambda b, j, e: (b, 0, 0)))
    nat = lambda w: pl.BlockSpec((1, tile, w), lambda b, j, e: (b, j, 0))
    const2 = lambda shape: pl.BlockSpec(shape, lambda b, j, e: (0, 0))
    return pl.pallas_call(
        functools.partial(_moe_kernel, tile=tile),
        grid=(B, S // tile, N_EXPERTS // DENSE_EXPERTS_PER_STEP),
        in_specs=[nat(D_MODEL), nat(LANES), nat(D_MODEL), mod_spec,
                  pl.BlockSpec((DENSE_EXPERTS_PER_STEP, D_MODEL, D_EXPERT), lambda b, j, e: (e, 0, 0)),
                  pl.BlockSpec((DENSE_EXPERTS_PER_STEP, D_MODEL, D_EXPERT), lambda b, j, e: (e, 0, 0)),
                  pl.BlockSpec((DENSE_EXPERTS_PER_STEP, D_EXPERT, D_MODEL), lambda b, j, e: (e, 0, 0)),
                  const2((D_MODEL, D_EXPERT)), const2((D_MODEL, D_EXPERT)), const2((D_EXPERT, D_MODEL))],
        out_specs=nat(D_MODEL),
        out_shape=jax.ShapeDtypeStruct((B, S, D_MODEL), F32),
        scratch_shapes=[pltpu.VMEM((tile, D_MODEL), F32)],
        compiler_params=_params(("arbitrary", "arbitrary", "arbitrary")),
        name="moe",
    )(h2, gates, x1, g2, wg, wu, wd, wgs, wus, wds)


FFN_TM = 1024
SUB_ROWS = SUB * TOP_K + N_EXPERTS * CHUNK
N_CHUNKS = SUB_ROWS // CHUNK
ZERO_CHUNK = N_CHUNKS
TABLE_W = 256
MXU_ROWS = 256
BLOCKS_PER_TRIP = 4


def _slot_onehot(first, pos_list, axis_iota):
    shape = axis_iota.shape
    hit = axis_iota == jnp.broadcast_to(pos_list[0] - first, shape).astype(BF16)
    for p in pos_list[1:]:
        hit = hit | (axis_iota == jnp.broadcast_to(p - first, shape).astype(BF16))
    return jnp.where(hit, jnp.ones(shape, BF16), jnp.zeros(shape, BF16))


def _chunk_copy(src, src_chunk, dst, dst_chunk, sem):
    return pltpu.make_async_copy(src.at[src_chunk], dst.at[dst_chunk], sem)


def _for_each(n, body):
    def two(t, carry):
        body(2 * t)
        body(2 * t + 1)
        return carry

    lax.fori_loop(0, n // 2, two, 0)

    @pl.when(n % 2 == 1)
    def _():
        body(n - 1)


def _start_each(n, copy):
    def two(t, carry):
        copy(2 * t).start(priority=0)
        copy(2 * t + 1).start(priority=1)
        return carry

    lax.fori_loop(0, n // 2, two, 0)

    @pl.when(n % 2 == 1)
    def _():
        copy(n - 1).start(priority=0)


def _dispatch_kernel(nblk_ref, nreal_ref, ncopy_ref, ntile_ref, src_ref, dst_ref, posr_ref, xg_ref, xs_hbm,
                     xs_scr, zero_scr, sem, tail_sem, *, nsub, n_tiles_max):
    s = pl.program_id(0)

    tile_chunks = FFN_TM // CHUNK
    blk_chunks = MXU_ROWS // CHUNK
    zero_scr[...] = jnp.zeros((tile_chunks, CHUNK, XG_W), BF16)
    tail_blocks = [ntile_ref[0] + s + r * nsub for r in range(-(-(n_tiles_max) // nsub))]

    def tail_copy(t):
        return pltpu.make_async_copy(
            zero_scr, xs_hbm.at[pl.ds(pl.multiple_of(t * tile_chunks, tile_chunks), tile_chunks)], tail_sem)

    for t in tail_blocks:
        @pl.when(t < n_tiles_max)
        def _():
            tail_copy(t).start()

    xg = xg_ref[...]
    pos = [posr_ref[0, k:k + 1, :] for k in range(TOP_K)]
    rows = lax.broadcasted_iota(jnp.int32, (MXU_ROWS, SUB), 0).astype(BF16)
    xs_scr[ZERO_CHUNK] = jnp.zeros((CHUNK, XG_W), BF16)

    def sort_blocks(i, carry):
        for j in range(BLOCKS_PER_TRIP):
            blk = i * BLOCKS_PER_TRIP + j
            onehot = _slot_onehot((blk * MXU_ROWS).astype(F32), pos, rows)
            sorted_rows = _dot(onehot, xg).astype(BF16)
            xs_scr[pl.ds(pl.multiple_of(blk * blk_chunks, blk_chunks), blk_chunks)] = (
                sorted_rows.reshape(blk_chunks, CHUNK, XG_W))
        k0 = i * trip_chunks
        _start_each(jnp.clip(n_real - k0, 0, trip_chunks), lambda t: send(k0 + t))
        return carry

    n_real = nreal_ref[s]
    n = ncopy_ref[s]
    trip_chunks = BLOCKS_PER_TRIP * blk_chunks

    def send(i):
        return _chunk_copy(xs_scr, src_ref[0, 0, i], xs_hbm, dst_ref[0, 0, i], sem)

    lax.fori_loop(0, nblk_ref[s], sort_blocks, 0)
    _start_each(n - n_real, lambda t: send(n_real + t))
    _for_each(n, lambda i: _chunk_copy(xs_scr, 0, xs_hbm, 0, sem).wait())
    for t in tail_blocks:
        @pl.when(t < n_tiles_max)
        def _():
            tail_copy(t).wait()


def _dispatch(nblk, n_real, ncopy, n_tiles, src, dst, posr, xg, n_rows):
    nsub = posr.shape[0]
    tab = pl.BlockSpec((1, 1, TABLE_W), lambda s, *_: (s, 0, 0), memory_space=pltpu.SMEM)
    return pl.pallas_call(
        functools.partial(_dispatch_kernel, nsub=nsub, n_tiles_max=n_rows // FFN_TM),
        grid_spec=pltpu.PrefetchScalarGridSpec(
            num_scalar_prefetch=4, grid=(nsub,),
            in_specs=[tab, tab, pl.BlockSpec((1, TOP_K, SUB), lambda s, *_: (s, 0, 0)),
                      pl.BlockSpec((SUB, XG_W), lambda s, *_: (s, 0))],
            out_specs=pl.BlockSpec(memory_space=pl.ANY),
            scratch_shapes=[pltpu.VMEM((N_CHUNKS + 1, CHUNK, XG_W), BF16),
                            pltpu.VMEM((FFN_TM // CHUNK, CHUNK, XG_W), BF16),
                            pltpu.SemaphoreType.DMA(()), pltpu.SemaphoreType.DMA(())]),
        out_shape=jax.ShapeDtypeStruct((n_rows // CHUNK, CHUNK, XG_W), BF16),
        compiler_params=_params(("arbitrary",)),
        name="moe_dispatch",
    )(nblk, n_real, ncopy, n_tiles, src, dst, posr, xg).reshape(n_rows, XG_W)


def _ffn_kernel(te_ref, nt_ref, xs_ref, wg_ref, wu_ref, wd_ref,
                q_ref, kc_ref, vc_ref, nk_ref, nv_ref, b1_ref, b2_ref,
                ys_ref, ko_ref, vo_ref, o_ref, l_ref, wg_s, wu_s, wd_s, *, steps_per_batch):
    i = pl.program_id(0)
    n_heads = HEADS_PER_GROUP // steps_per_batch
    first_head = (i % steps_per_batch) * n_heads

    def ride():
        _attn_sample_items([(i // steps_per_batch, 0, first_head + j) for j in range(n_heads)],
                           q_ref, kc_ref, vc_ref, nk_ref, nv_ref, b1_ref, b2_ref, ko_ref, vo_ref, o_ref, l_ref)

    @pl.when(i >= nt_ref[0])
    def _():
        ride()
        ys_ref[...] = jnp.zeros((FFN_TM, D_MODEL), BF16)

    @pl.when(i < nt_ref[0])
    def _():
        e = te_ref[i]

        @pl.when((i == 0) | (e != te_ref[jnp.maximum(i - 1, 0)]))
        def _():
            wg_s[...] = wg_ref[0].astype(BF16)
            wu_s[...] = wu_ref[0].astype(BF16)
            wd_s[...] = wd_ref[0].astype(BF16)

        ride()
        x = xs_ref[:, 0:D_MODEL]
        g = xs_ref[:, D_MODEL:XG_W].astype(F32)
        lane = lax.broadcasted_iota(jnp.int32, (FFN_TM, LANES), 1)
        gcol = jnp.sum(jnp.where((lane == e) | (lane == e + N_EXPERTS), g, 0.0), axis=1, keepdims=True)
        a = _dot(x, wg_s[...])
        b = _dot(x, wu_s[...])
        hc = _silu(a) * b * gcol
        ys_ref[...] = _dot(hc.astype(BF16), wd_s[...]).astype(BF16)


def _ffn(tile_expert, n_tiles, xs, wg, wu, wd, ride):
    q, kc, vc, nk, nv, b1, b2 = ride
    nb, _, _, wb = kc.shape
    n_rows = xs.shape[0]
    n_steps = n_rows // FFN_TM
    assert n_steps % nb == 0 and HEADS_PER_GROUP % (n_steps // nb) == 0
    per = n_steps // nb
    row = lambda i, te, nt: (jnp.minimum(i, nt[0] - 1), 0)
    wspec = lambda shape: pl.BlockSpec((1,) + shape, lambda i, te, nt: (te[i], 0, 0))
    rblk = lambda *tail: pl.BlockSpec((1, HEADS_PER_GROUP) + tail, lambda i, te, nt: (i // per, 0, 0, 0))
    const = lambda shape: pl.BlockSpec(shape, lambda i, te, nt: (0, 0, 0))
    return pl.pallas_call(
        functools.partial(_ffn_kernel, steps_per_batch=per),
        grid_spec=pltpu.PrefetchScalarGridSpec(
            num_scalar_prefetch=2, grid=(n_steps,),
            in_specs=[pl.BlockSpec((FFN_TM, XG_W), row), wspec((D_MODEL, D_EXPERT)),
                      wspec((D_MODEL, D_EXPERT)), wspec((D_EXPERT, D_MODEL)),
                      rblk(8, HEAD_DIM), rblk(HEAD_DIM, wb), rblk(HEAD_DIM, wb), const(nk.shape), const(nv.shape),
                      const(b1.shape), const(b2.shape)],
            out_specs=[pl.BlockSpec((FFN_TM, D_MODEL), lambda i, te, nt: (i, 0)),
                       rblk(HEAD_DIM, wb), rblk(HEAD_DIM, wb), rblk(8, HEAD_DIM), rblk(8, LANES)],
            scratch_shapes=[pltpu.VMEM((D_MODEL, D_EXPERT), BF16), pltpu.VMEM((D_MODEL, D_EXPERT), BF16),
                            pltpu.VMEM((D_EXPERT, D_MODEL), BF16)]),
        out_shape=[jax.ShapeDtypeStruct((n_rows, D_MODEL), BF16),
                   jax.ShapeDtypeStruct(kc.shape, F32), jax.ShapeDtypeStruct(kc.shape, F32),
                   jax.ShapeDtypeStruct((nb, HEADS_PER_GROUP, 8, HEAD_DIM), F32),
                   jax.ShapeDtypeStruct((nb, HEADS_PER_GROUP, 8, LANES), F32)],
        compiler_params=_params(("arbitrary",)),
        name="moe_ffn",
    )(tile_expert, n_tiles, xs, wg, wu, wd, q, kc, vc, nk, nv, b1, b2)


def _combine_kernel(nblk_ref, ncopy_ref, src_ref, post_ref, xg_ref, x1_ref, g2_ref, wgs_ref, wus_ref,
                    wds_ref, ys_hbm, y_ref, ys_scr, wgs_s, wus_s, wds_s, sem):
    s = pl.program_id(0)
    n = ncopy_ref[s]
    blk_chunks = MXU_ROWS // CHUNK
    trip_chunks = BLOCKS_PER_TRIP * blk_chunks

    def fetch_trip(t, carry):
        k0 = t * trip_chunks
        _start_each(trip_chunks,
                    lambda u: _chunk_copy(ys_hbm, src_ref[0, 0, k0 + u], ys_scr, k0 + u, sem.at[t]))
        return carry

    lax.fori_loop(0, nblk_ref[s], fetch_trip, 0)

    @pl.when(s == 0)
    def _():
        wgs_s[...] = wgs_ref[...].astype(BF16)
        wus_s[...] = wus_ref[...].astype(BF16)
        wds_s[...] = wds_ref[...].astype(BF16)

    h = xg_ref[:, 0:D_MODEL]
    a = _dot(h, wgs_s[...])
    b = _dot(h, wus_s[...])
    shared = _dot((_silu(a) * b).astype(BF16), wds_s[...])

    post = post_ref[...]
    pos = [jnp.broadcast_to(post[:, k:k + 1], (SUB, MXU_ROWS)) for k in range(TOP_K)]
    lanes = lax.broadcasted_iota(jnp.int32, (SUB, MXU_ROWS), 1).astype(BF16)

    def gather_blocks(i, acc):
        _for_each(trip_chunks, lambda t: _chunk_copy(ys_hbm, 0, ys_scr, 0, sem.at[i]).wait())
        for j in range(BLOCKS_PER_TRIP):
            blk = i * BLOCKS_PER_TRIP + j
            onehot = _slot_onehot((blk * MXU_ROWS).astype(F32), pos, lanes)
            rows = ys_scr[pl.ds(pl.multiple_of(blk * blk_chunks, blk_chunks), blk_chunks)]
            acc = acc + _dot(onehot, rows.reshape(MXU_ROWS, D_MODEL))
        return acc

    routed = lax.fori_loop(0, nblk_ref[s], gather_blocks, jnp.zeros((SUB, D_MODEL), F32))
    y_ref[...] = x1_ref[...] + g2_ref[0] * (routed + shared)


def _combine(nblk, ncopy, src, post, xg, x1, g2, wgs, wus, wds, ys):
    n_tok = x1.shape[0]
    nsub = n_tok // SUB
    per_b = nsub // g2.shape[0]
    tab = pl.BlockSpec((1, 1, TABLE_W), lambda s, *_: (s, 0, 0), memory_space=pltpu.SMEM)
    const2 = lambda shape: pl.BlockSpec(shape, lambda s, *_: (0, 0))
    return pl.pallas_call(
        _combine_kernel,
        grid_spec=pltpu.PrefetchScalarGridSpec(
            num_scalar_prefetch=2, grid=(nsub,),
            in_specs=[tab, pl.BlockSpec((SUB, LANES), lambda s, *_: (s, 0)),
                      pl.BlockSpec((SUB, XG_W), lambda s, *_: (s, 0)),
                      pl.BlockSpec((SUB, D_MODEL), lambda s, *_: (s, 0)),
                      pl.BlockSpec((1, 1, D_MODEL), lambda s, *_: (s // per_b, 0, 0)),
                      const2((D_MODEL, D_EXPERT)), const2((D_MODEL, D_EXPERT)), const2((D_EXPERT, D_MODEL)),
                      pl.BlockSpec(memory_space=pl.ANY)],
            out_specs=pl.BlockSpec((SUB, D_MODEL), lambda s, *_: (s, 0)),
            scratch_shapes=[pltpu.VMEM((N_CHUNKS, CHUNK, D_MODEL), BF16), pltpu.VMEM((D_MODEL, D_EXPERT), BF16),
                            pltpu.VMEM((D_MODEL, D_EXPERT), BF16), pltpu.VMEM((D_EXPERT, D_MODEL), BF16),
                            pltpu.SemaphoreType.DMA((SUB_ROWS // (BLOCKS_PER_TRIP * MXU_ROWS),))]),
        out_shape=jax.ShapeDtypeStruct((n_tok, D_MODEL), F32),
        compiler_params=_params(("arbitrary",)),
        name="moe_combine",
    )(nblk, ncopy, src, post, xg, x1, g2, wgs, wus, wds, ys.reshape(-1, CHUNK, D_MODEL))


def _route_tables(cnt):
    nsub = cnt.shape[0]
    i32 = jnp.int32
    seg_end = jnp.cumsum(cnt, axis=1)
    seg_start = seg_end - cnt
    tot = jnp.sum(cnt, axis=0)
    tot_al = (tot + FFN_TM - 1) // FFN_TM * FFN_TM
    reg_end = jnp.cumsum(tot_al)
    reg_start = reg_end - tot_al
    base = reg_start[None, :] + jnp.cumsum(cnt, axis=0) - cnt
    n_real = seg_end[:, -1] // CHUNK
    k = jnp.arange(TABLE_W, dtype=i32)
    owner = jnp.sum((seg_end[:, None, :] // CHUNK) <= k[None, :, None], axis=2)
    owner = jnp.minimum(owner, N_EXPERTS - 1)
    delta = (base - seg_start) // CHUNK
    e_ids = jnp.arange(N_EXPERTS, dtype=i32)
    real_dst = jnp.sum(jnp.where(owner[:, :, None] == e_ids, delta[:, None, :], 0), axis=2) + k[None, :]
    fill_n = (tot_al - tot) // CHUNK
    fill_dst0 = (reg_start + tot) // CHUNK
    src = jnp.broadcast_to(k[None, :], (nsub, TABLE_W))
    dst = real_dst
    ncopy = n_real
    for r in range(-(-N_EXPERTS // nsub)):
        e_of = jnp.arange(nsub, dtype=i32) + r * nsub
        ok = e_of < N_EXPERTS
        e_cl = jnp.minimum(e_of, N_EXPERTS - 1)
        fn = jnp.where(ok, fill_n[e_cl], 0)
        j = k[None, :] - ncopy[:, None]
        is_fill = (j >= 0) & (j < fn[:, None])
        src = jnp.where(is_fill, ZERO_CHUNK, src)
        dst = jnp.where(is_fill, fill_dst0[e_cl][:, None] + j, dst)
        ncopy = ncopy + fn
    trip_rows = BLOCKS_PER_TRIP * MXU_ROWS
    disp_nblk = (n_real * CHUNK + trip_rows - 1) // trip_rows
    comb_n = disp_nblk * (trip_rows // CHUNK)
    comb_src = jnp.where(k[None, :] < n_real[:, None], real_dst, real_dst[:, 0:1])
    n_tiles = reg_end[-1] // FFN_TM
    t = jnp.arange((nsub * SUB_ROWS + N_EXPERTS * FFN_TM) // FFN_TM, dtype=i32)
    tile_expert = jnp.minimum(jnp.sum((reg_end[None, :] // FFN_TM) <= t[:, None], axis=1), N_EXPERTS - 1)
    as3 = lambda a: a.astype(i32).reshape(nsub, 1, TABLE_W)
    return (disp_nblk.astype(i32), n_real.astype(i32), ncopy.astype(i32), as3(src), as3(dst),
            comb_n.astype(i32), as3(comb_src), tile_expert.astype(i32), n_tiles.astype(i32).reshape(1))


def _moe_sparse(xg, post, posr, cnt, x1, g2, wg, wu, wd, wgs, wus, wds, ride):
    B, S, _ = x1.shape
    n_tok = B * S
    nsub = n_tok // SUB
    (disp_nblk, n_real, disp_n, disp_src, disp_dst, comb_n, comb_src, tile_expert,
     n_tiles) = _route_tables(cnt[:, :, 0].astype(jnp.int32))
    n_rows = nsub * SUB_ROWS + N_EXPERTS * FFN_TM
    xg2 = xg.reshape(n_tok, XG_W)
    xs = _dispatch(disp_nblk, n_real, disp_n, n_tiles, disp_src, disp_dst, posr, xg2, n_rows)
    ys, *ride_out = _ffn(tile_expert, n_tiles, xs, wg, wu, wd, ride)
    y = _combine(disp_nblk, comb_n, comb_src, post.reshape(n_tok, LANES), xg2, x1.reshape(n_tok, D_MODEL),
                 g2, wgs, wus, wds, ys)
    return y.reshape(B, S, D_MODEL), ride_out


def _t5_bucket(dist):
    max_exact = N_BUCKETS // 2
    df = jnp.maximum(dist, 1).astype(F32)
    large = max_exact + (jnp.log(df / max_exact) / math.log(MAX_DISTANCE / max_exact)
                         * (N_BUCKETS - max_exact)).astype(jnp.int32)
    large = jnp.minimum(large, N_BUCKETS - 1)
    return jnp.where(dist < max_exact, dist, large)


def _step_bias(rel_bias, g):
    dist = jnp.arange(N_KEYS, dtype=jnp.int32) * DILATIONS[g]
    cols = rel_bias[:, g * HEADS_PER_GROUP:(g + 1) * HEADS_PER_GROUP]
    return cols[_t5_bucket(dist)].T.astype(F32)


def _prompt_bias(bias_k):
    h = bias_k.shape[0]
    n = 3 * BLK
    row = jnp.concatenate([bias_k[:, ::-1], jnp.full((h, n - N_KEYS), NEG, F32)], axis=1)
    t = jnp.tile(row, (1, BLK))[:, :BLK * (n - 1)].reshape(h, BLK, n - 1)
    return t[:, :, :2 * BLK]


def _sample_bias(bias_k, wb, d):
    h = bias_k.shape[0]
    rev = bias_k[:, :0:-1]
    rows = []
    for t in range(4):
        if d == 1:
            rows.append(jnp.concatenate([jnp.full((h, t), NEG, F32), rev[:, :wb - t]], axis=1))
        else:
            r = jnp.arange(d)[None, None, :]
            rows.append(jnp.where(r == t, rev[:, :, None], NEG).reshape(h, wb))
    b1 = jnp.concatenate([jnp.stack(rows, axis=1), jnp.zeros((h, 4, wb), F32)], axis=1)
    rows = []
    for t in range(4):
        cols = [bias_k[:, (t - tn) // d] if (t >= tn and (t - tn) % d == 0) else jnp.full((h,), NEG, F32)
                for tn in range(4)]
        rows.append(jnp.concatenate([jnp.full((h, LANES - 4), NEG, F32), jnp.stack(cols, axis=1)], axis=1))
    b2 = jnp.concatenate([jnp.stack(rows, axis=1), jnp.zeros((h, 4, LANES), F32)], axis=1)
    return b1, b2


TILE = 512
RIDE_ON_OPROJ, RIDE_ON_PROJ, RIDE_ON_FFN = 0, 1, 2


def _sample_front(x, mods, hist, caches, rel_bias, weights):
    norm1_g, _, w_in_bf, qg, kg, conv_w, bd = weights[:7]
    sh1, sc1 = mods[:2]
    S = x.shape[1]
    nb = S // 4
    qf, kt, vt, u, co = _proj(x, sc1, sh1, norm1_g, w_in_bf, qg, kg, conv_w, bd, hist,
                              tile=TILE, dils=(1, 1, 1), sample=True)
    rides = []
    for g in range(N_GROUPS):
        d, wb = DILATIONS[g], WINDOWS[g]
        kc, vc = caches[g]
        b1, b2 = _sample_bias(_step_bias(rel_bias, g), wb, d)
        q = qf[0, :, g * GROUP_W:(g + 1) * GROUP_W].reshape(nb, 4, HEADS_PER_GROUP, HEAD_DIM)
        q = jnp.pad(q.transpose(0, 2, 1, 3), ((0, 0), (0, 0), (0, 4), (0, 0))).astype(BF16)
        new_t = lambda a: a[g * GROUP_W:(g + 1) * GROUP_W].reshape(HEADS_PER_GROUP, HEAD_DIM, S)
        rides.append((q, kc, vc, new_t(kt), new_t(vt), b1, b2))
    return dict(x=x, mods=mods, u=u, co=co), rides


def _sample_back(front, ride_outs, weights):
    (_, norm2_g, _, _, _, _, _, w_o_bf, w_r_t, b_r, w_gate_e, w_up_e, w_down_e,
     w_gate_s, w_up_s, w_down_s) = weights
    x, u, co = front["x"], front["u"], front["co"]
    _, _, g1, sh2, sc2, g2 = front["mods"]
    S = x.shape[1]
    nb = S // 4
    states, o_list, l_list = [], [], []
    for g in range(N_GROUPS):
        ko, vo, o, lse = ride_outs[g]
        states += [ko.transpose(0, 3, 1, 2)[None], vo.transpose(0, 3, 1, 2)[None]]
        o_list.append(o[:, :, :4].transpose(0, 2, 1, 3).reshape(1, 1, S, GROUP_W))
        lse = jnp.broadcast_to(lse[:, :, :4, :1], (nb, HEADS_PER_GROUP, 4, HEAD_DIM))
        l_list.append(lse.transpose(0, 2, 1, 3).reshape(1, 1, S, GROUP_W))
    states.append(u.reshape(nb, 4, CONV_CH)[:, 2:][None])
    x1, xg, gates, _, _, _ = _oproj(x, o_list, l_list, co, g1, sc2, sh2, norm2_g, w_o_bf, w_r_t, b_r,
                                    tile=TILE, dils=(1, 1, 1))
    y = _moe(xg, gates, x1, g2, w_gate_e, w_up_e, w_down_e, w_gate_s, w_up_s, w_down_s, tile=TILE)
    return y, states


def _prompt_layer(x, mods, rel_bias, weights, rides):
    (norm1_g, norm2_g, w_in_bf, qg, kg, conv_w, bd, w_o_bf, w_r_t, b_r, w_gate_e, w_up_e, w_down_e,
     w_gate_s, w_up_s, w_down_s) = weights
    sh1, sc1, g1, sh2, sc2, g2 = mods
    B, S, _ = x.shape
    ride_outs = [None] * N_GROUPS
    res = _proj(x, sc1, sh1, norm1_g, w_in_bf, qg, kg, conv_w, bd, None, tile=TILE, dils=DILATIONS, sample=False,
                ride=rides[RIDE_ON_PROJ])
    qkv, tails, (u, co), ride_outs[RIDE_ON_PROJ] = res[:9], res[9:15], res[15:17], res[17:]
    states, o_list, l_list = [], [], []
    for g in range(N_GROUPS):
        d, w = DILATIONS[g], WINDOWS[g]
        q, k, v = (a.reshape(B * d, S // d, GROUP_W) for a in qkv[3 * g:3 * g + 3])
        o, lse = _attn_prompt(q, k, v, _prompt_bias(_step_bias(rel_bias, g)))
        o_list.append(o.reshape(B, d, S // d, GROUP_W))
        l_list.append(lse.reshape(B, d, S // d, GROUP_W))
        keep = lambda a: a.reshape(B, HEADS_PER_GROUP, HEAD_DIM, w).transpose(0, 3, 1, 2)[None]
        states += [keep(tails[2 * g]), keep(tails[2 * g + 1])]
    states.append(u[:, 6:][None])
    x1, xg, _, posr, post, cnt, *ride_outs[RIDE_ON_OPROJ] = _oproj(
        x, o_list, l_list, co, g1, sc2, sh2, norm2_g, w_o_bf, w_r_t, b_r, tile=TILE, dils=DILATIONS,
        ride=rides[RIDE_ON_OPROJ])
    y, ride_outs[RIDE_ON_FFN] = _moe_sparse(xg, post, posr, cnt, x1, g2, w_gate_e, w_up_e, w_down_e, w_gate_s,
                                            w_up_s, w_down_s, rides[RIDE_ON_FFN])
    return y, states, ride_outs


def kernel(x_prompt, x_sample, c_prompt, c_sample, cache_k_w128, cache_v_w128, cache_k_w512, cache_v_w512, cache_k_w2048, cache_v_w2048, state_conv, rel_bias, norm1_g, norm2_g, w_ada, b_ada, w_in, q_norm_g, k_norm_g, conv_w, w_o, w_router, b_router, w_gate_e, w_up_e, w_down_e, w_gate_s, w_up_s, w_down_s):
    B = x_prompt.shape[0]
    DB, T = x_sample.shape[:2]
    n_c = B + DB
    c_all = jnp.pad(jnp.concatenate([c_prompt, c_sample], axis=0), ((0, (-n_c) % 8), (0, 0)))
    mod = _ada(c_all, w_ada[0], b_ada)
    chunks = [mod[:, i * D_MODEL:(i + 1) * D_MODEL] for i in range(6)]
    mods_p = [c[:B].reshape(B, 1, D_MODEL) for c in chunks]
    mods_s = [jnp.repeat(c[B:n_c], T, axis=0).reshape(1, DB * T, D_MODEL) for c in chunks]

    eye = jnp.arange(GROUP_W) // HEAD_DIM
    bd = jnp.where(eye[:, None] == eye[None, :], 1.0 / HEAD_DIM, 0.0).astype(BF16)
    weights = (norm1_g, norm2_g, w_in[0].astype(BF16),
               jnp.tile(q_norm_g, (1, HEADS_PER_GROUP)), jnp.tile(k_norm_g, (1, HEADS_PER_GROUP)),
               conv_w[0], bd, w_o[0].astype(BF16), w_router[0].T, b_router.reshape(N_EXPERTS, 1),
               w_gate_e[0], w_up_e[0], w_down_e[0], w_gate_s[0], w_up_s[0], w_down_s[0])

    s0, s1 = state_conv[0, :, 0], state_conv[0, :, 1]
    zero = jnp.zeros_like(s0)
    hist_a = jnp.stack([s0, s1, zero, zero], axis=1).reshape(DB * T, CONV_CH)
    hist_b = jnp.stack([s1, zero, zero, zero], axis=1).reshape(DB * T, CONV_CH)
    caches = [(ck[0].transpose(0, 2, 3, 1), cv[0].transpose(0, 2, 3, 1))
              for ck, cv in ((cache_k_w128, cache_v_w128), (cache_k_w512, cache_v_w512),
                             (cache_k_w2048, cache_v_w2048))]
    front, ride = _sample_front(x_sample.reshape(1, DB * T, D_MODEL), mods_s, (hist_a, hist_b), caches,
                                rel_bias, weights)
    yp, st_p, ride_out = _prompt_layer(x_prompt, mods_p, rel_bias, weights, ride)
    ys, st_s = _sample_back(front, ride_out, weights)
    return (yp, ys.reshape(DB, T, D_MODEL), *st_p, *st_s)
```
